```python
import jax, jax.numpy as jnp
from jax import lax
import numpy as np

D_MODEL = 1024
BATCH = 8
SEQ = 8192
DEPTH = 4

PLE_DIM = 256
N_BRANCH = 4
BRANCH_WIDTH = 256
HEAD_DIM = 64
N_HEADS = BRANCH_WIDTH // HEAD_DIM
CONV_WIDTH = 3
ATTN_BLOCK = 128
GLA_CHUNK = 64
SPATIAL_CHUNK = 128
EPS = 1e-6
MASK_VALUE = -1e30
IN_COLS = 15 * BRANCH_WIDTH + N_HEADS + N_BRANCH * D_MODEL

kernel_name = "hybrid_conv_fox_hgrn2_gmlp_gated_merge"


def _split_points():
    W = BRANCH_WIDTH
    sizes = [W] * 4 + [W] * 4 + [N_HEADS] + [W] * 4 + [W] * 3 + [N_BRANCH * D_MODEL]
    return [int(s) for s in np.cumsum(sizes)[:-1]]


def rms_norm(x, g):
    xf = x.astype(jnp.float32)
    return xf * lax.rsqrt(jnp.mean(xf * xf, axis=-1, keepdims=True) + EPS) * g.astype(jnp.float32)


def group_rms_norm(x, g):
    Bn, S, W = x.shape
    xg = x.astype(jnp.float32).reshape(Bn, S, N_HEADS, HEAD_DIM)
    xg = xg * lax.rsqrt(jnp.mean(xg * xg, axis=-1, keepdims=True) + EPS)
    return (xg * g.astype(jnp.float32).reshape(N_HEADS, HEAD_DIM)).reshape(Bn, S, W)


def short_conv_mixer(x_in, b, c, w, bias):
    S = x_in.shape[1]
    z = c.astype(jnp.float32) * x_in.astype(jnp.float32)
    zp = jnp.pad(z, ((0, 0), (CONV_WIDTH - 1, 0), (0, 0)))
    wf = w.astype(jnp.float32)
    y = zp[:, 0:S] * wf[0]
    for tap in range(1, CONV_WIDTH):
        y = y + zp[:, tap:tap + S] * wf[tap]
    return b.astype(jnp.float32) * (y + bias.astype(jnp.float32))


def forgetting_attention(q, k, v, f_logit, gq, gk):
    Bn, S, _ = q.shape
    f32 = jnp.float32

    def heads(t):
        return t.astype(f32).reshape(Bn, S, N_HEADS, HEAD_DIM).transpose(0, 2, 1, 3)

    qh = rms_norm(heads(q), gq)
    kh = rms_norm(heads(k), gk)
    vh = heads(v)
    cum = jnp.cumsum(jax.nn.log_sigmoid(f_logit.astype(f32)).transpose(0, 2, 1), axis=-1)
    nb = S // ATTN_BLOCK
    qb = qh.reshape(Bn, N_HEADS, nb, ATTN_BLOCK, HEAD_DIM).transpose(2, 0, 1, 3, 4)
    cb = cum.reshape(Bn, N_HEADS, nb, ATTN_BLOCK).transpose(2, 0, 1, 3)
    kpos = jnp.arange(S)
    scale = HEAD_DIM ** -0.5

    def block(args):
        qi, ci, bi = args
        logits = jnp.einsum('bhqd,bhkd->bhqk', qi, kh) * scale + (ci[..., None] - cum[:, :, None, :])
        qpos = bi * ATTN_BLOCK + jnp.arange(ATTN_BLOCK)
        mask = qpos[:, None] >= kpos[None, :]
        probs = jax.nn.softmax(jnp.where(mask, logits, MASK_VALUE), axis=-1)
        return jnp.einsum('bhqk,bhkd->bhqd', probs, vh)

    o = lax.map(block, (qb, cb, jnp.arange(nb)))
    return o.transpose(1, 0, 3, 2, 4).reshape(Bn, S, N_HEADS * HEAD_DIM)


def hgrn2_recurrence(q, f_logit, i_in, lb, gain):
    Bn, S, W = q.shape
    f32 = jnp.float32
    qf = jax.nn.silu(q.astype(f32))
    fl = f_logit.astype(f32)
    lbf = lb.astype(f32)
    log_g = jnp.log(lbf + (1.0 - lbf) * jax.nn.sigmoid(fl))
    kf = (1.0 - lbf) * jax.nn.sigmoid(-fl)
    vf = i_in.astype(f32)
    nc = S // GLA_CHUNK

    def chunks(t):
        return t.reshape(Bn, nc, GLA_CHUNK, N_HEADS, HEAD_DIM).transpose(1, 0, 3, 2, 4)

    causal = jnp.tril(jnp.ones((GLA_CHUNK, GLA_CHUNK), dtype=bool))[:, :, None]

    def step(state, inp):
        qc, kc, vc, gc = inp
        b = jnp.cumsum(gc, axis=2)
        o_inter = jnp.einsum('bhtk,bhkv->bhtv', qc * jnp.exp(b), state)
        diff = b[:, :, :, None, :] - b[:, :, None, :, :]
        decay = jnp.where(causal, jnp.exp(jnp.where(causal, diff, 0.0)), 0.0)
        scores = jnp.einsum('bhtk,bhsk,bhtsk->bhts', qc, kc, decay)
        o_intra = jnp.einsum('bhts,bhsv->bhtv', scores, vc)
        b_last = b[:, :, -1]
        new_state = jnp.exp(b_last)[..., None] * state + jnp.einsum(
            'bhsk,bhsv->bhkv', kc * jnp.exp(b_last[:, :, None] - b), vc)
        return new_state, o_inter + o_intra

    state0 = jnp.zeros((Bn, N_HEADS, HEAD_DIM, HEAD_DIM), f32)
    _, o = lax.scan(step, state0, (chunks(qf), chunks(kf), chunks(vf), chunks(log_g)))
    o = o.transpose(1, 0, 3, 2, 4).reshape(Bn, S, W)
    return group_rms_norm(o, gain)


def spatial_gating_mixer(u, v, gv, w_s, b_s):
    Bn, S, W = u.shape
    vn = group_rms_norm(v, gv).reshape(Bn, S // SPATIAL_CHUNK, SPATIAL_CHUNK, N_HEADS, HEAD_DIM)
    causal = jnp.tril(jnp.ones((SPATIAL_CHUNK, SPATIAL_CHUNK), dtype=jnp.float32))
    w = w_s.astype(jnp.float32) * causal
    s = jnp.einsum('gts,bnsgc->bntgc', w, vn) + b_s.astype(jnp.float32).T[None, None, :, :, None]
    return u.astype(jnp.float32) * s.reshape(Bn, S, W)


def _fwd_setup_inputs(seed: int = 0) -> dict:
    key = jax.random.key(seed)
    ks = jax.random.split(key, 24)
    W = BRANCH_WIDTH
    n = jax.random.normal
    f32 = jnp.float32
    return {
        "x": n(ks[0], (BATCH, SEQ, D_MODEL), f32),
        "p": n(ks[1], (DEPTH, BATCH, SEQ, PLE_DIM), f32),
        "norm_mix": 1.0 + 0.02 * n(ks[2], (DEPTH, D_MODEL), f32),
        "w_in": n(ks[3], (DEPTH, D_MODEL, IN_COLS), f32) * D_MODEL ** -0.5,
        "conv_w": n(ks[4], (DEPTH, CONV_WIDTH, W), f32) * CONV_WIDTH ** -0.5,
        "conv_b": 0.02 * n(ks[5], (DEPTH, W), f32),
        "fgate_bias": jnp.linspace(1.0, 4.0, N_HEADS, dtype=f32) + 0.1 * n(ks[6], (DEPTH, N_HEADS), f32),
        "q_norm": 1.0 + 0.02 * n(ks[7], (DEPTH, HEAD_DIM), f32),
        "k_norm": 1.0 + 0.02 * n(ks[8], (DEPTH, HEAD_DIM), f32),
        "lb_logits": 0.5 * n(ks[9], (DEPTH, W), f32),
        "hgrn_norm": 1.0 + 0.02 * n(ks[10], (DEPTH, W), f32),
        "sgu_norm": 1.0 + 0.02 * n(ks[11], (DEPTH, W), f32),
        "spatial_w": 0.5 * n(ks[12], (DEPTH, N_HEADS, SPATIAL_CHUNK, SPATIAL_CHUNK), f32) * SPATIAL_CHUNK ** -0.5,
        "spatial_b": 1.0 + 0.02 * n(ks[13], (DEPTH, N_HEADS, SPATIAL_CHUNK), f32),
        "w_up": n(ks[14], (DEPTH, N_BRANCH, W, D_MODEL), f32) * W ** -0.5,
        "merge_b": 0.02 * n(ks[15], (DEPTH, N_BRANCH, D_MODEL), f32),
        "w_o": n(ks[16], (DEPTH, D_MODEL, D_MODEL), f32) * (0.5 * D_MODEL ** -0.5),
        "norm_ple": 1.0 + 0.02 * n(ks[17], (DEPTH, D_MODEL), f32),
        "w_ple_gate": n(ks[18], (DEPTH, D_MODEL, D_MODEL), f32) * D_MODEL ** -0.5,
        "w_ple_proj": n(ks[19], (DEPTH, PLE_DIM, D_MODEL), f32) * (0.5 * PLE_DIM ** -0.5),
    }


def _fwd_reference(x, p, norm_mix, w_in, conv_w, conv_b, fgate_bias, q_norm, k_norm, lb_logits,
              hgrn_norm, sgu_norm, spatial_w, spatial_b, w_up, merge_b, w_o, norm_ple,
              w_ple_gate, w_ple_proj):
    dt = x.dtype
    Bn, S, _ = x.shape
    splits = _split_points()
    lb_p = jax.nn.softmax(lb_logits.astype(jnp.float32), axis=0)
    lower_bounds = jnp.clip(jnp.cumsum(lb_p, axis=0) - lb_p[0], 0.0, 1.0)
    for li in range(DEPTH):
        h = rms_norm(x, norm_mix[li]).astype(dt)
        z = h @ w_in[li]
        (a_x, a_b, a_c, a_g,
         b_q, b_k, b_v, b_g, b_f,
         c_q, c_f, c_i, c_g,
         d_u, d_v, d_g, m_logits) = jnp.split(z, splits, axis=-1)

        y_a = short_conv_mixer(a_x, a_b, a_c, conv_w[li], conv_b[li]).astype(dt) * jax.nn.silu(a_g)
        y_b = forgetting_attention(b_q, b_k, b_v, b_f + fgate_bias[li], q_norm[li], k_norm[li]).astype(dt) * jax.nn.silu(b_g)
        y_c = hgrn2_recurrence(c_q, c_f, c_i, lower_bounds[li], hgrn_norm[li]).astype(dt) * jax.nn.silu(c_g)
        y_d = spatial_gating_mixer(d_u, d_v, sgu_norm[li], spatial_w[li], spatial_b[li]).astype(dt) * jax.nn.silu(d_g)

        branches = (y_a, y_b, y_c, y_d)
        gate_logits = m_logits.reshape(Bn, S, N_BRANCH, D_MODEL)
        merged = jax.nn.sigmoid(gate_logits[:, :, 0] + merge_b[li, 0]) * (branches[0] @ w_up[li, 0])
        for bi in range(1, N_BRANCH):
            merged = merged + jax.nn.sigmoid(gate_logits[:, :, bi] + merge_b[li, bi]) * (branches[bi] @ w_up[li, bi])
        x = x + merged @ w_o[li]

        hp = rms_norm(x, norm_ple[li]).astype(dt)
        x = x + jax.nn.sigmoid(hp @ w_ple_gate[li]) * (p[li] @ w_ple_proj[li])
    return x


import jax as _jax
import jax.numpy as _jnp

TWIN_FORMAT = 'train_step'
FWD_PARAMS = ['x', 'p', 'norm_mix', 'w_in', 'conv_w', 'conv_b', 'fgate_bias', 'q_norm', 'k_norm', 'lb_logits', 'hgrn_norm', 'sgu_norm', 'spatial_w', 'spatial_b', 'w_up', 'merge_b', 'w_o', 'norm_ple', 'w_ple_gate', 'w_ple_proj']
TWIN_WEIGHTS = ['norm_mix', 'w_in', 'conv_w', 'conv_b', 'fgate_bias', 'q_norm', 'k_norm', 'lb_logits', 'hgrn_norm', 'sgu_norm', 'spatial_w', 'spatial_b', 'w_up', 'merge_b', 'w_o', 'norm_ple', 'w_ple_gate', 'w_ple_proj']
TWIN_DIFF_INPUT = 'x'
TWIN_INPUTS = ['x', 'p', 'norm_mix', 'w_in', 'conv_w', 'conv_b', 'fgate_bias', 'q_norm', 'k_norm', 'lb_logits', 'hgrn_norm', 'sgu_norm', 'spatial_w', 'spatial_b', 'w_up', 'merge_b', 'w_o', 'norm_ple', 'w_ple_gate', 'w_ple_proj', 'loss_target', 'm_norm_mix', 'm_w_in', 'm_conv_w', 'm_conv_b', 'm_fgate_bias', 'm_q_norm', 'm_k_norm', 'm_lb_logits', 'm_hgrn_norm', 'm_sgu_norm', 'm_spatial_w', 'm_spatial_b', 'm_w_up', 'm_merge_b', 'm_w_o', 'm_norm_ple', 'm_w_ple_gate', 'm_w_ple_proj', 'v_norm_mix', 'v_w_in', 'v_conv_w', 'v_conv_b', 'v_fgate_bias', 'v_q_norm', 'v_k_norm', 'v_lb_logits', 'v_hgrn_norm', 'v_sgu_norm', 'v_spatial_w', 'v_spatial_b', 'v_w_up', 'v_merge_b', 'v_w_o', 'v_norm_ple', 'v_w_ple_gate', 'v_w_ple_proj']
TWIN_OUTPUTS = ['loss', 'grad_x', 'grad_norm_mix', 'grad_w_in', 'grad_conv_w', 'grad_conv_b', 'grad_fgate_bias', 'grad_q_norm', 'grad_k_norm', 'grad_lb_logits', 'grad_hgrn_norm', 'grad_sgu_norm', 'grad_spatial_w', 'grad_spatial_b', 'grad_w_up', 'grad_merge_b', 'grad_w_o', 'grad_norm_ple', 'grad_w_ple_gate', 'grad_w_ple_proj', 'delta_norm_mix', 'delta_w_in', 'delta_conv_w', 'delta_conv_b', 'delta_fgate_bias', 'delta_q_norm', 'delta_k_norm', 'delta_lb_logits', 'delta_hgrn_norm', 'delta_sgu_norm', 'delta_spatial_w', 'delta_spatial_b', 'delta_w_up', 'delta_merge_b', 'delta_w_o', 'delta_norm_ple', 'delta_w_ple_gate', 'delta_w_ple_proj', 'new_m_norm_mix', 'new_m_w_in', 'new_m_conv_w', 'new_m_conv_b', 'new_m_fgate_bias', 'new_m_q_norm', 'new_m_k_norm', 'new_m_lb_logits', 'new_m_hgrn_norm', 'new_m_sgu_norm', 'new_m_spatial_w', 'new_m_spatial_b', 'new_m_w_up', 'new_m_merge_b', 'new_m_w_o', 'new_m_norm_ple', 'new_m_w_ple_gate', 'new_m_w_ple_proj', 'new_v_norm_mix', 'new_v_w_in', 'new_v_conv_w', 'new_v_conv_b', 'new_v_fgate_bias', 'new_v_q_norm', 'new_v_k_norm', 'new_v_lb_logits', 'new_v_hgrn_norm', 'new_v_sgu_norm', 'new_v_spatial_w', 'new_v_spatial_b', 'new_v_w_up', 'new_v_merge_b', 'new_v_w_o', 'new_v_norm_ple', 'new_v_w_ple_gate', 'new_v_w_ple_proj']
TWIN_LEAF_KINDS = {'loss': 'loss', 'grad_x': 'grad_x', 'grad_norm_mix': 'grad_w', 'grad_w_in': 'grad_w', 'grad_conv_w': 'grad_w', 'grad_conv_b': 'grad_w', 'grad_fgate_bias': 'grad_w', 'grad_q_norm': 'grad_w', 'grad_k_norm': 'grad_w', 'grad_lb_logits': 'grad_w', 'grad_hgrn_norm': 'grad_w', 'grad_sgu_norm': 'grad_w', 'grad_spatial_w': 'grad_w', 'grad_spatial_b': 'grad_w', 'grad_w_up': 'grad_w', 'grad_merge_b': 'grad_w', 'grad_w_o': 'grad_w', 'grad_norm_ple': 'grad_w', 'grad_w_ple_gate': 'grad_w', 'grad_w_ple_proj': 'grad_w', 'delta_norm_mix': 'delta_w', 'delta_w_in': 'delta_w', 'delta_conv_w': 'delta_w', 'delta_conv_b': 'delta_w', 'delta_fgate_bias': 'delta_w', 'delta_q_norm': 'delta_w', 'delta_k_norm': 'delta_w', 'delta_lb_logits': 'delta_w', 'delta_hgrn_norm': 'delta_w', 'delta_sgu_norm': 'delta_w', 'delta_spatial_w': 'delta_w', 'delta_spatial_b': 'delta_w', 'delta_w_up': 'delta_w', 'delta_merge_b': 'delta_w', 'delta_w_o': 'delta_w', 'delta_norm_ple': 'delta_w', 'delta_w_ple_gate': 'delta_w', 'delta_w_ple_proj': 'delta_w', 'new_m_norm_mix': 'new_m', 'new_m_w_in': 'new_m', 'new_m_conv_w': 'new_m', 'new_m_conv_b': 'new_m', 'new_m_fgate_bias': 'new_m', 'new_m_q_norm': 'new_m', 'new_m_k_norm': 'new_m', 'new_m_lb_logits': 'new_m', 'new_m_hgrn_norm': 'new_m', 'new_m_sgu_norm': 'new_m', 'new_m_spatial_w': 'new_m', 'new_m_spatial_b': 'new_m', 'new_m_w_up': 'new_m', 'new_m_merge_b': 'new_m', 'new_m_w_o': 'new_m', 'new_m_norm_ple': 'new_m', 'new_m_w_ple_gate': 'new_m', 'new_m_w_ple_proj': 'new_m', 'new_v_norm_mix': 'new_v', 'new_v_w_in': 'new_v', 'new_v_conv_w': 'new_v', 'new_v_conv_b': 'new_v', 'new_v_fgate_bias': 'new_v', 'new_v_q_norm': 'new_v', 'new_v_k_norm': 'new_v', 'new_v_lb_logits': 'new_v', 'new_v_hgrn_norm': 'new_v', 'new_v_sgu_norm': 'new_v', 'new_v_spatial_w': 'new_v', 'new_v_spatial_b': 'new_v', 'new_v_w_up': 'new_v', 'new_v_merge_b': 'new_v', 'new_v_w_o': 'new_v', 'new_v_norm_ple': 'new_v', 'new_v_w_ple_gate': 'new_v', 'new_v_w_ple_proj': 'new_v'}


def _forward(args):
    return _fwd_reference(*[args[k] for k in FWD_PARAMS])


def _output_shape():
    def fwd():
        inp = _fwd_setup_inputs(0)
        return _fwd_reference(*[inp[k] for k in FWD_PARAMS])
    out = _jax.eval_shape(fwd)
    return out.shape, out.dtype

N_MICROBATCH = 1
ADAM_LR = 0.001
ADAM_B1 = 0.9
ADAM_B2 = 0.999
ADAM_EPS = 1e-08
ADAM_WD = 0.01
ADAM_STEP = 10
PER_EXAMPLE_BATCH_AXIS = {'x': 0, 'p': 1, 'loss_target': 0}
SHARED_INPUTS = []
_WEIGHT_DTYPES = {'norm_mix': _jnp.float32, 'w_in': _jnp.float32, 'conv_w': _jnp.float32, 'conv_b': _jnp.float32, 'fgate_bias': _jnp.float32, 'q_norm': _jnp.float32, 'k_norm': _jnp.float32, 'lb_logits': _jnp.float32, 'hgrn_norm': _jnp.float32, 'sgu_norm': _jnp.float32, 'spatial_w': _jnp.float32, 'spatial_b': _jnp.float32, 'w_up': _jnp.float32, 'merge_b': _jnp.float32, 'w_o': _jnp.float32, 'norm_ple': _jnp.float32, 'w_ple_gate': _jnp.float32, 'w_ple_proj': _jnp.float32}
MOMENT_SCALE = {'norm_mix': 1.389348e+01, 'w_in': 1.631076e-01, 'conv_w': 3.563597e+00, 'conv_b': 2.907619e-01, 'fgate_bias': 1.906983e+01, 'q_norm': 1.205124e+00, 'k_norm': 1.205344e+00, 'lb_logits': 2.811152e-02, 'hgrn_norm': 6.608482e+00, 'sgu_norm': 7.802635e-01, 'spatial_w': 1.280836e-01, 'spatial_b': 3.251105e+00, 'w_up': 1.248783e-01, 'merge_b': 5.137469e-01, 'w_o': 4.761279e-01, 'norm_ple': 4.407231e-01, 'w_ple_gate': 6.287457e-02, 'w_ple_proj': 4.658461e-01}


def _to_microbatches(a, axis):
    t = _jnp.moveaxis(a, axis, 0)
    t = t.reshape((N_MICROBATCH, t.shape[0] // N_MICROBATCH) + t.shape[1:])
    return _jnp.moveaxis(t, 1, axis + 1)


def setup_inputs(seed: int = 0) -> dict:
    inp = _fwd_setup_inputs(seed)
    key = _jax.random.fold_in(_jax.random.key(seed), 7919)
    shape, _ = _output_shape()
    out = dict(inp)
    out["loss_target"] = _jax.random.normal(_jax.random.fold_in(key, 0), shape, _jnp.float32)
    for i, name in enumerate(TWIN_WEIGHTS):
        w = inp[name].astype(_jnp.float32)
        if MOMENT_SCALE is None:
            s = _jnp.sqrt(_jnp.mean(_jnp.square(w)) + 1e-30)
        else:
            s = MOMENT_SCALE[name]
        km, kv = _jax.random.split(_jax.random.fold_in(key, i + 1))
        out[name] = w
        out["m_" + name] = s * _jax.random.normal(km, w.shape, _jnp.float32)
        out["v_" + name] = (s * s) * _jax.random.uniform(kv, w.shape, _jnp.float32, 0.5, 1.5)
    if N_MICROBATCH > 1:
        for name, axis in PER_EXAMPLE_BATCH_AXIS.items():
            out[name] = _to_microbatches(out[name], axis)
    return {'x': out['x'], 'p': out['p'], 'norm_mix': out['norm_mix'], 'w_in': out['w_in'], 'conv_w': out['conv_w'], 'conv_b': out['conv_b'], 'fgate_bias': out['fgate_bias'], 'q_norm': out['q_norm'], 'k_norm': out['k_norm'], 'lb_logits': out['lb_logits'], 'hgrn_norm': out['hgrn_norm'], 'sgu_norm': out['sgu_norm'], 'spatial_w': out['spatial_w'], 'spatial_b': out['spatial_b'], 'w_up': out['w_up'], 'merge_b': out['merge_b'], 'w_o': out['w_o'], 'norm_ple': out['norm_ple'], 'w_ple_gate': out['w_ple_gate'], 'w_ple_proj': out['w_ple_proj'], 'loss_target': out['loss_target'], 'm_norm_mix': out['m_norm_mix'], 'm_w_in': out['m_w_in'], 'm_conv_w': out['m_conv_w'], 'm_conv_b': out['m_conv_b'], 'm_fgate_bias': out['m_fgate_bias'], 'm_q_norm': out['m_q_norm'], 'm_k_norm': out['m_k_norm'], 'm_lb_logits': out['m_lb_logits'], 'm_hgrn_norm': out['m_hgrn_norm'], 'm_sgu_norm': out['m_sgu_norm'], 'm_spatial_w': out['m_spatial_w'], 'm_spatial_b': out['m_spatial_b'], 'm_w_up': out['m_w_up'], 'm_merge_b': out['m_merge_b'], 'm_w_o': out['m_w_o'], 'm_norm_ple': out['m_norm_ple'], 'm_w_ple_gate': out['m_w_ple_gate'], 'm_w_ple_proj': out['m_w_ple_proj'], 'v_norm_mix': out['v_norm_mix'], 'v_w_in': out['v_w_in'], 'v_conv_w': out['v_conv_w'], 'v_conv_b': out['v_conv_b'], 'v_fgate_bias': out['v_fgate_bias'], 'v_q_norm': out['v_q_norm'], 'v_k_norm': out['v_k_norm'], 'v_lb_logits': out['v_lb_logits'], 'v_hgrn_norm': out['v_hgrn_norm'], 'v_sgu_norm': out['v_sgu_norm'], 'v_spatial_w': out['v_spatial_w'], 'v_spatial_b': out['v_spatial_b'], 'v_w_up': out['v_w_up'], 'v_merge_b': out['v_merge_b'], 'v_w_o': out['v_w_o'], 'v_norm_ple': out['v_norm_ple'], 'v_w_ple_gate': out['v_w_ple_gate'], 'v_w_ple_proj': out['v_w_ple_proj']}


def _loss(weights, diff, rest, loss_target):
    with _jax.named_scope("forward"):
        args = {**rest, TWIN_DIFF_INPUT: diff, **{k: w.astype(_WEIGHT_DTYPES[k]) for k, w in weights.items()}}
        y = _forward(args)
    with _jax.named_scope("loss_head"):
        err = _jnp.square(y.astype(_jnp.float32) - loss_target)
        return 0.5 * _jnp.sum(_jnp.mean(err, axis=-1)) if err.ndim else 0.5 * err


def _adamw(w, g, m, v):
    m = ADAM_B1 * m + (1.0 - ADAM_B1) * g
    v = ADAM_B2 * v + (1.0 - ADAM_B2) * _jnp.square(g)
    m_hat = m / (1.0 - ADAM_B1 ** ADAM_STEP)
    v_hat = v / (1.0 - ADAM_B2 ** ADAM_STEP)
    delta = -ADAM_LR * (m_hat / (_jnp.sqrt(v_hat) + ADAM_EPS) + ADAM_WD * w)
    return delta, m, v


def reference(x, p, norm_mix, w_in, conv_w, conv_b, fgate_bias, q_norm, k_norm, lb_logits, hgrn_norm, sgu_norm, spatial_w, spatial_b, w_up, merge_b, w_o, norm_ple, w_ple_gate, w_ple_proj, loss_target, m_norm_mix, m_w_in, m_conv_w, m_conv_b, m_fgate_bias, m_q_norm, m_k_norm, m_lb_logits, m_hgrn_norm, m_sgu_norm, m_spatial_w, m_spatial_b, m_w_up, m_merge_b, m_w_o, m_norm_ple, m_w_ple_gate, m_w_ple_proj, v_norm_mix, v_w_in, v_conv_w, v_conv_b, v_fgate_bias, v_q_norm, v_k_norm, v_lb_logits, v_hgrn_norm, v_sgu_norm, v_spatial_w, v_spatial_b, v_w_up, v_merge_b, v_w_o, v_norm_ple, v_w_ple_gate, v_w_ple_proj):
    given = dict(x=x, p=p, norm_mix=norm_mix, w_in=w_in, conv_w=conv_w, conv_b=conv_b, fgate_bias=fgate_bias, q_norm=q_norm, k_norm=k_norm, lb_logits=lb_logits, hgrn_norm=hgrn_norm, sgu_norm=sgu_norm, spatial_w=spatial_w, spatial_b=spatial_b, w_up=w_up, merge_b=merge_b, w_o=w_o, norm_ple=norm_ple, w_ple_gate=w_ple_gate, w_ple_proj=w_ple_proj, loss_target=loss_target, m_norm_mix=m_norm_mix, m_w_in=m_w_in, m_conv_w=m_conv_w, m_conv_b=m_conv_b, m_fgate_bias=m_fgate_bias, m_q_norm=m_q_norm, m_k_norm=m_k_norm, m_lb_logits=m_lb_logits, m_hgrn_norm=m_hgrn_norm, m_sgu_norm=m_sgu_norm, m_spatial_w=m_spatial_w, m_spatial_b=m_spatial_b, m_w_up=m_w_up, m_merge_b=m_merge_b, m_w_o=m_w_o, m_norm_ple=m_norm_ple, m_w_ple_gate=m_w_ple_gate, m_w_ple_proj=m_w_ple_proj, v_norm_mix=v_norm_mix, v_w_in=v_w_in, v_conv_w=v_conv_w, v_conv_b=v_conv_b, v_fgate_bias=v_fgate_bias, v_q_norm=v_q_norm, v_k_norm=v_k_norm, v_lb_logits=v_lb_logits, v_hgrn_norm=v_hgrn_norm, v_sgu_norm=v_sgu_norm, v_spatial_w=v_spatial_w, v_spatial_b=v_spatial_b, v_w_up=v_w_up, v_merge_b=v_merge_b, v_w_o=v_w_o, v_norm_ple=v_norm_ple, v_w_ple_gate=v_w_ple_gate, v_w_ple_proj=v_w_ple_proj)
    weights = {n: given[n] for n in TWIN_WEIGHTS}
    shared = {n: given[n] for n in SHARED_INPUTS}
    per_example = {n: given[n] for n in ['x', 'p']}
    grad_fn = _jax.value_and_grad(_loss, argnums=(0, 1))

    def one_microbatch(ex, loss_target):
        ex = dict(ex)
        diff = ex.pop(TWIN_DIFF_INPUT)
        return grad_fn(weights, diff, {**shared, **ex}, loss_target)

    if N_MICROBATCH == 1:
        loss, (grad_w, grad_x) = one_microbatch(per_example, given["loss_target"])
    else:
        def body(carry, xs):
            loss_sum, grad_sum = carry
            l_k, (gw_k, gx_k) = one_microbatch(xs[0], xs[1])
            with _jax.named_scope("update"):
                return (loss_sum + l_k, _jax.tree.map(_jnp.add, grad_sum, gw_k)), gx_k

        init = (_jnp.zeros((), _jnp.float32), _jax.tree.map(_jnp.zeros_like, weights))
        (loss, grad_w), grad_x = _jax.lax.scan(body, init, (per_example, given["loss_target"]))
    with _jax.named_scope("update"):
        delta_w, new_m, new_v = {}, {}, {}
        for n in TWIN_WEIGHTS:
            delta_w[n], new_m[n], new_v[n] = _adamw(weights[n], grad_w[n], given["m_" + n], given["v_" + n])
    return (loss, grad_x, *[grad_w[n] for n in TWIN_WEIGHTS], *[delta_w[n] for n in TWIN_WEIGHTS],
            *[new_m[n] for n in TWIN_WEIGHTS], *[new_v[n] for n in TWIN_WEIGHTS])
```

```python
import functools

import jax
import jax.numpy as jnp
from jax import lax
from jax.experimental import pallas as pl
from jax.experimental.pallas import tpu as pltpu

f32 = jnp.float32
bf16 = jnp.bfloat16

D = 1024
W = 256
NH = 4
DH = 64
NBR = 4
PLE = 256
DEPTH = 4
CONV_WIDTH = 3
SGU_CHUNK = 128
GLA_CHUNK = 64
EPS = 1e-6
MASK_VALUE = -1e30
IN_COLS = 7940
NZ = 8064
OFF_CONV = 4096
OFF_HGRN = 5120
OFF_SGU = 6144
OFF_ATT = 6912
OFF_F = 7936
ZT = 1152
NZT = NZ // ZT
EXP_CLAMP = 80.0

ADAM_LR = 0.001
ADAM_B1 = 0.9
ADAM_B2 = 0.999
ADAM_EPS = 1e-08
ADAM_WD = 0.01
ADAM_STEP = 10

N_DEV = 8
AXES = ("x", "y", "c")
VMEM_LIMIT = 56 * 1024 * 1024
HI = lax.Precision.HIGHEST

NT_DIMS = (((1,), (1,)), ((), ()))
TN_DIMS = (((0,), (0,)), ((), ()))


def _pcall(body, **kw):
    return pl.pallas_call(body, **kw)


def _params(*sem):
    return pltpu.CompilerParams(dimension_semantics=sem, vmem_limit_bytes=VMEM_LIMIT)


def _mm(a, b):
    return jnp.dot(a, b, preferred_element_type=f32)


def _mm_nt(a, b):
    return lax.dot_general(a, b, NT_DIMS, preferred_element_type=f32)


def _mm_tn(a, b):
    return lax.dot_general(a, b, TN_DIMS, preferred_element_type=f32)


def _sigmoid(x):
    return 1.0 / (1.0 + jnp.exp(-x))


def _silu(x):
    return x * _sigmoid(x)


def _dsilu(x):
    s = _sigmoid(x)
    return s * (1.0 + x * (1.0 - s))


def _logsigmoid(x):
    return jnp.minimum(x, 0.0) - jnp.log(1.0 + jnp.exp(-jnp.abs(x)))


def _iota2(shape, axis):
    return lax.broadcasted_iota(jnp.int32, shape, axis)


def _group_mean_matrix(n, group):
    shift = group.bit_length() - 1
    r = lax.shift_right_logical(_iota2((n, n), 0), shift)
    c = lax.shift_right_logical(_iota2((n, n), 1), shift)
    return jnp.where(r == c, 1.0 / group, 0.0).astype(f32)


def _group_mean(x, gm):
    return jnp.dot(x, gm, precision=HI, preferred_element_type=f32)


def _lower_tri(n):
    return jnp.where(_iota2((n, n), 0) >= _iota2((n, n), 1), 1.0, 0.0).astype(f32)


def _upper_tri(n):
    return jnp.where(_iota2((n, n), 0) <= _iota2((n, n), 1), 1.0, 0.0).astype(f32)


def _rows3(r0, r1, r2, width):
    row = _iota2((8, width), 0)
    return jnp.where(row == 0, r0, jnp.where(row == 1, r1, jnp.where(row == 2, r2, 0.0)))


def _inproj_fwd(x, g, w, tm):
    T = x.shape[0]

    def body(x_ref, g_ref, w_ref, z_ref, h_ref):
        @pl.when(pl.program_id(1) == 0)
        def _():
            xv = x_ref[...]
            r = lax.rsqrt(jnp.mean(xv * xv, axis=-1, keepdims=True) + EPS)
            h_ref[...] = (xv * r * g_ref[...]).astype(bf16)

        z_ref[...] = _mm(h_ref[...], w_ref[...])

    return _pcall(
        body, name="inproj_fwd", grid=(T // tm, NZT),
        in_specs=[pl.BlockSpec((tm, D), lambda i, j: (i, 0)),
                  pl.BlockSpec((1, D), lambda i, j: (0, 0)),
                  pl.BlockSpec((D, ZT), lambda i, j: (0, j))],
        out_specs=[pl.BlockSpec((tm, ZT), lambda i, j: (i, j)),
                   pl.BlockSpec((tm, D), lambda i, j: (i, 0))],
        out_shape=[jax.ShapeDtypeStruct((T, NZ), f32), jax.ShapeDtypeStruct((T, D), bf16)],
        compiler_params=_params("parallel", "arbitrary"),
    )(x, g, w)


def _inproj_bwd_x(dz, w, x, dx1, g, tm):
    T = x.shape[0]

    def body(dz_ref, w_ref, x_ref, dx1_ref, g_ref, dx_ref, gg_ref, acc):
        i, k = pl.program_id(0), pl.program_id(1)

        @pl.when(k == 0)
        def _():
            acc[...] = jnp.zeros_like(acc)

        @pl.when((i == 0) & (k == 0))
        def _():
            gg_ref[...] = jnp.zeros_like(gg_ref)

        acc[...] += _mm_nt(dz_ref[...], w_ref[...])

        @pl.when(k == NZT - 1)
        def _():
            xv = x_ref[...]
            r = lax.rsqrt(jnp.mean(xv * xv, axis=-1, keepdims=True) + EPS)
            dh = acc[...]
            gg_ref[...] += jnp.sum(dh * xv * r, axis=0, keepdims=True)
            u = dh * g_ref[...]
            dx_ref[...] = dx1_ref[...] + r * u - xv * (r * r * r) * jnp.mean(u * xv, axis=-1, keepdims=True)

    return _pcall(
        body, name="inproj_bwd_x", grid=(T // tm, NZT),
        in_specs=[pl.BlockSpec((tm, ZT), lambda i, k: (i, k)),
                  pl.BlockSpec((D, ZT), lambda i, k: (0, k)),
                  pl.BlockSpec((tm, D), lambda i, k: (i, 0)),
                  pl.BlockSpec((tm, D), lambda i, k: (i, 0)),
                  pl.BlockSpec((1, D), lambda i, k: (0, 0))],
        out_specs=[pl.BlockSpec((tm, D), lambda i, k: (i, 0)),
                   pl.BlockSpec((1, D), lambda i, k: (0, 0))],
        out_shape=[jax.ShapeDtypeStruct((T, D), f32), jax.ShapeDtypeStruct((1, D), f32)],
        scratch_shapes=[pltpu.VMEM((tm, D), f32)],
        compiler_params=_params("arbitrary", "arbitrary"),
    )(dz, w, x, dx1, g)


def _inproj_bwd_w(h, dz, tm):
    T = h.shape[0]

    def body(h_ref, dz_ref, gw_ref):
        @pl.when(pl.program_id(1) == 0)
        def _():
            gw_ref[...] = jnp.zeros_like(gw_ref)

        gw_ref[...] += _mm_tn(h_ref[...], dz_ref[...])

    return _pcall(
        body, name="inproj_bwd_w", grid=(NZT, T // tm),
        in_specs=[pl.BlockSpec((tm, D), lambda j, i: (i, 0)),
                  pl.BlockSpec((tm, ZT), lambda j, i: (i, j))],
        out_specs=pl.BlockSpec((D, ZT), lambda j, i: (0, j)),
        out_shape=jax.ShapeDtypeStruct((D, NZ), f32),
        compiler_params=_params("parallel", "arbitrary"),
    )(h, dz)


def _zblock(tm, col256):
    return pl.BlockSpec((tm, W), lambda i, c=col256: (i, c))


def _conv_taps(zc, halo, cw_ref, n):
    ext = jnp.concatenate([halo, zc], axis=0)
    z1 = pltpu.roll(ext, 1, 0)[8:]
    z2 = pltpu.roll(ext, 2, 0)[8:]
    return z1, z2


def _conv_fwd(z, cw, cb, tm):
    T = z.shape[0]
    c0 = OFF_CONV // W
    hb = tm // 8

    def body(ax_ref, ab_ref, ac_ref, ag_ref, hx_ref, hc_ref, cw_ref, cb_ref, y_ref):
        i = pl.program_id(0)
        zc = ac_ref[...] * ax_ref[...]
        halo = jnp.where(i > 0, hc_ref[...] * hx_ref[...], 0.0)
        z1, z2 = _conv_taps(zc, halo, cw_ref, tm)
        y = cw_ref[2:3, :] * zc + cw_ref[1:2, :] * z1 + cw_ref[0:1, :] * z2
        ya = ab_ref[...] * (y + cb_ref[...])
        y_ref[...] = (ya * _silu(ag_ref[...])).astype(bf16)

    halo_spec = lambda col: pl.BlockSpec((8, W), lambda i, c=col: (jnp.maximum(i * hb - 1, 0), c))
    return _pcall(
        body, name="conv_fwd", grid=(T // tm,),
        in_specs=[_zblock(tm, c0), _zblock(tm, c0 + 1), _zblock(tm, c0 + 2), _zblock(tm, c0 + 3),
                  halo_spec(c0), halo_spec(c0 + 2),
                  pl.BlockSpec((CONV_WIDTH, W), lambda i: (0, 0)),
                  pl.BlockSpec((1, W), lambda i: (0, 0))],
        out_specs=pl.BlockSpec((tm, W), lambda i: (i, 0)),
        out_shape=jax.ShapeDtypeStruct((T, W), bf16),
        compiler_params=_params("parallel"),
    )(z, z, z, z, z, z, cw, cb)


def _conv_bwd(z, dy, cw, cb, tm):
    T = z.shape[0]
    c0 = OFF_CONV // W
    hb = tm // 8
    nt = T // tm

    def body(ax_ref, ab_ref, ac_ref, ag_ref, hx_ref, hc_ref, nb_ref, ng_ref, dy_ref, ndy_ref,
             cw_ref, cb_ref, dz_ref, gcw_ref, gcb_ref):
        i = pl.program_id(0)

        @pl.when(i == 0)
        def _():
            gcw_ref[...] = jnp.zeros_like(gcw_ref)
            gcb_ref[...] = jnp.zeros_like(gcb_ref)

        ax, ab, ac, ag = ax_ref[...], ab_ref[...], ac_ref[...], ag_ref[...]
        w0, w1, w2 = cw_ref[0:1, :], cw_ref[1:2, :], cw_ref[2:3, :]
        zc = ac * ax
        halo = jnp.where(i > 0, hc_ref[...] * hx_ref[...], 0.0)
        z1, z2 = _conv_taps(zc, halo, cw_ref, tm)
        yb = w2 * zc + w1 * z1 + w0 * z2 + cb_ref[...]
        ya = ab * yb
        dyg = dy_ref[...]
        dag = dyg * ya * _dsilu(ag)
        dya = dyg * _silu(ag)
        dab = dya * yb
        dyc = dya * ab
        nxt = jnp.where(i < nt - 1, ndy_ref[...] * _silu(ng_ref[...]) * nb_ref[...], 0.0)
        ext = jnp.concatenate([dyc, nxt], axis=0)
        d1 = pltpu.roll(ext, tm + 8 - 1, 0)[:tm]
        d2 = pltpu.roll(ext, tm + 8 - 2, 0)[:tm]
        dzc = w2 * dyc + w1 * d1 + w0 * d2
        dz_ref[:, 0:W] = (dzc * ac).astype(bf16)
        dz_ref[:, W:2 * W] = dab.astype(bf16)
        dz_ref[:, 2 * W:3 * W] = (dzc * ax).astype(bf16)
        dz_ref[:, 3 * W:4 * W] = dag.astype(bf16)
        gcb_ref[...] += jnp.sum(dyc, axis=0, keepdims=True)
        gcw_ref[...] += _rows3(jnp.sum(dyc * z2, axis=0, keepdims=True),
                               jnp.sum(dyc * z1, axis=0, keepdims=True),
                               jnp.sum(dyc * zc, axis=0, keepdims=True), W)

    prev_spec = lambda col: pl.BlockSpec((8, W), lambda i, c=col: (jnp.maximum(i * hb - 1, 0), c))
    next_z = lambda col: pl.BlockSpec((8, W), lambda i, c=col: (jnp.minimum((i + 1) * hb, T // 8 - 1), c))
    next_dy = pl.BlockSpec((8, W), lambda i: (jnp.minimum((i + 1) * hb, T // 8 - 1), 0))
    return _pcall(
        body, name="conv_bwd", grid=(nt,),
        in_specs=[_zblock(tm, c0), _zblock(tm, c0 + 1), _zblock(tm, c0 + 2), _zblock(tm, c0 + 3),
                  prev_spec(c0), prev_spec(c0 + 2), next_z(c0 + 1), next_z(c0 + 3),
                  pl.BlockSpec((tm, W), lambda i: (i, 0)), next_dy,
                  pl.BlockSpec((CONV_WIDTH, W), lambda i: (0, 0)),
                  pl.BlockSpec((1, W), lambda i: (0, 0))],
        out_specs=[pl.BlockSpec((tm, 4 * W), lambda i: (i, 0)),
                   pl.BlockSpec((8, W), lambda i: (0, 0)),
                   pl.BlockSpec((1, W), lambda i: (0, 0))],
        out_shape=[jax.ShapeDtypeStruct((T, 4 * W), bf16), jax.ShapeDtypeStruct((8, W), f32),
                   jax.ShapeDtypeStruct((1, W), f32)],
        compiler_params=_params("arbitrary"),
    )(z, z, z, z, z, z, z, z, dy, dy, cw, cb)


def _sgu_core(dv_ref, gv_ref, sw_ref, sbe_ref, s_scr, tm):
    v = dv_ref[...]
    gm = _group_mean_matrix(W, DH)
    rv = lax.rsqrt(_group_mean(v * v, gm) + EPS)
    vh = v * rv
    vnb = (vh * gv_ref[...]).astype(bf16)
    causal = _iota2((SGU_CHUNK, SGU_CHUNK), 0) >= _iota2((SGU_CHUNK, SGU_CHUNK), 1)
    wgs = [jnp.where(causal, sw_ref[g], 0.0).astype(bf16) for g in range(NH)]
    for c in range(tm // SGU_CHUNK):
        rows = slice(c * SGU_CHUNK, (c + 1) * SGU_CHUNK)
        for g in range(NH):
            cols = slice(g * DH, (g + 1) * DH)
            s_scr[rows, cols] = _mm(wgs[g], vnb[rows, cols])
    sb = sbe_ref[...]
    s = s_scr[...] + jnp.concatenate([sb] * (tm // SGU_CHUNK), axis=0)
    return v, rv, vh, vnb, wgs, causal, gm, s


def _sgu_fwd(z, gv, sw, sbe, tm):
    T = z.shape[0]
    c0 = OFF_SGU // W

    def body(du_ref, dv_ref, dg_ref, gv_ref, sw_ref, sbe_ref, y_ref, s_scr):
        s = _sgu_core(dv_ref, gv_ref, sw_ref, sbe_ref, s_scr, tm)[-1]
        y_ref[...] = ((du_ref[...] * s) * _silu(dg_ref[...])).astype(bf16)

    return _pcall(
        body, name="sgu_fwd", grid=(T // tm,),
        in_specs=[_zblock(tm, c0), _zblock(tm, c0 + 1), _zblock(tm, c0 + 2),
                  pl.BlockSpec((1, W), lambda i: (0, 0)),
                  pl.BlockSpec((NH, SGU_CHUNK, SGU_CHUNK), lambda i: (0, 0, 0)),
                  pl.BlockSpec((SGU_CHUNK, W), lambda i: (0, 0))],
        out_specs=pl.BlockSpec((tm, W), lambda i: (i, 0)),
        out_shape=jax.ShapeDtypeStruct((T, W), bf16),
        scratch_shapes=[pltpu.VMEM((tm, W), f32)],
        compiler_params=_params("parallel"),
    )(z, z, z, gv, sw, sbe)


def _sgu_bwd(z, dy, gv, sw, sbe, tm):
    T = z.shape[0]
    c0 = OFF_SGU // W
    nt = T // tm

    def body(du_ref, dv_ref, dg_ref, dy_ref, gv_ref, sw_ref, sbe_ref,
             dz_ref, gsw_ref, gsb_ref, ggv_ref, s_scr, dvn_scr, sb_acc):
        i = pl.program_id(0)

        @pl.when(i == 0)
        def _():
            gsw_ref[...] = jnp.zeros_like(gsw_ref)
            ggv_ref[...] = jnp.zeros_like(ggv_ref)
            sb_acc[...] = jnp.zeros_like(sb_acc)

        v, rv, vh, vnb, wgs, causal, gm, s = _sgu_core(dv_ref, gv_ref, sw_ref, sbe_ref, s_scr, tm)
        du, dg, dyv = du_ref[...], dg_ref[...], dy_ref[...]
        ddg = dyv * (du * s) * _dsilu(dg)
        t = dyv * _silu(dg)
        ddu = t * s
        ds = t * du
        dsb = ds.astype(bf16)
        acc = sb_acc[...]
        for c in range(tm // SGU_CHUNK):
            rows = slice(c * SGU_CHUNK, (c + 1) * SGU_CHUNK)
            acc = acc + ds[rows, :]
            for g in range(NH):
                cols = slice(g * DH, (g + 1) * DH)
                gsw_ref[g] += jnp.where(causal, _mm_nt(dsb[rows, cols], vnb[rows, cols]), 0.0)
                dvn_scr[rows, cols] = _mm_tn(wgs[g], dsb[rows, cols])
        sb_acc[...] = acc
        dvn = dvn_scr[...]
        ggv_ref[...] += jnp.sum(dvn * vh, axis=0, keepdims=True)
        u = dvn * gv_ref[...]
        ddv = rv * u - v * (rv * rv * rv) * _group_mean(u * v, gm)
        dz_ref[:, 0:W] = ddu.astype(bf16)
        dz_ref[:, W:2 * W] = ddv.astype(bf16)
        dz_ref[:, 2 * W:3 * W] = ddg.astype(bf16)

        @pl.when(i == nt - 1)
        def _():
            gsb_ref[...] = _group_mean(sb_acc[...], gm) * float(DH)

    return _pcall(
        body, name="sgu_bwd", grid=(nt,),
        in_specs=[_zblock(tm, c0), _zblock(tm, c0 + 1), _zblock(tm, c0 + 2),
                  pl.BlockSpec((tm, W), lambda i: (i, 0)),
                  pl.BlockSpec((1, W), lambda i: (0, 0)),
                  pl.BlockSpec((NH, SGU_CHUNK, SGU_CHUNK), lambda i: (0, 0, 0)),
                  pl.BlockSpec((SGU_CHUNK, W), lambda i: (0, 0))],
        out_specs=[pl.BlockSpec((tm, 3 * W), lambda i: (i, 0)),
                   pl.BlockSpec((NH, SGU_CHUNK, SGU_CHUNK), lambda i: (0, 0, 0)),
                   pl.BlockSpec((SGU_CHUNK, W), lambda i: (0, 0)),
                   pl.BlockSpec((1, W), lambda i: (0, 0))],
        out_shape=[jax.ShapeDtypeStruct((T, 3 * W), bf16),
                   jax.ShapeDtypeStruct((NH, SGU_CHUNK, SGU_CHUNK), f32),
                   jax.ShapeDtypeStruct((SGU_CHUNK, W), f32),
                   jax.ShapeDtypeStruct((1, W), f32)],
        scratch_shapes=[pltpu.VMEM((tm, W), f32), pltpu.VMEM((tm, W), f32), pltpu.VMEM((SGU_CHUNK, W), f32)],
        compiler_params=_params("arbitrary"),
    )(z, z, z, dy, gv, sw, sbe)


def _hgrn_gates(cq_ref, cf_ref, lb_ref):
    q = _silu(cq_ref[...])
    sig = _sigmoid(cf_ref[...])
    lb = lb_ref[...]
    g = lb + (1.0 - lb) * sig
    return q, sig, g, jnp.log(g), (1.0 - lb) * (1.0 - sig)


def _hgrn_chunk_terms(lgc, qc, kc):
    C = GLA_CHUNK
    b = jnp.dot(_lower_tri(C), lgc, precision=HI, preferred_element_type=f32)
    bl = jnp.sum(lgc, axis=0, keepdims=True)
    mid = jnp.sum(jnp.where(_iota2((C, W), 0) <= C // 2, lgc, 0.0), axis=0, keepdims=True)
    eb = jnp.exp(b)
    em = jnp.exp(jnp.minimum(b - mid, EXP_CLAMP))
    emi = jnp.exp(jnp.minimum(mid - b, EXP_CLAMP))
    ek = jnp.exp(bl - b)
    return dict(eb=eb, em=em, emi=emi, ek=ek, ebl=jnp.exp(bl),
                qe=qc * eb, qm=qc * em, km=kc * emi, kd=kc * ek)


def _hgrn_fwd(z, lb, gain, tm):
    T = z.shape[0]
    c0 = OFF_HGRN // W
    C = GLA_CHUNK
    ncp = tm // C

    def body(cq_ref, cf_ref, ci_ref, cg_ref, lb_ref, gn_ref, y_ref, o_ref, st_ref, state, o_scr):
        @pl.when(pl.program_id(0) == 0)
        def _():
            state[...] = jnp.zeros_like(state)

        q, sig, g, lg, kf = _hgrn_gates(cq_ref, cf_ref, lb_ref)
        v = ci_ref[...]
        causal = _iota2((C, C), 0) >= _iota2((C, C), 1)
        for c in range(ncp):
            rows = slice(c * C, (c + 1) * C)
            tr = _hgrn_chunk_terms(lg[rows], q[rows], kf[rows])
            vb = v[rows].astype(bf16)
            qmb, kmb, qeb, kdb = (tr[n].astype(bf16) for n in ("qm", "km", "qe", "kd"))
            for h in range(NH):
                cols = slice(h * DH, (h + 1) * DH)
                hr = slice(h * DH, (h + 1) * DH)
                st = state[hr, :]
                st_ref[c, hr, :] = st
                p = jnp.where(causal, _mm_nt(qmb[:, cols], kmb[:, cols]), 0.0)
                o_scr[rows, cols] = _mm(p.astype(bf16), vb[:, cols]) + _mm_nt(qeb[:, cols], st.astype(bf16))
                state[hr, :] = st * tr["ebl"][:, cols] + _mm_tn(vb[:, cols], kdb[:, cols])
        o = o_scr[...]
        o_ref[...] = o
        gm = _group_mean_matrix(W, DH)
        r = lax.rsqrt(_group_mean(o * o, gm) + EPS)
        y_ref[...] = ((o * r * gn_ref[...]) * _silu(cg_ref[...])).astype(bf16)

    return _pcall(
        body, name="hgrn_fwd", grid=(T // tm,),
        in_specs=[_zblock(tm, c0), _zblock(tm, c0 + 1), _zblock(tm, c0 + 2), _zblock(tm, c0 + 3),
                  pl.BlockSpec((1, W), lambda i: (0, 0)), pl.BlockSpec((1, W), lambda i: (0, 0))],
        out_specs=[pl.BlockSpec((tm, W), lambda i: (i, 0)),
                   pl.BlockSpec((tm, W), lambda i: (i, 0)),
                   pl.BlockSpec((ncp, W, DH), lambda i: (i, 0, 0))],
        out_shape=[jax.ShapeDtypeStruct((T, W), bf16), jax.ShapeDtypeStruct((T, W), f32),
                   jax.ShapeDtypeStruct((T // C, W, DH), f32)],
        scratch_shapes=[pltpu.VMEM((W, DH), f32), pltpu.VMEM((tm, W), f32)],
        compiler_params=_params("arbitrary"),
    )(z, z, z, z, lb, gain)


def _hgrn_bwd(z, lb, gain, o_pre, states, dy, tm):
    T = z.shape[0]
    c0 = OFF_HGRN // W
    C = GLA_CHUNK
    ncp = tm // C
    nt = T // tm

    def body(cq_ref, cf_ref, ci_ref, cg_ref, lb_ref, gn_ref, o_ref, st_ref, dy_ref,
             dz_ref, ggn_ref, glb_ref, dstate, dq_s, dk_s, dv_s, db_s):
        @pl.when(pl.program_id(0) == 0)
        def _():
            dstate[...] = jnp.zeros_like(dstate)
            ggn_ref[...] = jnp.zeros_like(ggn_ref)
            glb_ref[...] = jnp.zeros_like(glb_ref)

        cq, cg = cq_ref[...], cg_ref[...]
        q, sig, g, lg, kf = _hgrn_gates(cq_ref, cf_ref, lb_ref)
        lb = lb_ref[...]
        v = ci_ref[...]
        o = o_ref[...]
        gm = _group_mean_matrix(W, DH)
        r = lax.rsqrt(_group_mean(o * o, gm) + EPS)
        oh = o * r
        gn = gn_ref[...]
        dyv = dy_ref[...]
        dcg = dyv * (oh * gn) * _dsilu(cg)
        don = dyv * _silu(cg)
        ggn_ref[...] += jnp.sum(don * oh, axis=0, keepdims=True)
        u = don * gn
        do = r * u - o * (r * r * r) * _group_mean(u * o, gm)

        causal = _iota2((C, C), 0) >= _iota2((C, C), 1)
        last_row = _iota2((C, DH), 0) == C - 1
        for c in reversed(range(ncp)):
            rows = slice(c * C, (c + 1) * C)
            tr = _hgrn_chunk_terms(lg[rows], q[rows], kf[rows])
            vb = v[rows].astype(bf16)
            dob = do[rows].astype(bf16)
            qmb, kmb, qeb, kdb = (tr[n].astype(bf16) for n in ("qm", "km", "qe", "kd"))
            for h in range(NH):
                cols = slice(h * DH, (h + 1) * DH)
                hr = slice(h * DH, (h + 1) * DH)
                st0 = st_ref[c, hr, :]
                dst = dstate[hr, :]
                dstb = dst.astype(bf16)
                doh = dob[:, cols]
                p = jnp.where(causal, _mm_nt(qmb[:, cols], kmb[:, cols]), 0.0)
                dp = jnp.where(causal, _mm_nt(doh, vb[:, cols]), 0.0)
                dpb = dp.astype(bf16)
                dvh = _mm_tn(p.astype(bf16), doh) + _mm_nt(kdb[:, cols], dstb)
                dqm = _mm(dpb, kmb[:, cols])
                dkm = _mm_tn(dpb, qmb[:, cols])
                dqe = _mm(doh, st0.astype(bf16))
                dkd = _mm(vb[:, cols], dstb)
                ebl = tr["ebl"][:, cols]
                dstate[hr, :] = dst * ebl + _mm_tn(doh, qeb[:, cols])
                qm, km, qe, kd = (a[:, cols].astype(f32) for a in (qmb, kmb, qeb, kdb))
                kterm = dkd * kd
                dbh = dqm * qm - dkm * km + dqe * qe - kterm
                extra = jnp.sum(kterm, axis=0, keepdims=True) + ebl * jnp.sum(dst * st0, axis=0, keepdims=True)
                dbh = dbh + jnp.where(last_row, extra, 0.0)
                dq_s[rows, cols] = dqm * tr["em"][:, cols] + dqe * tr["eb"][:, cols]
                dk_s[rows, cols] = dkm * tr["emi"][:, cols] + dkd * tr["ek"][:, cols]
                dv_s[rows, cols] = dvh
                db_s[rows, cols] = dbh
            db_s[rows, :] = jnp.dot(_upper_tri(C), db_s[rows, :], precision=HI, preferred_element_type=f32)
        dlg = db_s[...]
        dk = dk_s[...]
        dsig = sig * (1.0 - sig)
        one_lb = 1.0 - lb
        dcf = (dlg / g - dk) * one_lb * dsig
        glb_ref[...] += jnp.sum((dlg / g - dk) * (1.0 - sig), axis=0, keepdims=True)
        dz_ref[:, 0:W] = (dq_s[...] * _dsilu(cq)).astype(bf16)
        dz_ref[:, W:2 * W] = dcf.astype(bf16)
        dz_ref[:, 2 * W:3 * W] = dv_s[...].astype(bf16)
        dz_ref[:, 3 * W:4 * W] = dcg.astype(bf16)

    rev = lambda i: nt - 1 - i
    zb = lambda col: pl.BlockSpec((tm, W), lambda i, c=col: (rev(i), c))
    return _pcall(
        body, name="hgrn_bwd", grid=(nt,),
        in_specs=[zb(c0), zb(c0 + 1), zb(c0 + 2), zb(c0 + 3),
                  pl.BlockSpec((1, W), lambda i: (0, 0)), pl.BlockSpec((1, W), lambda i: (0, 0)),
                  pl.BlockSpec((tm, W), lambda i: (rev(i), 0)),
                  pl.BlockSpec((ncp, W, DH), lambda i: (rev(i), 0, 0)),
                  pl.BlockSpec((tm, W), lambda i: (rev(i), 0))],
        out_specs=[pl.BlockSpec((tm, 4 * W), lambda i: (rev(i), 0)),
                   pl.BlockSpec((1, W), lambda i: (0, 0)),
                   pl.BlockSpec((1, W), lambda i: (0, 0))],
        out_shape=[jax.ShapeDtypeStruct((T, 4 * W), bf16), jax.ShapeDtypeStruct((1, W), f32),
                   jax.ShapeDtypeStruct((1, W), f32)],
        scratch_shapes=[pltpu.VMEM((W, DH), f32)] + [pltpu.VMEM((tm, W), f32)] * 4,
        compiler_params=_params("arbitrary"),
    )(z, z, z, z, lb, gain, o_pre, states, dy)


def _attn_prep(z, fbias, gq, gk, tm):
    T = z.shape[0]
    c0 = OFF_ATT // W

    def body(q_ref, k_ref, v_ref, f_ref, fb_ref, gq_ref, gk_ref, qh_ref, kh_ref, vh_ref, cum_ref, carry):
        @pl.when(pl.program_id(0) == 0)
        def _():
            carry[...] = jnp.zeros_like(carry)

        gm = _group_mean_matrix(W, DH)
        q, k, v = q_ref[...], k_ref[...], v_ref[...]
        qs = q * lax.rsqrt(_group_mean(q * q, gm) + EPS) * (gq_ref[...] * (DH ** -0.5))
        kn = k * lax.rsqrt(_group_mean(k * k, gm) + EPS) * gk_ref[...]
        for h in range(NH):
            cols = slice(h * DH, (h + 1) * DH)
            qh_ref[h] = qs[:, cols].astype(bf16)
            kh_ref[h] = kn[:, cols].astype(bf16)
            vh_ref[h] = v[:, cols].astype(bf16)
        ls = _logsigmoid(f_ref[...] + fb_ref[...])
        cum_ref[...] = jnp.dot(_lower_tri(tm), ls, precision=HI, preferred_element_type=f32) + carry[...]
        carry[...] += jnp.sum(ls, axis=0, keepdims=True)

    hspec = pl.BlockSpec((NH, tm, DH), lambda i: (0, i, 0))
    return _pcall(
        body, name="attn_prep", grid=(T // tm,),
        in_specs=[_zblock(tm, c0), _zblock(tm, c0 + 1), _zblock(tm, c0 + 2),
                  pl.BlockSpec((tm, 128), lambda i: (i, OFF_F // 128)),
                  pl.BlockSpec((1, 128), lambda i: (0, 0)),
                  pl.BlockSpec((1, W), lambda i: (0, 0)), pl.BlockSpec((1, W), lambda i: (0, 0))],
        out_specs=[hspec, hspec, hspec, pl.BlockSpec((tm, 128), lambda i: (i, 0))],
        out_shape=[jax.ShapeDtypeStruct((NH, T, DH), bf16)] * 3 + [jax.ShapeDtypeStruct((T, 128), f32)],
        scratch_shapes=[pltpu.VMEM((1, 128), f32)],
        compiler_params=_params("arbitrary"),
    )(z, z, z, z, fbias, gq, gk)


def _attn_logits(q, k, cq, ck, qi, ki, bq, bk):
    s = _mm_nt(q, k) + cq - ck
    rows = qi * bq + _iota2((bq, bk), 0)
    cols = ki * bk + _iota2((bq, bk), 1)
    mask = rows >= cols
    return jnp.where(mask, s, MASK_VALUE), mask


def _attn_fwd(qh, kh, vh, cq, ck, bq):
    T = qh.shape[1]
    nq = T // bq
    bk = bq

    def body(q_ref, k_ref, v_ref, cq_ref, ck_ref, o_ref, lse_ref, m_s, l_s, acc_s):
        qi, ki = pl.program_id(1), pl.program_id(2)

        @pl.when(ki == 0)
        def _():
            m_s[...] = jnp.full_like(m_s, MASK_VALUE)
            l_s[...] = jnp.zeros_like(l_s)
            acc_s[...] = jnp.zeros_like(acc_s)

        @pl.when(ki <= qi)
        def _():
            s, _ = _attn_logits(q_ref[...], k_ref[...], cq_ref[...], ck_ref[...], qi, ki, bq, bk)
            m_old = m_s[...]
            m_new = jnp.maximum(m_old, jnp.max(s, axis=1, keepdims=True))
            p = jnp.exp(s - m_new)
            alpha = jnp.exp(m_old - m_new)
            l_s[...] = alpha * l_s[...] + jnp.sum(p, axis=1, keepdims=True)
            acc_s[...] = alpha * acc_s[...] + _mm(p.astype(bf16), v_ref[...])
            m_s[...] = m_new

        @pl.when(ki == nq - 1)
        def _():
            o_ref[...] = acc_s[...] / l_s[...]
            lse_ref[...] = m_s[...] + jnp.log(l_s[...])

    kv = lambda h, qi, ki: (h, jnp.minimum(ki, qi), 0)
    return _pcall(
        body, name="attn_fwd", grid=(NH, nq, nq),
        in_specs=[pl.BlockSpec((None, bq, DH), lambda h, qi, ki: (h, qi, 0)),
                  pl.BlockSpec((None, bk, DH), kv),
                  pl.BlockSpec((None, bk, DH), kv),
                  pl.BlockSpec((None, bq, 1), lambda h, qi, ki: (h, qi, 0)),
                  pl.BlockSpec((None, 1, bk), lambda h, qi, ki: (h, 0, jnp.minimum(ki, qi)))],
        out_specs=[pl.BlockSpec((None, bq, DH), lambda h, qi, ki: (h, qi, 0)),
                   pl.BlockSpec((None, bq, 1), lambda h, qi, ki: (h, qi, 0))],
        out_shape=[jax.ShapeDtypeStruct((NH, T, DH), f32), jax.ShapeDtypeStruct((NH, T, 1), f32)],
        scratch_shapes=[pltpu.VMEM((bq, 1), f32), pltpu.VMEM((bq, 1), f32), pltpu.VMEM((bq, DH), f32)],
        compiler_params=_params("parallel", "parallel", "arbitrary"),
    )(qh, kh, vh, cq, ck)


def _attn_bwd_prep(dy, oh, z, tm):
    T = dy.shape[0]
    cg = OFF_ATT // W + 3

    def body(dy_ref, o_ref, g_ref, doh_ref, dl_ref):
        do = dy_ref[...] * _silu(g_ref[...])
        for h in range(NH):
            d = do[:, h * DH:(h + 1) * DH].astype(bf16)
            doh_ref[h] = d
            dl_ref[h] = jnp.sum(d.astype(f32) * o_ref[h], axis=1, keepdims=True)

    return _pcall(
        body, name="attn_bwd_prep", grid=(T // tm,),
        in_specs=[pl.BlockSpec((tm, W), lambda i: (i, 0)),
                  pl.BlockSpec((NH, tm, DH), lambda i: (0, i, 0)),
                  _zblock(tm, cg)],
        out_specs=[pl.BlockSpec((NH, tm, DH), lambda i: (0, i, 0)),
                   pl.BlockSpec((NH, tm, 1), lambda i: (0, i, 0))],
        out_shape=[jax.ShapeDtypeStruct((NH, T, DH), bf16), jax.ShapeDtypeStruct((NH, T, 1), f32)],
        compiler_params=_params("parallel"),
    )(dy, oh, z)


def _attn_bwd(qh, kh, vh, cq, ck, doh, lse, delta, bq):
    T = qh.shape[1]
    nq = T // bq
    bk = bq

    def body(q_ref, k_ref, v_ref, cq_ref, ck_ref, do_ref, lse_ref, dl_ref,
             dq_ref, dk_ref, dv_ref, dck_ref, dcq_ref, dk_s, dv_s, dck_s):
        ki, qi = pl.program_id(1), pl.program_id(2)

        @pl.when((ki == 0) & (qi == 0))
        def _():
            dq_ref[...] = jnp.zeros_like(dq_ref)
            dcq_ref[...] = jnp.zeros_like(dcq_ref)

        @pl.when(qi == 0)
        def _():
            dk_s[...] = jnp.zeros_like(dk_s)
            dv_s[...] = jnp.zeros_like(dv_s)
            dck_s[...] = jnp.zeros_like(dck_s)

        @pl.when(qi >= ki)
        def _():
            q, k, v, do = q_ref[...], k_ref[...], v_ref[...], do_ref[...]
            s, mask = _attn_logits(q, k, cq_ref[...], ck_ref[...], qi, ki, bq, bk)
            p = jnp.where(mask, jnp.exp(s - lse_ref[...]), 0.0)
            dv_s[...] += _mm_tn(p.astype(bf16), do)
            ds = p * (_mm_nt(do, v) - dl_ref[...])
            dck_s[...] -= jnp.sum(ds, axis=0, keepdims=True)
            dsb = ds.astype(bf16)
            dk_s[...] += _mm_tn(dsb, q)
            r0 = pl.multiple_of(qi * bq, bq)
            dq_ref[pl.ds(r0, bq), :] += _mm(dsb, k)
            dcq_ref[pl.ds(r0, bq), :] += jnp.sum(ds, axis=1, keepdims=True)

        @pl.when(qi == nq - 1)
        def _():
            dk_ref[...] = dk_s[...]
            dv_ref[...] = dv_s[...]
            dck_ref[...] = dck_s[...]

    qm = lambda h, ki, qi: (h, jnp.maximum(qi, ki), 0)
    return _pcall(
        body, name="attn_bwd", grid=(NH, nq, nq),
        in_specs=[pl.BlockSpec((None, bq, DH), qm),
                  pl.BlockSpec((None, bk, DH), lambda h, ki, qi: (h, ki, 0)),
                  pl.BlockSpec((None, bk, DH), lambda h, ki, qi: (h, ki, 0)),
                  pl.BlockSpec((None, bq, 1), qm),
                  pl.BlockSpec((None, 1, bk), lambda h, ki, qi: (h, 0, ki)),
                  pl.BlockSpec((None, bq, DH), qm),
                  pl.BlockSpec((None, bq, 1), qm),
                  pl.BlockSpec((None, bq, 1), qm)],
        out_specs=[pl.BlockSpec((None, T, DH), lambda h, ki, qi: (h, 0, 0)),
                   pl.BlockSpec((None, bk, DH), lambda h, ki, qi: (h, ki, 0)),
                   pl.BlockSpec((None, bk, DH), lambda h, ki, qi: (h, ki, 0)),
                   pl.BlockSpec((None, 1, bk), lambda h, ki, qi: (h, 0, ki)),
                   pl.BlockSpec((None, T, 1), lambda h, ki, qi: (h, 0, 0))],
        out_shape=[jax.ShapeDtypeStruct((NH, T, DH), f32)] * 3 + [jax.ShapeDtypeStruct((NH, 1, T), f32),
                                                                  jax.ShapeDtypeStruct((NH, T, 1), f32)],
        scratch_shapes=[pltpu.VMEM((bk, DH), f32), pltpu.VMEM((bk, DH), f32), pltpu.VMEM((1, bk), f32)],
        compiler_params=_params("parallel", "arbitrary", "arbitrary"),
    )(qh, kh, vh, cq, ck, doh, lse, delta)


def _attn_post(z, dy, oh, dqh, dkh, dvh, dck, dcq, fbias, gq, gk, tm):
    T = z.shape[0]
    c0 = OFF_ATT // W
    nt = T // tm

    def body(q_ref, k_ref, g_ref, f_ref, dy_ref, o_ref, dq_ref, dk_ref, dv_ref, dck_ref, dcq_ref, fb_ref, gq_ref,
             gk_ref, dz_ref, ggq_ref, ggk_ref, gfb_ref, carry, a_s, b_s, c_s, d_s):
        @pl.when(pl.program_id(0) == 0)
        def _():
            carry[...] = jnp.zeros_like(carry)
            ggq_ref[...] = jnp.zeros_like(ggq_ref)
            ggk_ref[...] = jnp.zeros_like(ggk_ref)
            gfb_ref[...] = jnp.zeros_like(gfb_ref)

        for h in range(NH):
            cols = slice(h * DH, (h + 1) * DH)
            a_s[:, cols] = dq_ref[h]
            b_s[:, cols] = dk_ref[h]
            c_s[:, cols] = dv_ref[h]
            d_s[:, cols] = o_ref[h]
        gm = _group_mean_matrix(W, DH)
        hs = jnp.where((_iota2((W, W), 0) & (DH - 1)) == (_iota2((W, W), 1) & (DH - 1)), 1.0, 0.0).astype(f32)

        def norm_bwd(x, dn, gain):
            r = lax.rsqrt(_group_mean(x * x, gm) + EPS)
            gg = jnp.sum(dn * x * r, axis=0, keepdims=True)
            u = dn * gain
            return r * u - x * (r * r * r) * _group_mean(u * x, gm), gg

        q, k, gate = q_ref[...], k_ref[...], g_ref[...]
        dq, ggq = norm_bwd(q, a_s[...] * (DH ** -0.5), gq_ref[...])
        dk, ggk = norm_bwd(k, b_s[...], gk_ref[...])
        ggq_ref[...] += jnp.dot(jnp.broadcast_to(ggq, (8, W)), hs, precision=HI, preferred_element_type=f32)[0:1]
        ggk_ref[...] += jnp.dot(jnp.broadcast_to(ggk, (8, W)), hs, precision=HI, preferred_element_type=f32)[0:1]
        dgate = dy_ref[...] * d_s[...] * _dsilu(gate)
        dck_v = dck_ref[...]
        lane = _iota2((tm, 128), 1)
        for h in range(NH):
            dck_v = dck_v + jnp.where(lane == h, dcq_ref[h], 0.0)
        rc = jnp.dot(_upper_tri(tm), dck_v, precision=HI, preferred_element_type=f32) + carry[...]
        carry[...] += jnp.sum(dck_v, axis=0, keepdims=True)
        f = f_ref[...] + fb_ref[...]
        df = jnp.where(_iota2((tm, 128), 1) < NH, rc * _sigmoid(-f), 0.0)
        gfb_ref[...] += jnp.sum(df, axis=0, keepdims=True)
        dz_ref[:, 0:W] = dq.astype(bf16)
        dz_ref[:, W:2 * W] = dk.astype(bf16)
        dz_ref[:, 2 * W:3 * W] = c_s[...].astype(bf16)
        dz_ref[:, 3 * W:4 * W] = dgate.astype(bf16)
        dz_ref[:, 4 * W:4 * W + 128] = df.astype(bf16)

    rev = lambda i: nt - 1 - i
    zb = lambda col: pl.BlockSpec((tm, W), lambda i, c=col: (rev(i), c))
    hspec = pl.BlockSpec((NH, tm, DH), lambda i: (0, rev(i), 0))
    return _pcall(
        body, name="attn_post", grid=(nt,),
        in_specs=[zb(c0), zb(c0 + 1), zb(c0 + 3),
                  pl.BlockSpec((tm, 128), lambda i: (rev(i), OFF_F // 128)),
                  pl.BlockSpec((tm, W), lambda i: (rev(i), 0)),
                  hspec, hspec, hspec, hspec,
                  pl.BlockSpec((tm, 128), lambda i: (rev(i), 0)),
                  pl.BlockSpec((NH, tm, 1), lambda i: (0, rev(i), 0)),
                  pl.BlockSpec((1, 128), lambda i: (0, 0)),
                  pl.BlockSpec((1, W), lambda i: (0, 0)), pl.BlockSpec((1, W), lambda i: (0, 0))],
        out_specs=[pl.BlockSpec((tm, 4 * W + 128), lambda i: (rev(i), 0)),
                   pl.BlockSpec((1, W), lambda i: (0, 0)), pl.BlockSpec((1, W), lambda i: (0, 0)),
                   pl.BlockSpec((1, 128), lambda i: (0, 0))],
        out_shape=[jax.ShapeDtypeStruct((T, 4 * W + 128), bf16), jax.ShapeDtypeStruct((1, W), f32),
                   jax.ShapeDtypeStruct((1, W), f32), jax.ShapeDtypeStruct((1, 128), f32)],
        scratch_shapes=[pltpu.VMEM((1, 128), f32)] + [pltpu.VMEM((tm, W), f32)] * 4,
        compiler_params=_params("arbitrary"),
    )(z, z, z, z, dy, oh, dqh, dkh, dvh, dck, dcq, fbias, gq, gk)


def _merge_fwd(ya, oh, z, yc, yd, mb, x, p, wup, wo, gp, wpg, wpp, tm):
    T = x.shape[0]
    cg = OFF_ATT // W + 3

    def body(ya_ref, oh_ref, bg_ref, yc_ref, yd_ref, ml_ref, mb_ref, x_ref, p_ref, wup_ref, wo_ref, gp_ref,
             wpg_ref, wpp_ref, yb_ref, mg_ref, x1_ref, x2_ref, scr):
        for h in range(NH):
            scr[:, h * DH:(h + 1) * DH] = oh_ref[h]
        yb = (scr[...] * _silu(bg_ref[...])).astype(bf16)
        yb_ref[...] = yb
        ys = (ya_ref[...], yb, yc_ref[...], yd_ref[...])
        merged = jnp.zeros((tm, D), f32)
        for b in range(NBR):
            sg = _sigmoid(ml_ref[:, b * D:(b + 1) * D] + mb_ref[b:b + 1, :])
            merged = merged + sg * _mm(ys[b], wup_ref[b])
        mgb = merged.astype(bf16)
        mg_ref[...] = mgb
        x1 = x_ref[...] + _mm(mgb, wo_ref[...])
        x1_ref[...] = x1
        r = lax.rsqrt(jnp.mean(x1 * x1, axis=-1, keepdims=True) + EPS)
        hp = (x1 * r * gp_ref[...]).astype(bf16)
        gate = _sigmoid(_mm(hp, wpg_ref[...]))
        x2_ref[...] = x1 + gate * _mm(p_ref[...].astype(bf16), wpp_ref[...])

    row = lambda width: pl.BlockSpec((tm, width), lambda i: (i, 0))
    full = lambda *shape: pl.BlockSpec(shape, lambda i: (0,) * len(shape))
    return _pcall(
        body, name="merge_fwd", grid=(T // tm,),
        in_specs=[row(W), pl.BlockSpec((NH, tm, DH), lambda i: (0, i, 0)), _zblock(tm, cg), row(W), row(W),
                  pl.BlockSpec((tm, NBR * D), lambda i: (i, 0)), full(NBR, D), row(D), row(PLE),
                  full(NBR, W, D), full(D, D), full(1, D), full(D, D), full(PLE, D)],
        out_specs=[row(W), row(D), row(D), row(D)],
        out_shape=[jax.ShapeDtypeStruct((T, W), bf16), jax.ShapeDtypeStruct((T, D), bf16),
                   jax.ShapeDtypeStruct((T, D), f32), jax.ShapeDtypeStruct((T, D), f32)],
        scratch_shapes=[pltpu.VMEM((tm, W), f32)],
        compiler_params=_params("parallel"),
    )(ya, oh, z, yc, yd, z, mb, x, p, wup, wo, gp, wpg, wpp)


def _ple_bwd(dx2, x1, p, gp, wpg, wpp, tm):
    T = x1.shape[0]

    def body(dx2_ref, x1_ref, p_ref, gp_ref, wpg_ref, wpp_ref, dx1_ref, gwpg_ref, gwpp_ref, ggp_ref):
        @pl.when(pl.program_id(0) == 0)
        def _():
            gwpg_ref[...] = jnp.zeros_like(gwpg_ref)
            gwpp_ref[...] = jnp.zeros_like(gwpp_ref)
            ggp_ref[...] = jnp.zeros_like(ggp_ref)

        x1, dx2 = x1_ref[...], dx2_ref[...]
        r = lax.rsqrt(jnp.mean(x1 * x1, axis=-1, keepdims=True) + EPS)
        xh = x1 * r
        gp = gp_ref[...]
        hp = (xh * gp).astype(bf16)
        gate = _sigmoid(_mm(hp, wpg_ref[...]))
        pb = p_ref[...].astype(bf16)
        pp = _mm(pb, wpp_ref[...])
        dpre = ((dx2 * pp) * gate * (1.0 - gate)).astype(bf16)
        gwpp_ref[...] += _mm_tn(pb, (dx2 * gate).astype(bf16))
        gwpg_ref[...] += _mm_tn(hp, dpre)
        dhp = _mm_nt(dpre, wpg_ref[...])
        ggp_ref[...] += jnp.sum(dhp * xh, axis=0, keepdims=True)
        u = dhp * gp
        dx1_ref[...] = dx2 + r * u - x1 * (r * r * r) * jnp.mean(u * x1, axis=-1, keepdims=True)

    row = lambda width: pl.BlockSpec((tm, width), lambda i: (i, 0))
    full = lambda *shape: pl.BlockSpec(shape, lambda i: (0,) * len(shape))
    return _pcall(
        body, name="ple_bwd", grid=(T // tm,),
        in_specs=[row(D), row(D), row(PLE), full(1, D), full(D, D), full(PLE, D)],
        out_specs=[row(D), full(D, D), full(PLE, D), full(1, D)],
        out_shape=[jax.ShapeDtypeStruct((T, D), f32), jax.ShapeDtypeStruct((D, D), f32),
                   jax.ShapeDtypeStruct((PLE, D), f32), jax.ShapeDtypeStruct((1, D), f32)],
        compiler_params=_params("arbitrary"),
    )(dx2, x1, p, gp, wpg, wpp)


def _merge_bwd(dx1, mg, ya, yb, yc, yd, z, mb, wup, wo, tm):
    T = dx1.shape[0]

    def body(dx1_ref, mg_ref, ya_ref, yb_ref, yc_ref, yd_ref, ml_ref, mb_ref, wup_ref, wo_ref,
             dml_ref, dya_ref, dyb_ref, dyc_ref, dyd_ref, gwo_ref, gwup_ref, gmb_ref):
        @pl.when(pl.program_id(0) == 0)
        def _():
            gwo_ref[...] = jnp.zeros_like(gwo_ref)
            gwup_ref[...] = jnp.zeros_like(gwup_ref)
            gmb_ref[...] = jnp.zeros_like(gmb_ref)

        dx1b = dx1_ref[...].astype(bf16)
        gwo_ref[...] += _mm_tn(mg_ref[...], dx1b)
        dm = _mm_nt(dx1b, wo_ref[...])
        ys = (ya_ref, yb_ref, yc_ref, yd_ref)
        dys = (dya_ref, dyb_ref, dyc_ref, dyd_ref)
        for b in range(NBR):
            y = ys[b][...]
            up = _mm(y, wup_ref[b])
            sg = _sigmoid(ml_ref[:, b * D:(b + 1) * D] + mb_ref[b:b + 1, :])
            dup = (dm * sg).astype(bf16)
            dml = dm * up * sg * (1.0 - sg)
            gmb_ref[b:b + 1, :] += jnp.sum(dml, axis=0, keepdims=True)
            dml_ref[:, b * D:(b + 1) * D] = dml.astype(bf16)
            gwup_ref[b] += _mm_tn(y, dup)
            dys[b][...] = _mm_nt(dup, wup_ref[b])

    row = lambda width: pl.BlockSpec((tm, width), lambda i: (i, 0))
    full = lambda *shape: pl.BlockSpec(shape, lambda i: (0,) * len(shape))
    return _pcall(
        body, name="merge_bwd", grid=(T // tm,),
        in_specs=[row(D), row(D), row(W), row(W), row(W), row(W), row(NBR * D), full(NBR, D),
                  full(NBR, W, D), full(D, D)],
        out_specs=[row(NBR * D), row(W), row(W), row(W), row(W), full(D, D), full(NBR, W, D), full(NBR, D)],
        out_shape=[jax.ShapeDtypeStruct((T, NBR * D), bf16)] + [jax.ShapeDtypeStruct((T, W), f32)] * 4
        + [jax.ShapeDtypeStruct((D, D), f32), jax.ShapeDtypeStruct((NBR, W, D), f32),
           jax.ShapeDtypeStruct((NBR, D), f32)],
        compiler_params=_params("arbitrary"),
    )(dx1, mg, ya, yb, yc, yd, z, mb, wup, wo)


def _loss_head(y, target, tm):
    T = y.shape[0]

    def body(y_ref, t_ref, loss_ref, dy_ref, acc):
        i = pl.program_id(0)

        @pl.when(i == 0)
        def _():
            acc[...] = jnp.zeros_like(acc)

        e = y_ref[...] - t_ref[...]
        dy_ref[...] = e * (1.0 / D)
        acc[...] += jnp.sum(e * e, axis=0, keepdims=True)

        @pl.when(i == T // tm - 1)
        def _():
            loss_ref[...] = jnp.sum(acc[...], axis=1, keepdims=True) * (0.5 / D)

    return _pcall(
        body, name="loss_head", grid=(T // tm,),
        in_specs=[pl.BlockSpec((tm, D), lambda i: (i, 0)), pl.BlockSpec((tm, D), lambda i: (i, 0))],
        out_specs=[pl.BlockSpec((1, 1), lambda i: (0, 0)), pl.BlockSpec((tm, D), lambda i: (i, 0))],
        out_shape=[jax.ShapeDtypeStruct((1, 1), f32), jax.ShapeDtypeStruct((T, D), f32)],
        scratch_shapes=[pltpu.VMEM((1, D), f32)],
        compiler_params=_params("arbitrary"),
    )(y, target)


def _lb_softmax_rows(l_ref):
    rows = [l_ref[i:i + 1, :] for i in range(DEPTH)]
    m = rows[0]
    for r in rows[1:]:
        m = jnp.maximum(m, r)
    es = [jnp.exp(r - m) for r in rows]
    tot = es[0]
    for e in es[1:]:
        tot = tot + e
    return [e / tot for e in es]


def _lb_partial_sums(pr):
    sums = [jnp.zeros_like(pr[0])]
    for i in range(1, DEPTH):
        sums.append(sums[-1] + pr[i])
    return sums


def _stack_rows(rows, width):
    idx = _iota2((8, width), 0)
    out = jnp.zeros((8, width), f32)
    for i, r in enumerate(rows):
        out = jnp.where(idx == i, r, out)
    return out


def _lower_bounds(lb_logits):
    def body(l_ref, o_ref):
        sums = _lb_partial_sums(_lb_softmax_rows(l_ref))
        o_ref[...] = _stack_rows([jnp.clip(s, 0.0, 1.0) for s in sums], W)

    return _pcall(body, name="lower_bounds", out_shape=jax.ShapeDtypeStruct((8, W), f32))(lb_logits)


def _lower_bounds_bwd(lb_logits, dlower):
    def body(l_ref, d_ref, o_ref):
        pr = _lb_softmax_rows(l_ref)
        sums = _lb_partial_sums(pr)
        dl = [jnp.where((sums[i] > 0.0) & (sums[i] < 1.0), d_ref[i:i + 1, :], 0.0) for i in range(DEPTH)]
        dp = [jnp.zeros_like(pr[0])] * DEPTH
        run = jnp.zeros_like(pr[0])
        for j in reversed(range(1, DEPTH)):
            run = run + dl[j]
            dp[j] = run
        inner = pr[0] * dp[0]
        for j in range(1, DEPTH):
            inner = inner + pr[j] * dp[j]
        o_ref[...] = _stack_rows([pr[j] * (dp[j] - inner) for j in range(DEPTH)], W)

    return _pcall(body, name="lower_bounds_bwd", out_shape=jax.ShapeDtypeStruct((8, W), f32))(lb_logits, dlower)


def _row_tile(rows, cols, budget_bytes=1 << 20):
    if rows % 8:
        return rows
    best = 8
    for t in range(8, rows + 1, 8):
        if rows % t == 0 and t * cols * 4 <= budget_bytes:
            best = t
    return best


def _sum_slabs(land):
    _, R, C = land.shape
    tr = _row_tile(R, C * N_DEV)

    def body(l_ref, o_ref):
        acc = l_ref[0]
        for j in range(1, N_DEV):
            acc = acc + l_ref[j]
        o_ref[...] = acc

    return _pcall(
        body, name="sum_slabs", grid=(R // tr,),
        in_specs=[pl.BlockSpec((N_DEV, tr, C), lambda i: (0, i, 0))],
        out_specs=pl.BlockSpec((tr, C), lambda i: (i, 0)),
        out_shape=jax.ShapeDtypeStruct((R, C), f32),
        compiler_params=_params("parallel"),
    )(land)


def _adamw(w, g, m, v):
    R, C = w.shape
    tr = _row_tile(R, C)
    c1 = 1.0 / (1.0 - ADAM_B1 ** ADAM_STEP)
    c2 = 1.0 / (1.0 - ADAM_B2 ** ADAM_STEP)

    def body(w_ref, g_ref, m_ref, v_ref, d_ref, nm_ref, nv_ref):
        gv = g_ref[...]
        nm = ADAM_B1 * m_ref[...] + (1.0 - ADAM_B1) * gv
        nv = ADAM_B2 * v_ref[...] + (1.0 - ADAM_B2) * (gv * gv)
        nm_ref[...] = nm
        nv_ref[...] = nv
        d_ref[...] = -ADAM_LR * ((nm * c1) / (jnp.sqrt(nv * c2) + ADAM_EPS) + ADAM_WD * w_ref[...])

    spec = pl.BlockSpec((tr, C), lambda i: (i, 0))
    return _pcall(
        body, name="adamw", grid=(R // tr,),
        in_specs=[spec] * 4, out_specs=[spec] * 3,
        out_shape=[jax.ShapeDtypeStruct((R, C), f32)] * 3,
        compiler_params=_params("parallel"),
    )(w, g, m, v)


def _my_id():
    return lax.axis_index("x") * 4 + lax.axis_index("y") * 2 + lax.axis_index("c")


def _peer(k):
    x, y, c = lax.axis_index("x"), lax.axis_index("y"), lax.axis_index("c")
    kx, ky, kc = (k >> 2) & 1, (k >> 1) & 1, k & 1
    px, py, pc = x ^ kx, y ^ ky, c ^ kc
    return (px, py, pc), px * 4 + py * 2 + pc


def _all_gather(shards):
    n = len(shards)

    def body(*refs):
        srcs, outs = refs[:n], refs[n:2 * n]
        send_sems, recv_sems, local_sems = refs[2 * n:]
        me = _my_id()
        locals_ = [pltpu.make_async_copy(srcs[a], outs[a].at[me], local_sems.at[a]) for a in range(n)]
        for cp in locals_:
            cp.start()
        sends = []
        for k in range(1, N_DEV):
            peer, _ = _peer(k)
            for a in range(n):
                cp = pltpu.make_async_remote_copy(
                    src_ref=srcs[a], dst_ref=outs[a].at[me],
                    send_sem=send_sems.at[a, k - 1], recv_sem=recv_sems.at[a, k - 1],
                    device_id=peer, device_id_type=pl.DeviceIdType.MESH)
                cp.start()
                sends.append(cp)
        for k in range(1, N_DEV):
            peer, pid = _peer(k)
            for a in range(n):
                pltpu.make_async_remote_copy(
                    src_ref=srcs[a], dst_ref=outs[a].at[pid],
                    send_sem=send_sems.at[a, k - 1], recv_sem=recv_sems.at[a, k - 1],
                    device_id=peer, device_id_type=pl.DeviceIdType.MESH).wait_recv()
        for cp in sends:
            cp.wait_send()
        for cp in locals_:
            cp.wait()

    hbm = pl.BlockSpec(memory_space=pltpu.HBM)
    return _pcall(
        body, name="all_gather",
        in_specs=[hbm] * n, out_specs=[hbm] * n,
        out_shape=[jax.ShapeDtypeStruct((N_DEV,) + s.shape, s.dtype) for s in shards],
        scratch_shapes=[pltpu.SemaphoreType.DMA((n, N_DEV - 1)), pltpu.SemaphoreType.DMA((n, N_DEV - 1)),
                        pltpu.SemaphoreType.DMA((n,))],
    )(*shards)


def _exchange(sliced, whole):
    ns, nw = len(sliced), len(whole)
    n = ns + nw

    def body(*refs):
        srcs, outs = refs[:n], refs[n:2 * n]
        send_sems, recv_sems, local_sems = refs[2 * n:]
        me = _my_id()

        def src_of(a, dest):
            return srcs[a].at[dest] if a < ns else srcs[a]

        locals_ = [pltpu.make_async_copy(src_of(a, me), outs[a].at[me], local_sems.at[a]) for a in range(n)]
        for cp in locals_:
            cp.start()
        sends = []
        for k in range(1, N_DEV):
            peer, pid = _peer(k)
            for a in range(n):
                cp = pltpu.make_async_remote_copy(
                    src_ref=src_of(a, pid), dst_ref=outs[a].at[me],
                    send_sem=send_sems.at[a, k - 1], recv_sem=recv_sems.at[a, k - 1],
                    device_id=peer, device_id_type=pl.DeviceIdType.MESH)
                cp.start()
                sends.append(cp)
        for k in range(1, N_DEV):
            peer, pid = _peer(k)
            for a in range(n):
                pltpu.make_async_remote_copy(
                    src_ref=src_of(a, pid), dst_ref=outs[a].at[pid],
                    send_sem=send_sems.at[a, k - 1], recv_sem=recv_sems.at[a, k - 1],
                    device_id=peer, device_id_type=pl.DeviceIdType.MESH).wait_recv()
        for cp in sends:
            cp.wait_send()
        for cp in locals_:
            cp.wait()

    hbm = pl.BlockSpec(memory_space=pltpu.HBM)
    shapes = [jax.ShapeDtypeStruct(s.shape, s.dtype) for s in sliced]
    shapes += [jax.ShapeDtypeStruct((N_DEV,) + s.shape, s.dtype) for s in whole]
    return _pcall(
        body, name="grad_exchange",
        in_specs=[hbm] * n, out_specs=[hbm] * n, out_shape=shapes,
        scratch_shapes=[pltpu.SemaphoreType.DMA((n, N_DEV - 1)), pltpu.SemaphoreType.DMA((n, N_DEV - 1)),
                        pltpu.SemaphoreType.DMA((n,))],
    )(*sliced, *whole)


def _permute_cols(w):
    pad = jnp.zeros(w.shape[:-1] + (NZ - OFF_F - NH,), w.dtype)
    return jnp.concatenate([
        w[..., 3844:7940],
        w[..., 0:1024],
        w[..., 2052:3076],
        w[..., 3076:3844],
        w[..., 1024:2048],
        w[..., 2048:2052], pad], axis=-1)


def _unpermute_cols(g):
    return jnp.concatenate([
        g[..., OFF_CONV:OFF_CONV + 1024],
        g[..., OFF_ATT:OFF_ATT + 1024],
        g[..., OFF_F:OFF_F + NH],
        g[..., OFF_HGRN:OFF_HGRN + 1024],
        g[..., OFF_SGU:OFF_SGU + 768],
        g[..., 0:4096]], axis=-1)


_SMALL = (
    ("norm_mix", (DEPTH, D)), ("conv_w", (DEPTH, CONV_WIDTH, W)), ("conv_b", (DEPTH, W)),
    ("fgate_bias", (DEPTH, NH)), ("q_norm", (DEPTH, DH)), ("k_norm", (DEPTH, DH)),
    ("lb_logits", (DEPTH, W)), ("hgrn_norm", (DEPTH, W)), ("sgu_norm", (DEPTH, W)),
    ("spatial_w", (DEPTH, NH, SGU_CHUNK, SGU_CHUNK)), ("spatial_b", (DEPTH, NH, SGU_CHUNK)),
    ("merge_b", (DEPTH, NBR, D)), ("norm_ple", (DEPTH, D)),
)


def _small_rows(shape):
    size = 1
    for s in shape:
        size *= s
    rows = -(-size // 128)
    return size, -(-rows // 8) * 8


def _pack_small(parts):
    out = []
    for name, shape in _SMALL:
        size, rows = _small_rows(shape)
        flat = parts[name].astype(f32).reshape(-1)
        flat = jnp.pad(flat, (0, rows * 128 - size))
        out.append(flat.reshape(rows, 128))
    return jnp.concatenate(out, axis=0)


def _unpack_small(buf):
    parts, r0 = {}, 0
    for name, shape in _SMALL:
        size, rows = _small_rows(shape)
        parts[name] = buf[r0:r0 + rows].reshape(-1)[:size].reshape(shape)
        r0 += rows
    return parts


def _shard_cols(a, width):
    return lax.dynamic_slice_in_dim(a, _my_id() * width, width, axis=a.ndim - 1)


def kernel(x, p, norm_mix, w_in, conv_w, conv_b, fgate_bias, q_norm, k_norm, lb_logits, hgrn_norm, sgu_norm, spatial_w, spatial_b, w_up, merge_b, w_o, norm_ple, w_ple_gate, w_ple_proj, loss_target, m_norm_mix, m_w_in, m_conv_w, m_conv_b, m_fgate_bias, m_q_norm, m_k_norm, m_lb_logits, m_hgrn_norm, m_sgu_norm, m_spatial_w, m_spatial_b, m_w_up, m_merge_b, m_w_o, m_norm_ple, m_w_ple_gate, m_w_ple_proj, v_norm_mix, v_w_in, v_conv_w, v_conv_b, v_fgate_bias, v_q_norm, v_k_norm, v_lb_logits, v_hgrn_norm, v_sgu_norm, v_spatial_w, v_spatial_b, v_w_up, v_merge_b, v_w_o, v_norm_ple, v_w_ple_gate, v_w_ple_proj):
    T = x.shape[1]
    SH = D // N_DEV
    CW = W // N_DEV
    tm = 512 if T % 512 == 0 else T
    tmm = 256 if T % 256 == 0 else T
    x0 = x.reshape(T, D)
    target = loss_target.reshape(T, D)

    small_shard = jnp.concatenate([
        merge_b.reshape(DEPTH * NBR, SH),
        jnp.pad(conv_w.reshape(DEPTH * CONV_WIDTH, CW), ((0, 16 - DEPTH * CONV_WIDTH), (0, SH - CW)))], axis=0)
    g_win, g_wup, g_wo, g_wpg, g_wpp, g_small = _all_gather([
        _permute_cols(w_in).astype(bf16).reshape(DEPTH * SH, NZ),
        w_up.astype(bf16).reshape(DEPTH * NBR * W, SH),
        w_o.astype(bf16).reshape(DEPTH * SH, D),
        w_ple_gate.astype(bf16).reshape(DEPTH * SH, D),
        w_ple_proj.astype(bf16).reshape(DEPTH * PLE, SH),
        small_shard])
    win_f = g_win.reshape(N_DEV, DEPTH, SH, NZ).transpose(1, 0, 2, 3).reshape(DEPTH, D, NZ)
    wup_f = g_wup.reshape(N_DEV, DEPTH, NBR, W, SH).transpose(1, 2, 3, 0, 4).reshape(DEPTH, NBR, W, D)
    wo_f = g_wo.reshape(N_DEV, DEPTH, SH, D).transpose(1, 0, 2, 3).reshape(DEPTH, D, D)
    wpg_f = g_wpg.reshape(N_DEV, DEPTH, SH, D).transpose(1, 0, 2, 3).reshape(DEPTH, D, D)
    wpp_f = g_wpp.reshape(N_DEV, DEPTH, PLE, SH).transpose(1, 2, 0, 3).reshape(DEPTH, PLE, D)
    mb_f = g_small[:, 0:DEPTH * NBR].reshape(N_DEV, DEPTH, NBR, SH).transpose(1, 2, 0, 3).reshape(DEPTH, NBR, D)
    cw_f = g_small[:, 16:16 + DEPTH * CONV_WIDTH, 0:CW].reshape(N_DEV, DEPTH, CONV_WIDTH, CW)
    cw_f = cw_f.transpose(1, 2, 0, 3).reshape(DEPTH, CONV_WIDTH, W)

    loss_local, dx, gw, gs_full = _forward_backward(
        x0, p[:, 0], target, win_f, wup_f, wo_f, wpg_f, wpp_f, mb_f, cw_f, norm_mix, conv_b, fgate_bias, q_norm,
        k_norm, lb_logits, hgrn_norm, sgu_norm, spatial_w, spatial_b, norm_ple)
    loss = lax.psum(loss_local[0, 0], AXES)
    grad_x = dx.reshape(1, T, D)

    weights = dict(norm_mix=norm_mix, w_in=w_in, conv_w=conv_w, conv_b=conv_b, fgate_bias=fgate_bias, q_norm=q_norm,
                   k_norm=k_norm, lb_logits=lb_logits, hgrn_norm=hgrn_norm, sgu_norm=sgu_norm, spatial_w=spatial_w,
                   spatial_b=spatial_b, w_up=w_up, merge_b=merge_b, w_o=w_o, norm_ple=norm_ple,
                   w_ple_gate=w_ple_gate, w_ple_proj=w_ple_proj)
    ms = dict(norm_mix=m_norm_mix, w_in=m_w_in, conv_w=m_conv_w, conv_b=m_conv_b, fgate_bias=m_fgate_bias,
              q_norm=m_q_norm, k_norm=m_k_norm, lb_logits=m_lb_logits, hgrn_norm=m_hgrn_norm, sgu_norm=m_sgu_norm,
              spatial_w=m_spatial_w, spatial_b=m_spatial_b, w_up=m_w_up, merge_b=m_merge_b, w_o=m_w_o,
              norm_ple=m_norm_ple, w_ple_gate=m_w_ple_gate, w_ple_proj=m_w_ple_proj)
    vs = dict(norm_mix=v_norm_mix, w_in=v_w_in, conv_w=v_conv_w, conv_b=v_conv_b, fgate_bias=v_fgate_bias,
              q_norm=v_q_norm, k_norm=v_k_norm, lb_logits=v_lb_logits, hgrn_norm=v_hgrn_norm, sgu_norm=v_sgu_norm,
              spatial_w=v_spatial_w, spatial_b=v_spatial_b, w_up=v_w_up, merge_b=v_merge_b, w_o=v_w_o,
              norm_ple=v_norm_ple, w_ple_gate=v_w_ple_gate, w_ple_proj=v_w_ple_proj)
    return _exchange_and_update(loss, grad_x, gw, gs_full, weights, ms, vs)


def _forward_backward(x0, p, target, win_f, wup_f, wo_f, wpg_f, wpp_f, mb_f, cw_f, norm_mix, conv_b, fgate_bias,
                      q_norm, k_norm, lb_logits, hgrn_norm, sgu_norm, spatial_w, spatial_b, norm_ple):
    T = x0.shape[0]
    tm = 512 if T % 512 == 0 else T
    tmm = 256 if T % 256 == 0 else T
    lower = _lower_bounds(lb_logits)
    fb_pad = jnp.pad(fgate_bias, ((0, 0), (0, 128 - NH)))
    gq_t = jnp.tile(q_norm, (1, NH))
    gk_t = jnp.tile(k_norm, (1, NH))
    sbe = jnp.repeat(jnp.swapaxes(spatial_b, 1, 2), DH, axis=2)

    saved = []
    xc = x0
    p = p[:, None]
    for li in range(DEPTH):
        row = lambda a: a[li:li + 1]
        z, h = _inproj_fwd(xc, row(norm_mix), win_f[li], tm)
        ya = _conv_fwd(z, cw_f[li], row(conv_b), tm)
        yd = _sgu_fwd(z, row(sgu_norm), spatial_w[li], sbe[li], tm)
        yc, o_pre, states = _hgrn_fwd(z, lower[li:li + 1], row(hgrn_norm), tmm)
        qh, kh, vh, cum = _attn_prep(z, row(fb_pad), row(gq_t), row(gk_t), tm)
        cum4 = jnp.transpose(cum[:, 0:NH])
        cq, ck = cum4[:, :, None], cum4[:, None, :]
        oh, lse = _attn_fwd(qh, kh, vh, cq, ck, tm)
        yb, mg, x1, x2 = _merge_fwd(ya, oh, z, yc, yd, mb_f[li], xc, p[li, 0], wup_f[li], wo_f[li],
                                    row(norm_ple), wpg_f[li], wpp_f[li], tmm)
        saved.append(dict(x=xc, z=z, h=h, ya=ya, yb=yb, yc=yc, yd=yd, o_pre=o_pre, states=states,
                          qh=qh, kh=kh, vh=vh, cq=cq, ck=ck, oh=oh, lse=lse, mg=mg, x1=x1))
        xc = x2

    loss_local, dx = _loss_head(xc, target, tm)

    gw = {n: [None] * DEPTH for n in ("w_in", "w_up", "w_o", "w_ple_gate", "w_ple_proj")}
    gs = {n: [None] * DEPTH for n, _ in _SMALL}
    dlower = [None] * DEPTH
    for li in reversed(range(DEPTH)):
        s = saved[li]
        row = lambda a: a[li:li + 1]
        dx1, gw["w_ple_gate"][li], gw["w_ple_proj"][li], ggp = _ple_bwd(
            dx, s["x1"], p[li, 0], row(norm_ple), wpg_f[li], wpp_f[li], tmm)
        gs["norm_ple"][li] = ggp[0]
        dml, dya, dyb, dyc, dyd, gw["w_o"][li], gw["w_up"][li], gs["merge_b"][li] = _merge_bwd(
            dx1, s["mg"], s["ya"], s["yb"], s["yc"], s["yd"], s["z"], mb_f[li], wup_f[li], wo_f[li], tmm)
        dz_conv, gcw, gcb = _conv_bwd(s["z"], dya, cw_f[li], row(conv_b), tm)
        gs["conv_w"][li], gs["conv_b"][li] = gcw[0:CONV_WIDTH], gcb[0]
        dz_sgu, gs["spatial_w"][li], gsb, ggv = _sgu_bwd(s["z"], dyd, row(sgu_norm), spatial_w[li], sbe[li], tm)
        gs["spatial_b"][li] = jnp.transpose(gsb[:, ::DH])
        gs["sgu_norm"][li] = ggv[0]
        dz_hgrn, ggn, glb = _hgrn_bwd(s["z"], lower[li:li + 1], row(hgrn_norm), s["o_pre"], s["states"], dyc, tmm)
        gs["hgrn_norm"][li], dlower[li] = ggn[0], glb[0]
        doh, delta = _attn_bwd_prep(dyb, s["oh"], s["z"], tm)
        dqh, dkh, dvh, dck, dcq = _attn_bwd(s["qh"], s["kh"], s["vh"], s["cq"], s["ck"], doh, s["lse"], delta, tm)
        dck_t = jnp.pad(jnp.transpose(dck.reshape(NH, T)), ((0, 0), (0, 128 - NH)))
        dz_att, ggq, ggk, gfb = _attn_post(s["z"], dyb, s["oh"], dqh, dkh, dvh, dck_t, dcq, row(fb_pad),
                                           row(gq_t), row(gk_t), tm)
        gs["q_norm"][li], gs["k_norm"][li], gs["fgate_bias"][li] = ggq[0, 0:DH], ggk[0, 0:DH], gfb[0, 0:NH]
        dz = jnp.concatenate([dml, dz_conv, dz_hgrn, dz_sgu, dz_att], axis=1)
        dx, gnm = _inproj_bwd_x(dz, win_f[li], s["x"], dx1, row(norm_mix), tm)
        gs["norm_mix"][li] = gnm[0]
        gw["w_in"][li] = _inproj_bwd_w(s["h"], dz, tm)
    dlower8 = jnp.pad(jnp.stack(dlower), ((0, 8 - DEPTH), (0, 0)))
    gs_full = {n: jnp.stack(v) for n, v in gs.items() if n != "lb_logits"}
    gs_full["lb_logits"] = _lower_bounds_bwd(lb_logits, dlower8)[0:DEPTH]
    return loss_local, dx, gw, gs_full


def _exchange_and_update(loss, grad_x, gw, gs_full, weights, ms, vs):
    SH = D // N_DEV
    CW = W // N_DEV

    def by_rows(g):
        C = g.shape[-1]
        return g.reshape(DEPTH, N_DEV, SH, C).transpose(1, 0, 2, 3).reshape(N_DEV, DEPTH * SH, C)

    def by_cols(g):
        lead = g.shape[:-1]
        r = 1
        for s_ in lead:
            r *= s_
        return g.reshape(r, N_DEV, SH).transpose(1, 0, 2)

    small_buf = _pack_small(gs_full)
    l_win, l_wup, l_wo, l_wpg, l_wpp, l_small = _exchange(
        [by_rows(jnp.stack(gw["w_in"])), by_cols(jnp.stack(gw["w_up"])), by_rows(jnp.stack(gw["w_o"])),
         by_rows(jnp.stack(gw["w_ple_gate"])), by_cols(jnp.stack(gw["w_ple_proj"]))],
        [small_buf])

    g_w_in = _unpermute_cols(_sum_slabs(l_win)).reshape(DEPTH, SH, IN_COLS)
    g_w_up = _sum_slabs(l_wup).reshape(DEPTH, NBR, W, SH)
    g_w_o = _sum_slabs(l_wo).reshape(DEPTH, SH, D)
    g_w_pg = _sum_slabs(l_wpg).reshape(DEPTH, SH, D)
    g_w_pp = _sum_slabs(l_wpp).reshape(DEPTH, PLE, SH)
    g_small = _unpack_small(_sum_slabs(l_small))
    g_small_local = dict(g_small)
    g_small_local["conv_w"] = _shard_cols(g_small["conv_w"], CW)
    g_small_local["merge_b"] = _shard_cols(g_small["merge_b"], SH)

    grads = dict(w_in=g_w_in, w_up=g_w_up, w_o=g_w_o, w_ple_gate=g_w_pg, w_ple_proj=g_w_pp)
    deltas, new_m, new_v = {}, {}, {}
    for name, cols in (("w_in", IN_COLS), ("w_up", SH), ("w_o", D), ("w_ple_gate", D), ("w_ple_proj", SH)):
        shape = weights[name].shape
        d_, m_, v_ = _adamw(weights[name].reshape(-1, cols), grads[name].reshape(-1, cols),
                            ms[name].reshape(-1, cols), vs[name].reshape(-1, cols))
        deltas[name], new_m[name], new_v[name] = d_.reshape(shape), m_.reshape(shape), v_.reshape(shape)

    def local_shapes(parts):
        return {n: (parts[n] if parts[n].shape == s else jnp.pad(
            parts[n], [(0, 0)] * (len(s) - 1) + [(0, s[-1] - parts[n].shape[-1])])) for n, s in _SMALL}

    d_, m_, v_ = _adamw(_pack_small(local_shapes(weights)), _pack_small(local_shapes(g_small_local)),
                        _pack_small(local_shapes(ms)), _pack_small(local_shapes(vs)))
    for buf, dst in ((d_, deltas), (m_, new_m), (v_, new_v)):
        parts = _unpack_small(buf)
        for n, _ in _SMALL:
            dst[n] = parts[n][..., :weights[n].shape[-1]]
    for n, _ in _SMALL:
        grads[n] = g_small_local[n]

    order = ["norm_mix", "w_in", "conv_w", "conv_b", "fgate_bias", "q_norm", "k_norm", "lb_logits", "hgrn_norm",
             "sgu_norm", "spatial_w", "spatial_b", "w_up", "merge_b", "w_o", "norm_ple", "w_ple_gate", "w_ple_proj"]
    return (loss, grad_x, *[grads[n] for n in order], *[deltas[n] for n in order],
            *[new_m[n] for n in order], *[new_v[n] for n in order])
```

```python
import functools

import jax
import jax.numpy as jnp
from jax import lax
from jax.experimental import pallas as pl
from jax.experimental.pallas import tpu as pltpu

f32 = jnp.float32
bf16 = jnp.bfloat16

D = 1024
W = 256
NH = 4
DH = 64
NBR = 4
PLE = 256
DEPTH = 4
CONV_WIDTH = 3
SGU_CHUNK = 128
GLA_CHUNK = 64
EPS = 1e-6
MASK_VALUE = -1e30
IN_COLS = 7940
NZ = 8064
OFF_CONV = 4096
OFF_HGRN = 5120
OFF_SGU = 6144
OFF_ATT = 6912
OFF_F = 7936
ZT = 1152
NZT = NZ // ZT
EXP_CLAMP = 80.0

ADAM_LR = 0.001
ADAM_B1 = 0.9
ADAM_B2 = 0.999
ADAM_EPS = 1e-08
ADAM_WD = 0.01
ADAM_STEP = 10

N_DEV = 8
AXES = ("x", "y", "c")
VMEM_LIMIT = 56 * 1024 * 1024
HI = lax.Precision.HIGHEST

NT_DIMS = (((1,), (1,)), ((), ()))
TN_DIMS = (((0,), (0,)), ((), ()))


def _pcall(body, **kw):
    return pl.pallas_call(body, **kw)


def _params(*sem):
    return pltpu.CompilerParams(dimension_semantics=sem, vmem_limit_bytes=VMEM_LIMIT)


def _mm(a, b):
    return jnp.dot(a, b, preferred_element_type=f32)


def _mm_nt(a, b):
    return lax.dot_general(a, b, NT_DIMS, preferred_element_type=f32)


def _mm_tn(a, b):
    return lax.dot_general(a, b, TN_DIMS, preferred_element_type=f32)


def _sigmoid(x):
    return 1.0 / (1.0 + jnp.exp(-x))


def _silu(x):
    return x * _sigmoid(x)


def _dsilu(x):
    s = _sigmoid(x)
    return s * (1.0 + x * (1.0 - s))


def _logsigmoid(x):
    return jnp.minimum(x, 0.0) - jnp.log(1.0 + jnp.exp(-jnp.abs(x)))


def _iota2(shape, axis):
    return lax.broadcasted_iota(jnp.int32, shape, axis)


def _group_mean_matrix(n, group):
    shift = group.bit_length() - 1
    r = lax.shift_right_logical(_iota2((n, n), 0), shift)
    c = lax.shift_right_logical(_iota2((n, n), 1), shift)
    return jnp.where(r == c, 1.0 / group, 0.0).astype(f32)


def _group_mean(x, gm):
    return jnp.dot(x, gm, precision=HI, preferred_element_type=f32)


def _lower_tri(n):
    return jnp.where(_iota2((n, n), 0) >= _iota2((n, n), 1), 1.0, 0.0).astype(f32)


def _upper_tri(n):
    return jnp.where(_iota2((n, n), 0) <= _iota2((n, n), 1), 1.0, 0.0).astype(f32)


def _rows3(r0, r1, r2, width):
    row = _iota2((8, width), 0)
    return jnp.where(row == 0, r0, jnp.where(row == 1, r1, jnp.where(row == 2, r2, 0.0)))


def _inproj_fwd(x, g, w, tm):
    T = x.shape[0]

    def body(x_ref, g_ref, w_ref, z_ref, h_ref):
        @pl.when(pl.program_id(1) == 0)
        def _():
            xv = x_ref[...]
            r = lax.rsqrt(jnp.mean(xv * xv, axis=-1, keepdims=True) + EPS)
            h_ref[...] = (xv * r * g_ref[...]).astype(bf16)

        z_ref[...] = _mm(h_ref[...], w_ref[...])

    return _pcall(
        body, name="inproj_fwd", grid=(T // tm, NZT),
        in_specs=[pl.BlockSpec((tm, D), lambda i, j: (i, 0)),
                  pl.BlockSpec((1, D), lambda i, j: (0, 0)),
                  pl.BlockSpec((D, ZT), lambda i, j: (0, j))],
        out_specs=[pl.BlockSpec((tm, ZT), lambda i, j: (i, j)),
                   pl.BlockSpec((tm, D), lambda i, j: (i, 0))],
        out_shape=[jax.ShapeDtypeStruct((T, NZ), f32), jax.ShapeDtypeStruct((T, D), bf16)],
        compiler_params=_params("parallel", "arbitrary"),
    )(x, g, w)


def _inproj_bwd_x(dz, w, x, dx1, g, tm):
    T = x.shape[0]

    def body(dz_ref, w_ref, x_ref, dx1_ref, g_ref, dx_ref, gg_ref, acc):
        i, k = pl.program_id(0), pl.program_id(1)

        @pl.when(k == 0)
        def _():
            acc[...] = jnp.zeros_like(acc)

        @pl.when((i == 0) & (k == 0))
        def _():
            gg_ref[...] = jnp.zeros_like(gg_ref)

        acc[...] += _mm_nt(dz_ref[...], w_ref[...])

        @pl.when(k == NZT - 1)
        def _():
            xv = x_ref[...]
            r = lax.rsqrt(jnp.mean(xv * xv, axis=-1, keepdims=True) + EPS)
            dh = acc[...]
            gg_ref[...] += jnp.sum(dh * xv * r, axis=0, keepdims=True)
            u = dh * g_ref[...]
            dx_ref[...] = dx1_ref[...] + r * u - xv * (r * r * r) * jnp.mean(u * xv, axis=-1, keepdims=True)

    return _pcall(
        body, name="inproj_bwd_x", grid=(T // tm, NZT),
        in_specs=[pl.BlockSpec((tm, ZT), lambda i, k: (i, k)),
                  pl.BlockSpec((D, ZT), lambda i, k: (0, k)),
                  pl.BlockSpec((tm, D), lambda i, k: (i, 0)),
                  pl.BlockSpec((tm, D), lambda i, k: (i, 0)),
                  pl.BlockSpec((1, D), lambda i, k: (0, 0))],
        out_specs=[pl.BlockSpec((tm, D), lambda i, k: (i, 0)),
                   pl.BlockSpec((1, D), lambda i, k: (0, 0))],
        out_shape=[jax.ShapeDtypeStruct((T, D), f32), jax.ShapeDtypeStruct((1, D), f32)],
        scratch_shapes=[pltpu.VMEM((tm, D), f32)],
        compiler_params=_params("arbitrary", "arbitrary"),
    )(dz, w, x, dx1, g)


def _inproj_bwd_w(h, dz, tm):
    T = h.shape[0]

    def body(h_ref, dz_ref, gw_ref):
        @pl.when(pl.program_id(1) == 0)
        def _():
            gw_ref[...] = jnp.zeros_like(gw_ref)

        gw_ref[...] += _mm_tn(h_ref[...], dz_ref[...])

    return _pcall(
        body, name="inproj_bwd_w", grid=(NZT, T // tm),
        in_specs=[pl.BlockSpec((tm, D), lambda j, i: (i, 0)),
                  pl.BlockSpec((tm, ZT), lambda j, i: (i, j))],
        out_specs=pl.BlockSpec((D, ZT), lambda j, i: (0, j)),
        out_shape=jax.ShapeDtypeStruct((D, NZ), f32),
        compiler_params=_params("parallel", "arbitrary"),
    )(h, dz)


def _zblock(tm, col256):
    return pl.BlockSpec((tm, W), lambda i, c=col256: (i, c))


def _conv_taps(zc, halo, cw_ref, n):
    ext = jnp.concatenate([halo, zc], axis=0)
    z1 = pltpu.roll(ext, 1, 0)[8:]
    z2 = pltpu.roll(ext, 2, 0)[8:]
    return z1, z2


def _conv_fwd(z, cw, cb, tm):
    T = z.shape[0]
    c0 = OFF_CONV // W
    hb = tm // 8

    def body(ax_ref, ab_ref, ac_ref, ag_ref, hx_ref, hc_ref, cw_ref, cb_ref, y_ref):
        i = pl.program_id(0)
        zc = ac_ref[...] * ax_ref[...]
        halo = jnp.where(i > 0, hc_ref[...] * hx_ref[...], 0.0)
        z1, z2 = _conv_taps(zc, halo, cw_ref, tm)
        y = cw_ref[2:3, :] * zc + cw_ref[1:2, :] * z1 + cw_ref[0:1, :] * z2
        ya = ab_ref[...] * (y + cb_ref[...])
        y_ref[...] = (ya * _silu(ag_ref[...])).astype(bf16)

    halo_spec = lambda col: pl.BlockSpec((8, W), lambda i, c=col: (jnp.maximum(i * hb - 1, 0), c))
    return _pcall(
        body, name="conv_fwd", grid=(T // tm,),
        in_specs=[_zblock(tm, c0), _zblock(tm, c0 + 1), _zblock(tm, c0 + 2), _zblock(tm, c0 + 3),
                  halo_spec(c0), halo_spec(c0 + 2),
                  pl.BlockSpec((CONV_WIDTH, W), lambda i: (0, 0)),
                  pl.BlockSpec((1, W), lambda i: (0, 0))],
        out_specs=pl.BlockSpec((tm, W), lambda i: (i, 0)),
        out_shape=jax.ShapeDtypeStruct((T, W), bf16),
        compiler_params=_params("parallel"),
    )(z, z, z, z, z, z, cw, cb)


def _conv_bwd(z, dy, cw, cb, tm):
    T = z.shape[0]
    c0 = OFF_CONV // W
    hb = tm // 8
    nt = T // tm

    def body(ax_ref, ab_ref, ac_ref, ag_ref, hx_ref, hc_ref, nb_ref, ng_ref, dy_ref, ndy_ref,
             cw_ref, cb_ref, dz_ref, gcw_ref, gcb_ref):
        i = pl.program_id(0)

        @pl.when(i == 0)
        def _():
            gcw_ref[...] = jnp.zeros_like(gcw_ref)
            gcb_ref[...] = jnp.zeros_like(gcb_ref)

        ax, ab, ac, ag = ax_ref[...], ab_ref[...], ac_ref[...], ag_ref[...]
        w0, w1, w2 = cw_ref[0:1, :], cw_ref[1:2, :], cw_ref[2:3, :]
        zc = ac * ax
        halo = jnp.where(i > 0, hc_ref[...] * hx_ref[...], 0.0)
        z1, z2 = _conv_taps(zc, halo, cw_ref, tm)
        yb = w2 * zc + w1 * z1 + w0 * z2 + cb_ref[...]
        ya = ab * yb
        dyg = dy_ref[...]
        dag = dyg * ya * _dsilu(ag)
        dya = dyg * _silu(ag)
        dab = dya * yb
        dyc = dya * ab
        nxt = jnp.where(i < nt - 1, ndy_ref[...] * _silu(ng_ref[...]) * nb_ref[...], 0.0)
        ext = jnp.concatenate([dyc, nxt], axis=0)
        d1 = pltpu.roll(ext, tm + 8 - 1, 0)[:tm]
        d2 = pltpu.roll(ext, tm + 8 - 2, 0)[:tm]
        dzc = w2 * dyc + w1 * d1 + w0 * d2
        dz_ref[:, 0:W] = (dzc * ac).astype(bf16)
        dz_ref[:, W:2 * W] = dab.astype(bf16)
        dz_ref[:, 2 * W:3 * W] = (dzc * ax).astype(bf16)
        dz_ref[:, 3 * W:4 * W] = dag.astype(bf16)
        gcb_ref[...] += jnp.sum(dyc, axis=0, keepdims=True)
        gcw_ref[...] += _rows3(jnp.sum(dyc * z2, axis=0, keepdims=True),
                               jnp.sum(dyc * z1, axis=0, keepdims=True),
                               jnp.sum(dyc * zc, axis=0, keepdims=True), W)

    prev_spec = lambda col: pl.BlockSpec((8, W), lambda i, c=col: (jnp.maximum(i * hb - 1, 0), c))
    next_z = lambda col: pl.BlockSpec((8, W), lambda i, c=col: (jnp.minimum((i + 1) * hb, T // 8 - 1), c))
    next_dy = pl.BlockSpec((8, W), lambda i: (jnp.minimum((i + 1) * hb, T // 8 - 1), 0))
    return _pcall(
        body, name="conv_bwd", grid=(nt,),
        in_specs=[_zblock(tm, c0), _zblock(tm, c0 + 1), _zblock(tm, c0 + 2), _zblock(tm, c0 + 3),
                  prev_spec(c0), prev_spec(c0 + 2), next_z(c0 + 1), next_z(c0 + 3),
                  pl.BlockSpec((tm, W), lambda i: (i, 0)), next_dy,
                  pl.BlockSpec((CONV_WIDTH, W), lambda i: (0, 0)),
                  pl.BlockSpec((1, W), lambda i: (0, 0))],
        out_specs=[pl.BlockSpec((tm, 4 * W), lambda i: (i, 0)),
                   pl.BlockSpec((8, W), lambda i: (0, 0)),
                   pl.BlockSpec((1, W), lambda i: (0, 0))],
        out_shape=[jax.ShapeDtypeStruct((T, 4 * W), bf16), jax.ShapeDtypeStruct((8, W), f32),
                   jax.ShapeDtypeStruct((1, W), f32)],
        compiler_params=_params("arbitrary"),
    )(z, z, z, z, z, z, z, z, dy, dy, cw, cb)


def _sgu_core(dv_ref, gv_ref, sw_ref, sbe_ref, s_scr, tm):
    v = dv_ref[...]
    gm = _group_mean_matrix(W, DH)
    rv = lax.rsqrt(_group_mean(v * v, gm) + EPS)
    vh = v * rv
    vnb = (vh * gv_ref[...]).astype(bf16)
    causal = _iota2((SGU_CHUNK, SGU_CHUNK), 0) >= _iota2((SGU_CHUNK, SGU_CHUNK), 1)
    wgs = [jnp.where(causal, sw_ref[g], 0.0).astype(bf16) for g in range(NH)]
    for c in range(tm // SGU_CHUNK):
        rows = slice(c * SGU_CHUNK, (c + 1) * SGU_CHUNK)
        for g in range(NH):
            cols = slice(g * DH, (g + 1) * DH)
            s_scr[rows, cols] = _mm(wgs[g], vnb[rows, cols])
    sb = sbe_ref[...]
    s = s_scr[...] + jnp.concatenate([sb] * (tm // SGU_CHUNK), axis=0)
    return v, rv, vh, vnb, wgs, causal, gm, s


def _sgu_fwd(z, gv, sw, sbe, tm):
    T = z.shape[0]
    c0 = OFF_SGU // W

    def body(du_ref, dv_ref, dg_ref, gv_ref, sw_ref, sbe_ref, y_ref, s_scr):
        s = _sgu_core(dv_ref, gv_ref, sw_ref, sbe_ref, s_scr, tm)[-1]
        y_ref[...] = ((du_ref[...] * s) * _silu(dg_ref[...])).astype(bf16)

    return _pcall(
        body, name="sgu_fwd", grid=(T // tm,),
        in_specs=[_zblock(tm, c0), _zblock(tm, c0 + 1), _zblock(tm, c0 + 2),
                  pl.BlockSpec((1, W), lambda i: (0, 0)),
                  pl.BlockSpec((NH, SGU_CHUNK, SGU_CHUNK), lambda i: (0, 0, 0)),
                  pl.BlockSpec((SGU_CHUNK, W), lambda i: (0, 0))],
        out_specs=pl.BlockSpec((tm, W), lambda i: (i, 0)),
        out_shape=jax.ShapeDtypeStruct((T, W), bf16),
        scratch_shapes=[pltpu.VMEM((tm, W), f32)],
        compiler_params=_params("parallel"),
    )(z, z, z, gv, sw, sbe)


def _sgu_bwd(z, dy, gv, sw, sbe, tm):
    T = z.shape[0]
    c0 = OFF_SGU // W
    nt = T // tm

    def body(du_ref, dv_ref, dg_ref, dy_ref, gv_ref, sw_ref, sbe_ref,
             dz_ref, gsw_ref, gsb_ref, ggv_ref, s_scr, dvn_scr, sb_acc):
        i = pl.program_id(0)

        @pl.when(i == 0)
        def _():
            gsw_ref[...] = jnp.zeros_like(gsw_ref)
            ggv_ref[...] = jnp.zeros_like(ggv_ref)
            sb_acc[...] = jnp.zeros_like(sb_acc)

        v, rv, vh, vnb, wgs, causal, gm, s = _sgu_core(dv_ref, gv_ref, sw_ref, sbe_ref, s_scr, tm)
        du, dg, dyv = du_ref[...], dg_ref[...], dy_ref[...]
        ddg = dyv * (du * s) * _dsilu(dg)
        t = dyv * _silu(dg)
        ddu = t * s
        ds = t * du
        dsb = ds.astype(bf16)
        acc = sb_acc[...]
        for c in range(tm // SGU_CHUNK):
            rows = slice(c * SGU_CHUNK, (c + 1) * SGU_CHUNK)
            acc = acc + ds[rows, :]
            for g in range(NH):
                cols = slice(g * DH, (g + 1) * DH)
                gsw_ref[g] += jnp.where(causal, _mm_nt(dsb[rows, cols], vnb[rows, cols]), 0.0)
                dvn_scr[rows, cols] = _mm_tn(wgs[g], dsb[rows, cols])
        sb_acc[...] = acc
        dvn = dvn_scr[...]
        ggv_ref[...] += jnp.sum(dvn * vh, axis=0, keepdims=True)
        u = dvn * gv_ref[...]
        ddv = rv * u - v * (rv * rv * rv) * _group_mean(u * v, gm)
        dz_ref[:, 0:W] = ddu.astype(bf16)
        dz_ref[:, W:2 * W] = ddv.astype(bf16)
        dz_ref[:, 2 * W:3 * W] = ddg.astype(bf16)

        @pl.when(i == nt - 1)
        def _():
            gsb_ref[...] = _group_mean(sb_acc[...], gm) * float(DH)

    return _pcall(
        body, name="sgu_bwd", grid=(nt,),
        in_specs=[_zblock(tm, c0), _zblock(tm, c0 + 1), _zblock(tm, c0 + 2),
                  pl.BlockSpec((tm, W), lambda i: (i, 0)),
                  pl.BlockSpec((1, W), lambda i: (0, 0)),
                  pl.BlockSpec((NH, SGU_CHUNK, SGU_CHUNK), lambda i: (0, 0, 0)),
                  pl.BlockSpec((SGU_CHUNK, W), lambda i: (0, 0))],
        out_specs=[pl.BlockSpec((tm, 3 * W), lambda i: (i, 0)),
                   pl.BlockSpec((NH, SGU_CHUNK, SGU_CHUNK), lambda i: (0, 0, 0)),
                   pl.BlockSpec((SGU_CHUNK, W), lambda i: (0, 0)),
                   pl.BlockSpec((1, W), lambda i: (0, 0))],
        out_shape=[jax.ShapeDtypeStruct((T, 3 * W), bf16),
                   jax.ShapeDtypeStruct((NH, SGU_CHUNK, SGU_CHUNK), f32),
                   jax.ShapeDtypeStruct((SGU_CHUNK, W), f32),
                   jax.ShapeDtypeStruct((1, W), f32)],
        scratch_shapes=[pltpu.VMEM((tm, W), f32), pltpu.VMEM((tm, W), f32), pltpu.VMEM((SGU_CHUNK, W), f32)],
        compiler_params=_params("arbitrary"),
    )(z, z, z, dy, gv, sw, sbe)


def _hgrn_gates(cq_ref, cf_ref, lb_ref):
    q = _silu(cq_ref[...])
    sig = _sigmoid(cf_ref[...])
    lb = lb_ref[...]
    g = lb + (1.0 - lb) * sig
    return q, sig, g, jnp.log(g), (1.0 - lb) * (1.0 - sig)


def _hgrn_chunk_terms(lgc, qc, kc):
    C = GLA_CHUNK
    b = jnp.dot(_lower_tri(C), lgc, precision=HI, preferred_element_type=f32)
    bl = jnp.sum(lgc, axis=0, keepdims=True)
    mid = jnp.sum(jnp.where(_iota2((C, W), 0) <= C // 2, lgc, 0.0), axis=0, keepdims=True)
    eb = jnp.exp(b)
    em = jnp.exp(jnp.minimum(b - mid, EXP_CLAMP))
    emi = jnp.exp(jnp.minimum(mid - b, EXP_CLAMP))
    ek = jnp.exp(bl - b)
    return dict(eb=eb, em=em, emi=emi, ek=ek, ebl=jnp.exp(bl),
                qe=qc * eb, qm=qc * em, km=kc * emi, kd=kc * ek)


def _hgrn_fwd(z, lb, gain, tm):
    T = z.shape[0]
    c0 = OFF_HGRN // W
    C = GLA_CHUNK
    ncp = tm // C

    def body(cq_ref, cf_ref, ci_ref, cg_ref, lb_ref, gn_ref, y_ref, o_ref, st_ref, state, o_scr):
        @pl.when(pl.program_id(0) == 0)
        def _():
            state[...] = jnp.zeros_like(state)

        q, sig, g, lg, kf = _hgrn_gates(cq_ref, cf_ref, lb_ref)
        v = ci_ref[...]
        causal = _iota2((C, C), 0) >= _iota2((C, C), 1)
        for c in range(ncp):
            rows = slice(c * C, (c + 1) * C)
            tr = _hgrn_chunk_terms(lg[rows], q[rows], kf[rows])
            vb = v[rows].astype(bf16)
            qmb, kmb, qeb, kdb = (tr[n].astype(bf16) for n in ("qm", "km", "qe", "kd"))
            for h in range(NH):
                cols = slice(h * DH, (h + 1) * DH)
                hr = slice(h * DH, (h + 1) * DH)
                st = state[hr, :]
                st_ref[c, hr, :] = st
                p = jnp.where(causal, _mm_nt(qmb[:, cols], kmb[:, cols]), 0.0)
                o_scr[rows, cols] = _mm(p.astype(bf16), vb[:, cols]) + _mm_nt(qeb[:, cols], st.astype(bf16))
                state[hr, :] = st * tr["ebl"][:, cols] + _mm_tn(vb[:, cols], kdb[:, cols])
        o = o_scr[...]
        o_ref[...] = o
        gm = _group_mean_matrix(W, DH)
        r = lax.rsqrt(_group_mean(o * o, gm) + EPS)
        y_ref[...] = ((o * r * gn_ref[...]) * _silu(cg_ref[...])).astype(bf16)

    return _pcall(
        body, name="hgrn_fwd", grid=(T // tm,),
        in_specs=[_zblock(tm, c0), _zblock(tm, c0 + 1), _zblock(tm, c0 + 2), _zblock(tm, c0 + 3),
                  pl.BlockSpec((1, W), lambda i: (0, 0)), pl.BlockSpec((1, W), lambda i: (0, 0))],
        out_specs=[pl.BlockSpec((tm, W), lambda i: (i, 0)),
                   pl.BlockSpec((tm, W), lambda i: (i, 0)),
                   pl.BlockSpec((ncp, W, DH), lambda i: (i, 0, 0))],
        out_shape=[jax.ShapeDtypeStruct((T, W), bf16), jax.ShapeDtypeStruct((T, W), f32),
                   jax.ShapeDtypeStruct((T // C, W, DH), f32)],
        scratch_shapes=[pltpu.VMEM((W, DH), f32), pltpu.VMEM((tm, W), f32)],
        compiler_params=_params("arbitrary"),
    )(z, z, z, z, lb, gain)


def _hgrn_bwd(z, lb, gain, o_pre, states, dy, tm):
    T = z.shape[0]
    c0 = OFF_HGRN // W
    C = GLA_CHUNK
    ncp = tm // C
    nt = T // tm

    def body(cq_ref, cf_ref, ci_ref, cg_ref, lb_ref, gn_ref, o_ref, st_ref, dy_ref,
             dz_ref, ggn_ref, glb_ref, dstate, dq_s, dk_s, dv_s, db_s):
        @pl.when(pl.program_id(0) == 0)
        def _():
            dstate[...] = jnp.zeros_like(dstate)
            ggn_ref[...] = jnp.zeros_like(ggn_ref)
            glb_ref[...] = jnp.zeros_like(glb_ref)

        cq, cg = cq_ref[...], cg_ref[...]
        q, sig, g, lg, kf = _hgrn_gates(cq_ref, cf_ref, lb_ref)
        lb = lb_ref[...]
        v = ci_ref[...]
        o = o_ref[...]
        gm = _group_mean_matrix(W, DH)
        r = lax.rsqrt(_group_mean(o * o, gm) + EPS)
        oh = o * r
        gn = gn_ref[...]
        dyv = dy_ref[...]
        dcg = dyv * (oh * gn) * _dsilu(cg)
        don = dyv * _silu(cg)
        ggn_ref[...] += jnp.sum(don * oh, axis=0, keepdims=True)
        u = don * gn
        do = r * u - o * (r * r * r) * _group_mean(u * o, gm)

        causal = _iota2((C, C), 0) >= _iota2((C, C), 1)
        last_row = _iota2((C, DH), 0) == C - 1
        for c in reversed(range(ncp)):
            rows = slice(c * C, (c + 1) * C)
            tr = _hgrn_chunk_terms(lg[rows], q[rows], kf[rows])
            vb = v[rows].astype(bf16)
            dob = do[rows].astype(bf16)
            qmb, kmb, qeb, kdb = (tr[n].astype(bf16) for n in ("qm", "km", "qe", "kd"))
            for h in range(NH):
                cols = slice(h * DH, (h + 1) * DH)
                hr = slice(h * DH, (h + 1) * DH)
                st0 = st_ref[c, hr, :]
                dst = dstate[hr, :]
                dstb = dst.astype(bf16)
                doh = dob[:, cols]
                p = jnp.where(causal, _mm_nt(qmb[:, cols], kmb[:, cols]), 0.0)
                dp = jnp.where(causal, _mm_nt(doh, vb[:, cols]), 0.0)
                dpb = dp.astype(bf16)
                dvh = _mm_tn(p.astype(bf16), doh) + _mm_nt(kdb[:, cols], dstb)
                dqm = _mm(dpb, kmb[:, cols])
                dkm = _mm_tn(dpb, qmb[:, cols])
                dqe = _mm(doh, st0.astype(bf16))
                dkd = _mm(vb[:, cols], dstb)
                ebl = tr["ebl"][:, cols]
                dstate[hr, :] = dst * ebl + _mm_tn(doh, qeb[:, cols])
                qm, km, qe, kd = (a[:, cols].astype(f32) for a in (qmb, kmb, qeb, kdb))
                kterm = dkd * kd
                dbh = dqm * qm - dkm * km + dqe * qe - kterm
                extra = jnp.sum(kterm, axis=0, keepdims=True) + ebl * jnp.sum(dst * st0, axis=0, keepdims=True)
                dbh = dbh + jnp.where(last_row, extra, 0.0)
                dq_s[rows, cols] = dqm * tr["em"][:, cols] + dqe * tr["eb"][:, cols]
                dk_s[rows, cols] = dkm * tr["emi"][:, cols] + dkd * tr["ek"][:, cols]
                dv_s[rows, cols] = dvh
                db_s[rows, cols] = dbh
            db_s[rows, :] = jnp.dot(_upper_tri(C), db_s[rows, :], precision=HI, preferred_element_type=f32)
        dlg = db_s[...]
        dk = dk_s[...]
        dsig = sig * (1.0 - sig)
        one_lb = 1.0 - lb
        dcf = (dlg / g - dk) * one_lb * dsig
        glb_ref[...] += jnp.sum((dlg / g - dk) * (1.0 - sig), axis=0, keepdims=True)
        dz_ref[:, 0:W] = (dq_s[...] * _dsilu(cq)).astype(bf16)
        dz_ref[:, W:2 * W] = dcf.astype(bf16)
        dz_ref[:, 2 * W:3 * W] = dv_s[...].astype(bf16)
        dz_ref[:, 3 * W:4 * W] = dcg.astype(bf16)

    rev = lambda i: nt - 1 - i
    zb = lambda col: pl.BlockSpec((tm, W), lambda i, c=col: (rev(i), c))
    return _pcall(
        body, name="hgrn_bwd", grid=(nt,),
        in_specs=[zb(c0), zb(c0 + 1), zb(c0 + 2), zb(c0 + 3),
                  pl.BlockSpec((1, W), lambda i: (0, 0)), pl.BlockSpec((1, W), lambda i: (0, 0)),
                  pl.BlockSpec((tm, W), lambda i: (rev(i), 0)),
                  pl.BlockSpec((ncp, W, DH), lambda i: (rev(i), 0, 0)),
                  pl.BlockSpec((tm, W), lambda i: (rev(i), 0))],
        out_specs=[pl.BlockSpec((tm, 4 * W), lambda i: (rev(i), 0)),
                   pl.BlockSpec((1, W), lambda i: (0, 0)),
                   pl.BlockSpec((1, W), lambda i: (0, 0))],
        out_shape=[jax.ShapeDtypeStruct((T, 4 * W), bf16), jax.ShapeDtypeStruct((1, W), f32),
                   jax.ShapeDtypeStruct((1, W), f32)],
        scratch_shapes=[pltpu.VMEM((W, DH), f32)] + [pltpu.VMEM((tm, W), f32)] * 4,
        compiler_params=_params("arbitrary"),
    )(z, z, z, z, lb, gain, o_pre, states, dy)


def _attn_prep(z, fbias, gq, gk, tm):
    T = z.shape[0]
    c0 = OFF_ATT // W

    def body(q_ref, k_ref, v_ref, f_ref, fb_ref, gq_ref, gk_ref, qh_ref, kh_ref, vh_ref, cum_ref, carry):
        @pl.when(pl.program_id(0) == 0)
        def _():
            carry[...] = jnp.zeros_like(carry)

        gm = _group_mean_matrix(W, DH)
        q, k, v = q_ref[...], k_ref[...], v_ref[...]
        qs = q * lax.rsqrt(_group_mean(q * q, gm) + EPS) * (gq_ref[...] * (DH ** -0.5))
        kn = k * lax.rsqrt(_group_mean(k * k, gm) + EPS) * gk_ref[...]
        for h in range(NH):
            cols = slice(h * DH, (h + 1) * DH)
            qh_ref[h] = qs[:, cols].astype(bf16)
            kh_ref[h] = kn[:, cols].astype(bf16)
            vh_ref[h] = v[:, cols].astype(bf16)
        ls = _logsigmoid(f_ref[...] + fb_ref[...])
        cum_ref[...] = jnp.dot(_lower_tri(tm), ls, precision=HI, preferred_element_type=f32) + carry[...]
        carry[...] += jnp.sum(ls, axis=0, keepdims=True)

    hspec = pl.BlockSpec((NH, tm, DH), lambda i: (0, i, 0))
    return _pcall(
        body, name="attn_prep", grid=(T // tm,),
        in_specs=[_zblock(tm, c0), _zblock(tm, c0 + 1), _zblock(tm, c0 + 2),
                  pl.BlockSpec((tm, 128), lambda i: (i, OFF_F // 128)),
                  pl.BlockSpec((1, 128), lambda i: (0, 0)),
                  pl.BlockSpec((1, W), lambda i: (0, 0)), pl.BlockSpec((1, W), lambda i: (0, 0))],
        out_specs=[hspec, hspec, hspec, pl.BlockSpec((tm, 128), lambda i: (i, 0))],
        out_shape=[jax.ShapeDtypeStruct((NH, T, DH), bf16)] * 3 + [jax.ShapeDtypeStruct((T, 128), f32)],
        scratch_shapes=[pltpu.VMEM((1, 128), f32)],
        compiler_params=_params("arbitrary"),
    )(z, z, z, z, fbias, gq, gk)


HP = 2


def _causal_pairs(nq, key_major):
    if key_major:
        pairs = [(qi, ki) for ki in range(nq) for qi in range(ki, nq)]
    else:
        pairs = [(qi, ki) for qi in range(nq) for ki in range(qi + 1)]
    return (jnp.asarray([p[0] for p in pairs], jnp.int32), jnp.asarray([p[1] for p in pairs], jnp.int32))


def _attn_fwd(qh, kh, vh, cq, ck, bq):
    T = qh.shape[1]
    nq = T // bq
    bk = bq
    qs, ks = _causal_pairs(nq, key_major=False)

    def body(qs_ref, ks_ref, q_ref, k_ref, v_ref, cq_ref, ck_ref, o_ref, lse_ref, m_s, l_s, acc_s):
        i = pl.program_id(1)
        qi, ki = qs_ref[i], ks_ref[i]

        @pl.when(ki == 0)
        def _():
            m_s[...] = jnp.full_like(m_s, MASK_VALUE)
            l_s[...] = jnp.zeros_like(l_s)
            acc_s[...] = jnp.zeros_like(acc_s)

        def step(diagonal):
            for h in range(HP):
                s = _mm_nt(q_ref[h], k_ref[h]) + cq_ref[h] - ck_ref[h]
                if diagonal:
                    s = jnp.where(_iota2((bq, bk), 0) >= _iota2((bq, bk), 1), s, MASK_VALUE)
                m_old = m_s[h]
                m_new = jnp.maximum(m_old, jnp.max(s, axis=1, keepdims=True))
                p = jnp.exp(s - m_new)
                alpha = jnp.exp(m_old - m_new)
                l_s[h] = alpha * l_s[h] + jnp.sum(p, axis=1, keepdims=True)
                acc_s[h] = alpha * acc_s[h] + _mm(p.astype(bf16), v_ref[h])
                m_s[h] = m_new

        @pl.when(ki < qi)
        def _():
            step(False)

        @pl.when(ki == qi)
        def _():
            step(True)
            for h in range(HP):
                o_ref[:, h * DH:(h + 1) * DH] = acc_s[h] / l_s[h]
                lse_ref[h] = m_s[h] + jnp.log(l_s[h])

    qb = lambda hp, i, qs, ks: (hp, qs[i], 0)
    kb = lambda hp, i, qs, ks: (hp, ks[i], 0)
    return _pcall(
        body, name="attn_fwd",
        grid_spec=pltpu.PrefetchScalarGridSpec(
            num_scalar_prefetch=2, grid=(NH // HP, qs.shape[0]),
            in_specs=[pl.BlockSpec((HP, bq, DH), qb), pl.BlockSpec((HP, bk, DH), kb), pl.BlockSpec((HP, bk, DH), kb),
                      pl.BlockSpec((HP, bq, 1), qb),
                      pl.BlockSpec((HP, 1, bk), lambda hp, i, qs, ks: (hp, 0, ks[i]))],
            out_specs=[pl.BlockSpec((bq, HP * DH), lambda hp, i, qs, ks: (qs[i], hp)),
                       pl.BlockSpec((HP, bq, 1), qb)],
            scratch_shapes=[pltpu.VMEM((HP, bq, 1), f32), pltpu.VMEM((HP, bq, 1), f32),
                            pltpu.VMEM((HP, bq, DH), f32)]),
        out_shape=[jax.ShapeDtypeStruct((T, W), f32), jax.ShapeDtypeStruct((NH, T, 1), f32)],
        compiler_params=_params("parallel", "arbitrary"),
    )(qs, ks, qh, kh, vh, cq, ck)


def _attn_bwd_prep(dy, oh, z, tm):
    T = dy.shape[0]
    cg = OFF_ATT // W + 3

    def body(dy_ref, o_ref, g_ref, doh_ref, dl_ref):
        do = dy_ref[...] * _silu(g_ref[...])
        for h in range(NH):
            d = do[:, h * DH:(h + 1) * DH].astype(bf16)
            doh_ref[h] = d
            dl_ref[h] = jnp.sum(d.astype(f32) * o_ref[:, h * DH:(h + 1) * DH], axis=1, keepdims=True)

    return _pcall(
        body, name="attn_bwd_prep", grid=(T // tm,),
        in_specs=[pl.BlockSpec((tm, W), lambda i: (i, 0)),
                  pl.BlockSpec((tm, W), lambda i: (i, 0)),
                  _zblock(tm, cg)],
        out_specs=[pl.BlockSpec((NH, tm, DH), lambda i: (0, i, 0)),
                   pl.BlockSpec((NH, tm, 1), lambda i: (0, i, 0))],
        out_shape=[jax.ShapeDtypeStruct((NH, T, DH), bf16), jax.ShapeDtypeStruct((NH, T, 1), f32)],
        compiler_params=_params("parallel"),
    )(dy, oh, z)


def _attn_bwd(qh, kh, vh, cq, ck, doh, lse, delta, bq):
    T = qh.shape[1]
    nq = T // bq
    bk = bq
    qs, ks = _causal_pairs(nq, key_major=True)
    BW = HP * DH

    def body(qs_ref, ks_ref, q_ref, k_ref, v_ref, cq_ref, ck_ref, do_ref, lse_ref, dl_ref,
             dq_ref, dk_ref, dv_ref, dck_ref, dcq_ref, dk_s, dv_s, dck_s):
        hp, i = pl.program_id(0), pl.program_id(1)
        qi, ki = qs_ref[i], ks_ref[i]

        @pl.when(i == 0)
        def _():
            dq_ref[...] = jnp.zeros_like(dq_ref)
            dcq_ref[...] = jnp.zeros_like(dcq_ref)

        @pl.when(qi == ki)
        def _():
            dk_s[...] = jnp.zeros_like(dk_s)
            dv_s[...] = jnp.zeros_like(dv_s)
            dck_s[...] = jnp.zeros_like(dck_s)

        r0 = pl.multiple_of(qi * bq, bq)

        def step(diagonal):
            rowsums = jnp.zeros((bq, 128), f32)
            for h in range(HP):
                cols = slice(h * DH, (h + 1) * DH)
                q, k, v, do = q_ref[h], k_ref[h], v_ref[h], do_ref[h]
                p = jnp.exp(_mm_nt(q, k) + cq_ref[h] - ck_ref[h] - lse_ref[h])
                if diagonal:
                    p = jnp.where(_iota2((bq, bk), 0) >= _iota2((bq, bk), 1), p, 0.0)
                dv_s[:, cols] += _mm_tn(p.astype(bf16), do)
                ds = p * (_mm_nt(do, v) - dl_ref[h])
                dck_s[h] -= jnp.sum(ds, axis=0, keepdims=True)
                dsb = ds.astype(bf16)
                dk_s[:, cols] += _mm_tn(dsb, q)
                dq_ref[pl.ds(r0, bq), cols] += _mm(dsb, k)
                rowsums = rowsums + jnp.where(_iota2((bq, 128), 1) == hp * HP + h,
                                              jnp.sum(ds, axis=1, keepdims=True), 0.0)
            dcq_ref[pl.ds(r0, bq), :] += rowsums

        @pl.when(qi > ki)
        def _():
            step(False)

        @pl.when(qi == ki)
        def _():
            step(True)

        @pl.when(qi == nq - 1)
        def _():
            dk_ref[...] = dk_s[...]
            dv_ref[...] = dv_s[...]
            dck_ref[...] = dck_s[...]

    qb = lambda hp, i, qs, ks: (hp, qs[i], 0)
    kb = lambda hp, i, qs, ks: (hp, ks[i], 0)
    kcol = lambda hp, i, qs, ks: (ks[i], hp)
    return _pcall(
        body, name="attn_bwd",
        grid_spec=pltpu.PrefetchScalarGridSpec(
            num_scalar_prefetch=2, grid=(NH // HP, qs.shape[0]),
            in_specs=[pl.BlockSpec((HP, bq, DH), qb), pl.BlockSpec((HP, bk, DH), kb), pl.BlockSpec((HP, bk, DH), kb),
                      pl.BlockSpec((HP, bq, 1), qb),
                      pl.BlockSpec((HP, 1, bk), lambda hp, i, qs, ks: (hp, 0, ks[i])),
                      pl.BlockSpec((HP, bq, DH), qb), pl.BlockSpec((HP, bq, 1), qb), pl.BlockSpec((HP, bq, 1), qb)],
            out_specs=[pl.BlockSpec((T, BW), lambda hp, i, qs, ks: (0, hp)),
                       pl.BlockSpec((bk, BW), kcol), pl.BlockSpec((bk, BW), kcol),
                       pl.BlockSpec((HP, 1, bk), lambda hp, i, qs, ks: (hp, 0, ks[i])),
                       pl.BlockSpec((None, T, 128), lambda hp, i, qs, ks: (hp, 0, 0))],
            scratch_shapes=[pltpu.VMEM((bk, BW), f32), pltpu.VMEM((bk, BW), f32), pltpu.VMEM((HP, 1, bk), f32)]),
        out_shape=[jax.ShapeDtypeStruct((T, W), f32)] * 3 + [jax.ShapeDtypeStruct((NH, 1, T), f32),
                                                             jax.ShapeDtypeStruct((NH // HP, T, 128), f32)],
        compiler_params=_params("parallel", "arbitrary"),
    )(qs, ks, qh, kh, vh, cq, ck, doh, lse, delta)


def _attn_post(z, dy, oh, dqh, dkh, dvh, dck, dcq, fbias, gq, gk, tm):
    T = z.shape[0]
    c0 = OFF_ATT // W
    nt = T // tm

    def body(q_ref, k_ref, g_ref, f_ref, dy_ref, o_ref, dq_ref, dk_ref, dv_ref, dck_ref, dcq_ref, fb_ref, gq_ref,
             gk_ref, dz_ref, ggq_ref, ggk_ref, gfb_ref, carry):
        @pl.when(pl.program_id(0) == 0)
        def _():
            carry[...] = jnp.zeros_like(carry)
            ggq_ref[...] = jnp.zeros_like(ggq_ref)
            ggk_ref[...] = jnp.zeros_like(ggk_ref)
            gfb_ref[...] = jnp.zeros_like(gfb_ref)

        gm = _group_mean_matrix(W, DH)
        hs = jnp.where((_iota2((W, W), 0) & (DH - 1)) == (_iota2((W, W), 1) & (DH - 1)), 1.0, 0.0).astype(f32)

        def norm_bwd(x, dn, gain):
            r = lax.rsqrt(_group_mean(x * x, gm) + EPS)
            gg = jnp.sum(dn * x * r, axis=0, keepdims=True)
            u = dn * gain
            return r * u - x * (r * r * r) * _group_mean(u * x, gm), gg

        q, k, gate = q_ref[...], k_ref[...], g_ref[...]
        dq, ggq = norm_bwd(q, dq_ref[...] * (DH ** -0.5), gq_ref[...])
        dk, ggk = norm_bwd(k, dk_ref[...], gk_ref[...])
        ggq_ref[...] += jnp.dot(jnp.broadcast_to(ggq, (8, W)), hs, precision=HI, preferred_element_type=f32)[0:1]
        ggk_ref[...] += jnp.dot(jnp.broadcast_to(ggk, (8, W)), hs, precision=HI, preferred_element_type=f32)[0:1]
        dgate = dy_ref[...] * o_ref[...] * _dsilu(gate)
        dck_v = dck_ref[...]
        for hp in range(NH // HP):
            dck_v = dck_v + dcq_ref[hp]
        rc = jnp.dot(_upper_tri(tm), dck_v, precision=HI, preferred_element_type=f32) + carry[...]
        carry[...] += jnp.sum(dck_v, axis=0, keepdims=True)
        f = f_ref[...] + fb_ref[...]
        df = jnp.where(_iota2((tm, 128), 1) < NH, rc * _sigmoid(-f), 0.0)
        gfb_ref[...] += jnp.sum(df, axis=0, keepdims=True)
        dz_ref[:, 0:W] = dq.astype(bf16)
        dz_ref[:, W:2 * W] = dk.astype(bf16)
        dz_ref[:, 2 * W:3 * W] = dv_ref[...].astype(bf16)
        dz_ref[:, 3 * W:4 * W] = dgate.astype(bf16)
        dz_ref[:, 4 * W:4 * W + 128] = df.astype(bf16)

    rev = lambda i: nt - 1 - i
    zb = lambda col: pl.BlockSpec((tm, W), lambda i, c=col: (rev(i), c))
    hspec = pl.BlockSpec((tm, W), lambda i: (rev(i), 0))
    return _pcall(
        body, name="attn_post", grid=(nt,),
        in_specs=[zb(c0), zb(c0 + 1), zb(c0 + 3),
                  pl.BlockSpec((tm, 128), lambda i: (rev(i), OFF_F // 128)),
                  pl.BlockSpec((tm, W), lambda i: (rev(i), 0)),
                  hspec, hspec, hspec, hspec,
                  pl.BlockSpec((tm, 128), lambda i: (rev(i), 0)),
                  pl.BlockSpec((NH // HP, tm, 128), lambda i: (0, rev(i), 0)),
                  pl.BlockSpec((1, 128), lambda i: (0, 0)),
                  pl.BlockSpec((1, W), lambda i: (0, 0)), pl.BlockSpec((1, W), lambda i: (0, 0))],
        out_specs=[pl.BlockSpec((tm, 4 * W + 128), lambda i: (rev(i), 0)),
                   pl.BlockSpec((1, W), lambda i: (0, 0)), pl.BlockSpec((1, W), lambda i: (0, 0)),
                   pl.BlockSpec((1, 128), lambda i: (0, 0))],
        out_shape=[jax.ShapeDtypeStruct((T, 4 * W + 128), bf16), jax.ShapeDtypeStruct((1, W), f32),
                   jax.ShapeDtypeStruct((1, W), f32), jax.ShapeDtypeStruct((1, 128), f32)],
        scratch_shapes=[pltpu.VMEM((1, 128), f32)],
        compiler_params=_params("arbitrary"),
    )(z, z, z, z, dy, oh, dqh, dkh, dvh, dck, dcq, fbias, gq, gk)


def _merge_fwd(ya, oh, z, yc, yd, mb, x, p, wup, wo, gp, wpg, wpp, tm):
    T = x.shape[0]
    cg = OFF_ATT // W + 3

    def body(ya_ref, oh_ref, bg_ref, yc_ref, yd_ref, ml_ref, mb_ref, x_ref, p_ref, wup_ref, wo_ref, gp_ref,
             wpg_ref, wpp_ref, yb_ref, mg_ref, x1_ref, x2_ref):
        yb = (oh_ref[...] * _silu(bg_ref[...])).astype(bf16)
        yb_ref[...] = yb
        ys = (ya_ref[...], yb, yc_ref[...], yd_ref[...])
        merged = jnp.zeros((tm, D), f32)
        for b in range(NBR):
            sg = _sigmoid(ml_ref[:, b * D:(b + 1) * D] + mb_ref[b:b + 1, :])
            merged = merged + sg * _mm(ys[b], wup_ref[b])
        mgb = merged.astype(bf16)
        mg_ref[...] = mgb
        x1 = x_ref[...] + _mm(mgb, wo_ref[...])
        x1_ref[...] = x1
        r = lax.rsqrt(jnp.mean(x1 * x1, axis=-1, keepdims=True) + EPS)
        hp = (x1 * r * gp_ref[...]).astype(bf16)
        gate = _sigmoid(_mm(hp, wpg_ref[...]))
        x2_ref[...] = x1 + gate * _mm(p_ref[...].astype(bf16), wpp_ref[...])

    row = lambda width: pl.BlockSpec((tm, width), lambda i: (i, 0))
    full = lambda *shape: pl.BlockSpec(shape, lambda i: (0,) * len(shape))
    return _pcall(
        body, name="merge_fwd", grid=(T // tm,),
        in_specs=[row(W), row(W), _zblock(tm, cg), row(W), row(W),
                  pl.BlockSpec((tm, NBR * D), lambda i: (i, 0)), full(NBR, D), row(D), row(PLE),
                  full(NBR, W, D), full(D, D), full(1, D), full(D, D), full(PLE, D)],
        out_specs=[row(W), row(D), row(D), row(D)],
        out_shape=[jax.ShapeDtypeStruct((T, W), bf16), jax.ShapeDtypeStruct((T, D), bf16),
                   jax.ShapeDtypeStruct((T, D), f32), jax.ShapeDtypeStruct((T, D), f32)],
        compiler_params=_params("parallel"),
    )(ya, oh, z, yc, yd, z, mb, x, p, wup, wo, gp, wpg, wpp)


def _ple_bwd(dx2, x1, p, gp, wpg, wpp, tm):
    T = x1.shape[0]

    def body(dx2_ref, x1_ref, p_ref, gp_ref, wpg_ref, wpp_ref, dx1_ref, gwpg_ref, gwpp_ref, ggp_ref):
        @pl.when(pl.program_id(0) == 0)
        def _():
            gwpg_ref[...] = jnp.zeros_like(gwpg_ref)
            gwpp_ref[...] = jnp.zeros_like(gwpp_ref)
            ggp_ref[...] = jnp.zeros_like(ggp_ref)

        x1, dx2 = x1_ref[...], dx2_ref[...]
        r = lax.rsqrt(jnp.mean(x1 * x1, axis=-1, keepdims=True) + EPS)
        xh = x1 * r
        gp = gp_ref[...]
        hp = (xh * gp).astype(bf16)
        gate = _sigmoid(_mm(hp, wpg_ref[...]))
        pb = p_ref[...].astype(bf16)
        pp = _mm(pb, wpp_ref[...])
        dpre = ((dx2 * pp) * gate * (1.0 - gate)).astype(bf16)
        gwpp_ref[...] += _mm_tn(pb, (dx2 * gate).astype(bf16))
        gwpg_ref[...] += _mm_tn(hp, dpre)
        dhp = _mm_nt(dpre, wpg_ref[...])
        ggp_ref[...] += jnp.sum(dhp * xh, axis=0, keepdims=True)
        u = dhp * gp
        dx1_ref[...] = dx2 + r * u - x1 * (r * r * r) * jnp.mean(u * x1, axis=-1, keepdims=True)

    row = lambda width: pl.BlockSpec((tm, width), lambda i: (i, 0))
    full = lambda *shape: pl.BlockSpec(shape, lambda i: (0,) * len(shape))
    return _pcall(
        body, name="ple_bwd", grid=(T // tm,),
        in_specs=[row(D), row(D), row(PLE), full(1, D), full(D, D), full(PLE, D)],
        out_specs=[row(D), full(D, D), full(PLE, D), full(1, D)],
        out_shape=[jax.ShapeDtypeStruct((T, D), f32), jax.ShapeDtypeStruct((D, D), f32),
                   jax.ShapeDtypeStruct((PLE, D), f32), jax.ShapeDtypeStruct((1, D), f32)],
        compiler_params=_params("arbitrary"),
    )(dx2, x1, p, gp, wpg, wpp)


def _merge_bwd(dx1, mg, ya, yb, yc, yd, z, mb, wup, wo, tm):
    T = dx1.shape[0]

    def body(dx1_ref, mg_ref, ya_ref, yb_ref, yc_ref, yd_ref, ml_ref, mb_ref, wup_ref, wo_ref,
             dml_ref, dya_ref, dyb_ref, dyc_ref, dyd_ref, gwo_ref, gwup_ref, gmb_ref):
        @pl.when(pl.program_id(0) == 0)
        def _():
            gwo_ref[...] = jnp.zeros_like(gwo_ref)
            gwup_ref[...] = jnp.zeros_like(gwup_ref)
            gmb_ref[...] = jnp.zeros_like(gmb_ref)

        dx1b = dx1_ref[...].astype(bf16)
        gwo_ref[...] += _mm_tn(mg_ref[...], dx1b)
        dm = _mm_nt(dx1b, wo_ref[...])
        ys = (ya_ref, yb_ref, yc_ref, yd_ref)
        dys = (dya_ref, dyb_ref, dyc_ref, dyd_ref)
        for b in range(NBR):
            y = ys[b][...]
            up = _mm(y, wup_ref[b])
            sg = _sigmoid(ml_ref[:, b * D:(b + 1) * D] + mb_ref[b:b + 1, :])
            dup = (dm * sg).astype(bf16)
            dml = dm * up * sg * (1.0 - sg)
            gmb_ref[b:b + 1, :] += jnp.sum(dml, axis=0, keepdims=True)
            dml_ref[:, b * D:(b + 1) * D] = dml.astype(bf16)
            gwup_ref[b] += _mm_tn(y, dup)
            dys[b][...] = _mm_nt(dup, wup_ref[b])

    row = lambda width: pl.BlockSpec((tm, width), lambda i: (i, 0))
    full = lambda *shape: pl.BlockSpec(shape, lambda i: (0,) * len(shape))
    return _pcall(
        body, name="merge_bwd", grid=(T // tm,),
        in_specs=[row(D), row(D), row(W), row(W), row(W), row(W), row(NBR * D), full(NBR, D),
                  full(NBR, W, D), full(D, D)],
        out_specs=[row(NBR * D), row(W), row(W), row(W), row(W), full(D, D), full(NBR, W, D), full(NBR, D)],
        out_shape=[jax.ShapeDtypeStruct((T, NBR * D), bf16)] + [jax.ShapeDtypeStruct((T, W), f32)] * 4
        + [jax.ShapeDtypeStruct((D, D), f32), jax.ShapeDtypeStruct((NBR, W, D), f32),
           jax.ShapeDtypeStruct((NBR, D), f32)],
        compiler_params=_params("arbitrary"),
    )(dx1, mg, ya, yb, yc, yd, z, mb, wup, wo)


def _loss_head(y, target, tm):
    T = y.shape[0]

    def body(y_ref, t_ref, loss_ref, dy_ref, acc):
        i = pl.program_id(0)

        @pl.when(i == 0)
        def _():
            acc[...] = jnp.zeros_like(acc)

        e = y_ref[...] - t_ref[...]
        dy_ref[...] = e * (1.0 / D)
        acc[...] += jnp.sum(e * e, axis=0, keepdims=True)

        @pl.when(i == T // tm - 1)
        def _():
            loss_ref[...] = jnp.sum(acc[...], axis=1, keepdims=True) * (0.5 / D)

    return _pcall(
        body, name="loss_head", grid=(T // tm,),
        in_specs=[pl.BlockSpec((tm, D), lambda i: (i, 0)), pl.BlockSpec((tm, D), lambda i: (i, 0))],
        out_specs=[pl.BlockSpec((1, 1), lambda i: (0, 0)), pl.BlockSpec((tm, D), lambda i: (i, 0))],
        out_shape=[jax.ShapeDtypeStruct((1, 1), f32), jax.ShapeDtypeStruct((T, D), f32)],
        scratch_shapes=[pltpu.VMEM((1, D), f32)],
        compiler_params=_params("arbitrary"),
    )(y, target)


def _lb_softmax_rows(l_ref):
    rows = [l_ref[i:i + 1, :] for i in range(DEPTH)]
    m = rows[0]
    for r in rows[1:]:
        m = jnp.maximum(m, r)
    es = [jnp.exp(r - m) for r in rows]
    tot = es[0]
    for e in es[1:]:
        tot = tot + e
    return [e / tot for e in es]


def _lb_partial_sums(pr):
    sums = [jnp.zeros_like(pr[0])]
    for i in range(1, DEPTH):
        sums.append(sums[-1] + pr[i])
    return sums


def _stack_rows(rows, width):
    idx = _iota2((8, width), 0)
    out = jnp.zeros((8, width), f32)
    for i, r in enumerate(rows):
        out = jnp.where(idx == i, r, out)
    return out


def _lower_bounds(lb_logits):
    def body(l_ref, o_ref):
        sums = _lb_partial_sums(_lb_softmax_rows(l_ref))
        o_ref[...] = _stack_rows([jnp.clip(s, 0.0, 1.0) for s in sums], W)

    return _pcall(body, name="lower_bounds", out_shape=jax.ShapeDtypeStruct((8, W), f32))(lb_logits)


def _lower_bounds_bwd(lb_logits, dlower):
    def body(l_ref, d_ref, o_ref):
        pr = _lb_softmax_rows(l_ref)
        sums = _lb_partial_sums(pr)
        dl = [jnp.where((sums[i] > 0.0) & (sums[i] < 1.0), d_ref[i:i + 1, :], 0.0) for i in range(DEPTH)]
        dp = [jnp.zeros_like(pr[0])] * DEPTH
        run = jnp.zeros_like(pr[0])
        for j in reversed(range(1, DEPTH)):
            run = run + dl[j]
            dp[j] = run
        inner = pr[0] * dp[0]
        for j in range(1, DEPTH):
            inner = inner + pr[j] * dp[j]
        o_ref[...] = _stack_rows([pr[j] * (dp[j] - inner) for j in range(DEPTH)], W)

    return _pcall(body, name="lower_bounds_bwd", out_shape=jax.ShapeDtypeStruct((8, W), f32))(lb_logits, dlower)


def _row_tile(rows, cols, budget_bytes=1 << 20, mult=8):
    if rows % mult:
        return rows
    best = mult
    for t in range(mult, rows + 1, mult):
        if rows % t == 0 and t * cols * 4 <= budget_bytes:
            best = t
    return best


def _sum_slabs(land):
    _, R, C = land.shape
    tr = _row_tile(R, C * N_DEV, mult=16)

    def body(l_ref, o_ref):
        acc = l_ref[0].astype(f32)
        for j in range(1, N_DEV):
            acc = acc + l_ref[j].astype(f32)
        o_ref[...] = acc

    return _pcall(
        body, name="sum_slabs", grid=(R // tr,),
        in_specs=[pl.BlockSpec((N_DEV, tr, C), lambda i: (0, i, 0))],
        out_specs=pl.BlockSpec((tr, C), lambda i: (i, 0)),
        out_shape=jax.ShapeDtypeStruct((R, C), f32),
        compiler_params=_params("parallel"),
    )(land)


def _adamw(w, g, m, v):
    R, C = w.shape
    tr = _row_tile(R, C)
    c1 = 1.0 / (1.0 - ADAM_B1 ** ADAM_STEP)
    c2 = 1.0 / (1.0 - ADAM_B2 ** ADAM_STEP)

    def body(w_ref, g_ref, m_ref, v_ref, d_ref, nm_ref, nv_ref):
        gv = g_ref[...]
        nm = ADAM_B1 * m_ref[...] + (1.0 - ADAM_B1) * gv
        nv = ADAM_B2 * v_ref[...] + (1.0 - ADAM_B2) * (gv * gv)
        nm_ref[...] = nm
        nv_ref[...] = nv
        d_ref[...] = -ADAM_LR * ((nm * c1) / (jnp.sqrt(nv * c2) + ADAM_EPS) + ADAM_WD * w_ref[...])

    spec = pl.BlockSpec((tr, C), lambda i: (i, 0))
    return _pcall(
        body, name="adamw", grid=(R // tr,),
        in_specs=[spec] * 4, out_specs=[spec] * 3,
        out_shape=[jax.ShapeDtypeStruct((R, C), f32)] * 3,
        compiler_params=_params("parallel"),
    )(w, g, m, v)


def _my_id():
    return lax.axis_index("x") * 4 + lax.axis_index("y") * 2 + lax.axis_index("c")


def _peer(k):
    x, y, c = lax.axis_index("x"), lax.axis_index("y"), lax.axis_index("c")
    kx, ky, kc = (k >> 2) & 1, (k >> 1) & 1, k & 1
    px, py, pc = x ^ kx, y ^ ky, c ^ kc
    return (px, py, pc), px * 4 + py * 2 + pc


def _all_gather(shards):
    n = len(shards)

    def body(*refs):
        srcs, outs = refs[:n], refs[n:2 * n]
        send_sems, recv_sems, local_sems = refs[2 * n:]
        me = _my_id()
        locals_ = [pltpu.make_async_copy(srcs[a], outs[a].at[me], local_sems.at[a]) for a in range(n)]
        for cp in locals_:
            cp.start()
        sends = []
        for k in range(1, N_DEV):
            peer, _ = _peer(k)
            for a in range(n):
                cp = pltpu.make_async_remote_copy(
                    src_ref=srcs[a], dst_ref=outs[a].at[me],
                    send_sem=send_sems.at[a, k - 1], recv_sem=recv_sems.at[a, k - 1],
                    device_id=peer, device_id_type=pl.DeviceIdType.MESH)
                cp.start()
                sends.append(cp)
        for k in range(1, N_DEV):
            peer, pid = _peer(k)
            for a in range(n):
                pltpu.make_async_remote_copy(
                    src_ref=srcs[a], dst_ref=outs[a].at[pid],
                    send_sem=send_sems.at[a, k - 1], recv_sem=recv_sems.at[a, k - 1],
                    device_id=peer, device_id_type=pl.DeviceIdType.MESH).wait_recv()
        for cp in sends:
            cp.wait_send()
        for cp in locals_:
            cp.wait()

    hbm = pl.BlockSpec(memory_space=pltpu.HBM)
    return _pcall(
        body, name="all_gather",
        in_specs=[hbm] * n, out_specs=[hbm] * n,
        out_shape=[jax.ShapeDtypeStruct((N_DEV,) + s.shape, s.dtype) for s in shards],
        scratch_shapes=[pltpu.SemaphoreType.DMA((n, N_DEV - 1)), pltpu.SemaphoreType.DMA((n, N_DEV - 1)),
                        pltpu.SemaphoreType.DMA((n,))],
    )(*shards)


def _exchange(sliced, whole):
    ns, nw = len(sliced), len(whole)
    n = ns + nw

    def body(*refs):
        srcs, outs = refs[:n], refs[n:2 * n]
        send_sems, recv_sems, local_sems = refs[2 * n:]
        me = _my_id()

        def src_of(a, dest):
            return srcs[a].at[dest] if a < ns else srcs[a]

        locals_ = [pltpu.make_async_copy(src_of(a, me), outs[a].at[me], local_sems.at[a]) for a in range(n)]
        for cp in locals_:
            cp.start()
        sends = []
        for k in range(1, N_DEV):
            peer, pid = _peer(k)
            for a in range(n):
                cp = pltpu.make_async_remote_copy(
                    src_ref=src_of(a, pid), dst_ref=outs[a].at[me],
                    send_sem=send_sems.at[a, k - 1], recv_sem=recv_sems.at[a, k - 1],
                    device_id=peer, device_id_type=pl.DeviceIdType.MESH)
                cp.start()
                sends.append(cp)
        for k in range(1, N_DEV):
            peer, pid = _peer(k)
            for a in range(n):
                pltpu.make_async_remote_copy(
                    src_ref=src_of(a, pid), dst_ref=outs[a].at[pid],
                    send_sem=send_sems.at[a, k - 1], recv_sem=recv_sems.at[a, k - 1],
                    device_id=peer, device_id_type=pl.DeviceIdType.MESH).wait_recv()
        for cp in sends:
            cp.wait_send()
        for cp in locals_:
            cp.wait()

    hbm = pl.BlockSpec(memory_space=pltpu.HBM)
    shapes = [jax.ShapeDtypeStruct(s.shape, s.dtype) for s in sliced]
    shapes += [jax.ShapeDtypeStruct((N_DEV,) + s.shape, s.dtype) for s in whole]
    return _pcall(
        body, name="grad_exchange",
        in_specs=[hbm] * n, out_specs=[hbm] * n, out_shape=shapes,
        scratch_shapes=[pltpu.SemaphoreType.DMA((n, N_DEV - 1)), pltpu.SemaphoreType.DMA((n, N_DEV - 1)),
                        pltpu.SemaphoreType.DMA((n,))],
    )(*sliced, *whole)


def _permute_cols(w):
    pad = jnp.zeros(w.shape[:-1] + (NZ - OFF_F - NH,), w.dtype)
    return jnp.concatenate([
        w[..., 3844:7940],
        w[..., 0:1024],
        w[..., 2052:3076],
        w[..., 3076:3844],
        w[..., 1024:2048],
        w[..., 2048:2052], pad], axis=-1)


def _unpermute_cols(g):
    return jnp.concatenate([
        g[..., OFF_CONV:OFF_CONV + 1024],
        g[..., OFF_ATT:OFF_ATT + 1024],
        g[..., OFF_F:OFF_F + NH],
        g[..., OFF_HGRN:OFF_HGRN + 1024],
        g[..., OFF_SGU:OFF_SGU + 768],
        g[..., 0:4096]], axis=-1)


_SMALL = (
    ("norm_mix", (DEPTH, D)), ("conv_w", (DEPTH, CONV_WIDTH, W)), ("conv_b", (DEPTH, W)),
    ("fgate_bias", (DEPTH, NH)), ("q_norm", (DEPTH, DH)), ("k_norm", (DEPTH, DH)),
    ("lb_logits", (DEPTH, W)), ("hgrn_norm", (DEPTH, W)), ("sgu_norm", (DEPTH, W)),
    ("spatial_w", (DEPTH, NH, SGU_CHUNK, SGU_CHUNK)), ("spatial_b", (DEPTH, NH, SGU_CHUNK)),
    ("merge_b", (DEPTH, NBR, D)), ("norm_ple", (DEPTH, D)),
)


def _small_rows(shape):
    size = 1
    for s in shape:
        size *= s
    rows = -(-size // 128)
    return size, -(-rows // 8) * 8


def _pack_small(parts):
    out = []
    for name, shape in _SMALL:
        size, rows = _small_rows(shape)
        flat = parts[name].astype(f32).reshape(-1)
        flat = jnp.pad(flat, (0, rows * 128 - size))
        out.append(flat.reshape(rows, 128))
    return jnp.concatenate(out, axis=0)


def _unpack_small(buf):
    parts, r0 = {}, 0
    for name, shape in _SMALL:
        size, rows = _small_rows(shape)
        parts[name] = buf[r0:r0 + rows].reshape(-1)[:size].reshape(shape)
        r0 += rows
    return parts


def _shard_cols(a, width):
    return lax.dynamic_slice_in_dim(a, _my_id() * width, width, axis=a.ndim - 1)


def kernel(x, p, norm_mix, w_in, conv_w, conv_b, fgate_bias, q_norm, k_norm, lb_logits, hgrn_norm, sgu_norm, spatial_w, spatial_b, w_up, merge_b, w_o, norm_ple, w_ple_gate, w_ple_proj, loss_target, m_norm_mix, m_w_in, m_conv_w, m_conv_b, m_fgate_bias, m_q_norm, m_k_norm, m_lb_logits, m_hgrn_norm, m_sgu_norm, m_spatial_w, m_spatial_b, m_w_up, m_merge_b, m_w_o, m_norm_ple, m_w_ple_gate, m_w_ple_proj, v_norm_mix, v_w_in, v_conv_w, v_conv_b, v_fgate_bias, v_q_norm, v_k_norm, v_lb_logits, v_hgrn_norm, v_sgu_norm, v_spatial_w, v_spatial_b, v_w_up, v_merge_b, v_w_o, v_norm_ple, v_w_ple_gate, v_w_ple_proj):
    T = x.shape[1]
    SH = D // N_DEV
    CW = W // N_DEV
    tm = 512 if T % 512 == 0 else T
    tmm = 256 if T % 256 == 0 else T
    x0 = x.reshape(T, D)
    target = loss_target.reshape(T, D)

    small_shard = jnp.concatenate([
        merge_b.reshape(DEPTH * NBR, SH),
        jnp.pad(conv_w.reshape(DEPTH * CONV_WIDTH, CW), ((0, 16 - DEPTH * CONV_WIDTH), (0, SH - CW)))], axis=0)
    g_win, g_wup, g_wo, g_wpg, g_wpp, g_small = _all_gather([
        _permute_cols(w_in).astype(bf16).reshape(DEPTH * SH, NZ),
        w_up.astype(bf16).reshape(DEPTH * NBR * W, SH),
        w_o.astype(bf16).reshape(DEPTH * SH, D),
        w_ple_gate.astype(bf16).reshape(DEPTH * SH, D),
        w_ple_proj.astype(bf16).reshape(DEPTH * PLE, SH),
        small_shard])
    win_f = g_win.reshape(N_DEV, DEPTH, SH, NZ).transpose(1, 0, 2, 3).reshape(DEPTH, D, NZ)
    wup_f = g_wup.reshape(N_DEV, DEPTH, NBR, W, SH).transpose(1, 2, 3, 0, 4).reshape(DEPTH, NBR, W, D)
    wo_f = g_wo.reshape(N_DEV, DEPTH, SH, D).transpose(1, 0, 2, 3).reshape(DEPTH, D, D)
    wpg_f = g_wpg.reshape(N_DEV, DEPTH, SH, D).transpose(1, 0, 2, 3).reshape(DEPTH, D, D)
    wpp_f = g_wpp.reshape(N_DEV, DEPTH, PLE, SH).transpose(1, 2, 0, 3).reshape(DEPTH, PLE, D)
    mb_f = g_small[:, 0:DEPTH * NBR].reshape(N_DEV, DEPTH, NBR, SH).transpose(1, 2, 0, 3).reshape(DEPTH, NBR, D)
    cw_f = g_small[:, 16:16 + DEPTH * CONV_WIDTH, 0:CW].reshape(N_DEV, DEPTH, CONV_WIDTH, CW)
    cw_f = cw_f.transpose(1, 2, 0, 3).reshape(DEPTH, CONV_WIDTH, W)

    loss_local, dx, gw, gs_full = _forward_backward(
        x0, p[:, 0], target, win_f, wup_f, wo_f, wpg_f, wpp_f, mb_f, cw_f, norm_mix, conv_b, fgate_bias, q_norm,
        k_norm, lb_logits, hgrn_norm, sgu_norm, spatial_w, spatial_b, norm_ple)
    loss = lax.psum(loss_local[0, 0], AXES)
    grad_x = dx.reshape(1, T, D)

    weights = dict(norm_mix=norm_mix, w_in=w_in, conv_w=conv_w, conv_b=conv_b, fgate_bias=fgate_bias, q_norm=q_norm,
                   k_norm=k_norm, lb_logits=lb_logits, hgrn_norm=hgrn_norm, sgu_norm=sgu_norm, spatial_w=spatial_w,
                   spatial_b=spatial_b, w_up=w_up, merge_b=merge_b, w_o=w_o, norm_ple=norm_ple,
                   w_ple_gate=w_ple_gate, w_ple_proj=w_ple_proj)
    ms = dict(norm_mix=m_norm_mix, w_in=m_w_in, conv_w=m_conv_w, conv_b=m_conv_b, fgate_bias=m_fgate_bias,
              q_norm=m_q_norm, k_norm=m_k_norm, lb_logits=m_lb_logits, hgrn_norm=m_hgrn_norm, sgu_norm=m_sgu_norm,
              spatial_w=m_spatial_w, spatial_b=m_spatial_b, w_up=m_w_up, merge_b=m_merge_b, w_o=m_w_o,
              norm_ple=m_norm_ple, w_ple_gate=m_w_ple_gate, w_ple_proj=m_w_ple_proj)
    vs = dict(norm_mix=v_norm_mix, w_in=v_w_in, conv_w=v_conv_w, conv_b=v_conv_b, fgate_bias=v_fgate_bias,
              q_norm=v_q_norm, k_norm=v_k_norm, lb_logits=v_lb_logits, hgrn_norm=v_hgrn_norm, sgu_norm=v_sgu_norm,
              spatial_w=v_spatial_w, spatial_b=v_spatial_b, w_up=v_w_up, merge_b=v_merge_b, w_o=v_w_o,
              norm_ple=v_norm_ple, w_ple_gate=v_w_ple_gate, w_ple_proj=v_w_ple_proj)
    return _exchange_and_update(loss, grad_x, gw, gs_full, weights, ms, vs)


def _forward_backward(x0, p, target, win_f, wup_f, wo_f, wpg_f, wpp_f, mb_f, cw_f, norm_mix, conv_b, fgate_bias,
                      q_norm, k_norm, lb_logits, hgrn_norm, sgu_norm, spatial_w, spatial_b, norm_ple):
    T = x0.shape[0]
    tm = 512 if T % 512 == 0 else T
    tmm = 256 if T % 256 == 0 else T
    tmi = 1024 if T % 1024 == 0 else tm
    lower = _lower_bounds(lb_logits)
    fb_pad = jnp.pad(fgate_bias, ((0, 0), (0, 128 - NH)))
    gq_t = jnp.tile(q_norm, (1, NH))
    gk_t = jnp.tile(k_norm, (1, NH))
    sbe = jnp.repeat(jnp.swapaxes(spatial_b, 1, 2), DH, axis=2)

    saved = []
    xc = x0
    p = p[:, None]
    for li in range(DEPTH):
        row = lambda a: a[li:li + 1]
        z, h = _inproj_fwd(xc, row(norm_mix), win_f[li], tmi)
        ya = _conv_fwd(z, cw_f[li], row(conv_b), tm)
        yd = _sgu_fwd(z, row(sgu_norm), spatial_w[li], sbe[li], tm)
        yc, o_pre, states = _hgrn_fwd(z, lower[li:li + 1], row(hgrn_norm), tmm)
        qh, kh, vh, cum = _attn_prep(z, row(fb_pad), row(gq_t), row(gk_t), tm)
        cum4 = jnp.transpose(cum[:, 0:NH])
        cq, ck = cum4[:, :, None], cum4[:, None, :]
        oh, lse = _attn_fwd(qh, kh, vh, cq, ck, tm)
        yb, mg, x1, x2 = _merge_fwd(ya, oh, z, yc, yd, mb_f[li], xc, p[li, 0], wup_f[li], wo_f[li],
                                    row(norm_ple), wpg_f[li], wpp_f[li], tmm)
        saved.append(dict(x=xc, z=z, h=h, ya=ya, yb=yb, yc=yc, yd=yd, o_pre=o_pre, states=states,
                          qh=qh, kh=kh, vh=vh, cq=cq, ck=ck, oh=oh, lse=lse, mg=mg, x1=x1))
        xc = x2

    loss_local, dx = _loss_head(xc, target, tm)

    gw = {n: [None] * DEPTH for n in ("w_in", "w_up", "w_o", "w_ple_gate", "w_ple_proj")}
    gs = {n: [None] * DEPTH for n, _ in _SMALL}
    dlower = [None] * DEPTH
    for li in reversed(range(DEPTH)):
        s = saved[li]
        row = lambda a: a[li:li + 1]
        dx1, gw["w_ple_gate"][li], gw["w_ple_proj"][li], ggp = _ple_bwd(
            dx, s["x1"], p[li, 0], row(norm_ple), wpg_f[li], wpp_f[li], tmm)
        gs["norm_ple"][li] = ggp[0]
        dml, dya, dyb, dyc, dyd, gw["w_o"][li], gw["w_up"][li], gs["merge_b"][li] = _merge_bwd(
            dx1, s["mg"], s["ya"], s["yb"], s["yc"], s["yd"], s["z"], mb_f[li], wup_f[li], wo_f[li], tmm)
        dz_conv, gcw, gcb = _conv_bwd(s["z"], dya, cw_f[li], row(conv_b), tm)
        gs["conv_w"][li], gs["conv_b"][li] = gcw[0:CONV_WIDTH], gcb[0]
        dz_sgu, gs["spatial_w"][li], gsb, ggv = _sgu_bwd(s["z"], dyd, row(sgu_norm), spatial_w[li], sbe[li], tm)
        gs["spatial_b"][li] = jnp.transpose(gsb[:, ::DH])
        gs["sgu_norm"][li] = ggv[0]
        dz_hgrn, ggn, glb = _hgrn_bwd(s["z"], lower[li:li + 1], row(hgrn_norm), s["o_pre"], s["states"], dyc, tmm)
        gs["hgrn_norm"][li], dlower[li] = ggn[0], glb[0]
        doh, delta = _attn_bwd_prep(dyb, s["oh"], s["z"], tm)
        dqh, dkh, dvh, dck, dcq = _attn_bwd(s["qh"], s["kh"], s["vh"], s["cq"], s["ck"], doh, s["lse"], delta, tm)
        dck_t = jnp.pad(jnp.transpose(dck.reshape(NH, T)), ((0, 0), (0, 128 - NH)))
        dz_att, ggq, ggk, gfb = _attn_post(s["z"], dyb, s["oh"], dqh, dkh, dvh, dck_t, dcq, row(fb_pad),
                                           row(gq_t), row(gk_t), tm)
        gs["q_norm"][li], gs["k_norm"][li], gs["fgate_bias"][li] = ggq[0, 0:DH], ggk[0, 0:DH], gfb[0, 0:NH]
        dz = jnp.concatenate([dml, dz_conv, dz_hgrn, dz_sgu, dz_att], axis=1)
        dx, gnm = _inproj_bwd_x(dz, win_f[li], s["x"], dx1, row(norm_mix), tmi)
        gs["norm_mix"][li] = gnm[0]
        gw["w_in"][li] = _inproj_bwd_w(s["h"], dz, tmi)
    dlower8 = jnp.pad(jnp.stack(dlower), ((0, 8 - DEPTH), (0, 0)))
    gs_full = {n: jnp.stack(v) for n, v in gs.items() if n != "lb_logits"}
    gs_full["lb_logits"] = _lower_bounds_bwd(lb_logits, dlower8)[0:DEPTH]
    return loss_local, dx, gw, gs_full


def _exchange_and_update(loss, grad_x, gw, gs_full, weights, ms, vs):
    SH = D // N_DEV
    CW = W // N_DEV

    def by_rows(g):
        C = g.shape[-1]
        return g.astype(bf16).reshape(DEPTH, N_DEV, SH, C).transpose(1, 0, 2, 3).reshape(N_DEV, DEPTH * SH, C)

    def by_cols(g):
        lead = g.shape[:-1]
        r = 1
        for s_ in lead:
            r *= s_
        return g.astype(bf16).reshape(r, N_DEV, SH).transpose(1, 0, 2)

    small_buf = _pack_small(gs_full)
    l_win, l_wup, l_wo, l_wpg, l_wpp, l_small = _exchange(
        [by_rows(jnp.stack(gw["w_in"])), by_cols(jnp.stack(gw["w_up"])), by_rows(jnp.stack(gw["w_o"])),
         by_rows(jnp.stack(gw["w_ple_gate"])), by_cols(jnp.stack(gw["w_ple_proj"]))],
        [small_buf])

    g_w_in = _unpermute_cols(_sum_slabs(l_win)).reshape(DEPTH, SH, IN_COLS)
    g_w_up = _sum_slabs(l_wup).reshape(DEPTH, NBR, W, SH)
    g_w_o = _sum_slabs(l_wo).reshape(DEPTH, SH, D)
    g_w_pg = _sum_slabs(l_wpg).reshape(DEPTH, SH, D)
    g_w_pp = _sum_slabs(l_wpp).reshape(DEPTH, PLE, SH)
    g_small = _unpack_small(_sum_slabs(l_small))
    g_small_local = dict(g_small)
    g_small_local["conv_w"] = _shard_cols(g_small["conv_w"], CW)
    g_small_local["merge_b"] = _shard_cols(g_small["merge_b"], SH)

    grads = dict(w_in=g_w_in, w_up=g_w_up, w_o=g_w_o, w_ple_gate=g_w_pg, w_ple_proj=g_w_pp)
    deltas, new_m, new_v = {}, {}, {}
    for name, cols in (("w_in", IN_COLS), ("w_up", SH), ("w_o", D), ("w_ple_gate", D), ("w_ple_proj", SH)):
        shape = weights[name].shape
        d_, m_, v_ = _adamw(weights[name].reshape(-1, cols), grads[name].reshape(-1, cols),
                            ms[name].reshape(-1, cols), vs[name].reshape(-1, cols))
        deltas[name], new_m[name], new_v[name] = d_.reshape(shape), m_.reshape(shape), v_.reshape(shape)

    def local_shapes(parts):
        return {n: (parts[n] if parts[n].shape == s else jnp.pad(
            parts[n], [(0, 0)] * (len(s) - 1) + [(0, s[-1] - parts[n].shape[-1])])) for n, s in _SMALL}

    d_, m_, v_ = _adamw(_pack_small(local_shapes(weights)), _pack_small(local_shapes(g_small_local)),
                        _pack_small(local_shapes(ms)), _pack_small(local_shapes(vs)))
    for buf, dst in ((d_, deltas), (m_, new_m), (v_, new_v)):
        parts = _unpack_small(buf)
        for n, _ in _SMALL:
            dst[n] = parts[n][..., :weights[n].shape[-1]]
    for n, _ in _SMALL:
        grads[n] = g_small_local[n]

    order = ["norm_mix", "w_in", "conv_w", "conv_b", "fgate_bias", "q_norm", "k_norm", "lb_logits", "hgrn_norm",
             "sgu_norm", "spatial_w", "spatial_b", "w_up", "merge_b", "w_o", "norm_ple", "w_ple_gate", "w_ple_proj"]
    return (loss, grad_x, *[grads[n] for n in order], *[deltas[n] for n in order],
            *[new_m[n] for n in order], *[new_v[n] for n in order])
```

```python
import functools

import jax
import jax.numpy as jnp
from jax import lax
from jax.experimental import pallas as pl
from jax.experimental.pallas import tpu as pltpu

f32 = jnp.float32
bf16 = jnp.bfloat16

D = 1024
W = 256
NH = 4
DH = 64
NBR = 4
PLE = 256
DEPTH = 4
CONV_WIDTH = 3
SGU_CHUNK = 128
GLA_CHUNK = 64
EPS = 1e-6
MASK_VALUE = -1e30
IN_COLS = 7940
NZ = 8064
OFF_CONV = 4096
OFF_HGRN = 5120
OFF_SGU = 6144
OFF_ATT = 6912
OFF_F = 7936
ZT = 1152
NZT = NZ // ZT
EXP_CLAMP = 80.0

ADAM_LR = 0.001
ADAM_B1 = 0.9
ADAM_B2 = 0.999
ADAM_EPS = 1e-08
ADAM_WD = 0.01
ADAM_STEP = 10

N_DEV = 8
AXES = ("x", "y", "c")
VMEM_LIMIT = 56 * 1024 * 1024
HI = lax.Precision.HIGHEST

NT_DIMS = (((1,), (1,)), ((), ()))
TN_DIMS = (((0,), (0,)), ((), ()))


def _pcall(body, **kw):
    return pl.pallas_call(body, **kw)


def _params(*sem):
    return pltpu.CompilerParams(dimension_semantics=sem, vmem_limit_bytes=VMEM_LIMIT)


def _mm(a, b):
    return jnp.dot(a, b, preferred_element_type=f32)


def _mm_nt(a, b):
    return lax.dot_general(a, b, NT_DIMS, preferred_element_type=f32)


def _mm_tn(a, b):
    return lax.dot_general(a, b, TN_DIMS, preferred_element_type=f32)


def _sigmoid(x):
    return 1.0 / (1.0 + jnp.exp(-x))


def _silu(x):
    return x * _sigmoid(x)


def _dsilu(x):
    s = _sigmoid(x)
    return s * (1.0 + x * (1.0 - s))


def _logsigmoid(x):
    return jnp.minimum(x, 0.0) - jnp.log(1.0 + jnp.exp(-jnp.abs(x)))


def _iota2(shape, axis):
    return lax.broadcasted_iota(jnp.int32, shape, axis)


def _group_mean_matrix(n, group):
    shift = group.bit_length() - 1
    r = lax.shift_right_logical(_iota2((n, n), 0), shift)
    c = lax.shift_right_logical(_iota2((n, n), 1), shift)
    return jnp.where(r == c, 1.0 / group, 0.0).astype(f32)


def _group_mean(x, gm):
    return jnp.dot(x, gm, precision=HI, preferred_element_type=f32)


def _lower_tri(n):
    return jnp.where(_iota2((n, n), 0) >= _iota2((n, n), 1), 1.0, 0.0).astype(f32)


def _upper_tri(n):
    return jnp.where(_iota2((n, n), 0) <= _iota2((n, n), 1), 1.0, 0.0).astype(f32)


def _rows3(r0, r1, r2, width):
    row = _iota2((8, width), 0)
    return jnp.where(row == 0, r0, jnp.where(row == 1, r1, jnp.where(row == 2, r2, 0.0)))


def _inproj_fwd(x, g, w, tm):
    T = x.shape[0]

    def body(x_ref, g_ref, w_ref, z_ref, h_ref):
        @pl.when(pl.program_id(1) == 0)
        def _():
            xv = x_ref[...]
            r = lax.rsqrt(jnp.mean(xv * xv, axis=-1, keepdims=True) + EPS)
            h_ref[...] = (xv * r * g_ref[...]).astype(bf16)

        z_ref[...] = _mm(h_ref[...], w_ref[...])

    return _pcall(
        body, name="inproj_fwd", grid=(T // tm, NZT),
        in_specs=[pl.BlockSpec((tm, D), lambda i, j: (i, 0)),
                  pl.BlockSpec((1, D), lambda i, j: (0, 0)),
                  pl.BlockSpec((D, ZT), lambda i, j: (0, j))],
        out_specs=[pl.BlockSpec((tm, ZT), lambda i, j: (i, j)),
                   pl.BlockSpec((tm, D), lambda i, j: (i, 0))],
        out_shape=[jax.ShapeDtypeStruct((T, NZ), f32), jax.ShapeDtypeStruct((T, D), bf16)],
        compiler_params=_params("parallel", "arbitrary"),
    )(x, g, w)


def _inproj_bwd_x(dz, w, x, dx1, g, tm):
    T = x.shape[0]

    def body(dz_ref, w_ref, x_ref, dx1_ref, g_ref, dx_ref, gg_ref, acc):
        i, k = pl.program_id(0), pl.program_id(1)

        @pl.when(k == 0)
        def _():
            acc[...] = jnp.zeros_like(acc)

        @pl.when((i == 0) & (k == 0))
        def _():
            gg_ref[...] = jnp.zeros_like(gg_ref)

        acc[...] += _mm_nt(dz_ref[...], w_ref[...])

        @pl.when(k == NZT - 1)
        def _():
            xv = x_ref[...]
            r = lax.rsqrt(jnp.mean(xv * xv, axis=-1, keepdims=True) + EPS)
            dh = acc[...]
            gg_ref[...] += jnp.sum(dh * xv * r, axis=0, keepdims=True)
            u = dh * g_ref[...]
            dx_ref[...] = dx1_ref[...] + r * u - xv * (r * r * r) * jnp.mean(u * xv, axis=-1, keepdims=True)

    return _pcall(
        body, name="inproj_bwd_x", grid=(T // tm, NZT),
        in_specs=[pl.BlockSpec((tm, ZT), lambda i, k: (i, k)),
                  pl.BlockSpec((D, ZT), lambda i, k: (0, k)),
                  pl.BlockSpec((tm, D), lambda i, k: (i, 0)),
                  pl.BlockSpec((tm, D), lambda i, k: (i, 0)),
                  pl.BlockSpec((1, D), lambda i, k: (0, 0))],
        out_specs=[pl.BlockSpec((tm, D), lambda i, k: (i, 0)),
                   pl.BlockSpec((1, D), lambda i, k: (0, 0))],
        out_shape=[jax.ShapeDtypeStruct((T, D), f32), jax.ShapeDtypeStruct((1, D), f32)],
        scratch_shapes=[pltpu.VMEM((tm, D), f32)],
        compiler_params=_params("arbitrary", "arbitrary"),
    )(dz, w, x, dx1, g)


def _inproj_bwd_w(h, dz, tm):
    T = h.shape[0]

    def body(h_ref, dz_ref, gw_ref):
        @pl.when(pl.program_id(1) == 0)
        def _():
            gw_ref[...] = jnp.zeros_like(gw_ref)

        gw_ref[...] += _mm_tn(h_ref[...], dz_ref[...])

    return _pcall(
        body, name="inproj_bwd_w", grid=(NZT, T // tm),
        in_specs=[pl.BlockSpec((tm, D), lambda j, i: (i, 0)),
                  pl.BlockSpec((tm, ZT), lambda j, i: (i, j))],
        out_specs=pl.BlockSpec((D, ZT), lambda j, i: (0, j)),
        out_shape=jax.ShapeDtypeStruct((D, NZ), f32),
        compiler_params=_params("parallel", "arbitrary"),
    )(h, dz)


def _zblock(tm, col256):
    return pl.BlockSpec((tm, W), lambda i, c=col256: (i, c))


def _conv_taps(zc, halo, cw_ref, n):
    ext = jnp.concatenate([halo, zc], axis=0)
    z1 = pltpu.roll(ext, 1, 0)[8:]
    z2 = pltpu.roll(ext, 2, 0)[8:]
    return z1, z2


def _conv_fwd(z, cw, cb, tm):
    T = z.shape[0]
    c0 = OFF_CONV // W
    hb = tm // 8

    def body(ax_ref, ab_ref, ac_ref, ag_ref, hx_ref, hc_ref, cw_ref, cb_ref, y_ref):
        i = pl.program_id(0)
        zc = ac_ref[...] * ax_ref[...]
        halo = jnp.where(i > 0, hc_ref[...] * hx_ref[...], 0.0)
        z1, z2 = _conv_taps(zc, halo, cw_ref, tm)
        y = cw_ref[2:3, :] * zc + cw_ref[1:2, :] * z1 + cw_ref[0:1, :] * z2
        ya = ab_ref[...] * (y + cb_ref[...])
        y_ref[...] = (ya * _silu(ag_ref[...])).astype(bf16)

    halo_spec = lambda col: pl.BlockSpec((8, W), lambda i, c=col: (jnp.maximum(i * hb - 1, 0), c))
    return _pcall(
        body, name="conv_fwd", grid=(T // tm,),
        in_specs=[_zblock(tm, c0), _zblock(tm, c0 + 1), _zblock(tm, c0 + 2), _zblock(tm, c0 + 3),
                  halo_spec(c0), halo_spec(c0 + 2),
                  pl.BlockSpec((CONV_WIDTH, W), lambda i: (0, 0)),
                  pl.BlockSpec((1, W), lambda i: (0, 0))],
        out_specs=pl.BlockSpec((tm, W), lambda i: (i, 0)),
        out_shape=jax.ShapeDtypeStruct((T, W), bf16),
        compiler_params=_params("parallel"),
    )(z, z, z, z, z, z, cw, cb)


def _conv_bwd(z, dy, cw, cb, tm):
    T = z.shape[0]
    c0 = OFF_CONV // W
    hb = tm // 8
    nt = T // tm

    def body(ax_ref, ab_ref, ac_ref, ag_ref, hx_ref, hc_ref, nb_ref, ng_ref, dy_ref, ndy_ref,
             cw_ref, cb_ref, dz_ref, gcw_ref, gcb_ref):
        i = pl.program_id(0)

        @pl.when(i == 0)
        def _():
            gcw_ref[...] = jnp.zeros_like(gcw_ref)
            gcb_ref[...] = jnp.zeros_like(gcb_ref)

        ax, ab, ac, ag = ax_ref[...], ab_ref[...], ac_ref[...], ag_ref[...]
        w0, w1, w2 = cw_ref[0:1, :], cw_ref[1:2, :], cw_ref[2:3, :]
        zc = ac * ax
        halo = jnp.where(i > 0, hc_ref[...] * hx_ref[...], 0.0)
        z1, z2 = _conv_taps(zc, halo, cw_ref, tm)
        yb = w2 * zc + w1 * z1 + w0 * z2 + cb_ref[...]
        ya = ab * yb
        dyg = dy_ref[...]
        dag = dyg * ya * _dsilu(ag)
        dya = dyg * _silu(ag)
        dab = dya * yb
        dyc = dya * ab
        nxt = jnp.where(i < nt - 1, ndy_ref[...] * _silu(ng_ref[...]) * nb_ref[...], 0.0)
        ext = jnp.concatenate([dyc, nxt], axis=0)
        d1 = pltpu.roll(ext, tm + 8 - 1, 0)[:tm]
        d2 = pltpu.roll(ext, tm + 8 - 2, 0)[:tm]
        dzc = w2 * dyc + w1 * d1 + w0 * d2
        dz_ref[:, 0:W] = (dzc * ac).astype(bf16)
        dz_ref[:, W:2 * W] = dab.astype(bf16)
        dz_ref[:, 2 * W:3 * W] = (dzc * ax).astype(bf16)
        dz_ref[:, 3 * W:4 * W] = dag.astype(bf16)
        gcb_ref[...] += jnp.sum(dyc, axis=0, keepdims=True)
        gcw_ref[...] += _rows3(jnp.sum(dyc * z2, axis=0, keepdims=True),
                               jnp.sum(dyc * z1, axis=0, keepdims=True),
                               jnp.sum(dyc * zc, axis=0, keepdims=True), W)

    prev_spec = lambda col: pl.BlockSpec((8, W), lambda i, c=col: (jnp.maximum(i * hb - 1, 0), c))
    next_z = lambda col: pl.BlockSpec((8, W), lambda i, c=col: (jnp.minimum((i + 1) * hb, T // 8 - 1), c))
    next_dy = pl.BlockSpec((8, W), lambda i: (jnp.minimum((i + 1) * hb, T // 8 - 1), 0))
    return _pcall(
        body, name="conv_bwd", grid=(nt,),
        in_specs=[_zblock(tm, c0), _zblock(tm, c0 + 1), _zblock(tm, c0 + 2), _zblock(tm, c0 + 3),
                  prev_spec(c0), prev_spec(c0 + 2), next_z(c0 + 1), next_z(c0 + 3),
                  pl.BlockSpec((tm, W), lambda i: (i, 0)), next_dy,
                  pl.BlockSpec((CONV_WIDTH, W), lambda i: (0, 0)),
                  pl.BlockSpec((1, W), lambda i: (0, 0))],
        out_specs=[pl.BlockSpec((tm, 4 * W), lambda i: (i, 0)),
                   pl.BlockSpec((8, W), lambda i: (0, 0)),
                   pl.BlockSpec((1, W), lambda i: (0, 0))],
        out_shape=[jax.ShapeDtypeStruct((T, 4 * W), bf16), jax.ShapeDtypeStruct((8, W), f32),
                   jax.ShapeDtypeStruct((1, W), f32)],
        compiler_params=_params("arbitrary"),
    )(z, z, z, z, z, z, z, z, dy, dy, cw, cb)


def _sgu_core(dv_ref, gv_ref, sw_ref, sbe_ref, s_scr, tm):
    v = dv_ref[...]
    gm = _group_mean_matrix(W, DH)
    rv = lax.rsqrt(_group_mean(v * v, gm) + EPS)
    vh = v * rv
    vnb = (vh * gv_ref[...]).astype(bf16)
    causal = _iota2((SGU_CHUNK, SGU_CHUNK), 0) >= _iota2((SGU_CHUNK, SGU_CHUNK), 1)
    wgs = [jnp.where(causal, sw_ref[g], 0.0).astype(bf16) for g in range(NH)]
    for c in range(tm // SGU_CHUNK):
        rows = slice(c * SGU_CHUNK, (c + 1) * SGU_CHUNK)
        for g in range(NH):
            cols = slice(g * DH, (g + 1) * DH)
            s_scr[rows, cols] = _mm(wgs[g], vnb[rows, cols])
    sb = sbe_ref[...]
    s = s_scr[...] + jnp.concatenate([sb] * (tm // SGU_CHUNK), axis=0)
    return v, rv, vh, vnb, wgs, causal, gm, s


def _sgu_fwd(z, gv, sw, sbe, tm):
    T = z.shape[0]
    c0 = OFF_SGU // W

    def body(du_ref, dv_ref, dg_ref, gv_ref, sw_ref, sbe_ref, y_ref, s_scr):
        s = _sgu_core(dv_ref, gv_ref, sw_ref, sbe_ref, s_scr, tm)[-1]
        y_ref[...] = ((du_ref[...] * s) * _silu(dg_ref[...])).astype(bf16)

    return _pcall(
        body, name="sgu_fwd", grid=(T // tm,),
        in_specs=[_zblock(tm, c0), _zblock(tm, c0 + 1), _zblock(tm, c0 + 2),
                  pl.BlockSpec((1, W), lambda i: (0, 0)),
                  pl.BlockSpec((NH, SGU_CHUNK, SGU_CHUNK), lambda i: (0, 0, 0)),
                  pl.BlockSpec((SGU_CHUNK, W), lambda i: (0, 0))],
        out_specs=pl.BlockSpec((tm, W), lambda i: (i, 0)),
        out_shape=jax.ShapeDtypeStruct((T, W), bf16),
        scratch_shapes=[pltpu.VMEM((tm, W), f32)],
        compiler_params=_params("parallel"),
    )(z, z, z, gv, sw, sbe)


def _sgu_bwd(z, dy, gv, sw, sbe, tm):
    T = z.shape[0]
    c0 = OFF_SGU // W
    nt = T // tm

    def body(du_ref, dv_ref, dg_ref, dy_ref, gv_ref, sw_ref, sbe_ref,
             dz_ref, gsw_ref, gsb_ref, ggv_ref, s_scr, dvn_scr, sb_acc):
        i = pl.program_id(0)

        @pl.when(i == 0)
        def _():
            gsw_ref[...] = jnp.zeros_like(gsw_ref)
            ggv_ref[...] = jnp.zeros_like(ggv_ref)
            sb_acc[...] = jnp.zeros_like(sb_acc)

        v, rv, vh, vnb, wgs, causal, gm, s = _sgu_core(dv_ref, gv_ref, sw_ref, sbe_ref, s_scr, tm)
        du, dg, dyv = du_ref[...], dg_ref[...], dy_ref[...]
        ddg = dyv * (du * s) * _dsilu(dg)
        t = dyv * _silu(dg)
        ddu = t * s
        ds = t * du
        dsb = ds.astype(bf16)
        acc = sb_acc[...]
        for c in range(tm // SGU_CHUNK):
            rows = slice(c * SGU_CHUNK, (c + 1) * SGU_CHUNK)
            acc = acc + ds[rows, :]
            for g in range(NH):
                cols = slice(g * DH, (g + 1) * DH)
                gsw_ref[g] += jnp.where(causal, _mm_nt(dsb[rows, cols], vnb[rows, cols]), 0.0)
                dvn_scr[rows, cols] = _mm_tn(wgs[g], dsb[rows, cols])
        sb_acc[...] = acc
        dvn = dvn_scr[...]
        ggv_ref[...] += jnp.sum(dvn * vh, axis=0, keepdims=True)
        u = dvn * gv_ref[...]
        ddv = rv * u - v * (rv * rv * rv) * _group_mean(u * v, gm)
        dz_ref[:, 0:W] = ddu.astype(bf16)
        dz_ref[:, W:2 * W] = ddv.astype(bf16)
        dz_ref[:, 2 * W:3 * W] = ddg.astype(bf16)

        @pl.when(i == nt - 1)
        def _():
            gsb_ref[...] = _group_mean(sb_acc[...], gm) * float(DH)

    return _pcall(
        body, name="sgu_bwd", grid=(nt,),
        in_specs=[_zblock(tm, c0), _zblock(tm, c0 + 1), _zblock(tm, c0 + 2),
                  pl.BlockSpec((tm, W), lambda i: (i, 0)),
                  pl.BlockSpec((1, W), lambda i: (0, 0)),
                  pl.BlockSpec((NH, SGU_CHUNK, SGU_CHUNK), lambda i: (0, 0, 0)),
                  pl.BlockSpec((SGU_CHUNK, W), lambda i: (0, 0))],
        out_specs=[pl.BlockSpec((tm, 3 * W), lambda i: (i, 0)),
                   pl.BlockSpec((NH, SGU_CHUNK, SGU_CHUNK), lambda i: (0, 0, 0)),
                   pl.BlockSpec((SGU_CHUNK, W), lambda i: (0, 0)),
                   pl.BlockSpec((1, W), lambda i: (0, 0))],
        out_shape=[jax.ShapeDtypeStruct((T, 3 * W), bf16),
                   jax.ShapeDtypeStruct((NH, SGU_CHUNK, SGU_CHUNK), f32),
                   jax.ShapeDtypeStruct((SGU_CHUNK, W), f32),
                   jax.ShapeDtypeStruct((1, W), f32)],
        scratch_shapes=[pltpu.VMEM((tm, W), f32), pltpu.VMEM((tm, W), f32), pltpu.VMEM((SGU_CHUNK, W), f32)],
        compiler_params=_params("arbitrary"),
    )(z, z, z, dy, gv, sw, sbe)


def _hgrn_gates(cq_ref, cf_ref, lb_ref):
    q = _silu(cq_ref[...])
    sig = _sigmoid(cf_ref[...])
    lb = lb_ref[...]
    g = lb + (1.0 - lb) * sig
    return q, sig, g, jnp.log(g), (1.0 - lb) * (1.0 - sig)


def _hgrn_chunk_terms(lgc, qc, kc):
    C = GLA_CHUNK
    b = jnp.dot(_lower_tri(C), lgc, precision=HI, preferred_element_type=f32)
    bl = jnp.sum(lgc, axis=0, keepdims=True)
    mid = jnp.sum(jnp.where(_iota2((C, W), 0) <= C // 2, lgc, 0.0), axis=0, keepdims=True)
    eb = jnp.exp(b)
    em = jnp.exp(jnp.minimum(b - mid, EXP_CLAMP))
    emi = jnp.exp(jnp.minimum(mid - b, EXP_CLAMP))
    ek = jnp.exp(bl - b)
    return dict(eb=eb, em=em, emi=emi, ek=ek, ebl=jnp.exp(bl),
                qe=qc * eb, qm=qc * em, km=kc * emi, kd=kc * ek)


def _hgrn_fwd(z, lb, gain, tm):
    T = z.shape[0]
    c0 = OFF_HGRN // W
    C = GLA_CHUNK
    ncp = tm // C

    def body(cq_ref, cf_ref, ci_ref, cg_ref, lb_ref, gn_ref, y_ref, o_ref, st_ref, state, o_scr):
        @pl.when(pl.program_id(0) == 0)
        def _():
            state[...] = jnp.zeros_like(state)

        q, sig, g, lg, kf = _hgrn_gates(cq_ref, cf_ref, lb_ref)
        v = ci_ref[...]
        causal = _iota2((C, C), 0) >= _iota2((C, C), 1)
        for c in range(ncp):
            rows = slice(c * C, (c + 1) * C)
            tr = _hgrn_chunk_terms(lg[rows], q[rows], kf[rows])
            vb = v[rows].astype(bf16)
            qmb, kmb, qeb, kdb = (tr[n].astype(bf16) for n in ("qm", "km", "qe", "kd"))
            for h in range(NH):
                cols = slice(h * DH, (h + 1) * DH)
                hr = slice(h * DH, (h + 1) * DH)
                st = state[hr, :]
                st_ref[c, hr, :] = st
                p = jnp.where(causal, _mm_nt(qmb[:, cols], kmb[:, cols]), 0.0)
                o_scr[rows, cols] = _mm(p.astype(bf16), vb[:, cols]) + _mm_nt(qeb[:, cols], st.astype(bf16))
                state[hr, :] = st * tr["ebl"][:, cols] + _mm_tn(vb[:, cols], kdb[:, cols])
        o = o_scr[...]
        o_ref[...] = o
        gm = _group_mean_matrix(W, DH)
        r = lax.rsqrt(_group_mean(o * o, gm) + EPS)
        y_ref[...] = ((o * r * gn_ref[...]) * _silu(cg_ref[...])).astype(bf16)

    return _pcall(
        body, name="hgrn_fwd", grid=(T // tm,),
        in_specs=[_zblock(tm, c0), _zblock(tm, c0 + 1), _zblock(tm, c0 + 2), _zblock(tm, c0 + 3),
                  pl.BlockSpec((1, W), lambda i: (0, 0)), pl.BlockSpec((1, W), lambda i: (0, 0))],
        out_specs=[pl.BlockSpec((tm, W), lambda i: (i, 0)),
                   pl.BlockSpec((tm, W), lambda i: (i, 0)),
                   pl.BlockSpec((ncp, W, DH), lambda i: (i, 0, 0))],
        out_shape=[jax.ShapeDtypeStruct((T, W), bf16), jax.ShapeDtypeStruct((T, W), f32),
                   jax.ShapeDtypeStruct((T // C, W, DH), f32)],
        scratch_shapes=[pltpu.VMEM((W, DH), f32), pltpu.VMEM((tm, W), f32)],
        compiler_params=_params("arbitrary"),
    )(z, z, z, z, lb, gain)


def _hgrn_bwd(z, lb, gain, o_pre, states, dy, tm):
    T = z.shape[0]
    c0 = OFF_HGRN // W
    C = GLA_CHUNK
    ncp = tm // C
    nt = T // tm

    def body(cq_ref, cf_ref, ci_ref, cg_ref, lb_ref, gn_ref, o_ref, st_ref, dy_ref,
             dz_ref, ggn_ref, glb_ref, dstate, dq_s, dk_s, dv_s, db_s):
        @pl.when(pl.program_id(0) == 0)
        def _():
            dstate[...] = jnp.zeros_like(dstate)
            ggn_ref[...] = jnp.zeros_like(ggn_ref)
            glb_ref[...] = jnp.zeros_like(glb_ref)

        cq, cg = cq_ref[...], cg_ref[...]
        q, sig, g, lg, kf = _hgrn_gates(cq_ref, cf_ref, lb_ref)
        lb = lb_ref[...]
        v = ci_ref[...]
        o = o_ref[...]
        gm = _group_mean_matrix(W, DH)
        r = lax.rsqrt(_group_mean(o * o, gm) + EPS)
        oh = o * r
        gn = gn_ref[...]
        dyv = dy_ref[...]
        dcg = dyv * (oh * gn) * _dsilu(cg)
        don = dyv * _silu(cg)
        ggn_ref[...] += jnp.sum(don * oh, axis=0, keepdims=True)
        u = don * gn
        do = r * u - o * (r * r * r) * _group_mean(u * o, gm)

        causal = _iota2((C, C), 0) >= _iota2((C, C), 1)
        last_row = _iota2((C, DH), 0) == C - 1
        for c in reversed(range(ncp)):
            rows = slice(c * C, (c + 1) * C)
            tr = _hgrn_chunk_terms(lg[rows], q[rows], kf[rows])
            vb = v[rows].astype(bf16)
            dob = do[rows].astype(bf16)
            qmb, kmb, qeb, kdb = (tr[n].astype(bf16) for n in ("qm", "km", "qe", "kd"))
            for h in range(NH):
                cols = slice(h * DH, (h + 1) * DH)
                hr = slice(h * DH, (h + 1) * DH)
                st0 = st_ref[c, hr, :]
                dst = dstate[hr, :]
                dstb = dst.astype(bf16)
                doh = dob[:, cols]
                p = jnp.where(causal, _mm_nt(qmb[:, cols], kmb[:, cols]), 0.0)
                dp = jnp.where(causal, _mm_nt(doh, vb[:, cols]), 0.0)
                dpb = dp.astype(bf16)
                dvh = _mm_tn(p.astype(bf16), doh) + _mm_nt(kdb[:, cols], dstb)
                dqm = _mm(dpb, kmb[:, cols])
                dkm = _mm_tn(dpb, qmb[:, cols])
                dqe = _mm(doh, st0.astype(bf16))
                dkd = _mm(vb[:, cols], dstb)
                ebl = tr["ebl"][:, cols]
                dstate[hr, :] = dst * ebl + _mm_tn(doh, qeb[:, cols])
                qm, km, qe, kd = (a[:, cols].astype(f32) for a in (qmb, kmb, qeb, kdb))
                kterm = dkd * kd
                dbh = dqm * qm - dkm * km + dqe * qe - kterm
                extra = jnp.sum(kterm, axis=0, keepdims=True) + ebl * jnp.sum(dst * st0, axis=0, keepdims=True)
                dbh = dbh + jnp.where(last_row, extra, 0.0)
                dq_s[rows, cols] = dqm * tr["em"][:, cols] + dqe * tr["eb"][:, cols]
                dk_s[rows, cols] = dkm * tr["emi"][:, cols] + dkd * tr["ek"][:, cols]
                dv_s[rows, cols] = dvh
                db_s[rows, cols] = dbh
            db_s[rows, :] = jnp.dot(_upper_tri(C), db_s[rows, :], precision=HI, preferred_element_type=f32)
        dlg = db_s[...]
        dk = dk_s[...]
        dsig = sig * (1.0 - sig)
        one_lb = 1.0 - lb
        dcf = (dlg / g - dk) * one_lb * dsig
        glb_ref[...] += jnp.sum((dlg / g - dk) * (1.0 - sig), axis=0, keepdims=True)
        dz_ref[:, 0:W] = (dq_s[...] * _dsilu(cq)).astype(bf16)
        dz_ref[:, W:2 * W] = dcf.astype(bf16)
        dz_ref[:, 2 * W:3 * W] = dv_s[...].astype(bf16)
        dz_ref[:, 3 * W:4 * W] = dcg.astype(bf16)

    rev = lambda i: nt - 1 - i
    zb = lambda col: pl.BlockSpec((tm, W), lambda i, c=col: (rev(i), c))
    return _pcall(
        body, name="hgrn_bwd", grid=(nt,),
        in_specs=[zb(c0), zb(c0 + 1), zb(c0 + 2), zb(c0 + 3),
                  pl.BlockSpec((1, W), lambda i: (0, 0)), pl.BlockSpec((1, W), lambda i: (0, 0)),
                  pl.BlockSpec((tm, W), lambda i: (rev(i), 0)),
                  pl.BlockSpec((ncp, W, DH), lambda i: (rev(i), 0, 0)),
                  pl.BlockSpec((tm, W), lambda i: (rev(i), 0))],
        out_specs=[pl.BlockSpec((tm, 4 * W), lambda i: (rev(i), 0)),
                   pl.BlockSpec((1, W), lambda i: (0, 0)),
                   pl.BlockSpec((1, W), lambda i: (0, 0))],
        out_shape=[jax.ShapeDtypeStruct((T, 4 * W), bf16), jax.ShapeDtypeStruct((1, W), f32),
                   jax.ShapeDtypeStruct((1, W), f32)],
        scratch_shapes=[pltpu.VMEM((W, DH), f32)] + [pltpu.VMEM((tm, W), f32)] * 4,
        compiler_params=_params("arbitrary"),
    )(z, z, z, z, lb, gain, o_pre, states, dy)


def _attn_prep(z, fbias, gq, gk, tm):
    T = z.shape[0]
    c0 = OFF_ATT // W

    def body(q_ref, k_ref, v_ref, f_ref, fb_ref, gq_ref, gk_ref, qt_ref, kt_ref, vt_ref, kh_ref, vh_ref, cum_ref,
             carry):
        @pl.when(pl.program_id(0) == 0)
        def _():
            carry[...] = jnp.zeros_like(carry)

        gm = _group_mean_matrix(W, DH)
        q, k, v = q_ref[...], k_ref[...], v_ref[...]
        qs = q * lax.rsqrt(_group_mean(q * q, gm) + EPS) * (gq_ref[...] * (DH ** -0.5))
        kn = k * lax.rsqrt(_group_mean(k * k, gm) + EPS) * gk_ref[...]
        qt_ref[...] = qs.T.astype(bf16)
        kt_ref[...] = kn.T.astype(bf16)
        vt_ref[...] = v.T.astype(bf16)
        for h in range(NH):
            cols = slice(h * DH, (h + 1) * DH)
            kh_ref[h] = kn[:, cols].astype(bf16)
            vh_ref[h] = v[:, cols].astype(bf16)
        ls = _logsigmoid(f_ref[...] + fb_ref[...])
        cum_ref[...] = jnp.dot(_lower_tri(tm), ls, precision=HI, preferred_element_type=f32) + carry[...]
        carry[...] += jnp.sum(ls, axis=0, keepdims=True)

    hspec = pl.BlockSpec((NH, tm, DH), lambda i: (0, i, 0))
    tspec = pl.BlockSpec((W, tm), lambda i: (0, i))
    return _pcall(
        body, name="attn_prep", grid=(T // tm,),
        in_specs=[_zblock(tm, c0), _zblock(tm, c0 + 1), _zblock(tm, c0 + 2),
                  pl.BlockSpec((tm, 128), lambda i: (i, OFF_F // 128)),
                  pl.BlockSpec((1, 128), lambda i: (0, 0)),
                  pl.BlockSpec((1, W), lambda i: (0, 0)), pl.BlockSpec((1, W), lambda i: (0, 0))],
        out_specs=[tspec, tspec, tspec, hspec, hspec, pl.BlockSpec((tm, 128), lambda i: (i, 0))],
        out_shape=[jax.ShapeDtypeStruct((W, T), bf16)] * 3 + [jax.ShapeDtypeStruct((NH, T, DH), bf16)] * 2
        + [jax.ShapeDtypeStruct((T, 128), f32)],
        scratch_shapes=[pltpu.VMEM((1, 128), f32)],
        compiler_params=_params("arbitrary"),
    )(z, z, z, z, fbias, gq, gk)


HP = 2


def _causal_pairs(nq, key_major):
    if key_major:
        pairs = [(qi, ki) for ki in range(nq) for qi in range(ki, nq)]
    else:
        pairs = [(qi, ki) for qi in range(nq) for ki in range(qi + 1)]
    return (jnp.asarray([p[0] for p in pairs], jnp.int32), jnp.asarray([p[1] for p in pairs], jnp.int32))


def _head_rows(rows, n):
    return jnp.concatenate([jnp.broadcast_to(r, (DH, n)) for r in rows], axis=0)


def _attn_fwd(qt, kh, vt, crow, ccol, bq):
    T = qt.shape[1]
    nq = T // bq
    bk = bq
    qs, ks = _causal_pairs(nq, key_major=False)
    BW = HP * DH

    def body(qs_ref, ks_ref, qt_ref, k_ref, vt_ref, cr_ref, cc_ref, o_ref, lse_ref, m_s, l_s, acc_s):
        i = pl.program_id(1)
        qi, ki = qs_ref[i], ks_ref[i]

        @pl.when(ki == 0)
        def _():
            m_s[...] = jnp.full_like(m_s, MASK_VALUE)
            l_s[...] = jnp.zeros_like(l_s)
            acc_s[...] = jnp.zeros_like(acc_s)

        def step(diagonal):
            for h in range(HP):
                rows = slice(h * DH, (h + 1) * DH)
                s = _mm(k_ref[h], qt_ref[rows, :]) + cr_ref[h] - cc_ref[h]
                if diagonal:
                    s = jnp.where(_iota2((bk, bq), 0) <= _iota2((bk, bq), 1), s, MASK_VALUE)
                m_old = m_s[h]
                m_new = jnp.maximum(m_old, jnp.max(s, axis=0, keepdims=True))
                p = jnp.exp(s - m_new)
                alpha = jnp.exp(m_old - m_new)
                l_s[h] = alpha * l_s[h] + jnp.sum(p, axis=0, keepdims=True)
                acc_s[rows, :] = alpha * acc_s[rows, :] + _mm(vt_ref[rows, :], p.astype(bf16))
                m_s[h] = m_new

        @pl.when(ki < qi)
        def _():
            step(False)

        @pl.when(ki == qi)
        def _():
            step(True)
            o_ref[...] = (acc_s[...] / _head_rows([l_s[h] for h in range(HP)], bq)).T
            for h in range(HP):
                lse_ref[h] = m_s[h] + jnp.log(l_s[h])

    qcol = lambda hp, i, qs, ks: (hp, qs[i])
    kcol = lambda hp, i, qs, ks: (hp, ks[i])
    qrow = lambda hp, i, qs, ks: (hp, 0, qs[i])
    return _pcall(
        body, name="attn_fwd",
        grid_spec=pltpu.PrefetchScalarGridSpec(
            num_scalar_prefetch=2, grid=(NH // HP, qs.shape[0]),
            in_specs=[pl.BlockSpec((BW, bq), qcol),
                      pl.BlockSpec((HP, bk, DH), lambda hp, i, qs, ks: (hp, ks[i], 0)),
                      pl.BlockSpec((BW, bk), kcol),
                      pl.BlockSpec((HP, 1, bq), qrow),
                      pl.BlockSpec((HP, bk, 1), lambda hp, i, qs, ks: (hp, ks[i], 0))],
            out_specs=[pl.BlockSpec((bq, BW), lambda hp, i, qs, ks: (qs[i], hp)),
                       pl.BlockSpec((HP, 1, bq), qrow)],
            scratch_shapes=[pltpu.VMEM((HP, 1, bq), f32), pltpu.VMEM((HP, 1, bq), f32),
                            pltpu.VMEM((BW, bq), f32)]),
        out_shape=[jax.ShapeDtypeStruct((T, W), f32), jax.ShapeDtypeStruct((NH, 1, T), f32)],
        compiler_params=_params("parallel", "arbitrary"),
    )(qs, ks, qt, kh, vt, crow, ccol)


def _attn_bwd_prep(dy, oh, z, tm):
    T = dy.shape[0]
    cg = OFF_ATT // W + 3

    def body(dy_ref, o_ref, g_ref, dot_ref, dl_ref):
        do = (dy_ref[...] * _silu(g_ref[...])).astype(bf16)
        dot_ref[...] = do.astype(f32).T.astype(bf16)
        prod = (do.astype(f32) * o_ref[...]).T
        for h in range(NH):
            dl_ref[h] = jnp.sum(prod[h * DH:(h + 1) * DH, :], axis=0, keepdims=True)

    return _pcall(
        body, name="attn_bwd_prep", grid=(T // tm,),
        in_specs=[pl.BlockSpec((tm, W), lambda i: (i, 0)),
                  pl.BlockSpec((tm, W), lambda i: (i, 0)),
                  _zblock(tm, cg)],
        out_specs=[pl.BlockSpec((W, tm), lambda i: (0, i)),
                   pl.BlockSpec((NH, 1, tm), lambda i: (0, 0, i))],
        out_shape=[jax.ShapeDtypeStruct((W, T), bf16), jax.ShapeDtypeStruct((NH, 1, T), f32)],
        compiler_params=_params("parallel"),
    )(dy, oh, z)


def _attn_bwd(qt, kt, kh, vh, crow, ccol, dot, lse, delta, bq):
    T = qt.shape[1]
    nq = T // bq
    bk = bq
    qs, ks = _causal_pairs(nq, key_major=True)
    BW = HP * DH

    def body(qs_ref, ks_ref, qt_ref, kt_ref, k_ref, v_ref, cr_ref, cc_ref, dot_ref, lse_ref, dl_ref,
             dq_ref, dk_ref, dv_ref, dck_ref, dcq_ref, dq_s, dk_s, dv_s, dck_s):
        i = pl.program_id(1)
        qi, ki = qs_ref[i], ks_ref[i]

        @pl.when(i == 0)
        def _():
            dq_s[...] = jnp.zeros_like(dq_s)
            dcq_ref[...] = jnp.zeros_like(dcq_ref)

        @pl.when(qi == ki)
        def _():
            dk_s[...] = jnp.zeros_like(dk_s)
            dv_s[...] = jnp.zeros_like(dv_s)
            dck_s[...] = jnp.zeros_like(dck_s)

        def step(diagonal):
            colsums = []
            for h in range(HP):
                rows = slice(h * DH, (h + 1) * DH)
                qth, doth = qt_ref[rows, :], dot_ref[rows, :]
                p = jnp.exp(_mm(k_ref[h], qth) + (cr_ref[h] - lse_ref[h]) - cc_ref[h])
                if diagonal:
                    p = jnp.where(_iota2((bk, bq), 0) <= _iota2((bk, bq), 1), p, 0.0)
                dv_s[rows, :] += _mm_nt(doth, p.astype(bf16))
                ds = p * (_mm(v_ref[h], doth) - dl_ref[h])
                dsb = ds.astype(bf16)
                dk_s[rows, :] += _mm_nt(qth, dsb)
                dq_s[qi, rows, :] += _mm(kt_ref[rows, :], dsb)
                part = ds[:, 0:128]
                for c in range(1, bq // 128):
                    part = part + ds[:, c * 128:(c + 1) * 128]
                dck_s[h] += part
                colsums.append(jnp.sum(ds, axis=0, keepdims=True))
            dcq_ref[qi] += _stack_rows(colsums, bq)

        @pl.when(qi > ki)
        def _():
            step(False)

        @pl.when(qi == ki)
        def _():
            step(True)

        @pl.when(qi == nq - 1)
        def _():
            dk_ref[...] = dk_s[...].T
            dv_ref[...] = dv_s[...].T
            for h in range(HP):
                dck_ref[h] = -jnp.sum(dck_s[h], axis=1, keepdims=True)

        @pl.when(i == qs.shape[0] - 1)
        def _():
            for qb in range(nq):
                dq_ref[qb * bq:(qb + 1) * bq, :] = dq_s[qb].T

    qcol = lambda hp, i, qs, ks: (hp, qs[i])
    kcol = lambda hp, i, qs, ks: (hp, ks[i])
    qrow = lambda hp, i, qs, ks: (hp, 0, qs[i])
    kh_spec = pl.BlockSpec((HP, bk, DH), lambda hp, i, qs, ks: (hp, ks[i], 0))
    return _pcall(
        body, name="attn_bwd",
        grid_spec=pltpu.PrefetchScalarGridSpec(
            num_scalar_prefetch=2, grid=(NH // HP, qs.shape[0]),
            in_specs=[pl.BlockSpec((BW, bq), qcol), pl.BlockSpec((BW, bk), kcol), kh_spec, kh_spec,
                      pl.BlockSpec((HP, 1, bq), qrow),
                      pl.BlockSpec((HP, bk, 1), lambda hp, i, qs, ks: (hp, ks[i], 0)),
                      pl.BlockSpec((BW, bq), qcol), pl.BlockSpec((HP, 1, bq), qrow), pl.BlockSpec((HP, 1, bq), qrow)],
            out_specs=[pl.BlockSpec((T, BW), lambda hp, i, qs, ks: (0, hp)),
                       pl.BlockSpec((bk, BW), lambda hp, i, qs, ks: (ks[i], hp)),
                       pl.BlockSpec((bk, BW), lambda hp, i, qs, ks: (ks[i], hp)),
                       pl.BlockSpec((HP, bk, 1), lambda hp, i, qs, ks: (hp, ks[i], 0)),
                       pl.BlockSpec((None, nq, 8, bq), lambda hp, i, qs, ks: (hp, 0, 0, 0))],
            scratch_shapes=[pltpu.VMEM((nq, BW, bq), f32), pltpu.VMEM((BW, bk), f32), pltpu.VMEM((BW, bk), f32),
                            pltpu.VMEM((HP, bk, 128), f32)]),
        out_shape=[jax.ShapeDtypeStruct((T, W), f32)] * 3 + [jax.ShapeDtypeStruct((NH, T, 1), f32),
                                                             jax.ShapeDtypeStruct((NH // HP, nq, 8, bq), f32)],
        compiler_params=_params("parallel", "arbitrary"),
    )(qs, ks, qt, kt, kh, vh, crow, ccol, dot, lse, delta)


def _attn_post(z, dy, oh, dqh, dkh, dvh, dck, dcq, fbias, gq, gk, tm):
    T = z.shape[0]
    c0 = OFF_ATT // W
    nt = T // tm

    def body(q_ref, k_ref, g_ref, f_ref, dy_ref, o_ref, dq_ref, dk_ref, dv_ref, dck_ref, dcq_ref, fb_ref, gq_ref,
             gk_ref, dz_ref, ggq_ref, ggk_ref, gfb_ref, carry):
        @pl.when(pl.program_id(0) == 0)
        def _():
            carry[...] = jnp.zeros_like(carry)
            ggq_ref[...] = jnp.zeros_like(ggq_ref)
            ggk_ref[...] = jnp.zeros_like(ggk_ref)
            gfb_ref[...] = jnp.zeros_like(gfb_ref)

        gm = _group_mean_matrix(W, DH)
        hs = jnp.where((_iota2((W, W), 0) & (DH - 1)) == (_iota2((W, W), 1) & (DH - 1)), 1.0, 0.0).astype(f32)

        def norm_bwd(x, dn, gain):
            r = lax.rsqrt(_group_mean(x * x, gm) + EPS)
            gg = jnp.sum(dn * x * r, axis=0, keepdims=True)
            u = dn * gain
            return r * u - x * (r * r * r) * _group_mean(u * x, gm), gg

        q, k, gate = q_ref[...], k_ref[...], g_ref[...]
        dq, ggq = norm_bwd(q, dq_ref[...] * (DH ** -0.5), gq_ref[...])
        dk, ggk = norm_bwd(k, dk_ref[...], gk_ref[...])
        ggq_ref[...] += jnp.dot(jnp.broadcast_to(ggq, (8, W)), hs, precision=HI, preferred_element_type=f32)[0:1]
        ggk_ref[...] += jnp.dot(jnp.broadcast_to(ggk, (8, W)), hs, precision=HI, preferred_element_type=f32)[0:1]
        dgate = dy_ref[...] * o_ref[...] * _dsilu(gate)
        dck_v = dck_ref[...] + dcq_ref[...]
        rc = jnp.dot(_upper_tri(tm), dck_v, precision=HI, preferred_element_type=f32) + carry[...]
        carry[...] += jnp.sum(dck_v, axis=0, keepdims=True)
        f = f_ref[...] + fb_ref[...]
        df = jnp.where(_iota2((tm, 128), 1) < NH, rc * _sigmoid(-f), 0.0)
        gfb_ref[...] += jnp.sum(df, axis=0, keepdims=True)
        dz_ref[:, 0:W] = dq.astype(bf16)
        dz_ref[:, W:2 * W] = dk.astype(bf16)
        dz_ref[:, 2 * W:3 * W] = dv_ref[...].astype(bf16)
        dz_ref[:, 3 * W:4 * W] = dgate.astype(bf16)
        dz_ref[:, 4 * W:4 * W + 128] = df.astype(bf16)

    rev = lambda i: nt - 1 - i
    zb = lambda col: pl.BlockSpec((tm, W), lambda i, c=col: (rev(i), c))
    hspec = pl.BlockSpec((tm, W), lambda i: (rev(i), 0))
    return _pcall(
        body, name="attn_post", grid=(nt,),
        in_specs=[zb(c0), zb(c0 + 1), zb(c0 + 3),
                  pl.BlockSpec((tm, 128), lambda i: (rev(i), OFF_F // 128)),
                  pl.BlockSpec((tm, W), lambda i: (rev(i), 0)),
                  hspec, hspec, hspec, hspec,
                  pl.BlockSpec((tm, 128), lambda i: (rev(i), 0)),
                  pl.BlockSpec((tm, 128), lambda i: (rev(i), 0)),
                  pl.BlockSpec((1, 128), lambda i: (0, 0)),
                  pl.BlockSpec((1, W), lambda i: (0, 0)), pl.BlockSpec((1, W), lambda i: (0, 0))],
        out_specs=[pl.BlockSpec((tm, 4 * W + 128), lambda i: (rev(i), 0)),
                   pl.BlockSpec((1, W), lambda i: (0, 0)), pl.BlockSpec((1, W), lambda i: (0, 0)),
                   pl.BlockSpec((1, 128), lambda i: (0, 0))],
        out_shape=[jax.ShapeDtypeStruct((T, 4 * W + 128), bf16), jax.ShapeDtypeStruct((1, W), f32),
                   jax.ShapeDtypeStruct((1, W), f32), jax.ShapeDtypeStruct((1, 128), f32)],
        scratch_shapes=[pltpu.VMEM((1, 128), f32)],
        compiler_params=_params("arbitrary"),
    )(z, z, z, z, dy, oh, dqh, dkh, dvh, dck, dcq, fbias, gq, gk)


def _merge_fwd(ya, oh, z, yc, yd, mb, x, p, wup, wo, gp, wpg, wpp, tm):
    T = x.shape[0]
    cg = OFF_ATT // W + 3

    def body(ya_ref, oh_ref, bg_ref, yc_ref, yd_ref, ml_ref, mb_ref, x_ref, p_ref, wup_ref, wo_ref, gp_ref,
             wpg_ref, wpp_ref, yb_ref, mg_ref, x1_ref, x2_ref):
        yb = (oh_ref[...] * _silu(bg_ref[...])).astype(bf16)
        yb_ref[...] = yb
        ys = (ya_ref[...], yb, yc_ref[...], yd_ref[...])
        merged = jnp.zeros((tm, D), f32)
        for b in range(NBR):
            sg = _sigmoid(ml_ref[:, b * D:(b + 1) * D] + mb_ref[b:b + 1, :])
            merged = merged + sg * _mm(ys[b], wup_ref[b])
        mgb = merged.astype(bf16)
        mg_ref[...] = mgb
        x1 = x_ref[...] + _mm(mgb, wo_ref[...])
        x1_ref[...] = x1
        r = lax.rsqrt(jnp.mean(x1 * x1, axis=-1, keepdims=True) + EPS)
        hp = (x1 * r * gp_ref[...]).astype(bf16)
        gate = _sigmoid(_mm(hp, wpg_ref[...]))
        x2_ref[...] = x1 + gate * _mm(p_ref[...].astype(bf16), wpp_ref[...])

    row = lambda width: pl.BlockSpec((tm, width), lambda i: (i, 0))
    full = lambda *shape: pl.BlockSpec(shape, lambda i: (0,) * len(shape))
    return _pcall(
        body, name="merge_fwd", grid=(T // tm,),
        in_specs=[row(W), row(W), _zblock(tm, cg), row(W), row(W),
                  pl.BlockSpec((tm, NBR * D), lambda i: (i, 0)), full(NBR, D), row(D), row(PLE),
                  full(NBR, W, D), full(D, D), full(1, D), full(D, D), full(PLE, D)],
        out_specs=[row(W), row(D), row(D), row(D)],
        out_shape=[jax.ShapeDtypeStruct((T, W), bf16), jax.ShapeDtypeStruct((T, D), bf16),
                   jax.ShapeDtypeStruct((T, D), f32), jax.ShapeDtypeStruct((T, D), f32)],
        compiler_params=_params("parallel"),
    )(ya, oh, z, yc, yd, z, mb, x, p, wup, wo, gp, wpg, wpp)


def _ple_bwd(dx2, x1, p, gp, wpg, wpp, tm):
    T = x1.shape[0]

    def body(dx2_ref, x1_ref, p_ref, gp_ref, wpg_ref, wpp_ref, dx1_ref, gwpg_ref, gwpp_ref, ggp_ref):
        @pl.when(pl.program_id(0) == 0)
        def _():
            gwpg_ref[...] = jnp.zeros_like(gwpg_ref)
            gwpp_ref[...] = jnp.zeros_like(gwpp_ref)
            ggp_ref[...] = jnp.zeros_like(ggp_ref)

        x1, dx2 = x1_ref[...], dx2_ref[...]
        r = lax.rsqrt(jnp.mean(x1 * x1, axis=-1, keepdims=True) + EPS)
        xh = x1 * r
        gp = gp_ref[...]
        hp = (xh * gp).astype(bf16)
        gate = _sigmoid(_mm(hp, wpg_ref[...]))
        pb = p_ref[...].astype(bf16)
        pp = _mm(pb, wpp_ref[...])
        dpre = ((dx2 * pp) * gate * (1.0 - gate)).astype(bf16)
        gwpp_ref[...] += _mm_tn(pb, (dx2 * gate).astype(bf16))
        gwpg_ref[...] += _mm_tn(hp, dpre)
        dhp = _mm_nt(dpre, wpg_ref[...])
        ggp_ref[...] += jnp.sum(dhp * xh, axis=0, keepdims=True)
        u = dhp * gp
        dx1_ref[...] = dx2 + r * u - x1 * (r * r * r) * jnp.mean(u * x1, axis=-1, keepdims=True)

    row = lambda width: pl.BlockSpec((tm, width), lambda i: (i, 0))
    full = lambda *shape: pl.BlockSpec(shape, lambda i: (0,) * len(shape))
    return _pcall(
        body, name="ple_bwd", grid=(T // tm,),
        in_specs=[row(D), row(D), row(PLE), full(1, D), full(D, D), full(PLE, D)],
        out_specs=[row(D), full(D, D), full(PLE, D), full(1, D)],
        out_shape=[jax.ShapeDtypeStruct((T, D), f32), jax.ShapeDtypeStruct((D, D), f32),
                   jax.ShapeDtypeStruct((PLE, D), f32), jax.ShapeDtypeStruct((1, D), f32)],
        compiler_params=_params("arbitrary"),
    )(dx2, x1, p, gp, wpg, wpp)


def _merge_bwd(dx1, mg, ya, yb, yc, yd, z, mb, wup, wo, tm):
    T = dx1.shape[0]

    def body(dx1_ref, mg_ref, ya_ref, yb_ref, yc_ref, yd_ref, ml_ref, mb_ref, wup_ref, wo_ref,
             dml_ref, dya_ref, dyb_ref, dyc_ref, dyd_ref, gwo_ref, gwup_ref, gmb_ref):
        @pl.when(pl.program_id(0) == 0)
        def _():
            gwo_ref[...] = jnp.zeros_like(gwo_ref)
            gwup_ref[...] = jnp.zeros_like(gwup_ref)
            gmb_ref[...] = jnp.zeros_like(gmb_ref)

        dx1b = dx1_ref[...].astype(bf16)
        gwo_ref[...] += _mm_tn(mg_ref[...], dx1b)
        dm = _mm_nt(dx1b, wo_ref[...])
        ys = (ya_ref, yb_ref, yc_ref, yd_ref)
        dys = (dya_ref, dyb_ref, dyc_ref, dyd_ref)
        for b in range(NBR):
            y = ys[b][...]
            up = _mm(y, wup_ref[b])
            sg = _sigmoid(ml_ref[:, b * D:(b + 1) * D] + mb_ref[b:b + 1, :])
            dup = (dm * sg).astype(bf16)
            dml = dm * up * sg * (1.0 - sg)
            gmb_ref[b:b + 1, :] += jnp.sum(dml, axis=0, keepdims=True)
            dml_ref[:, b * D:(b + 1) * D] = dml.astype(bf16)
            gwup_ref[b] += _mm_tn(y, dup)
            dys[b][...] = _mm_nt(dup, wup_ref[b])

    row = lambda width: pl.BlockSpec((tm, width), lambda i: (i, 0))
    full = lambda *shape: pl.BlockSpec(shape, lambda i: (0,) * len(shape))
    return _pcall(
        body, name="merge_bwd", grid=(T // tm,),
        in_specs=[row(D), row(D), row(W), row(W), row(W), row(W), row(NBR * D), full(NBR, D),
                  full(NBR, W, D), full(D, D)],
        out_specs=[row(NBR * D), row(W), row(W), row(W), row(W), full(D, D), full(NBR, W, D), full(NBR, D)],
        out_shape=[jax.ShapeDtypeStruct((T, NBR * D), bf16)] + [jax.ShapeDtypeStruct((T, W), f32)] * 4
        + [jax.ShapeDtypeStruct((D, D), f32), jax.ShapeDtypeStruct((NBR, W, D), f32),
           jax.ShapeDtypeStruct((NBR, D), f32)],
        compiler_params=_params("arbitrary"),
    )(dx1, mg, ya, yb, yc, yd, z, mb, wup, wo)


def _loss_head(y, target, tm):
    T = y.shape[0]

    def body(y_ref, t_ref, loss_ref, dy_ref, acc):
        i = pl.program_id(0)

        @pl.when(i == 0)
        def _():
            acc[...] = jnp.zeros_like(acc)

        e = y_ref[...] - t_ref[...]
        dy_ref[...] = e * (1.0 / D)
        acc[...] += jnp.sum(e * e, axis=0, keepdims=True)

        @pl.when(i == T // tm - 1)
        def _():
            loss_ref[...] = jnp.sum(acc[...], axis=1, keepdims=True) * (0.5 / D)

    return _pcall(
        body, name="loss_head", grid=(T // tm,),
        in_specs=[pl.BlockSpec((tm, D), lambda i: (i, 0)), pl.BlockSpec((tm, D), lambda i: (i, 0))],
        out_specs=[pl.BlockSpec((1, 1), lambda i: (0, 0)), pl.BlockSpec((tm, D), lambda i: (i, 0))],
        out_shape=[jax.ShapeDtypeStruct((1, 1), f32), jax.ShapeDtypeStruct((T, D), f32)],
        scratch_shapes=[pltpu.VMEM((1, D), f32)],
        compiler_params=_params("arbitrary"),
    )(y, target)


def _lb_softmax_rows(l_ref):
    rows = [l_ref[i:i + 1, :] for i in range(DEPTH)]
    m = rows[0]
    for r in rows[1:]:
        m = jnp.maximum(m, r)
    es = [jnp.exp(r - m) for r in rows]
    tot = es[0]
    for e in es[1:]:
        tot = tot + e
    return [e / tot for e in es]


def _lb_partial_sums(pr):
    sums = [jnp.zeros_like(pr[0])]
    for i in range(1, DEPTH):
        sums.append(sums[-1] + pr[i])
    return sums


def _stack_rows(rows, width):
    idx = _iota2((8, width), 0)
    out = jnp.zeros((8, width), f32)
    for i, r in enumerate(rows):
        out = jnp.where(idx == i, r, out)
    return out


def _lower_bounds(lb_logits):
    def body(l_ref, o_ref):
        sums = _lb_partial_sums(_lb_softmax_rows(l_ref))
        o_ref[...] = _stack_rows([jnp.clip(s, 0.0, 1.0) for s in sums], W)

    return _pcall(body, name="lower_bounds", out_shape=jax.ShapeDtypeStruct((8, W), f32))(lb_logits)


def _lower_bounds_bwd(lb_logits, dlower):
    def body(l_ref, d_ref, o_ref):
        pr = _lb_softmax_rows(l_ref)
        sums = _lb_partial_sums(pr)
        dl = [jnp.where((sums[i] > 0.0) & (sums[i] < 1.0), d_ref[i:i + 1, :], 0.0) for i in range(DEPTH)]
        dp = [jnp.zeros_like(pr[0])] * DEPTH
        run = jnp.zeros_like(pr[0])
        for j in reversed(range(1, DEPTH)):
            run = run + dl[j]
            dp[j] = run
        inner = pr[0] * dp[0]
        for j in range(1, DEPTH):
            inner = inner + pr[j] * dp[j]
        o_ref[...] = _stack_rows([pr[j] * (dp[j] - inner) for j in range(DEPTH)], W)

    return _pcall(body, name="lower_bounds_bwd", out_shape=jax.ShapeDtypeStruct((8, W), f32))(lb_logits, dlower)


def _row_tile(rows, cols, budget_bytes=1 << 20, mult=8):
    if rows % mult:
        return rows
    best = mult
    for t in range(mult, rows + 1, mult):
        if rows % t == 0 and t * cols * 4 <= budget_bytes:
            best = t
    return best


def _sum_slabs(land):
    _, R, C = land.shape
    tr = _row_tile(R, C * N_DEV, mult=16)

    def body(l_ref, o_ref):
        acc = l_ref[0].astype(f32)
        for j in range(1, N_DEV):
            acc = acc + l_ref[j].astype(f32)
        o_ref[...] = acc

    return _pcall(
        body, name="sum_slabs", grid=(R // tr,),
        in_specs=[pl.BlockSpec((N_DEV, tr, C), lambda i: (0, i, 0))],
        out_specs=pl.BlockSpec((tr, C), lambda i: (i, 0)),
        out_shape=jax.ShapeDtypeStruct((R, C), f32),
        compiler_params=_params("parallel"),
    )(land)


def _adamw(w, g, m, v):
    R, C = w.shape
    tr = _row_tile(R, C)
    c1 = 1.0 / (1.0 - ADAM_B1 ** ADAM_STEP)
    c2 = 1.0 / (1.0 - ADAM_B2 ** ADAM_STEP)

    def body(w_ref, g_ref, m_ref, v_ref, d_ref, nm_ref, nv_ref):
        gv = g_ref[...]
        nm = ADAM_B1 * m_ref[...] + (1.0 - ADAM_B1) * gv
        nv = ADAM_B2 * v_ref[...] + (1.0 - ADAM_B2) * (gv * gv)
        nm_ref[...] = nm
        nv_ref[...] = nv
        d_ref[...] = -ADAM_LR * ((nm * c1) / (jnp.sqrt(nv * c2) + ADAM_EPS) + ADAM_WD * w_ref[...])

    spec = pl.BlockSpec((tr, C), lambda i: (i, 0))
    return _pcall(
        body, name="adamw", grid=(R // tr,),
        in_specs=[spec] * 4, out_specs=[spec] * 3,
        out_shape=[jax.ShapeDtypeStruct((R, C), f32)] * 3,
        compiler_params=_params("parallel"),
    )(w, g, m, v)


def _my_id():
    return lax.axis_index("x") * 4 + lax.axis_index("y") * 2 + lax.axis_index("c")


def _peer(k):
    x, y, c = lax.axis_index("x"), lax.axis_index("y"), lax.axis_index("c")
    kx, ky, kc = (k >> 2) & 1, (k >> 1) & 1, k & 1
    px, py, pc = x ^ kx, y ^ ky, c ^ kc
    return (px, py, pc), px * 4 + py * 2 + pc


def _all_gather(shards):
    n = len(shards)

    def body(*refs):
        srcs, outs = refs[:n], refs[n:2 * n]
        send_sems, recv_sems, local_sems = refs[2 * n:]
        me = _my_id()
        locals_ = [pltpu.make_async_copy(srcs[a], outs[a].at[me], local_sems.at[a]) for a in range(n)]
        for cp in locals_:
            cp.start()
        sends = []
        for k in range(1, N_DEV):
            peer, _ = _peer(k)
            for a in range(n):
                cp = pltpu.make_async_remote_copy(
                    src_ref=srcs[a], dst_ref=outs[a].at[me],
                    send_sem=send_sems.at[a, k - 1], recv_sem=recv_sems.at[a, k - 1],
                    device_id=peer, device_id_type=pl.DeviceIdType.MESH)
                cp.start()
                sends.append(cp)
        for k in range(1, N_DEV):
            peer, pid = _peer(k)
            for a in range(n):
                pltpu.make_async_remote_copy(
                    src_ref=srcs[a], dst_ref=outs[a].at[pid],
                    send_sem=send_sems.at[a, k - 1], recv_sem=recv_sems.at[a, k - 1],
                    device_id=peer, device_id_type=pl.DeviceIdType.MESH).wait_recv()
        for cp in sends:
            cp.wait_send()
        for cp in locals_:
            cp.wait()

    hbm = pl.BlockSpec(memory_space=pltpu.HBM)
    return _pcall(
        body, name="all_gather",
        in_specs=[hbm] * n, out_specs=[hbm] * n,
        out_shape=[jax.ShapeDtypeStruct((N_DEV,) + s.shape, s.dtype) for s in shards],
        scratch_shapes=[pltpu.SemaphoreType.DMA((n, N_DEV - 1)), pltpu.SemaphoreType.DMA((n, N_DEV - 1)),
                        pltpu.SemaphoreType.DMA((n,))],
    )(*shards)


def _exchange(sliced, whole):
    ns, nw = len(sliced), len(whole)
    n = ns + nw

    def body(*refs):
        srcs, outs = refs[:n], refs[n:2 * n]
        send_sems, recv_sems, local_sems = refs[2 * n:]
        me = _my_id()

        def src_of(a, dest):
            return srcs[a].at[dest] if a < ns else srcs[a]

        locals_ = [pltpu.make_async_copy(src_of(a, me), outs[a].at[me], local_sems.at[a]) for a in range(n)]
        for cp in locals_:
            cp.start()
        sends = []
        for k in range(1, N_DEV):
            peer, pid = _peer(k)
            for a in range(n):
                cp = pltpu.make_async_remote_copy(
                    src_ref=src_of(a, pid), dst_ref=outs[a].at[me],
                    send_sem=send_sems.at[a, k - 1], recv_sem=recv_sems.at[a, k - 1],
                    device_id=peer, device_id_type=pl.DeviceIdType.MESH)
                cp.start()
                sends.append(cp)
        for k in range(1, N_DEV):
            peer, pid = _peer(k)
            for a in range(n):
                pltpu.make_async_remote_copy(
                    src_ref=src_of(a, pid), dst_ref=outs[a].at[pid],
                    send_sem=send_sems.at[a, k - 1], recv_sem=recv_sems.at[a, k - 1],
                    device_id=peer, device_id_type=pl.DeviceIdType.MESH).wait_recv()
        for cp in sends:
            cp.wait_send()
        for cp in locals_:
            cp.wait()

    hbm = pl.BlockSpec(memory_space=pltpu.HBM)
    shapes = [jax.ShapeDtypeStruct(s.shape, s.dtype) for s in sliced]
    shapes += [jax.ShapeDtypeStruct((N_DEV,) + s.shape, s.dtype) for s in whole]
    return _pcall(
        body, name="grad_exchange",
        in_specs=[hbm] * n, out_specs=[hbm] * n, out_shape=shapes,
        scratch_shapes=[pltpu.SemaphoreType.DMA((n, N_DEV - 1)), pltpu.SemaphoreType.DMA((n, N_DEV - 1)),
                        pltpu.SemaphoreType.DMA((n,))],
    )(*sliced, *whole)


def _permute_cols(w):
    pad = jnp.zeros(w.shape[:-1] + (NZ - OFF_F - NH,), w.dtype)
    return jnp.concatenate([
        w[..., 3844:7940],
        w[..., 0:1024],
        w[..., 2052:3076],
        w[..., 3076:3844],
        w[..., 1024:2048],
        w[..., 2048:2052], pad], axis=-1)


def _unpermute_cols(g):
    return jnp.concatenate([
        g[..., OFF_CONV:OFF_CONV + 1024],
        g[..., OFF_ATT:OFF_ATT + 1024],
        g[..., OFF_F:OFF_F + NH],
        g[..., OFF_HGRN:OFF_HGRN + 1024],
        g[..., OFF_SGU:OFF_SGU + 768],
        g[..., 0:4096]], axis=-1)


_SMALL = (
    ("norm_mix", (DEPTH, D)), ("conv_w", (DEPTH, CONV_WIDTH, W)), ("conv_b", (DEPTH, W)),
    ("fgate_bias", (DEPTH, NH)), ("q_norm", (DEPTH, DH)), ("k_norm", (DEPTH, DH)),
    ("lb_logits", (DEPTH, W)), ("hgrn_norm", (DEPTH, W)), ("sgu_norm", (DEPTH, W)),
    ("spatial_w", (DEPTH, NH, SGU_CHUNK, SGU_CHUNK)), ("spatial_b", (DEPTH, NH, SGU_CHUNK)),
    ("merge_b", (DEPTH, NBR, D)), ("norm_ple", (DEPTH, D)),
)


def _small_rows(shape):
    size = 1
    for s in shape:
        size *= s
    rows = -(-size // 128)
    return size, -(-rows // 8) * 8


def _pack_small(parts):
    out = []
    for name, shape in _SMALL:
        size, rows = _small_rows(shape)
        flat = parts[name].astype(f32).reshape(-1)
        flat = jnp.pad(flat, (0, rows * 128 - size))
        out.append(flat.reshape(rows, 128))
    return jnp.concatenate(out, axis=0)


def _unpack_small(buf):
    parts, r0 = {}, 0
    for name, shape in _SMALL:
        size, rows = _small_rows(shape)
        parts[name] = buf[r0:r0 + rows].reshape(-1)[:size].reshape(shape)
        r0 += rows
    return parts


def _shard_cols(a, width):
    return lax.dynamic_slice_in_dim(a, _my_id() * width, width, axis=a.ndim - 1)


def kernel(x, p, norm_mix, w_in, conv_w, conv_b, fgate_bias, q_norm, k_norm, lb_logits, hgrn_norm, sgu_norm, spatial_w, spatial_b, w_up, merge_b, w_o, norm_ple, w_ple_gate, w_ple_proj, loss_target, m_norm_mix, m_w_in, m_conv_w, m_conv_b, m_fgate_bias, m_q_norm, m_k_norm, m_lb_logits, m_hgrn_norm, m_sgu_norm, m_spatial_w, m_spatial_b, m_w_up, m_merge_b, m_w_o, m_norm_ple, m_w_ple_gate, m_w_ple_proj, v_norm_mix, v_w_in, v_conv_w, v_conv_b, v_fgate_bias, v_q_norm, v_k_norm, v_lb_logits, v_hgrn_norm, v_sgu_norm, v_spatial_w, v_spatial_b, v_w_up, v_merge_b, v_w_o, v_norm_ple, v_w_ple_gate, v_w_ple_proj):
    T = x.shape[1]
    SH = D // N_DEV
    CW = W // N_DEV
    tm = 512 if T % 512 == 0 else T
    tmm = 256 if T % 256 == 0 else T
    x0 = x.reshape(T, D)
    target = loss_target.reshape(T, D)

    small_shard = jnp.concatenate([
        merge_b.reshape(DEPTH * NBR, SH),
        jnp.pad(conv_w.reshape(DEPTH * CONV_WIDTH, CW), ((0, 16 - DEPTH * CONV_WIDTH), (0, SH - CW)))], axis=0)
    g_win, g_wup, g_wo, g_wpg, g_wpp, g_small = _all_gather([
        _permute_cols(w_in).astype(bf16).reshape(DEPTH * SH, NZ),
        w_up.astype(bf16).reshape(DEPTH * NBR * W, SH),
        w_o.astype(bf16).reshape(DEPTH * SH, D),
        w_ple_gate.astype(bf16).reshape(DEPTH * SH, D),
        w_ple_proj.astype(bf16).reshape(DEPTH * PLE, SH),
        small_shard])
    win_f = g_win.reshape(N_DEV, DEPTH, SH, NZ).transpose(1, 0, 2, 3).reshape(DEPTH, D, NZ)
    wup_f = g_wup.reshape(N_DEV, DEPTH, NBR, W, SH).transpose(1, 2, 3, 0, 4).reshape(DEPTH, NBR, W, D)
    wo_f = g_wo.reshape(N_DEV, DEPTH, SH, D).transpose(1, 0, 2, 3).reshape(DEPTH, D, D)
    wpg_f = g_wpg.reshape(N_DEV, DEPTH, SH, D).transpose(1, 0, 2, 3).reshape(DEPTH, D, D)
    wpp_f = g_wpp.reshape(N_DEV, DEPTH, PLE, SH).transpose(1, 2, 0, 3).reshape(DEPTH, PLE, D)
    mb_f = g_small[:, 0:DEPTH * NBR].reshape(N_DEV, DEPTH, NBR, SH).transpose(1, 2, 0, 3).reshape(DEPTH, NBR, D)
    cw_f = g_small[:, 16:16 + DEPTH * CONV_WIDTH, 0:CW].reshape(N_DEV, DEPTH, CONV_WIDTH, CW)
    cw_f = cw_f.transpose(1, 2, 0, 3).reshape(DEPTH, CONV_WIDTH, W)

    loss_local, dx, gw, gs_full = _forward_backward(
        x0, p[:, 0], target, win_f, wup_f, wo_f, wpg_f, wpp_f, mb_f, cw_f, norm_mix, conv_b, fgate_bias, q_norm,
        k_norm, lb_logits, hgrn_norm, sgu_norm, spatial_w, spatial_b, norm_ple)
    loss = lax.psum(loss_local[0, 0], AXES)
    grad_x = dx.reshape(1, T, D)

    weights = dict(norm_mix=norm_mix, w_in=w_in, conv_w=conv_w, conv_b=conv_b, fgate_bias=fgate_bias, q_norm=q_norm,
                   k_norm=k_norm, lb_logits=lb_logits, hgrn_norm=hgrn_norm, sgu_norm=sgu_norm, spatial_w=spatial_w,
                   spatial_b=spatial_b, w_up=w_up, merge_b=merge_b, w_o=w_o, norm_ple=norm_ple,
                   w_ple_gate=w_ple_gate, w_ple_proj=w_ple_proj)
    ms = dict(norm_mix=m_norm_mix, w_in=m_w_in, conv_w=m_conv_w, conv_b=m_conv_b, fgate_bias=m_fgate_bias,
              q_norm=m_q_norm, k_norm=m_k_norm, lb_logits=m_lb_logits, hgrn_norm=m_hgrn_norm, sgu_norm=m_sgu_norm,
              spatial_w=m_spatial_w, spatial_b=m_spatial_b, w_up=m_w_up, merge_b=m_merge_b, w_o=m_w_o,
              norm_ple=m_norm_ple, w_ple_gate=m_w_ple_gate, w_ple_proj=m_w_ple_proj)
    vs = dict(norm_mix=v_norm_mix, w_in=v_w_in, conv_w=v_conv_w, conv_b=v_conv_b, fgate_bias=v_fgate_bias,
              q_norm=v_q_norm, k_norm=v_k_norm, lb_logits=v_lb_logits, hgrn_norm=v_hgrn_norm, sgu_norm=v_sgu_norm,
              spatial_w=v_spatial_w, spatial_b=v_spatial_b, w_up=v_w_up, merge_b=v_merge_b, w_o=v_w_o,
              norm_ple=v_norm_ple, w_ple_gate=v_w_ple_gate, w_ple_proj=v_w_ple_proj)
    return _exchange_and_update(loss, grad_x, gw, gs_full, weights, ms, vs)


def _forward_backward(x0, p, target, win_f, wup_f, wo_f, wpg_f, wpp_f, mb_f, cw_f, norm_mix, conv_b, fgate_bias,
                      q_norm, k_norm, lb_logits, hgrn_norm, sgu_norm, spatial_w, spatial_b, norm_ple):
    T = x0.shape[0]
    tm = 512 if T % 512 == 0 else T
    tmm = 256 if T % 256 == 0 else T
    tmi = 1024 if T % 1024 == 0 else tm
    lower = _lower_bounds(lb_logits)
    fb_pad = jnp.pad(fgate_bias, ((0, 0), (0, 128 - NH)))
    gq_t = jnp.tile(q_norm, (1, NH))
    gk_t = jnp.tile(k_norm, (1, NH))
    sbe = jnp.repeat(jnp.swapaxes(spatial_b, 1, 2), DH, axis=2)

    saved = []
    xc = x0
    p = p[:, None]
    for li in range(DEPTH):
        row = lambda a: a[li:li + 1]
        z, h = _inproj_fwd(xc, row(norm_mix), win_f[li], tmi)
        ya = _conv_fwd(z, cw_f[li], row(conv_b), tm)
        yd = _sgu_fwd(z, row(sgu_norm), spatial_w[li], sbe[li], tm)
        yc, o_pre, states = _hgrn_fwd(z, lower[li:li + 1], row(hgrn_norm), tmm)
        qt, kt, vt, kh, vh, cum = _attn_prep(z, row(fb_pad), row(gq_t), row(gk_t), tm)
        cum4 = jnp.transpose(cum[:, 0:NH])
        ccol, crow = cum4[:, :, None], cum4[:, None, :]
        oh, lse = _attn_fwd(qt, kh, vt, crow, ccol, tm)
        yb, mg, x1, x2 = _merge_fwd(ya, oh, z, yc, yd, mb_f[li], xc, p[li, 0], wup_f[li], wo_f[li],
                                    row(norm_ple), wpg_f[li], wpp_f[li], tmm)
        saved.append(dict(x=xc, z=z, h=h, ya=ya, yb=yb, yc=yc, yd=yd, o_pre=o_pre, states=states,
                          qt=qt, kt=kt, kh=kh, vh=vh, crow=crow, ccol=ccol, oh=oh, lse=lse, mg=mg, x1=x1))
        xc = x2

    loss_local, dx = _loss_head(xc, target, tm)

    gw = {n: [None] * DEPTH for n in ("w_in", "w_up", "w_o", "w_ple_gate", "w_ple_proj")}
    gs = {n: [None] * DEPTH for n, _ in _SMALL}
    dlower = [None] * DEPTH
    for li in reversed(range(DEPTH)):
        s = saved[li]
        row = lambda a: a[li:li + 1]
        dx1, gw["w_ple_gate"][li], gw["w_ple_proj"][li], ggp = _ple_bwd(
            dx, s["x1"], p[li, 0], row(norm_ple), wpg_f[li], wpp_f[li], tmm)
        gs["norm_ple"][li] = ggp[0]
        dml, dya, dyb, dyc, dyd, gw["w_o"][li], gw["w_up"][li], gs["merge_b"][li] = _merge_bwd(
            dx1, s["mg"], s["ya"], s["yb"], s["yc"], s["yd"], s["z"], mb_f[li], wup_f[li], wo_f[li], tmm)
        dz_conv, gcw, gcb = _conv_bwd(s["z"], dya, cw_f[li], row(conv_b), tm)
        gs["conv_w"][li], gs["conv_b"][li] = gcw[0:CONV_WIDTH], gcb[0]
        dz_sgu, gs["spatial_w"][li], gsb, ggv = _sgu_bwd(s["z"], dyd, row(sgu_norm), spatial_w[li], sbe[li], tm)
        gs["spatial_b"][li] = jnp.transpose(gsb[:, ::DH])
        gs["sgu_norm"][li] = ggv[0]
        dz_hgrn, ggn, glb = _hgrn_bwd(s["z"], lower[li:li + 1], row(hgrn_norm), s["o_pre"], s["states"], dyc, tmm)
        gs["hgrn_norm"][li], dlower[li] = ggn[0], glb[0]
        dot, delta = _attn_bwd_prep(dyb, s["oh"], s["z"], tm)
        dqh, dkh, dvh, dck, dcq = _attn_bwd(s["qt"], s["kt"], s["kh"], s["vh"], s["crow"], s["ccol"], dot,
                                            s["lse"], delta, tm)
        dck_t = jnp.pad(jnp.transpose(dck.reshape(NH, T)), ((0, 0), (0, 128 - NH)))
        dcq_t = jnp.transpose(dcq[:, :, 0:HP, :], (0, 2, 1, 3)).reshape(NH, T)
        dcq_t = jnp.pad(jnp.transpose(dcq_t), ((0, 0), (0, 128 - NH)))
        dz_att, ggq, ggk, gfb = _attn_post(s["z"], dyb, s["oh"], dqh, dkh, dvh, dck_t, dcq_t, row(fb_pad),
                                           row(gq_t), row(gk_t), tm)
        gs["q_norm"][li], gs["k_norm"][li], gs["fgate_bias"][li] = ggq[0, 0:DH], ggk[0, 0:DH], gfb[0, 0:NH]
        dz = jnp.concatenate([dml, dz_conv, dz_hgrn, dz_sgu, dz_att], axis=1)
        dx, gnm = _inproj_bwd_x(dz, win_f[li], s["x"], dx1, row(norm_mix), tmi)
        gs["norm_mix"][li] = gnm[0]
        gw["w_in"][li] = _inproj_bwd_w(s["h"], dz, tmi)
    dlower8 = jnp.pad(jnp.stack(dlower), ((0, 8 - DEPTH), (0, 0)))
    gs_full = {n: jnp.stack(v) for n, v in gs.items() if n != "lb_logits"}
    gs_full["lb_logits"] = _lower_bounds_bwd(lb_logits, dlower8)[0:DEPTH]
    return loss_local, dx, gw, gs_full


def _exchange_and_update(loss, grad_x, gw, gs_full, weights, ms, vs):
    SH = D // N_DEV
    CW = W // N_DEV

    def by_rows(g):
        C = g.shape[-1]
        return g.astype(bf16).reshape(DEPTH, N_DEV, SH, C).transpose(1, 0, 2, 3).reshape(N_DEV, DEPTH * SH, C)

    def by_cols(g):
        lead = g.shape[:-1]
        r = 1
        for s_ in lead:
            r *= s_
        return g.astype(bf16).reshape(r, N_DEV, SH).transpose(1, 0, 2)

    small_buf = _pack_small(gs_full)
    l_win, l_wup, l_wo, l_wpg, l_wpp, l_small = _exchange(
        [by_rows(jnp.stack(gw["w_in"])), by_cols(jnp.stack(gw["w_up"])), by_rows(jnp.stack(gw["w_o"])),
         by_rows(jnp.stack(gw["w_ple_gate"])), by_cols(jnp.stack(gw["w_ple_proj"]))],
        [small_buf])

    g_w_in = _unpermute_cols(_sum_slabs(l_win)).reshape(DEPTH, SH, IN_COLS)
    g_w_up = _sum_slabs(l_wup).reshape(DEPTH, NBR, W, SH)
    g_w_o = _sum_slabs(l_wo).reshape(DEPTH, SH, D)
    g_w_pg = _sum_slabs(l_wpg).reshape(DEPTH, SH, D)
    g_w_pp = _sum_slabs(l_wpp).reshape(DEPTH, PLE, SH)
    g_small = _unpack_small(_sum_slabs(l_small))
    g_small_local = dict(g_small)
    g_small_local["conv_w"] = _shard_cols(g_small["conv_w"], CW)
    g_small_local["merge_b"] = _shard_cols(g_small["merge_b"], SH)

    grads = dict(w_in=g_w_in, w_up=g_w_up, w_o=g_w_o, w_ple_gate=g_w_pg, w_ple_proj=g_w_pp)
    deltas, new_m, new_v = {}, {}, {}
    for name, cols in (("w_in", IN_COLS), ("w_up", SH), ("w_o", D), ("w_ple_gate", D), ("w_ple_proj", SH)):
        shape = weights[name].shape
        d_, m_, v_ = _adamw(weights[name].reshape(-1, cols), grads[name].reshape(-1, cols),
                            ms[name].reshape(-1, cols), vs[name].reshape(-1, cols))
        deltas[name], new_m[name], new_v[name] = d_.reshape(shape), m_.reshape(shape), v_.reshape(shape)

    def local_shapes(parts):
        return {n: (parts[n] if parts[n].shape == s else jnp.pad(
            parts[n], [(0, 0)] * (len(s) - 1) + [(0, s[-1] - parts[n].shape[-1])])) for n, s in _SMALL}

    d_, m_, v_ = _adamw(_pack_small(local_shapes(weights)), _pack_small(local_shapes(g_small_local)),
                        _pack_small(local_shapes(ms)), _pack_small(local_shapes(vs)))
    for buf, dst in ((d_, deltas), (m_, new_m), (v_, new_v)):
        parts = _unpack_small(buf)
        for n, _ in _SMALL:
            dst[n] = parts[n][..., :weights[n].shape[-1]]
    for n, _ in _SMALL:
        grads[n] = g_small_local[n]

    order = ["norm_mix", "w_in", "conv_w", "conv_b", "fgate_bias", "q_norm", "k_norm", "lb_logits", "hgrn_norm",
             "sgu_norm", "spatial_w", "spatial_b", "w_up", "merge_b", "w_o", "norm_ple", "w_ple_gate", "w_ple_proj"]
    return (loss, grad_x, *[grads[n] for n in order], *[deltas[n] for n in order],
            *[new_m[n] for n in order], *[new_v[n] for n in order])
```

```python
import functools

import jax
import jax.numpy as jnp
from jax import lax
from jax.experimental import pallas as pl
from jax.experimental.pallas import tpu as pltpu

f32 = jnp.float32
bf16 = jnp.bfloat16

D = 1024
W = 256
NH = 4
DH = 64
NBR = 4
PLE = 256
DEPTH = 4
CONV_WIDTH = 3
SGU_CHUNK = 128
GLA_CHUNK = 64
EPS = 1e-6
MASK_VALUE = -1e30
IN_COLS = 7940
NZ = 8064
OFF_CONV = 4096
OFF_HGRN = 5120
OFF_SGU = 6144
OFF_ATT = 6912
OFF_F = 7936
ZT = 1152
NZT = NZ // ZT
EXP_CLAMP = 80.0

ADAM_LR = 0.001
ADAM_B1 = 0.9
ADAM_B2 = 0.999
ADAM_EPS = 1e-08
ADAM_WD = 0.01
ADAM_STEP = 10

N_DEV = 8
AXES = ("x", "y", "c")
VMEM_LIMIT = 56 * 1024 * 1024
HI = lax.Precision.HIGHEST

NT_DIMS = (((1,), (1,)), ((), ()))
TN_DIMS = (((0,), (0,)), ((), ()))


def _pcall(body, **kw):
    return pl.pallas_call(body, **kw)


def _params(*sem):
    return pltpu.CompilerParams(dimension_semantics=sem, vmem_limit_bytes=VMEM_LIMIT)


def _mm(a, b):
    return jnp.dot(a, b, preferred_element_type=f32)


def _mm_nt(a, b):
    return lax.dot_general(a, b, NT_DIMS, preferred_element_type=f32)


def _mm_tn(a, b):
    return lax.dot_general(a, b, TN_DIMS, preferred_element_type=f32)


def _sigmoid(x):
    return 1.0 / (1.0 + jnp.exp(-x))


def _silu(x):
    return x * _sigmoid(x)


def _dsilu(x):
    s = _sigmoid(x)
    return s * (1.0 + x * (1.0 - s))


def _logsigmoid(x):
    return jnp.minimum(x, 0.0) - jnp.log(1.0 + jnp.exp(-jnp.abs(x)))


def _iota2(shape, axis):
    return lax.broadcasted_iota(jnp.int32, shape, axis)


def _group_mean_matrix(n, group):
    shift = group.bit_length() - 1
    r = lax.shift_right_logical(_iota2((n, n), 0), shift)
    c = lax.shift_right_logical(_iota2((n, n), 1), shift)
    return jnp.where(r == c, 1.0 / group, 0.0).astype(f32)


def _group_mean(x, gm):
    return jnp.dot(x, gm, precision=HI, preferred_element_type=f32)


def _lower_tri(n):
    return jnp.where(_iota2((n, n), 0) >= _iota2((n, n), 1), 1.0, 0.0).astype(f32)


def _upper_tri(n):
    return jnp.where(_iota2((n, n), 0) <= _iota2((n, n), 1), 1.0, 0.0).astype(f32)


def _rows3(r0, r1, r2, width):
    row = _iota2((8, width), 0)
    return jnp.where(row == 0, r0, jnp.where(row == 1, r1, jnp.where(row == 2, r2, 0.0)))


def _inproj_fwd(x, g, w, tm):
    T = x.shape[0]

    def body(x_ref, g_ref, w_ref, z_ref, h_ref):
        @pl.when(pl.program_id(1) == 0)
        def _():
            xv = x_ref[...]
            r = lax.rsqrt(jnp.mean(xv * xv, axis=-1, keepdims=True) + EPS)
            h_ref[...] = (xv * r * g_ref[...]).astype(bf16)

        z_ref[...] = _mm(h_ref[...], w_ref[...])

    return _pcall(
        body, name="inproj_fwd", grid=(T // tm, NZT),
        in_specs=[pl.BlockSpec((tm, D), lambda i, j: (i, 0)),
                  pl.BlockSpec((1, D), lambda i, j: (0, 0)),
                  pl.BlockSpec((D, ZT), lambda i, j: (0, j))],
        out_specs=[pl.BlockSpec((tm, ZT), lambda i, j: (i, j)),
                   pl.BlockSpec((tm, D), lambda i, j: (i, 0))],
        out_shape=[jax.ShapeDtypeStruct((T, NZ), f32), jax.ShapeDtypeStruct((T, D), bf16)],
        compiler_params=_params("parallel", "arbitrary"),
    )(x, g, w)


def _inproj_bwd_x(dz, w, x, dx1, g, tm):
    T = x.shape[0]

    def body(dz_ref, w_ref, x_ref, dx1_ref, g_ref, dx_ref, gg_ref, acc):
        i, k = pl.program_id(0), pl.program_id(1)

        @pl.when(k == 0)
        def _():
            acc[...] = jnp.zeros_like(acc)

        @pl.when((i == 0) & (k == 0))
        def _():
            gg_ref[...] = jnp.zeros_like(gg_ref)

        acc[...] += _mm_nt(dz_ref[...], w_ref[...])

        @pl.when(k == NZT - 1)
        def _():
            xv = x_ref[...]
            r = lax.rsqrt(jnp.mean(xv * xv, axis=-1, keepdims=True) + EPS)
            dh = acc[...]
            gg_ref[...] += jnp.sum(dh * xv * r, axis=0, keepdims=True)
            u = dh * g_ref[...]
            dx_ref[...] = dx1_ref[...] + r * u - xv * (r * r * r) * jnp.mean(u * xv, axis=-1, keepdims=True)

    return _pcall(
        body, name="inproj_bwd_x", grid=(T // tm, NZT),
        in_specs=[pl.BlockSpec((tm, ZT), lambda i, k: (i, k)),
                  pl.BlockSpec((D, ZT), lambda i, k: (0, k)),
                  pl.BlockSpec((tm, D), lambda i, k: (i, 0)),
                  pl.BlockSpec((tm, D), lambda i, k: (i, 0)),
                  pl.BlockSpec((1, D), lambda i, k: (0, 0))],
        out_specs=[pl.BlockSpec((tm, D), lambda i, k: (i, 0)),
                   pl.BlockSpec((1, D), lambda i, k: (0, 0))],
        out_shape=[jax.ShapeDtypeStruct((T, D), f32), jax.ShapeDtypeStruct((1, D), f32)],
        scratch_shapes=[pltpu.VMEM((tm, D), f32)],
        compiler_params=_params("arbitrary", "arbitrary"),
    )(dz, w, x, dx1, g)


def _inproj_bwd_w(h, dz, tm, li, buf):
    T = h.shape[0]
    SH = D // N_DEV
    nt = T // tm
    extra = [] if buf is None else [buf]

    def body(h_ref, dz_ref, *rest):
        gw_ref, acc = rest[len(extra):]

        @pl.when(pl.program_id(1) == 0)
        def _():
            acc[...] = jnp.zeros_like(acc)

        acc[...] += _mm_tn(h_ref[...], dz_ref[...])

        @pl.when(pl.program_id(1) == nt - 1)
        def _():
            gw_ref[...] = acc[...].reshape(N_DEV, SH, ZT).astype(bf16)

    return _pcall(
        body, name="inproj_bwd_w", grid=(NZT, nt),
        in_specs=[pl.BlockSpec((tm, D), lambda j, i: (i, 0)),
                  pl.BlockSpec((tm, ZT), lambda j, i: (i, j))] + [pl.BlockSpec(memory_space=pl.ANY)] * len(extra),
        out_specs=pl.BlockSpec((N_DEV, SH, ZT), lambda j, i: (0, li, j)),
        out_shape=jax.ShapeDtypeStruct((N_DEV, DEPTH * SH, NZ), bf16),
        scratch_shapes=[pltpu.VMEM((D, ZT), f32)],
        input_output_aliases={2: 0} if extra else {},
        compiler_params=_params("parallel", "arbitrary"),
    )(h, dz, *extra)


def _zblock(tm, col256):
    return pl.BlockSpec((tm, W), lambda i, c=col256: (i, c))


def _conv_taps(zc, halo, cw_ref, n):
    ext = jnp.concatenate([halo, zc], axis=0)
    z1 = pltpu.roll(ext, 1, 0)[8:]
    z2 = pltpu.roll(ext, 2, 0)[8:]
    return z1, z2


def _conv_fwd(z, cw, cb, tm):
    T = z.shape[0]
    c0 = OFF_CONV // W
    hb = tm // 8

    def body(ax_ref, ab_ref, ac_ref, ag_ref, hx_ref, hc_ref, cw_ref, cb_ref, y_ref):
        i = pl.program_id(0)
        zc = ac_ref[...] * ax_ref[...]
        halo = jnp.where(i > 0, hc_ref[...] * hx_ref[...], 0.0)
        z1, z2 = _conv_taps(zc, halo, cw_ref, tm)
        y = cw_ref[2:3, :] * zc + cw_ref[1:2, :] * z1 + cw_ref[0:1, :] * z2
        ya = ab_ref[...] * (y + cb_ref[...])
        y_ref[...] = (ya * _silu(ag_ref[...])).astype(bf16)

    halo_spec = lambda col: pl.BlockSpec((8, W), lambda i, c=col: (jnp.maximum(i * hb - 1, 0), c))
    return _pcall(
        body, name="conv_fwd", grid=(T // tm,),
        in_specs=[_zblock(tm, c0), _zblock(tm, c0 + 1), _zblock(tm, c0 + 2), _zblock(tm, c0 + 3),
                  halo_spec(c0), halo_spec(c0 + 2),
                  pl.BlockSpec((CONV_WIDTH, W), lambda i: (0, 0)),
                  pl.BlockSpec((1, W), lambda i: (0, 0))],
        out_specs=pl.BlockSpec((tm, W), lambda i: (i, 0)),
        out_shape=jax.ShapeDtypeStruct((T, W), bf16),
        compiler_params=_params("parallel"),
    )(z, z, z, z, z, z, cw, cb)


def _conv_bwd(z, dy, cw, cb, dzbuf, tm):
    T = z.shape[0]
    c0 = OFF_CONV // W
    hb = tm // 8
    nt = T // tm

    def body(ax_ref, ab_ref, ac_ref, ag_ref, hx_ref, hc_ref, nb_ref, ng_ref, dy_ref, ndy_ref,
             cw_ref, cb_ref, dzin_ref, dz_ref, gcw_ref, gcb_ref):
        i = pl.program_id(0)

        @pl.when(i == 0)
        def _():
            gcw_ref[...] = jnp.zeros_like(gcw_ref)
            gcb_ref[...] = jnp.zeros_like(gcb_ref)

        ax, ab, ac, ag = ax_ref[...], ab_ref[...], ac_ref[...], ag_ref[...]
        w0, w1, w2 = cw_ref[0:1, :], cw_ref[1:2, :], cw_ref[2:3, :]
        zc = ac * ax
        halo = jnp.where(i > 0, hc_ref[...] * hx_ref[...], 0.0)
        z1, z2 = _conv_taps(zc, halo, cw_ref, tm)
        yb = w2 * zc + w1 * z1 + w0 * z2 + cb_ref[...]
        ya = ab * yb
        dyg = dy_ref[...]
        dag = dyg * ya * _dsilu(ag)
        dya = dyg * _silu(ag)
        dab = dya * yb
        dyc = dya * ab
        nxt = jnp.where(i < nt - 1, ndy_ref[...] * _silu(ng_ref[...]) * nb_ref[...], 0.0)
        ext = jnp.concatenate([dyc, nxt], axis=0)
        d1 = pltpu.roll(ext, tm + 8 - 1, 0)[:tm]
        d2 = pltpu.roll(ext, tm + 8 - 2, 0)[:tm]
        dzc = w2 * dyc + w1 * d1 + w0 * d2
        dz_ref[:, 0:W] = (dzc * ac).astype(bf16)
        dz_ref[:, W:2 * W] = dab.astype(bf16)
        dz_ref[:, 2 * W:3 * W] = (dzc * ax).astype(bf16)
        dz_ref[:, 3 * W:4 * W] = dag.astype(bf16)
        gcb_ref[...] += jnp.sum(dyc, axis=0, keepdims=True)
        gcw_ref[...] += _rows3(jnp.sum(dyc * z2, axis=0, keepdims=True),
                               jnp.sum(dyc * z1, axis=0, keepdims=True),
                               jnp.sum(dyc * zc, axis=0, keepdims=True), W)

    prev_spec = lambda col: pl.BlockSpec((8, W), lambda i, c=col: (jnp.maximum(i * hb - 1, 0), c))
    next_z = lambda col: pl.BlockSpec((8, W), lambda i, c=col: (jnp.minimum((i + 1) * hb, T // 8 - 1), c))
    next_dy = pl.BlockSpec((8, W), lambda i: (jnp.minimum((i + 1) * hb, T // 8 - 1), 0))
    return _pcall(
        body, name="conv_bwd", grid=(nt,),
        in_specs=[_zblock(tm, c0), _zblock(tm, c0 + 1), _zblock(tm, c0 + 2), _zblock(tm, c0 + 3),
                  prev_spec(c0), prev_spec(c0 + 2), next_z(c0 + 1), next_z(c0 + 3),
                  pl.BlockSpec((tm, W), lambda i: (i, 0)), next_dy,
                  pl.BlockSpec((CONV_WIDTH, W), lambda i: (0, 0)),
                  pl.BlockSpec((1, W), lambda i: (0, 0)),
                  pl.BlockSpec(memory_space=pl.ANY)],
        out_specs=[pl.BlockSpec((tm, 4 * W), lambda i: (i, OFF_CONV // (4 * W))),
                   pl.BlockSpec((8, W), lambda i: (0, 0)),
                   pl.BlockSpec((1, W), lambda i: (0, 0))],
        out_shape=[jax.ShapeDtypeStruct((T, NZ), bf16), jax.ShapeDtypeStruct((8, W), f32),
                   jax.ShapeDtypeStruct((1, W), f32)],
        input_output_aliases={12: 0},
        compiler_params=_params("arbitrary"),
    )(z, z, z, z, z, z, z, z, dy, dy, cw, cb, dzbuf)


def _sgu_core(dv_ref, gv_ref, sw_ref, sbe_ref, s_scr, tm):
    v = dv_ref[...]
    gm = _group_mean_matrix(W, DH)
    rv = lax.rsqrt(_group_mean(v * v, gm) + EPS)
    vh = v * rv
    vnb = (vh * gv_ref[...]).astype(bf16)
    causal = _iota2((SGU_CHUNK, SGU_CHUNK), 0) >= _iota2((SGU_CHUNK, SGU_CHUNK), 1)
    wgs = [jnp.where(causal, sw_ref[g], 0.0).astype(bf16) for g in range(NH)]
    for c in range(tm // SGU_CHUNK):
        rows = slice(c * SGU_CHUNK, (c + 1) * SGU_CHUNK)
        for g in range(NH):
            cols = slice(g * DH, (g + 1) * DH)
            s_scr[rows, cols] = _mm(wgs[g], vnb[rows, cols])
    sb = sbe_ref[...]
    s = s_scr[...] + jnp.concatenate([sb] * (tm // SGU_CHUNK), axis=0)
    return v, rv, vh, vnb, wgs, causal, gm, s


def _sgu_fwd(z, gv, sw, sbe, tm):
    T = z.shape[0]
    c0 = OFF_SGU // W

    def body(du_ref, dv_ref, dg_ref, gv_ref, sw_ref, sbe_ref, y_ref, s_scr):
        s = _sgu_core(dv_ref, gv_ref, sw_ref, sbe_ref, s_scr, tm)[-1]
        y_ref[...] = ((du_ref[...] * s) * _silu(dg_ref[...])).astype(bf16)

    return _pcall(
        body, name="sgu_fwd", grid=(T // tm,),
        in_specs=[_zblock(tm, c0), _zblock(tm, c0 + 1), _zblock(tm, c0 + 2),
                  pl.BlockSpec((1, W), lambda i: (0, 0)),
                  pl.BlockSpec((NH, SGU_CHUNK, SGU_CHUNK), lambda i: (0, 0, 0)),
                  pl.BlockSpec((SGU_CHUNK, W), lambda i: (0, 0))],
        out_specs=pl.BlockSpec((tm, W), lambda i: (i, 0)),
        out_shape=jax.ShapeDtypeStruct((T, W), bf16),
        scratch_shapes=[pltpu.VMEM((tm, W), f32)],
        compiler_params=_params("parallel"),
    )(z, z, z, gv, sw, sbe)


def _sgu_bwd(z, dy, gv, sw, sbe, dzbuf, tm):
    T = z.shape[0]
    c0 = OFF_SGU // W
    nt = T // tm

    def body(du_ref, dv_ref, dg_ref, dy_ref, gv_ref, sw_ref, sbe_ref, dzin_ref,
             dz_ref, gsw_ref, gsb_ref, ggv_ref, s_scr, dvn_scr, sb_acc):
        i = pl.program_id(0)

        @pl.when(i == 0)
        def _():
            gsw_ref[...] = jnp.zeros_like(gsw_ref)
            ggv_ref[...] = jnp.zeros_like(ggv_ref)
            sb_acc[...] = jnp.zeros_like(sb_acc)

        v, rv, vh, vnb, wgs, causal, gm, s = _sgu_core(dv_ref, gv_ref, sw_ref, sbe_ref, s_scr, tm)
        du, dg, dyv = du_ref[...], dg_ref[...], dy_ref[...]
        ddg = dyv * (du * s) * _dsilu(dg)
        t = dyv * _silu(dg)
        ddu = t * s
        ds = t * du
        dsb = ds.astype(bf16)
        acc = sb_acc[...]
        for c in range(tm // SGU_CHUNK):
            rows = slice(c * SGU_CHUNK, (c + 1) * SGU_CHUNK)
            acc = acc + ds[rows, :]
            for g in range(NH):
                cols = slice(g * DH, (g + 1) * DH)
                gsw_ref[g] += jnp.where(causal, _mm_nt(dsb[rows, cols], vnb[rows, cols]), 0.0)
                dvn_scr[rows, cols] = _mm_tn(wgs[g], dsb[rows, cols])
        sb_acc[...] = acc
        dvn = dvn_scr[...]
        ggv_ref[...] += jnp.sum(dvn * vh, axis=0, keepdims=True)
        u = dvn * gv_ref[...]
        ddv = rv * u - v * (rv * rv * rv) * _group_mean(u * v, gm)
        dz_ref[:, 0:W] = ddu.astype(bf16)
        dz_ref[:, W:2 * W] = ddv.astype(bf16)
        dz_ref[:, 2 * W:3 * W] = ddg.astype(bf16)

        @pl.when(i == nt - 1)
        def _():
            gsb_ref[...] = _group_mean(sb_acc[...], gm) * float(DH)

    return _pcall(
        body, name="sgu_bwd", grid=(nt,),
        in_specs=[_zblock(tm, c0), _zblock(tm, c0 + 1), _zblock(tm, c0 + 2),
                  pl.BlockSpec((tm, W), lambda i: (i, 0)),
                  pl.BlockSpec((1, W), lambda i: (0, 0)),
                  pl.BlockSpec((NH, SGU_CHUNK, SGU_CHUNK), lambda i: (0, 0, 0)),
                  pl.BlockSpec((SGU_CHUNK, W), lambda i: (0, 0)),
                  pl.BlockSpec(memory_space=pl.ANY)],
        out_specs=[pl.BlockSpec((tm, 3 * W), lambda i: (i, OFF_SGU // (3 * W))),
                   pl.BlockSpec((NH, SGU_CHUNK, SGU_CHUNK), lambda i: (0, 0, 0)),
                   pl.BlockSpec((SGU_CHUNK, W), lambda i: (0, 0)),
                   pl.BlockSpec((1, W), lambda i: (0, 0))],
        out_shape=[jax.ShapeDtypeStruct((T, NZ), bf16),
                   jax.ShapeDtypeStruct((NH, SGU_CHUNK, SGU_CHUNK), f32),
                   jax.ShapeDtypeStruct((SGU_CHUNK, W), f32),
                   jax.ShapeDtypeStruct((1, W), f32)],
        scratch_shapes=[pltpu.VMEM((tm, W), f32), pltpu.VMEM((tm, W), f32), pltpu.VMEM((SGU_CHUNK, W), f32)],
        input_output_aliases={7: 0},
        compiler_params=_params("arbitrary"),
    )(z, z, z, dy, gv, sw, sbe, dzbuf)


def _hgrn_gates(cq_ref, cf_ref, lb_ref):
    q = _silu(cq_ref[...])
    sig = _sigmoid(cf_ref[...])
    lb = lb_ref[...]
    g = lb + (1.0 - lb) * sig
    return q, sig, g, jnp.log(g), (1.0 - lb) * (1.0 - sig)


def _hgrn_chunk_terms(lgc, qc, kc):
    C = GLA_CHUNK
    b = jnp.dot(_lower_tri(C), lgc, precision=HI, preferred_element_type=f32)
    bl = jnp.sum(lgc, axis=0, keepdims=True)
    mid = jnp.sum(jnp.where(_iota2((C, W), 0) <= C // 2, lgc, 0.0), axis=0, keepdims=True)
    eb = jnp.exp(b)
    em = jnp.exp(jnp.minimum(b - mid, EXP_CLAMP))
    emi = jnp.exp(jnp.minimum(mid - b, EXP_CLAMP))
    ek = jnp.exp(bl - b)
    return dict(eb=eb, em=em, emi=emi, ek=ek, ebl=jnp.exp(bl),
                qe=qc * eb, qm=qc * em, km=kc * emi, kd=kc * ek)


def _hgrn_fwd(z, lb, gain, tm):
    T = z.shape[0]
    c0 = OFF_HGRN // W
    C = GLA_CHUNK
    ncp = tm // C

    def body(cq_ref, cf_ref, ci_ref, cg_ref, lb_ref, gn_ref, y_ref, o_ref, st_ref, state, o_scr):
        @pl.when(pl.program_id(0) == 0)
        def _():
            state[...] = jnp.zeros_like(state)

        q, sig, g, lg, kf = _hgrn_gates(cq_ref, cf_ref, lb_ref)
        v = ci_ref[...]
        causal = _iota2((C, C), 0) >= _iota2((C, C), 1)
        for c in range(ncp):
            rows = slice(c * C, (c + 1) * C)
            tr = _hgrn_chunk_terms(lg[rows], q[rows], kf[rows])
            vb = v[rows].astype(bf16)
            qmb, kmb, qeb, kdb = (tr[n].astype(bf16) for n in ("qm", "km", "qe", "kd"))
            for h in range(NH):
                cols = slice(h * DH, (h + 1) * DH)
                hr = slice(h * DH, (h + 1) * DH)
                st = state[hr, :]
                st_ref[c, hr, :] = st
                p = jnp.where(causal, _mm_nt(qmb[:, cols], kmb[:, cols]), 0.0)
                o_scr[rows, cols] = _mm(p.astype(bf16), vb[:, cols]) + _mm_nt(qeb[:, cols], st.astype(bf16))
                state[hr, :] = st * tr["ebl"][:, cols] + _mm_tn(vb[:, cols], kdb[:, cols])
        o = o_scr[...]
        o_ref[...] = o
        gm = _group_mean_matrix(W, DH)
        r = lax.rsqrt(_group_mean(o * o, gm) + EPS)
        y_ref[...] = ((o * r * gn_ref[...]) * _silu(cg_ref[...])).astype(bf16)

    return _pcall(
        body, name="hgrn_fwd", grid=(T // tm,),
        in_specs=[_zblock(tm, c0), _zblock(tm, c0 + 1), _zblock(tm, c0 + 2), _zblock(tm, c0 + 3),
                  pl.BlockSpec((1, W), lambda i: (0, 0)), pl.BlockSpec((1, W), lambda i: (0, 0))],
        out_specs=[pl.BlockSpec((tm, W), lambda i: (i, 0)),
                   pl.BlockSpec((tm, W), lambda i: (i, 0)),
                   pl.BlockSpec((ncp, W, DH), lambda i: (i, 0, 0))],
        out_shape=[jax.ShapeDtypeStruct((T, W), bf16), jax.ShapeDtypeStruct((T, W), f32),
                   jax.ShapeDtypeStruct((T // C, W, DH), f32)],
        scratch_shapes=[pltpu.VMEM((W, DH), f32), pltpu.VMEM((tm, W), f32)],
        compiler_params=_params("arbitrary"),
    )(z, z, z, z, lb, gain)


def _hgrn_bwd(z, lb, gain, o_pre, states, dy, dzbuf, tm):
    T = z.shape[0]
    c0 = OFF_HGRN // W
    C = GLA_CHUNK
    ncp = tm // C
    nt = T // tm

    def body(cq_ref, cf_ref, ci_ref, cg_ref, lb_ref, gn_ref, o_ref, st_ref, dy_ref, dzin_ref,
             dz_ref, ggn_ref, glb_ref, dstate, dq_s, dk_s, dv_s, db_s):
        @pl.when(pl.program_id(0) == 0)
        def _():
            dstate[...] = jnp.zeros_like(dstate)
            ggn_ref[...] = jnp.zeros_like(ggn_ref)
            glb_ref[...] = jnp.zeros_like(glb_ref)

        cq, cg = cq_ref[...], cg_ref[...]
        q, sig, g, lg, kf = _hgrn_gates(cq_ref, cf_ref, lb_ref)
        lb = lb_ref[...]
        v = ci_ref[...]
        o = o_ref[...]
        gm = _group_mean_matrix(W, DH)
        r = lax.rsqrt(_group_mean(o * o, gm) + EPS)
        oh = o * r
        gn = gn_ref[...]
        dyv = dy_ref[...]
        dcg = dyv * (oh * gn) * _dsilu(cg)
        don = dyv * _silu(cg)
        ggn_ref[...] += jnp.sum(don * oh, axis=0, keepdims=True)
        u = don * gn
        do = r * u - o * (r * r * r) * _group_mean(u * o, gm)

        causal = _iota2((C, C), 0) >= _iota2((C, C), 1)
        last_row = _iota2((C, DH), 0) == C - 1
        for c in reversed(range(ncp)):
            rows = slice(c * C, (c + 1) * C)
            tr = _hgrn_chunk_terms(lg[rows], q[rows], kf[rows])
            vb = v[rows].astype(bf16)
            dob = do[rows].astype(bf16)
            qmb, kmb, qeb, kdb = (tr[n].astype(bf16) for n in ("qm", "km", "qe", "kd"))
            for h in range(NH):
                cols = slice(h * DH, (h + 1) * DH)
                hr = slice(h * DH, (h + 1) * DH)
                st0 = st_ref[c, hr, :]
                dst = dstate[hr, :]
                dstb = dst.astype(bf16)
                doh = dob[:, cols]
                p = jnp.where(causal, _mm_nt(qmb[:, cols], kmb[:, cols]), 0.0)
                dp = jnp.where(causal, _mm_nt(doh, vb[:, cols]), 0.0)
                dpb = dp.astype(bf16)
                dvh = _mm_tn(p.astype(bf16), doh) + _mm_nt(kdb[:, cols], dstb)
                dqm = _mm(dpb, kmb[:, cols])
                dkm = _mm_tn(dpb, qmb[:, cols])
                dqe = _mm(doh, st0.astype(bf16))
                dkd = _mm(vb[:, cols], dstb)
                ebl = tr["ebl"][:, cols]
                dstate[hr, :] = dst * ebl + _mm_tn(doh, qeb[:, cols])
                qm, km, qe, kd = (a[:, cols].astype(f32) for a in (qmb, kmb, qeb, kdb))
                kterm = dkd * kd
                dbh = dqm * qm - dkm * km + dqe * qe - kterm
                extra = jnp.sum(kterm, axis=0, keepdims=True) + ebl * jnp.sum(dst * st0, axis=0, keepdims=True)
                dbh = dbh + jnp.where(last_row, extra, 0.0)
                dq_s[rows, cols] = dqm * tr["em"][:, cols] + dqe * tr["eb"][:, cols]
                dk_s[rows, cols] = dkm * tr["emi"][:, cols] + dkd * tr["ek"][:, cols]
                dv_s[rows, cols] = dvh
                db_s[rows, cols] = dbh
            db_s[rows, :] = jnp.dot(_upper_tri(C), db_s[rows, :], precision=HI, preferred_element_type=f32)
        dlg = db_s[...]
        dk = dk_s[...]
        dsig = sig * (1.0 - sig)
        one_lb = 1.0 - lb
        dcf = (dlg / g - dk) * one_lb * dsig
        glb_ref[...] += jnp.sum((dlg / g - dk) * (1.0 - sig), axis=0, keepdims=True)
        dz_ref[:, 0:W] = (dq_s[...] * _dsilu(cq)).astype(bf16)
        dz_ref[:, W:2 * W] = dcf.astype(bf16)
        dz_ref[:, 2 * W:3 * W] = dv_s[...].astype(bf16)
        dz_ref[:, 3 * W:4 * W] = dcg.astype(bf16)

    rev = lambda i: nt - 1 - i
    zb = lambda col: pl.BlockSpec((tm, W), lambda i, c=col: (rev(i), c))
    return _pcall(
        body, name="hgrn_bwd", grid=(nt,),
        in_specs=[zb(c0), zb(c0 + 1), zb(c0 + 2), zb(c0 + 3),
                  pl.BlockSpec((1, W), lambda i: (0, 0)), pl.BlockSpec((1, W), lambda i: (0, 0)),
                  pl.BlockSpec((tm, W), lambda i: (rev(i), 0)),
                  pl.BlockSpec((ncp, W, DH), lambda i: (rev(i), 0, 0)),
                  pl.BlockSpec((tm, W), lambda i: (rev(i), 0)),
                  pl.BlockSpec(memory_space=pl.ANY)],
        out_specs=[pl.BlockSpec((tm, 4 * W), lambda i: (rev(i), OFF_HGRN // (4 * W))),
                   pl.BlockSpec((1, W), lambda i: (0, 0)),
                   pl.BlockSpec((1, W), lambda i: (0, 0))],
        out_shape=[jax.ShapeDtypeStruct((T, NZ), bf16), jax.ShapeDtypeStruct((1, W), f32),
                   jax.ShapeDtypeStruct((1, W), f32)],
        scratch_shapes=[pltpu.VMEM((W, DH), f32)] + [pltpu.VMEM((tm, W), f32)] * 4,
        input_output_aliases={9: 0},
        compiler_params=_params("arbitrary"),
    )(z, z, z, z, lb, gain, o_pre, states, dy, dzbuf)


def _attn_prep(z, fbias, gq, gk, tm):
    T = z.shape[0]
    c0 = OFF_ATT // W

    def body(q_ref, k_ref, v_ref, f_ref, fb_ref, gq_ref, gk_ref, qt_ref, kt_ref, vt_ref, kh_ref, vh_ref, cum_ref,
             carry):
        @pl.when(pl.program_id(0) == 0)
        def _():
            carry[...] = jnp.zeros_like(carry)

        gm = _group_mean_matrix(W, DH)
        q, k, v = q_ref[...], k_ref[...], v_ref[...]
        qs = q * lax.rsqrt(_group_mean(q * q, gm) + EPS) * (gq_ref[...] * (DH ** -0.5))
        kn = k * lax.rsqrt(_group_mean(k * k, gm) + EPS) * gk_ref[...]
        qt_ref[...] = qs.T.astype(bf16)
        kt_ref[...] = kn.T.astype(bf16)
        vt_ref[...] = v.T.astype(bf16)
        for h in range(NH):
            cols = slice(h * DH, (h + 1) * DH)
            kh_ref[h] = kn[:, cols].astype(bf16)
            vh_ref[h] = v[:, cols].astype(bf16)
        ls = _logsigmoid(f_ref[...] + fb_ref[...])
        cum_ref[...] = jnp.dot(_lower_tri(tm), ls, precision=HI, preferred_element_type=f32) + carry[...]
        carry[...] += jnp.sum(ls, axis=0, keepdims=True)

    hspec = pl.BlockSpec((NH, tm, DH), lambda i: (0, i, 0))
    tspec = pl.BlockSpec((W, tm), lambda i: (0, i))
    return _pcall(
        body, name="attn_prep", grid=(T // tm,),
        in_specs=[_zblock(tm, c0), _zblock(tm, c0 + 1), _zblock(tm, c0 + 2),
                  pl.BlockSpec((tm, 128), lambda i: (i, OFF_F // 128)),
                  pl.BlockSpec((1, 128), lambda i: (0, 0)),
                  pl.BlockSpec((1, W), lambda i: (0, 0)), pl.BlockSpec((1, W), lambda i: (0, 0))],
        out_specs=[tspec, tspec, tspec, hspec, hspec, pl.BlockSpec((tm, 128), lambda i: (i, 0))],
        out_shape=[jax.ShapeDtypeStruct((W, T), bf16)] * 3 + [jax.ShapeDtypeStruct((NH, T, DH), bf16)] * 2
        + [jax.ShapeDtypeStruct((T, 128), f32)],
        scratch_shapes=[pltpu.VMEM((1, 128), f32)],
        compiler_params=_params("arbitrary"),
    )(z, z, z, z, fbias, gq, gk)


HP = 2


def _causal_pairs(nq, key_major):
    if key_major:
        pairs = [(qi, ki) for ki in range(nq) for qi in range(ki, nq)]
    else:
        pairs = [(qi, ki) for qi in range(nq) for ki in range(qi + 1)]
    return (jnp.asarray([p[0] for p in pairs], jnp.int32), jnp.asarray([p[1] for p in pairs], jnp.int32))


def _head_rows(rows, n):
    return jnp.concatenate([jnp.broadcast_to(r, (DH, n)) for r in rows], axis=0)


def _attn_fwd(qt, kh, vt, crow, ccol, bq):
    T = qt.shape[1]
    nq = T // bq
    bk = bq
    qs, ks = _causal_pairs(nq, key_major=False)
    BW = HP * DH

    def body(qs_ref, ks_ref, qt_ref, k_ref, vt_ref, cr_ref, cc_ref, o_ref, lse_ref, m_s, l_s, acc_s):
        i = pl.program_id(1)
        qi, ki = qs_ref[i], ks_ref[i]

        @pl.when(ki == 0)
        def _():
            m_s[...] = jnp.full_like(m_s, MASK_VALUE)
            l_s[...] = jnp.zeros_like(l_s)
            acc_s[...] = jnp.zeros_like(acc_s)

        def step(diagonal):
            for h in range(HP):
                rows = slice(h * DH, (h + 1) * DH)
                s = _mm(k_ref[h], qt_ref[rows, :]) + cr_ref[h] - cc_ref[h]
                if diagonal:
                    s = jnp.where(_iota2((bk, bq), 0) <= _iota2((bk, bq), 1), s, MASK_VALUE)
                m_old = m_s[h]
                m_new = jnp.maximum(m_old, jnp.max(s, axis=0, keepdims=True))
                p = jnp.exp(s - m_new)
                alpha = jnp.exp(m_old - m_new)
                l_s[h] = alpha * l_s[h] + jnp.sum(p, axis=0, keepdims=True)
                acc_s[rows, :] = alpha * acc_s[rows, :] + _mm(vt_ref[rows, :], p.astype(bf16))
                m_s[h] = m_new

        @pl.when(ki < qi)
        def _():
            step(False)

        @pl.when(ki == qi)
        def _():
            step(True)
            o_ref[...] = (acc_s[...] / _head_rows([l_s[h] for h in range(HP)], bq)).T
            for h in range(HP):
                lse_ref[h] = m_s[h] + jnp.log(l_s[h])

    qcol = lambda hp, i, qs, ks: (hp, qs[i])
    kcol = lambda hp, i, qs, ks: (hp, ks[i])
    qrow = lambda hp, i, qs, ks: (hp, 0, qs[i])
    return _pcall(
        body, name="attn_fwd",
        grid_spec=pltpu.PrefetchScalarGridSpec(
            num_scalar_prefetch=2, grid=(NH // HP, qs.shape[0]),
            in_specs=[pl.BlockSpec((BW, bq), qcol),
                      pl.BlockSpec((HP, bk, DH), lambda hp, i, qs, ks: (hp, ks[i], 0)),
                      pl.BlockSpec((BW, bk), kcol),
                      pl.BlockSpec((HP, 1, bq), qrow),
                      pl.BlockSpec((HP, bk, 1), lambda hp, i, qs, ks: (hp, ks[i], 0))],
            out_specs=[pl.BlockSpec((bq, BW), lambda hp, i, qs, ks: (qs[i], hp)),
                       pl.BlockSpec((HP, 1, bq), qrow)],
            scratch_shapes=[pltpu.VMEM((HP, 1, bq), f32), pltpu.VMEM((HP, 1, bq), f32),
                            pltpu.VMEM((BW, bq), f32)]),
        out_shape=[jax.ShapeDtypeStruct((T, W), f32), jax.ShapeDtypeStruct((NH, 1, T), f32)],
        compiler_params=_params("parallel", "arbitrary"),
    )(qs, ks, qt, kh, vt, crow, ccol)


def _attn_bwd_prep(dy, oh, z, tm):
    T = dy.shape[0]
    cg = OFF_ATT // W + 3

    def body(dy_ref, o_ref, g_ref, dot_ref, dl_ref):
        do = (dy_ref[...] * _silu(g_ref[...])).astype(bf16)
        dot_ref[...] = do.astype(f32).T.astype(bf16)
        prod = (do.astype(f32) * o_ref[...]).T
        for h in range(NH):
            dl_ref[h] = jnp.sum(prod[h * DH:(h + 1) * DH, :], axis=0, keepdims=True)

    return _pcall(
        body, name="attn_bwd_prep", grid=(T // tm,),
        in_specs=[pl.BlockSpec((tm, W), lambda i: (i, 0)),
                  pl.BlockSpec((tm, W), lambda i: (i, 0)),
                  _zblock(tm, cg)],
        out_specs=[pl.BlockSpec((W, tm), lambda i: (0, i)),
                   pl.BlockSpec((NH, 1, tm), lambda i: (0, 0, i))],
        out_shape=[jax.ShapeDtypeStruct((W, T), bf16), jax.ShapeDtypeStruct((NH, 1, T), f32)],
        compiler_params=_params("parallel"),
    )(dy, oh, z)


def _attn_bwd(qt, kt, kh, vh, crow, ccol, dot, lse, delta, bq):
    T = qt.shape[1]
    nq = T // bq
    bk = bq
    qs, ks = _causal_pairs(nq, key_major=True)
    BW = HP * DH

    def body(qs_ref, ks_ref, qt_ref, kt_ref, k_ref, v_ref, cr_ref, cc_ref, dot_ref, lse_ref, dl_ref,
             dq_ref, dk_ref, dv_ref, dck_ref, dcq_ref, dq_s, dk_s, dv_s, dck_s):
        i = pl.program_id(1)
        qi, ki = qs_ref[i], ks_ref[i]

        @pl.when(i == 0)
        def _():
            dq_s[...] = jnp.zeros_like(dq_s)
            dcq_ref[...] = jnp.zeros_like(dcq_ref)

        @pl.when(qi == ki)
        def _():
            dk_s[...] = jnp.zeros_like(dk_s)
            dv_s[...] = jnp.zeros_like(dv_s)
            dck_s[...] = jnp.zeros_like(dck_s)

        def step(diagonal):
            colsums = []
            for h in range(HP):
                rows = slice(h * DH, (h + 1) * DH)
                qth, doth = qt_ref[rows, :], dot_ref[rows, :]
                p = jnp.exp(_mm(k_ref[h], qth) + (cr_ref[h] - lse_ref[h]) - cc_ref[h])
                if diagonal:
                    p = jnp.where(_iota2((bk, bq), 0) <= _iota2((bk, bq), 1), p, 0.0)
                dv_s[rows, :] += _mm_nt(doth, p.astype(bf16))
                ds = p * (_mm(v_ref[h], doth) - dl_ref[h])
                dsb = ds.astype(bf16)
                dk_s[rows, :] += _mm_nt(qth, dsb)
                dq_s[qi, rows, :] += _mm(kt_ref[rows, :], dsb)
                part = ds[:, 0:128]
                for c in range(1, bq // 128):
                    part = part + ds[:, c * 128:(c + 1) * 128]
                dck_s[h] += part
                colsums.append(jnp.sum(ds, axis=0, keepdims=True))
            dcq_ref[qi] += _stack_rows(colsums, bq)

        @pl.when(qi > ki)
        def _():
            step(False)

        @pl.when(qi == ki)
        def _():
            step(True)

        @pl.when(qi == nq - 1)
        def _():
            dk_ref[...] = dk_s[...].T
            dv_ref[...] = dv_s[...].T
            for h in range(HP):
                dck_ref[h] = -jnp.sum(dck_s[h], axis=1, keepdims=True)

        @pl.when(i == qs.shape[0] - 1)
        def _():
            for qb in range(nq):
                dq_ref[qb * bq:(qb + 1) * bq, :] = dq_s[qb].T

    qcol = lambda hp, i, qs, ks: (hp, qs[i])
    kcol = lambda hp, i, qs, ks: (hp, ks[i])
    qrow = lambda hp, i, qs, ks: (hp, 0, qs[i])
    kh_spec = pl.BlockSpec((HP, bk, DH), lambda hp, i, qs, ks: (hp, ks[i], 0))
    return _pcall(
        body, name="attn_bwd",
        grid_spec=pltpu.PrefetchScalarGridSpec(
            num_scalar_prefetch=2, grid=(NH // HP, qs.shape[0]),
            in_specs=[pl.BlockSpec((BW, bq), qcol), pl.BlockSpec((BW, bk), kcol), kh_spec, kh_spec,
                      pl.BlockSpec((HP, 1, bq), qrow),
                      pl.BlockSpec((HP, bk, 1), lambda hp, i, qs, ks: (hp, ks[i], 0)),
                      pl.BlockSpec((BW, bq), qcol), pl.BlockSpec((HP, 1, bq), qrow), pl.BlockSpec((HP, 1, bq), qrow)],
            out_specs=[pl.BlockSpec((T, BW), lambda hp, i, qs, ks: (0, hp)),
                       pl.BlockSpec((bk, BW), lambda hp, i, qs, ks: (ks[i], hp)),
                       pl.BlockSpec((bk, BW), lambda hp, i, qs, ks: (ks[i], hp)),
                       pl.BlockSpec((HP, bk, 1), lambda hp, i, qs, ks: (hp, ks[i], 0)),
                       pl.BlockSpec((None, nq, 8, bq), lambda hp, i, qs, ks: (hp, 0, 0, 0))],
            scratch_shapes=[pltpu.VMEM((nq, BW, bq), f32), pltpu.VMEM((BW, bk), f32), pltpu.VMEM((BW, bk), f32),
                            pltpu.VMEM((HP, bk, 128), f32)]),
        out_shape=[jax.ShapeDtypeStruct((T, W), f32)] * 3 + [jax.ShapeDtypeStruct((NH, T, 1), f32),
                                                             jax.ShapeDtypeStruct((NH // HP, nq, 8, bq), f32)],
        compiler_params=_params("parallel", "arbitrary"),
    )(qs, ks, qt, kt, kh, vh, crow, ccol, dot, lse, delta)


def _attn_post(z, dy, oh, dqh, dkh, dvh, dck, dcq, fbias, gq, gk, dzbuf, tm):
    T = z.shape[0]
    c0 = OFF_ATT // W
    nt = T // tm

    def body(q_ref, k_ref, g_ref, f_ref, dy_ref, o_ref, dq_ref, dk_ref, dv_ref, dck_ref, dcq_ref, fb_ref, gq_ref,
             gk_ref, dzin_ref, dz_ref, ggq_ref, ggk_ref, gfb_ref, carry):
        @pl.when(pl.program_id(0) == 0)
        def _():
            carry[...] = jnp.zeros_like(carry)
            ggq_ref[...] = jnp.zeros_like(ggq_ref)
            ggk_ref[...] = jnp.zeros_like(ggk_ref)
            gfb_ref[...] = jnp.zeros_like(gfb_ref)

        gm = _group_mean_matrix(W, DH)
        hs = jnp.where((_iota2((W, W), 0) & (DH - 1)) == (_iota2((W, W), 1) & (DH - 1)), 1.0, 0.0).astype(f32)

        def norm_bwd(x, dn, gain):
            r = lax.rsqrt(_group_mean(x * x, gm) + EPS)
            gg = jnp.sum(dn * x * r, axis=0, keepdims=True)
            u = dn * gain
            return r * u - x * (r * r * r) * _group_mean(u * x, gm), gg

        q, k, gate = q_ref[...], k_ref[...], g_ref[...]
        dq, ggq = norm_bwd(q, dq_ref[...] * (DH ** -0.5), gq_ref[...])
        dk, ggk = norm_bwd(k, dk_ref[...], gk_ref[...])
        ggq_ref[...] += jnp.dot(jnp.broadcast_to(ggq, (8, W)), hs, precision=HI, preferred_element_type=f32)[0:1]
        ggk_ref[...] += jnp.dot(jnp.broadcast_to(ggk, (8, W)), hs, precision=HI, preferred_element_type=f32)[0:1]
        dgate = dy_ref[...] * o_ref[...] * _dsilu(gate)
        dck_v = dck_ref[...] + dcq_ref[...]
        rc = jnp.dot(_upper_tri(tm), dck_v, precision=HI, preferred_element_type=f32) + carry[...]
        carry[...] += jnp.sum(dck_v, axis=0, keepdims=True)
        f = f_ref[...] + fb_ref[...]
        df = jnp.where(_iota2((tm, 128), 1) < NH, rc * _sigmoid(-f), 0.0)
        gfb_ref[...] += jnp.sum(df, axis=0, keepdims=True)
        dz_ref[:, 0:W] = dq.astype(bf16)
        dz_ref[:, W:2 * W] = dk.astype(bf16)
        dz_ref[:, 2 * W:3 * W] = dv_ref[...].astype(bf16)
        dz_ref[:, 3 * W:4 * W] = dgate.astype(bf16)
        dz_ref[:, 4 * W:4 * W + 128] = df.astype(bf16)

    rev = lambda i: nt - 1 - i
    zb = lambda col: pl.BlockSpec((tm, W), lambda i, c=col: (rev(i), c))
    hspec = pl.BlockSpec((tm, W), lambda i: (rev(i), 0))
    return _pcall(
        body, name="attn_post", grid=(nt,),
        in_specs=[zb(c0), zb(c0 + 1), zb(c0 + 3),
                  pl.BlockSpec((tm, 128), lambda i: (rev(i), OFF_F // 128)),
                  pl.BlockSpec((tm, W), lambda i: (rev(i), 0)),
                  hspec, hspec, hspec, hspec,
                  pl.BlockSpec((tm, 128), lambda i: (rev(i), 0)),
                  pl.BlockSpec((tm, 128), lambda i: (rev(i), 0)),
                  pl.BlockSpec((1, 128), lambda i: (0, 0)),
                  pl.BlockSpec((1, W), lambda i: (0, 0)), pl.BlockSpec((1, W), lambda i: (0, 0)),
                  pl.BlockSpec(memory_space=pl.ANY)],
        out_specs=[pl.BlockSpec((tm, 4 * W + 128), lambda i: (rev(i), OFF_ATT // (4 * W + 128))),
                   pl.BlockSpec((1, W), lambda i: (0, 0)), pl.BlockSpec((1, W), lambda i: (0, 0)),
                   pl.BlockSpec((1, 128), lambda i: (0, 0))],
        out_shape=[jax.ShapeDtypeStruct((T, NZ), bf16), jax.ShapeDtypeStruct((1, W), f32),
                   jax.ShapeDtypeStruct((1, W), f32), jax.ShapeDtypeStruct((1, 128), f32)],
        scratch_shapes=[pltpu.VMEM((1, 128), f32)],
        input_output_aliases={14: 0},
        compiler_params=_params("arbitrary"),
    )(z, z, z, z, dy, oh, dqh, dkh, dvh, dck, dcq, fbias, gq, gk, dzbuf)


def _merge_fwd(ya, oh, z, yc, yd, mb, x, p, wup, wo, gp, wpg, wpp, tm):
    T = x.shape[0]
    cg = OFF_ATT // W + 3

    def body(ya_ref, oh_ref, bg_ref, yc_ref, yd_ref, ml_ref, mb_ref, x_ref, p_ref, wup_ref, wo_ref, gp_ref,
             wpg_ref, wpp_ref, yb_ref, mg_ref, x1_ref, x2_ref):
        yb = (oh_ref[...] * _silu(bg_ref[...])).astype(bf16)
        yb_ref[...] = yb
        ys = (ya_ref[...], yb, yc_ref[...], yd_ref[...])
        merged = jnp.zeros((tm, D), f32)
        for b in range(NBR):
            sg = _sigmoid(ml_ref[:, b * D:(b + 1) * D] + mb_ref[b:b + 1, :])
            merged = merged + sg * _mm(ys[b], wup_ref[b])
        mgb = merged.astype(bf16)
        mg_ref[...] = mgb
        x1 = x_ref[...] + _mm(mgb, wo_ref[...])
        x1_ref[...] = x1
        r = lax.rsqrt(jnp.mean(x1 * x1, axis=-1, keepdims=True) + EPS)
        hp = (x1 * r * gp_ref[...]).astype(bf16)
        gate = _sigmoid(_mm(hp, wpg_ref[...]))
        x2_ref[...] = x1 + gate * _mm(p_ref[...].astype(bf16), wpp_ref[...])

    row = lambda width: pl.BlockSpec((tm, width), lambda i: (i, 0))
    full = lambda *shape: pl.BlockSpec(shape, lambda i: (0,) * len(shape))
    return _pcall(
        body, name="merge_fwd", grid=(T // tm,),
        in_specs=[row(W), row(W), _zblock(tm, cg), row(W), row(W),
                  pl.BlockSpec((tm, NBR * D), lambda i: (i, 0)), full(NBR, D), row(D), row(PLE),
                  full(NBR, W, D), full(D, D), full(1, D), full(D, D), full(PLE, D)],
        out_specs=[row(W), row(D), row(D), row(D)],
        out_shape=[jax.ShapeDtypeStruct((T, W), bf16), jax.ShapeDtypeStruct((T, D), bf16),
                   jax.ShapeDtypeStruct((T, D), f32), jax.ShapeDtypeStruct((T, D), f32)],
        compiler_params=_params("parallel"),
    )(ya, oh, z, yc, yd, z, mb, x, p, wup, wo, gp, wpg, wpp)


def _layer_slabs(li, bufs):
    if bufs is None:
        return [], []
    return list(bufs), [pl.BlockSpec(memory_space=pl.ANY)] * len(bufs)


def _ple_bwd(dx2, x1, p, gp, wpg, wpp, tm, li, bufs):
    T = x1.shape[0]
    SH = D // N_DEV
    nt = T // tm
    extra, extra_specs = _layer_slabs(li, bufs)

    def body(dx2_ref, x1_ref, p_ref, gp_ref, wpg_ref, wpp_ref, *rest):
        dx1_ref, gwpg_ref, gwpp_ref, ggp_ref, gwpg_acc, gwpp_acc = rest[len(extra):]

        @pl.when(pl.program_id(0) == 0)
        def _():
            gwpg_acc[...] = jnp.zeros_like(gwpg_acc)
            gwpp_acc[...] = jnp.zeros_like(gwpp_acc)
            ggp_ref[...] = jnp.zeros_like(ggp_ref)

        x1, dx2 = x1_ref[...], dx2_ref[...]
        r = lax.rsqrt(jnp.mean(x1 * x1, axis=-1, keepdims=True) + EPS)
        xh = x1 * r
        gp = gp_ref[...]
        hp = (xh * gp).astype(bf16)
        gate = _sigmoid(_mm(hp, wpg_ref[...]))
        pb = p_ref[...].astype(bf16)
        pp = _mm(pb, wpp_ref[...])
        dpre = ((dx2 * pp) * gate * (1.0 - gate)).astype(bf16)
        gwpp_acc[...] += _mm_tn(pb, (dx2 * gate).astype(bf16))
        gwpg_acc[...] += _mm_tn(hp, dpre)
        dhp = _mm_nt(dpre, wpg_ref[...])
        ggp_ref[...] += jnp.sum(dhp * xh, axis=0, keepdims=True)
        u = dhp * gp
        dx1_ref[...] = dx2 + r * u - x1 * (r * r * r) * jnp.mean(u * x1, axis=-1, keepdims=True)

        @pl.when(pl.program_id(0) == nt - 1)
        def _():
            gwpg_ref[...] = gwpg_acc[...].reshape(N_DEV, SH, D).astype(bf16)
            for d in range(N_DEV):
                gwpp_ref[d] = gwpp_acc[:, d * SH:(d + 1) * SH].astype(bf16)

    row = lambda width: pl.BlockSpec((tm, width), lambda i: (i, 0))
    full = lambda *shape: pl.BlockSpec(shape, lambda i: (0,) * len(shape))
    n_in = 6
    return _pcall(
        body, name="ple_bwd", grid=(nt,),
        in_specs=[row(D), row(D), row(PLE), full(1, D), full(D, D), full(PLE, D)] + extra_specs,
        out_specs=[row(D), pl.BlockSpec((N_DEV, SH, D), lambda i: (0, li, 0)),
                   pl.BlockSpec((N_DEV, PLE, SH), lambda i: (0, li, 0)), full(1, D)],
        out_shape=[jax.ShapeDtypeStruct((T, D), f32), jax.ShapeDtypeStruct((N_DEV, DEPTH * SH, D), bf16),
                   jax.ShapeDtypeStruct((N_DEV, DEPTH * PLE, SH), bf16), jax.ShapeDtypeStruct((1, D), f32)],
        scratch_shapes=[pltpu.VMEM((D, D), f32), pltpu.VMEM((PLE, D), f32)],
        input_output_aliases={n_in + k: 1 + k for k in range(len(extra))},
        compiler_params=_params("arbitrary"),
    )(dx2, x1, p, gp, wpg, wpp, *extra)


def _merge_bwd(dx1, mg, ya, yb, yc, yd, z, mb, wup, wo, tm, li, bufs):
    T = dx1.shape[0]
    SH = D // N_DEV
    nt = T // tm
    extra, extra_specs = _layer_slabs(li, bufs)

    def body(dx1_ref, mg_ref, ya_ref, yb_ref, yc_ref, yd_ref, ml_ref, mb_ref, wup_ref, wo_ref, *rest):
        dml_ref, dya_ref, dyb_ref, dyc_ref, dyd_ref, gwo_ref, gwup_ref, gmb_ref, gwo_acc, gwup_acc = rest[len(extra):]

        @pl.when(pl.program_id(0) == 0)
        def _():
            gwo_acc[...] = jnp.zeros_like(gwo_acc)
            gwup_acc[...] = jnp.zeros_like(gwup_acc)
            gmb_ref[...] = jnp.zeros_like(gmb_ref)

        dx1b = dx1_ref[...].astype(bf16)
        gwo_acc[...] += _mm_tn(mg_ref[...], dx1b)
        dm = _mm_nt(dx1b, wo_ref[...])
        ys = (ya_ref, yb_ref, yc_ref, yd_ref)
        dys = (dya_ref, dyb_ref, dyc_ref, dyd_ref)
        for b in range(NBR):
            y = ys[b][...]
            up = _mm(y, wup_ref[b])
            sg = _sigmoid(ml_ref[:, b * D:(b + 1) * D] + mb_ref[b:b + 1, :])
            dup = (dm * sg).astype(bf16)
            dml = dm * up * sg * (1.0 - sg)
            gmb_ref[b:b + 1, :] += jnp.sum(dml, axis=0, keepdims=True)
            dml_ref[:, b * D:(b + 1) * D] = dml.astype(bf16)
            gwup_acc[b] += _mm_tn(y, dup)
            dys[b][...] = _mm_nt(dup, wup_ref[b])

        @pl.when(pl.program_id(0) == nt - 1)
        def _():
            gwo_ref[...] = gwo_acc[...].reshape(N_DEV, SH, D).astype(bf16)
            for d in range(N_DEV):
                gwup_ref[d] = gwup_acc[:, :, d * SH:(d + 1) * SH].reshape(NBR * W, SH).astype(bf16)

    row = lambda width: pl.BlockSpec((tm, width), lambda i: (i, 0))
    full = lambda *shape: pl.BlockSpec(shape, lambda i: (0,) * len(shape))
    n_in = 10
    return _pcall(
        body, name="merge_bwd", grid=(nt,),
        in_specs=[row(D), row(D), row(W), row(W), row(W), row(W), row(NBR * D), full(NBR, D),
                  full(NBR, W, D), full(D, D)] + extra_specs,
        out_specs=[row(NBR * D), row(W), row(W), row(W), row(W),
                   pl.BlockSpec((N_DEV, SH, D), lambda i: (0, li, 0)),
                   pl.BlockSpec((N_DEV, NBR * W, SH), lambda i: (0, li, 0)), full(NBR, D)],
        out_shape=[jax.ShapeDtypeStruct((T, NZ), bf16)] + [jax.ShapeDtypeStruct((T, W), f32)] * 4
        + [jax.ShapeDtypeStruct((N_DEV, DEPTH * SH, D), bf16),
           jax.ShapeDtypeStruct((N_DEV, DEPTH * NBR * W, SH), bf16),
           jax.ShapeDtypeStruct((NBR, D), f32)],
        scratch_shapes=[pltpu.VMEM((D, D), f32), pltpu.VMEM((NBR, W, D), f32)],
        input_output_aliases={n_in + k: 5 + k for k in range(len(extra))},
        compiler_params=_params("arbitrary"),
    )(dx1, mg, ya, yb, yc, yd, z, mb, wup, wo, *extra)


def _loss_head(y, target, tm):
    T = y.shape[0]

    def body(y_ref, t_ref, loss_ref, dy_ref, acc):
        i = pl.program_id(0)

        @pl.when(i == 0)
        def _():
            acc[...] = jnp.zeros_like(acc)

        e = y_ref[...] - t_ref[...]
        dy_ref[...] = e * (1.0 / D)
        acc[...] += jnp.sum(e * e, axis=0, keepdims=True)

        @pl.when(i == T // tm - 1)
        def _():
            loss_ref[...] = jnp.sum(acc[...], axis=1, keepdims=True) * (0.5 / D)

    return _pcall(
        body, name="loss_head", grid=(T // tm,),
        in_specs=[pl.BlockSpec((tm, D), lambda i: (i, 0)), pl.BlockSpec((tm, D), lambda i: (i, 0))],
        out_specs=[pl.BlockSpec((1, 1), lambda i: (0, 0)), pl.BlockSpec((tm, D), lambda i: (i, 0))],
        out_shape=[jax.ShapeDtypeStruct((1, 1), f32), jax.ShapeDtypeStruct((T, D), f32)],
        scratch_shapes=[pltpu.VMEM((1, D), f32)],
        compiler_params=_params("arbitrary"),
    )(y, target)


def _lb_softmax_rows(l_ref):
    rows = [l_ref[i:i + 1, :] for i in range(DEPTH)]
    m = rows[0]
    for r in rows[1:]:
        m = jnp.maximum(m, r)
    es = [jnp.exp(r - m) for r in rows]
    tot = es[0]
    for e in es[1:]:
        tot = tot + e
    return [e / tot for e in es]


def _lb_partial_sums(pr):
    sums = [jnp.zeros_like(pr[0])]
    for i in range(1, DEPTH):
        sums.append(sums[-1] + pr[i])
    return sums


def _stack_rows(rows, width):
    idx = _iota2((8, width), 0)
    out = jnp.zeros((8, width), f32)
    for i, r in enumerate(rows):
        out = jnp.where(idx == i, r, out)
    return out


def _lower_bounds(lb_logits):
    def body(l_ref, o_ref):
        sums = _lb_partial_sums(_lb_softmax_rows(l_ref))
        o_ref[...] = _stack_rows([jnp.clip(s, 0.0, 1.0) for s in sums], W)

    return _pcall(body, name="lower_bounds", out_shape=jax.ShapeDtypeStruct((8, W), f32))(lb_logits)


def _lower_bounds_bwd(lb_logits, dlower):
    def body(l_ref, d_ref, o_ref):
        pr = _lb_softmax_rows(l_ref)
        sums = _lb_partial_sums(pr)
        dl = [jnp.where((sums[i] > 0.0) & (sums[i] < 1.0), d_ref[i:i + 1, :], 0.0) for i in range(DEPTH)]
        dp = [jnp.zeros_like(pr[0])] * DEPTH
        run = jnp.zeros_like(pr[0])
        for j in reversed(range(1, DEPTH)):
            run = run + dl[j]
            dp[j] = run
        inner = pr[0] * dp[0]
        for j in range(1, DEPTH):
            inner = inner + pr[j] * dp[j]
        o_ref[...] = _stack_rows([pr[j] * (dp[j] - inner) for j in range(DEPTH)], W)

    return _pcall(body, name="lower_bounds_bwd", out_shape=jax.ShapeDtypeStruct((8, W), f32))(lb_logits, dlower)


def _row_tile(rows, cols, budget_bytes=1 << 20, mult=8):
    if rows % mult:
        return rows
    best = mult
    for t in range(mult, rows + 1, mult):
        if rows % t == 0 and t * cols * 4 <= budget_bytes:
            best = t
    return best


def _sum_slabs(land):
    _, R, C = land.shape
    tr = _row_tile(R, C * N_DEV, mult=16)

    def body(l_ref, o_ref):
        acc = l_ref[0].astype(f32)
        for j in range(1, N_DEV):
            acc = acc + l_ref[j].astype(f32)
        o_ref[...] = acc

    return _pcall(
        body, name="sum_slabs", grid=(R // tr,),
        in_specs=[pl.BlockSpec((N_DEV, tr, C), lambda i: (0, i, 0))],
        out_specs=pl.BlockSpec((tr, C), lambda i: (i, 0)),
        out_shape=jax.ShapeDtypeStruct((R, C), f32),
        compiler_params=_params("parallel"),
    )(land)


def _adamw(w, g, m, v):
    R, C = w.shape
    tr = _row_tile(R, C)
    c1 = 1.0 / (1.0 - ADAM_B1 ** ADAM_STEP)
    c2 = 1.0 / (1.0 - ADAM_B2 ** ADAM_STEP)

    def body(w_ref, g_ref, m_ref, v_ref, d_ref, nm_ref, nv_ref):
        gv = g_ref[...]
        nm = ADAM_B1 * m_ref[...] + (1.0 - ADAM_B1) * gv
        nv = ADAM_B2 * v_ref[...] + (1.0 - ADAM_B2) * (gv * gv)
        nm_ref[...] = nm
        nv_ref[...] = nv
        d_ref[...] = -ADAM_LR * ((nm * c1) / (jnp.sqrt(nv * c2) + ADAM_EPS) + ADAM_WD * w_ref[...])

    spec = pl.BlockSpec((tr, C), lambda i: (i, 0))
    return _pcall(
        body, name="adamw", grid=(R // tr,),
        in_specs=[spec] * 4, out_specs=[spec] * 3,
        out_shape=[jax.ShapeDtypeStruct((R, C), f32)] * 3,
        compiler_params=_params("parallel"),
    )(w, g, m, v)


def _my_id():
    return lax.axis_index("x") * 4 + lax.axis_index("y") * 2 + lax.axis_index("c")


def _peer(k):
    x, y, c = lax.axis_index("x"), lax.axis_index("y"), lax.axis_index("c")
    kx, ky, kc = (k >> 2) & 1, (k >> 1) & 1, k & 1
    px, py, pc = x ^ kx, y ^ ky, c ^ kc
    return (px, py, pc), px * 4 + py * 2 + pc


def _all_gather(shards, axes):
    n = len(shards)

    def full_shape(s, ax):
        shp = list(s.shape)
        shp[ax] *= N_DEV
        return tuple(shp)

    def body(*refs):
        srcs, outs = refs[:n], refs[n:2 * n]
        send_sems, recv_sems, local_sems = refs[2 * n:]
        x, y, c = lax.axis_index("x"), lax.axis_index("y"), lax.axis_index("c")
        me, sibling = (x, y, c), (x, y, 1 - c)
        chips = [(1 - x, y), (x, 1 - y), (1 - x, 1 - y)]

        def block(a, dev):
            j = dev[0] * 4 + dev[1] * 2 + dev[2]
            size = shards[a].shape[axes[a]]
            start = pl.multiple_of(j * size, size)
            if axes[a] == 1:
                return outs[a].at[:, pl.ds(start, size), :]
            return outs[a].at[:, pl.ds(start, size)]

        def copy(a, k, dev, to, src=None):
            return pltpu.make_async_remote_copy(
                src_ref=block(a, dev) if src is None else src, dst_ref=block(a, dev),
                send_sem=send_sems.at[a, k], recv_sem=recv_sems.at[a, k],
                device_id=to, device_id_type=pl.DeviceIdType.MESH)

        mine = [pltpu.make_async_copy(srcs[a], block(a, me), local_sems.at[a]) for a in range(n)]
        for cp in mine:
            cp.start()
        first = []
        for a in range(n):
            first.append(copy(a, 0, me, sibling, src=srcs[a]))
            first += [copy(a, 1 + j, me, (*chip, c), src=srcs[a]) for j, chip in enumerate(chips)]
        for cp in first:
            cp.start()
        passed = []
        for j, chip in enumerate(chips):
            for a in range(n):
                copy(a, 1 + j, (*chip, c), me).wait_recv()
                cp = copy(a, 4 + j, (*chip, c), sibling)
                cp.start()
                passed.append(cp)
        for a in range(n):
            copy(a, 0, sibling, me).wait_recv()
            for j, chip in enumerate(chips):
                copy(a, 4 + j, (*chip, 1 - c), me).wait_recv()
        for cp in first + passed:
            cp.wait_send()
        for cp in mine:
            cp.wait()

    hbm = pl.BlockSpec(memory_space=pltpu.HBM)
    return _pcall(
        body, name="all_gather",
        in_specs=[hbm] * n, out_specs=[hbm] * n,
        out_shape=[jax.ShapeDtypeStruct(full_shape(s, ax), s.dtype) for s, ax in zip(shards, axes)],
        scratch_shapes=[pltpu.SemaphoreType.DMA((n, N_DEV - 1)), pltpu.SemaphoreType.DMA((n, N_DEV - 1)),
                        pltpu.SemaphoreType.DMA((n,))],
    )(*shards)


def _exchange(sliced, whole):
    ns, nw = len(sliced), len(whole)
    n = ns + nw

    def body(*refs):
        srcs, outs = refs[:n], refs[n:2 * n]
        send_sems, recv_sems, local_sems = refs[2 * n:]
        me = _my_id()

        def src_of(a, dest):
            return srcs[a].at[dest] if a < ns else srcs[a]

        locals_ = [pltpu.make_async_copy(src_of(a, me), outs[a].at[me], local_sems.at[a]) for a in range(n)]
        for cp in locals_:
            cp.start()
        sends = []
        for k in range(1, N_DEV):
            peer, pid = _peer(k)
            for a in range(n):
                cp = pltpu.make_async_remote_copy(
                    src_ref=src_of(a, pid), dst_ref=outs[a].at[me],
                    send_sem=send_sems.at[a, k - 1], recv_sem=recv_sems.at[a, k - 1],
                    device_id=peer, device_id_type=pl.DeviceIdType.MESH)
                cp.start()
                sends.append(cp)
        for k in range(1, N_DEV):
            peer, pid = _peer(k)
            for a in range(n):
                pltpu.make_async_remote_copy(
                    src_ref=src_of(a, pid), dst_ref=outs[a].at[pid],
                    send_sem=send_sems.at[a, k - 1], recv_sem=recv_sems.at[a, k - 1],
                    device_id=peer, device_id_type=pl.DeviceIdType.MESH).wait_recv()
        for cp in sends:
            cp.wait_send()
        for cp in locals_:
            cp.wait()

    hbm = pl.BlockSpec(memory_space=pltpu.HBM)
    shapes = [jax.ShapeDtypeStruct(s.shape, s.dtype) for s in sliced]
    shapes += [jax.ShapeDtypeStruct((N_DEV,) + s.shape, s.dtype) for s in whole]
    return _pcall(
        body, name="grad_exchange",
        in_specs=[hbm] * n, out_specs=[hbm] * n, out_shape=shapes,
        scratch_shapes=[pltpu.SemaphoreType.DMA((n, N_DEV - 1)), pltpu.SemaphoreType.DMA((n, N_DEV - 1)),
                        pltpu.SemaphoreType.DMA((n,))],
    )(*sliced, *whole)


def _permute_cols(w):
    pad = jnp.zeros(w.shape[:-1] + (NZ - OFF_F - NH,), w.dtype)
    return jnp.concatenate([
        w[..., 3844:7940],
        w[..., 0:1024],
        w[..., 2052:3076],
        w[..., 3076:3844],
        w[..., 1024:2048],
        w[..., 2048:2052], pad], axis=-1)


def _unpermute_cols(g):
    return jnp.concatenate([
        g[..., OFF_CONV:OFF_CONV + 1024],
        g[..., OFF_ATT:OFF_ATT + 1024],
        g[..., OFF_F:OFF_F + NH],
        g[..., OFF_HGRN:OFF_HGRN + 1024],
        g[..., OFF_SGU:OFF_SGU + 768],
        g[..., 0:4096]], axis=-1)


_SMALL = (
    ("norm_mix", (DEPTH, D)), ("conv_w", (DEPTH, CONV_WIDTH, W)), ("conv_b", (DEPTH, W)),
    ("fgate_bias", (DEPTH, NH)), ("q_norm", (DEPTH, DH)), ("k_norm", (DEPTH, DH)),
    ("lb_logits", (DEPTH, W)), ("hgrn_norm", (DEPTH, W)), ("sgu_norm", (DEPTH, W)),
    ("spatial_w", (DEPTH, NH, SGU_CHUNK, SGU_CHUNK)), ("spatial_b", (DEPTH, NH, SGU_CHUNK)),
    ("merge_b", (DEPTH, NBR, D)), ("norm_ple", (DEPTH, D)),
)


def _small_rows(shape):
    size = 1
    for s in shape:
        size *= s
    rows = -(-size // 128)
    return size, -(-rows // 8) * 8


def _pack_small(parts):
    out = []
    for name, shape in _SMALL:
        size, rows = _small_rows(shape)
        flat = parts[name].astype(f32).reshape(-1)
        flat = jnp.pad(flat, (0, rows * 128 - size))
        out.append(flat.reshape(rows, 128))
    return jnp.concatenate(out, axis=0)


def _unpack_small(buf):
    parts, r0 = {}, 0
    for name, shape in _SMALL:
        size, rows = _small_rows(shape)
        parts[name] = buf[r0:r0 + rows].reshape(-1)[:size].reshape(shape)
        r0 += rows
    return parts


def _shard_cols(a, width):
    return lax.dynamic_slice_in_dim(a, _my_id() * width, width, axis=a.ndim - 1)


def kernel(x, p, norm_mix, w_in, conv_w, conv_b, fgate_bias, q_norm, k_norm, lb_logits, hgrn_norm, sgu_norm, spatial_w, spatial_b, w_up, merge_b, w_o, norm_ple, w_ple_gate, w_ple_proj, loss_target, m_norm_mix, m_w_in, m_conv_w, m_conv_b, m_fgate_bias, m_q_norm, m_k_norm, m_lb_logits, m_hgrn_norm, m_sgu_norm, m_spatial_w, m_spatial_b, m_w_up, m_merge_b, m_w_o, m_norm_ple, m_w_ple_gate, m_w_ple_proj, v_norm_mix, v_w_in, v_conv_w, v_conv_b, v_fgate_bias, v_q_norm, v_k_norm, v_lb_logits, v_hgrn_norm, v_sgu_norm, v_spatial_w, v_spatial_b, v_w_up, v_merge_b, v_w_o, v_norm_ple, v_w_ple_gate, v_w_ple_proj):
    T = x.shape[1]
    SH = D // N_DEV
    CW = W // N_DEV
    tm = 512 if T % 512 == 0 else T
    tmm = 256 if T % 256 == 0 else T
    x0 = x.reshape(T, D)
    target = loss_target.reshape(T, D)

    small_shard = jnp.concatenate([
        merge_b.reshape(DEPTH * NBR, SH),
        jnp.pad(conv_w.reshape(DEPTH * CONV_WIDTH, CW), ((0, 16 - DEPTH * CONV_WIDTH), (0, SH - CW)))], axis=0)
    win_f, wup_f, wo_f, wpg_f, wpp_f, g_small = _all_gather(
        [_permute_cols(w_in).astype(bf16),
         w_up.astype(bf16).reshape(DEPTH * NBR * W, SH),
         w_o.astype(bf16),
         w_ple_gate.astype(bf16),
         w_ple_proj.astype(bf16).reshape(DEPTH * PLE, SH),
         small_shard],
        [1, -1, 1, 1, -1, -1])
    wup_f = wup_f.reshape(DEPTH, NBR, W, D)
    wpp_f = wpp_f.reshape(DEPTH, PLE, D)
    mb_f = g_small[0:DEPTH * NBR].reshape(DEPTH, NBR, D)
    cw_f = g_small[16:16 + DEPTH * CONV_WIDTH].reshape(DEPTH, CONV_WIDTH, N_DEV, SH)[..., 0:CW]
    cw_f = cw_f.reshape(DEPTH, CONV_WIDTH, W)

    loss_local, dx, gw, gs_full = _forward_backward(
        x0, p[:, 0], target, win_f, wup_f, wo_f, wpg_f, wpp_f, mb_f, cw_f, norm_mix, conv_b, fgate_bias, q_norm,
        k_norm, lb_logits, hgrn_norm, sgu_norm, spatial_w, spatial_b, norm_ple)
    loss = lax.psum(loss_local[0, 0], AXES)
    grad_x = dx.reshape(1, T, D)

    weights = dict(norm_mix=norm_mix, w_in=w_in, conv_w=conv_w, conv_b=conv_b, fgate_bias=fgate_bias, q_norm=q_norm,
                   k_norm=k_norm, lb_logits=lb_logits, hgrn_norm=hgrn_norm, sgu_norm=sgu_norm, spatial_w=spatial_w,
                   spatial_b=spatial_b, w_up=w_up, merge_b=merge_b, w_o=w_o, norm_ple=norm_ple,
                   w_ple_gate=w_ple_gate, w_ple_proj=w_ple_proj)
    ms = dict(norm_mix=m_norm_mix, w_in=m_w_in, conv_w=m_conv_w, conv_b=m_conv_b, fgate_bias=m_fgate_bias,
              q_norm=m_q_norm, k_norm=m_k_norm, lb_logits=m_lb_logits, hgrn_norm=m_hgrn_norm, sgu_norm=m_sgu_norm,
              spatial_w=m_spatial_w, spatial_b=m_spatial_b, w_up=m_w_up, merge_b=m_merge_b, w_o=m_w_o,
              norm_ple=m_norm_ple, w_ple_gate=m_w_ple_gate, w_ple_proj=m_w_ple_proj)
    vs = dict(norm_mix=v_norm_mix, w_in=v_w_in, conv_w=v_conv_w, conv_b=v_conv_b, fgate_bias=v_fgate_bias,
              q_norm=v_q_norm, k_norm=v_k_norm, lb_logits=v_lb_logits, hgrn_norm=v_hgrn_norm, sgu_norm=v_sgu_norm,
              spatial_w=v_spatial_w, spatial_b=v_spatial_b, w_up=v_w_up, merge_b=v_merge_b, w_o=v_w_o,
              norm_ple=v_norm_ple, w_ple_gate=v_w_ple_gate, w_ple_proj=v_w_ple_proj)
    return _exchange_and_update(loss, grad_x, gw, gs_full, weights, ms, vs)


def _forward_backward(x0, p, target, win_f, wup_f, wo_f, wpg_f, wpp_f, mb_f, cw_f, norm_mix, conv_b, fgate_bias,
                      q_norm, k_norm, lb_logits, hgrn_norm, sgu_norm, spatial_w, spatial_b, norm_ple):
    T = x0.shape[0]
    tm = 512 if T % 512 == 0 else T
    tmm = 256 if T % 256 == 0 else T
    tmi = 1024 if T % 1024 == 0 else tm
    lower = _lower_bounds(lb_logits)
    fb_pad = jnp.pad(fgate_bias, ((0, 0), (0, 128 - NH)))
    gq_t = jnp.tile(q_norm, (1, NH))
    gk_t = jnp.tile(k_norm, (1, NH))
    sbe = jnp.repeat(jnp.swapaxes(spatial_b, 1, 2), DH, axis=2)

    saved = []
    xc = x0
    p = p[:, None]
    for li in range(DEPTH):
        row = lambda a: a[li:li + 1]
        z, h = _inproj_fwd(xc, row(norm_mix), win_f[li], tmi)
        ya = _conv_fwd(z, cw_f[li], row(conv_b), tm)
        yd = _sgu_fwd(z, row(sgu_norm), spatial_w[li], sbe[li], tm)
        yc, o_pre, states = _hgrn_fwd(z, lower[li:li + 1], row(hgrn_norm), tmm)
        qt, kt, vt, kh, vh, cum = _attn_prep(z, row(fb_pad), row(gq_t), row(gk_t), tm)
        cum4 = jnp.transpose(cum[:, 0:NH])
        ccol, crow = cum4[:, :, None], cum4[:, None, :]
        oh, lse = _attn_fwd(qt, kh, vt, crow, ccol, tm)
        yb, mg, x1, x2 = _merge_fwd(ya, oh, z, yc, yd, mb_f[li], xc, p[li, 0], wup_f[li], wo_f[li],
                                    row(norm_ple), wpg_f[li], wpp_f[li], tmm)
        saved.append(dict(x=xc, z=z, h=h, ya=ya, yb=yb, yc=yc, yd=yd, o_pre=o_pre, states=states,
                          qt=qt, kt=kt, kh=kh, vh=vh, crow=crow, ccol=ccol, oh=oh, lse=lse, mg=mg, x1=x1))
        xc = x2

    loss_local, dx = _loss_head(xc, target, tm)

    gw = dict(w_in=None, w_up=None, w_o=None, w_ple_gate=None, w_ple_proj=None)
    gs = {n: [None] * DEPTH for n, _ in _SMALL}
    dlower = [None] * DEPTH
    for li in reversed(range(DEPTH)):
        s = saved[li]
        row = lambda a: a[li:li + 1]
        first = li == DEPTH - 1
        dx1, gw["w_ple_gate"], gw["w_ple_proj"], ggp = _ple_bwd(
            dx, s["x1"], p[li, 0], row(norm_ple), wpg_f[li], wpp_f[li], tmm, li,
            None if first else (gw["w_ple_gate"], gw["w_ple_proj"]))
        gs["norm_ple"][li] = ggp[0]
        dz, dya, dyb, dyc, dyd, gw["w_o"], gw["w_up"], gs["merge_b"][li] = _merge_bwd(
            dx1, s["mg"], s["ya"], s["yb"], s["yc"], s["yd"], s["z"], mb_f[li], wup_f[li], wo_f[li], tmm, li,
            None if first else (gw["w_o"], gw["w_up"]))
        dz, gcw, gcb = _conv_bwd(s["z"], dya, cw_f[li], row(conv_b), dz, tm)
        gs["conv_w"][li], gs["conv_b"][li] = gcw[0:CONV_WIDTH], gcb[0]
        dz, gs["spatial_w"][li], gsb, ggv = _sgu_bwd(s["z"], dyd, row(sgu_norm), spatial_w[li], sbe[li], dz, tm)
        gs["spatial_b"][li] = jnp.transpose(gsb[:, ::DH])
        gs["sgu_norm"][li] = ggv[0]
        dz, ggn, glb = _hgrn_bwd(s["z"], lower[li:li + 1], row(hgrn_norm), s["o_pre"], s["states"], dyc, dz, tmm)
        gs["hgrn_norm"][li], dlower[li] = ggn[0], glb[0]
        dot, delta = _attn_bwd_prep(dyb, s["oh"], s["z"], tm)
        dqh, dkh, dvh, dck, dcq = _attn_bwd(s["qt"], s["kt"], s["kh"], s["vh"], s["crow"], s["ccol"], dot,
                                            s["lse"], delta, tm)
        dck_t = jnp.pad(jnp.transpose(dck.reshape(NH, T)), ((0, 0), (0, 128 - NH)))
        dcq_t = jnp.transpose(dcq[:, :, 0:HP, :], (0, 2, 1, 3)).reshape(NH, T)
        dcq_t = jnp.pad(jnp.transpose(dcq_t), ((0, 0), (0, 128 - NH)))
        dz, ggq, ggk, gfb = _attn_post(s["z"], dyb, s["oh"], dqh, dkh, dvh, dck_t, dcq_t, row(fb_pad),
                                       row(gq_t), row(gk_t), dz, tm)
        gs["q_norm"][li], gs["k_norm"][li], gs["fgate_bias"][li] = ggq[0, 0:DH], ggk[0, 0:DH], gfb[0, 0:NH]
        dx, gnm = _inproj_bwd_x(dz, win_f[li], s["x"], dx1, row(norm_mix), tmi)
        gs["norm_mix"][li] = gnm[0]
        gw["w_in"] = _inproj_bwd_w(s["h"], dz, tmi, li, gw["w_in"])
    dlower8 = jnp.pad(jnp.stack(dlower), ((0, 8 - DEPTH), (0, 0)))
    gs_full = {n: jnp.stack(v) for n, v in gs.items() if n != "lb_logits"}
    gs_full["lb_logits"] = _lower_bounds_bwd(lb_logits, dlower8)[0:DEPTH]
    return loss_local, dx, gw, gs_full


def _exchange_and_update(loss, grad_x, gw, gs_full, weights, ms, vs):
    SH = D // N_DEV
    CW = W // N_DEV

    small_buf = _pack_small(gs_full)
    l_win, l_wup, l_wo, l_wpg, l_wpp, l_small = _exchange(
        [gw["w_in"], gw["w_up"], gw["w_o"], gw["w_ple_gate"], gw["w_ple_proj"]], [small_buf])

    g_w_in = _unpermute_cols(_sum_slabs(l_win)).reshape(DEPTH, SH, IN_COLS)
    g_w_up = _sum_slabs(l_wup).reshape(DEPTH, NBR, W, SH)
    g_w_o = _sum_slabs(l_wo).reshape(DEPTH, SH, D)
    g_w_pg = _sum_slabs(l_wpg).reshape(DEPTH, SH, D)
    g_w_pp = _sum_slabs(l_wpp).reshape(DEPTH, PLE, SH)
    g_small = _unpack_small(_sum_slabs(l_small))
    g_small_local = dict(g_small)
    g_small_local["conv_w"] = _shard_cols(g_small["conv_w"], CW)
    g_small_local["merge_b"] = _shard_cols(g_small["merge_b"], SH)

    grads = dict(w_in=g_w_in, w_up=g_w_up, w_o=g_w_o, w_ple_gate=g_w_pg, w_ple_proj=g_w_pp)
    deltas, new_m, new_v = {}, {}, {}
    for name, cols in (("w_in", IN_COLS), ("w_up", SH), ("w_o", D), ("w_ple_gate", D), ("w_ple_proj", SH)):
        shape = weights[name].shape
        d_, m_, v_ = _adamw(weights[name].reshape(-1, cols), grads[name].reshape(-1, cols),
                            ms[name].reshape(-1, cols), vs[name].reshape(-1, cols))
        deltas[name], new_m[name], new_v[name] = d_.reshape(shape), m_.reshape(shape), v_.reshape(shape)

    def local_shapes(parts):
        return {n: (parts[n] if parts[n].shape == s else jnp.pad(
            parts[n], [(0, 0)] * (len(s) - 1) + [(0, s[-1] - parts[n].shape[-1])])) for n, s in _SMALL}

    d_, m_, v_ = _adamw(_pack_small(local_shapes(weights)), _pack_small(local_shapes(g_small_local)),
                        _pack_small(local_shapes(ms)), _pack_small(local_shapes(vs)))
    for buf, dst in ((d_, deltas), (m_, new_m), (v_, new_v)):
        parts = _unpack_small(buf)
        for n, _ in _SMALL:
            dst[n] = parts[n][..., :weights[n].shape[-1]]
    for n, _ in _SMALL:
        grads[n] = g_small_local[n]

    order = ["norm_mix", "w_in", "conv_w", "conv_b", "fgate_bias", "q_norm", "k_norm", "lb_logits", "hgrn_norm",
             "sgu_norm", "spatial_w", "spatial_b", "w_up", "merge_b", "w_o", "norm_ple", "w_ple_gate", "w_ple_proj"]
    return (loss, grad_x, *[grads[n] for n in order], *[deltas[n] for n in order],
            *[new_m[n] for n in order], *[new_v[n] for n in order])
```

```python
import functools

import jax
import jax.numpy as jnp
from jax import lax
from jax.experimental import pallas as pl
from jax.experimental.pallas import tpu as pltpu

f32 = jnp.float32
bf16 = jnp.bfloat16

D = 1024
W = 256
NH = 4
DH = 64
NBR = 4
PLE = 256
DEPTH = 4
CONV_WIDTH = 3
SGU_CHUNK = 128
GLA_CHUNK = 64
EPS = 1e-6
MASK_VALUE = -1e30
IN_COLS = 7940
NZ = 8064
OFF_CONV = 4096
OFF_HGRN = 5120
OFF_SGU = 6144
OFF_ATT = 6912
OFF_F = 7936
ZT = 1152
NZT = NZ // ZT
EXP_CLAMP = 80.0

ADAM_LR = 0.001
ADAM_B1 = 0.9
ADAM_B2 = 0.999
ADAM_EPS = 1e-08
ADAM_WD = 0.01
ADAM_STEP = 10

N_DEV = 8
AXES = ("x", "y", "c")
VMEM_LIMIT = 56 * 1024 * 1024
HI = lax.Precision.HIGHEST

NT_DIMS = (((1,), (1,)), ((), ()))
TN_DIMS = (((0,), (0,)), ((), ()))


def _pcall(body, **kw):
    return pl.pallas_call(body, **kw)


def _params(*sem):
    return pltpu.CompilerParams(dimension_semantics=sem, vmem_limit_bytes=VMEM_LIMIT)


def _mm(a, b):
    return jnp.dot(a, b, preferred_element_type=f32)


def _mm_nt(a, b):
    return lax.dot_general(a, b, NT_DIMS, preferred_element_type=f32)


def _mm_tn(a, b):
    return lax.dot_general(a, b, TN_DIMS, preferred_element_type=f32)


def _sigmoid(x):
    return 1.0 / (1.0 + jnp.exp(-x))


def _silu(x):
    return x * _sigmoid(x)


def _dsilu(x):
    s = _sigmoid(x)
    return s * (1.0 + x * (1.0 - s))


def _logsigmoid(x):
    return jnp.minimum(x, 0.0) - jnp.log(1.0 + jnp.exp(-jnp.abs(x)))


def _iota2(shape, axis):
    return lax.broadcasted_iota(jnp.int32, shape, axis)


def _group_mean_matrix(n, group):
    shift = group.bit_length() - 1
    r = lax.shift_right_logical(_iota2((n, n), 0), shift)
    c = lax.shift_right_logical(_iota2((n, n), 1), shift)
    return jnp.where(r == c, 1.0 / group, 0.0).astype(f32)


def _group_mean(x, gm):
    return jnp.dot(x, gm, precision=HI, preferred_element_type=f32)


def _lower_tri(n):
    return jnp.where(_iota2((n, n), 0) >= _iota2((n, n), 1), 1.0, 0.0).astype(f32)


def _upper_tri(n):
    return jnp.where(_iota2((n, n), 0) <= _iota2((n, n), 1), 1.0, 0.0).astype(f32)


def _rows3(r0, r1, r2, width):
    row = _iota2((8, width), 0)
    return jnp.where(row == 0, r0, jnp.where(row == 1, r1, jnp.where(row == 2, r2, 0.0)))


def _inproj_fwd(x, g, w, tm):
    T = x.shape[0]

    def body(x_ref, g_ref, w_ref, z_ref, h_ref):
        @pl.when(pl.program_id(1) == 0)
        def _():
            xv = x_ref[...]
            r = lax.rsqrt(jnp.mean(xv * xv, axis=-1, keepdims=True) + EPS)
            h_ref[...] = (xv * r * g_ref[...]).astype(bf16)

        z_ref[...] = _mm(h_ref[...], w_ref[...])

    return _pcall(
        body, name="inproj_fwd", grid=(T // tm, NZT),
        in_specs=[pl.BlockSpec((tm, D), lambda i, j: (i, 0)),
                  pl.BlockSpec((1, D), lambda i, j: (0, 0)),
                  pl.BlockSpec((D, ZT), lambda i, j: (0, j))],
        out_specs=[pl.BlockSpec((tm, ZT), lambda i, j: (i, j)),
                   pl.BlockSpec((tm, D), lambda i, j: (i, 0))],
        out_shape=[jax.ShapeDtypeStruct((T, NZ), f32), jax.ShapeDtypeStruct((T, D), bf16)],
        compiler_params=_params("parallel", "arbitrary"),
    )(x, g, w)


def _inproj_bwd_x(dz, w, x, dx1, g, tm):
    T = x.shape[0]

    def body(dz_ref, w_ref, x_ref, dx1_ref, g_ref, dx_ref, gg_ref, acc):
        i, k = pl.program_id(0), pl.program_id(1)

        @pl.when(k == 0)
        def _():
            acc[...] = jnp.zeros_like(acc)

        @pl.when((i == 0) & (k == 0))
        def _():
            gg_ref[...] = jnp.zeros_like(gg_ref)

        acc[...] += _mm_nt(dz_ref[...], w_ref[...])

        @pl.when(k == NZT - 1)
        def _():
            xv = x_ref[...]
            r = lax.rsqrt(jnp.mean(xv * xv, axis=-1, keepdims=True) + EPS)
            dh = acc[...]
            gg_ref[...] += jnp.sum(dh * xv * r, axis=0, keepdims=True)
            u = dh * g_ref[...]
            dx_ref[...] = dx1_ref[...] + r * u - xv * (r * r * r) * jnp.mean(u * xv, axis=-1, keepdims=True)

    return _pcall(
        body, name="inproj_bwd_x", grid=(T // tm, NZT),
        in_specs=[pl.BlockSpec((tm, ZT), lambda i, k: (i, k)),
                  pl.BlockSpec((D, ZT), lambda i, k: (0, k)),
                  pl.BlockSpec((tm, D), lambda i, k: (i, 0)),
                  pl.BlockSpec((tm, D), lambda i, k: (i, 0)),
                  pl.BlockSpec((1, D), lambda i, k: (0, 0))],
        out_specs=[pl.BlockSpec((tm, D), lambda i, k: (i, 0)),
                   pl.BlockSpec((1, D), lambda i, k: (0, 0))],
        out_shape=[jax.ShapeDtypeStruct((T, D), f32), jax.ShapeDtypeStruct((1, D), f32)],
        scratch_shapes=[pltpu.VMEM((tm, D), f32)],
        compiler_params=_params("arbitrary", "arbitrary"),
    )(dz, w, x, dx1, g)


def _inproj_bwd_w(h, dz, tm, li, buf):
    T = h.shape[0]
    SH = D // N_DEV
    nt = T // tm
    extra = [] if buf is None else [buf]

    def body(h_ref, dz_ref, *rest):
        gw_ref, acc = rest[len(extra):]

        @pl.when(pl.program_id(1) == 0)
        def _():
            acc[...] = jnp.zeros_like(acc)

        acc[...] += _mm_tn(h_ref[...], dz_ref[...])

        @pl.when(pl.program_id(1) == nt - 1)
        def _():
            gw_ref[...] = acc[...].reshape(N_DEV, SH, ZT).astype(bf16)

    return _pcall(
        body, name="inproj_bwd_w", grid=(NZT, nt),
        in_specs=[pl.BlockSpec((tm, D), lambda j, i: (i, 0)),
                  pl.BlockSpec((tm, ZT), lambda j, i: (i, j))] + [pl.BlockSpec(memory_space=pl.ANY)] * len(extra),
        out_specs=pl.BlockSpec((N_DEV, SH, ZT), lambda j, i: (0, li, j)),
        out_shape=jax.ShapeDtypeStruct((N_DEV, DEPTH * SH, NZ), bf16),
        scratch_shapes=[pltpu.VMEM((D, ZT), f32)],
        input_output_aliases={2: 0} if extra else {},
        compiler_params=_params("parallel", "arbitrary"),
    )(h, dz, *extra)


def _zblock(tm, col256):
    return pl.BlockSpec((tm, W), lambda i, c=col256: (i, c))


def _conv_taps(zc, halo, cw_ref, n):
    ext = jnp.concatenate([halo, zc], axis=0)
    z1 = pltpu.roll(ext, 1, 0)[8:]
    z2 = pltpu.roll(ext, 2, 0)[8:]
    return z1, z2


def _conv_fwd(z, cw, cb, tm):
    T = z.shape[0]
    c0 = OFF_CONV // W
    hb = tm // 8

    def body(ax_ref, ab_ref, ac_ref, ag_ref, hx_ref, hc_ref, cw_ref, cb_ref, y_ref):
        i = pl.program_id(0)
        zc = ac_ref[...] * ax_ref[...]
        halo = jnp.where(i > 0, hc_ref[...] * hx_ref[...], 0.0)
        z1, z2 = _conv_taps(zc, halo, cw_ref, tm)
        y = cw_ref[2:3, :] * zc + cw_ref[1:2, :] * z1 + cw_ref[0:1, :] * z2
        ya = ab_ref[...] * (y + cb_ref[...])
        y_ref[...] = (ya * _silu(ag_ref[...])).astype(bf16)

    halo_spec = lambda col: pl.BlockSpec((8, W), lambda i, c=col: (jnp.maximum(i * hb - 1, 0), c))
    return _pcall(
        body, name="conv_fwd", grid=(T // tm,),
        in_specs=[_zblock(tm, c0), _zblock(tm, c0 + 1), _zblock(tm, c0 + 2), _zblock(tm, c0 + 3),
                  halo_spec(c0), halo_spec(c0 + 2),
                  pl.BlockSpec((CONV_WIDTH, W), lambda i: (0, 0)),
                  pl.BlockSpec((1, W), lambda i: (0, 0))],
        out_specs=pl.BlockSpec((tm, W), lambda i: (i, 0)),
        out_shape=jax.ShapeDtypeStruct((T, W), bf16),
        compiler_params=_params("parallel"),
    )(z, z, z, z, z, z, cw, cb)


def _conv_bwd(z, dy, cw, cb, dzbuf, tm):
    T = z.shape[0]
    c0 = OFF_CONV // W
    hb = tm // 8
    nt = T // tm

    def body(ax_ref, ab_ref, ac_ref, ag_ref, hx_ref, hc_ref, nb_ref, ng_ref, dy_ref, ndy_ref,
             cw_ref, cb_ref, dzin_ref, dz_ref, gcw_ref, gcb_ref):
        i = pl.program_id(0)

        @pl.when(i == 0)
        def _():
            gcw_ref[...] = jnp.zeros_like(gcw_ref)
            gcb_ref[...] = jnp.zeros_like(gcb_ref)

        ax, ab, ac, ag = ax_ref[...], ab_ref[...], ac_ref[...], ag_ref[...]
        w0, w1, w2 = cw_ref[0:1, :], cw_ref[1:2, :], cw_ref[2:3, :]
        zc = ac * ax
        halo = jnp.where(i > 0, hc_ref[...] * hx_ref[...], 0.0)
        z1, z2 = _conv_taps(zc, halo, cw_ref, tm)
        yb = w2 * zc + w1 * z1 + w0 * z2 + cb_ref[...]
        ya = ab * yb
        dyg = dy_ref[...]
        dag = dyg * ya * _dsilu(ag)
        dya = dyg * _silu(ag)
        dab = dya * yb
        dyc = dya * ab
        nxt = jnp.where(i < nt - 1, ndy_ref[...] * _silu(ng_ref[...]) * nb_ref[...], 0.0)
        ext = jnp.concatenate([dyc, nxt], axis=0)
        d1 = pltpu.roll(ext, tm + 8 - 1, 0)[:tm]
        d2 = pltpu.roll(ext, tm + 8 - 2, 0)[:tm]
        dzc = w2 * dyc + w1 * d1 + w0 * d2
        dz_ref[:, 0:W] = (dzc * ac).astype(bf16)
        dz_ref[:, W:2 * W] = dab.astype(bf16)
        dz_ref[:, 2 * W:3 * W] = (dzc * ax).astype(bf16)
        dz_ref[:, 3 * W:4 * W] = dag.astype(bf16)
        gcb_ref[...] += jnp.sum(dyc, axis=0, keepdims=True)
        gcw_ref[...] += _rows3(jnp.sum(dyc * z2, axis=0, keepdims=True),
                               jnp.sum(dyc * z1, axis=0, keepdims=True),
                               jnp.sum(dyc * zc, axis=0, keepdims=True), W)

    prev_spec = lambda col: pl.BlockSpec((8, W), lambda i, c=col: (jnp.maximum(i * hb - 1, 0), c))
    next_z = lambda col: pl.BlockSpec((8, W), lambda i, c=col: (jnp.minimum((i + 1) * hb, T // 8 - 1), c))
    next_dy = pl.BlockSpec((8, W), lambda i: (jnp.minimum((i + 1) * hb, T // 8 - 1), 0))
    return _pcall(
        body, name="conv_bwd", grid=(nt,),
        in_specs=[_zblock(tm, c0), _zblock(tm, c0 + 1), _zblock(tm, c0 + 2), _zblock(tm, c0 + 3),
                  prev_spec(c0), prev_spec(c0 + 2), next_z(c0 + 1), next_z(c0 + 3),
                  pl.BlockSpec((tm, W), lambda i: (i, 0)), next_dy,
                  pl.BlockSpec((CONV_WIDTH, W), lambda i: (0, 0)),
                  pl.BlockSpec((1, W), lambda i: (0, 0)),
                  pl.BlockSpec(memory_space=pl.ANY)],
        out_specs=[pl.BlockSpec((tm, 4 * W), lambda i: (i, OFF_CONV // (4 * W))),
                   pl.BlockSpec((8, W), lambda i: (0, 0)),
                   pl.BlockSpec((1, W), lambda i: (0, 0))],
        out_shape=[jax.ShapeDtypeStruct((T, NZ), bf16), jax.ShapeDtypeStruct((8, W), f32),
                   jax.ShapeDtypeStruct((1, W), f32)],
        input_output_aliases={12: 0},
        compiler_params=_params("arbitrary"),
    )(z, z, z, z, z, z, z, z, dy, dy, cw, cb, dzbuf)


def _sgu_core(dv_ref, gv_ref, sw_ref, sbe_ref, s_scr, tm):
    v = dv_ref[...]
    gm = _group_mean_matrix(W, DH)
    rv = lax.rsqrt(_group_mean(v * v, gm) + EPS)
    vh = v * rv
    vnb = (vh * gv_ref[...]).astype(bf16)
    causal = _iota2((SGU_CHUNK, SGU_CHUNK), 0) >= _iota2((SGU_CHUNK, SGU_CHUNK), 1)
    wgs = [jnp.where(causal, sw_ref[g], 0.0).astype(bf16) for g in range(NH)]
    for c in range(tm // SGU_CHUNK):
        rows = slice(c * SGU_CHUNK, (c + 1) * SGU_CHUNK)
        for g in range(NH):
            cols = slice(g * DH, (g + 1) * DH)
            s_scr[rows, cols] = _mm(wgs[g], vnb[rows, cols])
    sb = sbe_ref[...]
    s = s_scr[...] + jnp.concatenate([sb] * (tm // SGU_CHUNK), axis=0)
    return v, rv, vh, vnb, wgs, causal, gm, s


def _sgu_fwd(z, gv, sw, sbe, tm):
    T = z.shape[0]
    c0 = OFF_SGU // W

    def body(du_ref, dv_ref, dg_ref, gv_ref, sw_ref, sbe_ref, y_ref, s_scr):
        s = _sgu_core(dv_ref, gv_ref, sw_ref, sbe_ref, s_scr, tm)[-1]
        y_ref[...] = ((du_ref[...] * s) * _silu(dg_ref[...])).astype(bf16)

    return _pcall(
        body, name="sgu_fwd", grid=(T // tm,),
        in_specs=[_zblock(tm, c0), _zblock(tm, c0 + 1), _zblock(tm, c0 + 2),
                  pl.BlockSpec((1, W), lambda i: (0, 0)),
                  pl.BlockSpec((NH, SGU_CHUNK, SGU_CHUNK), lambda i: (0, 0, 0)),
                  pl.BlockSpec((SGU_CHUNK, W), lambda i: (0, 0))],
        out_specs=pl.BlockSpec((tm, W), lambda i: (i, 0)),
        out_shape=jax.ShapeDtypeStruct((T, W), bf16),
        scratch_shapes=[pltpu.VMEM((tm, W), f32)],
        compiler_params=_params("parallel"),
    )(z, z, z, gv, sw, sbe)


def _sgu_bwd(z, dy, gv, sw, sbe, dzbuf, tm):
    T = z.shape[0]
    c0 = OFF_SGU // W
    nt = T // tm

    def body(du_ref, dv_ref, dg_ref, dy_ref, gv_ref, sw_ref, sbe_ref, dzin_ref,
             dz_ref, gsw_ref, gsb_ref, ggv_ref, s_scr, dvn_scr, sb_acc):
        i = pl.program_id(0)

        @pl.when(i == 0)
        def _():
            gsw_ref[...] = jnp.zeros_like(gsw_ref)
            ggv_ref[...] = jnp.zeros_like(ggv_ref)
            sb_acc[...] = jnp.zeros_like(sb_acc)

        v, rv, vh, vnb, wgs, causal, gm, s = _sgu_core(dv_ref, gv_ref, sw_ref, sbe_ref, s_scr, tm)
        du, dg, dyv = du_ref[...], dg_ref[...], dy_ref[...]
        ddg = dyv * (du * s) * _dsilu(dg)
        t = dyv * _silu(dg)
        ddu = t * s
        ds = t * du
        dsb = ds.astype(bf16)
        acc = sb_acc[...]
        for c in range(tm // SGU_CHUNK):
            rows = slice(c * SGU_CHUNK, (c + 1) * SGU_CHUNK)
            acc = acc + ds[rows, :]
            for g in range(NH):
                cols = slice(g * DH, (g + 1) * DH)
                gsw_ref[g] += jnp.where(causal, _mm_nt(dsb[rows, cols], vnb[rows, cols]), 0.0)
                dvn_scr[rows, cols] = _mm_tn(wgs[g], dsb[rows, cols])
        sb_acc[...] = acc
        dvn = dvn_scr[...]
        ggv_ref[...] += jnp.sum(dvn * vh, axis=0, keepdims=True)
        u = dvn * gv_ref[...]
        ddv = rv * u - v * (rv * rv * rv) * _group_mean(u * v, gm)
        dz_ref[:, 0:W] = ddu.astype(bf16)
        dz_ref[:, W:2 * W] = ddv.astype(bf16)
        dz_ref[:, 2 * W:3 * W] = ddg.astype(bf16)

        @pl.when(i == nt - 1)
        def _():
            gsb_ref[...] = _group_mean(sb_acc[...], gm) * float(DH)

    return _pcall(
        body, name="sgu_bwd", grid=(nt,),
        in_specs=[_zblock(tm, c0), _zblock(tm, c0 + 1), _zblock(tm, c0 + 2),
                  pl.BlockSpec((tm, W), lambda i: (i, 0)),
                  pl.BlockSpec((1, W), lambda i: (0, 0)),
                  pl.BlockSpec((NH, SGU_CHUNK, SGU_CHUNK), lambda i: (0, 0, 0)),
                  pl.BlockSpec((SGU_CHUNK, W), lambda i: (0, 0)),
                  pl.BlockSpec(memory_space=pl.ANY)],
        out_specs=[pl.BlockSpec((tm, 3 * W), lambda i: (i, OFF_SGU // (3 * W))),
                   pl.BlockSpec((NH, SGU_CHUNK, SGU_CHUNK), lambda i: (0, 0, 0)),
                   pl.BlockSpec((SGU_CHUNK, W), lambda i: (0, 0)),
                   pl.BlockSpec((1, W), lambda i: (0, 0))],
        out_shape=[jax.ShapeDtypeStruct((T, NZ), bf16),
                   jax.ShapeDtypeStruct((NH, SGU_CHUNK, SGU_CHUNK), f32),
                   jax.ShapeDtypeStruct((SGU_CHUNK, W), f32),
                   jax.ShapeDtypeStruct((1, W), f32)],
        scratch_shapes=[pltpu.VMEM((tm, W), f32), pltpu.VMEM((tm, W), f32), pltpu.VMEM((SGU_CHUNK, W), f32)],
        input_output_aliases={7: 0},
        compiler_params=_params("arbitrary"),
    )(z, z, z, dy, gv, sw, sbe, dzbuf)


def _hgrn_gates(cq_ref, cf_ref, lb_ref):
    q = _silu(cq_ref[...])
    sig = _sigmoid(cf_ref[...])
    lb = lb_ref[...]
    g = lb + (1.0 - lb) * sig
    return q, sig, g, jnp.log(g), (1.0 - lb) * (1.0 - sig)


def _hgrn_chunk_terms(lgc, qc, kc):
    C = GLA_CHUNK
    b = jnp.dot(_lower_tri(C), lgc, precision=HI, preferred_element_type=f32)
    bl = jnp.sum(lgc, axis=0, keepdims=True)
    mid = jnp.sum(jnp.where(_iota2((C, W), 0) <= C // 2, lgc, 0.0), axis=0, keepdims=True)
    eb = jnp.exp(b)
    em = jnp.exp(jnp.minimum(b - mid, EXP_CLAMP))
    emi = jnp.exp(jnp.minimum(mid - b, EXP_CLAMP))
    ek = jnp.exp(bl - b)
    return dict(eb=eb, em=em, emi=emi, ek=ek, ebl=jnp.exp(bl),
                qe=qc * eb, qm=qc * em, km=kc * emi, kd=kc * ek)


def _hgrn_fwd(z, lb, gain, tm):
    T = z.shape[0]
    c0 = OFF_HGRN // W
    C = GLA_CHUNK
    ncp = tm // C

    def body(cq_ref, cf_ref, ci_ref, cg_ref, lb_ref, gn_ref, y_ref, o_ref, st_ref, state, o_scr):
        @pl.when(pl.program_id(0) == 0)
        def _():
            state[...] = jnp.zeros_like(state)

        q, sig, g, lg, kf = _hgrn_gates(cq_ref, cf_ref, lb_ref)
        v = ci_ref[...]
        causal = _iota2((C, C), 0) >= _iota2((C, C), 1)
        for c in range(ncp):
            rows = slice(c * C, (c + 1) * C)
            tr = _hgrn_chunk_terms(lg[rows], q[rows], kf[rows])
            vb = v[rows].astype(bf16)
            qmb, kmb, qeb, kdb = (tr[n].astype(bf16) for n in ("qm", "km", "qe", "kd"))
            for h in range(NH):
                cols = slice(h * DH, (h + 1) * DH)
                hr = slice(h * DH, (h + 1) * DH)
                st = state[hr, :]
                st_ref[c, hr, :] = st
                p = jnp.where(causal, _mm_nt(qmb[:, cols], kmb[:, cols]), 0.0)
                o_scr[rows, cols] = _mm(p.astype(bf16), vb[:, cols]) + _mm_nt(qeb[:, cols], st.astype(bf16))
                state[hr, :] = st * tr["ebl"][:, cols] + _mm_tn(vb[:, cols], kdb[:, cols])
        o = o_scr[...]
        o_ref[...] = o
        gm = _group_mean_matrix(W, DH)
        r = lax.rsqrt(_group_mean(o * o, gm) + EPS)
        y_ref[...] = ((o * r * gn_ref[...]) * _silu(cg_ref[...])).astype(bf16)

    return _pcall(
        body, name="hgrn_fwd", grid=(T // tm,),
        in_specs=[_zblock(tm, c0), _zblock(tm, c0 + 1), _zblock(tm, c0 + 2), _zblock(tm, c0 + 3),
                  pl.BlockSpec((1, W), lambda i: (0, 0)), pl.BlockSpec((1, W), lambda i: (0, 0))],
        out_specs=[pl.BlockSpec((tm, W), lambda i: (i, 0)),
                   pl.BlockSpec((tm, W), lambda i: (i, 0)),
                   pl.BlockSpec((ncp, W, DH), lambda i: (i, 0, 0))],
        out_shape=[jax.ShapeDtypeStruct((T, W), bf16), jax.ShapeDtypeStruct((T, W), f32),
                   jax.ShapeDtypeStruct((T // C, W, DH), f32)],
        scratch_shapes=[pltpu.VMEM((W, DH), f32), pltpu.VMEM((tm, W), f32)],
        compiler_params=_params("arbitrary"),
    )(z, z, z, z, lb, gain)


def _hgrn_bwd(z, lb, gain, o_pre, states, dy, dzbuf, tm):
    T = z.shape[0]
    c0 = OFF_HGRN // W
    C = GLA_CHUNK
    ncp = tm // C
    nt = T // tm

    def body(cq_ref, cf_ref, ci_ref, cg_ref, lb_ref, gn_ref, o_ref, st_ref, dy_ref, dzin_ref,
             dz_ref, ggn_ref, glb_ref, dstate, dq_s, dk_s, dv_s, db_s):
        @pl.when(pl.program_id(0) == 0)
        def _():
            dstate[...] = jnp.zeros_like(dstate)
            ggn_ref[...] = jnp.zeros_like(ggn_ref)
            glb_ref[...] = jnp.zeros_like(glb_ref)

        cq, cg = cq_ref[...], cg_ref[...]
        q, sig, g, lg, kf = _hgrn_gates(cq_ref, cf_ref, lb_ref)
        lb = lb_ref[...]
        v = ci_ref[...]
        o = o_ref[...]
        gm = _group_mean_matrix(W, DH)
        r = lax.rsqrt(_group_mean(o * o, gm) + EPS)
        oh = o * r
        gn = gn_ref[...]
        dyv = dy_ref[...]
        dcg = dyv * (oh * gn) * _dsilu(cg)
        don = dyv * _silu(cg)
        ggn_ref[...] += jnp.sum(don * oh, axis=0, keepdims=True)
        u = don * gn
        do = r * u - o * (r * r * r) * _group_mean(u * o, gm)

        causal = _iota2((C, C), 0) >= _iota2((C, C), 1)
        last_row = _iota2((C, DH), 0) == C - 1
        for c in reversed(range(ncp)):
            rows = slice(c * C, (c + 1) * C)
            tr = _hgrn_chunk_terms(lg[rows], q[rows], kf[rows])
            vb = v[rows].astype(bf16)
            dob = do[rows].astype(bf16)
            qmb, kmb, qeb, kdb = (tr[n].astype(bf16) for n in ("qm", "km", "qe", "kd"))
            for h in range(NH):
                cols = slice(h * DH, (h + 1) * DH)
                hr = slice(h * DH, (h + 1) * DH)
                st0 = st_ref[c, hr, :]
                dst = dstate[hr, :]
                dstb = dst.astype(bf16)
                doh = dob[:, cols]
                p = jnp.where(causal, _mm_nt(qmb[:, cols], kmb[:, cols]), 0.0)
                dp = jnp.where(causal, _mm_nt(doh, vb[:, cols]), 0.0)
                dpb = dp.astype(bf16)
                dvh = _mm_tn(p.astype(bf16), doh) + _mm_nt(kdb[:, cols], dstb)
                dqm = _mm(dpb, kmb[:, cols])
                dkm = _mm_tn(dpb, qmb[:, cols])
                dqe = _mm(doh, st0.astype(bf16))
                dkd = _mm(vb[:, cols], dstb)
                ebl = tr["ebl"][:, cols]
                dstate[hr, :] = dst * ebl + _mm_tn(doh, qeb[:, cols])
                qm, km, qe, kd = (a[:, cols].astype(f32) for a in (qmb, kmb, qeb, kdb))
                kterm = dkd * kd
                dbh = dqm * qm - dkm * km + dqe * qe - kterm
                extra = jnp.sum(kterm, axis=0, keepdims=True) + ebl * jnp.sum(dst * st0, axis=0, keepdims=True)
                dbh = dbh + jnp.where(last_row, extra, 0.0)
                dq_s[rows, cols] = dqm * tr["em"][:, cols] + dqe * tr["eb"][:, cols]
                dk_s[rows, cols] = dkm * tr["emi"][:, cols] + dkd * tr["ek"][:, cols]
                dv_s[rows, cols] = dvh
                db_s[rows, cols] = dbh
            db_s[rows, :] = jnp.dot(_upper_tri(C), db_s[rows, :], precision=HI, preferred_element_type=f32)
        dlg = db_s[...]
        dk = dk_s[...]
        dsig = sig * (1.0 - sig)
        one_lb = 1.0 - lb
        dcf = (dlg / g - dk) * one_lb * dsig
        glb_ref[...] += jnp.sum((dlg / g - dk) * (1.0 - sig), axis=0, keepdims=True)
        dz_ref[:, 0:W] = (dq_s[...] * _dsilu(cq)).astype(bf16)
        dz_ref[:, W:2 * W] = dcf.astype(bf16)
        dz_ref[:, 2 * W:3 * W] = dv_s[...].astype(bf16)
        dz_ref[:, 3 * W:4 * W] = dcg.astype(bf16)

    rev = lambda i: nt - 1 - i
    zb = lambda col: pl.BlockSpec((tm, W), lambda i, c=col: (rev(i), c))
    return _pcall(
        body, name="hgrn_bwd", grid=(nt,),
        in_specs=[zb(c0), zb(c0 + 1), zb(c0 + 2), zb(c0 + 3),
                  pl.BlockSpec((1, W), lambda i: (0, 0)), pl.BlockSpec((1, W), lambda i: (0, 0)),
                  pl.BlockSpec((tm, W), lambda i: (rev(i), 0)),
                  pl.BlockSpec((ncp, W, DH), lambda i: (rev(i), 0, 0)),
                  pl.BlockSpec((tm, W), lambda i: (rev(i), 0)),
                  pl.BlockSpec(memory_space=pl.ANY)],
        out_specs=[pl.BlockSpec((tm, 4 * W), lambda i: (rev(i), OFF_HGRN // (4 * W))),
                   pl.BlockSpec((1, W), lambda i: (0, 0)),
                   pl.BlockSpec((1, W), lambda i: (0, 0))],
        out_shape=[jax.ShapeDtypeStruct((T, NZ), bf16), jax.ShapeDtypeStruct((1, W), f32),
                   jax.ShapeDtypeStruct((1, W), f32)],
        scratch_shapes=[pltpu.VMEM((W, DH), f32)] + [pltpu.VMEM((tm, W), f32)] * 4,
        input_output_aliases={9: 0},
        compiler_params=_params("arbitrary"),
    )(z, z, z, z, lb, gain, o_pre, states, dy, dzbuf)


def _attn_prep(z, fbias, gq, gk, tm):
    T = z.shape[0]
    c0 = OFF_ATT // W

    def body(q_ref, k_ref, v_ref, f_ref, fb_ref, gq_ref, gk_ref, qt_ref, kt_ref, vt_ref, kh_ref, vh_ref, cum_ref,
             carry):
        @pl.when(pl.program_id(0) == 0)
        def _():
            carry[...] = jnp.zeros_like(carry)

        gm = _group_mean_matrix(W, DH)
        q, k, v = q_ref[...], k_ref[...], v_ref[...]
        qs = q * lax.rsqrt(_group_mean(q * q, gm) + EPS) * (gq_ref[...] * (DH ** -0.5))
        kn = k * lax.rsqrt(_group_mean(k * k, gm) + EPS) * gk_ref[...]
        qt_ref[...] = qs.T.astype(bf16)
        kt_ref[...] = kn.T.astype(bf16)
        vt_ref[...] = v.T.astype(bf16)
        for h in range(NH):
            cols = slice(h * DH, (h + 1) * DH)
            kh_ref[h] = kn[:, cols].astype(bf16)
            vh_ref[h] = v[:, cols].astype(bf16)
        ls = _logsigmoid(f_ref[...] + fb_ref[...])
        cum_ref[...] = jnp.dot(_lower_tri(tm), ls, precision=HI, preferred_element_type=f32) + carry[...]
        carry[...] += jnp.sum(ls, axis=0, keepdims=True)

    hspec = pl.BlockSpec((NH, tm, DH), lambda i: (0, i, 0))
    tspec = pl.BlockSpec((W, tm), lambda i: (0, i))
    return _pcall(
        body, name="attn_prep", grid=(T // tm,),
        in_specs=[_zblock(tm, c0), _zblock(tm, c0 + 1), _zblock(tm, c0 + 2),
                  pl.BlockSpec((tm, 128), lambda i: (i, OFF_F // 128)),
                  pl.BlockSpec((1, 128), lambda i: (0, 0)),
                  pl.BlockSpec((1, W), lambda i: (0, 0)), pl.BlockSpec((1, W), lambda i: (0, 0))],
        out_specs=[tspec, tspec, tspec, hspec, hspec, pl.BlockSpec((tm, 128), lambda i: (i, 0))],
        out_shape=[jax.ShapeDtypeStruct((W, T), bf16)] * 3 + [jax.ShapeDtypeStruct((NH, T, DH), bf16)] * 2
        + [jax.ShapeDtypeStruct((T, 128), f32)],
        scratch_shapes=[pltpu.VMEM((1, 128), f32)],
        compiler_params=_params("arbitrary"),
    )(z, z, z, z, fbias, gq, gk)


HP = 2


def _causal_pairs(nq, key_major):
    if key_major:
        pairs = [(qi, ki) for ki in range(nq) for qi in range(ki, nq)]
    else:
        pairs = [(qi, ki) for qi in range(nq) for ki in range(qi + 1)]
    return (jnp.asarray([p[0] for p in pairs], jnp.int32), jnp.asarray([p[1] for p in pairs], jnp.int32))


def _head_rows(rows, n):
    return jnp.concatenate([jnp.broadcast_to(r, (DH, n)) for r in rows], axis=0)


def _attn_fwd(qt, kh, vt, crow, ccol, bq):
    T = qt.shape[1]
    nq = T // bq
    bk = bq
    qs, ks = _causal_pairs(nq, key_major=False)
    BW = HP * DH

    def body(qs_ref, ks_ref, qt_ref, k_ref, vt_ref, cr_ref, cc_ref, o_ref, lse_ref, m_s, l_s, acc_s):
        i = pl.program_id(1)
        qi, ki = qs_ref[i], ks_ref[i]

        @pl.when(ki == 0)
        def _():
            m_s[...] = jnp.full_like(m_s, MASK_VALUE)
            l_s[...] = jnp.zeros_like(l_s)
            acc_s[...] = jnp.zeros_like(acc_s)

        def step(diagonal):
            for h in range(HP):
                rows = slice(h * DH, (h + 1) * DH)
                s = _mm(k_ref[h], qt_ref[rows, :]) + cr_ref[h] - cc_ref[h]
                if diagonal:
                    s = jnp.where(_iota2((bk, bq), 0) <= _iota2((bk, bq), 1), s, MASK_VALUE)
                m_old = m_s[h]
                m_new = jnp.maximum(m_old, jnp.max(s, axis=0, keepdims=True))
                p = jnp.exp(s - m_new)
                alpha = jnp.exp(m_old - m_new)
                l_s[h] = alpha * l_s[h] + jnp.sum(p, axis=0, keepdims=True)
                acc_s[rows, :] = alpha * acc_s[rows, :] + _mm(vt_ref[rows, :], p.astype(bf16))
                m_s[h] = m_new

        @pl.when(ki < qi)
        def _():
            step(False)

        @pl.when(ki == qi)
        def _():
            step(True)
            o_ref[...] = (acc_s[...] / _head_rows([l_s[h] for h in range(HP)], bq)).T
            for h in range(HP):
                lse_ref[h] = m_s[h] + jnp.log(l_s[h])

    qcol = lambda hp, i, qs, ks: (hp, qs[i])
    kcol = lambda hp, i, qs, ks: (hp, ks[i])
    qrow = lambda hp, i, qs, ks: (hp, 0, qs[i])
    return _pcall(
        body, name="attn_fwd",
        grid_spec=pltpu.PrefetchScalarGridSpec(
            num_scalar_prefetch=2, grid=(NH // HP, qs.shape[0]),
            in_specs=[pl.BlockSpec((BW, bq), qcol),
                      pl.BlockSpec((HP, bk, DH), lambda hp, i, qs, ks: (hp, ks[i], 0)),
                      pl.BlockSpec((BW, bk), kcol),
                      pl.BlockSpec((HP, 1, bq), qrow),
                      pl.BlockSpec((HP, bk, 1), lambda hp, i, qs, ks: (hp, ks[i], 0))],
            out_specs=[pl.BlockSpec((bq, BW), lambda hp, i, qs, ks: (qs[i], hp)),
                       pl.BlockSpec((HP, 1, bq), qrow)],
            scratch_shapes=[pltpu.VMEM((HP, 1, bq), f32), pltpu.VMEM((HP, 1, bq), f32),
                            pltpu.VMEM((BW, bq), f32)]),
        out_shape=[jax.ShapeDtypeStruct((T, W), f32), jax.ShapeDtypeStruct((NH, 1, T), f32)],
        compiler_params=_params("parallel", "arbitrary"),
    )(qs, ks, qt, kh, vt, crow, ccol)


def _attn_bwd_prep(dy, oh, z, tm):
    T = dy.shape[0]
    cg = OFF_ATT // W + 3

    def body(dy_ref, o_ref, g_ref, dot_ref, dl_ref):
        do = (dy_ref[...] * _silu(g_ref[...])).astype(bf16)
        dot_ref[...] = do.astype(f32).T.astype(bf16)
        prod = (do.astype(f32) * o_ref[...]).T
        for h in range(NH):
            dl_ref[h] = jnp.sum(prod[h * DH:(h + 1) * DH, :], axis=0, keepdims=True)

    return _pcall(
        body, name="attn_bwd_prep", grid=(T // tm,),
        in_specs=[pl.BlockSpec((tm, W), lambda i: (i, 0)),
                  pl.BlockSpec((tm, W), lambda i: (i, 0)),
                  _zblock(tm, cg)],
        out_specs=[pl.BlockSpec((W, tm), lambda i: (0, i)),
                   pl.BlockSpec((NH, 1, tm), lambda i: (0, 0, i))],
        out_shape=[jax.ShapeDtypeStruct((W, T), bf16), jax.ShapeDtypeStruct((NH, 1, T), f32)],
        compiler_params=_params("parallel"),
    )(dy, oh, z)


def _attn_bwd(qt, kt, kh, vh, crow, ccol, dot, lse, delta, bq):
    T = qt.shape[1]
    nq = T // bq
    bk = bq
    qs, ks = _causal_pairs(nq, key_major=True)
    BW = HP * DH

    def body(qs_ref, ks_ref, qt_ref, kt_ref, k_ref, v_ref, cr_ref, cc_ref, dot_ref, lse_ref, dl_ref,
             dq_ref, dk_ref, dv_ref, dck_ref, dcq_ref, dq_s, dk_s, dv_s, dck_s):
        i = pl.program_id(1)
        qi, ki = qs_ref[i], ks_ref[i]

        @pl.when(i == 0)
        def _():
            dq_s[...] = jnp.zeros_like(dq_s)
            dcq_ref[...] = jnp.zeros_like(dcq_ref)

        @pl.when(qi == ki)
        def _():
            dk_s[...] = jnp.zeros_like(dk_s)
            dv_s[...] = jnp.zeros_like(dv_s)
            dck_s[...] = jnp.zeros_like(dck_s)

        def step(diagonal):
            colsums = []
            for h in range(HP):
                rows = slice(h * DH, (h + 1) * DH)
                qth, doth = qt_ref[rows, :], dot_ref[rows, :]
                p = jnp.exp(_mm(k_ref[h], qth) + (cr_ref[h] - lse_ref[h]) - cc_ref[h])
                if diagonal:
                    p = jnp.where(_iota2((bk, bq), 0) <= _iota2((bk, bq), 1), p, 0.0)
                dv_s[rows, :] += _mm_nt(doth, p.astype(bf16))
                ds = p * (_mm(v_ref[h], doth) - dl_ref[h])
                dsb = ds.astype(bf16)
                dk_s[rows, :] += _mm_nt(qth, dsb)
                dq_s[qi, rows, :] += _mm(kt_ref[rows, :], dsb)
                part = ds[:, 0:128]
                for c in range(1, bq // 128):
                    part = part + ds[:, c * 128:(c + 1) * 128]
                dck_s[h] += part
                colsums.append(jnp.sum(ds, axis=0, keepdims=True))
            dcq_ref[qi] += _stack_rows(colsums, bq)

        @pl.when(qi > ki)
        def _():
            step(False)

        @pl.when(qi == ki)
        def _():
            step(True)

        @pl.when(qi == nq - 1)
        def _():
            dk_ref[...] = dk_s[...].T
            dv_ref[...] = dv_s[...].T
            lane = _iota2((bk, 128), 1)
            out = jnp.zeros((bk, 128), f32)
            for h in range(HP):
                out = out - jnp.where(lane == pl.program_id(0) * HP + h,
                                      jnp.sum(dck_s[h], axis=1, keepdims=True), 0.0)
            dck_ref[...] = out

        @pl.when(i == qs.shape[0] - 1)
        def _():
            for qb in range(nq):
                dq_ref[qb * bq:(qb + 1) * bq, :] = dq_s[qb].T

    qcol = lambda hp, i, qs, ks: (hp, qs[i])
    kcol = lambda hp, i, qs, ks: (hp, ks[i])
    qrow = lambda hp, i, qs, ks: (hp, 0, qs[i])
    kh_spec = pl.BlockSpec((HP, bk, DH), lambda hp, i, qs, ks: (hp, ks[i], 0))
    return _pcall(
        body, name="attn_bwd",
        grid_spec=pltpu.PrefetchScalarGridSpec(
            num_scalar_prefetch=2, grid=(NH // HP, qs.shape[0]),
            in_specs=[pl.BlockSpec((BW, bq), qcol), pl.BlockSpec((BW, bk), kcol), kh_spec, kh_spec,
                      pl.BlockSpec((HP, 1, bq), qrow),
                      pl.BlockSpec((HP, bk, 1), lambda hp, i, qs, ks: (hp, ks[i], 0)),
                      pl.BlockSpec((BW, bq), qcol), pl.BlockSpec((HP, 1, bq), qrow), pl.BlockSpec((HP, 1, bq), qrow)],
            out_specs=[pl.BlockSpec((T, BW), lambda hp, i, qs, ks: (0, hp)),
                       pl.BlockSpec((bk, BW), lambda hp, i, qs, ks: (ks[i], hp)),
                       pl.BlockSpec((bk, BW), lambda hp, i, qs, ks: (ks[i], hp)),
                       pl.BlockSpec((None, bk, 128), lambda hp, i, qs, ks: (hp, ks[i], 0)),
                       pl.BlockSpec((None, nq, 8, bq), lambda hp, i, qs, ks: (hp, 0, 0, 0))],
            scratch_shapes=[pltpu.VMEM((nq, BW, bq), f32), pltpu.VMEM((BW, bk), f32), pltpu.VMEM((BW, bk), f32),
                            pltpu.VMEM((HP, bk, 128), f32)]),
        out_shape=[jax.ShapeDtypeStruct((T, W), f32)] * 3 + [jax.ShapeDtypeStruct((NH // HP, T, 128), f32),
                                                             jax.ShapeDtypeStruct((NH // HP, nq, 8, bq), f32)],
        compiler_params=_params("parallel", "arbitrary"),
    )(qs, ks, qt, kt, kh, vh, crow, ccol, dot, lse, delta)


def _attn_post(z, dy, oh, dqh, dkh, dvh, dck, dcq, fbias, gq, gk, dzbuf, tm):
    T = z.shape[0]
    c0 = OFF_ATT // W
    nt = T // tm

    def body(q_ref, k_ref, g_ref, f_ref, dy_ref, o_ref, dq_ref, dk_ref, dv_ref, dck_ref, dcq_ref, fb_ref, gq_ref,
             gk_ref, dzin_ref, dz_ref, ggq_ref, ggk_ref, gfb_ref, carry):
        @pl.when(pl.program_id(0) == 0)
        def _():
            carry[...] = jnp.zeros_like(carry)
            ggq_ref[...] = jnp.zeros_like(ggq_ref)
            ggk_ref[...] = jnp.zeros_like(ggk_ref)
            gfb_ref[...] = jnp.zeros_like(gfb_ref)

        gm = _group_mean_matrix(W, DH)
        hs = jnp.where((_iota2((W, W), 0) & (DH - 1)) == (_iota2((W, W), 1) & (DH - 1)), 1.0, 0.0).astype(f32)

        def norm_bwd(x, dn, gain):
            r = lax.rsqrt(_group_mean(x * x, gm) + EPS)
            gg = jnp.sum(dn * x * r, axis=0, keepdims=True)
            u = dn * gain
            return r * u - x * (r * r * r) * _group_mean(u * x, gm), gg

        q, k, gate = q_ref[...], k_ref[...], g_ref[...]
        dq, ggq = norm_bwd(q, dq_ref[...] * (DH ** -0.5), gq_ref[...])
        dk, ggk = norm_bwd(k, dk_ref[...], gk_ref[...])
        ggq_ref[...] += jnp.dot(jnp.broadcast_to(ggq, (8, W)), hs, precision=HI, preferred_element_type=f32)[0:1]
        ggk_ref[...] += jnp.dot(jnp.broadcast_to(ggk, (8, W)), hs, precision=HI, preferred_element_type=f32)[0:1]
        dgate = dy_ref[...] * o_ref[...] * _dsilu(gate)
        dck_v = dcq_ref[...]
        for hp in range(NH // HP):
            dck_v = dck_v + dck_ref[hp]
        rc = jnp.dot(_upper_tri(tm), dck_v, precision=HI, preferred_element_type=f32) + carry[...]
        carry[...] += jnp.sum(dck_v, axis=0, keepdims=True)
        f = f_ref[...] + fb_ref[...]
        df = jnp.where(_iota2((tm, 128), 1) < NH, rc * _sigmoid(-f), 0.0)
        gfb_ref[...] += jnp.sum(df, axis=0, keepdims=True)
        dz_ref[:, 0:W] = dq.astype(bf16)
        dz_ref[:, W:2 * W] = dk.astype(bf16)
        dz_ref[:, 2 * W:3 * W] = dv_ref[...].astype(bf16)
        dz_ref[:, 3 * W:4 * W] = dgate.astype(bf16)
        dz_ref[:, 4 * W:4 * W + 128] = df.astype(bf16)

    rev = lambda i: nt - 1 - i
    zb = lambda col: pl.BlockSpec((tm, W), lambda i, c=col: (rev(i), c))
    hspec = pl.BlockSpec((tm, W), lambda i: (rev(i), 0))
    return _pcall(
        body, name="attn_post", grid=(nt,),
        in_specs=[zb(c0), zb(c0 + 1), zb(c0 + 3),
                  pl.BlockSpec((tm, 128), lambda i: (rev(i), OFF_F // 128)),
                  pl.BlockSpec((tm, W), lambda i: (rev(i), 0)),
                  hspec, hspec, hspec, hspec,
                  pl.BlockSpec((NH // HP, tm, 128), lambda i: (0, rev(i), 0)),
                  pl.BlockSpec((tm, 128), lambda i: (rev(i), 0)),
                  pl.BlockSpec((1, 128), lambda i: (0, 0)),
                  pl.BlockSpec((1, W), lambda i: (0, 0)), pl.BlockSpec((1, W), lambda i: (0, 0)),
                  pl.BlockSpec(memory_space=pl.ANY)],
        out_specs=[pl.BlockSpec((tm, 4 * W + 128), lambda i: (rev(i), OFF_ATT // (4 * W + 128))),
                   pl.BlockSpec((1, W), lambda i: (0, 0)), pl.BlockSpec((1, W), lambda i: (0, 0)),
                   pl.BlockSpec((1, 128), lambda i: (0, 0))],
        out_shape=[jax.ShapeDtypeStruct((T, NZ), bf16), jax.ShapeDtypeStruct((1, W), f32),
                   jax.ShapeDtypeStruct((1, W), f32), jax.ShapeDtypeStruct((1, 128), f32)],
        scratch_shapes=[pltpu.VMEM((1, 128), f32)],
        input_output_aliases={14: 0},
        compiler_params=_params("arbitrary"),
    )(z, z, z, z, dy, oh, dqh, dkh, dvh, dck, dcq, fbias, gq, gk, dzbuf)


def _merge_fwd(ya, oh, z, yc, yd, mb, x, p, wup, wo, gp, wpg, wpp, tm):
    T = x.shape[0]
    cg = OFF_ATT // W + 3

    def body(ya_ref, oh_ref, bg_ref, yc_ref, yd_ref, ml_ref, mb_ref, x_ref, p_ref, wup_ref, wo_ref, gp_ref,
             wpg_ref, wpp_ref, yb_ref, mg_ref, x1_ref, x2_ref):
        yb = (oh_ref[...] * _silu(bg_ref[...])).astype(bf16)
        yb_ref[...] = yb
        ys = (ya_ref[...], yb, yc_ref[...], yd_ref[...])
        merged = jnp.zeros((tm, D), f32)
        for b in range(NBR):
            sg = _sigmoid(ml_ref[:, b * D:(b + 1) * D] + mb_ref[b:b + 1, :])
            merged = merged + sg * _mm(ys[b], wup_ref[b])
        mgb = merged.astype(bf16)
        mg_ref[...] = mgb
        x1 = x_ref[...] + _mm(mgb, wo_ref[...])
        x1_ref[...] = x1
        r = lax.rsqrt(jnp.mean(x1 * x1, axis=-1, keepdims=True) + EPS)
        hp = (x1 * r * gp_ref[...]).astype(bf16)
        gate = _sigmoid(_mm(hp, wpg_ref[...]))
        x2_ref[...] = x1 + gate * _mm(p_ref[...].astype(bf16), wpp_ref[...])

    row = lambda width: pl.BlockSpec((tm, width), lambda i: (i, 0))
    full = lambda *shape: pl.BlockSpec(shape, lambda i: (0,) * len(shape))
    return _pcall(
        body, name="merge_fwd", grid=(T // tm,),
        in_specs=[row(W), row(W), _zblock(tm, cg), row(W), row(W),
                  pl.BlockSpec((tm, NBR * D), lambda i: (i, 0)), full(NBR, D), row(D), row(PLE),
                  full(NBR, W, D), full(D, D), full(1, D), full(D, D), full(PLE, D)],
        out_specs=[row(W), row(D), row(D), row(D)],
        out_shape=[jax.ShapeDtypeStruct((T, W), bf16), jax.ShapeDtypeStruct((T, D), bf16),
                   jax.ShapeDtypeStruct((T, D), f32), jax.ShapeDtypeStruct((T, D), f32)],
        compiler_params=_params("parallel"),
    )(ya, oh, z, yc, yd, z, mb, x, p, wup, wo, gp, wpg, wpp)


def _layer_slabs(li, bufs):
    if bufs is None:
        return [], []
    return list(bufs), [pl.BlockSpec(memory_space=pl.ANY)] * len(bufs)


def _ple_bwd(dx2, x1, p, gp, wpg, wpp, tm, li, bufs):
    T = x1.shape[0]
    SH = D // N_DEV
    nt = T // tm
    extra, extra_specs = _layer_slabs(li, bufs)

    def body(dx2_ref, x1_ref, p_ref, gp_ref, wpg_ref, wpp_ref, *rest):
        dx1_ref, gwpg_ref, gwpp_ref, ggp_ref, gwpg_acc, gwpp_acc = rest[len(extra):]

        @pl.when(pl.program_id(0) == 0)
        def _():
            gwpg_acc[...] = jnp.zeros_like(gwpg_acc)
            gwpp_acc[...] = jnp.zeros_like(gwpp_acc)
            ggp_ref[...] = jnp.zeros_like(ggp_ref)

        x1, dx2 = x1_ref[...], dx2_ref[...]
        r = lax.rsqrt(jnp.mean(x1 * x1, axis=-1, keepdims=True) + EPS)
        xh = x1 * r
        gp = gp_ref[...]
        hp = (xh * gp).astype(bf16)
        gate = _sigmoid(_mm(hp, wpg_ref[...]))
        pb = p_ref[...].astype(bf16)
        pp = _mm(pb, wpp_ref[...])
        dpre = ((dx2 * pp) * gate * (1.0 - gate)).astype(bf16)
        gwpp_acc[...] += _mm_tn(pb, (dx2 * gate).astype(bf16))
        gwpg_acc[...] += _mm_tn(hp, dpre)
        dhp = _mm_nt(dpre, wpg_ref[...])
        ggp_ref[...] += jnp.sum(dhp * xh, axis=0, keepdims=True)
        u = dhp * gp
        dx1_ref[...] = dx2 + r * u - x1 * (r * r * r) * jnp.mean(u * x1, axis=-1, keepdims=True)

        @pl.when(pl.program_id(0) == nt - 1)
        def _():
            gwpg_ref[...] = gwpg_acc[...].reshape(N_DEV, SH, D).astype(bf16)
            for d in range(N_DEV):
                gwpp_ref[d] = gwpp_acc[:, d * SH:(d + 1) * SH].astype(bf16)

    row = lambda width: pl.BlockSpec((tm, width), lambda i: (i, 0))
    full = lambda *shape: pl.BlockSpec(shape, lambda i: (0,) * len(shape))
    n_in = 6
    return _pcall(
        body, name="ple_bwd", grid=(nt,),
        in_specs=[row(D), row(D), row(PLE), full(1, D), full(D, D), full(PLE, D)] + extra_specs,
        out_specs=[row(D), pl.BlockSpec((N_DEV, SH, D), lambda i: (0, li, 0)),
                   pl.BlockSpec((N_DEV, PLE, SH), lambda i: (0, li, 0)), full(1, D)],
        out_shape=[jax.ShapeDtypeStruct((T, D), f32), jax.ShapeDtypeStruct((N_DEV, DEPTH * SH, D), bf16),
                   jax.ShapeDtypeStruct((N_DEV, DEPTH * PLE, SH), bf16), jax.ShapeDtypeStruct((1, D), f32)],
        scratch_shapes=[pltpu.VMEM((D, D), f32), pltpu.VMEM((PLE, D), f32)],
        input_output_aliases={n_in + k: 1 + k for k in range(len(extra))},
        compiler_params=_params("arbitrary"),
    )(dx2, x1, p, gp, wpg, wpp, *extra)


def _merge_bwd(dx1, mg, ya, yb, yc, yd, z, mb, wup, wo, tm, li, bufs):
    T = dx1.shape[0]
    SH = D // N_DEV
    nt = T // tm
    extra, extra_specs = _layer_slabs(li, bufs)

    def body(dx1_ref, mg_ref, ya_ref, yb_ref, yc_ref, yd_ref, ml_ref, mb_ref, wup_ref, wo_ref, *rest):
        dml_ref, dya_ref, dyb_ref, dyc_ref, dyd_ref, gwo_ref, gwup_ref, gmb_ref, gwo_acc, gwup_acc = rest[len(extra):]

        @pl.when(pl.program_id(0) == 0)
        def _():
            gwo_acc[...] = jnp.zeros_like(gwo_acc)
            gwup_acc[...] = jnp.zeros_like(gwup_acc)
            gmb_ref[...] = jnp.zeros_like(gmb_ref)

        dx1b = dx1_ref[...].astype(bf16)
        gwo_acc[...] += _mm_tn(mg_ref[...], dx1b)
        dm = _mm_nt(dx1b, wo_ref[...])
        ys = (ya_ref, yb_ref, yc_ref, yd_ref)
        dys = (dya_ref, dyb_ref, dyc_ref, dyd_ref)
        for b in range(NBR):
            y = ys[b][...]
            up = _mm(y, wup_ref[b])
            sg = _sigmoid(ml_ref[:, b * D:(b + 1) * D] + mb_ref[b:b + 1, :])
            dup = (dm * sg).astype(bf16)
            dml = dm * up * sg * (1.0 - sg)
            gmb_ref[b:b + 1, :] += jnp.sum(dml, axis=0, keepdims=True)
            dml_ref[:, b * D:(b + 1) * D] = dml.astype(bf16)
            gwup_acc[b] += _mm_tn(y, dup)
            dys[b][...] = _mm_nt(dup, wup_ref[b])

        @pl.when(pl.program_id(0) == nt - 1)
        def _():
            gwo_ref[...] = gwo_acc[...].reshape(N_DEV, SH, D).astype(bf16)
            for d in range(N_DEV):
                gwup_ref[d] = gwup_acc[:, :, d * SH:(d + 1) * SH].reshape(NBR * W, SH).astype(bf16)

    row = lambda width: pl.BlockSpec((tm, width), lambda i: (i, 0))
    full = lambda *shape: pl.BlockSpec(shape, lambda i: (0,) * len(shape))
    n_in = 10
    return _pcall(
        body, name="merge_bwd", grid=(nt,),
        in_specs=[row(D), row(D), row(W), row(W), row(W), row(W), row(NBR * D), full(NBR, D),
                  full(NBR, W, D), full(D, D)] + extra_specs,
        out_specs=[row(NBR * D), row(W), row(W), row(W), row(W),
                   pl.BlockSpec((N_DEV, SH, D), lambda i: (0, li, 0)),
                   pl.BlockSpec((N_DEV, NBR * W, SH), lambda i: (0, li, 0)), full(NBR, D)],
        out_shape=[jax.ShapeDtypeStruct((T, NZ), bf16)] + [jax.ShapeDtypeStruct((T, W), f32)] * 4
        + [jax.ShapeDtypeStruct((N_DEV, DEPTH * SH, D), bf16),
           jax.ShapeDtypeStruct((N_DEV, DEPTH * NBR * W, SH), bf16),
           jax.ShapeDtypeStruct((NBR, D), f32)],
        scratch_shapes=[pltpu.VMEM((D, D), f32), pltpu.VMEM((NBR, W, D), f32)],
        input_output_aliases={n_in + k: 5 + k for k in range(len(extra))},
        compiler_params=_params("arbitrary"),
    )(dx1, mg, ya, yb, yc, yd, z, mb, wup, wo, *extra)


def _loss_head(y, target, tm):
    T = y.shape[0]

    def body(y_ref, t_ref, loss_ref, dy_ref, acc):
        i = pl.program_id(0)

        @pl.when(i == 0)
        def _():
            acc[...] = jnp.zeros_like(acc)

        e = y_ref[...] - t_ref[...]
        dy_ref[...] = e * (1.0 / D)
        acc[...] += jnp.sum(e * e, axis=0, keepdims=True)

        @pl.when(i == T // tm - 1)
        def _():
            loss_ref[...] = jnp.sum(acc[...], axis=1, keepdims=True) * (0.5 / D)

    return _pcall(
        body, name="loss_head", grid=(T // tm,),
        in_specs=[pl.BlockSpec((tm, D), lambda i: (i, 0)), pl.BlockSpec((tm, D), lambda i: (i, 0))],
        out_specs=[pl.BlockSpec((1, 1), lambda i: (0, 0)), pl.BlockSpec((tm, D), lambda i: (i, 0))],
        out_shape=[jax.ShapeDtypeStruct((1, 1), f32), jax.ShapeDtypeStruct((T, D), f32)],
        scratch_shapes=[pltpu.VMEM((1, D), f32)],
        compiler_params=_params("arbitrary"),
    )(y, target)


def _lb_softmax_rows(l_ref):
    rows = [l_ref[i:i + 1, :] for i in range(DEPTH)]
    m = rows[0]
    for r in rows[1:]:
        m = jnp.maximum(m, r)
    es = [jnp.exp(r - m) for r in rows]
    tot = es[0]
    for e in es[1:]:
        tot = tot + e
    return [e / tot for e in es]


def _lb_partial_sums(pr):
    sums = [jnp.zeros_like(pr[0])]
    for i in range(1, DEPTH):
        sums.append(sums[-1] + pr[i])
    return sums


def _stack_rows(rows, width):
    idx = _iota2((8, width), 0)
    out = jnp.zeros((8, width), f32)
    for i, r in enumerate(rows):
        out = jnp.where(idx == i, r, out)
    return out


def _lower_bounds(lb_logits):
    def body(l_ref, o_ref):
        sums = _lb_partial_sums(_lb_softmax_rows(l_ref))
        o_ref[...] = _stack_rows([jnp.clip(s, 0.0, 1.0) for s in sums], W)

    return _pcall(body, name="lower_bounds", out_shape=jax.ShapeDtypeStruct((8, W), f32))(lb_logits)


def _lower_bounds_bwd(lb_logits, dlower):
    def body(l_ref, d_ref, o_ref):
        pr = _lb_softmax_rows(l_ref)
        sums = _lb_partial_sums(pr)
        dl = [jnp.where((sums[i] > 0.0) & (sums[i] < 1.0), d_ref[i:i + 1, :], 0.0) for i in range(DEPTH)]
        dp = [jnp.zeros_like(pr[0])] * DEPTH
        run = jnp.zeros_like(pr[0])
        for j in reversed(range(1, DEPTH)):
            run = run + dl[j]
            dp[j] = run
        inner = pr[0] * dp[0]
        for j in range(1, DEPTH):
            inner = inner + pr[j] * dp[j]
        o_ref[...] = _stack_rows([pr[j] * (dp[j] - inner) for j in range(DEPTH)], W)

    return _pcall(body, name="lower_bounds_bwd", out_shape=jax.ShapeDtypeStruct((8, W), f32))(lb_logits, dlower)


def _row_tile(rows, cols, budget_bytes=1 << 20, mult=8):
    if rows % mult:
        return rows
    best = mult
    for t in range(mult, rows + 1, mult):
        if rows % t == 0 and t * cols * 4 <= budget_bytes:
            best = t
    return best


def _sum_slabs(land):
    N, R, C = land.shape
    tr = _row_tile(R, C * N, mult=16)

    def body(l_ref, o_ref):
        acc = l_ref[0].astype(f32)
        for j in range(1, N):
            acc = acc + l_ref[j].astype(f32)
        o_ref[...] = acc

    return _pcall(
        body, name="sum_slabs", grid=(R // tr,),
        in_specs=[pl.BlockSpec((N, tr, C), lambda i: (0, i, 0))],
        out_specs=pl.BlockSpec((tr, C), lambda i: (i, 0)),
        out_shape=jax.ShapeDtypeStruct((R, C), f32),
        compiler_params=_params("parallel"),
    )(land)


def _adamw_update(w_ref, g_ref, m_ref, v_ref, d_ref, nm_ref, nv_ref):
    c1 = 1.0 / (1.0 - ADAM_B1 ** ADAM_STEP)
    c2 = 1.0 / (1.0 - ADAM_B2 ** ADAM_STEP)
    gv = g_ref[...]
    nm = ADAM_B1 * m_ref[...] + (1.0 - ADAM_B1) * gv
    nv = ADAM_B2 * v_ref[...] + (1.0 - ADAM_B2) * (gv * gv)
    nm_ref[...] = nm
    nv_ref[...] = nv
    d_ref[...] = -ADAM_LR * ((nm * c1) / (jnp.sqrt(nv * c2) + ADAM_EPS) + ADAM_WD * w_ref[...])


def _adamw3(w, g, m, v):
    L, R, C = w.shape
    tr = _row_tile(R, C)

    def body(*refs):
        _adamw_update(*refs)

    spec = pl.BlockSpec((None, tr, C), lambda l, i: (l, i, 0))
    return _pcall(
        body, name="adamw3", grid=(L, R // tr),
        in_specs=[spec] * 4, out_specs=[spec] * 3,
        out_shape=[jax.ShapeDtypeStruct((L, R, C), f32)] * 3,
        compiler_params=_params("parallel", "parallel"),
    )(w, g, m, v)


def _adamw(w, g, m, v):
    if w.ndim == 3:
        return _adamw3(w, g, m, v)
    R, C = w.shape
    tr = _row_tile(R, C)
    c1 = 1.0 / (1.0 - ADAM_B1 ** ADAM_STEP)
    c2 = 1.0 / (1.0 - ADAM_B2 ** ADAM_STEP)

    def body(w_ref, g_ref, m_ref, v_ref, d_ref, nm_ref, nv_ref):
        gv = g_ref[...]
        nm = ADAM_B1 * m_ref[...] + (1.0 - ADAM_B1) * gv
        nv = ADAM_B2 * v_ref[...] + (1.0 - ADAM_B2) * (gv * gv)
        nm_ref[...] = nm
        nv_ref[...] = nv
        d_ref[...] = -ADAM_LR * ((nm * c1) / (jnp.sqrt(nv * c2) + ADAM_EPS) + ADAM_WD * w_ref[...])

    spec = pl.BlockSpec((tr, C), lambda i: (i, 0))
    return _pcall(
        body, name="adamw", grid=(R // tr,),
        in_specs=[spec] * 4, out_specs=[spec] * 3,
        out_shape=[jax.ShapeDtypeStruct((R, C), f32)] * 3,
        compiler_params=_params("parallel"),
    )(w, g, m, v)


def _my_id():
    return lax.axis_index("x") * 4 + lax.axis_index("y") * 2 + lax.axis_index("c")


def _peer(k):
    x, y, c = lax.axis_index("x"), lax.axis_index("y"), lax.axis_index("c")
    kx, ky, kc = (k >> 2) & 1, (k >> 1) & 1, k & 1
    px, py, pc = x ^ kx, y ^ ky, c ^ kc
    return (px, py, pc), px * 4 + py * 2 + pc


def _all_gather(shards, axes):
    n = len(shards)

    def full_shape(s, ax):
        shp = list(s.shape)
        shp[ax] *= N_DEV
        return tuple(shp)

    def body(*refs):
        srcs, outs = refs[:n], refs[n:2 * n]
        send_sems, recv_sems, local_sems = refs[2 * n:]
        x, y, c = lax.axis_index("x"), lax.axis_index("y"), lax.axis_index("c")
        me, sibling = (x, y, c), (x, y, 1 - c)
        chips = [(1 - x, y), (x, 1 - y), (1 - x, 1 - y)]

        def block(a, dev):
            j = dev[0] * 4 + dev[1] * 2 + dev[2]
            size = shards[a].shape[axes[a]]
            start = pl.multiple_of(j * size, size)
            if axes[a] == 0:
                return outs[a].at[pl.ds(start, size), :]
            if axes[a] == 1:
                return outs[a].at[:, pl.ds(start, size), :]
            return outs[a].at[:, pl.ds(start, size)]

        def copy(a, k, dev, to, src=None):
            return pltpu.make_async_remote_copy(
                src_ref=block(a, dev) if src is None else src, dst_ref=block(a, dev),
                send_sem=send_sems.at[a, k], recv_sem=recv_sems.at[a, k],
                device_id=to, device_id_type=pl.DeviceIdType.MESH)

        mine = [pltpu.make_async_copy(srcs[a], block(a, me), local_sems.at[a]) for a in range(n)]
        for cp in mine:
            cp.start()
        first = []
        for a in range(n):
            first.append(copy(a, 0, me, sibling, src=srcs[a]))
            first += [copy(a, 1 + j, me, (*chip, c), src=srcs[a]) for j, chip in enumerate(chips)]
        for cp in first:
            cp.start()
        passed = []
        for j, chip in enumerate(chips):
            for a in range(n):
                copy(a, 1 + j, (*chip, c), me).wait_recv()
                cp = copy(a, 4 + j, (*chip, c), sibling)
                cp.start()
                passed.append(cp)
        for a in range(n):
            copy(a, 0, sibling, me).wait_recv()
            for j, chip in enumerate(chips):
                copy(a, 4 + j, (*chip, 1 - c), me).wait_recv()
        for cp in first + passed:
            cp.wait_send()
        for cp in mine:
            cp.wait()

    hbm = pl.BlockSpec(memory_space=pltpu.HBM)
    return _pcall(
        body, name="all_gather",
        in_specs=[hbm] * n, out_specs=[hbm] * n,
        out_shape=[jax.ShapeDtypeStruct(full_shape(s, ax), s.dtype) for s, ax in zip(shards, axes)],
        scratch_shapes=[pltpu.SemaphoreType.DMA((n, N_DEV - 1)), pltpu.SemaphoreType.DMA((n, N_DEV - 1)),
                        pltpu.SemaphoreType.DMA((n,))],
    )(*shards)


N_CHIP = N_DEV // 2


def _exchange_sibling(sliced):
    n = len(sliced)

    def body(*refs):
        srcs, outs = refs[:n], refs[n:2 * n]
        send_sems, recv_sems = refs[2 * n:]
        x, y, c = lax.axis_index("x"), lax.axis_index("y"), lax.axis_index("c")
        copies = []
        for a in range(n):
            for q in range(N_CHIP):
                cp = pltpu.make_async_remote_copy(
                    src_ref=srcs[a].at[2 * q + (1 - c)], dst_ref=outs[a].at[q],
                    send_sem=send_sems.at[a, q], recv_sem=recv_sems.at[a, q],
                    device_id=(x, y, 1 - c), device_id_type=pl.DeviceIdType.MESH)
                cp.start()
                copies.append(cp)
        for cp in copies:
            cp.wait_recv()
        for cp in copies:
            cp.wait_send()

    hbm = pl.BlockSpec(memory_space=pltpu.HBM)
    return _pcall(
        body, name="grad_exchange_sibling",
        in_specs=[hbm] * n, out_specs=[hbm] * n,
        out_shape=[jax.ShapeDtypeStruct((N_CHIP,) + s.shape[1:], s.dtype) for s in sliced],
        scratch_shapes=[pltpu.SemaphoreType.DMA((n, N_CHIP)), pltpu.SemaphoreType.DMA((n, N_CHIP))],
    )(*sliced)


def _pair_sum(own, recv):
    _, R, C = own.shape
    tr = _row_tile(R, C, mult=16)
    side = lax.axis_index("c").astype(jnp.int32).reshape(1)

    def body(c_ref, own_ref, recv_ref, o_ref):
        o_ref[...] = (own_ref[...].astype(f32) + recv_ref[...].astype(f32)).astype(o_ref.dtype)

    return _pcall(
        body, name="pair_sum",
        grid_spec=pltpu.PrefetchScalarGridSpec(
            num_scalar_prefetch=1, grid=(N_CHIP, R // tr),
            in_specs=[pl.BlockSpec((None, tr, C), lambda q, i, c: (2 * q + c[0], i, 0)),
                      pl.BlockSpec((None, tr, C), lambda q, i, c: (q, i, 0))],
            out_specs=pl.BlockSpec((None, tr, C), lambda q, i, c: (q, i, 0))),
        out_shape=jax.ShapeDtypeStruct((N_CHIP, R, C), own.dtype),
        compiler_params=_params("parallel", "parallel"),
    )(side, own, recv)


def _exchange_chips(partial, whole):
    ns, nw = len(partial), len(whole)

    def body(*refs):
        srcs, outs = refs[:ns + nw], refs[ns + nw:2 * (ns + nw)]
        send_sems, recv_sems, wsend_sems, wrecv_sems, local_sems = refs[2 * (ns + nw):]
        x, y, c = lax.axis_index("x"), lax.axis_index("y"), lax.axis_index("c")
        me, myq = _my_id(), x * 2 + y
        chips = [(1 - x, y), (x, 1 - y), (1 - x, 1 - y)]
        locals_ = [pltpu.make_async_copy(srcs[a].at[myq], outs[a].at[myq], local_sems.at[a]) for a in range(ns)]
        locals_ += [pltpu.make_async_copy(srcs[ns + b], outs[ns + b].at[me], local_sems.at[ns + b])
                    for b in range(nw)]
        for cp in locals_:
            cp.start()
        sends, recvs = [], []
        for j, chip in enumerate(chips):
            q = chip[0] * 2 + chip[1]
            for a in range(ns):
                cp = pltpu.make_async_remote_copy(
                    src_ref=srcs[a].at[q], dst_ref=outs[a].at[myq],
                    send_sem=send_sems.at[a, j], recv_sem=recv_sems.at[a, j],
                    device_id=(*chip, c), device_id_type=pl.DeviceIdType.MESH)
                cp.start()
                sends.append(cp)
                recvs.append(pltpu.make_async_remote_copy(
                    src_ref=srcs[a].at[q], dst_ref=outs[a].at[q],
                    send_sem=send_sems.at[a, j], recv_sem=recv_sems.at[a, j],
                    device_id=(*chip, c), device_id_type=pl.DeviceIdType.MESH))
        for k in range(1, N_DEV):
            peer, pid = _peer(k)
            for b in range(nw):
                cp = pltpu.make_async_remote_copy(
                    src_ref=srcs[ns + b], dst_ref=outs[ns + b].at[me],
                    send_sem=wsend_sems.at[b, k - 1], recv_sem=wrecv_sems.at[b, k - 1],
                    device_id=peer, device_id_type=pl.DeviceIdType.MESH)
                cp.start()
                sends.append(cp)
                recvs.append(pltpu.make_async_remote_copy(
                    src_ref=srcs[ns + b], dst_ref=outs[ns + b].at[pid],
                    send_sem=wsend_sems.at[b, k - 1], recv_sem=wrecv_sems.at[b, k - 1],
                    device_id=peer, device_id_type=pl.DeviceIdType.MESH))
        for cp in recvs:
            cp.wait_recv()
        for cp in sends:
            cp.wait_send()
        for cp in locals_:
            cp.wait()

    hbm = pl.BlockSpec(memory_space=pltpu.HBM)
    shapes = [jax.ShapeDtypeStruct(s.shape, s.dtype) for s in partial]
    shapes += [jax.ShapeDtypeStruct((N_DEV,) + s.shape, s.dtype) for s in whole]
    return _pcall(
        body, name="grad_exchange_chips",
        in_specs=[hbm] * (ns + nw), out_specs=[hbm] * (ns + nw), out_shape=shapes,
        scratch_shapes=[pltpu.SemaphoreType.DMA((ns, N_CHIP - 1)), pltpu.SemaphoreType.DMA((ns, N_CHIP - 1)),
                        pltpu.SemaphoreType.DMA((nw, N_DEV - 1)), pltpu.SemaphoreType.DMA((nw, N_DEV - 1)),
                        pltpu.SemaphoreType.DMA((ns + nw,))],
    )(*partial, *whole)


def _permute_cols(w):
    pad = jnp.zeros(w.shape[:-1] + (NZ - OFF_F - NH,), w.dtype)
    return jnp.concatenate([
        w[..., 3844:7940],
        w[..., 0:1024],
        w[..., 2052:3076],
        w[..., 3076:3844],
        w[..., 1024:2048],
        w[..., 2048:2052], pad], axis=-1)


def _unpermute_cols(g):
    return jnp.concatenate([
        g[..., OFF_CONV:OFF_CONV + 1024],
        g[..., OFF_ATT:OFF_ATT + 1024],
        g[..., OFF_F:OFF_F + NH],
        g[..., OFF_HGRN:OFF_HGRN + 1024],
        g[..., OFF_SGU:OFF_SGU + 768],
        g[..., 0:4096]], axis=-1)


_SMALL = (
    ("norm_mix", (DEPTH, D)), ("conv_w", (DEPTH, CONV_WIDTH, W)), ("conv_b", (DEPTH, W)),
    ("fgate_bias", (DEPTH, NH)), ("q_norm", (DEPTH, DH)), ("k_norm", (DEPTH, DH)),
    ("lb_logits", (DEPTH, W)), ("hgrn_norm", (DEPTH, W)), ("sgu_norm", (DEPTH, W)),
    ("spatial_w", (DEPTH, NH, SGU_CHUNK, SGU_CHUNK)), ("spatial_b", (DEPTH, NH, SGU_CHUNK)),
    ("merge_b", (DEPTH, NBR, D)), ("norm_ple", (DEPTH, D)),
)


def _small_rows(shape):
    size = 1
    for s in shape:
        size *= s
    rows = -(-size // 128)
    return size, -(-rows // 8) * 8


def _pack_small(parts):
    out = []
    for name, shape in _SMALL:
        size, rows = _small_rows(shape)
        flat = parts[name].astype(f32).reshape(-1)
        flat = jnp.pad(flat, (0, rows * 128 - size))
        out.append(flat.reshape(rows, 128))
    return jnp.concatenate(out, axis=0)


def _unpack_small(buf):
    parts, r0 = {}, 0
    for name, shape in _SMALL:
        size, rows = _small_rows(shape)
        parts[name] = buf[r0:r0 + rows].reshape(-1)[:size].reshape(shape)
        r0 += rows
    return parts


def _shard_cols(a, width):
    return lax.dynamic_slice_in_dim(a, _my_id() * width, width, axis=a.ndim - 1)


def kernel(x, p, norm_mix, w_in, conv_w, conv_b, fgate_bias, q_norm, k_norm, lb_logits, hgrn_norm, sgu_norm, spatial_w, spatial_b, w_up, merge_b, w_o, norm_ple, w_ple_gate, w_ple_proj, loss_target, m_norm_mix, m_w_in, m_conv_w, m_conv_b, m_fgate_bias, m_q_norm, m_k_norm, m_lb_logits, m_hgrn_norm, m_sgu_norm, m_spatial_w, m_spatial_b, m_w_up, m_merge_b, m_w_o, m_norm_ple, m_w_ple_gate, m_w_ple_proj, v_norm_mix, v_w_in, v_conv_w, v_conv_b, v_fgate_bias, v_q_norm, v_k_norm, v_lb_logits, v_hgrn_norm, v_sgu_norm, v_spatial_w, v_spatial_b, v_w_up, v_merge_b, v_w_o, v_norm_ple, v_w_ple_gate, v_w_ple_proj):
    T = x.shape[1]
    SH = D // N_DEV
    CW = W // N_DEV
    tm = 512 if T % 512 == 0 else T
    tmm = 256 if T % 256 == 0 else T
    x0 = x.reshape(T, D)
    target = loss_target.reshape(T, D)

    small_shard = jnp.concatenate([
        merge_b.reshape(DEPTH * NBR, SH),
        jnp.pad(conv_w.reshape(DEPTH * CONV_WIDTH, CW), ((0, 16 - DEPTH * CONV_WIDTH), (0, SH - CW)))], axis=0)
    win_s = _permute_cols(w_in).astype(bf16)
    *win_f, wup_f, wo_f, wpg_f, wpp_f, g_small = _all_gather(
        [win_s[li] for li in range(DEPTH)]
        + [w_up.astype(bf16).reshape(DEPTH * NBR * W, SH),
           w_o.astype(bf16),
           w_ple_gate.astype(bf16),
           w_ple_proj.astype(bf16).reshape(DEPTH * PLE, SH),
           small_shard],
        [0] * DEPTH + [-1, 1, 1, -1, -1])
    wup_f = wup_f.reshape(DEPTH, NBR, W, D)
    wpp_f = wpp_f.reshape(DEPTH, PLE, D)
    mb_f = g_small[0:DEPTH * NBR].reshape(DEPTH, NBR, D)
    cw_f = g_small[16:16 + DEPTH * CONV_WIDTH].reshape(DEPTH, CONV_WIDTH, N_DEV, SH)[..., 0:CW]
    cw_f = cw_f.reshape(DEPTH, CONV_WIDTH, W)

    loss_local, dx, gw, gs_full = _forward_backward(
        x0, p[:, 0], target, win_f, wup_f, wo_f, wpg_f, wpp_f, mb_f, cw_f, norm_mix, conv_b, fgate_bias, q_norm,
        k_norm, lb_logits, hgrn_norm, sgu_norm, spatial_w, spatial_b, norm_ple)
    loss = lax.psum(loss_local[0, 0], AXES)
    grad_x = dx.reshape(1, T, D)

    weights = dict(norm_mix=norm_mix, w_in=w_in, conv_w=conv_w, conv_b=conv_b, fgate_bias=fgate_bias, q_norm=q_norm,
                   k_norm=k_norm, lb_logits=lb_logits, hgrn_norm=hgrn_norm, sgu_norm=sgu_norm, spatial_w=spatial_w,
                   spatial_b=spatial_b, w_up=w_up, merge_b=merge_b, w_o=w_o, norm_ple=norm_ple,
                   w_ple_gate=w_ple_gate, w_ple_proj=w_ple_proj)
    ms = dict(norm_mix=m_norm_mix, w_in=m_w_in, conv_w=m_conv_w, conv_b=m_conv_b, fgate_bias=m_fgate_bias,
              q_norm=m_q_norm, k_norm=m_k_norm, lb_logits=m_lb_logits, hgrn_norm=m_hgrn_norm, sgu_norm=m_sgu_norm,
              spatial_w=m_spatial_w, spatial_b=m_spatial_b, w_up=m_w_up, merge_b=m_merge_b, w_o=m_w_o,
              norm_ple=m_norm_ple, w_ple_gate=m_w_ple_gate, w_ple_proj=m_w_ple_proj)
    vs = dict(norm_mix=v_norm_mix, w_in=v_w_in, conv_w=v_conv_w, conv_b=v_conv_b, fgate_bias=v_fgate_bias,
              q_norm=v_q_norm, k_norm=v_k_norm, lb_logits=v_lb_logits, hgrn_norm=v_hgrn_norm, sgu_norm=v_sgu_norm,
              spatial_w=v_spatial_w, spatial_b=v_spatial_b, w_up=v_w_up, merge_b=v_merge_b, w_o=v_w_o,
              norm_ple=v_norm_ple, w_ple_gate=v_w_ple_gate, w_ple_proj=v_w_ple_proj)
    return _exchange_and_update(loss, grad_x, gw, gs_full, weights, ms, vs)


def _forward_backward(x0, p, target, win_f, wup_f, wo_f, wpg_f, wpp_f, mb_f, cw_f, norm_mix, conv_b, fgate_bias,
                      q_norm, k_norm, lb_logits, hgrn_norm, sgu_norm, spatial_w, spatial_b, norm_ple):
    T = x0.shape[0]
    tm = 512 if T % 512 == 0 else T
    tmm = 256 if T % 256 == 0 else T
    tmi = 1024 if T % 1024 == 0 else tm
    lower = _lower_bounds(lb_logits)
    fb_pad = jnp.pad(fgate_bias, ((0, 0), (0, 128 - NH)))
    gq_t = jnp.tile(q_norm, (1, NH))
    gk_t = jnp.tile(k_norm, (1, NH))
    sbe = jnp.repeat(jnp.swapaxes(spatial_b, 1, 2), DH, axis=2)

    saved = []
    xc = x0
    p = p[:, None]
    for li in range(DEPTH):
        row = lambda a: a[li:li + 1]
        z, h = _inproj_fwd(xc, row(norm_mix), win_f[li], tmi)
        ya = _conv_fwd(z, cw_f[li], row(conv_b), tm)
        yd = _sgu_fwd(z, row(sgu_norm), spatial_w[li], sbe[li], tm)
        yc, o_pre, states = _hgrn_fwd(z, lower[li:li + 1], row(hgrn_norm), tmm)
        qt, kt, vt, kh, vh, cum = _attn_prep(z, row(fb_pad), row(gq_t), row(gk_t), tm)
        cum4 = jnp.transpose(cum[:, 0:NH])
        ccol, crow = cum4[:, :, None], cum4[:, None, :]
        oh, lse = _attn_fwd(qt, kh, vt, crow, ccol, tm)
        yb, mg, x1, x2 = _merge_fwd(ya, oh, z, yc, yd, mb_f[li], xc, p[li, 0], wup_f[li], wo_f[li],
                                    row(norm_ple), wpg_f[li], wpp_f[li], tmm)
        saved.append(dict(x=xc, z=z, h=h, ya=ya, yb=yb, yc=yc, yd=yd, o_pre=o_pre, states=states,
                          qt=qt, kt=kt, kh=kh, vh=vh, crow=crow, ccol=ccol, oh=oh, lse=lse, mg=mg, x1=x1))
        xc = x2

    loss_local, dx = _loss_head(xc, target, tm)

    gw = dict(w_in=None, w_up=None, w_o=None, w_ple_gate=None, w_ple_proj=None)
    gs = {n: [None] * DEPTH for n, _ in _SMALL}
    dlower = [None] * DEPTH
    for li in reversed(range(DEPTH)):
        s = saved[li]
        row = lambda a: a[li:li + 1]
        first = li == DEPTH - 1
        dx1, gw["w_ple_gate"], gw["w_ple_proj"], ggp = _ple_bwd(
            dx, s["x1"], p[li, 0], row(norm_ple), wpg_f[li], wpp_f[li], tmm, li,
            None if first else (gw["w_ple_gate"], gw["w_ple_proj"]))
        gs["norm_ple"][li] = ggp[0]
        dz, dya, dyb, dyc, dyd, gw["w_o"], gw["w_up"], gs["merge_b"][li] = _merge_bwd(
            dx1, s["mg"], s["ya"], s["yb"], s["yc"], s["yd"], s["z"], mb_f[li], wup_f[li], wo_f[li], tmm, li,
            None if first else (gw["w_o"], gw["w_up"]))
        dz, gcw, gcb = _conv_bwd(s["z"], dya, cw_f[li], row(conv_b), dz, tm)
        gs["conv_w"][li], gs["conv_b"][li] = gcw[0:CONV_WIDTH], gcb[0]
        dz, gs["spatial_w"][li], gsb, ggv = _sgu_bwd(s["z"], dyd, row(sgu_norm), spatial_w[li], sbe[li], dz, tm)
        gs["spatial_b"][li] = jnp.transpose(gsb[:, ::DH])
        gs["sgu_norm"][li] = ggv[0]
        dz, ggn, glb = _hgrn_bwd(s["z"], lower[li:li + 1], row(hgrn_norm), s["o_pre"], s["states"], dyc, dz, tmm)
        gs["hgrn_norm"][li], dlower[li] = ggn[0], glb[0]
        dot, delta = _attn_bwd_prep(dyb, s["oh"], s["z"], tm)
        dqh, dkh, dvh, dck, dcq = _attn_bwd(s["qt"], s["kt"], s["kh"], s["vh"], s["crow"], s["ccol"], dot,
                                            s["lse"], delta, tm)
        dcq_t = jnp.transpose(dcq[:, :, 0:HP, :], (0, 2, 1, 3)).reshape(NH, T)
        dcq_t = jnp.pad(jnp.transpose(dcq_t), ((0, 0), (0, 128 - NH)))
        dz, ggq, ggk, gfb = _attn_post(s["z"], dyb, s["oh"], dqh, dkh, dvh, dck, dcq_t, row(fb_pad),
                                       row(gq_t), row(gk_t), dz, tm)
        gs["q_norm"][li], gs["k_norm"][li], gs["fgate_bias"][li] = ggq[0, 0:DH], ggk[0, 0:DH], gfb[0, 0:NH]
        dx, gnm = _inproj_bwd_x(dz, win_f[li], s["x"], dx1, row(norm_mix), tmi)
        gs["norm_mix"][li] = gnm[0]
        gw["w_in"] = _inproj_bwd_w(s["h"], dz, tmi, li, gw["w_in"])
    dlower8 = jnp.pad(jnp.stack(dlower), ((0, 8 - DEPTH), (0, 0)))
    gs_full = {n: jnp.stack(v) for n, v in gs.items() if n != "lb_logits"}
    gs_full["lb_logits"] = _lower_bounds_bwd(lb_logits, dlower8)[0:DEPTH]
    return loss_local, dx, gw, gs_full


def _exchange_and_update(loss, grad_x, gw, gs_full, weights, ms, vs):
    SH = D // N_DEV
    CW = W // N_DEV

    small_buf = _pack_small(gs_full)
    own = [gw["w_in"], gw["w_up"], gw["w_o"], gw["w_ple_gate"], gw["w_ple_proj"]]
    from_sibling = _exchange_sibling(own)
    chip_sums = [_pair_sum(o, r) for o, r in zip(own, from_sibling)]
    l_win, l_wup, l_wo, l_wpg, l_wpp, l_small = _exchange_chips(chip_sums, [small_buf])

    g_w_in = _unpermute_cols(_sum_slabs(l_win)).reshape(DEPTH, SH, IN_COLS)
    g_w_up = _sum_slabs(l_wup).reshape(DEPTH, NBR, W, SH)
    g_w_o = _sum_slabs(l_wo).reshape(DEPTH, SH, D)
    g_w_pg = _sum_slabs(l_wpg).reshape(DEPTH, SH, D)
    g_w_pp = _sum_slabs(l_wpp).reshape(DEPTH, PLE, SH)
    g_small = _unpack_small(_sum_slabs(l_small))
    g_small_local = dict(g_small)
    g_small_local["conv_w"] = _shard_cols(g_small["conv_w"], CW)
    g_small_local["merge_b"] = _shard_cols(g_small["merge_b"], SH)

    grads = dict(w_in=g_w_in, w_up=g_w_up, w_o=g_w_o, w_ple_gate=g_w_pg, w_ple_proj=g_w_pp)
    deltas, new_m, new_v = {}, {}, {}
    for name in ("w_in", "w_up", "w_o", "w_ple_gate", "w_ple_proj"):
        shape = weights[name].shape
        as3 = (shape[0], -1, shape[-1])
        d_, m_, v_ = _adamw(weights[name].reshape(as3), grads[name].reshape(as3),
                            ms[name].reshape(as3), vs[name].reshape(as3))
        deltas[name], new_m[name], new_v[name] = d_.reshape(shape), m_.reshape(shape), v_.reshape(shape)

    def local_shapes(parts):
        return {n: (parts[n] if parts[n].shape == s else jnp.pad(
            parts[n], [(0, 0)] * (len(s) - 1) + [(0, s[-1] - parts[n].shape[-1])])) for n, s in _SMALL}

    d_, m_, v_ = _adamw(_pack_small(local_shapes(weights)), _pack_small(local_shapes(g_small_local)),
                        _pack_small(local_shapes(ms)), _pack_small(local_shapes(vs)))
    for buf, dst in ((d_, deltas), (m_, new_m), (v_, new_v)):
        parts = _unpack_small(buf)
        for n, _ in _SMALL:
            dst[n] = parts[n][..., :weights[n].shape[-1]]
    for n, _ in _SMALL:
        grads[n] = g_small_local[n]

    order = ["norm_mix", "w_in", "conv_w", "conv_b", "fgate_bias", "q_norm", "k_norm", "lb_logits", "hgrn_norm",
             "sgu_norm", "spatial_w", "spatial_b", "w_up", "merge_b", "w_o", "norm_ple", "w_ple_gate", "w_ple_proj"]
    return (loss, grad_x, *[grads[n] for n in order], *[deltas[n] for n in order],
            *[new_m[n] for n in order], *[new_v[n] for n in order])
```

```python
import functools

import jax
import jax.numpy as jnp
from jax import lax
from jax.experimental import pallas as pl
from jax.experimental.pallas import tpu as pltpu

f32 = jnp.float32
bf16 = jnp.bfloat16

D = 1024
W = 256
NH = 4
DH = 64
NBR = 4
PLE = 256
DEPTH = 4
CONV_WIDTH = 3
SGU_CHUNK = 128
GLA_CHUNK = 128
EPS = 1e-6
MASK_VALUE = -1e30
IN_COLS = 7940
NZ = 8064
OFF_CONV = 4096
OFF_HGRN = 5120
OFF_SGU = 6144
OFF_ATT = 6912
OFF_F = 7936
ZT = 1152
NZT = NZ // ZT
EXP_CLAMP = 80.0

ADAM_LR = 0.001
ADAM_B1 = 0.9
ADAM_B2 = 0.999
ADAM_EPS = 1e-08
ADAM_WD = 0.01
ADAM_STEP = 10

N_DEV = 8
AXES = ("x", "y", "c")
VMEM_LIMIT = 56 * 1024 * 1024
HI = lax.Precision.HIGHEST

NT_DIMS = (((1,), (1,)), ((), ()))
TN_DIMS = (((0,), (0,)), ((), ()))


def _pcall(body, **kw):
    return pl.pallas_call(body, **kw)


def _params(*sem):
    return pltpu.CompilerParams(dimension_semantics=sem, vmem_limit_bytes=VMEM_LIMIT)


def _mm(a, b):
    return jnp.dot(a, b, preferred_element_type=f32)


def _mm_nt(a, b):
    return lax.dot_general(a, b, NT_DIMS, preferred_element_type=f32)


def _mm_tn(a, b):
    return lax.dot_general(a, b, TN_DIMS, preferred_element_type=f32)


def _sigmoid(x):
    return 1.0 / (1.0 + jnp.exp(-x))


def _silu(x):
    return x * _sigmoid(x)


def _dsilu(x):
    s = _sigmoid(x)
    return s * (1.0 + x * (1.0 - s))


def _logsigmoid(x):
    return jnp.minimum(x, 0.0) - jnp.log(1.0 + jnp.exp(-jnp.abs(x)))


def _iota2(shape, axis):
    return lax.broadcasted_iota(jnp.int32, shape, axis)


def _group_mean_matrix(n, group):
    shift = group.bit_length() - 1
    r = lax.shift_right_logical(_iota2((n, n), 0), shift)
    c = lax.shift_right_logical(_iota2((n, n), 1), shift)
    return jnp.where(r == c, 1.0 / group, 0.0).astype(f32)


def _group_mean(x, gm):
    return jnp.dot(x, gm, precision=HI, preferred_element_type=f32)


def _lower_tri(n):
    return jnp.where(_iota2((n, n), 0) >= _iota2((n, n), 1), 1.0, 0.0).astype(f32)


def _upper_tri(n):
    return jnp.where(_iota2((n, n), 0) <= _iota2((n, n), 1), 1.0, 0.0).astype(f32)


def _rows3(r0, r1, r2, width):
    row = _iota2((8, width), 0)
    return jnp.where(row == 0, r0, jnp.where(row == 1, r1, jnp.where(row == 2, r2, 0.0)))


def _inproj_fwd(x, g, w, tm, gather=()):
    T = x.shape[0]
    n = len(gather)
    axes = [0] * n
    steps = (T // tm) * NZT

    def body(x_ref, g_ref, w_ref, *rest):
        z_ref, h_ref = rest[n:n + 2]

        @pl.when(pl.program_id(1) == 0)
        def _():
            xv = x_ref[...]
            r = lax.rsqrt(jnp.mean(xv * xv, axis=-1, keepdims=True) + EPS)
            h_ref[...] = (xv * r * g_ref[...]).astype(bf16)

        if n:
            start, forward, finish = _gather_phases(gather, axes, rest[:n], rest[n + 2:2 * n + 2], *rest[2 * n + 2:])
            step = pl.program_id(0) * NZT + pl.program_id(1)
            pl.when(step == 0)(start)
            pl.when(step == steps // 2)(forward)

        z_ref[...] = _mm(h_ref[...], w_ref[...])

        if n:
            pl.when(step == steps - 1)(finish)

    hbm = pl.BlockSpec(memory_space=pltpu.HBM)
    return _pcall(
        body, name="inproj_fwd_gather" if n else "inproj_fwd", grid=(T // tm, NZT),
        in_specs=[pl.BlockSpec((tm, D), lambda i, j: (i, 0)),
                  pl.BlockSpec((1, D), lambda i, j: (0, 0)),
                  pl.BlockSpec((D, ZT), lambda i, j: (0, j))] + [hbm] * n,
        out_specs=[pl.BlockSpec((tm, ZT), lambda i, j: (i, j)),
                   pl.BlockSpec((tm, D), lambda i, j: (i, 0))] + [hbm] * n,
        out_shape=[jax.ShapeDtypeStruct((T, NZ), f32), jax.ShapeDtypeStruct((T, D), bf16)]
        + _gathered_shapes(gather, axes),
        scratch_shapes=_gather_semaphores(n) if n else [],
        compiler_params=_params("arbitrary" if n else "parallel", "arbitrary"),
    )(x, g, w, *gather)


def _inproj_bwd_x(dz, w, x, dx1, g, tm):
    T = x.shape[0]

    def body(dz_ref, w_ref, x_ref, dx1_ref, g_ref, dx_ref, gg_ref, acc):
        i, k = pl.program_id(0), pl.program_id(1)

        @pl.when(k == 0)
        def _():
            acc[...] = jnp.zeros_like(acc)

        @pl.when((i == 0) & (k == 0))
        def _():
            gg_ref[...] = jnp.zeros_like(gg_ref)

        acc[...] += _mm_nt(dz_ref[...], w_ref[...])

        @pl.when(k == NZT - 1)
        def _():
            xv = x_ref[...]
            r = lax.rsqrt(jnp.mean(xv * xv, axis=-1, keepdims=True) + EPS)
            dh = acc[...]
            gg_ref[...] += jnp.sum(dh * xv * r, axis=0, keepdims=True)
            u = dh * g_ref[...]
            dx_ref[...] = dx1_ref[...] + r * u - xv * (r * r * r) * jnp.mean(u * xv, axis=-1, keepdims=True)

    return _pcall(
        body, name="inproj_bwd_x", grid=(T // tm, NZT),
        in_specs=[pl.BlockSpec((tm, ZT), lambda i, k: (i, k)),
                  pl.BlockSpec((D, ZT), lambda i, k: (0, k)),
                  pl.BlockSpec((tm, D), lambda i, k: (i, 0)),
                  pl.BlockSpec((tm, D), lambda i, k: (i, 0)),
                  pl.BlockSpec((1, D), lambda i, k: (0, 0))],
        out_specs=[pl.BlockSpec((tm, D), lambda i, k: (i, 0)),
                   pl.BlockSpec((1, D), lambda i, k: (0, 0))],
        out_shape=[jax.ShapeDtypeStruct((T, D), f32), jax.ShapeDtypeStruct((1, D), f32)],
        scratch_shapes=[pltpu.VMEM((tm, D), f32)],
        compiler_params=_params("arbitrary", "arbitrary"),
    )(dz, w, x, dx1, g)


def _inproj_bwd_w(h, dz, tm, li, buf):
    T = h.shape[0]
    SH = D // N_DEV
    nt = T // tm
    extra = [] if buf is None else [buf]

    def body(h_ref, dz_ref, *rest):
        gw_ref, acc = rest[len(extra):]

        @pl.when(pl.program_id(1) == 0)
        def _():
            acc[...] = jnp.zeros_like(acc)

        acc[...] += _mm_tn(h_ref[...], dz_ref[...])

        @pl.when(pl.program_id(1) == nt - 1)
        def _():
            gw_ref[...] = acc[...].reshape(N_DEV, SH, ZT).astype(bf16)

    return _pcall(
        body, name="inproj_bwd_w", grid=(NZT, nt),
        in_specs=[pl.BlockSpec((tm, D), lambda j, i: (i, 0)),
                  pl.BlockSpec((tm, ZT), lambda j, i: (i, j))] + [pl.BlockSpec(memory_space=pl.ANY)] * len(extra),
        out_specs=pl.BlockSpec((N_DEV, SH, ZT), lambda j, i: (0, li, j)),
        out_shape=jax.ShapeDtypeStruct((N_DEV, DEPTH * SH, NZ), bf16),
        scratch_shapes=[pltpu.VMEM((D, ZT), f32)],
        input_output_aliases={2: 0} if extra else {},
        compiler_params=_params("parallel", "arbitrary"),
    )(h, dz, *extra)


def _zblock(tm, col256):
    return pl.BlockSpec((tm, W), lambda i, c=col256: (i, c))


def _conv_taps(zc, halo, cw_ref, n):
    ext = jnp.concatenate([halo, zc], axis=0)
    z1 = pltpu.roll(ext, 1, 0)[8:]
    z2 = pltpu.roll(ext, 2, 0)[8:]
    return z1, z2


def _conv_fwd(z, cw, cb, tm):
    T = z.shape[0]
    c0 = OFF_CONV // W
    hb = tm // 8

    def body(ax_ref, ab_ref, ac_ref, ag_ref, hx_ref, hc_ref, cw_ref, cb_ref, y_ref):
        i = pl.program_id(0)
        zc = ac_ref[...] * ax_ref[...]
        halo = jnp.where(i > 0, hc_ref[...] * hx_ref[...], 0.0)
        z1, z2 = _conv_taps(zc, halo, cw_ref, tm)
        y = cw_ref[2:3, :] * zc + cw_ref[1:2, :] * z1 + cw_ref[0:1, :] * z2
        ya = ab_ref[...] * (y + cb_ref[...])
        y_ref[...] = (ya * _silu(ag_ref[...])).astype(bf16)

    halo_spec = lambda col: pl.BlockSpec((8, W), lambda i, c=col: (jnp.maximum(i * hb - 1, 0), c))
    return _pcall(
        body, name="conv_fwd", grid=(T // tm,),
        in_specs=[_zblock(tm, c0), _zblock(tm, c0 + 1), _zblock(tm, c0 + 2), _zblock(tm, c0 + 3),
                  halo_spec(c0), halo_spec(c0 + 2),
                  pl.BlockSpec((CONV_WIDTH, W), lambda i: (0, 0)),
                  pl.BlockSpec((1, W), lambda i: (0, 0))],
        out_specs=pl.BlockSpec((tm, W), lambda i: (i, 0)),
        out_shape=jax.ShapeDtypeStruct((T, W), bf16),
        compiler_params=_params("parallel"),
    )(z, z, z, z, z, z, cw, cb)


def _conv_bwd(z, dy, cw, cb, dzbuf, tm):
    T = z.shape[0]
    c0 = OFF_CONV // W
    hb = tm // 8
    nt = T // tm

    def body(ax_ref, ab_ref, ac_ref, ag_ref, hx_ref, hc_ref, nb_ref, ng_ref, dy_ref, ndy_ref,
             cw_ref, cb_ref, dzin_ref, dz_ref, gcw_ref, gcb_ref):
        i = pl.program_id(0)

        @pl.when(i == 0)
        def _():
            gcw_ref[...] = jnp.zeros_like(gcw_ref)
            gcb_ref[...] = jnp.zeros_like(gcb_ref)

        ax, ab, ac, ag = ax_ref[...], ab_ref[...], ac_ref[...], ag_ref[...]
        w0, w1, w2 = cw_ref[0:1, :], cw_ref[1:2, :], cw_ref[2:3, :]
        zc = ac * ax
        halo = jnp.where(i > 0, hc_ref[...] * hx_ref[...], 0.0)
        z1, z2 = _conv_taps(zc, halo, cw_ref, tm)
        yb = w2 * zc + w1 * z1 + w0 * z2 + cb_ref[...]
        ya = ab * yb
        dyg = dy_ref[...]
        dag = dyg * ya * _dsilu(ag)
        dya = dyg * _silu(ag)
        dab = dya * yb
        dyc = dya * ab
        nxt = jnp.where(i < nt - 1, ndy_ref[...] * _silu(ng_ref[...]) * nb_ref[...], 0.0)
        ext = jnp.concatenate([dyc, nxt], axis=0)
        d1 = pltpu.roll(ext, tm + 8 - 1, 0)[:tm]
        d2 = pltpu.roll(ext, tm + 8 - 2, 0)[:tm]
        dzc = w2 * dyc + w1 * d1 + w0 * d2
        dz_ref[:, 0:W] = (dzc * ac).astype(bf16)
        dz_ref[:, W:2 * W] = dab.astype(bf16)
        dz_ref[:, 2 * W:3 * W] = (dzc * ax).astype(bf16)
        dz_ref[:, 3 * W:4 * W] = dag.astype(bf16)
        gcb_ref[...] += jnp.sum(dyc, axis=0, keepdims=True)
        gcw_ref[...] += _rows3(jnp.sum(dyc * z2, axis=0, keepdims=True),
                               jnp.sum(dyc * z1, axis=0, keepdims=True),
                               jnp.sum(dyc * zc, axis=0, keepdims=True), W)

    prev_spec = lambda col: pl.BlockSpec((8, W), lambda i, c=col: (jnp.maximum(i * hb - 1, 0), c))
    next_z = lambda col: pl.BlockSpec((8, W), lambda i, c=col: (jnp.minimum((i + 1) * hb, T // 8 - 1), c))
    next_dy = pl.BlockSpec((8, W), lambda i: (jnp.minimum((i + 1) * hb, T // 8 - 1), 0))
    return _pcall(
        body, name="conv_bwd", grid=(nt,),
        in_specs=[_zblock(tm, c0), _zblock(tm, c0 + 1), _zblock(tm, c0 + 2), _zblock(tm, c0 + 3),
                  prev_spec(c0), prev_spec(c0 + 2), next_z(c0 + 1), next_z(c0 + 3),
                  pl.BlockSpec((tm, W), lambda i: (i, 0)), next_dy,
                  pl.BlockSpec((CONV_WIDTH, W), lambda i: (0, 0)),
                  pl.BlockSpec((1, W), lambda i: (0, 0)),
                  pl.BlockSpec(memory_space=pl.ANY)],
        out_specs=[pl.BlockSpec((tm, 4 * W), lambda i: (i, OFF_CONV // (4 * W))),
                   pl.BlockSpec((8, W), lambda i: (0, 0)),
                   pl.BlockSpec((1, W), lambda i: (0, 0))],
        out_shape=[jax.ShapeDtypeStruct((T, NZ), bf16), jax.ShapeDtypeStruct((8, W), f32),
                   jax.ShapeDtypeStruct((1, W), f32)],
        input_output_aliases={12: 0},
        compiler_params=_params("arbitrary"),
    )(z, z, z, z, z, z, z, z, dy, dy, cw, cb, dzbuf)


def _sgu_core(dv_ref, gv_ref, sw_ref, sbe_ref, s_scr, tm):
    v = dv_ref[...]
    gm = _group_mean_matrix(W, DH)
    rv = lax.rsqrt(_group_mean(v * v, gm) + EPS)
    vh = v * rv
    vnb = (vh * gv_ref[...]).astype(bf16)
    causal = _iota2((SGU_CHUNK, SGU_CHUNK), 0) >= _iota2((SGU_CHUNK, SGU_CHUNK), 1)
    wgs = [jnp.where(causal, sw_ref[g], 0.0).astype(bf16) for g in range(NH)]
    for c in range(tm // SGU_CHUNK):
        rows = slice(c * SGU_CHUNK, (c + 1) * SGU_CHUNK)
        for g in range(NH):
            cols = slice(g * DH, (g + 1) * DH)
            s_scr[rows, cols] = _mm(wgs[g], vnb[rows, cols])
    sb = sbe_ref[...]
    s = s_scr[...] + jnp.concatenate([sb] * (tm // SGU_CHUNK), axis=0)
    return v, rv, vh, vnb, wgs, causal, gm, s


def _sgu_fwd(z, gv, sw, sbe, tm):
    T = z.shape[0]
    c0 = OFF_SGU // W

    def body(du_ref, dv_ref, dg_ref, gv_ref, sw_ref, sbe_ref, y_ref, s_scr):
        s = _sgu_core(dv_ref, gv_ref, sw_ref, sbe_ref, s_scr, tm)[-1]
        y_ref[...] = ((du_ref[...] * s) * _silu(dg_ref[...])).astype(bf16)

    return _pcall(
        body, name="sgu_fwd", grid=(T // tm,),
        in_specs=[_zblock(tm, c0), _zblock(tm, c0 + 1), _zblock(tm, c0 + 2),
                  pl.BlockSpec((1, W), lambda i: (0, 0)),
                  pl.BlockSpec((NH, SGU_CHUNK, SGU_CHUNK), lambda i: (0, 0, 0)),
                  pl.BlockSpec((SGU_CHUNK, W), lambda i: (0, 0))],
        out_specs=pl.BlockSpec((tm, W), lambda i: (i, 0)),
        out_shape=jax.ShapeDtypeStruct((T, W), bf16),
        scratch_shapes=[pltpu.VMEM((tm, W), f32)],
        compiler_params=_params("parallel"),
    )(z, z, z, gv, sw, sbe)


def _sgu_bwd(z, dy, gv, sw, sbe, dzbuf, tm):
    T = z.shape[0]
    c0 = OFF_SGU // W
    nt = T // tm

    def body(du_ref, dv_ref, dg_ref, dy_ref, gv_ref, sw_ref, sbe_ref, dzin_ref,
             dz_ref, gsw_ref, gsb_ref, ggv_ref, s_scr, dvn_scr, sb_acc):
        i = pl.program_id(0)

        @pl.when(i == 0)
        def _():
            gsw_ref[...] = jnp.zeros_like(gsw_ref)
            ggv_ref[...] = jnp.zeros_like(ggv_ref)
            sb_acc[...] = jnp.zeros_like(sb_acc)

        v, rv, vh, vnb, wgs, causal, gm, s = _sgu_core(dv_ref, gv_ref, sw_ref, sbe_ref, s_scr, tm)
        du, dg, dyv = du_ref[...], dg_ref[...], dy_ref[...]
        ddg = dyv * (du * s) * _dsilu(dg)
        t = dyv * _silu(dg)
        ddu = t * s
        ds = t * du
        dsb = ds.astype(bf16)
        acc = sb_acc[...]
        for c in range(tm // SGU_CHUNK):
            rows = slice(c * SGU_CHUNK, (c + 1) * SGU_CHUNK)
            acc = acc + ds[rows, :]
            for g in range(NH):
                cols = slice(g * DH, (g + 1) * DH)
                gsw_ref[g] += jnp.where(causal, _mm_nt(dsb[rows, cols], vnb[rows, cols]), 0.0)
                dvn_scr[rows, cols] = _mm_tn(wgs[g], dsb[rows, cols])
        sb_acc[...] = acc
        dvn = dvn_scr[...]
        ggv_ref[...] += jnp.sum(dvn * vh, axis=0, keepdims=True)
        u = dvn * gv_ref[...]
        ddv = rv * u - v * (rv * rv * rv) * _group_mean(u * v, gm)
        dz_ref[:, 0:W] = ddu.astype(bf16)
        dz_ref[:, W:2 * W] = ddv.astype(bf16)
        dz_ref[:, 2 * W:3 * W] = ddg.astype(bf16)

        @pl.when(i == nt - 1)
        def _():
            gsb_ref[...] = _group_mean(sb_acc[...], gm) * float(DH)

    return _pcall(
        body, name="sgu_bwd", grid=(nt,),
        in_specs=[_zblock(tm, c0), _zblock(tm, c0 + 1), _zblock(tm, c0 + 2),
                  pl.BlockSpec((tm, W), lambda i: (i, 0)),
                  pl.BlockSpec((1, W), lambda i: (0, 0)),
                  pl.BlockSpec((NH, SGU_CHUNK, SGU_CHUNK), lambda i: (0, 0, 0)),
                  pl.BlockSpec((SGU_CHUNK, W), lambda i: (0, 0)),
                  pl.BlockSpec(memory_space=pl.ANY)],
        out_specs=[pl.BlockSpec((tm, 3 * W), lambda i: (i, OFF_SGU // (3 * W))),
                   pl.BlockSpec((NH, SGU_CHUNK, SGU_CHUNK), lambda i: (0, 0, 0)),
                   pl.BlockSpec((SGU_CHUNK, W), lambda i: (0, 0)),
                   pl.BlockSpec((1, W), lambda i: (0, 0))],
        out_shape=[jax.ShapeDtypeStruct((T, NZ), bf16),
                   jax.ShapeDtypeStruct((NH, SGU_CHUNK, SGU_CHUNK), f32),
                   jax.ShapeDtypeStruct((SGU_CHUNK, W), f32),
                   jax.ShapeDtypeStruct((1, W), f32)],
        scratch_shapes=[pltpu.VMEM((tm, W), f32), pltpu.VMEM((tm, W), f32), pltpu.VMEM((SGU_CHUNK, W), f32)],
        input_output_aliases={7: 0},
        compiler_params=_params("arbitrary"),
    )(z, z, z, dy, gv, sw, sbe, dzbuf)


def _hgrn_gates(cq_ref, cf_ref, lb_ref):
    q = _silu(cq_ref[...])
    sig = _sigmoid(cf_ref[...])
    lb = lb_ref[...]
    g = lb + (1.0 - lb) * sig
    return q, sig, g, jnp.log(g), (1.0 - lb) * (1.0 - sig)


def _hgrn_chunk_terms(lgc, qc, kc):
    C = GLA_CHUNK
    b = jnp.dot(_lower_tri(C), lgc, precision=HI, preferred_element_type=f32)
    bl = jnp.sum(lgc, axis=0, keepdims=True)
    mid = jnp.sum(jnp.where(_iota2((C, W), 0) <= C // 2, lgc, 0.0), axis=0, keepdims=True)
    eb = jnp.exp(b)
    em = jnp.exp(jnp.minimum(b - mid, EXP_CLAMP))
    emi = jnp.exp(jnp.minimum(mid - b, EXP_CLAMP))
    ek = jnp.exp(bl - b)
    return dict(eb=eb, em=em, emi=emi, ek=ek, ebl=jnp.exp(bl),
                qe=qc * eb, qm=qc * em, km=kc * emi, kd=kc * ek)


def _hgrn_fwd(z, lb, gain, tm):
    T = z.shape[0]
    c0 = OFF_HGRN // W
    C = GLA_CHUNK
    ncp = tm // C

    def body(cq_ref, cf_ref, ci_ref, cg_ref, lb_ref, gn_ref, y_ref, o_ref, st_ref, state, o_scr):
        @pl.when(pl.program_id(0) == 0)
        def _():
            state[...] = jnp.zeros_like(state)

        q, sig, g, lg, kf = _hgrn_gates(cq_ref, cf_ref, lb_ref)
        v = ci_ref[...]
        causal = _iota2((C, C), 0) >= _iota2((C, C), 1)
        for c in range(ncp):
            rows = slice(c * C, (c + 1) * C)
            tr = _hgrn_chunk_terms(lg[rows], q[rows], kf[rows])
            vb = v[rows].astype(bf16)
            qmb, kmb, qeb, kdb = (tr[n].astype(bf16) for n in ("qm", "km", "qe", "kd"))
            for h in range(NH):
                cols = slice(h * DH, (h + 1) * DH)
                hr = slice(h * DH, (h + 1) * DH)
                st = state[hr, :]
                st_ref[c, hr, :] = st
                p = jnp.where(causal, _mm_nt(qmb[:, cols], kmb[:, cols]), 0.0)
                o_scr[rows, cols] = _mm(p.astype(bf16), vb[:, cols]) + _mm_nt(qeb[:, cols], st.astype(bf16))
                state[hr, :] = st * tr["ebl"][:, cols] + _mm_tn(vb[:, cols], kdb[:, cols])
        o = o_scr[...]
        o_ref[...] = o
        gm = _group_mean_matrix(W, DH)
        r = lax.rsqrt(_group_mean(o * o, gm) + EPS)
        y_ref[...] = ((o * r * gn_ref[...]) * _silu(cg_ref[...])).astype(bf16)

    return _pcall(
        body, name="hgrn_fwd", grid=(T // tm,),
        in_specs=[_zblock(tm, c0), _zblock(tm, c0 + 1), _zblock(tm, c0 + 2), _zblock(tm, c0 + 3),
                  pl.BlockSpec((1, W), lambda i: (0, 0)), pl.BlockSpec((1, W), lambda i: (0, 0))],
        out_specs=[pl.BlockSpec((tm, W), lambda i: (i, 0)),
                   pl.BlockSpec((tm, W), lambda i: (i, 0)),
                   pl.BlockSpec((ncp, W, DH), lambda i: (i, 0, 0))],
        out_shape=[jax.ShapeDtypeStruct((T, W), bf16), jax.ShapeDtypeStruct((T, W), f32),
                   jax.ShapeDtypeStruct((T // C, W, DH), f32)],
        scratch_shapes=[pltpu.VMEM((W, DH), f32), pltpu.VMEM((tm, W), f32)],
        compiler_params=_params("arbitrary"),
    )(z, z, z, z, lb, gain)


def _hgrn_bwd(z, lb, gain, o_pre, states, dy, dzbuf, tm):
    T = z.shape[0]
    c0 = OFF_HGRN // W
    C = GLA_CHUNK
    ncp = tm // C
    nt = T // tm

    def body(cq_ref, cf_ref, ci_ref, cg_ref, lb_ref, gn_ref, o_ref, st_ref, dy_ref, dzin_ref,
             dz_ref, ggn_ref, glb_ref, dstate, dq_s, dk_s, dv_s, db_s):
        @pl.when(pl.program_id(0) == 0)
        def _():
            dstate[...] = jnp.zeros_like(dstate)
            ggn_ref[...] = jnp.zeros_like(ggn_ref)
            glb_ref[...] = jnp.zeros_like(glb_ref)

        cq, cg = cq_ref[...], cg_ref[...]
        q, sig, g, lg, kf = _hgrn_gates(cq_ref, cf_ref, lb_ref)
        lb = lb_ref[...]
        v = ci_ref[...]
        o = o_ref[...]
        gm = _group_mean_matrix(W, DH)
        r = lax.rsqrt(_group_mean(o * o, gm) + EPS)
        oh = o * r
        gn = gn_ref[...]
        dyv = dy_ref[...]
        dcg = dyv * (oh * gn) * _dsilu(cg)
        don = dyv * _silu(cg)
        ggn_ref[...] += jnp.sum(don * oh, axis=0, keepdims=True)
        u = don * gn
        do = r * u - o * (r * r * r) * _group_mean(u * o, gm)

        causal = _iota2((C, C), 0) >= _iota2((C, C), 1)
        last_row = _iota2((C, DH), 0) == C - 1
        for c in reversed(range(ncp)):
            rows = slice(c * C, (c + 1) * C)
            tr = _hgrn_chunk_terms(lg[rows], q[rows], kf[rows])
            vb = v[rows].astype(bf16)
            dob = do[rows].astype(bf16)
            qmb, kmb, qeb, kdb = (tr[n].astype(bf16) for n in ("qm", "km", "qe", "kd"))
            for h in range(NH):
                cols = slice(h * DH, (h + 1) * DH)
                hr = slice(h * DH, (h + 1) * DH)
                st0 = st_ref[c, hr, :]
                dst = dstate[hr, :]
                dstb = dst.astype(bf16)
                doh = dob[:, cols]
                p = jnp.where(causal, _mm_nt(qmb[:, cols], kmb[:, cols]), 0.0)
                dp = jnp.where(causal, _mm_nt(doh, vb[:, cols]), 0.0)
                dpb = dp.astype(bf16)
                dvh = _mm_tn(p.astype(bf16), doh) + _mm_nt(kdb[:, cols], dstb)
                dqm = _mm(dpb, kmb[:, cols])
                dkm = _mm_tn(dpb, qmb[:, cols])
                dqe = _mm(doh, st0.astype(bf16))
                dkd = _mm(vb[:, cols], dstb)
                ebl = tr["ebl"][:, cols]
                dstate[hr, :] = dst * ebl + _mm_tn(doh, qeb[:, cols])
                qm, km, qe, kd = (a[:, cols].astype(f32) for a in (qmb, kmb, qeb, kdb))
                kterm = dkd * kd
                dbh = dqm * qm - dkm * km + dqe * qe - kterm
                extra = jnp.sum(kterm, axis=0, keepdims=True) + ebl * jnp.sum(dst * st0, axis=0, keepdims=True)
                dbh = dbh + jnp.where(last_row, extra, 0.0)
                dq_s[rows, cols] = dqm * tr["em"][:, cols] + dqe * tr["eb"][:, cols]
                dk_s[rows, cols] = dkm * tr["emi"][:, cols] + dkd * tr["ek"][:, cols]
                dv_s[rows, cols] = dvh
                db_s[rows, cols] = dbh
            db_s[rows, :] = jnp.dot(_upper_tri(C), db_s[rows, :], precision=HI, preferred_element_type=f32)
        dlg = db_s[...]
        dk = dk_s[...]
        dsig = sig * (1.0 - sig)
        one_lb = 1.0 - lb
        dcf = (dlg / g - dk) * one_lb * dsig
        glb_ref[...] += jnp.sum((dlg / g - dk) * (1.0 - sig), axis=0, keepdims=True)
        dz_ref[:, 0:W] = (dq_s[...] * _dsilu(cq)).astype(bf16)
        dz_ref[:, W:2 * W] = dcf.astype(bf16)
        dz_ref[:, 2 * W:3 * W] = dv_s[...].astype(bf16)
        dz_ref[:, 3 * W:4 * W] = dcg.astype(bf16)

    rev = lambda i: nt - 1 - i
    zb = lambda col: pl.BlockSpec((tm, W), lambda i, c=col: (rev(i), c))
    return _pcall(
        body, name="hgrn_bwd", grid=(nt,),
        in_specs=[zb(c0), zb(c0 + 1), zb(c0 + 2), zb(c0 + 3),
                  pl.BlockSpec((1, W), lambda i: (0, 0)), pl.BlockSpec((1, W), lambda i: (0, 0)),
                  pl.BlockSpec((tm, W), lambda i: (rev(i), 0)),
                  pl.BlockSpec((ncp, W, DH), lambda i: (rev(i), 0, 0)),
                  pl.BlockSpec((tm, W), lambda i: (rev(i), 0)),
                  pl.BlockSpec(memory_space=pl.ANY)],
        out_specs=[pl.BlockSpec((tm, 4 * W), lambda i: (rev(i), OFF_HGRN // (4 * W))),
                   pl.BlockSpec((1, W), lambda i: (0, 0)),
                   pl.BlockSpec((1, W), lambda i: (0, 0))],
        out_shape=[jax.ShapeDtypeStruct((T, NZ), bf16), jax.ShapeDtypeStruct((1, W), f32),
                   jax.ShapeDtypeStruct((1, W), f32)],
        scratch_shapes=[pltpu.VMEM((W, DH), f32)] + [pltpu.VMEM((tm, W), f32)] * 4,
        input_output_aliases={9: 0},
        compiler_params=_params("arbitrary"),
    )(z, z, z, z, lb, gain, o_pre, states, dy, dzbuf)


def _attn_prep(z, fbias, gq, gk, tm):
    T = z.shape[0]
    c0 = OFF_ATT // W

    def body(q_ref, k_ref, v_ref, f_ref, fb_ref, gq_ref, gk_ref, qt_ref, kt_ref, vt_ref, kh_ref, vh_ref, cum_ref,
             carry):
        @pl.when(pl.program_id(0) == 0)
        def _():
            carry[...] = jnp.zeros_like(carry)

        gm = _group_mean_matrix(W, DH)
        q, k, v = q_ref[...], k_ref[...], v_ref[...]
        qs = q * lax.rsqrt(_group_mean(q * q, gm) + EPS) * (gq_ref[...] * (DH ** -0.5))
        kn = k * lax.rsqrt(_group_mean(k * k, gm) + EPS) * gk_ref[...]
        qt_ref[...] = qs.T.astype(bf16)
        kt_ref[...] = kn.T.astype(bf16)
        vt_ref[...] = v.T.astype(bf16)
        for h in range(NH):
            cols = slice(h * DH, (h + 1) * DH)
            kh_ref[h] = kn[:, cols].astype(bf16)
            vh_ref[h] = v[:, cols].astype(bf16)
        ls = _logsigmoid(f_ref[...] + fb_ref[...])
        cum_ref[...] = jnp.dot(_lower_tri(tm), ls, precision=HI, preferred_element_type=f32) + carry[...]
        carry[...] += jnp.sum(ls, axis=0, keepdims=True)

    hspec = pl.BlockSpec((NH, tm, DH), lambda i: (0, i, 0))
    tspec = pl.BlockSpec((W, tm), lambda i: (0, i))
    return _pcall(
        body, name="attn_prep", grid=(T // tm,),
        in_specs=[_zblock(tm, c0), _zblock(tm, c0 + 1), _zblock(tm, c0 + 2),
                  pl.BlockSpec((tm, 128), lambda i: (i, OFF_F // 128)),
                  pl.BlockSpec((1, 128), lambda i: (0, 0)),
                  pl.BlockSpec((1, W), lambda i: (0, 0)), pl.BlockSpec((1, W), lambda i: (0, 0))],
        out_specs=[tspec, tspec, tspec, hspec, hspec, pl.BlockSpec((tm, 128), lambda i: (i, 0))],
        out_shape=[jax.ShapeDtypeStruct((W, T), bf16)] * 3 + [jax.ShapeDtypeStruct((NH, T, DH), bf16)] * 2
        + [jax.ShapeDtypeStruct((T, 128), f32)],
        scratch_shapes=[pltpu.VMEM((1, 128), f32)],
        compiler_params=_params("arbitrary"),
    )(z, z, z, z, fbias, gq, gk)


HP = 2


def _causal_pairs(nq, key_major):
    if key_major:
        pairs = [(qi, ki) for ki in range(nq) for qi in range(ki, nq)]
    else:
        pairs = [(qi, ki) for qi in range(nq) for ki in range(qi + 1)]
    return (jnp.asarray([p[0] for p in pairs], jnp.int32), jnp.asarray([p[1] for p in pairs], jnp.int32))


def _head_rows(rows, n):
    return jnp.concatenate([jnp.broadcast_to(r, (DH, n)) for r in rows], axis=0)


def _attn_fwd(qt, kh, vt, crow, ccol, bq):
    T = qt.shape[1]
    nq = T // bq
    bk = bq
    qs, ks = _causal_pairs(nq, key_major=False)
    BW = HP * DH

    def body(qs_ref, ks_ref, qt_ref, k_ref, vt_ref, cr_ref, cc_ref, o_ref, lse_ref, m_s, l_s, acc_s):
        i = pl.program_id(1)
        qi, ki = qs_ref[i], ks_ref[i]

        @pl.when(ki == 0)
        def _():
            m_s[...] = jnp.full_like(m_s, MASK_VALUE)
            l_s[...] = jnp.zeros_like(l_s)
            acc_s[...] = jnp.zeros_like(acc_s)

        def step(diagonal):
            for h in range(HP):
                rows = slice(h * DH, (h + 1) * DH)
                s = _mm(k_ref[h], qt_ref[rows, :]) + cr_ref[h] - cc_ref[h]
                if diagonal:
                    s = jnp.where(_iota2((bk, bq), 0) <= _iota2((bk, bq), 1), s, MASK_VALUE)
                m_old = m_s[h]
                m_new = jnp.maximum(m_old, jnp.max(s, axis=0, keepdims=True))
                p = jnp.exp(s - m_new)
                alpha = jnp.exp(m_old - m_new)
                l_s[h] = alpha * l_s[h] + jnp.sum(p, axis=0, keepdims=True)
                acc_s[rows, :] = alpha * acc_s[rows, :] + _mm(vt_ref[rows, :], p.astype(bf16))
                m_s[h] = m_new

        @pl.when(ki < qi)
        def _():
            step(False)

        @pl.when(ki == qi)
        def _():
            step(True)
            o_ref[...] = (acc_s[...] / _head_rows([l_s[h] for h in range(HP)], bq)).T
            for h in range(HP):
                lse_ref[h] = m_s[h] + jnp.log(l_s[h])

    qcol = lambda hp, i, qs, ks: (hp, qs[i])
    kcol = lambda hp, i, qs, ks: (hp, ks[i])
    qrow = lambda hp, i, qs, ks: (hp, 0, qs[i])
    return _pcall(
        body, name="attn_fwd",
        grid_spec=pltpu.PrefetchScalarGridSpec(
            num_scalar_prefetch=2, grid=(NH // HP, qs.shape[0]),
            in_specs=[pl.BlockSpec((BW, bq), qcol),
                      pl.BlockSpec((HP, bk, DH), lambda hp, i, qs, ks: (hp, ks[i], 0)),
                      pl.BlockSpec((BW, bk), kcol),
                      pl.BlockSpec((HP, 1, bq), qrow),
                      pl.BlockSpec((HP, bk, 1), lambda hp, i, qs, ks: (hp, ks[i], 0))],
            out_specs=[pl.BlockSpec((bq, BW), lambda hp, i, qs, ks: (qs[i], hp)),
                       pl.BlockSpec((HP, 1, bq), qrow)],
            scratch_shapes=[pltpu.VMEM((HP, 1, bq), f32), pltpu.VMEM((HP, 1, bq), f32),
                            pltpu.VMEM((BW, bq), f32)]),
        out_shape=[jax.ShapeDtypeStruct((T, W), f32), jax.ShapeDtypeStruct((NH, 1, T), f32)],
        compiler_params=_params("parallel", "arbitrary"),
    )(qs, ks, qt, kh, vt, crow, ccol)


def _attn_bwd_prep(dy, oh, z, tm):
    T = dy.shape[0]
    cg = OFF_ATT // W + 3

    def body(dy_ref, o_ref, g_ref, dot_ref, dl_ref):
        do = (dy_ref[...] * _silu(g_ref[...])).astype(bf16)
        dot_ref[...] = do.astype(f32).T.astype(bf16)
        prod = (do.astype(f32) * o_ref[...]).T
        for h in range(NH):
            dl_ref[h] = jnp.sum(prod[h * DH:(h + 1) * DH, :], axis=0, keepdims=True)

    return _pcall(
        body, name="attn_bwd_prep", grid=(T // tm,),
        in_specs=[pl.BlockSpec((tm, W), lambda i: (i, 0)),
                  pl.BlockSpec((tm, W), lambda i: (i, 0)),
                  _zblock(tm, cg)],
        out_specs=[pl.BlockSpec((W, tm), lambda i: (0, i)),
                   pl.BlockSpec((NH, 1, tm), lambda i: (0, 0, i))],
        out_shape=[jax.ShapeDtypeStruct((W, T), bf16), jax.ShapeDtypeStruct((NH, 1, T), f32)],
        compiler_params=_params("parallel"),
    )(dy, oh, z)


def _attn_bwd(qt, kt, kh, vh, crow, ccol, dot, lse, delta, bq):
    T = qt.shape[1]
    nq = T // bq
    bk = bq
    qs, ks = _causal_pairs(nq, key_major=True)
    BW = HP * DH

    def body(qs_ref, ks_ref, qt_ref, kt_ref, k_ref, v_ref, cr_ref, cc_ref, dot_ref, lse_ref, dl_ref,
             dq_ref, dk_ref, dv_ref, dck_ref, dcq_ref, dq_s, dk_s, dv_s, dck_s):
        i = pl.program_id(1)
        qi, ki = qs_ref[i], ks_ref[i]

        @pl.when(i == 0)
        def _():
            dq_s[...] = jnp.zeros_like(dq_s)
            dcq_ref[...] = jnp.zeros_like(dcq_ref)

        @pl.when(qi == ki)
        def _():
            dk_s[...] = jnp.zeros_like(dk_s)
            dv_s[...] = jnp.zeros_like(dv_s)
            dck_s[...] = jnp.zeros_like(dck_s)

        def step(diagonal):
            colsums = []
            for h in range(HP):
                rows = slice(h * DH, (h + 1) * DH)
                qth, doth = qt_ref[rows, :], dot_ref[rows, :]
                p = jnp.exp(_mm(k_ref[h], qth) + (cr_ref[h] - lse_ref[h]) - cc_ref[h])
                if diagonal:
                    p = jnp.where(_iota2((bk, bq), 0) <= _iota2((bk, bq), 1), p, 0.0)
                dv_s[rows, :] += _mm_nt(doth, p.astype(bf16))
                ds = p * (_mm(v_ref[h], doth) - dl_ref[h])
                dsb = ds.astype(bf16)
                dk_s[rows, :] += _mm_nt(qth, dsb)
                dq_s[qi, rows, :] += _mm(kt_ref[rows, :], dsb)
                part = ds[:, 0:128]
                for c in range(1, bq // 128):
                    part = part + ds[:, c * 128:(c + 1) * 128]
                dck_s[h] += part
                colsums.append(jnp.sum(ds, axis=0, keepdims=True))
            dcq_ref[qi] += _stack_rows(colsums, bq)

        @pl.when(qi > ki)
        def _():
            step(False)

        @pl.when(qi == ki)
        def _():
            step(True)

        @pl.when(qi == nq - 1)
        def _():
            dk_ref[...] = dk_s[...].T
            dv_ref[...] = dv_s[...].T
            lane = _iota2((bk, 128), 1)
            out = jnp.zeros((bk, 128), f32)
            for h in range(HP):
                out = out - jnp.where(lane == pl.program_id(0) * HP + h,
                                      jnp.sum(dck_s[h], axis=1, keepdims=True), 0.0)
            dck_ref[...] = out

        @pl.when(i == qs.shape[0] - 1)
        def _():
            for qb in range(nq):
                dq_ref[qb * bq:(qb + 1) * bq, :] = dq_s[qb].T

    qcol = lambda hp, i, qs, ks: (hp, qs[i])
    kcol = lambda hp, i, qs, ks: (hp, ks[i])
    qrow = lambda hp, i, qs, ks: (hp, 0, qs[i])
    kh_spec = pl.BlockSpec((HP, bk, DH), lambda hp, i, qs, ks: (hp, ks[i], 0))
    return _pcall(
        body, name="attn_bwd",
        grid_spec=pltpu.PrefetchScalarGridSpec(
            num_scalar_prefetch=2, grid=(NH // HP, qs.shape[0]),
            in_specs=[pl.BlockSpec((BW, bq), qcol), pl.BlockSpec((BW, bk), kcol), kh_spec, kh_spec,
                      pl.BlockSpec((HP, 1, bq), qrow),
                      pl.BlockSpec((HP, bk, 1), lambda hp, i, qs, ks: (hp, ks[i], 0)),
                      pl.BlockSpec((BW, bq), qcol), pl.BlockSpec((HP, 1, bq), qrow), pl.BlockSpec((HP, 1, bq), qrow)],
            out_specs=[pl.BlockSpec((T, BW), lambda hp, i, qs, ks: (0, hp)),
                       pl.BlockSpec((bk, BW), lambda hp, i, qs, ks: (ks[i], hp)),
                       pl.BlockSpec((bk, BW), lambda hp, i, qs, ks: (ks[i], hp)),
                       pl.BlockSpec((None, bk, 128), lambda hp, i, qs, ks: (hp, ks[i], 0)),
                       pl.BlockSpec((None, nq, 8, bq), lambda hp, i, qs, ks: (hp, 0, 0, 0))],
            scratch_shapes=[pltpu.VMEM((nq, BW, bq), f32), pltpu.VMEM((BW, bk), f32), pltpu.VMEM((BW, bk), f32),
                            pltpu.VMEM((HP, bk, 128), f32)]),
        out_shape=[jax.ShapeDtypeStruct((T, W), f32)] * 3 + [jax.ShapeDtypeStruct((NH // HP, T, 128), f32),
                                                             jax.ShapeDtypeStruct((NH // HP, nq, 8, bq), f32)],
        compiler_params=_params("parallel", "arbitrary"),
    )(qs, ks, qt, kt, kh, vh, crow, ccol, dot, lse, delta)


def _attn_post(z, dy, oh, dqh, dkh, dvh, dck, dcq, fbias, gq, gk, dzbuf, tm):
    T = z.shape[0]
    c0 = OFF_ATT // W
    nt = T // tm

    def body(q_ref, k_ref, g_ref, f_ref, dy_ref, o_ref, dq_ref, dk_ref, dv_ref, dck_ref, dcq_ref, fb_ref, gq_ref,
             gk_ref, dzin_ref, dz_ref, ggq_ref, ggk_ref, gfb_ref, carry):
        @pl.when(pl.program_id(0) == 0)
        def _():
            carry[...] = jnp.zeros_like(carry)
            ggq_ref[...] = jnp.zeros_like(ggq_ref)
            ggk_ref[...] = jnp.zeros_like(ggk_ref)
            gfb_ref[...] = jnp.zeros_like(gfb_ref)

        gm = _group_mean_matrix(W, DH)
        hs = jnp.where((_iota2((W, W), 0) & (DH - 1)) == (_iota2((W, W), 1) & (DH - 1)), 1.0, 0.0).astype(f32)

        def norm_bwd(x, dn, gain):
            r = lax.rsqrt(_group_mean(x * x, gm) + EPS)
            gg = jnp.sum(dn * x * r, axis=0, keepdims=True)
            u = dn * gain
            return r * u - x * (r * r * r) * _group_mean(u * x, gm), gg

        q, k, gate = q_ref[...], k_ref[...], g_ref[...]
        dq, ggq = norm_bwd(q, dq_ref[...] * (DH ** -0.5), gq_ref[...])
        dk, ggk = norm_bwd(k, dk_ref[...], gk_ref[...])
        ggq_ref[...] += jnp.dot(jnp.broadcast_to(ggq, (8, W)), hs, precision=HI, preferred_element_type=f32)[0:1]
        ggk_ref[...] += jnp.dot(jnp.broadcast_to(ggk, (8, W)), hs, precision=HI, preferred_element_type=f32)[0:1]
        dgate = dy_ref[...] * o_ref[...] * _dsilu(gate)
        dck_v = dcq_ref[...]
        for hp in range(NH // HP):
            dck_v = dck_v + dck_ref[hp]
        rc = jnp.dot(_upper_tri(tm), dck_v, precision=HI, preferred_element_type=f32) + carry[...]
        carry[...] += jnp.sum(dck_v, axis=0, keepdims=True)
        f = f_ref[...] + fb_ref[...]
        df = jnp.where(_iota2((tm, 128), 1) < NH, rc * _sigmoid(-f), 0.0)
        gfb_ref[...] += jnp.sum(df, axis=0, keepdims=True)
        dz_ref[:, 0:W] = dq.astype(bf16)
        dz_ref[:, W:2 * W] = dk.astype(bf16)
        dz_ref[:, 2 * W:3 * W] = dv_ref[...].astype(bf16)
        dz_ref[:, 3 * W:4 * W] = dgate.astype(bf16)
        dz_ref[:, 4 * W:4 * W + 128] = df.astype(bf16)

    rev = lambda i: nt - 1 - i
    zb = lambda col: pl.BlockSpec((tm, W), lambda i, c=col: (rev(i), c))
    hspec = pl.BlockSpec((tm, W), lambda i: (rev(i), 0))
    return _pcall(
        body, name="attn_post", grid=(nt,),
        in_specs=[zb(c0), zb(c0 + 1), zb(c0 + 3),
                  pl.BlockSpec((tm, 128), lambda i: (rev(i), OFF_F // 128)),
                  pl.BlockSpec((tm, W), lambda i: (rev(i), 0)),
                  hspec, hspec, hspec, hspec,
                  pl.BlockSpec((NH // HP, tm, 128), lambda i: (0, rev(i), 0)),
                  pl.BlockSpec((tm, 128), lambda i: (rev(i), 0)),
                  pl.BlockSpec((1, 128), lambda i: (0, 0)),
                  pl.BlockSpec((1, W), lambda i: (0, 0)), pl.BlockSpec((1, W), lambda i: (0, 0)),
                  pl.BlockSpec(memory_space=pl.ANY)],
        out_specs=[pl.BlockSpec((tm, 4 * W + 128), lambda i: (rev(i), OFF_ATT // (4 * W + 128))),
                   pl.BlockSpec((1, W), lambda i: (0, 0)), pl.BlockSpec((1, W), lambda i: (0, 0)),
                   pl.BlockSpec((1, 128), lambda i: (0, 0))],
        out_shape=[jax.ShapeDtypeStruct((T, NZ), bf16), jax.ShapeDtypeStruct((1, W), f32),
                   jax.ShapeDtypeStruct((1, W), f32), jax.ShapeDtypeStruct((1, 128), f32)],
        scratch_shapes=[pltpu.VMEM((1, 128), f32)],
        input_output_aliases={14: 0},
        compiler_params=_params("arbitrary"),
    )(z, z, z, z, dy, oh, dqh, dkh, dvh, dck, dcq, fbias, gq, gk, dzbuf)


def _merge_fwd(ya, oh, z, yc, yd, mb, x, p, wup, wo, gp, wpg, wpp, tm):
    T = x.shape[0]
    cg = OFF_ATT // W + 3

    def body(ya_ref, oh_ref, bg_ref, yc_ref, yd_ref, ml_ref, mb_ref, x_ref, p_ref, wup_ref, wo_ref, gp_ref,
             wpg_ref, wpp_ref, yb_ref, mg_ref, x1_ref, x2_ref):
        yb = (oh_ref[...] * _silu(bg_ref[...])).astype(bf16)
        yb_ref[...] = yb
        ys = (ya_ref[...], yb, yc_ref[...], yd_ref[...])
        merged = jnp.zeros((tm, D), f32)
        for b in range(NBR):
            sg = _sigmoid(ml_ref[:, b * D:(b + 1) * D] + mb_ref[b:b + 1, :])
            merged = merged + sg * _mm(ys[b], wup_ref[b])
        mgb = merged.astype(bf16)
        mg_ref[...] = mgb
        x1 = x_ref[...] + _mm(mgb, wo_ref[...])
        x1_ref[...] = x1
        r = lax.rsqrt(jnp.mean(x1 * x1, axis=-1, keepdims=True) + EPS)
        hp = (x1 * r * gp_ref[...]).astype(bf16)
        gate = _sigmoid(_mm(hp, wpg_ref[...]))
        x2_ref[...] = x1 + gate * _mm(p_ref[...].astype(bf16), wpp_ref[...])

    row = lambda width: pl.BlockSpec((tm, width), lambda i: (i, 0))
    full = lambda *shape: pl.BlockSpec(shape, lambda i: (0,) * len(shape))
    return _pcall(
        body, name="merge_fwd", grid=(T // tm,),
        in_specs=[row(W), row(W), _zblock(tm, cg), row(W), row(W),
                  pl.BlockSpec((tm, NBR * D), lambda i: (i, 0)), full(NBR, D), row(D), row(PLE),
                  full(NBR, W, D), full(D, D), full(1, D), full(D, D), full(PLE, D)],
        out_specs=[row(W), row(D), row(D), row(D)],
        out_shape=[jax.ShapeDtypeStruct((T, W), bf16), jax.ShapeDtypeStruct((T, D), bf16),
                   jax.ShapeDtypeStruct((T, D), f32), jax.ShapeDtypeStruct((T, D), f32)],
        compiler_params=_params("parallel"),
    )(ya, oh, z, yc, yd, z, mb, x, p, wup, wo, gp, wpg, wpp)


def _layer_slabs(li, bufs):
    if bufs is None:
        return [], []
    return list(bufs), [pl.BlockSpec(memory_space=pl.ANY)] * len(bufs)


def _ple_bwd(dx2, x1, p, gp, wpg, wpp, tm, li, bufs):
    T = x1.shape[0]
    SH = D // N_DEV
    nt = T // tm
    extra, extra_specs = _layer_slabs(li, bufs)

    def body(dx2_ref, x1_ref, p_ref, gp_ref, wpg_ref, wpp_ref, *rest):
        dx1_ref, gwpg_ref, gwpp_ref, ggp_ref, gwpg_acc, gwpp_acc = rest[len(extra):]

        @pl.when(pl.program_id(0) == 0)
        def _():
            gwpg_acc[...] = jnp.zeros_like(gwpg_acc)
            gwpp_acc[...] = jnp.zeros_like(gwpp_acc)
            ggp_ref[...] = jnp.zeros_like(ggp_ref)

        x1, dx2 = x1_ref[...], dx2_ref[...]
        r = lax.rsqrt(jnp.mean(x1 * x1, axis=-1, keepdims=True) + EPS)
        xh = x1 * r
        gp = gp_ref[...]
        hp = (xh * gp).astype(bf16)
        gate = _sigmoid(_mm(hp, wpg_ref[...]))
        pb = p_ref[...].astype(bf16)
        pp = _mm(pb, wpp_ref[...])
        dpre = ((dx2 * pp) * gate * (1.0 - gate)).astype(bf16)
        gwpp_acc[...] += _mm_tn(pb, (dx2 * gate).astype(bf16))
        gwpg_acc[...] += _mm_tn(hp, dpre)
        dhp = _mm_nt(dpre, wpg_ref[...])
        ggp_ref[...] += jnp.sum(dhp * xh, axis=0, keepdims=True)
        u = dhp * gp
        dx1_ref[...] = dx2 + r * u - x1 * (r * r * r) * jnp.mean(u * x1, axis=-1, keepdims=True)

        @pl.when(pl.program_id(0) == nt - 1)
        def _():
            gwpg_ref[...] = gwpg_acc[...].reshape(N_DEV, SH, D).astype(bf16)
            for d in range(N_DEV):
                gwpp_ref[d] = gwpp_acc[:, d * SH:(d + 1) * SH].astype(bf16)

    row = lambda width: pl.BlockSpec((tm, width), lambda i: (i, 0))
    full = lambda *shape: pl.BlockSpec(shape, lambda i: (0,) * len(shape))
    n_in = 6
    return _pcall(
        body, name="ple_bwd", grid=(nt,),
        in_specs=[row(D), row(D), row(PLE), full(1, D), full(D, D), full(PLE, D)] + extra_specs,
        out_specs=[row(D), pl.BlockSpec((N_DEV, SH, D), lambda i: (0, li, 0)),
                   pl.BlockSpec((N_DEV, PLE, SH), lambda i: (0, li, 0)), full(1, D)],
        out_shape=[jax.ShapeDtypeStruct((T, D), f32), jax.ShapeDtypeStruct((N_DEV, DEPTH * SH, D), bf16),
                   jax.ShapeDtypeStruct((N_DEV, DEPTH * PLE, SH), bf16), jax.ShapeDtypeStruct((1, D), f32)],
        scratch_shapes=[pltpu.VMEM((D, D), f32), pltpu.VMEM((PLE, D), f32)],
        input_output_aliases={n_in + k: 1 + k for k in range(len(extra))},
        compiler_params=_params("arbitrary"),
    )(dx2, x1, p, gp, wpg, wpp, *extra)


def _merge_bwd(dx1, mg, ya, yb, yc, yd, z, mb, wup, wo, tm, li, bufs):
    T = dx1.shape[0]
    SH = D // N_DEV
    nt = T // tm
    extra, extra_specs = _layer_slabs(li, bufs)

    def body(dx1_ref, mg_ref, ya_ref, yb_ref, yc_ref, yd_ref, ml_ref, mb_ref, wup_ref, wo_ref, *rest):
        dml_ref, dya_ref, dyb_ref, dyc_ref, dyd_ref, gwo_ref, gwup_ref, gmb_ref, gwo_acc, gwup_acc = rest[len(extra):]

        @pl.when(pl.program_id(0) == 0)
        def _():
            gwo_acc[...] = jnp.zeros_like(gwo_acc)
            gwup_acc[...] = jnp.zeros_like(gwup_acc)
            gmb_ref[...] = jnp.zeros_like(gmb_ref)

        dx1b = dx1_ref[...].astype(bf16)
        gwo_acc[...] += _mm_tn(mg_ref[...], dx1b)
        dm = _mm_nt(dx1b, wo_ref[...])
        ys = (ya_ref, yb_ref, yc_ref, yd_ref)
        dys = (dya_ref, dyb_ref, dyc_ref, dyd_ref)
        for b in range(NBR):
            y = ys[b][...]
            up = _mm(y, wup_ref[b])
            sg = _sigmoid(ml_ref[:, b * D:(b + 1) * D] + mb_ref[b:b + 1, :])
            dup = (dm * sg).astype(bf16)
            dml = dm * up * sg * (1.0 - sg)
            gmb_ref[b:b + 1, :] += jnp.sum(dml, axis=0, keepdims=True)
            dml_ref[:, b * D:(b + 1) * D] = dml.astype(bf16)
            gwup_acc[b] += _mm_tn(y, dup)
            dys[b][...] = _mm_nt(dup, wup_ref[b])

        @pl.when(pl.program_id(0) == nt - 1)
        def _():
            gwo_ref[...] = gwo_acc[...].reshape(N_DEV, SH, D).astype(bf16)
            for d in range(N_DEV):
                gwup_ref[d] = gwup_acc[:, :, d * SH:(d + 1) * SH].reshape(NBR * W, SH).astype(bf16)

    row = lambda width: pl.BlockSpec((tm, width), lambda i: (i, 0))
    full = lambda *shape: pl.BlockSpec(shape, lambda i: (0,) * len(shape))
    n_in = 10
    return _pcall(
        body, name="merge_bwd", grid=(nt,),
        in_specs=[row(D), row(D), row(W), row(W), row(W), row(W), row(NBR * D), full(NBR, D),
                  full(NBR, W, D), full(D, D)] + extra_specs,
        out_specs=[row(NBR * D), row(W), row(W), row(W), row(W),
                   pl.BlockSpec((N_DEV, SH, D), lambda i: (0, li, 0)),
                   pl.BlockSpec((N_DEV, NBR * W, SH), lambda i: (0, li, 0)), full(NBR, D)],
        out_shape=[jax.ShapeDtypeStruct((T, NZ), bf16)] + [jax.ShapeDtypeStruct((T, W), f32)] * 4
        + [jax.ShapeDtypeStruct((N_DEV, DEPTH * SH, D), bf16),
           jax.ShapeDtypeStruct((N_DEV, DEPTH * NBR * W, SH), bf16),
           jax.ShapeDtypeStruct((NBR, D), f32)],
        scratch_shapes=[pltpu.VMEM((D, D), f32), pltpu.VMEM((NBR, W, D), f32)],
        input_output_aliases={n_in + k: 5 + k for k in range(len(extra))},
        compiler_params=_params("arbitrary"),
    )(dx1, mg, ya, yb, yc, yd, z, mb, wup, wo, *extra)


def _loss_head(y, target, tm):
    T = y.shape[0]

    def body(y_ref, t_ref, loss_ref, dy_ref, acc):
        i = pl.program_id(0)

        @pl.when(i == 0)
        def _():
            acc[...] = jnp.zeros_like(acc)

        e = y_ref[...] - t_ref[...]
        dy_ref[...] = e * (1.0 / D)
        acc[...] += jnp.sum(e * e, axis=0, keepdims=True)

        @pl.when(i == T // tm - 1)
        def _():
            loss_ref[...] = jnp.sum(acc[...], axis=1, keepdims=True) * (0.5 / D)

    return _pcall(
        body, name="loss_head", grid=(T // tm,),
        in_specs=[pl.BlockSpec((tm, D), lambda i: (i, 0)), pl.BlockSpec((tm, D), lambda i: (i, 0))],
        out_specs=[pl.BlockSpec((1, 1), lambda i: (0, 0)), pl.BlockSpec((tm, D), lambda i: (i, 0))],
        out_shape=[jax.ShapeDtypeStruct((1, 1), f32), jax.ShapeDtypeStruct((T, D), f32)],
        scratch_shapes=[pltpu.VMEM((1, D), f32)],
        compiler_params=_params("arbitrary"),
    )(y, target)


def _lb_softmax_rows(l_ref):
    rows = [l_ref[i:i + 1, :] for i in range(DEPTH)]
    m = rows[0]
    for r in rows[1:]:
        m = jnp.maximum(m, r)
    es = [jnp.exp(r - m) for r in rows]
    tot = es[0]
    for e in es[1:]:
        tot = tot + e
    return [e / tot for e in es]


def _lb_partial_sums(pr):
    sums = [jnp.zeros_like(pr[0])]
    for i in range(1, DEPTH):
        sums.append(sums[-1] + pr[i])
    return sums


def _stack_rows(rows, width):
    idx = _iota2((8, width), 0)
    out = jnp.zeros((8, width), f32)
    for i, r in enumerate(rows):
        out = jnp.where(idx == i, r, out)
    return out


def _lower_bounds(lb_logits):
    def body(l_ref, o_ref):
        sums = _lb_partial_sums(_lb_softmax_rows(l_ref))
        o_ref[...] = _stack_rows([jnp.clip(s, 0.0, 1.0) for s in sums], W)

    return _pcall(body, name="lower_bounds", out_shape=jax.ShapeDtypeStruct((8, W), f32))(lb_logits)


def _lower_bounds_bwd(lb_logits, dlower):
    def body(l_ref, d_ref, o_ref):
        pr = _lb_softmax_rows(l_ref)
        sums = _lb_partial_sums(pr)
        dl = [jnp.where((sums[i] > 0.0) & (sums[i] < 1.0), d_ref[i:i + 1, :], 0.0) for i in range(DEPTH)]
        dp = [jnp.zeros_like(pr[0])] * DEPTH
        run = jnp.zeros_like(pr[0])
        for j in reversed(range(1, DEPTH)):
            run = run + dl[j]
            dp[j] = run
        inner = pr[0] * dp[0]
        for j in range(1, DEPTH):
            inner = inner + pr[j] * dp[j]
        o_ref[...] = _stack_rows([pr[j] * (dp[j] - inner) for j in range(DEPTH)], W)

    return _pcall(body, name="lower_bounds_bwd", out_shape=jax.ShapeDtypeStruct((8, W), f32))(lb_logits, dlower)


def _row_tile(rows, cols, budget_bytes=1 << 20, mult=8):
    if rows % mult:
        return rows
    best = mult
    for t in range(mult, rows + 1, mult):
        if rows % t == 0 and t * cols * 4 <= budget_bytes:
            best = t
    return best


def _sum_slabs(land):
    N, R, C = land.shape
    tr = _row_tile(R, C * N, mult=16)

    def body(l_ref, o_ref):
        acc = l_ref[0].astype(f32)
        for j in range(1, N):
            acc = acc + l_ref[j].astype(f32)
        o_ref[...] = acc

    return _pcall(
        body, name="sum_slabs", grid=(R // tr,),
        in_specs=[pl.BlockSpec((N, tr, C), lambda i: (0, i, 0))],
        out_specs=pl.BlockSpec((tr, C), lambda i: (i, 0)),
        out_shape=jax.ShapeDtypeStruct((R, C), f32),
        compiler_params=_params("parallel"),
    )(land)


def _adamw_update(w_ref, g_ref, m_ref, v_ref, d_ref, nm_ref, nv_ref):
    c1 = 1.0 / (1.0 - ADAM_B1 ** ADAM_STEP)
    c2 = 1.0 / (1.0 - ADAM_B2 ** ADAM_STEP)
    gv = g_ref[...]
    nm = ADAM_B1 * m_ref[...] + (1.0 - ADAM_B1) * gv
    nv = ADAM_B2 * v_ref[...] + (1.0 - ADAM_B2) * (gv * gv)
    nm_ref[...] = nm
    nv_ref[...] = nv
    d_ref[...] = -ADAM_LR * ((nm * c1) / (jnp.sqrt(nv * c2) + ADAM_EPS) + ADAM_WD * w_ref[...])


def _adamw3(w, g, m, v):
    L, R, C = w.shape
    tr = _row_tile(R, C)

    def body(*refs):
        _adamw_update(*refs)

    spec = pl.BlockSpec((None, tr, C), lambda l, i: (l, i, 0))
    return _pcall(
        body, name="adamw3", grid=(L, R // tr),
        in_specs=[spec] * 4, out_specs=[spec] * 3,
        out_shape=[jax.ShapeDtypeStruct((L, R, C), f32)] * 3,
        compiler_params=_params("parallel", "parallel"),
    )(w, g, m, v)


def _adamw(w, g, m, v):
    if w.ndim == 3:
        return _adamw3(w, g, m, v)
    R, C = w.shape
    tr = _row_tile(R, C)
    c1 = 1.0 / (1.0 - ADAM_B1 ** ADAM_STEP)
    c2 = 1.0 / (1.0 - ADAM_B2 ** ADAM_STEP)

    def body(w_ref, g_ref, m_ref, v_ref, d_ref, nm_ref, nv_ref):
        gv = g_ref[...]
        nm = ADAM_B1 * m_ref[...] + (1.0 - ADAM_B1) * gv
        nv = ADAM_B2 * v_ref[...] + (1.0 - ADAM_B2) * (gv * gv)
        nm_ref[...] = nm
        nv_ref[...] = nv
        d_ref[...] = -ADAM_LR * ((nm * c1) / (jnp.sqrt(nv * c2) + ADAM_EPS) + ADAM_WD * w_ref[...])

    spec = pl.BlockSpec((tr, C), lambda i: (i, 0))
    return _pcall(
        body, name="adamw", grid=(R // tr,),
        in_specs=[spec] * 4, out_specs=[spec] * 3,
        out_shape=[jax.ShapeDtypeStruct((R, C), f32)] * 3,
        compiler_params=_params("parallel"),
    )(w, g, m, v)


def _my_id():
    return lax.axis_index("x") * 4 + lax.axis_index("y") * 2 + lax.axis_index("c")


def _peer(k):
    x, y, c = lax.axis_index("x"), lax.axis_index("y"), lax.axis_index("c")
    kx, ky, kc = (k >> 2) & 1, (k >> 1) & 1, k & 1
    px, py, pc = x ^ kx, y ^ ky, c ^ kc
    return (px, py, pc), px * 4 + py * 2 + pc


def _all_gather(shards, axes):
    n = len(shards)

    def body(*refs):
        start, forward, finish = _gather_phases(shards, axes, refs[:n], refs[n:2 * n], *refs[2 * n:])
        start()
        forward()
        finish()

    hbm = pl.BlockSpec(memory_space=pltpu.HBM)
    return _pcall(
        body, name="all_gather",
        in_specs=[hbm] * n, out_specs=[hbm] * n,
        out_shape=_gathered_shapes(shards, axes),
        scratch_shapes=_gather_semaphores(n),
    )(*shards)


def _gathered_shapes(shards, axes):
    def full_shape(s, ax):
        shp = list(s.shape)
        shp[ax] *= N_DEV
        return tuple(shp)

    return [jax.ShapeDtypeStruct(full_shape(s, ax), s.dtype) for s, ax in zip(shards, axes)]


def _gather_semaphores(n):
    return [pltpu.SemaphoreType.DMA((n, N_DEV - 1)), pltpu.SemaphoreType.DMA((n, N_DEV - 1)),
            pltpu.SemaphoreType.DMA((n,))]


def _gather_phases(shards, axes, srcs, outs, send_sems, recv_sems, local_sems):
    n = len(shards)
    x, y, c = lax.axis_index("x"), lax.axis_index("y"), lax.axis_index("c")
    me, sibling = (x, y, c), (x, y, 1 - c)
    chips = [(1 - x, y), (x, 1 - y), (1 - x, 1 - y)]

    def block(a, dev):
        j = dev[0] * 4 + dev[1] * 2 + dev[2]
        size = shards[a].shape[axes[a]]
        start = pl.multiple_of(j * size, size)
        if axes[a] == 0:
            return outs[a].at[pl.ds(start, size), :]
        if axes[a] == 1:
            return outs[a].at[:, pl.ds(start, size), :]
        return outs[a].at[:, pl.ds(start, size)]

    def copy(a, k, dev, to, src=None):
        return pltpu.make_async_remote_copy(
            src_ref=block(a, dev) if src is None else src, dst_ref=block(a, dev),
            send_sem=send_sems.at[a, k], recv_sem=recv_sems.at[a, k],
            device_id=to, device_id_type=pl.DeviceIdType.MESH)

    def mine():
        return [pltpu.make_async_copy(srcs[a], block(a, me), local_sems.at[a]) for a in range(n)]

    def first():
        cps = []
        for a in range(n):
            cps.append(copy(a, 0, me, sibling, src=srcs[a]))
            cps += [copy(a, 1 + j, me, (*chip, c), src=srcs[a]) for j, chip in enumerate(chips)]
        return cps

    def passed():
        return [copy(a, 4 + j, (*chip, c), sibling) for j, chip in enumerate(chips) for a in range(n)]

    def start():
        for cp in mine() + first():
            cp.start()

    def forward():
        for j, chip in enumerate(chips):
            for a in range(n):
                copy(a, 1 + j, (*chip, c), me).wait_recv()
                copy(a, 4 + j, (*chip, c), sibling).start()

    def finish():
        for a in range(n):
            copy(a, 0, sibling, me).wait_recv()
            for j, chip in enumerate(chips):
                copy(a, 4 + j, (*chip, 1 - c), me).wait_recv()
        for cp in first() + passed():
            cp.wait_send()
        for cp in mine():
            cp.wait()

    return start, forward, finish


N_CHIP = N_DEV // 2


def _exchange_sibling(sliced):
    n = len(sliced)

    def body(*refs):
        srcs, outs = refs[:n], refs[n:2 * n]
        send_sems, recv_sems = refs[2 * n:]
        x, y, c = lax.axis_index("x"), lax.axis_index("y"), lax.axis_index("c")
        copies = []
        for a in range(n):
            for q in range(N_CHIP):
                cp = pltpu.make_async_remote_copy(
                    src_ref=srcs[a].at[2 * q + (1 - c)], dst_ref=outs[a].at[q],
                    send_sem=send_sems.at[a, q], recv_sem=recv_sems.at[a, q],
                    device_id=(x, y, 1 - c), device_id_type=pl.DeviceIdType.MESH)
                cp.start()
                copies.append(cp)
        for cp in copies:
            cp.wait_recv()
        for cp in copies:
            cp.wait_send()

    hbm = pl.BlockSpec(memory_space=pltpu.HBM)
    return _pcall(
        body, name="grad_exchange_sibling",
        in_specs=[hbm] * n, out_specs=[hbm] * n,
        out_shape=[jax.ShapeDtypeStruct((N_CHIP,) + s.shape[1:], s.dtype) for s in sliced],
        scratch_shapes=[pltpu.SemaphoreType.DMA((n, N_CHIP)), pltpu.SemaphoreType.DMA((n, N_CHIP))],
    )(*sliced)


def _pair_sum(own, recv):
    _, R, C = own.shape
    tr = _row_tile(R, C, mult=16)
    side = lax.axis_index("c").astype(jnp.int32).reshape(1)

    def body(c_ref, own_ref, recv_ref, o_ref):
        o_ref[...] = (own_ref[...].astype(f32) + recv_ref[...].astype(f32)).astype(o_ref.dtype)

    return _pcall(
        body, name="pair_sum",
        grid_spec=pltpu.PrefetchScalarGridSpec(
            num_scalar_prefetch=1, grid=(N_CHIP, R // tr),
            in_specs=[pl.BlockSpec((None, tr, C), lambda q, i, c: (2 * q + c[0], i, 0)),
                      pl.BlockSpec((None, tr, C), lambda q, i, c: (q, i, 0))],
            out_specs=pl.BlockSpec((None, tr, C), lambda q, i, c: (q, i, 0))),
        out_shape=jax.ShapeDtypeStruct((N_CHIP, R, C), own.dtype),
        compiler_params=_params("parallel", "parallel"),
    )(side, own, recv)


def _exchange_chips(partial, whole):
    ns, nw = len(partial), len(whole)

    def body(*refs):
        srcs, outs = refs[:ns + nw], refs[ns + nw:2 * (ns + nw)]
        send_sems, recv_sems, wsend_sems, wrecv_sems, local_sems = refs[2 * (ns + nw):]
        x, y, c = lax.axis_index("x"), lax.axis_index("y"), lax.axis_index("c")
        me, myq = _my_id(), x * 2 + y
        chips = [(1 - x, y), (x, 1 - y), (1 - x, 1 - y)]
        locals_ = [pltpu.make_async_copy(srcs[a].at[myq], outs[a].at[myq], local_sems.at[a]) for a in range(ns)]
        locals_ += [pltpu.make_async_copy(srcs[ns + b], outs[ns + b].at[me], local_sems.at[ns + b])
                    for b in range(nw)]
        for cp in locals_:
            cp.start()
        sends, recvs = [], []
        for j, chip in enumerate(chips):
            q = chip[0] * 2 + chip[1]
            for a in range(ns):
                cp = pltpu.make_async_remote_copy(
                    src_ref=srcs[a].at[q], dst_ref=outs[a].at[myq],
                    send_sem=send_sems.at[a, j], recv_sem=recv_sems.at[a, j],
                    device_id=(*chip, c), device_id_type=pl.DeviceIdType.MESH)
                cp.start()
                sends.append(cp)
                recvs.append(pltpu.make_async_remote_copy(
                    src_ref=srcs[a].at[q], dst_ref=outs[a].at[q],
                    send_sem=send_sems.at[a, j], recv_sem=recv_sems.at[a, j],
                    device_id=(*chip, c), device_id_type=pl.DeviceIdType.MESH))
        for k in range(1, N_DEV):
            peer, pid = _peer(k)
            for b in range(nw):
                cp = pltpu.make_async_remote_copy(
                    src_ref=srcs[ns + b], dst_ref=outs[ns + b].at[me],
                    send_sem=wsend_sems.at[b, k - 1], recv_sem=wrecv_sems.at[b, k - 1],
                    device_id=peer, device_id_type=pl.DeviceIdType.MESH)
                cp.start()
                sends.append(cp)
                recvs.append(pltpu.make_async_remote_copy(
                    src_ref=srcs[ns + b], dst_ref=outs[ns + b].at[pid],
                    send_sem=wsend_sems.at[b, k - 1], recv_sem=wrecv_sems.at[b, k - 1],
                    device_id=peer, device_id_type=pl.DeviceIdType.MESH))
        for cp in recvs:
            cp.wait_recv()
        for cp in sends:
            cp.wait_send()
        for cp in locals_:
            cp.wait()

    hbm = pl.BlockSpec(memory_space=pltpu.HBM)
    shapes = [jax.ShapeDtypeStruct(s.shape, s.dtype) for s in partial]
    shapes += [jax.ShapeDtypeStruct((N_DEV,) + s.shape, s.dtype) for s in whole]
    return _pcall(
        body, name="grad_exchange_chips",
        in_specs=[hbm] * (ns + nw), out_specs=[hbm] * (ns + nw), out_shape=shapes,
        scratch_shapes=[pltpu.SemaphoreType.DMA((ns, N_CHIP - 1)), pltpu.SemaphoreType.DMA((ns, N_CHIP - 1)),
                        pltpu.SemaphoreType.DMA((nw, N_DEV - 1)), pltpu.SemaphoreType.DMA((nw, N_DEV - 1)),
                        pltpu.SemaphoreType.DMA((ns + nw,))],
    )(*partial, *whole)


def _permute_cols(w):
    pad = jnp.zeros(w.shape[:-1] + (NZ - OFF_F - NH,), w.dtype)
    return jnp.concatenate([
        w[..., 3844:7940],
        w[..., 0:1024],
        w[..., 2052:3076],
        w[..., 3076:3844],
        w[..., 1024:2048],
        w[..., 2048:2052], pad], axis=-1)


def _unpermute_cols(g):
    return jnp.concatenate([
        g[..., OFF_CONV:OFF_CONV + 1024],
        g[..., OFF_ATT:OFF_ATT + 1024],
        g[..., OFF_F:OFF_F + NH],
        g[..., OFF_HGRN:OFF_HGRN + 1024],
        g[..., OFF_SGU:OFF_SGU + 768],
        g[..., 0:4096]], axis=-1)


_SMALL = (
    ("norm_mix", (DEPTH, D)), ("conv_w", (DEPTH, CONV_WIDTH, W)), ("conv_b", (DEPTH, W)),
    ("fgate_bias", (DEPTH, NH)), ("q_norm", (DEPTH, DH)), ("k_norm", (DEPTH, DH)),
    ("lb_logits", (DEPTH, W)), ("hgrn_norm", (DEPTH, W)), ("sgu_norm", (DEPTH, W)),
    ("spatial_w", (DEPTH, NH, SGU_CHUNK, SGU_CHUNK)), ("spatial_b", (DEPTH, NH, SGU_CHUNK)),
    ("merge_b", (DEPTH, NBR, D)), ("norm_ple", (DEPTH, D)),
)


def _small_rows(shape):
    size = 1
    for s in shape:
        size *= s
    rows = -(-size // 128)
    return size, -(-rows // 8) * 8


def _pack_small(parts):
    out = []
    for name, shape in _SMALL:
        size, rows = _small_rows(shape)
        flat = parts[name].astype(f32).reshape(-1)
        flat = jnp.pad(flat, (0, rows * 128 - size))
        out.append(flat.reshape(rows, 128))
    return jnp.concatenate(out, axis=0)


def _unpack_small(buf):
    parts, r0 = {}, 0
    for name, shape in _SMALL:
        size, rows = _small_rows(shape)
        parts[name] = buf[r0:r0 + rows].reshape(-1)[:size].reshape(shape)
        r0 += rows
    return parts


def _shard_cols(a, width):
    return lax.dynamic_slice_in_dim(a, _my_id() * width, width, axis=a.ndim - 1)


def kernel(x, p, norm_mix, w_in, conv_w, conv_b, fgate_bias, q_norm, k_norm, lb_logits, hgrn_norm, sgu_norm, spatial_w, spatial_b, w_up, merge_b, w_o, norm_ple, w_ple_gate, w_ple_proj, loss_target, m_norm_mix, m_w_in, m_conv_w, m_conv_b, m_fgate_bias, m_q_norm, m_k_norm, m_lb_logits, m_hgrn_norm, m_sgu_norm, m_spatial_w, m_spatial_b, m_w_up, m_merge_b, m_w_o, m_norm_ple, m_w_ple_gate, m_w_ple_proj, v_norm_mix, v_w_in, v_conv_w, v_conv_b, v_fgate_bias, v_q_norm, v_k_norm, v_lb_logits, v_hgrn_norm, v_sgu_norm, v_spatial_w, v_spatial_b, v_w_up, v_merge_b, v_w_o, v_norm_ple, v_w_ple_gate, v_w_ple_proj):
    T = x.shape[1]
    SH = D // N_DEV
    CW = W // N_DEV
    tm = 512 if T % 512 == 0 else T
    tmm = 256 if T % 256 == 0 else T
    x0 = x.reshape(T, D)
    target = loss_target.reshape(T, D)

    small_shard = jnp.concatenate([
        merge_b.reshape(DEPTH * NBR, SH),
        jnp.pad(conv_w.reshape(DEPTH * CONV_WIDTH, CW), ((0, 16 - DEPTH * CONV_WIDTH), (0, SH - CW)))], axis=0)
    win_s = _permute_cols(w_in).astype(bf16)
    win0, wup_f, wo_f, wpg_f, wpp_f, g_small = _all_gather(
        [win_s[0],
         w_up.astype(bf16).reshape(DEPTH * NBR * W, SH),
         w_o.astype(bf16),
         w_ple_gate.astype(bf16),
         w_ple_proj.astype(bf16).reshape(DEPTH * PLE, SH),
         small_shard],
        [0, -1, 1, 1, -1, -1])
    win_f = [win0]
    win_later = [win_s[li] for li in range(1, DEPTH)]
    wup_f = wup_f.reshape(DEPTH, NBR, W, D)
    wpp_f = wpp_f.reshape(DEPTH, PLE, D)
    mb_f = g_small[0:DEPTH * NBR].reshape(DEPTH, NBR, D)
    cw_f = g_small[16:16 + DEPTH * CONV_WIDTH].reshape(DEPTH, CONV_WIDTH, N_DEV, SH)[..., 0:CW]
    cw_f = cw_f.reshape(DEPTH, CONV_WIDTH, W)

    loss_local, dx, gw, gs_full = _forward_backward(
        x0, p[:, 0], target, win_f, wup_f, wo_f, wpg_f, wpp_f, mb_f, cw_f, norm_mix, conv_b, fgate_bias, q_norm,
        k_norm, lb_logits, hgrn_norm, sgu_norm, spatial_w, spatial_b, norm_ple, win_later)
    loss = lax.psum(loss_local[0, 0], AXES)
    grad_x = dx.reshape(1, T, D)

    weights = dict(norm_mix=norm_mix, w_in=w_in, conv_w=conv_w, conv_b=conv_b, fgate_bias=fgate_bias, q_norm=q_norm,
                   k_norm=k_norm, lb_logits=lb_logits, hgrn_norm=hgrn_norm, sgu_norm=sgu_norm, spatial_w=spatial_w,
                   spatial_b=spatial_b, w_up=w_up, merge_b=merge_b, w_o=w_o, norm_ple=norm_ple,
                   w_ple_gate=w_ple_gate, w_ple_proj=w_ple_proj)
    ms = dict(norm_mix=m_norm_mix, w_in=m_w_in, conv_w=m_conv_w, conv_b=m_conv_b, fgate_bias=m_fgate_bias,
              q_norm=m_q_norm, k_norm=m_k_norm, lb_logits=m_lb_logits, hgrn_norm=m_hgrn_norm, sgu_norm=m_sgu_norm,
              spatial_w=m_spatial_w, spatial_b=m_spatial_b, w_up=m_w_up, merge_b=m_merge_b, w_o=m_w_o,
              norm_ple=m_norm_ple, w_ple_gate=m_w_ple_gate, w_ple_proj=m_w_ple_proj)
    vs = dict(norm_mix=v_norm_mix, w_in=v_w_in, conv_w=v_conv_w, conv_b=v_conv_b, fgate_bias=v_fgate_bias,
              q_norm=v_q_norm, k_norm=v_k_norm, lb_logits=v_lb_logits, hgrn_norm=v_hgrn_norm, sgu_norm=v_sgu_norm,
              spatial_w=v_spatial_w, spatial_b=v_spatial_b, w_up=v_w_up, merge_b=v_merge_b, w_o=v_w_o,
              norm_ple=v_norm_ple, w_ple_gate=v_w_ple_gate, w_ple_proj=v_w_ple_proj)
    return _exchange_and_update(loss, grad_x, gw, gs_full, weights, ms, vs)


def _forward_backward(x0, p, target, win_f, wup_f, wo_f, wpg_f, wpp_f, mb_f, cw_f, norm_mix, conv_b, fgate_bias,
                      q_norm, k_norm, lb_logits, hgrn_norm, sgu_norm, spatial_w, spatial_b, norm_ple, win_later=()):
    T = x0.shape[0]
    tm = 512 if T % 512 == 0 else T
    tmm = 256 if T % 256 == 0 else T
    tmi = 1024 if T % 1024 == 0 else tm
    lower = _lower_bounds(lb_logits)
    fb_pad = jnp.pad(fgate_bias, ((0, 0), (0, 128 - NH)))
    gq_t = jnp.tile(q_norm, (1, NH))
    gk_t = jnp.tile(k_norm, (1, NH))
    sbe = jnp.repeat(jnp.swapaxes(spatial_b, 1, 2), DH, axis=2)

    saved = []
    xc = x0
    p = p[:, None]
    for li in range(DEPTH):
        row = lambda a: a[li:li + 1]
        if li == 0 and win_later:
            z, h, *gathered = _inproj_fwd(xc, row(norm_mix), win_f[0], tmi, gather=win_later)
            win_f = [win_f[0]] + gathered
        else:
            z, h = _inproj_fwd(xc, row(norm_mix), win_f[li], tmi)
        ya = _conv_fwd(z, cw_f[li], row(conv_b), tm)
        yd = _sgu_fwd(z, row(sgu_norm), spatial_w[li], sbe[li], tm)
        yc, o_pre, states = _hgrn_fwd(z, lower[li:li + 1], row(hgrn_norm), tmm)
        qt, kt, vt, kh, vh, cum = _attn_prep(z, row(fb_pad), row(gq_t), row(gk_t), tm)
        cum4 = jnp.transpose(cum[:, 0:NH])
        ccol, crow = cum4[:, :, None], cum4[:, None, :]
        oh, lse = _attn_fwd(qt, kh, vt, crow, ccol, tm)
        yb, mg, x1, x2 = _merge_fwd(ya, oh, z, yc, yd, mb_f[li], xc, p[li, 0], wup_f[li], wo_f[li],
                                    row(norm_ple), wpg_f[li], wpp_f[li], tmm)
        saved.append(dict(x=xc, z=z, h=h, ya=ya, yb=yb, yc=yc, yd=yd, o_pre=o_pre, states=states,
                          qt=qt, kt=kt, kh=kh, vh=vh, crow=crow, ccol=ccol, oh=oh, lse=lse, mg=mg, x1=x1))
        xc = x2

    loss_local, dx = _loss_head(xc, target, tm)

    gw = dict(w_in=None, w_up=None, w_o=None, w_ple_gate=None, w_ple_proj=None)
    gs = {n: [None] * DEPTH for n, _ in _SMALL}
    dlower = [None] * DEPTH
    for li in reversed(range(DEPTH)):
        s = saved[li]
        row = lambda a: a[li:li + 1]
        first = li == DEPTH - 1
        dx1, gw["w_ple_gate"], gw["w_ple_proj"], ggp = _ple_bwd(
            dx, s["x1"], p[li, 0], row(norm_ple), wpg_f[li], wpp_f[li], tmm, li,
            None if first else (gw["w_ple_gate"], gw["w_ple_proj"]))
        gs["norm_ple"][li] = ggp[0]
        dz, dya, dyb, dyc, dyd, gw["w_o"], gw["w_up"], gs["merge_b"][li] = _merge_bwd(
            dx1, s["mg"], s["ya"], s["yb"], s["yc"], s["yd"], s["z"], mb_f[li], wup_f[li], wo_f[li], tmm, li,
            None if first else (gw["w_o"], gw["w_up"]))
        dz, gcw, gcb = _conv_bwd(s["z"], dya, cw_f[li], row(conv_b), dz, tm)
        gs["conv_w"][li], gs["conv_b"][li] = gcw[0:CONV_WIDTH], gcb[0]
        dz, gs["spatial_w"][li], gsb, ggv = _sgu_bwd(s["z"], dyd, row(sgu_norm), spatial_w[li], sbe[li], dz, tm)
        gs["spatial_b"][li] = jnp.transpose(gsb[:, ::DH])
        gs["sgu_norm"][li] = ggv[0]
        dz, ggn, glb = _hgrn_bwd(s["z"], lower[li:li + 1], row(hgrn_norm), s["o_pre"], s["states"], dyc, dz, tmm)
        gs["hgrn_norm"][li], dlower[li] = ggn[0], glb[0]
        dot, delta = _attn_bwd_prep(dyb, s["oh"], s["z"], tm)
        dqh, dkh, dvh, dck, dcq = _attn_bwd(s["qt"], s["kt"], s["kh"], s["vh"], s["crow"], s["ccol"], dot,
                                            s["lse"], delta, tm)
        dcq_t = jnp.transpose(dcq[:, :, 0:HP, :], (0, 2, 1, 3)).reshape(NH, T)
        dcq_t = jnp.pad(jnp.transpose(dcq_t), ((0, 0), (0, 128 - NH)))
        dz, ggq, ggk, gfb = _attn_post(s["z"], dyb, s["oh"], dqh, dkh, dvh, dck, dcq_t, row(fb_pad),
                                       row(gq_t), row(gk_t), dz, tm)
        gs["q_norm"][li], gs["k_norm"][li], gs["fgate_bias"][li] = ggq[0, 0:DH], ggk[0, 0:DH], gfb[0, 0:NH]
        dx, gnm = _inproj_bwd_x(dz, win_f[li], s["x"], dx1, row(norm_mix), tmi)
        gs["norm_mix"][li] = gnm[0]
        gw["w_in"] = _inproj_bwd_w(s["h"], dz, tmi, li, gw["w_in"])
    dlower8 = jnp.pad(jnp.stack(dlower), ((0, 8 - DEPTH), (0, 0)))
    gs_full = {n: jnp.stack(v) for n, v in gs.items() if n != "lb_logits"}
    gs_full["lb_logits"] = _lower_bounds_bwd(lb_logits, dlower8)[0:DEPTH]
    return loss_local, dx, gw, gs_full


def _exchange_and_update(loss, grad_x, gw, gs_full, weights, ms, vs):
    SH = D // N_DEV
    CW = W // N_DEV

    small_buf = _pack_small(gs_full)
    own = [gw["w_in"], gw["w_up"], gw["w_o"], gw["w_ple_gate"], gw["w_ple_proj"]]
    from_sibling = _exchange_sibling(own)
    chip_sums = [_pair_sum(o, r) for o, r in zip(own, from_sibling)]
    l_win, l_wup, l_wo, l_wpg, l_wpp, l_small = _exchange_chips(chip_sums, [small_buf])

    g_w_in = _unpermute_cols(_sum_slabs(l_win)).reshape(DEPTH, SH, IN_COLS)
    g_w_up = _sum_slabs(l_wup).reshape(DEPTH, NBR, W, SH)
    g_w_o = _sum_slabs(l_wo).reshape(DEPTH, SH, D)
    g_w_pg = _sum_slabs(l_wpg).reshape(DEPTH, SH, D)
    g_w_pp = _sum_slabs(l_wpp).reshape(DEPTH, PLE, SH)
    g_small = _unpack_small(_sum_slabs(l_small))
    g_small_local = dict(g_small)
    g_small_local["conv_w"] = _shard_cols(g_small["conv_w"], CW)
    g_small_local["merge_b"] = _shard_cols(g_small["merge_b"], SH)

    grads = dict(w_in=g_w_in, w_up=g_w_up, w_o=g_w_o, w_ple_gate=g_w_pg, w_ple_proj=g_w_pp)
    deltas, new_m, new_v = {}, {}, {}
    for name in ("w_in", "w_up", "w_o", "w_ple_gate", "w_ple_proj"):
        shape = weights[name].shape
        as3 = (shape[0], -1, shape[-1])
        d_, m_, v_ = _adamw(weights[name].reshape(as3), grads[name].reshape(as3),
                            ms[name].reshape(as3), vs[name].reshape(as3))
        deltas[name], new_m[name], new_v[name] = d_.reshape(shape), m_.reshape(shape), v_.reshape(shape)

    def local_shapes(parts):
        return {n: (parts[n] if parts[n].shape == s else jnp.pad(
            parts[n], [(0, 0)] * (len(s) - 1) + [(0, s[-1] - parts[n].shape[-1])])) for n, s in _SMALL}

    d_, m_, v_ = _adamw(_pack_small(local_shapes(weights)), _pack_small(local_shapes(g_small_local)),
                        _pack_small(local_shapes(ms)), _pack_small(local_shapes(vs)))
    for buf, dst in ((d_, deltas), (m_, new_m), (v_, new_v)):
        parts = _unpack_small(buf)
        for n, _ in _SMALL:
            dst[n] = parts[n][..., :weights[n].shape[-1]]
    for n, _ in _SMALL:
        grads[n] = g_small_local[n]

    order = ["norm_mix", "w_in", "conv_w", "conv_b", "fgate_bias", "q_norm", "k_norm", "lb_logits", "hgrn_norm",
             "sgu_norm", "spatial_w", "spatial_b", "w_up", "merge_b", "w_o", "norm_ple", "w_ple_gate", "w_ple_proj"]
    return (loss, grad_x, *[grads[n] for n in order], *[deltas[n] for n in order],
            *[new_m[n] for n in order], *[new_v[n] for n in order])
```

```python
import functools

import jax
import jax.numpy as jnp
from jax import lax
from jax.experimental import pallas as pl
from jax.experimental.pallas import tpu as pltpu

f32 = jnp.float32
bf16 = jnp.bfloat16

D = 1024
W = 256
NH = 4
DH = 64
NBR = 4
PLE = 256
DEPTH = 4
CONV_WIDTH = 3
SGU_CHUNK = 128
GLA_CHUNK = 128
EPS = 1e-6
MASK_VALUE = -1e30
IN_COLS = 7940
NZ = 8064
OFF_CONV = 4096
OFF_HGRN = 5120
OFF_SGU = 6144
OFF_ATT = 6912
OFF_F = 7936
ZT = 1152
NZT = NZ // ZT
EXP_CLAMP = 80.0
LOG2E = 1.4426950408889634

ADAM_LR = 0.001
ADAM_B1 = 0.9
ADAM_B2 = 0.999
ADAM_EPS = 1e-08
ADAM_WD = 0.01
ADAM_STEP = 10

N_DEV = 8
AXES = ("x", "y", "c")
VMEM_LIMIT = 56 * 1024 * 1024
HI = lax.Precision.HIGHEST

NT_DIMS = (((1,), (1,)), ((), ()))
TN_DIMS = (((0,), (0,)), ((), ()))


def _pcall(body, **kw):
    return pl.pallas_call(body, **kw)


def _params(*sem):
    return pltpu.CompilerParams(dimension_semantics=sem, vmem_limit_bytes=VMEM_LIMIT)


def _mm(a, b):
    return jnp.dot(a, b, preferred_element_type=f32)


def _mm_nt(a, b):
    return lax.dot_general(a, b, NT_DIMS, preferred_element_type=f32)


def _mm_tn(a, b):
    return lax.dot_general(a, b, TN_DIMS, preferred_element_type=f32)


def _sigmoid(x):
    return 1.0 / (1.0 + jnp.exp(-x))


def _silu(x):
    return x * _sigmoid(x)


def _dsilu(x):
    s = _sigmoid(x)
    return s * (1.0 + x * (1.0 - s))


def _logsigmoid(x):
    return jnp.minimum(x, 0.0) - jnp.log(1.0 + jnp.exp(-jnp.abs(x)))


def _iota2(shape, axis):
    return lax.broadcasted_iota(jnp.int32, shape, axis)


def _group_mean_matrix(n, group):
    shift = group.bit_length() - 1
    r = lax.shift_right_logical(_iota2((n, n), 0), shift)
    c = lax.shift_right_logical(_iota2((n, n), 1), shift)
    return jnp.where(r == c, 1.0 / group, 0.0).astype(f32)


def _group_mean(x, gm):
    return jnp.dot(x, gm, precision=HI, preferred_element_type=f32)


def _lower_tri(n):
    return jnp.where(_iota2((n, n), 0) >= _iota2((n, n), 1), 1.0, 0.0).astype(f32)


def _upper_tri(n):
    return jnp.where(_iota2((n, n), 0) <= _iota2((n, n), 1), 1.0, 0.0).astype(f32)


def _rows3(r0, r1, r2, width):
    row = _iota2((8, width), 0)
    return jnp.where(row == 0, r0, jnp.where(row == 1, r1, jnp.where(row == 2, r2, 0.0)))


def _inproj_fwd(x, g, w, tm, gather=()):
    T = x.shape[0]
    n = len(gather)
    axes = [0] * n
    steps = (T // tm) * NZT

    def body(x_ref, g_ref, w_ref, *rest):
        z_ref, h_ref = rest[n:n + 2]

        @pl.when(pl.program_id(1) == 0)
        def _():
            xv = x_ref[...]
            r = lax.rsqrt(jnp.mean(xv * xv, axis=-1, keepdims=True) + EPS)
            h_ref[...] = (xv * r * g_ref[...]).astype(bf16)

        if n:
            start, forward, finish = _gather_phases(gather, axes, rest[:n], rest[n + 2:2 * n + 2], *rest[2 * n + 2:])
            step = pl.program_id(0) * NZT + pl.program_id(1)
            pl.when(step == 0)(start)
            pl.when(step == steps // 2)(forward)

        z_ref[...] = _mm(h_ref[...], w_ref[...])

        if n:
            pl.when(step == steps - 1)(finish)

    hbm = pl.BlockSpec(memory_space=pltpu.HBM)
    return _pcall(
        body, name="inproj_fwd_gather" if n else "inproj_fwd", grid=(T // tm, NZT),
        in_specs=[pl.BlockSpec((tm, D), lambda i, j: (i, 0)),
                  pl.BlockSpec((1, D), lambda i, j: (0, 0)),
                  pl.BlockSpec((D, ZT), lambda i, j: (0, j))] + [hbm] * n,
        out_specs=[pl.BlockSpec((tm, ZT), lambda i, j: (i, j)),
                   pl.BlockSpec((tm, D), lambda i, j: (i, 0))] + [hbm] * n,
        out_shape=[jax.ShapeDtypeStruct((T, NZ), f32), jax.ShapeDtypeStruct((T, D), bf16)]
        + _gathered_shapes(gather, axes),
        scratch_shapes=_gather_semaphores(n) if n else [],
        compiler_params=_params("arbitrary" if n else "parallel", "arbitrary"),
    )(x, g, w, *gather)


def _inproj_bwd_x(dz, w, x, dx1, g, tm):
    T = x.shape[0]

    def body(dz_ref, w_ref, x_ref, dx1_ref, g_ref, dx_ref, gg_ref, acc):
        i, k = pl.program_id(0), pl.program_id(1)

        @pl.when(k == 0)
        def _():
            acc[...] = jnp.zeros_like(acc)

        @pl.when((i == 0) & (k == 0))
        def _():
            gg_ref[...] = jnp.zeros_like(gg_ref)

        acc[...] += _mm_nt(dz_ref[...], w_ref[...])

        @pl.when(k == NZT - 1)
        def _():
            xv = x_ref[...]
            r = lax.rsqrt(jnp.mean(xv * xv, axis=-1, keepdims=True) + EPS)
            dh = acc[...]
            gg_ref[...] += jnp.sum(dh * xv * r, axis=0, keepdims=True)
            u = dh * g_ref[...]
            dx_ref[...] = dx1_ref[...] + r * u - xv * (r * r * r) * jnp.mean(u * xv, axis=-1, keepdims=True)

    return _pcall(
        body, name="inproj_bwd_x", grid=(T // tm, NZT),
        in_specs=[pl.BlockSpec((tm, ZT), lambda i, k: (i, k)),
                  pl.BlockSpec((D, ZT), lambda i, k: (0, k)),
                  pl.BlockSpec((tm, D), lambda i, k: (i, 0)),
                  pl.BlockSpec((tm, D), lambda i, k: (i, 0)),
                  pl.BlockSpec((1, D), lambda i, k: (0, 0))],
        out_specs=[pl.BlockSpec((tm, D), lambda i, k: (i, 0)),
                   pl.BlockSpec((1, D), lambda i, k: (0, 0))],
        out_shape=[jax.ShapeDtypeStruct((T, D), f32), jax.ShapeDtypeStruct((1, D), f32)],
        scratch_shapes=[pltpu.VMEM((tm, D), f32)],
        compiler_params=_params("arbitrary", "arbitrary"),
    )(dz, w, x, dx1, g)


def _inproj_bwd_w(h, dz, tm, li, buf):
    T = h.shape[0]
    SH = D // N_DEV
    nt = T // tm
    extra = [] if buf is None else [buf]

    def body(h_ref, dz_ref, *rest):
        gw_ref, acc = rest[len(extra):]

        @pl.when(pl.program_id(1) == 0)
        def _():
            acc[...] = jnp.zeros_like(acc)

        acc[...] += _mm_tn(h_ref[...], dz_ref[...])

        @pl.when(pl.program_id(1) == nt - 1)
        def _():
            gw_ref[...] = acc[...].reshape(N_DEV, SH, ZT).astype(bf16)

    return _pcall(
        body, name="inproj_bwd_w", grid=(NZT, nt),
        in_specs=[pl.BlockSpec((tm, D), lambda j, i: (i, 0)),
                  pl.BlockSpec((tm, ZT), lambda j, i: (i, j))] + [pl.BlockSpec(memory_space=pl.ANY)] * len(extra),
        out_specs=pl.BlockSpec((N_DEV, SH, ZT), lambda j, i: (0, li, j)),
        out_shape=jax.ShapeDtypeStruct((N_DEV, DEPTH * SH, NZ), bf16),
        scratch_shapes=[pltpu.VMEM((D, ZT), f32)],
        input_output_aliases={2: 0} if extra else {},
        compiler_params=_params("parallel", "arbitrary"),
    )(h, dz, *extra)


def _zblock(tm, col256):
    return pl.BlockSpec((tm, W), lambda i, c=col256: (i, c))


def _conv_taps(zc, halo, cw_ref, n):
    ext = jnp.concatenate([halo, zc], axis=0)
    z1 = pltpu.roll(ext, 1, 0)[8:]
    z2 = pltpu.roll(ext, 2, 0)[8:]
    return z1, z2


def _conv_fwd(z, cw, cb, tm):
    T = z.shape[0]
    c0 = OFF_CONV // W
    hb = tm // 8

    def body(ax_ref, ab_ref, ac_ref, ag_ref, hx_ref, hc_ref, cw_ref, cb_ref, y_ref):
        i = pl.program_id(0)
        zc = ac_ref[...] * ax_ref[...]
        halo = jnp.where(i > 0, hc_ref[...] * hx_ref[...], 0.0)
        z1, z2 = _conv_taps(zc, halo, cw_ref, tm)
        y = cw_ref[2:3, :] * zc + cw_ref[1:2, :] * z1 + cw_ref[0:1, :] * z2
        ya = ab_ref[...] * (y + cb_ref[...])
        y_ref[...] = (ya * _silu(ag_ref[...])).astype(bf16)

    halo_spec = lambda col: pl.BlockSpec((8, W), lambda i, c=col: (jnp.maximum(i * hb - 1, 0), c))
    return _pcall(
        body, name="conv_fwd", grid=(T // tm,),
        in_specs=[_zblock(tm, c0), _zblock(tm, c0 + 1), _zblock(tm, c0 + 2), _zblock(tm, c0 + 3),
                  halo_spec(c0), halo_spec(c0 + 2),
                  pl.BlockSpec((CONV_WIDTH, W), lambda i: (0, 0)),
                  pl.BlockSpec((1, W), lambda i: (0, 0))],
        out_specs=pl.BlockSpec((tm, W), lambda i: (i, 0)),
        out_shape=jax.ShapeDtypeStruct((T, W), bf16),
        compiler_params=_params("parallel"),
    )(z, z, z, z, z, z, cw, cb)


def _conv_bwd(z, dy, cw, cb, dzbuf, tm):
    T = z.shape[0]
    c0 = OFF_CONV // W
    hb = tm // 8
    nt = T // tm

    def body(ax_ref, ab_ref, ac_ref, ag_ref, hx_ref, hc_ref, nb_ref, ng_ref, dy_ref, ndy_ref,
             cw_ref, cb_ref, dzin_ref, dz_ref, gcw_ref, gcb_ref):
        i = pl.program_id(0)

        @pl.when(i == 0)
        def _():
            gcw_ref[...] = jnp.zeros_like(gcw_ref)
            gcb_ref[...] = jnp.zeros_like(gcb_ref)

        ax, ab, ac, ag = ax_ref[...], ab_ref[...], ac_ref[...], ag_ref[...]
        w0, w1, w2 = cw_ref[0:1, :], cw_ref[1:2, :], cw_ref[2:3, :]
        zc = ac * ax
        halo = jnp.where(i > 0, hc_ref[...] * hx_ref[...], 0.0)
        z1, z2 = _conv_taps(zc, halo, cw_ref, tm)
        yb = w2 * zc + w1 * z1 + w0 * z2 + cb_ref[...]
        ya = ab * yb
        dyg = dy_ref[...]
        dag = dyg * ya * _dsilu(ag)
        dya = dyg * _silu(ag)
        dab = dya * yb
        dyc = dya * ab
        nxt = jnp.where(i < nt - 1, ndy_ref[...] * _silu(ng_ref[...]) * nb_ref[...], 0.0)
        ext = jnp.concatenate([dyc, nxt], axis=0)
        d1 = pltpu.roll(ext, tm + 8 - 1, 0)[:tm]
        d2 = pltpu.roll(ext, tm + 8 - 2, 0)[:tm]
        dzc = w2 * dyc + w1 * d1 + w0 * d2
        dz_ref[:, 0:W] = (dzc * ac).astype(bf16)
        dz_ref[:, W:2 * W] = dab.astype(bf16)
        dz_ref[:, 2 * W:3 * W] = (dzc * ax).astype(bf16)
        dz_ref[:, 3 * W:4 * W] = dag.astype(bf16)
        gcb_ref[...] += jnp.sum(dyc, axis=0, keepdims=True)
        gcw_ref[...] += _rows3(jnp.sum(dyc * z2, axis=0, keepdims=True),
                               jnp.sum(dyc * z1, axis=0, keepdims=True),
                               jnp.sum(dyc * zc, axis=0, keepdims=True), W)

    prev_spec = lambda col: pl.BlockSpec((8, W), lambda i, c=col: (jnp.maximum(i * hb - 1, 0), c))
    next_z = lambda col: pl.BlockSpec((8, W), lambda i, c=col: (jnp.minimum((i + 1) * hb, T // 8 - 1), c))
    next_dy = pl.BlockSpec((8, W), lambda i: (jnp.minimum((i + 1) * hb, T // 8 - 1), 0))
    return _pcall(
        body, name="conv_bwd", grid=(nt,),
        in_specs=[_zblock(tm, c0), _zblock(tm, c0 + 1), _zblock(tm, c0 + 2), _zblock(tm, c0 + 3),
                  prev_spec(c0), prev_spec(c0 + 2), next_z(c0 + 1), next_z(c0 + 3),
                  pl.BlockSpec((tm, W), lambda i: (i, 0)), next_dy,
                  pl.BlockSpec((CONV_WIDTH, W), lambda i: (0, 0)),
                  pl.BlockSpec((1, W), lambda i: (0, 0)),
                  pl.BlockSpec(memory_space=pl.ANY)],
        out_specs=[pl.BlockSpec((tm, 4 * W), lambda i: (i, OFF_CONV // (4 * W))),
                   pl.BlockSpec((8, W), lambda i: (0, 0)),
                   pl.BlockSpec((1, W), lambda i: (0, 0))],
        out_shape=[jax.ShapeDtypeStruct((T, NZ), bf16), jax.ShapeDtypeStruct((8, W), f32),
                   jax.ShapeDtypeStruct((1, W), f32)],
        input_output_aliases={12: 0},
        compiler_params=_params("arbitrary"),
    )(z, z, z, z, z, z, z, z, dy, dy, cw, cb, dzbuf)


def _sgu_core(dv_ref, gv_ref, sw_ref, sbe_ref, s_scr, tm):
    v = dv_ref[...]
    gm = _group_mean_matrix(W, DH)
    rv = lax.rsqrt(_group_mean(v * v, gm) + EPS)
    vh = v * rv
    vnb = (vh * gv_ref[...]).astype(bf16)
    causal = _iota2((SGU_CHUNK, SGU_CHUNK), 0) >= _iota2((SGU_CHUNK, SGU_CHUNK), 1)
    wgs = [jnp.where(causal, sw_ref[g], 0.0).astype(bf16) for g in range(NH)]
    for c in range(tm // SGU_CHUNK):
        rows = slice(c * SGU_CHUNK, (c + 1) * SGU_CHUNK)
        for g in range(NH):
            cols = slice(g * DH, (g + 1) * DH)
            s_scr[rows, cols] = _mm(wgs[g], vnb[rows, cols])
    sb = sbe_ref[...]
    s = s_scr[...] + jnp.concatenate([sb] * (tm // SGU_CHUNK), axis=0)
    return v, rv, vh, vnb, wgs, causal, gm, s


def _sgu_fwd(z, gv, sw, sbe, tm):
    T = z.shape[0]
    c0 = OFF_SGU // W

    def body(du_ref, dv_ref, dg_ref, gv_ref, sw_ref, sbe_ref, y_ref, s_scr):
        s = _sgu_core(dv_ref, gv_ref, sw_ref, sbe_ref, s_scr, tm)[-1]
        y_ref[...] = ((du_ref[...] * s) * _silu(dg_ref[...])).astype(bf16)

    return _pcall(
        body, name="sgu_fwd", grid=(T // tm,),
        in_specs=[_zblock(tm, c0), _zblock(tm, c0 + 1), _zblock(tm, c0 + 2),
                  pl.BlockSpec((1, W), lambda i: (0, 0)),
                  pl.BlockSpec((NH, SGU_CHUNK, SGU_CHUNK), lambda i: (0, 0, 0)),
                  pl.BlockSpec((SGU_CHUNK, W), lambda i: (0, 0))],
        out_specs=pl.BlockSpec((tm, W), lambda i: (i, 0)),
        out_shape=jax.ShapeDtypeStruct((T, W), bf16),
        scratch_shapes=[pltpu.VMEM((tm, W), f32)],
        compiler_params=_params("parallel"),
    )(z, z, z, gv, sw, sbe)


def _sgu_bwd(z, dy, gv, sw, sbe, dzbuf, tm):
    T = z.shape[0]
    c0 = OFF_SGU // W
    nt = T // tm

    def body(du_ref, dv_ref, dg_ref, dy_ref, gv_ref, sw_ref, sbe_ref, dzin_ref,
             dz_ref, gsw_ref, gsb_ref, ggv_ref, s_scr, dvn_scr, sb_acc):
        i = pl.program_id(0)

        @pl.when(i == 0)
        def _():
            gsw_ref[...] = jnp.zeros_like(gsw_ref)
            ggv_ref[...] = jnp.zeros_like(ggv_ref)
            sb_acc[...] = jnp.zeros_like(sb_acc)

        v, rv, vh, vnb, wgs, causal, gm, s = _sgu_core(dv_ref, gv_ref, sw_ref, sbe_ref, s_scr, tm)
        du, dg, dyv = du_ref[...], dg_ref[...], dy_ref[...]
        ddg = dyv * (du * s) * _dsilu(dg)
        t = dyv * _silu(dg)
        ddu = t * s
        ds = t * du
        dsb = ds.astype(bf16)
        acc = sb_acc[...]
        for c in range(tm // SGU_CHUNK):
            rows = slice(c * SGU_CHUNK, (c + 1) * SGU_CHUNK)
            acc = acc + ds[rows, :]
            for g in range(NH):
                cols = slice(g * DH, (g + 1) * DH)
                gsw_ref[g] += jnp.where(causal, _mm_nt(dsb[rows, cols], vnb[rows, cols]), 0.0)
                dvn_scr[rows, cols] = _mm_tn(wgs[g], dsb[rows, cols])
        sb_acc[...] = acc
        dvn = dvn_scr[...]
        ggv_ref[...] += jnp.sum(dvn * vh, axis=0, keepdims=True)
        u = dvn * gv_ref[...]
        ddv = rv * u - v * (rv * rv * rv) * _group_mean(u * v, gm)
        dz_ref[:, 0:W] = ddu.astype(bf16)
        dz_ref[:, W:2 * W] = ddv.astype(bf16)
        dz_ref[:, 2 * W:3 * W] = ddg.astype(bf16)

        @pl.when(i == nt - 1)
        def _():
            gsb_ref[...] = _group_mean(sb_acc[...], gm) * float(DH)

    return _pcall(
        body, name="sgu_bwd", grid=(nt,),
        in_specs=[_zblock(tm, c0), _zblock(tm, c0 + 1), _zblock(tm, c0 + 2),
                  pl.BlockSpec((tm, W), lambda i: (i, 0)),
                  pl.BlockSpec((1, W), lambda i: (0, 0)),
                  pl.BlockSpec((NH, SGU_CHUNK, SGU_CHUNK), lambda i: (0, 0, 0)),
                  pl.BlockSpec((SGU_CHUNK, W), lambda i: (0, 0)),
                  pl.BlockSpec(memory_space=pl.ANY)],
        out_specs=[pl.BlockSpec((tm, 3 * W), lambda i: (i, OFF_SGU // (3 * W))),
                   pl.BlockSpec((NH, SGU_CHUNK, SGU_CHUNK), lambda i: (0, 0, 0)),
                   pl.BlockSpec((SGU_CHUNK, W), lambda i: (0, 0)),
                   pl.BlockSpec((1, W), lambda i: (0, 0))],
        out_shape=[jax.ShapeDtypeStruct((T, NZ), bf16),
                   jax.ShapeDtypeStruct((NH, SGU_CHUNK, SGU_CHUNK), f32),
                   jax.ShapeDtypeStruct((SGU_CHUNK, W), f32),
                   jax.ShapeDtypeStruct((1, W), f32)],
        scratch_shapes=[pltpu.VMEM((tm, W), f32), pltpu.VMEM((tm, W), f32), pltpu.VMEM((SGU_CHUNK, W), f32)],
        input_output_aliases={7: 0},
        compiler_params=_params("arbitrary"),
    )(z, z, z, dy, gv, sw, sbe, dzbuf)


def _hgrn_gates(cq_ref, cf_ref, lb_ref):
    q = _silu(cq_ref[...])
    sig = _sigmoid(cf_ref[...])
    lb = lb_ref[...]
    g = lb + (1.0 - lb) * sig
    return q, sig, g, jnp.log(g), (1.0 - lb) * (1.0 - sig)


def _hgrn_chunk_terms(lgc, qc, kc):
    C = GLA_CHUNK
    b = jnp.dot(_lower_tri(C), lgc, precision=HI, preferred_element_type=f32)
    bl = jnp.sum(lgc, axis=0, keepdims=True)
    mid = jnp.sum(jnp.where(_iota2((C, W), 0) <= C // 2, lgc, 0.0), axis=0, keepdims=True)
    eb = jnp.exp(b)
    em = jnp.exp(jnp.minimum(b - mid, EXP_CLAMP))
    emi = jnp.exp(jnp.minimum(mid - b, EXP_CLAMP))
    ek = jnp.exp(bl - b)
    return dict(eb=eb, em=em, emi=emi, ek=ek, ebl=jnp.exp(bl),
                qe=qc * eb, qm=qc * em, km=kc * emi, kd=kc * ek)


def _hgrn_fwd(z, lb, gain, tm):
    T = z.shape[0]
    c0 = OFF_HGRN // W
    C = GLA_CHUNK
    ncp = tm // C

    def body(cq_ref, cf_ref, ci_ref, cg_ref, lb_ref, gn_ref, y_ref, o_ref, st_ref, state, o_scr):
        @pl.when(pl.program_id(0) == 0)
        def _():
            state[...] = jnp.zeros_like(state)

        q, sig, g, lg, kf = _hgrn_gates(cq_ref, cf_ref, lb_ref)
        v = ci_ref[...]
        causal = _iota2((C, C), 0) >= _iota2((C, C), 1)
        for c in range(ncp):
            rows = slice(c * C, (c + 1) * C)
            tr = _hgrn_chunk_terms(lg[rows], q[rows], kf[rows])
            vb = v[rows].astype(bf16)
            qmb, kmb, qeb, kdb = (tr[n].astype(bf16) for n in ("qm", "km", "qe", "kd"))
            for h in range(NH):
                cols = slice(h * DH, (h + 1) * DH)
                hr = slice(h * DH, (h + 1) * DH)
                st = state[hr, :]
                st_ref[c, hr, :] = st
                p = jnp.where(causal, _mm_nt(qmb[:, cols], kmb[:, cols]), 0.0)
                o_scr[rows, cols] = _mm(p.astype(bf16), vb[:, cols]) + _mm_nt(qeb[:, cols], st.astype(bf16))
                state[hr, :] = st * tr["ebl"][:, cols] + _mm_tn(vb[:, cols], kdb[:, cols])
        o = o_scr[...]
        o_ref[...] = o
        gm = _group_mean_matrix(W, DH)
        r = lax.rsqrt(_group_mean(o * o, gm) + EPS)
        y_ref[...] = ((o * r * gn_ref[...]) * _silu(cg_ref[...])).astype(bf16)

    return _pcall(
        body, name="hgrn_fwd", grid=(T // tm,),
        in_specs=[_zblock(tm, c0), _zblock(tm, c0 + 1), _zblock(tm, c0 + 2), _zblock(tm, c0 + 3),
                  pl.BlockSpec((1, W), lambda i: (0, 0)), pl.BlockSpec((1, W), lambda i: (0, 0))],
        out_specs=[pl.BlockSpec((tm, W), lambda i: (i, 0)),
                   pl.BlockSpec((tm, W), lambda i: (i, 0)),
                   pl.BlockSpec((ncp, W, DH), lambda i: (i, 0, 0))],
        out_shape=[jax.ShapeDtypeStruct((T, W), bf16), jax.ShapeDtypeStruct((T, W), f32),
                   jax.ShapeDtypeStruct((T // C, W, DH), f32)],
        scratch_shapes=[pltpu.VMEM((W, DH), f32), pltpu.VMEM((tm, W), f32)],
        compiler_params=_params("arbitrary"),
    )(z, z, z, z, lb, gain)


def _hgrn_bwd(z, lb, gain, o_pre, states, dy, dzbuf, tm):
    T = z.shape[0]
    c0 = OFF_HGRN // W
    C = GLA_CHUNK
    ncp = tm // C
    nt = T // tm

    def body(cq_ref, cf_ref, ci_ref, cg_ref, lb_ref, gn_ref, o_ref, st_ref, dy_ref, dzin_ref,
             dz_ref, ggn_ref, glb_ref, dstate, dq_s, dk_s, dv_s, db_s):
        @pl.when(pl.program_id(0) == 0)
        def _():
            dstate[...] = jnp.zeros_like(dstate)
            ggn_ref[...] = jnp.zeros_like(ggn_ref)
            glb_ref[...] = jnp.zeros_like(glb_ref)

        cq, cg = cq_ref[...], cg_ref[...]
        q, sig, g, lg, kf = _hgrn_gates(cq_ref, cf_ref, lb_ref)
        lb = lb_ref[...]
        v = ci_ref[...]
        o = o_ref[...]
        gm = _group_mean_matrix(W, DH)
        r = lax.rsqrt(_group_mean(o * o, gm) + EPS)
        oh = o * r
        gn = gn_ref[...]
        dyv = dy_ref[...]
        dcg = dyv * (oh * gn) * _dsilu(cg)
        don = dyv * _silu(cg)
        ggn_ref[...] += jnp.sum(don * oh, axis=0, keepdims=True)
        u = don * gn
        do = r * u - o * (r * r * r) * _group_mean(u * o, gm)

        causal = _iota2((C, C), 0) >= _iota2((C, C), 1)
        last_row = _iota2((C, DH), 0) == C - 1
        for c in reversed(range(ncp)):
            rows = slice(c * C, (c + 1) * C)
            tr = _hgrn_chunk_terms(lg[rows], q[rows], kf[rows])
            vb = v[rows].astype(bf16)
            dob = do[rows].astype(bf16)
            qmb, kmb, qeb, kdb = (tr[n].astype(bf16) for n in ("qm", "km", "qe", "kd"))
            for h in range(NH):
                cols = slice(h * DH, (h + 1) * DH)
                hr = slice(h * DH, (h + 1) * DH)
                st0 = st_ref[c, hr, :]
                dst = dstate[hr, :]
                dstb = dst.astype(bf16)
                doh = dob[:, cols]
                p = jnp.where(causal, _mm_nt(qmb[:, cols], kmb[:, cols]), 0.0)
                dp = jnp.where(causal, _mm_nt(doh, vb[:, cols]), 0.0)
                dpb = dp.astype(bf16)
                dvh = _mm_tn(p.astype(bf16), doh) + _mm_nt(kdb[:, cols], dstb)
                dqm = _mm(dpb, kmb[:, cols])
                dkm = _mm_tn(dpb, qmb[:, cols])
                dqe = _mm(doh, st0.astype(bf16))
                dkd = _mm(vb[:, cols], dstb)
                ebl = tr["ebl"][:, cols]
                dstate[hr, :] = dst * ebl + _mm_tn(doh, qeb[:, cols])
                qm, km, qe, kd = (a[:, cols].astype(f32) for a in (qmb, kmb, qeb, kdb))
                kterm = dkd * kd
                dbh = dqm * qm - dkm * km + dqe * qe - kterm
                extra = jnp.sum(kterm, axis=0, keepdims=True) + ebl * jnp.sum(dst * st0, axis=0, keepdims=True)
                dbh = dbh + jnp.where(last_row, extra, 0.0)
                dq_s[rows, cols] = dqm * tr["em"][:, cols] + dqe * tr["eb"][:, cols]
                dk_s[rows, cols] = dkm * tr["emi"][:, cols] + dkd * tr["ek"][:, cols]
                dv_s[rows, cols] = dvh
                db_s[rows, cols] = dbh
            db_s[rows, :] = jnp.dot(_upper_tri(C), db_s[rows, :], precision=HI, preferred_element_type=f32)
        dlg = db_s[...]
        dk = dk_s[...]
        dsig = sig * (1.0 - sig)
        one_lb = 1.0 - lb
        dcf = (dlg / g - dk) * one_lb * dsig
        glb_ref[...] += jnp.sum((dlg / g - dk) * (1.0 - sig), axis=0, keepdims=True)
        dz_ref[:, 0:W] = (dq_s[...] * _dsilu(cq)).astype(bf16)
        dz_ref[:, W:2 * W] = dcf.astype(bf16)
        dz_ref[:, 2 * W:3 * W] = dv_s[...].astype(bf16)
        dz_ref[:, 3 * W:4 * W] = dcg.astype(bf16)

    rev = lambda i: nt - 1 - i
    zb = lambda col: pl.BlockSpec((tm, W), lambda i, c=col: (rev(i), c))
    return _pcall(
        body, name="hgrn_bwd", grid=(nt,),
        in_specs=[zb(c0), zb(c0 + 1), zb(c0 + 2), zb(c0 + 3),
                  pl.BlockSpec((1, W), lambda i: (0, 0)), pl.BlockSpec((1, W), lambda i: (0, 0)),
                  pl.BlockSpec((tm, W), lambda i: (rev(i), 0)),
                  pl.BlockSpec((ncp, W, DH), lambda i: (rev(i), 0, 0)),
                  pl.BlockSpec((tm, W), lambda i: (rev(i), 0)),
                  pl.BlockSpec(memory_space=pl.ANY)],
        out_specs=[pl.BlockSpec((tm, 4 * W), lambda i: (rev(i), OFF_HGRN // (4 * W))),
                   pl.BlockSpec((1, W), lambda i: (0, 0)),
                   pl.BlockSpec((1, W), lambda i: (0, 0))],
        out_shape=[jax.ShapeDtypeStruct((T, NZ), bf16), jax.ShapeDtypeStruct((1, W), f32),
                   jax.ShapeDtypeStruct((1, W), f32)],
        scratch_shapes=[pltpu.VMEM((W, DH), f32)] + [pltpu.VMEM((tm, W), f32)] * 4,
        input_output_aliases={9: 0},
        compiler_params=_params("arbitrary"),
    )(z, z, z, z, lb, gain, o_pre, states, dy, dzbuf)


def _attn_prep(z, fbias, gq, gk, tm):
    T = z.shape[0]
    c0 = OFF_ATT // W

    def body(q_ref, k_ref, v_ref, f_ref, fb_ref, gq_ref, gk_ref, qt_ref, kt_ref, vt_ref, kh_ref, vh_ref, cum_ref,
             carry):
        @pl.when(pl.program_id(0) == 0)
        def _():
            carry[...] = jnp.zeros_like(carry)

        gm = _group_mean_matrix(W, DH)
        q, k, v = q_ref[...], k_ref[...], v_ref[...]
        qs = q * lax.rsqrt(_group_mean(q * q, gm) + EPS) * (gq_ref[...] * (DH ** -0.5 * LOG2E))
        kn = k * lax.rsqrt(_group_mean(k * k, gm) + EPS) * gk_ref[...]
        qt_ref[...] = qs.T.astype(bf16)
        kt_ref[...] = kn.T.astype(bf16)
        vt_ref[...] = v.T.astype(bf16)
        for h in range(NH):
            cols = slice(h * DH, (h + 1) * DH)
            kh_ref[h] = kn[:, cols].astype(bf16)
            vh_ref[h] = v[:, cols].astype(bf16)
        ls = _logsigmoid(f_ref[...] + fb_ref[...])
        cum = jnp.dot(_lower_tri(tm), ls, precision=HI, preferred_element_type=f32) + carry[...]
        cum_ref[...] = cum * LOG2E
        carry[...] += jnp.sum(ls, axis=0, keepdims=True)

    hspec = pl.BlockSpec((NH, tm, DH), lambda i: (0, i, 0))
    tspec = pl.BlockSpec((W, tm), lambda i: (0, i))
    return _pcall(
        body, name="attn_prep", grid=(T // tm,),
        in_specs=[_zblock(tm, c0), _zblock(tm, c0 + 1), _zblock(tm, c0 + 2),
                  pl.BlockSpec((tm, 128), lambda i: (i, OFF_F // 128)),
                  pl.BlockSpec((1, 128), lambda i: (0, 0)),
                  pl.BlockSpec((1, W), lambda i: (0, 0)), pl.BlockSpec((1, W), lambda i: (0, 0))],
        out_specs=[tspec, tspec, tspec, hspec, hspec, pl.BlockSpec((tm, 128), lambda i: (i, 0))],
        out_shape=[jax.ShapeDtypeStruct((W, T), bf16)] * 3 + [jax.ShapeDtypeStruct((NH, T, DH), bf16)] * 2
        + [jax.ShapeDtypeStruct((T, 128), f32)],
        scratch_shapes=[pltpu.VMEM((1, 128), f32)],
        compiler_params=_params("arbitrary"),
    )(z, z, z, z, fbias, gq, gk)


HP = 2


def _causal_pairs(nq, key_major):
    if key_major:
        pairs = [(qi, ki) for ki in range(nq) for qi in range(ki, nq)]
    else:
        pairs = [(qi, ki) for qi in range(nq) for ki in range(qi + 1)]
    return (jnp.asarray([p[0] for p in pairs], jnp.int32), jnp.asarray([p[1] for p in pairs], jnp.int32))


def _head_rows(rows, n):
    return jnp.concatenate([jnp.broadcast_to(r, (DH, n)) for r in rows], axis=0)


def _attn_fwd(qt, kh, vt, crow, ccol, bq):
    T = qt.shape[1]
    nq = T // bq
    bk = bq
    qs, ks = _causal_pairs(nq, key_major=False)
    BW = HP * DH

    def body(qs_ref, ks_ref, qt_ref, k_ref, vt_ref, cr_ref, cc_ref, o_ref, lse_ref, m_s, l_s, acc_s):
        i = pl.program_id(1)
        qi, ki = qs_ref[i], ks_ref[i]

        @pl.when(ki == 0)
        def _():
            m_s[...] = jnp.full_like(m_s, MASK_VALUE)
            l_s[...] = jnp.zeros_like(l_s)
            acc_s[...] = jnp.zeros_like(acc_s)

        def step(diagonal):
            for h in range(HP):
                rows = slice(h * DH, (h + 1) * DH)
                s = _mm(k_ref[h], qt_ref[rows, :]) - cc_ref[h]
                if diagonal:
                    s = jnp.where(_iota2((bk, bq), 0) <= _iota2((bk, bq), 1), s, MASK_VALUE)
                cr = cr_ref[h]
                m_old = m_s[h]
                m_new = jnp.maximum(m_old, jnp.max(s, axis=0, keepdims=True) + cr)
                p = jnp.exp2(s + (cr - m_new))
                alpha = jnp.exp2(m_old - m_new)
                l_s[h] = alpha * l_s[h] + jnp.sum(p, axis=0, keepdims=True)
                acc_s[rows, :] = alpha * acc_s[rows, :] + _mm(vt_ref[rows, :], p.astype(bf16))
                m_s[h] = m_new

        @pl.when(ki < qi)
        def _():
            step(False)

        @pl.when(ki == qi)
        def _():
            step(True)
            o_ref[...] = (acc_s[...] / _head_rows([l_s[h] for h in range(HP)], bq)).T
            for h in range(HP):
                lse_ref[h] = m_s[h] + jnp.log(l_s[h]) * LOG2E

    qcol = lambda hp, i, qs, ks: (hp, qs[i])
    kcol = lambda hp, i, qs, ks: (hp, ks[i])
    qrow = lambda hp, i, qs, ks: (hp, 0, qs[i])
    return _pcall(
        body, name="attn_fwd",
        grid_spec=pltpu.PrefetchScalarGridSpec(
            num_scalar_prefetch=2, grid=(NH // HP, qs.shape[0]),
            in_specs=[pl.BlockSpec((BW, bq), qcol),
                      pl.BlockSpec((HP, bk, DH), lambda hp, i, qs, ks: (hp, ks[i], 0)),
                      pl.BlockSpec((BW, bk), kcol),
                      pl.BlockSpec((HP, 1, bq), qrow),
                      pl.BlockSpec((HP, bk, 1), lambda hp, i, qs, ks: (hp, ks[i], 0))],
            out_specs=[pl.BlockSpec((bq, BW), lambda hp, i, qs, ks: (qs[i], hp)),
                       pl.BlockSpec((HP, 1, bq), qrow)],
            scratch_shapes=[pltpu.VMEM((HP, 1, bq), f32), pltpu.VMEM((HP, 1, bq), f32),
                            pltpu.VMEM((BW, bq), f32)]),
        out_shape=[jax.ShapeDtypeStruct((T, W), f32), jax.ShapeDtypeStruct((NH, 1, T), f32)],
        compiler_params=_params("parallel", "arbitrary"),
    )(qs, ks, qt, kh, vt, crow, ccol)


def _attn_bwd_prep(dy, oh, z, tm):
    T = dy.shape[0]
    cg = OFF_ATT // W + 3

    def body(dy_ref, o_ref, g_ref, dot_ref, dl_ref):
        do = (dy_ref[...] * _silu(g_ref[...])).astype(bf16)
        dot_ref[...] = do.astype(f32).T.astype(bf16)
        prod = (do.astype(f32) * o_ref[...]).T
        for h in range(NH):
            dl_ref[h] = jnp.sum(prod[h * DH:(h + 1) * DH, :], axis=0, keepdims=True)

    return _pcall(
        body, name="attn_bwd_prep", grid=(T // tm,),
        in_specs=[pl.BlockSpec((tm, W), lambda i: (i, 0)),
                  pl.BlockSpec((tm, W), lambda i: (i, 0)),
                  _zblock(tm, cg)],
        out_specs=[pl.BlockSpec((W, tm), lambda i: (0, i)),
                   pl.BlockSpec((NH, 1, tm), lambda i: (0, 0, i))],
        out_shape=[jax.ShapeDtypeStruct((W, T), bf16), jax.ShapeDtypeStruct((NH, 1, T), f32)],
        compiler_params=_params("parallel"),
    )(dy, oh, z)


def _attn_bwd(qt, kt, kh, vh, crow, ccol, dot, lse, delta, bq):
    T = qt.shape[1]
    nq = T // bq
    bk = bq
    qs, ks = _causal_pairs(nq, key_major=True)
    BW = HP * DH

    def body(qs_ref, ks_ref, qt_ref, kt_ref, k_ref, v_ref, cr_ref, cc_ref, dot_ref, lse_ref, dl_ref,
             dq_ref, dk_ref, dv_ref, dck_ref, dcq_ref, dq_s, dk_s, dv_s, dck_s):
        i = pl.program_id(1)
        qi, ki = qs_ref[i], ks_ref[i]

        @pl.when(i == 0)
        def _():
            dq_s[...] = jnp.zeros_like(dq_s)
            dcq_ref[...] = jnp.zeros_like(dcq_ref)

        @pl.when(qi == ki)
        def _():
            dk_s[...] = jnp.zeros_like(dk_s)
            dv_s[...] = jnp.zeros_like(dv_s)
            dck_s[...] = jnp.zeros_like(dck_s)

        def step(diagonal):
            colsums = []
            for h in range(HP):
                rows = slice(h * DH, (h + 1) * DH)
                qth, doth = qt_ref[rows, :], dot_ref[rows, :]
                p = jnp.exp2(_mm(k_ref[h], qth) + (cr_ref[h] - lse_ref[h]) - cc_ref[h])
                if diagonal:
                    p = jnp.where(_iota2((bk, bq), 0) <= _iota2((bk, bq), 1), p, 0.0)
                dv_s[rows, :] += _mm_nt(doth, p.astype(bf16))
                ds = p * (_mm(v_ref[h], doth) - dl_ref[h])
                dsb = ds.astype(bf16)
                dk_s[rows, :] += _mm_nt(qth, dsb)
                dq_s[qi, rows, :] += _mm(kt_ref[rows, :], dsb)
                part = ds[:, 0:128]
                for c in range(1, bq // 128):
                    part = part + ds[:, c * 128:(c + 1) * 128]
                dck_s[h] += part
                colsums.append(jnp.sum(ds, axis=0, keepdims=True))
            dcq_ref[qi] += _stack_rows(colsums, bq)

        @pl.when(qi > ki)
        def _():
            step(False)

        @pl.when(qi == ki)
        def _():
            step(True)

        @pl.when(qi == nq - 1)
        def _():
            dk_ref[...] = (dk_s[...] * (1.0 / LOG2E)).T
            dv_ref[...] = dv_s[...].T
            lane = _iota2((bk, 128), 1)
            out = jnp.zeros((bk, 128), f32)
            for h in range(HP):
                out = out - jnp.where(lane == pl.program_id(0) * HP + h,
                                      jnp.sum(dck_s[h], axis=1, keepdims=True), 0.0)
            dck_ref[...] = out

        @pl.when(i == qs.shape[0] - 1)
        def _():
            for qb in range(nq):
                dq_ref[qb * bq:(qb + 1) * bq, :] = dq_s[qb].T

    qcol = lambda hp, i, qs, ks: (hp, qs[i])
    kcol = lambda hp, i, qs, ks: (hp, ks[i])
    qrow = lambda hp, i, qs, ks: (hp, 0, qs[i])
    kh_spec = pl.BlockSpec((HP, bk, DH), lambda hp, i, qs, ks: (hp, ks[i], 0))
    return _pcall(
        body, name="attn_bwd",
        grid_spec=pltpu.PrefetchScalarGridSpec(
            num_scalar_prefetch=2, grid=(NH // HP, qs.shape[0]),
            in_specs=[pl.BlockSpec((BW, bq), qcol), pl.BlockSpec((BW, bk), kcol), kh_spec, kh_spec,
                      pl.BlockSpec((HP, 1, bq), qrow),
                      pl.BlockSpec((HP, bk, 1), lambda hp, i, qs, ks: (hp, ks[i], 0)),
                      pl.BlockSpec((BW, bq), qcol), pl.BlockSpec((HP, 1, bq), qrow), pl.BlockSpec((HP, 1, bq), qrow)],
            out_specs=[pl.BlockSpec((T, BW), lambda hp, i, qs, ks: (0, hp)),
                       pl.BlockSpec((bk, BW), lambda hp, i, qs, ks: (ks[i], hp)),
                       pl.BlockSpec((bk, BW), lambda hp, i, qs, ks: (ks[i], hp)),
                       pl.BlockSpec((None, bk, 128), lambda hp, i, qs, ks: (hp, ks[i], 0)),
                       pl.BlockSpec((None, nq, 8, bq), lambda hp, i, qs, ks: (hp, 0, 0, 0))],
            scratch_shapes=[pltpu.VMEM((nq, BW, bq), f32), pltpu.VMEM((BW, bk), f32), pltpu.VMEM((BW, bk), f32),
                            pltpu.VMEM((HP, bk, 128), f32)]),
        out_shape=[jax.ShapeDtypeStruct((T, W), f32)] * 3 + [jax.ShapeDtypeStruct((NH // HP, T, 128), f32),
                                                             jax.ShapeDtypeStruct((NH // HP, nq, 8, bq), f32)],
        compiler_params=_params("parallel", "arbitrary"),
    )(qs, ks, qt, kt, kh, vh, crow, ccol, dot, lse, delta)


def _attn_post(z, dy, oh, dqh, dkh, dvh, dck, dcq, fbias, gq, gk, dzbuf, tm):
    T = z.shape[0]
    c0 = OFF_ATT // W
    nt = T // tm

    def body(q_ref, k_ref, g_ref, f_ref, dy_ref, o_ref, dq_ref, dk_ref, dv_ref, dck_ref, dcq_ref, fb_ref, gq_ref,
             gk_ref, dzin_ref, dz_ref, ggq_ref, ggk_ref, gfb_ref, carry):
        @pl.when(pl.program_id(0) == 0)
        def _():
            carry[...] = jnp.zeros_like(carry)
            ggq_ref[...] = jnp.zeros_like(ggq_ref)
            ggk_ref[...] = jnp.zeros_like(ggk_ref)
            gfb_ref[...] = jnp.zeros_like(gfb_ref)

        gm = _group_mean_matrix(W, DH)
        hs = jnp.where((_iota2((W, W), 0) & (DH - 1)) == (_iota2((W, W), 1) & (DH - 1)), 1.0, 0.0).astype(f32)

        def norm_bwd(x, dn, gain):
            r = lax.rsqrt(_group_mean(x * x, gm) + EPS)
            gg = jnp.sum(dn * x * r, axis=0, keepdims=True)
            u = dn * gain
            return r * u - x * (r * r * r) * _group_mean(u * x, gm), gg

        q, k, gate = q_ref[...], k_ref[...], g_ref[...]
        dq, ggq = norm_bwd(q, dq_ref[...] * (DH ** -0.5), gq_ref[...])
        dk, ggk = norm_bwd(k, dk_ref[...], gk_ref[...])
        ggq_ref[...] += jnp.dot(jnp.broadcast_to(ggq, (8, W)), hs, precision=HI, preferred_element_type=f32)[0:1]
        ggk_ref[...] += jnp.dot(jnp.broadcast_to(ggk, (8, W)), hs, precision=HI, preferred_element_type=f32)[0:1]
        dgate = dy_ref[...] * o_ref[...] * _dsilu(gate)
        dck_v = dcq_ref[...]
        for hp in range(NH // HP):
            dck_v = dck_v + dck_ref[hp]
        rc = jnp.dot(_upper_tri(tm), dck_v, precision=HI, preferred_element_type=f32) + carry[...]
        carry[...] += jnp.sum(dck_v, axis=0, keepdims=True)
        f = f_ref[...] + fb_ref[...]
        df = jnp.where(_iota2((tm, 128), 1) < NH, rc * _sigmoid(-f), 0.0)
        gfb_ref[...] += jnp.sum(df, axis=0, keepdims=True)
        dz_ref[:, 0:W] = dq.astype(bf16)
        dz_ref[:, W:2 * W] = dk.astype(bf16)
        dz_ref[:, 2 * W:3 * W] = dv_ref[...].astype(bf16)
        dz_ref[:, 3 * W:4 * W] = dgate.astype(bf16)
        dz_ref[:, 4 * W:4 * W + 128] = df.astype(bf16)

    rev = lambda i: nt - 1 - i
    zb = lambda col: pl.BlockSpec((tm, W), lambda i, c=col: (rev(i), c))
    hspec = pl.BlockSpec((tm, W), lambda i: (rev(i), 0))
    return _pcall(
        body, name="attn_post", grid=(nt,),
        in_specs=[zb(c0), zb(c0 + 1), zb(c0 + 3),
                  pl.BlockSpec((tm, 128), lambda i: (rev(i), OFF_F // 128)),
                  pl.BlockSpec((tm, W), lambda i: (rev(i), 0)),
                  hspec, hspec, hspec, hspec,
                  pl.BlockSpec((NH // HP, tm, 128), lambda i: (0, rev(i), 0)),
                  pl.BlockSpec((tm, 128), lambda i: (rev(i), 0)),
                  pl.BlockSpec((1, 128), lambda i: (0, 0)),
                  pl.BlockSpec((1, W), lambda i: (0, 0)), pl.BlockSpec((1, W), lambda i: (0, 0)),
                  pl.BlockSpec(memory_space=pl.ANY)],
        out_specs=[pl.BlockSpec((tm, 4 * W + 128), lambda i: (rev(i), OFF_ATT // (4 * W + 128))),
                   pl.BlockSpec((1, W), lambda i: (0, 0)), pl.BlockSpec((1, W), lambda i: (0, 0)),
                   pl.BlockSpec((1, 128), lambda i: (0, 0))],
        out_shape=[jax.ShapeDtypeStruct((T, NZ), bf16), jax.ShapeDtypeStruct((1, W), f32),
                   jax.ShapeDtypeStruct((1, W), f32), jax.ShapeDtypeStruct((1, 128), f32)],
        scratch_shapes=[pltpu.VMEM((1, 128), f32)],
        input_output_aliases={14: 0},
        compiler_params=_params("arbitrary"),
    )(z, z, z, z, dy, oh, dqh, dkh, dvh, dck, dcq, fbias, gq, gk, dzbuf)


def _merge_fwd(ya, oh, z, yc, yd, mb, x, p, wup, wo, gp, wpg, wpp, tm):
    T = x.shape[0]
    cg = OFF_ATT // W + 3

    def body(ya_ref, oh_ref, bg_ref, yc_ref, yd_ref, ml_ref, mb_ref, x_ref, p_ref, wup_ref, wo_ref, gp_ref,
             wpg_ref, wpp_ref, yb_ref, mg_ref, x1_ref, x2_ref):
        yb = (oh_ref[...] * _silu(bg_ref[...])).astype(bf16)
        yb_ref[...] = yb
        ys = (ya_ref[...], yb, yc_ref[...], yd_ref[...])
        merged = jnp.zeros((tm, D), f32)
        for b in range(NBR):
            sg = _sigmoid(ml_ref[:, b * D:(b + 1) * D] + mb_ref[b:b + 1, :])
            merged = merged + sg * _mm(ys[b], wup_ref[b])
        mgb = merged.astype(bf16)
        mg_ref[...] = mgb
        x1 = x_ref[...] + _mm(mgb, wo_ref[...])
        x1_ref[...] = x1
        r = lax.rsqrt(jnp.mean(x1 * x1, axis=-1, keepdims=True) + EPS)
        hp = (x1 * r * gp_ref[...]).astype(bf16)
        gate = _sigmoid(_mm(hp, wpg_ref[...]))
        x2_ref[...] = x1 + gate * _mm(p_ref[...].astype(bf16), wpp_ref[...])

    row = lambda width: pl.BlockSpec((tm, width), lambda i: (i, 0))
    full = lambda *shape: pl.BlockSpec(shape, lambda i: (0,) * len(shape))
    return _pcall(
        body, name="merge_fwd", grid=(T // tm,),
        in_specs=[row(W), row(W), _zblock(tm, cg), row(W), row(W),
                  pl.BlockSpec((tm, NBR * D), lambda i: (i, 0)), full(NBR, D), row(D), row(PLE),
                  full(NBR, W, D), full(D, D), full(1, D), full(D, D), full(PLE, D)],
        out_specs=[row(W), row(D), row(D), row(D)],
        out_shape=[jax.ShapeDtypeStruct((T, W), bf16), jax.ShapeDtypeStruct((T, D), bf16),
                   jax.ShapeDtypeStruct((T, D), f32), jax.ShapeDtypeStruct((T, D), f32)],
        compiler_params=_params("parallel"),
    )(ya, oh, z, yc, yd, z, mb, x, p, wup, wo, gp, wpg, wpp)


def _layer_slabs(li, bufs):
    if bufs is None:
        return [], []
    return list(bufs), [pl.BlockSpec(memory_space=pl.ANY)] * len(bufs)


def _ple_bwd(dx2, x1, p, gp, wpg, wpp, tm, li, bufs):
    T = x1.shape[0]
    SH = D // N_DEV
    nt = T // tm
    extra, extra_specs = _layer_slabs(li, bufs)

    def body(dx2_ref, x1_ref, p_ref, gp_ref, wpg_ref, wpp_ref, *rest):
        dx1_ref, gwpg_ref, gwpp_ref, ggp_ref, gwpg_acc, gwpp_acc = rest[len(extra):]

        @pl.when(pl.program_id(0) == 0)
        def _():
            gwpg_acc[...] = jnp.zeros_like(gwpg_acc)
            gwpp_acc[...] = jnp.zeros_like(gwpp_acc)
            ggp_ref[...] = jnp.zeros_like(ggp_ref)

        x1, dx2 = x1_ref[...], dx2_ref[...]
        r = lax.rsqrt(jnp.mean(x1 * x1, axis=-1, keepdims=True) + EPS)
        xh = x1 * r
        gp = gp_ref[...]
        hp = (xh * gp).astype(bf16)
        gate = _sigmoid(_mm(hp, wpg_ref[...]))
        pb = p_ref[...].astype(bf16)
        pp = _mm(pb, wpp_ref[...])
        dpre = ((dx2 * pp) * gate * (1.0 - gate)).astype(bf16)
        gwpp_acc[...] += _mm_tn(pb, (dx2 * gate).astype(bf16))
        gwpg_acc[...] += _mm_tn(hp, dpre)
        dhp = _mm_nt(dpre, wpg_ref[...])
        ggp_ref[...] += jnp.sum(dhp * xh, axis=0, keepdims=True)
        u = dhp * gp
        dx1_ref[...] = dx2 + r * u - x1 * (r * r * r) * jnp.mean(u * x1, axis=-1, keepdims=True)

        @pl.when(pl.program_id(0) == nt - 1)
        def _():
            gwpg_ref[...] = gwpg_acc[...].reshape(N_DEV, SH, D).astype(bf16)
            for d in range(N_DEV):
                gwpp_ref[d] = gwpp_acc[:, d * SH:(d + 1) * SH].astype(bf16)

    row = lambda width: pl.BlockSpec((tm, width), lambda i: (i, 0))
    full = lambda *shape: pl.BlockSpec(shape, lambda i: (0,) * len(shape))
    n_in = 6
    return _pcall(
        body, name="ple_bwd", grid=(nt,),
        in_specs=[row(D), row(D), row(PLE), full(1, D), full(D, D), full(PLE, D)] + extra_specs,
        out_specs=[row(D), pl.BlockSpec((N_DEV, SH, D), lambda i: (0, li, 0)),
                   pl.BlockSpec((N_DEV, PLE, SH), lambda i: (0, li, 0)), full(1, D)],
        out_shape=[jax.ShapeDtypeStruct((T, D), f32), jax.ShapeDtypeStruct((N_DEV, DEPTH * SH, D), bf16),
                   jax.ShapeDtypeStruct((N_DEV, DEPTH * PLE, SH), bf16), jax.ShapeDtypeStruct((1, D), f32)],
        scratch_shapes=[pltpu.VMEM((D, D), f32), pltpu.VMEM((PLE, D), f32)],
        input_output_aliases={n_in + k: 1 + k for k in range(len(extra))},
        compiler_params=_params("arbitrary"),
    )(dx2, x1, p, gp, wpg, wpp, *extra)


def _merge_bwd(dx1, mg, ya, yb, yc, yd, z, mb, wup, wo, tm, li, bufs):
    T = dx1.shape[0]
    SH = D // N_DEV
    nt = T // tm
    extra, extra_specs = _layer_slabs(li, bufs)

    def body(dx1_ref, mg_ref, ya_ref, yb_ref, yc_ref, yd_ref, ml_ref, mb_ref, wup_ref, wo_ref, *rest):
        dml_ref, dya_ref, dyb_ref, dyc_ref, dyd_ref, gwo_ref, gwup_ref, gmb_ref, gwo_acc, gwup_acc = rest[len(extra):]

        @pl.when(pl.program_id(0) == 0)
        def _():
            gwo_acc[...] = jnp.zeros_like(gwo_acc)
            gwup_acc[...] = jnp.zeros_like(gwup_acc)
            gmb_ref[...] = jnp.zeros_like(gmb_ref)

        dx1b = dx1_ref[...].astype(bf16)
        gwo_acc[...] += _mm_tn(mg_ref[...], dx1b)
        dm = _mm_nt(dx1b, wo_ref[...])
        ys = (ya_ref, yb_ref, yc_ref, yd_ref)
        dys = (dya_ref, dyb_ref, dyc_ref, dyd_ref)
        for b in range(NBR):
            y = ys[b][...]
            up = _mm(y, wup_ref[b])
            sg = _sigmoid(ml_ref[:, b * D:(b + 1) * D] + mb_ref[b:b + 1, :])
            dup = (dm * sg).astype(bf16)
            dml = dm * up * sg * (1.0 - sg)
            gmb_ref[b:b + 1, :] += jnp.sum(dml, axis=0, keepdims=True)
            dml_ref[:, b * D:(b + 1) * D] = dml.astype(bf16)
            gwup_acc[b] += _mm_tn(y, dup)
            dys[b][...] = _mm_nt(dup, wup_ref[b])

        @pl.when(pl.program_id(0) == nt - 1)
        def _():
            gwo_ref[...] = gwo_acc[...].reshape(N_DEV, SH, D).astype(bf16)
            for d in range(N_DEV):
                gwup_ref[d] = gwup_acc[:, :, d * SH:(d + 1) * SH].reshape(NBR * W, SH).astype(bf16)

    row = lambda width: pl.BlockSpec((tm, width), lambda i: (i, 0))
    full = lambda *shape: pl.BlockSpec(shape, lambda i: (0,) * len(shape))
    n_in = 10
    return _pcall(
        body, name="merge_bwd", grid=(nt,),
        in_specs=[row(D), row(D), row(W), row(W), row(W), row(W), row(NBR * D), full(NBR, D),
                  full(NBR, W, D), full(D, D)] + extra_specs,
        out_specs=[row(NBR * D), row(W), row(W), row(W), row(W),
                   pl.BlockSpec((N_DEV, SH, D), lambda i: (0, li, 0)),
                   pl.BlockSpec((N_DEV, NBR * W, SH), lambda i: (0, li, 0)), full(NBR, D)],
        out_shape=[jax.ShapeDtypeStruct((T, NZ), bf16)] + [jax.ShapeDtypeStruct((T, W), f32)] * 4
        + [jax.ShapeDtypeStruct((N_DEV, DEPTH * SH, D), bf16),
           jax.ShapeDtypeStruct((N_DEV, DEPTH * NBR * W, SH), bf16),
           jax.ShapeDtypeStruct((NBR, D), f32)],
        scratch_shapes=[pltpu.VMEM((D, D), f32), pltpu.VMEM((NBR, W, D), f32)],
        input_output_aliases={n_in + k: 5 + k for k in range(len(extra))},
        compiler_params=_params("arbitrary"),
    )(dx1, mg, ya, yb, yc, yd, z, mb, wup, wo, *extra)


def _loss_head(y, target, tm):
    T = y.shape[0]

    def body(y_ref, t_ref, loss_ref, dy_ref, acc):
        i = pl.program_id(0)

        @pl.when(i == 0)
        def _():
            acc[...] = jnp.zeros_like(acc)

        e = y_ref[...] - t_ref[...]
        dy_ref[...] = e * (1.0 / D)
        acc[...] += jnp.sum(e * e, axis=0, keepdims=True)

        @pl.when(i == T // tm - 1)
        def _():
            loss_ref[...] = jnp.sum(acc[...], axis=1, keepdims=True) * (0.5 / D)

    return _pcall(
        body, name="loss_head", grid=(T // tm,),
        in_specs=[pl.BlockSpec((tm, D), lambda i: (i, 0)), pl.BlockSpec((tm, D), lambda i: (i, 0))],
        out_specs=[pl.BlockSpec((1, 1), lambda i: (0, 0)), pl.BlockSpec((tm, D), lambda i: (i, 0))],
        out_shape=[jax.ShapeDtypeStruct((1, 1), f32), jax.ShapeDtypeStruct((T, D), f32)],
        scratch_shapes=[pltpu.VMEM((1, D), f32)],
        compiler_params=_params("arbitrary"),
    )(y, target)


def _lb_softmax_rows(l_ref):
    rows = [l_ref[i:i + 1, :] for i in range(DEPTH)]
    m = rows[0]
    for r in rows[1:]:
        m = jnp.maximum(m, r)
    es = [jnp.exp(r - m) for r in rows]
    tot = es[0]
    for e in es[1:]:
        tot = tot + e
    return [e / tot for e in es]


def _lb_partial_sums(pr):
    sums = [jnp.zeros_like(pr[0])]
    for i in range(1, DEPTH):
        sums.append(sums[-1] + pr[i])
    return sums


def _stack_rows(rows, width):
    idx = _iota2((8, width), 0)
    out = jnp.zeros((8, width), f32)
    for i, r in enumerate(rows):
        out = jnp.where(idx == i, r, out)
    return out


def _lower_bounds(lb_logits):
    def body(l_ref, o_ref):
        sums = _lb_partial_sums(_lb_softmax_rows(l_ref))
        o_ref[...] = _stack_rows([jnp.clip(s, 0.0, 1.0) for s in sums], W)

    return _pcall(body, name="lower_bounds", out_shape=jax.ShapeDtypeStruct((8, W), f32))(lb_logits)


def _lower_bounds_bwd(lb_logits, dlower):
    def body(l_ref, d_ref, o_ref):
        pr = _lb_softmax_rows(l_ref)
        sums = _lb_partial_sums(pr)
        dl = [jnp.where((sums[i] > 0.0) & (sums[i] < 1.0), d_ref[i:i + 1, :], 0.0) for i in range(DEPTH)]
        dp = [jnp.zeros_like(pr[0])] * DEPTH
        run = jnp.zeros_like(pr[0])
        for j in reversed(range(1, DEPTH)):
            run = run + dl[j]
            dp[j] = run
        inner = pr[0] * dp[0]
        for j in range(1, DEPTH):
            inner = inner + pr[j] * dp[j]
        o_ref[...] = _stack_rows([pr[j] * (dp[j] - inner) for j in range(DEPTH)], W)

    return _pcall(body, name="lower_bounds_bwd", out_shape=jax.ShapeDtypeStruct((8, W), f32))(lb_logits, dlower)


def _row_tile(rows, cols, budget_bytes=1 << 20, mult=8):
    if rows % mult:
        return rows
    best = mult
    for t in range(mult, rows + 1, mult):
        if rows % t == 0 and t * cols * 4 <= budget_bytes:
            best = t
    return best


def _sum_slabs(land):
    N, R, C = land.shape
    tr = _row_tile(R, C * N, mult=16)

    def body(l_ref, o_ref):
        acc = l_ref[0].astype(f32)
        for j in range(1, N):
            acc = acc + l_ref[j].astype(f32)
        o_ref[...] = acc

    return _pcall(
        body, name="sum_slabs", grid=(R // tr,),
        in_specs=[pl.BlockSpec((N, tr, C), lambda i: (0, i, 0))],
        out_specs=pl.BlockSpec((tr, C), lambda i: (i, 0)),
        out_shape=jax.ShapeDtypeStruct((R, C), f32),
        compiler_params=_params("parallel"),
    )(land)


def _adamw_update(w_ref, g_ref, m_ref, v_ref, d_ref, nm_ref, nv_ref):
    c1 = 1.0 / (1.0 - ADAM_B1 ** ADAM_STEP)
    c2 = 1.0 / (1.0 - ADAM_B2 ** ADAM_STEP)
    gv = g_ref[...]
    nm = ADAM_B1 * m_ref[...] + (1.0 - ADAM_B1) * gv
    nv = ADAM_B2 * v_ref[...] + (1.0 - ADAM_B2) * (gv * gv)
    nm_ref[...] = nm
    nv_ref[...] = nv
    d_ref[...] = -ADAM_LR * ((nm * c1) / (jnp.sqrt(nv * c2) + ADAM_EPS) + ADAM_WD * w_ref[...])


def _adamw3(w, g, m, v):
    L, R, C = w.shape
    tr = _row_tile(R, C)

    def body(*refs):
        _adamw_update(*refs)

    spec = pl.BlockSpec((None, tr, C), lambda l, i: (l, i, 0))
    return _pcall(
        body, name="adamw3", grid=(L, R // tr),
        in_specs=[spec] * 4, out_specs=[spec] * 3,
        out_shape=[jax.ShapeDtypeStruct((L, R, C), f32)] * 3,
        compiler_params=_params("parallel", "parallel"),
    )(w, g, m, v)


def _adamw(w, g, m, v):
    if w.ndim == 3:
        return _adamw3(w, g, m, v)
    R, C = w.shape
    tr = _row_tile(R, C)
    c1 = 1.0 / (1.0 - ADAM_B1 ** ADAM_STEP)
    c2 = 1.0 / (1.0 - ADAM_B2 ** ADAM_STEP)

    def body(w_ref, g_ref, m_ref, v_ref, d_ref, nm_ref, nv_ref):
        gv = g_ref[...]
        nm = ADAM_B1 * m_ref[...] + (1.0 - ADAM_B1) * gv
        nv = ADAM_B2 * v_ref[...] + (1.0 - ADAM_B2) * (gv * gv)
        nm_ref[...] = nm
        nv_ref[...] = nv
        d_ref[...] = -ADAM_LR * ((nm * c1) / (jnp.sqrt(nv * c2) + ADAM_EPS) + ADAM_WD * w_ref[...])

    spec = pl.BlockSpec((tr, C), lambda i: (i, 0))
    return _pcall(
        body, name="adamw", grid=(R // tr,),
        in_specs=[spec] * 4, out_specs=[spec] * 3,
        out_shape=[jax.ShapeDtypeStruct((R, C), f32)] * 3,
        compiler_params=_params("parallel"),
    )(w, g, m, v)


def _my_id():
    return lax.axis_index("x") * 4 + lax.axis_index("y") * 2 + lax.axis_index("c")


def _peer(k):
    x, y, c = lax.axis_index("x"), lax.axis_index("y"), lax.axis_index("c")
    kx, ky, kc = (k >> 2) & 1, (k >> 1) & 1, k & 1
    px, py, pc = x ^ kx, y ^ ky, c ^ kc
    return (px, py, pc), px * 4 + py * 2 + pc


def _all_gather(shards, axes):
    n = len(shards)

    def body(*refs):
        start, forward, finish = _gather_phases(shards, axes, refs[:n], refs[n:2 * n], *refs[2 * n:])
        start()
        forward()
        finish()

    hbm = pl.BlockSpec(memory_space=pltpu.HBM)
    return _pcall(
        body, name="all_gather",
        in_specs=[hbm] * n, out_specs=[hbm] * n,
        out_shape=_gathered_shapes(shards, axes),
        scratch_shapes=_gather_semaphores(n),
    )(*shards)


def _gathered_shapes(shards, axes):
    def full_shape(s, ax):
        shp = list(s.shape)
        shp[ax] *= N_DEV
        return tuple(shp)

    return [jax.ShapeDtypeStruct(full_shape(s, ax), s.dtype) for s, ax in zip(shards, axes)]


def _gather_semaphores(n):
    return [pltpu.SemaphoreType.DMA((n, N_DEV - 1)), pltpu.SemaphoreType.DMA((n, N_DEV - 1)),
            pltpu.SemaphoreType.DMA((n,))]


def _gather_phases(shards, axes, srcs, outs, send_sems, recv_sems, local_sems):
    n = len(shards)
    x, y, c = lax.axis_index("x"), lax.axis_index("y"), lax.axis_index("c")
    me, sibling = (x, y, c), (x, y, 1 - c)
    chips = [(1 - x, y), (x, 1 - y), (1 - x, 1 - y)]

    def block(a, dev):
        j = dev[0] * 4 + dev[1] * 2 + dev[2]
        size = shards[a].shape[axes[a]]
        start = pl.multiple_of(j * size, size)
        if axes[a] == 0:
            return outs[a].at[pl.ds(start, size), :]
        if axes[a] == 1:
            return outs[a].at[:, pl.ds(start, size), :]
        return outs[a].at[:, pl.ds(start, size)]

    def copy(a, k, dev, to, src=None):
        return pltpu.make_async_remote_copy(
            src_ref=block(a, dev) if src is None else src, dst_ref=block(a, dev),
            send_sem=send_sems.at[a, k], recv_sem=recv_sems.at[a, k],
            device_id=to, device_id_type=pl.DeviceIdType.MESH)

    def mine():
        return [pltpu.make_async_copy(srcs[a], block(a, me), local_sems.at[a]) for a in range(n)]

    def first():
        cps = []
        for a in range(n):
            cps.append(copy(a, 0, me, sibling, src=srcs[a]))
            cps += [copy(a, 1 + j, me, (*chip, c), src=srcs[a]) for j, chip in enumerate(chips)]
        return cps

    def passed():
        return [copy(a, 4 + j, (*chip, c), sibling) for j, chip in enumerate(chips) for a in range(n)]

    def start():
        for cp in mine() + first():
            cp.start()

    def forward():
        for j, chip in enumerate(chips):
            for a in range(n):
                copy(a, 1 + j, (*chip, c), me).wait_recv()
                copy(a, 4 + j, (*chip, c), sibling).start()

    def finish():
        for a in range(n):
            copy(a, 0, sibling, me).wait_recv()
            for j, chip in enumerate(chips):
                copy(a, 4 + j, (*chip, 1 - c), me).wait_recv()
        for cp in first() + passed():
            cp.wait_send()
        for cp in mine():
            cp.wait()

    return start, forward, finish


N_CHIP = N_DEV // 2


def _exchange_sibling(sliced):
    n = len(sliced)

    def body(*refs):
        srcs, outs = refs[:n], refs[n:2 * n]
        send_sems, recv_sems = refs[2 * n:]
        x, y, c = lax.axis_index("x"), lax.axis_index("y"), lax.axis_index("c")
        copies = []
        for a in range(n):
            for q in range(N_CHIP):
                cp = pltpu.make_async_remote_copy(
                    src_ref=srcs[a].at[2 * q + (1 - c)], dst_ref=outs[a].at[q],
                    send_sem=send_sems.at[a, q], recv_sem=recv_sems.at[a, q],
                    device_id=(x, y, 1 - c), device_id_type=pl.DeviceIdType.MESH)
                cp.start()
                copies.append(cp)
        for cp in copies:
            cp.wait_recv()
        for cp in copies:
            cp.wait_send()

    hbm = pl.BlockSpec(memory_space=pltpu.HBM)
    return _pcall(
        body, name="grad_exchange_sibling",
        in_specs=[hbm] * n, out_specs=[hbm] * n,
        out_shape=[jax.ShapeDtypeStruct((N_CHIP,) + s.shape[1:], s.dtype) for s in sliced],
        scratch_shapes=[pltpu.SemaphoreType.DMA((n, N_CHIP)), pltpu.SemaphoreType.DMA((n, N_CHIP))],
    )(*sliced)


def _pair_sum(own, recv):
    _, R, C = own.shape
    tr = _row_tile(R, C, mult=16)
    side = lax.axis_index("c").astype(jnp.int32).reshape(1)

    def body(c_ref, own_ref, recv_ref, o_ref):
        o_ref[...] = (own_ref[...].astype(f32) + recv_ref[...].astype(f32)).astype(o_ref.dtype)

    return _pcall(
        body, name="pair_sum",
        grid_spec=pltpu.PrefetchScalarGridSpec(
            num_scalar_prefetch=1, grid=(N_CHIP, R // tr),
            in_specs=[pl.BlockSpec((None, tr, C), lambda q, i, c: (2 * q + c[0], i, 0)),
                      pl.BlockSpec((None, tr, C), lambda q, i, c: (q, i, 0))],
            out_specs=pl.BlockSpec((None, tr, C), lambda q, i, c: (q, i, 0))),
        out_shape=jax.ShapeDtypeStruct((N_CHIP, R, C), own.dtype),
        compiler_params=_params("parallel", "parallel"),
    )(side, own, recv)


def _exchange_chips(partial, whole):
    ns, nw = len(partial), len(whole)

    def body(*refs):
        srcs, outs = refs[:ns + nw], refs[ns + nw:2 * (ns + nw)]
        send_sems, recv_sems, wsend_sems, wrecv_sems, local_sems = refs[2 * (ns + nw):]
        x, y, c = lax.axis_index("x"), lax.axis_index("y"), lax.axis_index("c")
        me, myq = _my_id(), x * 2 + y
        chips = [(1 - x, y), (x, 1 - y), (1 - x, 1 - y)]
        locals_ = [pltpu.make_async_copy(srcs[a].at[myq], outs[a].at[myq], local_sems.at[a]) for a in range(ns)]
        locals_ += [pltpu.make_async_copy(srcs[ns + b], outs[ns + b].at[me], local_sems.at[ns + b])
                    for b in range(nw)]
        for cp in locals_:
            cp.start()
        sends, recvs = [], []
        for j, chip in enumerate(chips):
            q = chip[0] * 2 + chip[1]
            for a in range(ns):
                cp = pltpu.make_async_remote_copy(
                    src_ref=srcs[a].at[q], dst_ref=outs[a].at[myq],
                    send_sem=send_sems.at[a, j], recv_sem=recv_sems.at[a, j],
                    device_id=(*chip, c), device_id_type=pl.DeviceIdType.MESH)
                cp.start()
                sends.append(cp)
                recvs.append(pltpu.make_async_remote_copy(
                    src_ref=srcs[a].at[q], dst_ref=outs[a].at[q],
                    send_sem=send_sems.at[a, j], recv_sem=recv_sems.at[a, j],
                    device_id=(*chip, c), device_id_type=pl.DeviceIdType.MESH))
        for k in range(1, N_DEV):
            peer, pid = _peer(k)
            for b in range(nw):
                cp = pltpu.make_async_remote_copy(
                    src_ref=srcs[ns + b], dst_ref=outs[ns + b].at[me],
                    send_sem=wsend_sems.at[b, k - 1], recv_sem=wrecv_sems.at[b, k - 1],
                    device_id=peer, device_id_type=pl.DeviceIdType.MESH)
                cp.start()
                sends.append(cp)
                recvs.append(pltpu.make_async_remote_copy(
                    src_ref=srcs[ns + b], dst_ref=outs[ns + b].at[pid],
                    send_sem=wsend_sems.at[b, k - 1], recv_sem=wrecv_sems.at[b, k - 1],
                    device_id=peer, device_id_type=pl.DeviceIdType.MESH))
        for cp in recvs:
            cp.wait_recv()
        for cp in sends:
            cp.wait_send()
        for cp in locals_:
            cp.wait()

    hbm = pl.BlockSpec(memory_space=pltpu.HBM)
    shapes = [jax.ShapeDtypeStruct(s.shape, s.dtype) for s in partial]
    shapes += [jax.ShapeDtypeStruct((N_DEV,) + s.shape, s.dtype) for s in whole]
    return _pcall(
        body, name="grad_exchange_chips",
        in_specs=[hbm] * (ns + nw), out_specs=[hbm] * (ns + nw), out_shape=shapes,
        scratch_shapes=[pltpu.SemaphoreType.DMA((ns, N_CHIP - 1)), pltpu.SemaphoreType.DMA((ns, N_CHIP - 1)),
                        pltpu.SemaphoreType.DMA((nw, N_DEV - 1)), pltpu.SemaphoreType.DMA((nw, N_DEV - 1)),
                        pltpu.SemaphoreType.DMA((ns + nw,))],
    )(*partial, *whole)


def _permute_cols(w):
    pad = jnp.zeros(w.shape[:-1] + (NZ - OFF_F - NH,), w.dtype)
    return jnp.concatenate([
        w[..., 3844:7940],
        w[..., 0:1024],
        w[..., 2052:3076],
        w[..., 3076:3844],
        w[..., 1024:2048],
        w[..., 2048:2052], pad], axis=-1)


def _unpermute_cols(g):
    return jnp.concatenate([
        g[..., OFF_CONV:OFF_CONV + 1024],
        g[..., OFF_ATT:OFF_ATT + 1024],
        g[..., OFF_F:OFF_F + NH],
        g[..., OFF_HGRN:OFF_HGRN + 1024],
        g[..., OFF_SGU:OFF_SGU + 768],
        g[..., 0:4096]], axis=-1)


_SMALL = (
    ("norm_mix", (DEPTH, D)), ("conv_w", (DEPTH, CONV_WIDTH, W)), ("conv_b", (DEPTH, W)),
    ("fgate_bias", (DEPTH, NH)), ("q_norm", (DEPTH, DH)), ("k_norm", (DEPTH, DH)),
    ("lb_logits", (DEPTH, W)), ("hgrn_norm", (DEPTH, W)), ("sgu_norm", (DEPTH, W)),
    ("spatial_w", (DEPTH, NH, SGU_CHUNK, SGU_CHUNK)), ("spatial_b", (DEPTH, NH, SGU_CHUNK)),
    ("merge_b", (DEPTH, NBR, D)), ("norm_ple", (DEPTH, D)),
)


def _small_rows(shape):
    size = 1
    for s in shape:
        size *= s
    rows = -(-size // 128)
    return size, -(-rows // 8) * 8


def _pack_small(parts):
    out = []
    for name, shape in _SMALL:
        size, rows = _small_rows(shape)
        flat = parts[name].astype(f32).reshape(-1)
        flat = jnp.pad(flat, (0, rows * 128 - size))
        out.append(flat.reshape(rows, 128))
    return jnp.concatenate(out, axis=0)


def _unpack_small(buf):
    parts, r0 = {}, 0
    for name, shape in _SMALL:
        size, rows = _small_rows(shape)
        parts[name] = buf[r0:r0 + rows].reshape(-1)[:size].reshape(shape)
        r0 += rows
    return parts


def _shard_cols(a, width):
    return lax.dynamic_slice_in_dim(a, _my_id() * width, width, axis=a.ndim - 1)


def kernel(x, p, norm_mix, w_in, conv_w, conv_b, fgate_bias, q_norm, k_norm, lb_logits, hgrn_norm, sgu_norm, spatial_w, spatial_b, w_up, merge_b, w_o, norm_ple, w_ple_gate, w_ple_proj, loss_target, m_norm_mix, m_w_in, m_conv_w, m_conv_b, m_fgate_bias, m_q_norm, m_k_norm, m_lb_logits, m_hgrn_norm, m_sgu_norm, m_spatial_w, m_spatial_b, m_w_up, m_merge_b, m_w_o, m_norm_ple, m_w_ple_gate, m_w_ple_proj, v_norm_mix, v_w_in, v_conv_w, v_conv_b, v_fgate_bias, v_q_norm, v_k_norm, v_lb_logits, v_hgrn_norm, v_sgu_norm, v_spatial_w, v_spatial_b, v_w_up, v_merge_b, v_w_o, v_norm_ple, v_w_ple_gate, v_w_ple_proj):
    T = x.shape[1]
    SH = D // N_DEV
    CW = W // N_DEV
    tm = 512 if T % 512 == 0 else T
    tmm = 256 if T % 256 == 0 else T
    x0 = x.reshape(T, D)
    target = loss_target.reshape(T, D)

    small_shard = jnp.concatenate([
        merge_b.reshape(DEPTH * NBR, SH),
        jnp.pad(conv_w.reshape(DEPTH * CONV_WIDTH, CW), ((0, 16 - DEPTH * CONV_WIDTH), (0, SH - CW)))], axis=0)
    win_s = _permute_cols(w_in).astype(bf16)
    win0, wup_f, wo_f, wpg_f, wpp_f, g_small = _all_gather(
        [win_s[0],
         w_up.astype(bf16).reshape(DEPTH * NBR * W, SH),
         w_o.astype(bf16),
         w_ple_gate.astype(bf16),
         w_ple_proj.astype(bf16).reshape(DEPTH * PLE, SH),
         small_shard],
        [0, -1, 1, 1, -1, -1])
    win_f = [win0]
    win_later = [win_s[li] for li in range(1, DEPTH)]
    wup_f = wup_f.reshape(DEPTH, NBR, W, D)
    wpp_f = wpp_f.reshape(DEPTH, PLE, D)
    mb_f = g_small[0:DEPTH * NBR].reshape(DEPTH, NBR, D)
    cw_f = g_small[16:16 + DEPTH * CONV_WIDTH].reshape(DEPTH, CONV_WIDTH, N_DEV, SH)[..., 0:CW]
    cw_f = cw_f.reshape(DEPTH, CONV_WIDTH, W)

    loss_local, dx, gw, gs_full = _forward_backward(
        x0, p[:, 0], target, win_f, wup_f, wo_f, wpg_f, wpp_f, mb_f, cw_f, norm_mix, conv_b, fgate_bias, q_norm,
        k_norm, lb_logits, hgrn_norm, sgu_norm, spatial_w, spatial_b, norm_ple, win_later)
    loss = lax.psum(loss_local[0, 0], AXES)
    grad_x = dx.reshape(1, T, D)

    weights = dict(norm_mix=norm_mix, w_in=w_in, conv_w=conv_w, conv_b=conv_b, fgate_bias=fgate_bias, q_norm=q_norm,
                   k_norm=k_norm, lb_logits=lb_logits, hgrn_norm=hgrn_norm, sgu_norm=sgu_norm, spatial_w=spatial_w,
                   spatial_b=spatial_b, w_up=w_up, merge_b=merge_b, w_o=w_o, norm_ple=norm_ple,
                   w_ple_gate=w_ple_gate, w_ple_proj=w_ple_proj)
    ms = dict(norm_mix=m_norm_mix, w_in=m_w_in, conv_w=m_conv_w, conv_b=m_conv_b, fgate_bias=m_fgate_bias,
              q_norm=m_q_norm, k_norm=m_k_norm, lb_logits=m_lb_logits, hgrn_norm=m_hgrn_norm, sgu_norm=m_sgu_norm,
              spatial_w=m_spatial_w, spatial_b=m_spatial_b, w_up=m_w_up, merge_b=m_merge_b, w_o=m_w_o,
              norm_ple=m_norm_ple, w_ple_gate=m_w_ple_gate, w_ple_proj=m_w_ple_proj)
    vs = dict(norm_mix=v_norm_mix, w_in=v_w_in, conv_w=v_conv_w, conv_b=v_conv_b, fgate_bias=v_fgate_bias,
              q_norm=v_q_norm, k_norm=v_k_norm, lb_logits=v_lb_logits, hgrn_norm=v_hgrn_norm, sgu_norm=v_sgu_norm,
              spatial_w=v_spatial_w, spatial_b=v_spatial_b, w_up=v_w_up, merge_b=v_merge_b, w_o=v_w_o,
              norm_ple=v_norm_ple, w_ple_gate=v_w_ple_gate, w_ple_proj=v_w_ple_proj)
    return _exchange_and_update(loss, grad_x, gw, gs_full, weights, ms, vs)


def _forward_backward(x0, p, target, win_f, wup_f, wo_f, wpg_f, wpp_f, mb_f, cw_f, norm_mix, conv_b, fgate_bias,
                      q_norm, k_norm, lb_logits, hgrn_norm, sgu_norm, spatial_w, spatial_b, norm_ple, win_later=()):
    T = x0.shape[0]
    tm = 512 if T % 512 == 0 else T
    tmm = 256 if T % 256 == 0 else T
    tmi = 1024 if T % 1024 == 0 else tm
    lower = _lower_bounds(lb_logits)
    fb_pad = jnp.pad(fgate_bias, ((0, 0), (0, 128 - NH)))
    gq_t = jnp.tile(q_norm, (1, NH))
    gk_t = jnp.tile(k_norm, (1, NH))
    sbe = jnp.repeat(jnp.swapaxes(spatial_b, 1, 2), DH, axis=2)

    saved = []
    xc = x0
    p = p[:, None]
    for li in range(DEPTH):
        row = lambda a: a[li:li + 1]
        if li == 0 and win_later:
            z, h, *gathered = _inproj_fwd(xc, row(norm_mix), win_f[0], tmi, gather=win_later)
            win_f = [win_f[0]] + gathered
        else:
            z, h = _inproj_fwd(xc, row(norm_mix), win_f[li], tmi)
        ya = _conv_fwd(z, cw_f[li], row(conv_b), tm)
        yd = _sgu_fwd(z, row(sgu_norm), spatial_w[li], sbe[li], tm)
        yc, o_pre, states = _hgrn_fwd(z, lower[li:li + 1], row(hgrn_norm), tmm)
        qt, kt, vt, kh, vh, cum = _attn_prep(z, row(fb_pad), row(gq_t), row(gk_t), tmm)
        cum4 = jnp.transpose(cum[:, 0:NH])
        ccol, crow = cum4[:, :, None], cum4[:, None, :]
        oh, lse = _attn_fwd(qt, kh, vt, crow, ccol, tm)
        yb, mg, x1, x2 = _merge_fwd(ya, oh, z, yc, yd, mb_f[li], xc, p[li, 0], wup_f[li], wo_f[li],
                                    row(norm_ple), wpg_f[li], wpp_f[li], tmm)
        saved.append(dict(x=xc, z=z, h=h, ya=ya, yb=yb, yc=yc, yd=yd, o_pre=o_pre, states=states,
                          qt=qt, kt=kt, kh=kh, vh=vh, crow=crow, ccol=ccol, oh=oh, lse=lse, mg=mg, x1=x1))
        xc = x2

    loss_local, dx = _loss_head(xc, target, tm)

    gw = dict(w_in=None, w_up=None, w_o=None, w_ple_gate=None, w_ple_proj=None)
    gs = {n: [None] * DEPTH for n, _ in _SMALL}
    dlower = [None] * DEPTH
    for li in reversed(range(DEPTH)):
        s = saved[li]
        row = lambda a: a[li:li + 1]
        first = li == DEPTH - 1
        dx1, gw["w_ple_gate"], gw["w_ple_proj"], ggp = _ple_bwd(
            dx, s["x1"], p[li, 0], row(norm_ple), wpg_f[li], wpp_f[li], tmm, li,
            None if first else (gw["w_ple_gate"], gw["w_ple_proj"]))
        gs["norm_ple"][li] = ggp[0]
        dz, dya, dyb, dyc, dyd, gw["w_o"], gw["w_up"], gs["merge_b"][li] = _merge_bwd(
            dx1, s["mg"], s["ya"], s["yb"], s["yc"], s["yd"], s["z"], mb_f[li], wup_f[li], wo_f[li], tmm, li,
            None if first else (gw["w_o"], gw["w_up"]))
        dz, gcw, gcb = _conv_bwd(s["z"], dya, cw_f[li], row(conv_b), dz, tm)
        gs["conv_w"][li], gs["conv_b"][li] = gcw[0:CONV_WIDTH], gcb[0]
        dz, gs["spatial_w"][li], gsb, ggv = _sgu_bwd(s["z"], dyd, row(sgu_norm), spatial_w[li], sbe[li], dz, tm)
        gs["spatial_b"][li] = jnp.transpose(gsb[:, ::DH])
        gs["sgu_norm"][li] = ggv[0]
        dz, ggn, glb = _hgrn_bwd(s["z"], lower[li:li + 1], row(hgrn_norm), s["o_pre"], s["states"], dyc, dz, tmm)
        gs["hgrn_norm"][li], dlower[li] = ggn[0], glb[0]
        dot, delta = _attn_bwd_prep(dyb, s["oh"], s["z"], tm)
        dqh, dkh, dvh, dck, dcq = _attn_bwd(s["qt"], s["kt"], s["kh"], s["vh"], s["crow"], s["ccol"], dot,
                                            s["lse"], delta, tm)
        dcq_t = jnp.transpose(dcq[:, :, 0:HP, :], (0, 2, 1, 3)).reshape(NH, T)
        dcq_t = jnp.pad(jnp.transpose(dcq_t), ((0, 0), (0, 128 - NH)))
        dz, ggq, ggk, gfb = _attn_post(s["z"], dyb, s["oh"], dqh, dkh, dvh, dck, dcq_t, row(fb_pad),
                                       row(gq_t), row(gk_t), dz, tmm)
        gs["q_norm"][li], gs["k_norm"][li], gs["fgate_bias"][li] = ggq[0, 0:DH], ggk[0, 0:DH], gfb[0, 0:NH]
        dx, gnm = _inproj_bwd_x(dz, win_f[li], s["x"], dx1, row(norm_mix), tmi)
        gs["norm_mix"][li] = gnm[0]
        gw["w_in"] = _inproj_bwd_w(s["h"], dz, tmi, li, gw["w_in"])
    dlower8 = jnp.pad(jnp.stack(dlower), ((0, 8 - DEPTH), (0, 0)))
    gs_full = {n: jnp.stack(v) for n, v in gs.items() if n != "lb_logits"}
    gs_full["lb_logits"] = _lower_bounds_bwd(lb_logits, dlower8)[0:DEPTH]
    return loss_local, dx, gw, gs_full


def _exchange_and_update(loss, grad_x, gw, gs_full, weights, ms, vs):
    SH = D // N_DEV
    CW = W // N_DEV

    small_buf = _pack_small(gs_full)
    own = [gw["w_in"], gw["w_up"], gw["w_o"], gw["w_ple_gate"], gw["w_ple_proj"]]
    from_sibling = _exchange_sibling(own)
    chip_sums = [_pair_sum(o, r) for o, r in zip(own, from_sibling)]
    l_win, l_wup, l_wo, l_wpg, l_wpp, l_small = _exchange_chips(chip_sums, [small_buf])

    g_w_in = _unpermute_cols(_sum_slabs(l_win)).reshape(DEPTH, SH, IN_COLS)
    g_w_up = _sum_slabs(l_wup).reshape(DEPTH, NBR, W, SH)
    g_w_o = _sum_slabs(l_wo).reshape(DEPTH, SH, D)
    g_w_pg = _sum_slabs(l_wpg).reshape(DEPTH, SH, D)
    g_w_pp = _sum_slabs(l_wpp).reshape(DEPTH, PLE, SH)
    g_small = _unpack_small(_sum_slabs(l_small))
    g_small_local = dict(g_small)
    g_small_local["conv_w"] = _shard_cols(g_small["conv_w"], CW)
    g_small_local["merge_b"] = _shard_cols(g_small["merge_b"], SH)

    grads = dict(w_in=g_w_in, w_up=g_w_up, w_o=g_w_o, w_ple_gate=g_w_pg, w_ple_proj=g_w_pp)
    deltas, new_m, new_v = {}, {}, {}
    for name in ("w_in", "w_up", "w_o", "w_ple_gate", "w_ple_proj"):
        shape = weights[name].shape
        as3 = (shape[0], -1, shape[-1])
        d_, m_, v_ = _adamw(weights[name].reshape(as3), grads[name].reshape(as3),
                            ms[name].reshape(as3), vs[name].reshape(as3))
        deltas[name], new_m[name], new_v[name] = d_.reshape(shape), m_.reshape(shape), v_.reshape(shape)

    def local_shapes(parts):
        return {n: (parts[n] if parts[n].shape == s else jnp.pad(
            parts[n], [(0, 0)] * (len(s) - 1) + [(0, s[-1] - parts[n].shape[-1])])) for n, s in _SMALL}

    d_, m_, v_ = _adamw(_pack_small(local_shapes(weights)), _pack_small(local_shapes(g_small_local)),
                        _pack_small(local_shapes(ms)), _pack_small(local_shapes(vs)))
    for buf, dst in ((d_, deltas), (m_, new_m), (v_, new_v)):
        parts = _unpack_small(buf)
        for n, _ in _SMALL:
            dst[n] = parts[n][..., :weights[n].shape[-1]]
    for n, _ in _SMALL:
        grads[n] = g_small_local[n]

    order = ["norm_mix", "w_in", "conv_w", "conv_b", "fgate_bias", "q_norm", "k_norm", "lb_logits", "hgrn_norm",
             "sgu_norm", "spatial_w", "spatial_b", "w_up", "merge_b", "w_o", "norm_ple", "w_ple_gate", "w_ple_proj"]
    return (loss, grad_x, *[grads[n] for n in order], *[deltas[n] for n in order],
            *[new_m[n] for n in order], *[new_v[n] for n in order])
```

```python
import functools

import jax
import jax.numpy as jnp
from jax import lax
from jax.experimental import pallas as pl
from jax.experimental.pallas import tpu as pltpu

f32 = jnp.float32
bf16 = jnp.bfloat16

D = 1024
W = 256
NH = 4
DH = 64
NBR = 4
PLE = 256
DEPTH = 4
CONV_WIDTH = 3
SGU_CHUNK = 128
GLA_CHUNK = 128
EPS = 1e-6
MASK_VALUE = -1e30
IN_COLS = 7940
NZ = 8064
OFF_CONV = 4096
OFF_HGRN = 5120
OFF_SGU = 6144
OFF_ATT = 6912
OFF_F = 7936
ZT = 1152
NZT = NZ // ZT
EXP_CLAMP = 80.0
LOG2E = 1.4426950408889634

ADAM_LR = 0.001
ADAM_B1 = 0.9
ADAM_B2 = 0.999
ADAM_EPS = 1e-08
ADAM_WD = 0.01
ADAM_STEP = 10

N_DEV = 8
AXES = ("x", "y", "c")
VMEM_LIMIT = 56 * 1024 * 1024
HI = lax.Precision.HIGHEST

NT_DIMS = (((1,), (1,)), ((), ()))
TN_DIMS = (((0,), (0,)), ((), ()))


def _pcall(body, **kw):
    return pl.pallas_call(body, **kw)


def _params(*sem):
    return pltpu.CompilerParams(dimension_semantics=sem, vmem_limit_bytes=VMEM_LIMIT)


def _mm(a, b):
    return jnp.dot(a, b, preferred_element_type=f32)


def _mm_nt(a, b):
    return lax.dot_general(a, b, NT_DIMS, preferred_element_type=f32)


def _mm_tn(a, b):
    return lax.dot_general(a, b, TN_DIMS, preferred_element_type=f32)


def _sigmoid(x):
    return 1.0 / (1.0 + jnp.exp(-x))


def _silu(x):
    return x * _sigmoid(x)


def _dsilu(x):
    s = _sigmoid(x)
    return s * (1.0 + x * (1.0 - s))


def _logsigmoid(x):
    return jnp.minimum(x, 0.0) - jnp.log(1.0 + jnp.exp(-jnp.abs(x)))


def _iota2(shape, axis):
    return lax.broadcasted_iota(jnp.int32, shape, axis)


def _group_mean_matrix(n, group):
    shift = group.bit_length() - 1
    r = lax.shift_right_logical(_iota2((n, n), 0), shift)
    c = lax.shift_right_logical(_iota2((n, n), 1), shift)
    return jnp.where(r == c, 1.0 / group, 0.0).astype(f32)


def _group_mean(x, gm):
    return jnp.dot(x, gm, precision=HI, preferred_element_type=f32)


def _lower_tri(n):
    return jnp.where(_iota2((n, n), 0) >= _iota2((n, n), 1), 1.0, 0.0).astype(f32)


def _upper_tri(n):
    return jnp.where(_iota2((n, n), 0) <= _iota2((n, n), 1), 1.0, 0.0).astype(f32)


def _rows3(r0, r1, r2, width):
    row = _iota2((8, width), 0)
    return jnp.where(row == 0, r0, jnp.where(row == 1, r1, jnp.where(row == 2, r2, 0.0)))


def _inproj_fwd(x, g, w, tm, gather=()):
    T = x.shape[0]
    n = len(gather)
    axes = [0] * n
    steps = (T // tm) * NZT

    def body(x_ref, g_ref, w_ref, *rest):
        z_ref, h_ref = rest[n:n + 2]

        @pl.when(pl.program_id(1) == 0)
        def _():
            xv = x_ref[...]
            r = lax.rsqrt(jnp.mean(xv * xv, axis=-1, keepdims=True) + EPS)
            h_ref[...] = (xv * r * g_ref[...]).astype(bf16)

        if n:
            start, forward, finish = _gather_phases(gather, axes, rest[:n], rest[n + 2:2 * n + 2], *rest[2 * n + 2:])
            step = pl.program_id(0) * NZT + pl.program_id(1)
            pl.when(step == 0)(start)
            pl.when(step == steps // 2)(forward)

        z_ref[...] = _mm(h_ref[...], w_ref[...])

        if n:
            pl.when(step == steps - 1)(finish)

    hbm = pl.BlockSpec(memory_space=pltpu.HBM)
    return _pcall(
        body, name="inproj_fwd_gather" if n else "inproj_fwd", grid=(T // tm, NZT),
        in_specs=[pl.BlockSpec((tm, D), lambda i, j: (i, 0)),
                  pl.BlockSpec((1, D), lambda i, j: (0, 0)),
                  pl.BlockSpec((D, ZT), lambda i, j: (0, j))] + [hbm] * n,
        out_specs=[pl.BlockSpec((tm, ZT), lambda i, j: (i, j)),
                   pl.BlockSpec((tm, D), lambda i, j: (i, 0))] + [hbm] * n,
        out_shape=[jax.ShapeDtypeStruct((T, NZ), f32), jax.ShapeDtypeStruct((T, D), bf16)]
        + _gathered_shapes(gather, axes),
        scratch_shapes=_gather_semaphores(n) if n else [],
        compiler_params=_params("arbitrary" if n else "parallel", "arbitrary"),
    )(x, g, w, *gather)


def _inproj_bwd_x(dz, w, x, dx1, g, tm):
    T = x.shape[0]

    def body(dz_ref, w_ref, x_ref, dx1_ref, g_ref, dx_ref, gg_ref, acc):
        i, k = pl.program_id(0), pl.program_id(1)

        @pl.when(k == 0)
        def _():
            acc[...] = jnp.zeros_like(acc)

        @pl.when((i == 0) & (k == 0))
        def _():
            gg_ref[...] = jnp.zeros_like(gg_ref)

        acc[...] += _mm_nt(dz_ref[...], w_ref[...])

        @pl.when(k == NZT - 1)
        def _():
            xv = x_ref[...]
            r = lax.rsqrt(jnp.mean(xv * xv, axis=-1, keepdims=True) + EPS)
            dh = acc[...]
            gg_ref[...] += jnp.sum(dh * xv * r, axis=0, keepdims=True)
            u = dh * g_ref[...]
            dx_ref[...] = dx1_ref[...] + r * u - xv * (r * r * r) * jnp.mean(u * xv, axis=-1, keepdims=True)

    return _pcall(
        body, name="inproj_bwd_x", grid=(T // tm, NZT),
        in_specs=[pl.BlockSpec((tm, ZT), lambda i, k: (i, k)),
                  pl.BlockSpec((D, ZT), lambda i, k: (0, k)),
                  pl.BlockSpec((tm, D), lambda i, k: (i, 0)),
                  pl.BlockSpec((tm, D), lambda i, k: (i, 0)),
                  pl.BlockSpec((1, D), lambda i, k: (0, 0))],
        out_specs=[pl.BlockSpec((tm, D), lambda i, k: (i, 0)),
                   pl.BlockSpec((1, D), lambda i, k: (0, 0))],
        out_shape=[jax.ShapeDtypeStruct((T, D), f32), jax.ShapeDtypeStruct((1, D), f32)],
        scratch_shapes=[pltpu.VMEM((tm, D), f32)],
        compiler_params=_params("arbitrary", "arbitrary"),
    )(dz, w, x, dx1, g)


def _inproj_bwd_w(h, dz, tm, li, buf):
    T = h.shape[0]
    SH = D // N_DEV
    nt = T // tm
    extra = [] if buf is None else [buf]

    def body(h_ref, dz_ref, *rest):
        gw_ref, acc = rest[len(extra):]

        @pl.when(pl.program_id(1) == 0)
        def _():
            acc[...] = jnp.zeros_like(acc)

        acc[...] += _mm_tn(h_ref[...], dz_ref[...])

        @pl.when(pl.program_id(1) == nt - 1)
        def _():
            gw_ref[...] = acc[...].reshape(N_DEV, SH, ZT).astype(bf16)

    return _pcall(
        body, name="inproj_bwd_w", grid=(NZT, nt),
        in_specs=[pl.BlockSpec((tm, D), lambda j, i: (i, 0)),
                  pl.BlockSpec((tm, ZT), lambda j, i: (i, j))] + [pl.BlockSpec(memory_space=pl.ANY)] * len(extra),
        out_specs=pl.BlockSpec((N_DEV, SH, ZT), lambda j, i: (0, li, j)),
        out_shape=jax.ShapeDtypeStruct((N_DEV, DEPTH * SH, NZ), bf16),
        scratch_shapes=[pltpu.VMEM((D, ZT), f32)],
        input_output_aliases={2: 0} if extra else {},
        compiler_params=_params("parallel", "arbitrary"),
    )(h, dz, *extra)


def _zblock(tm, col256):
    return pl.BlockSpec((tm, W), lambda i, c=col256: (i, c))


def _conv_taps(zc, halo, cw_ref, n):
    ext = jnp.concatenate([halo, zc], axis=0)
    z1 = pltpu.roll(ext, 1, 0)[8:]
    z2 = pltpu.roll(ext, 2, 0)[8:]
    return z1, z2


def _conv_fwd(z, cw, cb, tm):
    T = z.shape[0]
    c0 = OFF_CONV // W
    hb = tm // 8

    def body(ax_ref, ab_ref, ac_ref, ag_ref, hx_ref, hc_ref, cw_ref, cb_ref, y_ref):
        i = pl.program_id(0)
        zc = ac_ref[...] * ax_ref[...]
        halo = jnp.where(i > 0, hc_ref[...] * hx_ref[...], 0.0)
        z1, z2 = _conv_taps(zc, halo, cw_ref, tm)
        y = cw_ref[2:3, :] * zc + cw_ref[1:2, :] * z1 + cw_ref[0:1, :] * z2
        ya = ab_ref[...] * (y + cb_ref[...])
        y_ref[...] = (ya * _silu(ag_ref[...])).astype(bf16)

    halo_spec = lambda col: pl.BlockSpec((8, W), lambda i, c=col: (jnp.maximum(i * hb - 1, 0), c))
    return _pcall(
        body, name="conv_fwd", grid=(T // tm,),
        in_specs=[_zblock(tm, c0), _zblock(tm, c0 + 1), _zblock(tm, c0 + 2), _zblock(tm, c0 + 3),
                  halo_spec(c0), halo_spec(c0 + 2),
                  pl.BlockSpec((CONV_WIDTH, W), lambda i: (0, 0)),
                  pl.BlockSpec((1, W), lambda i: (0, 0))],
        out_specs=pl.BlockSpec((tm, W), lambda i: (i, 0)),
        out_shape=jax.ShapeDtypeStruct((T, W), bf16),
        compiler_params=_params("parallel"),
    )(z, z, z, z, z, z, cw, cb)


def _conv_bwd(z, dy, cw, cb, dzbuf, tm):
    T = z.shape[0]
    c0 = OFF_CONV // W
    hb = tm // 8
    nt = T // tm

    def body(ax_ref, ab_ref, ac_ref, ag_ref, hx_ref, hc_ref, nb_ref, ng_ref, dy_ref, ndy_ref,
             cw_ref, cb_ref, dzin_ref, dz_ref, gcw_ref, gcb_ref):
        i = pl.program_id(0)

        @pl.when(i == 0)
        def _():
            gcw_ref[...] = jnp.zeros_like(gcw_ref)
            gcb_ref[...] = jnp.zeros_like(gcb_ref)

        ax, ab, ac, ag = ax_ref[...], ab_ref[...], ac_ref[...], ag_ref[...]
        w0, w1, w2 = cw_ref[0:1, :], cw_ref[1:2, :], cw_ref[2:3, :]
        zc = ac * ax
        halo = jnp.where(i > 0, hc_ref[...] * hx_ref[...], 0.0)
        z1, z2 = _conv_taps(zc, halo, cw_ref, tm)
        yb = w2 * zc + w1 * z1 + w0 * z2 + cb_ref[...]
        ya = ab * yb
        dyg = dy_ref[...]
        dag = dyg * ya * _dsilu(ag)
        dya = dyg * _silu(ag)
        dab = dya * yb
        dyc = dya * ab
        nxt = jnp.where(i < nt - 1, ndy_ref[...] * _silu(ng_ref[...]) * nb_ref[...], 0.0)
        ext = jnp.concatenate([dyc, nxt], axis=0)
        d1 = pltpu.roll(ext, tm + 8 - 1, 0)[:tm]
        d2 = pltpu.roll(ext, tm + 8 - 2, 0)[:tm]
        dzc = w2 * dyc + w1 * d1 + w0 * d2
        dz_ref[:, 0:W] = (dzc * ac).astype(bf16)
        dz_ref[:, W:2 * W] = dab.astype(bf16)
        dz_ref[:, 2 * W:3 * W] = (dzc * ax).astype(bf16)
        dz_ref[:, 3 * W:4 * W] = dag.astype(bf16)
        gcb_ref[...] += jnp.sum(dyc, axis=0, keepdims=True)
        gcw_ref[...] += _rows3(jnp.sum(dyc * z2, axis=0, keepdims=True),
                               jnp.sum(dyc * z1, axis=0, keepdims=True),
                               jnp.sum(dyc * zc, axis=0, keepdims=True), W)

    prev_spec = lambda col: pl.BlockSpec((8, W), lambda i, c=col: (jnp.maximum(i * hb - 1, 0), c))
    next_z = lambda col: pl.BlockSpec((8, W), lambda i, c=col: (jnp.minimum((i + 1) * hb, T // 8 - 1), c))
    next_dy = pl.BlockSpec((8, W), lambda i: (jnp.minimum((i + 1) * hb, T // 8 - 1), 0))
    return _pcall(
        body, name="conv_bwd", grid=(nt,),
        in_specs=[_zblock(tm, c0), _zblock(tm, c0 + 1), _zblock(tm, c0 + 2), _zblock(tm, c0 + 3),
                  prev_spec(c0), prev_spec(c0 + 2), next_z(c0 + 1), next_z(c0 + 3),
                  pl.BlockSpec((tm, W), lambda i: (i, 0)), next_dy,
                  pl.BlockSpec((CONV_WIDTH, W), lambda i: (0, 0)),
                  pl.BlockSpec((1, W), lambda i: (0, 0)),
                  pl.BlockSpec(memory_space=pl.ANY)],
        out_specs=[pl.BlockSpec((tm, 4 * W), lambda i: (i, OFF_CONV // (4 * W))),
                   pl.BlockSpec((8, W), lambda i: (0, 0)),
                   pl.BlockSpec((1, W), lambda i: (0, 0))],
        out_shape=[jax.ShapeDtypeStruct((T, NZ), bf16), jax.ShapeDtypeStruct((8, W), f32),
                   jax.ShapeDtypeStruct((1, W), f32)],
        input_output_aliases={12: 0},
        compiler_params=_params("arbitrary"),
    )(z, z, z, z, z, z, z, z, dy, dy, cw, cb, dzbuf)


def _sgu_core(dv_ref, gv_ref, sw_ref, sbe_ref, s_scr, tm):
    v = dv_ref[...]
    gm = _group_mean_matrix(W, DH)
    rv = lax.rsqrt(_group_mean(v * v, gm) + EPS)
    vh = v * rv
    vnb = (vh * gv_ref[...]).astype(bf16)
    causal = _iota2((SGU_CHUNK, SGU_CHUNK), 0) >= _iota2((SGU_CHUNK, SGU_CHUNK), 1)
    wgs = [jnp.where(causal, sw_ref[g], 0.0).astype(bf16) for g in range(NH)]
    for c in range(tm // SGU_CHUNK):
        rows = slice(c * SGU_CHUNK, (c + 1) * SGU_CHUNK)
        for g in range(NH):
            cols = slice(g * DH, (g + 1) * DH)
            s_scr[rows, cols] = _mm(wgs[g], vnb[rows, cols])
    sb = sbe_ref[...]
    s = s_scr[...] + jnp.concatenate([sb] * (tm // SGU_CHUNK), axis=0)
    return v, rv, vh, vnb, wgs, causal, gm, s


def _sgu_fwd(z, gv, sw, sbe, tm):
    T = z.shape[0]
    c0 = OFF_SGU // W

    def body(du_ref, dv_ref, dg_ref, gv_ref, sw_ref, sbe_ref, y_ref, s_scr):
        s = _sgu_core(dv_ref, gv_ref, sw_ref, sbe_ref, s_scr, tm)[-1]
        y_ref[...] = ((du_ref[...] * s) * _silu(dg_ref[...])).astype(bf16)

    return _pcall(
        body, name="sgu_fwd", grid=(T // tm,),
        in_specs=[_zblock(tm, c0), _zblock(tm, c0 + 1), _zblock(tm, c0 + 2),
                  pl.BlockSpec((1, W), lambda i: (0, 0)),
                  pl.BlockSpec((NH, SGU_CHUNK, SGU_CHUNK), lambda i: (0, 0, 0)),
                  pl.BlockSpec((SGU_CHUNK, W), lambda i: (0, 0))],
        out_specs=pl.BlockSpec((tm, W), lambda i: (i, 0)),
        out_shape=jax.ShapeDtypeStruct((T, W), bf16),
        scratch_shapes=[pltpu.VMEM((tm, W), f32)],
        compiler_params=_params("parallel"),
    )(z, z, z, gv, sw, sbe)


def _sgu_bwd(z, dy, gv, sw, sbe, dzbuf, tm):
    T = z.shape[0]
    c0 = OFF_SGU // W
    nt = T // tm

    def body(du_ref, dv_ref, dg_ref, dy_ref, gv_ref, sw_ref, sbe_ref, dzin_ref,
             dz_ref, gsw_ref, gsb_ref, ggv_ref, s_scr, dvn_scr, sb_acc):
        i = pl.program_id(0)

        @pl.when(i == 0)
        def _():
            gsw_ref[...] = jnp.zeros_like(gsw_ref)
            ggv_ref[...] = jnp.zeros_like(ggv_ref)
            sb_acc[...] = jnp.zeros_like(sb_acc)

        v, rv, vh, vnb, wgs, causal, gm, s = _sgu_core(dv_ref, gv_ref, sw_ref, sbe_ref, s_scr, tm)
        du, dg, dyv = du_ref[...], dg_ref[...], dy_ref[...]
        ddg = dyv * (du * s) * _dsilu(dg)
        t = dyv * _silu(dg)
        ddu = t * s
        ds = t * du
        dsb = ds.astype(bf16)
        acc = sb_acc[...]
        for c in range(tm // SGU_CHUNK):
            rows = slice(c * SGU_CHUNK, (c + 1) * SGU_CHUNK)
            acc = acc + ds[rows, :]
            for g in range(NH):
                cols = slice(g * DH, (g + 1) * DH)
                gsw_ref[g] += jnp.where(causal, _mm_nt(dsb[rows, cols], vnb[rows, cols]), 0.0)
                dvn_scr[rows, cols] = _mm_tn(wgs[g], dsb[rows, cols])
        sb_acc[...] = acc
        dvn = dvn_scr[...]
        ggv_ref[...] += jnp.sum(dvn * vh, axis=0, keepdims=True)
        u = dvn * gv_ref[...]
        ddv = rv * u - v * (rv * rv * rv) * _group_mean(u * v, gm)
        dz_ref[:, 0:W] = ddu.astype(bf16)
        dz_ref[:, W:2 * W] = ddv.astype(bf16)
        dz_ref[:, 2 * W:3 * W] = ddg.astype(bf16)

        @pl.when(i == nt - 1)
        def _():
            gsb_ref[...] = _group_mean(sb_acc[...], gm) * float(DH)

    return _pcall(
        body, name="sgu_bwd", grid=(nt,),
        in_specs=[_zblock(tm, c0), _zblock(tm, c0 + 1), _zblock(tm, c0 + 2),
                  pl.BlockSpec((tm, W), lambda i: (i, 0)),
                  pl.BlockSpec((1, W), lambda i: (0, 0)),
                  pl.BlockSpec((NH, SGU_CHUNK, SGU_CHUNK), lambda i: (0, 0, 0)),
                  pl.BlockSpec((SGU_CHUNK, W), lambda i: (0, 0)),
                  pl.BlockSpec(memory_space=pl.ANY)],
        out_specs=[pl.BlockSpec((tm, 3 * W), lambda i: (i, OFF_SGU // (3 * W))),
                   pl.BlockSpec((NH, SGU_CHUNK, SGU_CHUNK), lambda i: (0, 0, 0)),
                   pl.BlockSpec((SGU_CHUNK, W), lambda i: (0, 0)),
                   pl.BlockSpec((1, W), lambda i: (0, 0))],
        out_shape=[jax.ShapeDtypeStruct((T, NZ), bf16),
                   jax.ShapeDtypeStruct((NH, SGU_CHUNK, SGU_CHUNK), f32),
                   jax.ShapeDtypeStruct((SGU_CHUNK, W), f32),
                   jax.ShapeDtypeStruct((1, W), f32)],
        scratch_shapes=[pltpu.VMEM((tm, W), f32), pltpu.VMEM((tm, W), f32), pltpu.VMEM((SGU_CHUNK, W), f32)],
        input_output_aliases={7: 0},
        compiler_params=_params("arbitrary"),
    )(z, z, z, dy, gv, sw, sbe, dzbuf)


def _hgrn_gates(cq_ref, cf_ref, lb_ref):
    q = _silu(cq_ref[...])
    sig = _sigmoid(cf_ref[...])
    lb = lb_ref[...]
    g = lb + (1.0 - lb) * sig
    return q, sig, g, jnp.log(g), (1.0 - lb) * (1.0 - sig)


def _hgrn_chunk_terms(lgc, qc, kc):
    C = GLA_CHUNK
    b = jnp.dot(_lower_tri(C), lgc, precision=HI, preferred_element_type=f32)
    bl = jnp.sum(lgc, axis=0, keepdims=True)
    mid = jnp.sum(jnp.where(_iota2((C, W), 0) <= C // 2, lgc, 0.0), axis=0, keepdims=True)
    eb = jnp.exp(b)
    em = jnp.exp(jnp.minimum(b - mid, EXP_CLAMP))
    emi = jnp.exp(jnp.minimum(mid - b, EXP_CLAMP))
    ek = jnp.exp(bl - b)
    return dict(eb=eb, em=em, emi=emi, ek=ek, ebl=jnp.exp(bl),
                qe=qc * eb, qm=qc * em, km=kc * emi, kd=kc * ek)


def _hgrn_fwd(z, lb, gain, tm):
    T = z.shape[0]
    c0 = OFF_HGRN // W
    C = GLA_CHUNK
    ncp = tm // C

    def body(cq_ref, cf_ref, ci_ref, cg_ref, lb_ref, gn_ref, y_ref, o_ref, st_ref, state, o_scr):
        @pl.when(pl.program_id(0) == 0)
        def _():
            state[...] = jnp.zeros_like(state)

        q, sig, g, lg, kf = _hgrn_gates(cq_ref, cf_ref, lb_ref)
        v = ci_ref[...]
        causal = _iota2((C, C), 0) >= _iota2((C, C), 1)
        for c in range(ncp):
            rows = slice(c * C, (c + 1) * C)
            tr = _hgrn_chunk_terms(lg[rows], q[rows], kf[rows])
            vb = v[rows].astype(bf16)
            qmb, kmb, qeb, kdb = (tr[n].astype(bf16) for n in ("qm", "km", "qe", "kd"))
            for h in range(NH):
                cols = slice(h * DH, (h + 1) * DH)
                hr = slice(h * DH, (h + 1) * DH)
                st = state[hr, :]
                st_ref[c, hr, :] = st
                p = jnp.where(causal, _mm_nt(qmb[:, cols], kmb[:, cols]), 0.0)
                o_scr[rows, cols] = _mm(p.astype(bf16), vb[:, cols]) + _mm_nt(qeb[:, cols], st.astype(bf16))
                state[hr, :] = st * tr["ebl"][:, cols] + _mm_tn(vb[:, cols], kdb[:, cols])
        o = o_scr[...]
        o_ref[...] = o
        gm = _group_mean_matrix(W, DH)
        r = lax.rsqrt(_group_mean(o * o, gm) + EPS)
        y_ref[...] = ((o * r * gn_ref[...]) * _silu(cg_ref[...])).astype(bf16)

    return _pcall(
        body, name="hgrn_fwd", grid=(T // tm,),
        in_specs=[_zblock(tm, c0), _zblock(tm, c0 + 1), _zblock(tm, c0 + 2), _zblock(tm, c0 + 3),
                  pl.BlockSpec((1, W), lambda i: (0, 0)), pl.BlockSpec((1, W), lambda i: (0, 0))],
        out_specs=[pl.BlockSpec((tm, W), lambda i: (i, 0)),
                   pl.BlockSpec((tm, W), lambda i: (i, 0)),
                   pl.BlockSpec((ncp, W, DH), lambda i: (i, 0, 0))],
        out_shape=[jax.ShapeDtypeStruct((T, W), bf16), jax.ShapeDtypeStruct((T, W), f32),
                   jax.ShapeDtypeStruct((T // C, W, DH), f32)],
        scratch_shapes=[pltpu.VMEM((W, DH), f32), pltpu.VMEM((tm, W), f32)],
        compiler_params=_params("arbitrary"),
    )(z, z, z, z, lb, gain)


def _hgrn_bwd(z, lb, gain, o_pre, states, dy, dzbuf, tm):
    T = z.shape[0]
    c0 = OFF_HGRN // W
    C = GLA_CHUNK
    ncp = tm // C
    nt = T // tm

    def body(cq_ref, cf_ref, ci_ref, cg_ref, lb_ref, gn_ref, o_ref, st_ref, dy_ref, dzin_ref,
             dz_ref, ggn_ref, glb_ref, dstate, dq_s, dk_s, dv_s, db_s):
        @pl.when(pl.program_id(0) == 0)
        def _():
            dstate[...] = jnp.zeros_like(dstate)
            ggn_ref[...] = jnp.zeros_like(ggn_ref)
            glb_ref[...] = jnp.zeros_like(glb_ref)

        cq, cg = cq_ref[...], cg_ref[...]
        q, sig, g, lg, kf = _hgrn_gates(cq_ref, cf_ref, lb_ref)
        lb = lb_ref[...]
        v = ci_ref[...]
        o = o_ref[...]
        gm = _group_mean_matrix(W, DH)
        r = lax.rsqrt(_group_mean(o * o, gm) + EPS)
        oh = o * r
        gn = gn_ref[...]
        dyv = dy_ref[...]
        dcg = dyv * (oh * gn) * _dsilu(cg)
        don = dyv * _silu(cg)
        ggn_ref[...] += jnp.sum(don * oh, axis=0, keepdims=True)
        u = don * gn
        do = r * u - o * (r * r * r) * _group_mean(u * o, gm)

        causal = _iota2((C, C), 0) >= _iota2((C, C), 1)
        last_row = _iota2((C, DH), 0) == C - 1
        for c in reversed(range(ncp)):
            rows = slice(c * C, (c + 1) * C)
            tr = _hgrn_chunk_terms(lg[rows], q[rows], kf[rows])
            vb = v[rows].astype(bf16)
            dob = do[rows].astype(bf16)
            qmb, kmb, qeb, kdb = (tr[n].astype(bf16) for n in ("qm", "km", "qe", "kd"))
            for h in range(NH):
                cols = slice(h * DH, (h + 1) * DH)
                hr = slice(h * DH, (h + 1) * DH)
                st0 = st_ref[c, hr, :]
                dst = dstate[hr, :]
                dstb = dst.astype(bf16)
                doh = dob[:, cols]
                p = jnp.where(causal, _mm_nt(qmb[:, cols], kmb[:, cols]), 0.0)
                dp = jnp.where(causal, _mm_nt(doh, vb[:, cols]), 0.0)
                dpb = dp.astype(bf16)
                dvh = _mm_tn(p.astype(bf16), doh) + _mm_nt(kdb[:, cols], dstb)
                dqm = _mm(dpb, kmb[:, cols])
                dkm = _mm_tn(dpb, qmb[:, cols])
                dqe = _mm(doh, st0.astype(bf16))
                dkd = _mm(vb[:, cols], dstb)
                ebl = tr["ebl"][:, cols]
                dstate[hr, :] = dst * ebl + _mm_tn(doh, qeb[:, cols])
                qm, km, qe, kd = (a[:, cols].astype(f32) for a in (qmb, kmb, qeb, kdb))
                kterm = dkd * kd
                dbh = dqm * qm - dkm * km + dqe * qe - kterm
                extra = jnp.sum(kterm, axis=0, keepdims=True) + ebl * jnp.sum(dst * st0, axis=0, keepdims=True)
                dbh = dbh + jnp.where(last_row, extra, 0.0)
                dq_s[rows, cols] = dqm * tr["em"][:, cols] + dqe * tr["eb"][:, cols]
                dk_s[rows, cols] = dkm * tr["emi"][:, cols] + dkd * tr["ek"][:, cols]
                dv_s[rows, cols] = dvh
                db_s[rows, cols] = dbh
            db_s[rows, :] = jnp.dot(_upper_tri(C), db_s[rows, :], precision=HI, preferred_element_type=f32)
        dlg = db_s[...]
        dk = dk_s[...]
        dsig = sig * (1.0 - sig)
        one_lb = 1.0 - lb
        dcf = (dlg / g - dk) * one_lb * dsig
        glb_ref[...] += jnp.sum((dlg / g - dk) * (1.0 - sig), axis=0, keepdims=True)
        dz_ref[:, 0:W] = (dq_s[...] * _dsilu(cq)).astype(bf16)
        dz_ref[:, W:2 * W] = dcf.astype(bf16)
        dz_ref[:, 2 * W:3 * W] = dv_s[...].astype(bf16)
        dz_ref[:, 3 * W:4 * W] = dcg.astype(bf16)

    rev = lambda i: nt - 1 - i
    zb = lambda col: pl.BlockSpec((tm, W), lambda i, c=col: (rev(i), c))
    return _pcall(
        body, name="hgrn_bwd", grid=(nt,),
        in_specs=[zb(c0), zb(c0 + 1), zb(c0 + 2), zb(c0 + 3),
                  pl.BlockSpec((1, W), lambda i: (0, 0)), pl.BlockSpec((1, W), lambda i: (0, 0)),
                  pl.BlockSpec((tm, W), lambda i: (rev(i), 0)),
                  pl.BlockSpec((ncp, W, DH), lambda i: (rev(i), 0, 0)),
                  pl.BlockSpec((tm, W), lambda i: (rev(i), 0)),
                  pl.BlockSpec(memory_space=pl.ANY)],
        out_specs=[pl.BlockSpec((tm, 4 * W), lambda i: (rev(i), OFF_HGRN // (4 * W))),
                   pl.BlockSpec((1, W), lambda i: (0, 0)),
                   pl.BlockSpec((1, W), lambda i: (0, 0))],
        out_shape=[jax.ShapeDtypeStruct((T, NZ), bf16), jax.ShapeDtypeStruct((1, W), f32),
                   jax.ShapeDtypeStruct((1, W), f32)],
        scratch_shapes=[pltpu.VMEM((W, DH), f32)] + [pltpu.VMEM((tm, W), f32)] * 4,
        input_output_aliases={9: 0},
        compiler_params=_params("arbitrary"),
    )(z, z, z, z, lb, gain, o_pre, states, dy, dzbuf)


def _attn_prep(z, fbias, gq, gk, tm):
    T = z.shape[0]
    c0 = OFF_ATT // W

    def body(q_ref, k_ref, v_ref, f_ref, fb_ref, gq_ref, gk_ref, qt_ref, kt_ref, vt_ref, kh_ref, vh_ref, cum_ref,
             carry):
        @pl.when(pl.program_id(0) == 0)
        def _():
            carry[...] = jnp.zeros_like(carry)

        gm = _group_mean_matrix(W, DH)
        q, k, v = q_ref[...], k_ref[...], v_ref[...]
        qs = q * lax.rsqrt(_group_mean(q * q, gm) + EPS) * (gq_ref[...] * (DH ** -0.5 * LOG2E))
        kn = k * lax.rsqrt(_group_mean(k * k, gm) + EPS) * gk_ref[...]
        qt_ref[...] = qs.T.astype(bf16)
        kt_ref[...] = kn.T.astype(bf16)
        vt_ref[...] = v.T.astype(bf16)
        for h in range(NH):
            cols = slice(h * DH, (h + 1) * DH)
            kh_ref[h] = kn[:, cols].astype(bf16)
            vh_ref[h] = v[:, cols].astype(bf16)
        ls = _logsigmoid(f_ref[...] + fb_ref[...])
        cum = jnp.dot(_lower_tri(tm), ls, precision=HI, preferred_element_type=f32) + carry[...]
        cum_ref[...] = cum * LOG2E
        carry[...] += jnp.sum(ls, axis=0, keepdims=True)

    hspec = pl.BlockSpec((NH, tm, DH), lambda i: (0, i, 0))
    tspec = pl.BlockSpec((W, tm), lambda i: (0, i))
    return _pcall(
        body, name="attn_prep", grid=(T // tm,),
        in_specs=[_zblock(tm, c0), _zblock(tm, c0 + 1), _zblock(tm, c0 + 2),
                  pl.BlockSpec((tm, 128), lambda i: (i, OFF_F // 128)),
                  pl.BlockSpec((1, 128), lambda i: (0, 0)),
                  pl.BlockSpec((1, W), lambda i: (0, 0)), pl.BlockSpec((1, W), lambda i: (0, 0))],
        out_specs=[tspec, tspec, tspec, hspec, hspec, pl.BlockSpec((tm, 128), lambda i: (i, 0))],
        out_shape=[jax.ShapeDtypeStruct((W, T), bf16)] * 3 + [jax.ShapeDtypeStruct((NH, T, DH), bf16)] * 2
        + [jax.ShapeDtypeStruct((T, 128), f32)],
        scratch_shapes=[pltpu.VMEM((1, 128), f32)],
        compiler_params=_params("arbitrary"),
    )(z, z, z, z, fbias, gq, gk)


HP = 2


def _causal_pairs(nq, key_major):
    if key_major:
        pairs = [(qi, ki) for ki in range(nq) for qi in range(ki, nq)]
    else:
        pairs = [(qi, ki) for qi in range(nq) for ki in range(qi + 1)]
    return (jnp.asarray([p[0] for p in pairs], jnp.int32), jnp.asarray([p[1] for p in pairs], jnp.int32))


def _head_rows(rows, n):
    return jnp.concatenate([jnp.broadcast_to(r, (DH, n)) for r in rows], axis=0)


def _attn_fwd(qt, kh, vt, crow, ccol, bq):
    T = qt.shape[1]
    nq = T // bq
    bk = bq
    qs, ks = _causal_pairs(nq, key_major=False)
    BW = HP * DH

    def body(qs_ref, ks_ref, qt_ref, k_ref, vt_ref, cr_ref, cc_ref, o_ref, lse_ref, m_s, l_s, acc_s):
        i = pl.program_id(1)
        qi, ki = qs_ref[i], ks_ref[i]

        @pl.when(ki == 0)
        def _():
            m_s[...] = jnp.full_like(m_s, MASK_VALUE)
            l_s[...] = jnp.zeros_like(l_s)
            acc_s[...] = jnp.zeros_like(acc_s)

        def step(diagonal):
            for h in range(HP):
                rows = slice(h * DH, (h + 1) * DH)
                s = _mm(k_ref[h], qt_ref[rows, :]) - cc_ref[h]
                if diagonal:
                    s = jnp.where(_iota2((bk, bq), 0) <= _iota2((bk, bq), 1), s, MASK_VALUE)
                cr = cr_ref[h]
                m_old = m_s[h]
                m_new = jnp.maximum(m_old, jnp.max(s, axis=0, keepdims=True) + cr)
                p = jnp.exp2(s + (cr - m_new))
                alpha = jnp.exp2(m_old - m_new)
                l_s[h] = alpha * l_s[h] + jnp.sum(p, axis=0, keepdims=True)
                acc_s[rows, :] = alpha * acc_s[rows, :] + _mm(vt_ref[rows, :], p.astype(bf16))
                m_s[h] = m_new

        @pl.when(ki < qi)
        def _():
            step(False)

        @pl.when(ki == qi)
        def _():
            step(True)
            o_ref[...] = (acc_s[...] / _head_rows([l_s[h] for h in range(HP)], bq)).T
            for h in range(HP):
                lse_ref[h] = m_s[h] + jnp.log(l_s[h]) * LOG2E

    qcol = lambda hp, i, qs, ks: (hp, qs[i])
    kcol = lambda hp, i, qs, ks: (hp, ks[i])
    qrow = lambda hp, i, qs, ks: (hp, 0, qs[i])
    return _pcall(
        body, name="attn_fwd",
        grid_spec=pltpu.PrefetchScalarGridSpec(
            num_scalar_prefetch=2, grid=(NH // HP, qs.shape[0]),
            in_specs=[pl.BlockSpec((BW, bq), qcol),
                      pl.BlockSpec((HP, bk, DH), lambda hp, i, qs, ks: (hp, ks[i], 0)),
                      pl.BlockSpec((BW, bk), kcol),
                      pl.BlockSpec((HP, 1, bq), qrow),
                      pl.BlockSpec((HP, bk, 1), lambda hp, i, qs, ks: (hp, ks[i], 0))],
            out_specs=[pl.BlockSpec((bq, BW), lambda hp, i, qs, ks: (qs[i], hp)),
                       pl.BlockSpec((HP, 1, bq), qrow)],
            scratch_shapes=[pltpu.VMEM((HP, 1, bq), f32), pltpu.VMEM((HP, 1, bq), f32),
                            pltpu.VMEM((BW, bq), f32)]),
        out_shape=[jax.ShapeDtypeStruct((T, W), f32), jax.ShapeDtypeStruct((NH, 1, T), f32)],
        compiler_params=_params("parallel", "arbitrary"),
    )(qs, ks, qt, kh, vt, crow, ccol)


def _attn_bwd_prep(dy, oh, z, tm):
    T = dy.shape[0]
    cg = OFF_ATT // W + 3

    def body(dy_ref, o_ref, g_ref, dot_ref, dl_ref):
        do = (dy_ref[...] * _silu(g_ref[...])).astype(bf16)
        dot_ref[...] = do.astype(f32).T.astype(bf16)
        prod = (do.astype(f32) * o_ref[...]).T
        for h in range(NH):
            dl_ref[h] = jnp.sum(prod[h * DH:(h + 1) * DH, :], axis=0, keepdims=True)

    return _pcall(
        body, name="attn_bwd_prep", grid=(T // tm,),
        in_specs=[pl.BlockSpec((tm, W), lambda i: (i, 0)),
                  pl.BlockSpec((tm, W), lambda i: (i, 0)),
                  _zblock(tm, cg)],
        out_specs=[pl.BlockSpec((W, tm), lambda i: (0, i)),
                   pl.BlockSpec((NH, 1, tm), lambda i: (0, 0, i))],
        out_shape=[jax.ShapeDtypeStruct((W, T), bf16), jax.ShapeDtypeStruct((NH, 1, T), f32)],
        compiler_params=_params("parallel"),
    )(dy, oh, z)


def _slab_exchange_phases(buf, land, row0, rows, send_sems, recv_sems, local_sem):
    me = _my_id()

    def local():
        return pltpu.make_async_copy(buf.at[me, pl.ds(row0, rows), :], land.at[me], local_sem)

    def remote(k, receive):
        peer, pid = _peer(k)
        return pltpu.make_async_remote_copy(
            src_ref=buf.at[pid, pl.ds(row0, rows), :], dst_ref=land.at[pid] if receive else land.at[me],
            send_sem=send_sems.at[k - 1], recv_sem=recv_sems.at[k - 1],
            device_id=peer, device_id_type=pl.DeviceIdType.MESH)

    def start():
        local().start()
        for k in range(1, N_DEV):
            remote(k, False).start()

    def finish():
        for k in range(1, N_DEV):
            remote(k, True).wait_recv()
        for k in range(1, N_DEV):
            remote(k, False).wait_send()
        local().wait()

    return start, finish


def _attn_bwd(qt, kt, kh, vh, crow, ccol, dot, lse, delta, bq, exchange=None):
    T = qt.shape[1]
    nq = T // bq
    bk = bq
    qs, ks = _causal_pairs(nq, key_major=True)
    BW = HP * DH
    nx = 0 if exchange is None else 1

    def body(qs_ref, ks_ref, qt_ref, kt_ref, k_ref, v_ref, cr_ref, cc_ref, dot_ref, lse_ref, dl_ref, *rest):
        dq_ref, dk_ref, dv_ref, dck_ref, dcq_ref = rest[nx:nx + 5]
        dq_s, dk_s, dv_s, dck_s = rest[2 * nx + 5:2 * nx + 9]
        i = pl.program_id(1)
        qi, ki = qs_ref[i], ks_ref[i]
        if nx:
            start, finish = _slab_exchange_phases(rest[0], rest[nx + 5], exchange[1], exchange[2], *rest[2 * nx + 9:])
            pl.when((pl.program_id(0) == 0) & (i == 0))(start)

        @pl.when(i == 0)
        def _():
            dq_s[...] = jnp.zeros_like(dq_s)
            dcq_ref[...] = jnp.zeros_like(dcq_ref)

        @pl.when(qi == ki)
        def _():
            dk_s[...] = jnp.zeros_like(dk_s)
            dv_s[...] = jnp.zeros_like(dv_s)
            dck_s[...] = jnp.zeros_like(dck_s)

        def step(diagonal):
            colsums = []
            for h in range(HP):
                rows = slice(h * DH, (h + 1) * DH)
                qth, doth = qt_ref[rows, :], dot_ref[rows, :]
                p = jnp.exp2(_mm(k_ref[h], qth) + (cr_ref[h] - lse_ref[h]) - cc_ref[h])
                if diagonal:
                    p = jnp.where(_iota2((bk, bq), 0) <= _iota2((bk, bq), 1), p, 0.0)
                dv_s[rows, :] += _mm_nt(doth, p.astype(bf16))
                ds = p * (_mm(v_ref[h], doth) - dl_ref[h])
                dsb = ds.astype(bf16)
                dk_s[rows, :] += _mm_nt(qth, dsb)
                dq_s[qi, rows, :] += _mm(kt_ref[rows, :], dsb)
                part = ds[:, 0:128]
                for c in range(1, bq // 128):
                    part = part + ds[:, c * 128:(c + 1) * 128]
                dck_s[h] += part
                colsums.append(jnp.sum(ds, axis=0, keepdims=True))
            dcq_ref[qi] += _stack_rows(colsums, bq)

        @pl.when(qi > ki)
        def _():
            step(False)

        @pl.when(qi == ki)
        def _():
            step(True)

        @pl.when(qi == nq - 1)
        def _():
            dk_ref[...] = (dk_s[...] * (1.0 / LOG2E)).T
            dv_ref[...] = dv_s[...].T
            lane = _iota2((bk, 128), 1)
            out = jnp.zeros((bk, 128), f32)
            for h in range(HP):
                out = out - jnp.where(lane == pl.program_id(0) * HP + h,
                                      jnp.sum(dck_s[h], axis=1, keepdims=True), 0.0)
            dck_ref[...] = out

        @pl.when(i == qs.shape[0] - 1)
        def _():
            for qb in range(nq):
                dq_ref[qb * bq:(qb + 1) * bq, :] = dq_s[qb].T

        if nx:
            pl.when((pl.program_id(0) == NH // HP - 1) & (i == qs.shape[0] - 1))(finish)

    qcol = lambda hp, i, qs, ks: (hp, qs[i])
    kcol = lambda hp, i, qs, ks: (hp, ks[i])
    qrow = lambda hp, i, qs, ks: (hp, 0, qs[i])
    kh_spec = pl.BlockSpec((HP, bk, DH), lambda hp, i, qs, ks: (hp, ks[i], 0))
    hbm = pl.BlockSpec(memory_space=pltpu.HBM)
    extra_in = [exchange[0]] if nx else []
    extra_out = [jax.ShapeDtypeStruct((N_DEV, exchange[2], exchange[0].shape[2]), exchange[0].dtype)] if nx else []
    extra_scratch = [pltpu.SemaphoreType.DMA((N_DEV - 1,)), pltpu.SemaphoreType.DMA((N_DEV - 1,)),
                     pltpu.SemaphoreType.DMA] if nx else []
    return _pcall(
        body, name="attn_bwd_exchange" if nx else "attn_bwd",
        grid_spec=pltpu.PrefetchScalarGridSpec(
            num_scalar_prefetch=2, grid=(NH // HP, qs.shape[0]),
            in_specs=[pl.BlockSpec((BW, bq), qcol), pl.BlockSpec((BW, bk), kcol), kh_spec, kh_spec,
                      pl.BlockSpec((HP, 1, bq), qrow),
                      pl.BlockSpec((HP, bk, 1), lambda hp, i, qs, ks: (hp, ks[i], 0)),
                      pl.BlockSpec((BW, bq), qcol), pl.BlockSpec((HP, 1, bq), qrow), pl.BlockSpec((HP, 1, bq), qrow)]
            + [hbm] * nx,
            out_specs=[pl.BlockSpec((T, BW), lambda hp, i, qs, ks: (0, hp)),
                       pl.BlockSpec((bk, BW), lambda hp, i, qs, ks: (ks[i], hp)),
                       pl.BlockSpec((bk, BW), lambda hp, i, qs, ks: (ks[i], hp)),
                       pl.BlockSpec((None, bk, 128), lambda hp, i, qs, ks: (hp, ks[i], 0)),
                       pl.BlockSpec((None, nq, 8, bq), lambda hp, i, qs, ks: (hp, 0, 0, 0))] + [hbm] * nx,
            scratch_shapes=[pltpu.VMEM((nq, BW, bq), f32), pltpu.VMEM((BW, bk), f32), pltpu.VMEM((BW, bk), f32),
                            pltpu.VMEM((HP, bk, 128), f32)] + extra_scratch),
        out_shape=[jax.ShapeDtypeStruct((T, W), f32)] * 3 + [jax.ShapeDtypeStruct((NH // HP, T, 128), f32),
                                                             jax.ShapeDtypeStruct((NH // HP, nq, 8, bq), f32)]
        + extra_out,
        compiler_params=_params("arbitrary" if nx else "parallel", "arbitrary"),
    )(qs, ks, qt, kt, kh, vh, crow, ccol, dot, lse, delta, *extra_in)


def _attn_post(z, dy, oh, dqh, dkh, dvh, dck, dcq, fbias, gq, gk, dzbuf, tm):
    T = z.shape[0]
    c0 = OFF_ATT // W
    nt = T // tm

    def body(q_ref, k_ref, g_ref, f_ref, dy_ref, o_ref, dq_ref, dk_ref, dv_ref, dck_ref, dcq_ref, fb_ref, gq_ref,
             gk_ref, dzin_ref, dz_ref, ggq_ref, ggk_ref, gfb_ref, carry):
        @pl.when(pl.program_id(0) == 0)
        def _():
            carry[...] = jnp.zeros_like(carry)
            ggq_ref[...] = jnp.zeros_like(ggq_ref)
            ggk_ref[...] = jnp.zeros_like(ggk_ref)
            gfb_ref[...] = jnp.zeros_like(gfb_ref)

        gm = _group_mean_matrix(W, DH)
        hs = jnp.where((_iota2((W, W), 0) & (DH - 1)) == (_iota2((W, W), 1) & (DH - 1)), 1.0, 0.0).astype(f32)

        def norm_bwd(x, dn, gain):
            r = lax.rsqrt(_group_mean(x * x, gm) + EPS)
            gg = jnp.sum(dn * x * r, axis=0, keepdims=True)
            u = dn * gain
            return r * u - x * (r * r * r) * _group_mean(u * x, gm), gg

        q, k, gate = q_ref[...], k_ref[...], g_ref[...]
        dq, ggq = norm_bwd(q, dq_ref[...] * (DH ** -0.5), gq_ref[...])
        dk, ggk = norm_bwd(k, dk_ref[...], gk_ref[...])
        ggq_ref[...] += jnp.dot(jnp.broadcast_to(ggq, (8, W)), hs, precision=HI, preferred_element_type=f32)[0:1]
        ggk_ref[...] += jnp.dot(jnp.broadcast_to(ggk, (8, W)), hs, precision=HI, preferred_element_type=f32)[0:1]
        dgate = dy_ref[...] * o_ref[...] * _dsilu(gate)
        dck_v = dcq_ref[...]
        for hp in range(NH // HP):
            dck_v = dck_v + dck_ref[hp]
        rc = jnp.dot(_upper_tri(tm), dck_v, precision=HI, preferred_element_type=f32) + carry[...]
        carry[...] += jnp.sum(dck_v, axis=0, keepdims=True)
        f = f_ref[...] + fb_ref[...]
        df = jnp.where(_iota2((tm, 128), 1) < NH, rc * _sigmoid(-f), 0.0)
        gfb_ref[...] += jnp.sum(df, axis=0, keepdims=True)
        dz_ref[:, 0:W] = dq.astype(bf16)
        dz_ref[:, W:2 * W] = dk.astype(bf16)
        dz_ref[:, 2 * W:3 * W] = dv_ref[...].astype(bf16)
        dz_ref[:, 3 * W:4 * W] = dgate.astype(bf16)
        dz_ref[:, 4 * W:4 * W + 128] = df.astype(bf16)

    rev = lambda i: nt - 1 - i
    zb = lambda col: pl.BlockSpec((tm, W), lambda i, c=col: (rev(i), c))
    hspec = pl.BlockSpec((tm, W), lambda i: (rev(i), 0))
    return _pcall(
        body, name="attn_post", grid=(nt,),
        in_specs=[zb(c0), zb(c0 + 1), zb(c0 + 3),
                  pl.BlockSpec((tm, 128), lambda i: (rev(i), OFF_F // 128)),
                  pl.BlockSpec((tm, W), lambda i: (rev(i), 0)),
                  hspec, hspec, hspec, hspec,
                  pl.BlockSpec((NH // HP, tm, 128), lambda i: (0, rev(i), 0)),
                  pl.BlockSpec((tm, 128), lambda i: (rev(i), 0)),
                  pl.BlockSpec((1, 128), lambda i: (0, 0)),
                  pl.BlockSpec((1, W), lambda i: (0, 0)), pl.BlockSpec((1, W), lambda i: (0, 0)),
                  pl.BlockSpec(memory_space=pl.ANY)],
        out_specs=[pl.BlockSpec((tm, 4 * W + 128), lambda i: (rev(i), OFF_ATT // (4 * W + 128))),
                   pl.BlockSpec((1, W), lambda i: (0, 0)), pl.BlockSpec((1, W), lambda i: (0, 0)),
                   pl.BlockSpec((1, 128), lambda i: (0, 0))],
        out_shape=[jax.ShapeDtypeStruct((T, NZ), bf16), jax.ShapeDtypeStruct((1, W), f32),
                   jax.ShapeDtypeStruct((1, W), f32), jax.ShapeDtypeStruct((1, 128), f32)],
        scratch_shapes=[pltpu.VMEM((1, 128), f32)],
        input_output_aliases={14: 0},
        compiler_params=_params("arbitrary"),
    )(z, z, z, z, dy, oh, dqh, dkh, dvh, dck, dcq, fbias, gq, gk, dzbuf)


def _merge_fwd(ya, oh, z, yc, yd, mb, x, p, wup, wo, gp, wpg, wpp, tm):
    T = x.shape[0]
    cg = OFF_ATT // W + 3

    def body(ya_ref, oh_ref, bg_ref, yc_ref, yd_ref, ml_ref, mb_ref, x_ref, p_ref, wup_ref, wo_ref, gp_ref,
             wpg_ref, wpp_ref, yb_ref, mg_ref, x1_ref, x2_ref):
        yb = (oh_ref[...] * _silu(bg_ref[...])).astype(bf16)
        yb_ref[...] = yb
        ys = (ya_ref[...], yb, yc_ref[...], yd_ref[...])
        merged = jnp.zeros((tm, D), f32)
        for b in range(NBR):
            sg = _sigmoid(ml_ref[:, b * D:(b + 1) * D] + mb_ref[b:b + 1, :])
            merged = merged + sg * _mm(ys[b], wup_ref[b])
        mgb = merged.astype(bf16)
        mg_ref[...] = mgb
        x1 = x_ref[...] + _mm(mgb, wo_ref[...])
        x1_ref[...] = x1
        r = lax.rsqrt(jnp.mean(x1 * x1, axis=-1, keepdims=True) + EPS)
        hp = (x1 * r * gp_ref[...]).astype(bf16)
        gate = _sigmoid(_mm(hp, wpg_ref[...]))
        x2_ref[...] = x1 + gate * _mm(p_ref[...].astype(bf16), wpp_ref[...])

    row = lambda width: pl.BlockSpec((tm, width), lambda i: (i, 0))
    full = lambda *shape: pl.BlockSpec(shape, lambda i: (0,) * len(shape))
    return _pcall(
        body, name="merge_fwd", grid=(T // tm,),
        in_specs=[row(W), row(W), _zblock(tm, cg), row(W), row(W),
                  pl.BlockSpec((tm, NBR * D), lambda i: (i, 0)), full(NBR, D), row(D), row(PLE),
                  full(NBR, W, D), full(D, D), full(1, D), full(D, D), full(PLE, D)],
        out_specs=[row(W), row(D), row(D), row(D)],
        out_shape=[jax.ShapeDtypeStruct((T, W), bf16), jax.ShapeDtypeStruct((T, D), bf16),
                   jax.ShapeDtypeStruct((T, D), f32), jax.ShapeDtypeStruct((T, D), f32)],
        compiler_params=_params("parallel"),
    )(ya, oh, z, yc, yd, z, mb, x, p, wup, wo, gp, wpg, wpp)


def _layer_slabs(li, bufs):
    if bufs is None:
        return [], []
    return list(bufs), [pl.BlockSpec(memory_space=pl.ANY)] * len(bufs)


def _ple_bwd(dx2, x1, p, gp, wpg, wpp, tm, li, bufs):
    T = x1.shape[0]
    SH = D // N_DEV
    nt = T // tm
    extra, extra_specs = _layer_slabs(li, bufs)

    def body(dx2_ref, x1_ref, p_ref, gp_ref, wpg_ref, wpp_ref, *rest):
        dx1_ref, gwpg_ref, gwpp_ref, ggp_ref, gwpg_acc, gwpp_acc = rest[len(extra):]

        @pl.when(pl.program_id(0) == 0)
        def _():
            gwpg_acc[...] = jnp.zeros_like(gwpg_acc)
            gwpp_acc[...] = jnp.zeros_like(gwpp_acc)
            ggp_ref[...] = jnp.zeros_like(ggp_ref)

        x1, dx2 = x1_ref[...], dx2_ref[...]
        r = lax.rsqrt(jnp.mean(x1 * x1, axis=-1, keepdims=True) + EPS)
        xh = x1 * r
        gp = gp_ref[...]
        hp = (xh * gp).astype(bf16)
        gate = _sigmoid(_mm(hp, wpg_ref[...]))
        pb = p_ref[...].astype(bf16)
        pp = _mm(pb, wpp_ref[...])
        dpre = ((dx2 * pp) * gate * (1.0 - gate)).astype(bf16)
        gwpp_acc[...] += _mm_tn(pb, (dx2 * gate).astype(bf16))
        gwpg_acc[...] += _mm_tn(hp, dpre)
        dhp = _mm_nt(dpre, wpg_ref[...])
        ggp_ref[...] += jnp.sum(dhp * xh, axis=0, keepdims=True)
        u = dhp * gp
        dx1_ref[...] = dx2 + r * u - x1 * (r * r * r) * jnp.mean(u * x1, axis=-1, keepdims=True)

        @pl.when(pl.program_id(0) == nt - 1)
        def _():
            gwpg_ref[...] = gwpg_acc[...].reshape(N_DEV, SH, D).astype(bf16)
            for d in range(N_DEV):
                gwpp_ref[d] = gwpp_acc[:, d * SH:(d + 1) * SH].astype(bf16)

    row = lambda width: pl.BlockSpec((tm, width), lambda i: (i, 0))
    full = lambda *shape: pl.BlockSpec(shape, lambda i: (0,) * len(shape))
    n_in = 6
    return _pcall(
        body, name="ple_bwd", grid=(nt,),
        in_specs=[row(D), row(D), row(PLE), full(1, D), full(D, D), full(PLE, D)] + extra_specs,
        out_specs=[row(D), pl.BlockSpec((N_DEV, SH, D), lambda i: (0, li, 0)),
                   pl.BlockSpec((N_DEV, PLE, SH), lambda i: (0, li, 0)), full(1, D)],
        out_shape=[jax.ShapeDtypeStruct((T, D), f32), jax.ShapeDtypeStruct((N_DEV, DEPTH * SH, D), bf16),
                   jax.ShapeDtypeStruct((N_DEV, DEPTH * PLE, SH), bf16), jax.ShapeDtypeStruct((1, D), f32)],
        scratch_shapes=[pltpu.VMEM((D, D), f32), pltpu.VMEM((PLE, D), f32)],
        input_output_aliases={n_in + k: 1 + k for k in range(len(extra))},
        compiler_params=_params("arbitrary"),
    )(dx2, x1, p, gp, wpg, wpp, *extra)


def _merge_bwd(dx1, mg, ya, yb, yc, yd, z, mb, wup, wo, tm, li, bufs):
    T = dx1.shape[0]
    SH = D // N_DEV
    nt = T // tm
    extra, extra_specs = _layer_slabs(li, bufs)

    def body(dx1_ref, mg_ref, ya_ref, yb_ref, yc_ref, yd_ref, ml_ref, mb_ref, wup_ref, wo_ref, *rest):
        dml_ref, dya_ref, dyb_ref, dyc_ref, dyd_ref, gwo_ref, gwup_ref, gmb_ref, gwo_acc, gwup_acc = rest[len(extra):]

        @pl.when(pl.program_id(0) == 0)
        def _():
            gwo_acc[...] = jnp.zeros_like(gwo_acc)
            gwup_acc[...] = jnp.zeros_like(gwup_acc)
            gmb_ref[...] = jnp.zeros_like(gmb_ref)

        dx1b = dx1_ref[...].astype(bf16)
        gwo_acc[...] += _mm_tn(mg_ref[...], dx1b)
        dm = _mm_nt(dx1b, wo_ref[...])
        ys = (ya_ref, yb_ref, yc_ref, yd_ref)
        dys = (dya_ref, dyb_ref, dyc_ref, dyd_ref)
        for b in range(NBR):
            y = ys[b][...]
            up = _mm(y, wup_ref[b])
            sg = _sigmoid(ml_ref[:, b * D:(b + 1) * D] + mb_ref[b:b + 1, :])
            dup = (dm * sg).astype(bf16)
            dml = dm * up * sg * (1.0 - sg)
            gmb_ref[b:b + 1, :] += jnp.sum(dml, axis=0, keepdims=True)
            dml_ref[:, b * D:(b + 1) * D] = dml.astype(bf16)
            gwup_acc[b] += _mm_tn(y, dup)
            dys[b][...] = _mm_nt(dup, wup_ref[b])

        @pl.when(pl.program_id(0) == nt - 1)
        def _():
            gwo_ref[...] = gwo_acc[...].reshape(N_DEV, SH, D).astype(bf16)
            for d in range(N_DEV):
                gwup_ref[d] = gwup_acc[:, :, d * SH:(d + 1) * SH].reshape(NBR * W, SH).astype(bf16)

    row = lambda width: pl.BlockSpec((tm, width), lambda i: (i, 0))
    full = lambda *shape: pl.BlockSpec(shape, lambda i: (0,) * len(shape))
    n_in = 10
    return _pcall(
        body, name="merge_bwd", grid=(nt,),
        in_specs=[row(D), row(D), row(W), row(W), row(W), row(W), row(NBR * D), full(NBR, D),
                  full(NBR, W, D), full(D, D)] + extra_specs,
        out_specs=[row(NBR * D), row(W), row(W), row(W), row(W),
                   pl.BlockSpec((N_DEV, SH, D), lambda i: (0, li, 0)),
                   pl.BlockSpec((N_DEV, NBR * W, SH), lambda i: (0, li, 0)), full(NBR, D)],
        out_shape=[jax.ShapeDtypeStruct((T, NZ), bf16)] + [jax.ShapeDtypeStruct((T, W), f32)] * 4
        + [jax.ShapeDtypeStruct((N_DEV, DEPTH * SH, D), bf16),
           jax.ShapeDtypeStruct((N_DEV, DEPTH * NBR * W, SH), bf16),
           jax.ShapeDtypeStruct((NBR, D), f32)],
        scratch_shapes=[pltpu.VMEM((D, D), f32), pltpu.VMEM((NBR, W, D), f32)],
        input_output_aliases={n_in + k: 5 + k for k in range(len(extra))},
        compiler_params=_params("arbitrary"),
    )(dx1, mg, ya, yb, yc, yd, z, mb, wup, wo, *extra)


def _loss_head(y, target, tm):
    T = y.shape[0]

    def body(y_ref, t_ref, loss_ref, dy_ref, acc):
        i = pl.program_id(0)

        @pl.when(i == 0)
        def _():
            acc[...] = jnp.zeros_like(acc)

        e = y_ref[...] - t_ref[...]
        dy_ref[...] = e * (1.0 / D)
        acc[...] += jnp.sum(e * e, axis=0, keepdims=True)

        @pl.when(i == T // tm - 1)
        def _():
            loss_ref[...] = jnp.sum(acc[...], axis=1, keepdims=True) * (0.5 / D)

    return _pcall(
        body, name="loss_head", grid=(T // tm,),
        in_specs=[pl.BlockSpec((tm, D), lambda i: (i, 0)), pl.BlockSpec((tm, D), lambda i: (i, 0))],
        out_specs=[pl.BlockSpec((1, 1), lambda i: (0, 0)), pl.BlockSpec((tm, D), lambda i: (i, 0))],
        out_shape=[jax.ShapeDtypeStruct((1, 1), f32), jax.ShapeDtypeStruct((T, D), f32)],
        scratch_shapes=[pltpu.VMEM((1, D), f32)],
        compiler_params=_params("arbitrary"),
    )(y, target)


def _lb_softmax_rows(l_ref):
    rows = [l_ref[i:i + 1, :] for i in range(DEPTH)]
    m = rows[0]
    for r in rows[1:]:
        m = jnp.maximum(m, r)
    es = [jnp.exp(r - m) for r in rows]
    tot = es[0]
    for e in es[1:]:
        tot = tot + e
    return [e / tot for e in es]


def _lb_partial_sums(pr):
    sums = [jnp.zeros_like(pr[0])]
    for i in range(1, DEPTH):
        sums.append(sums[-1] + pr[i])
    return sums


def _stack_rows(rows, width):
    idx = _iota2((8, width), 0)
    out = jnp.zeros((8, width), f32)
    for i, r in enumerate(rows):
        out = jnp.where(idx == i, r, out)
    return out


def _lower_bounds(lb_logits):
    def body(l_ref, o_ref):
        sums = _lb_partial_sums(_lb_softmax_rows(l_ref))
        o_ref[...] = _stack_rows([jnp.clip(s, 0.0, 1.0) for s in sums], W)

    return _pcall(body, name="lower_bounds", out_shape=jax.ShapeDtypeStruct((8, W), f32))(lb_logits)


def _lower_bounds_bwd(lb_logits, dlower):
    def body(l_ref, d_ref, o_ref):
        pr = _lb_softmax_rows(l_ref)
        sums = _lb_partial_sums(pr)
        dl = [jnp.where((sums[i] > 0.0) & (sums[i] < 1.0), d_ref[i:i + 1, :], 0.0) for i in range(DEPTH)]
        dp = [jnp.zeros_like(pr[0])] * DEPTH
        run = jnp.zeros_like(pr[0])
        for j in reversed(range(1, DEPTH)):
            run = run + dl[j]
            dp[j] = run
        inner = pr[0] * dp[0]
        for j in range(1, DEPTH):
            inner = inner + pr[j] * dp[j]
        o_ref[...] = _stack_rows([pr[j] * (dp[j] - inner) for j in range(DEPTH)], W)

    return _pcall(body, name="lower_bounds_bwd", out_shape=jax.ShapeDtypeStruct((8, W), f32))(lb_logits, dlower)


def _row_tile(rows, cols, budget_bytes=1 << 20, mult=8):
    if rows % mult:
        return rows
    best = mult
    for t in range(mult, rows + 1, mult):
        if rows % t == 0 and t * cols * 4 <= budget_bytes:
            best = t
    return best


def _sum_slabs(land):
    N, R, C = land.shape
    tr = _row_tile(R, C * N, mult=16)

    def body(l_ref, o_ref):
        acc = l_ref[0].astype(f32)
        for j in range(1, N):
            acc = acc + l_ref[j].astype(f32)
        o_ref[...] = acc

    return _pcall(
        body, name="sum_slabs", grid=(R // tr,),
        in_specs=[pl.BlockSpec((N, tr, C), lambda i: (0, i, 0))],
        out_specs=pl.BlockSpec((tr, C), lambda i: (i, 0)),
        out_shape=jax.ShapeDtypeStruct((R, C), f32),
        compiler_params=_params("parallel"),
    )(land)


def _adamw_update(w_ref, g_ref, m_ref, v_ref, d_ref, nm_ref, nv_ref):
    c1 = 1.0 / (1.0 - ADAM_B1 ** ADAM_STEP)
    c2 = 1.0 / (1.0 - ADAM_B2 ** ADAM_STEP)
    gv = g_ref[...]
    nm = ADAM_B1 * m_ref[...] + (1.0 - ADAM_B1) * gv
    nv = ADAM_B2 * v_ref[...] + (1.0 - ADAM_B2) * (gv * gv)
    nm_ref[...] = nm
    nv_ref[...] = nv
    d_ref[...] = -ADAM_LR * ((nm * c1) / (jnp.sqrt(nv * c2) + ADAM_EPS) + ADAM_WD * w_ref[...])


def _adamw3(w, g, m, v):
    L, R, C = w.shape
    tr = _row_tile(R, C)

    def body(*refs):
        _adamw_update(*refs)

    spec = pl.BlockSpec((None, tr, C), lambda l, i: (l, i, 0))
    return _pcall(
        body, name="adamw3", grid=(L, R // tr),
        in_specs=[spec] * 4, out_specs=[spec] * 3,
        out_shape=[jax.ShapeDtypeStruct((L, R, C), f32)] * 3,
        compiler_params=_params("parallel", "parallel"),
    )(w, g, m, v)


def _adamw(w, g, m, v):
    if w.ndim == 3:
        return _adamw3(w, g, m, v)
    R, C = w.shape
    tr = _row_tile(R, C)
    c1 = 1.0 / (1.0 - ADAM_B1 ** ADAM_STEP)
    c2 = 1.0 / (1.0 - ADAM_B2 ** ADAM_STEP)

    def body(w_ref, g_ref, m_ref, v_ref, d_ref, nm_ref, nv_ref):
        gv = g_ref[...]
        nm = ADAM_B1 * m_ref[...] + (1.0 - ADAM_B1) * gv
        nv = ADAM_B2 * v_ref[...] + (1.0 - ADAM_B2) * (gv * gv)
        nm_ref[...] = nm
        nv_ref[...] = nv
        d_ref[...] = -ADAM_LR * ((nm * c1) / (jnp.sqrt(nv * c2) + ADAM_EPS) + ADAM_WD * w_ref[...])

    spec = pl.BlockSpec((tr, C), lambda i: (i, 0))
    return _pcall(
        body, name="adamw", grid=(R // tr,),
        in_specs=[spec] * 4, out_specs=[spec] * 3,
        out_shape=[jax.ShapeDtypeStruct((R, C), f32)] * 3,
        compiler_params=_params("parallel"),
    )(w, g, m, v)


def _my_id():
    return lax.axis_index("x") * 4 + lax.axis_index("y") * 2 + lax.axis_index("c")


def _peer(k):
    x, y, c = lax.axis_index("x"), lax.axis_index("y"), lax.axis_index("c")
    kx, ky, kc = (k >> 2) & 1, (k >> 1) & 1, k & 1
    px, py, pc = x ^ kx, y ^ ky, c ^ kc
    return (px, py, pc), px * 4 + py * 2 + pc


def _all_gather(shards, axes):
    n = len(shards)

    def body(*refs):
        start, forward, finish = _gather_phases(shards, axes, refs[:n], refs[n:2 * n], *refs[2 * n:])
        start()
        forward()
        finish()

    hbm = pl.BlockSpec(memory_space=pltpu.HBM)
    return _pcall(
        body, name="all_gather",
        in_specs=[hbm] * n, out_specs=[hbm] * n,
        out_shape=_gathered_shapes(shards, axes),
        scratch_shapes=_gather_semaphores(n),
    )(*shards)


def _gathered_shapes(shards, axes):
    def full_shape(s, ax):
        shp = list(s.shape)
        shp[ax] *= N_DEV
        return tuple(shp)

    return [jax.ShapeDtypeStruct(full_shape(s, ax), s.dtype) for s, ax in zip(shards, axes)]


def _gather_semaphores(n):
    return [pltpu.SemaphoreType.DMA((n, N_DEV - 1)), pltpu.SemaphoreType.DMA((n, N_DEV - 1)),
            pltpu.SemaphoreType.DMA((n,))]


def _gather_phases(shards, axes, srcs, outs, send_sems, recv_sems, local_sems):
    n = len(shards)
    x, y, c = lax.axis_index("x"), lax.axis_index("y"), lax.axis_index("c")
    me, sibling = (x, y, c), (x, y, 1 - c)
    chips = [(1 - x, y), (x, 1 - y), (1 - x, 1 - y)]

    def block(a, dev):
        j = dev[0] * 4 + dev[1] * 2 + dev[2]
        size = shards[a].shape[axes[a]]
        start = pl.multiple_of(j * size, size)
        if axes[a] == 0:
            return outs[a].at[pl.ds(start, size), :]
        if axes[a] == 1:
            return outs[a].at[:, pl.ds(start, size), :]
        return outs[a].at[:, pl.ds(start, size)]

    def copy(a, k, dev, to, src=None):
        return pltpu.make_async_remote_copy(
            src_ref=block(a, dev) if src is None else src, dst_ref=block(a, dev),
            send_sem=send_sems.at[a, k], recv_sem=recv_sems.at[a, k],
            device_id=to, device_id_type=pl.DeviceIdType.MESH)

    def mine():
        return [pltpu.make_async_copy(srcs[a], block(a, me), local_sems.at[a]) for a in range(n)]

    def first():
        cps = []
        for a in range(n):
            cps.append(copy(a, 0, me, sibling, src=srcs[a]))
            cps += [copy(a, 1 + j, me, (*chip, c), src=srcs[a]) for j, chip in enumerate(chips)]
        return cps

    def passed():
        return [copy(a, 4 + j, (*chip, c), sibling) for j, chip in enumerate(chips) for a in range(n)]

    def start():
        for cp in mine() + first():
            cp.start()

    def forward():
        for j, chip in enumerate(chips):
            for a in range(n):
                copy(a, 1 + j, (*chip, c), me).wait_recv()
                copy(a, 4 + j, (*chip, c), sibling).start()

    def finish():
        for a in range(n):
            copy(a, 0, sibling, me).wait_recv()
            for j, chip in enumerate(chips):
                copy(a, 4 + j, (*chip, 1 - c), me).wait_recv()
        for cp in first() + passed():
            cp.wait_send()
        for cp in mine():
            cp.wait()

    return start, forward, finish


N_CHIP = N_DEV // 2


def _exchange_sibling(sliced):
    n = len(sliced)

    def body(*refs):
        srcs, outs = refs[:n], refs[n:2 * n]
        send_sems, recv_sems = refs[2 * n:]
        x, y, c = lax.axis_index("x"), lax.axis_index("y"), lax.axis_index("c")
        copies = []
        for a in range(n):
            for q in range(N_CHIP):
                cp = pltpu.make_async_remote_copy(
                    src_ref=srcs[a].at[2 * q + (1 - c)], dst_ref=outs[a].at[q],
                    send_sem=send_sems.at[a, q], recv_sem=recv_sems.at[a, q],
                    device_id=(x, y, 1 - c), device_id_type=pl.DeviceIdType.MESH)
                cp.start()
                copies.append(cp)
        for cp in copies:
            cp.wait_recv()
        for cp in copies:
            cp.wait_send()

    hbm = pl.BlockSpec(memory_space=pltpu.HBM)
    return _pcall(
        body, name="grad_exchange_sibling",
        in_specs=[hbm] * n, out_specs=[hbm] * n,
        out_shape=[jax.ShapeDtypeStruct((N_CHIP,) + s.shape[1:], s.dtype) for s in sliced],
        scratch_shapes=[pltpu.SemaphoreType.DMA((n, N_CHIP)), pltpu.SemaphoreType.DMA((n, N_CHIP))],
    )(*sliced)


def _pair_sum(own, recv):
    _, R, C = own.shape
    tr = _row_tile(R, C, mult=16)
    side = lax.axis_index("c").astype(jnp.int32).reshape(1)

    def body(c_ref, own_ref, recv_ref, o_ref):
        o_ref[...] = (own_ref[...].astype(f32) + recv_ref[...].astype(f32)).astype(o_ref.dtype)

    return _pcall(
        body, name="pair_sum",
        grid_spec=pltpu.PrefetchScalarGridSpec(
            num_scalar_prefetch=1, grid=(N_CHIP, R // tr),
            in_specs=[pl.BlockSpec((None, tr, C), lambda q, i, c: (2 * q + c[0], i, 0)),
                      pl.BlockSpec((None, tr, C), lambda q, i, c: (q, i, 0))],
            out_specs=pl.BlockSpec((None, tr, C), lambda q, i, c: (q, i, 0))),
        out_shape=jax.ShapeDtypeStruct((N_CHIP, R, C), own.dtype),
        compiler_params=_params("parallel", "parallel"),
    )(side, own, recv)


def _exchange_chips(partial, whole):
    ns, nw = len(partial), len(whole)

    def body(*refs):
        srcs, outs = refs[:ns + nw], refs[ns + nw:2 * (ns + nw)]
        send_sems, recv_sems, wsend_sems, wrecv_sems, local_sems = refs[2 * (ns + nw):]
        x, y, c = lax.axis_index("x"), lax.axis_index("y"), lax.axis_index("c")
        me, myq = _my_id(), x * 2 + y
        chips = [(1 - x, y), (x, 1 - y), (1 - x, 1 - y)]
        locals_ = [pltpu.make_async_copy(srcs[a].at[myq], outs[a].at[myq], local_sems.at[a]) for a in range(ns)]
        locals_ += [pltpu.make_async_copy(srcs[ns + b], outs[ns + b].at[me], local_sems.at[ns + b])
                    for b in range(nw)]
        for cp in locals_:
            cp.start()
        sends, recvs = [], []
        for j, chip in enumerate(chips):
            q = chip[0] * 2 + chip[1]
            for a in range(ns):
                cp = pltpu.make_async_remote_copy(
                    src_ref=srcs[a].at[q], dst_ref=outs[a].at[myq],
                    send_sem=send_sems.at[a, j], recv_sem=recv_sems.at[a, j],
                    device_id=(*chip, c), device_id_type=pl.DeviceIdType.MESH)
                cp.start()
                sends.append(cp)
                recvs.append(pltpu.make_async_remote_copy(
                    src_ref=srcs[a].at[q], dst_ref=outs[a].at[q],
                    send_sem=send_sems.at[a, j], recv_sem=recv_sems.at[a, j],
                    device_id=(*chip, c), device_id_type=pl.DeviceIdType.MESH))
        for k in range(1, N_DEV):
            peer, pid = _peer(k)
            for b in range(nw):
                cp = pltpu.make_async_remote_copy(
                    src_ref=srcs[ns + b], dst_ref=outs[ns + b].at[me],
                    send_sem=wsend_sems.at[b, k - 1], recv_sem=wrecv_sems.at[b, k - 1],
                    device_id=peer, device_id_type=pl.DeviceIdType.MESH)
                cp.start()
                sends.append(cp)
                recvs.append(pltpu.make_async_remote_copy(
                    src_ref=srcs[ns + b], dst_ref=outs[ns + b].at[pid],
                    send_sem=wsend_sems.at[b, k - 1], recv_sem=wrecv_sems.at[b, k - 1],
                    device_id=peer, device_id_type=pl.DeviceIdType.MESH))
        for cp in recvs:
            cp.wait_recv()
        for cp in sends:
            cp.wait_send()
        for cp in locals_:
            cp.wait()

    hbm = pl.BlockSpec(memory_space=pltpu.HBM)
    shapes = [jax.ShapeDtypeStruct(s.shape, s.dtype) for s in partial]
    shapes += [jax.ShapeDtypeStruct((N_DEV,) + s.shape, s.dtype) for s in whole]
    return _pcall(
        body, name="grad_exchange_chips",
        in_specs=[hbm] * (ns + nw), out_specs=[hbm] * (ns + nw), out_shape=shapes,
        scratch_shapes=[pltpu.SemaphoreType.DMA((ns, N_CHIP - 1)), pltpu.SemaphoreType.DMA((ns, N_CHIP - 1)),
                        pltpu.SemaphoreType.DMA((nw, N_DEV - 1)), pltpu.SemaphoreType.DMA((nw, N_DEV - 1)),
                        pltpu.SemaphoreType.DMA((ns + nw,))],
    )(*partial, *whole)


def _permute_cols(w):
    pad = jnp.zeros(w.shape[:-1] + (NZ - OFF_F - NH,), w.dtype)
    return jnp.concatenate([
        w[..., 3844:7940],
        w[..., 0:1024],
        w[..., 2052:3076],
        w[..., 3076:3844],
        w[..., 1024:2048],
        w[..., 2048:2052], pad], axis=-1)


def _unpermute_cols(g):
    return jnp.concatenate([
        g[..., OFF_CONV:OFF_CONV + 1024],
        g[..., OFF_ATT:OFF_ATT + 1024],
        g[..., OFF_F:OFF_F + NH],
        g[..., OFF_HGRN:OFF_HGRN + 1024],
        g[..., OFF_SGU:OFF_SGU + 768],
        g[..., 0:4096]], axis=-1)


_SMALL = (
    ("norm_mix", (DEPTH, D)), ("conv_w", (DEPTH, CONV_WIDTH, W)), ("conv_b", (DEPTH, W)),
    ("fgate_bias", (DEPTH, NH)), ("q_norm", (DEPTH, DH)), ("k_norm", (DEPTH, DH)),
    ("lb_logits", (DEPTH, W)), ("hgrn_norm", (DEPTH, W)), ("sgu_norm", (DEPTH, W)),
    ("spatial_w", (DEPTH, NH, SGU_CHUNK, SGU_CHUNK)), ("spatial_b", (DEPTH, NH, SGU_CHUNK)),
    ("merge_b", (DEPTH, NBR, D)), ("norm_ple", (DEPTH, D)),
)


def _small_rows(shape):
    size = 1
    for s in shape:
        size *= s
    rows = -(-size // 128)
    return size, -(-rows // 8) * 8


def _pack_small(parts):
    out = []
    for name, shape in _SMALL:
        size, rows = _small_rows(shape)
        flat = parts[name].astype(f32).reshape(-1)
        flat = jnp.pad(flat, (0, rows * 128 - size))
        out.append(flat.reshape(rows, 128))
    return jnp.concatenate(out, axis=0)


def _unpack_small(buf):
    parts, r0 = {}, 0
    for name, shape in _SMALL:
        size, rows = _small_rows(shape)
        parts[name] = buf[r0:r0 + rows].reshape(-1)[:size].reshape(shape)
        r0 += rows
    return parts


def _shard_cols(a, width):
    return lax.dynamic_slice_in_dim(a, _my_id() * width, width, axis=a.ndim - 1)


def kernel(x, p, norm_mix, w_in, conv_w, conv_b, fgate_bias, q_norm, k_norm, lb_logits, hgrn_norm, sgu_norm, spatial_w, spatial_b, w_up, merge_b, w_o, norm_ple, w_ple_gate, w_ple_proj, loss_target, m_norm_mix, m_w_in, m_conv_w, m_conv_b, m_fgate_bias, m_q_norm, m_k_norm, m_lb_logits, m_hgrn_norm, m_sgu_norm, m_spatial_w, m_spatial_b, m_w_up, m_merge_b, m_w_o, m_norm_ple, m_w_ple_gate, m_w_ple_proj, v_norm_mix, v_w_in, v_conv_w, v_conv_b, v_fgate_bias, v_q_norm, v_k_norm, v_lb_logits, v_hgrn_norm, v_sgu_norm, v_spatial_w, v_spatial_b, v_w_up, v_merge_b, v_w_o, v_norm_ple, v_w_ple_gate, v_w_ple_proj):
    T = x.shape[1]
    SH = D // N_DEV
    CW = W // N_DEV
    tm = 512 if T % 512 == 0 else T
    tmm = 256 if T % 256 == 0 else T
    x0 = x.reshape(T, D)
    target = loss_target.reshape(T, D)

    small_shard = jnp.concatenate([
        merge_b.reshape(DEPTH * NBR, SH),
        jnp.pad(conv_w.reshape(DEPTH * CONV_WIDTH, CW), ((0, 16 - DEPTH * CONV_WIDTH), (0, SH - CW)))], axis=0)
    win_s = _permute_cols(w_in).astype(bf16)
    win0, wup_f, wo_f, wpg_f, wpp_f, g_small = _all_gather(
        [win_s[0],
         w_up.astype(bf16).reshape(DEPTH * NBR * W, SH),
         w_o.astype(bf16),
         w_ple_gate.astype(bf16),
         w_ple_proj.astype(bf16).reshape(DEPTH * PLE, SH),
         small_shard],
        [0, -1, 1, 1, -1, -1])
    win_f = [win0]
    win_later = [win_s[li] for li in range(1, DEPTH)]
    wup_f = wup_f.reshape(DEPTH, NBR, W, D)
    wpp_f = wpp_f.reshape(DEPTH, PLE, D)
    mb_f = g_small[0:DEPTH * NBR].reshape(DEPTH, NBR, D)
    cw_f = g_small[16:16 + DEPTH * CONV_WIDTH].reshape(DEPTH, CONV_WIDTH, N_DEV, SH)[..., 0:CW]
    cw_f = cw_f.reshape(DEPTH, CONV_WIDTH, W)

    loss_local, dx, gw, gs_full = _forward_backward(
        x0, p[:, 0], target, win_f, wup_f, wo_f, wpg_f, wpp_f, mb_f, cw_f, norm_mix, conv_b, fgate_bias, q_norm,
        k_norm, lb_logits, hgrn_norm, sgu_norm, spatial_w, spatial_b, norm_ple, win_later)
    loss = lax.psum(loss_local[0, 0], AXES)
    grad_x = dx.reshape(1, T, D)

    weights = dict(norm_mix=norm_mix, w_in=w_in, conv_w=conv_w, conv_b=conv_b, fgate_bias=fgate_bias, q_norm=q_norm,
                   k_norm=k_norm, lb_logits=lb_logits, hgrn_norm=hgrn_norm, sgu_norm=sgu_norm, spatial_w=spatial_w,
                   spatial_b=spatial_b, w_up=w_up, merge_b=merge_b, w_o=w_o, norm_ple=norm_ple,
                   w_ple_gate=w_ple_gate, w_ple_proj=w_ple_proj)
    ms = dict(norm_mix=m_norm_mix, w_in=m_w_in, conv_w=m_conv_w, conv_b=m_conv_b, fgate_bias=m_fgate_bias,
              q_norm=m_q_norm, k_norm=m_k_norm, lb_logits=m_lb_logits, hgrn_norm=m_hgrn_norm, sgu_norm=m_sgu_norm,
              spatial_w=m_spatial_w, spatial_b=m_spatial_b, w_up=m_w_up, merge_b=m_merge_b, w_o=m_w_o,
              norm_ple=m_norm_ple, w_ple_gate=m_w_ple_gate, w_ple_proj=m_w_ple_proj)
    vs = dict(norm_mix=v_norm_mix, w_in=v_w_in, conv_w=v_conv_w, conv_b=v_conv_b, fgate_bias=v_fgate_bias,
              q_norm=v_q_norm, k_norm=v_k_norm, lb_logits=v_lb_logits, hgrn_norm=v_hgrn_norm, sgu_norm=v_sgu_norm,
              spatial_w=v_spatial_w, spatial_b=v_spatial_b, w_up=v_w_up, merge_b=v_merge_b, w_o=v_w_o,
              norm_ple=v_norm_ple, w_ple_gate=v_w_ple_gate, w_ple_proj=v_w_ple_proj)
    return _exchange_and_update(loss, grad_x, gw, gs_full, weights, ms, vs)


def _forward_backward(x0, p, target, win_f, wup_f, wo_f, wpg_f, wpp_f, mb_f, cw_f, norm_mix, conv_b, fgate_bias,
                      q_norm, k_norm, lb_logits, hgrn_norm, sgu_norm, spatial_w, spatial_b, norm_ple, win_later=()):
    T = x0.shape[0]
    tm = 512 if T % 512 == 0 else T
    tmm = 256 if T % 256 == 0 else T
    tmi = 1024 if T % 1024 == 0 else tm
    lower = _lower_bounds(lb_logits)
    fb_pad = jnp.pad(fgate_bias, ((0, 0), (0, 128 - NH)))
    gq_t = jnp.tile(q_norm, (1, NH))
    gk_t = jnp.tile(k_norm, (1, NH))
    sbe = jnp.repeat(jnp.swapaxes(spatial_b, 1, 2), DH, axis=2)

    saved = []
    xc = x0
    p = p[:, None]
    for li in range(DEPTH):
        row = lambda a: a[li:li + 1]
        if li == 0 and win_later:
            z, h, *gathered = _inproj_fwd(xc, row(norm_mix), win_f[0], tmi, gather=win_later)
            win_f = [win_f[0]] + gathered
        else:
            z, h = _inproj_fwd(xc, row(norm_mix), win_f[li], tmi)
        ya = _conv_fwd(z, cw_f[li], row(conv_b), tm)
        yd = _sgu_fwd(z, row(sgu_norm), spatial_w[li], sbe[li], tm)
        yc, o_pre, states = _hgrn_fwd(z, lower[li:li + 1], row(hgrn_norm), tmm)
        qt, kt, vt, kh, vh, cum = _attn_prep(z, row(fb_pad), row(gq_t), row(gk_t), tm)
        cum4 = jnp.transpose(cum[:, 0:NH])
        ccol, crow = cum4[:, :, None], cum4[:, None, :]
        oh, lse = _attn_fwd(qt, kh, vt, crow, ccol, tm)
        yb, mg, x1, x2 = _merge_fwd(ya, oh, z, yc, yd, mb_f[li], xc, p[li, 0], wup_f[li], wo_f[li],
                                    row(norm_ple), wpg_f[li], wpp_f[li], tmm)
        saved.append(dict(x=xc, z=z, h=h, ya=ya, yb=yb, yc=yc, yd=yd, o_pre=o_pre, states=states,
                          qt=qt, kt=kt, kh=kh, vh=vh, crow=crow, ccol=ccol, oh=oh, lse=lse, mg=mg, x1=x1))
        xc = x2

    loss_local, dx = _loss_head(xc, target, tm)

    gw = dict(w_in=None, w_up=None, w_o=None, w_ple_gate=None, w_ple_proj=None)
    gs = {n: [None] * DEPTH for n, _ in _SMALL}
    dlower = [None] * DEPTH
    win_landed = {}
    for li in reversed(range(DEPTH)):
        s = saved[li]
        row = lambda a: a[li:li + 1]
        first = li == DEPTH - 1
        dx1, gw["w_ple_gate"], gw["w_ple_proj"], ggp = _ple_bwd(
            dx, s["x1"], p[li, 0], row(norm_ple), wpg_f[li], wpp_f[li], tmm, li,
            None if first else (gw["w_ple_gate"], gw["w_ple_proj"]))
        gs["norm_ple"][li] = ggp[0]
        dz, dya, dyb, dyc, dyd, gw["w_o"], gw["w_up"], gs["merge_b"][li] = _merge_bwd(
            dx1, s["mg"], s["ya"], s["yb"], s["yc"], s["yd"], s["z"], mb_f[li], wup_f[li], wo_f[li], tmm, li,
            None if first else (gw["w_o"], gw["w_up"]))
        dz, gcw, gcb = _conv_bwd(s["z"], dya, cw_f[li], row(conv_b), dz, tm)
        gs["conv_w"][li], gs["conv_b"][li] = gcw[0:CONV_WIDTH], gcb[0]
        dz, gs["spatial_w"][li], gsb, ggv = _sgu_bwd(s["z"], dyd, row(sgu_norm), spatial_w[li], sbe[li], dz, tm)
        gs["spatial_b"][li] = jnp.transpose(gsb[:, ::DH])
        gs["sgu_norm"][li] = ggv[0]
        dz, ggn, glb = _hgrn_bwd(s["z"], lower[li:li + 1], row(hgrn_norm), s["o_pre"], s["states"], dyc, dz, tmm)
        gs["hgrn_norm"][li], dlower[li] = ggn[0], glb[0]
        dot, delta = _attn_bwd_prep(dyb, s["oh"], s["z"], tm)
        SH = D // N_DEV
        exchange = None if first else (gw["w_in"], (li + 1) * SH, SH)
        dqh, dkh, dvh, dck, dcq, *landed = _attn_bwd(s["qt"], s["kt"], s["kh"], s["vh"], s["crow"], s["ccol"], dot,
                                                     s["lse"], delta, tm, exchange)
        if landed:
            win_landed[li + 1] = landed[0]
        dcq_t = jnp.transpose(dcq[:, :, 0:HP, :], (0, 2, 1, 3)).reshape(NH, T)
        dcq_t = jnp.pad(jnp.transpose(dcq_t), ((0, 0), (0, 128 - NH)))
        dz, ggq, ggk, gfb = _attn_post(s["z"], dyb, s["oh"], dqh, dkh, dvh, dck, dcq_t, row(fb_pad),
                                       row(gq_t), row(gk_t), dz, tmm)
        gs["q_norm"][li], gs["k_norm"][li], gs["fgate_bias"][li] = ggq[0, 0:DH], ggk[0, 0:DH], gfb[0, 0:NH]
        dx, gnm = _inproj_bwd_x(dz, win_f[li], s["x"], dx1, row(norm_mix), tmi)
        gs["norm_mix"][li] = gnm[0]
        gw["w_in"] = _inproj_bwd_w(s["h"], dz, tmi, li, gw["w_in"])
    dlower8 = jnp.pad(jnp.stack(dlower), ((0, 8 - DEPTH), (0, 0)))
    gs_full = {n: jnp.stack(v) for n, v in gs.items() if n != "lb_logits"}
    gs_full["lb_logits"] = _lower_bounds_bwd(lb_logits, dlower8)[0:DEPTH]
    gw["w_in_landed"] = win_landed
    return loss_local, dx, gw, gs_full


def _exchange_and_update(loss, grad_x, gw, gs_full, weights, ms, vs):
    SH = D // N_DEV
    CW = W // N_DEV

    small_buf = _pack_small(gs_full)
    landed = gw["w_in_landed"]
    rest = [li for li in range(DEPTH) if li not in landed]
    win_rest = jnp.concatenate([gw["w_in"][:, li * SH:(li + 1) * SH] for li in rest], axis=1)
    own = [win_rest, gw["w_up"], gw["w_o"], gw["w_ple_gate"], gw["w_ple_proj"]]
    from_sibling = _exchange_sibling(own)
    chip_sums = [_pair_sum(o, r) for o, r in zip(own, from_sibling)]
    l_win, l_wup, l_wo, l_wpg, l_wpp, l_small = _exchange_chips(chip_sums, [small_buf])

    g_rest = _sum_slabs(l_win)
    g_layers = {li: g_rest[n * SH:(n + 1) * SH] for n, li in enumerate(rest)}
    g_layers.update({li: _sum_slabs(land) for li, land in landed.items()})
    g_w_in = _unpermute_cols(jnp.stack([g_layers[li] for li in range(DEPTH)]))
    g_w_up = _sum_slabs(l_wup).reshape(DEPTH, NBR, W, SH)
    g_w_o = _sum_slabs(l_wo).reshape(DEPTH, SH, D)
    g_w_pg = _sum_slabs(l_wpg).reshape(DEPTH, SH, D)
    g_w_pp = _sum_slabs(l_wpp).reshape(DEPTH, PLE, SH)
    g_small = _unpack_small(_sum_slabs(l_small))
    g_small_local = dict(g_small)
    g_small_local["conv_w"] = _shard_cols(g_small["conv_w"], CW)
    g_small_local["merge_b"] = _shard_cols(g_small["merge_b"], SH)

    grads = dict(w_in=g_w_in, w_up=g_w_up, w_o=g_w_o, w_ple_gate=g_w_pg, w_ple_proj=g_w_pp)
    deltas, new_m, new_v = {}, {}, {}
    for name in ("w_in", "w_up", "w_o", "w_ple_gate", "w_ple_proj"):
        shape = weights[name].shape
        as3 = (shape[0], -1, shape[-1])
        d_, m_, v_ = _adamw(weights[name].reshape(as3), grads[name].reshape(as3),
                            ms[name].reshape(as3), vs[name].reshape(as3))
        deltas[name], new_m[name], new_v[name] = d_.reshape(shape), m_.reshape(shape), v_.reshape(shape)

    def local_shapes(parts):
        return {n: (parts[n] if parts[n].shape == s else jnp.pad(
            parts[n], [(0, 0)] * (len(s) - 1) + [(0, s[-1] - parts[n].shape[-1])])) for n, s in _SMALL}

    d_, m_, v_ = _adamw(_pack_small(local_shapes(weights)), _pack_small(local_shapes(g_small_local)),
                        _pack_small(local_shapes(ms)), _pack_small(local_shapes(vs)))
    for buf, dst in ((d_, deltas), (m_, new_m), (v_, new_v)):
        parts = _unpack_small(buf)
        for n, _ in _SMALL:
            dst[n] = parts[n][..., :weights[n].shape[-1]]
    for n, _ in _SMALL:
        grads[n] = g_small_local[n]

    order = ["norm_mix", "w_in", "conv_w", "conv_b", "fgate_bias", "q_norm", "k_norm", "lb_logits", "hgrn_norm",
             "sgu_norm", "spatial_w", "spatial_b", "w_up", "merge_b", "w_o", "norm_ple", "w_ple_gate", "w_ple_proj"]
    return (loss, grad_x, *[grads[n] for n in order], *[deltas[n] for n in order],
            *[new_m[n] for n in order], *[new_v[n] for n in order])
```

```python
import functools

import jax
import jax.numpy as jnp
from jax import lax
from jax.experimental import pallas as pl
from jax.experimental.pallas import tpu as pltpu

f32 = jnp.float32
bf16 = jnp.bfloat16

D = 1024
W = 256
NH = 4
DH = 64
NBR = 4
PLE = 256
DEPTH = 4
CONV_WIDTH = 3
SGU_CHUNK = 128
GLA_CHUNK = 128
EPS = 1e-6
MASK_VALUE = -1e30
IN_COLS = 7940
NZ = 8064
OFF_CONV = 4096
OFF_HGRN = 5120
OFF_SGU = 6144
OFF_ATT = 6912
OFF_F = 7936
ZT = 1152
NZT = NZ // ZT
EXP_CLAMP = 80.0
LOG2E = 1.4426950408889634

ADAM_LR = 0.001
ADAM_B1 = 0.9
ADAM_B2 = 0.999
ADAM_EPS = 1e-08
ADAM_WD = 0.01
ADAM_STEP = 10

N_DEV = 8
AXES = ("x", "y", "c")
VMEM_LIMIT = 56 * 1024 * 1024
HI = lax.Precision.HIGHEST

NT_DIMS = (((1,), (1,)), ((), ()))
TN_DIMS = (((0,), (0,)), ((), ()))


def _pcall(body, **kw):
    return pl.pallas_call(body, **kw)


def _params(*sem):
    return pltpu.CompilerParams(dimension_semantics=sem, vmem_limit_bytes=VMEM_LIMIT)


def _mm(a, b):
    return jnp.dot(a, b, preferred_element_type=f32)


def _mm_nt(a, b):
    return lax.dot_general(a, b, NT_DIMS, preferred_element_type=f32)


def _mm_tn(a, b):
    return lax.dot_general(a, b, TN_DIMS, preferred_element_type=f32)


def _sigmoid(x):
    return 1.0 / (1.0 + jnp.exp(-x))


def _silu(x):
    return x * _sigmoid(x)


def _dsilu(x):
    s = _sigmoid(x)
    return s * (1.0 + x * (1.0 - s))


def _logsigmoid(x):
    return jnp.minimum(x, 0.0) - jnp.log(1.0 + jnp.exp(-jnp.abs(x)))


def _iota2(shape, axis):
    return lax.broadcasted_iota(jnp.int32, shape, axis)


def _group_mean_matrix(n, group):
    shift = group.bit_length() - 1
    r = lax.shift_right_logical(_iota2((n, n), 0), shift)
    c = lax.shift_right_logical(_iota2((n, n), 1), shift)
    return jnp.where(r == c, 1.0 / group, 0.0).astype(f32)


def _group_mean(x, gm):
    return jnp.dot(x, gm, precision=HI, preferred_element_type=f32)


def _lower_tri(n):
    return jnp.where(_iota2((n, n), 0) >= _iota2((n, n), 1), 1.0, 0.0).astype(f32)


def _upper_tri(n):
    return jnp.where(_iota2((n, n), 0) <= _iota2((n, n), 1), 1.0, 0.0).astype(f32)


def _rows3(r0, r1, r2, width):
    row = _iota2((8, width), 0)
    return jnp.where(row == 0, r0, jnp.where(row == 1, r1, jnp.where(row == 2, r2, 0.0)))


def _inproj_fwd(x, g, w, tm, gather=()):
    T = x.shape[0]
    n = len(gather)
    axes = [0] * n
    steps = (T // tm) * NZT

    def body(x_ref, g_ref, w_ref, *rest):
        z_ref, h_ref = rest[n:n + 2]

        @pl.when(pl.program_id(1) == 0)
        def _():
            xv = x_ref[...]
            r = lax.rsqrt(jnp.mean(xv * xv, axis=-1, keepdims=True) + EPS)
            h_ref[...] = (xv * r * g_ref[...]).astype(bf16)

        if n:
            start, forward, finish = _gather_phases(gather, axes, rest[:n], rest[n + 2:2 * n + 2], *rest[2 * n + 2:])
            step = pl.program_id(0) * NZT + pl.program_id(1)
            pl.when(step == 0)(start)
            pl.when(step == steps // 2)(forward)

        z_ref[...] = _mm(h_ref[...], w_ref[...])

        if n:
            pl.when(step == steps - 1)(finish)

    hbm = pl.BlockSpec(memory_space=pltpu.HBM)
    return _pcall(
        body, name="inproj_fwd_gather" if n else "inproj_fwd", grid=(T // tm, NZT),
        in_specs=[pl.BlockSpec((tm, D), lambda i, j: (i, 0)),
                  pl.BlockSpec((1, D), lambda i, j: (0, 0)),
                  pl.BlockSpec((D, ZT), lambda i, j: (0, j))] + [hbm] * n,
        out_specs=[pl.BlockSpec((tm, ZT), lambda i, j: (i, j)),
                   pl.BlockSpec((tm, D), lambda i, j: (i, 0))] + [hbm] * n,
        out_shape=[jax.ShapeDtypeStruct((T, NZ), f32), jax.ShapeDtypeStruct((T, D), bf16)]
        + _gathered_shapes(gather, axes),
        scratch_shapes=_gather_semaphores(n) if n else [],
        compiler_params=_params("arbitrary" if n else "parallel", "arbitrary"),
    )(x, g, w, *gather)


def _inproj_bwd_x(dz, w, x, dx1, g, tm):
    T = x.shape[0]

    def body(dz_ref, w_ref, x_ref, dx1_ref, g_ref, dx_ref, gg_ref, acc):
        i, k = pl.program_id(0), pl.program_id(1)

        @pl.when(k == 0)
        def _():
            acc[...] = jnp.zeros_like(acc)

        @pl.when((i == 0) & (k == 0))
        def _():
            gg_ref[...] = jnp.zeros_like(gg_ref)

        acc[...] += _mm_nt(dz_ref[...], w_ref[...])

        @pl.when(k == NZT - 1)
        def _():
            xv = x_ref[...]
            r = lax.rsqrt(jnp.mean(xv * xv, axis=-1, keepdims=True) + EPS)
            dh = acc[...]
            gg_ref[...] += jnp.sum(dh * xv * r, axis=0, keepdims=True)
            u = dh * g_ref[...]
            dx_ref[...] = dx1_ref[...] + r * u - xv * (r * r * r) * jnp.mean(u * xv, axis=-1, keepdims=True)

    return _pcall(
        body, name="inproj_bwd_x", grid=(T // tm, NZT),
        in_specs=[pl.BlockSpec((tm, ZT), lambda i, k: (i, k)),
                  pl.BlockSpec((D, ZT), lambda i, k: (0, k)),
                  pl.BlockSpec((tm, D), lambda i, k: (i, 0)),
                  pl.BlockSpec((tm, D), lambda i, k: (i, 0)),
                  pl.BlockSpec((1, D), lambda i, k: (0, 0))],
        out_specs=[pl.BlockSpec((tm, D), lambda i, k: (i, 0)),
                   pl.BlockSpec((1, D), lambda i, k: (0, 0))],
        out_shape=[jax.ShapeDtypeStruct((T, D), f32), jax.ShapeDtypeStruct((1, D), f32)],
        scratch_shapes=[pltpu.VMEM((tm, D), f32)],
        compiler_params=_params("arbitrary", "arbitrary"),
    )(dz, w, x, dx1, g)


def _inproj_bwd_w(h, dz, tm, li, buf):
    T = h.shape[0]
    SH = D // N_DEV
    nt = T // tm
    extra = [] if buf is None else [buf]

    def body(h_ref, dz_ref, *rest):
        gw_ref, acc = rest[len(extra):]

        @pl.when(pl.program_id(1) == 0)
        def _():
            acc[...] = jnp.zeros_like(acc)

        acc[...] += _mm_tn(h_ref[...], dz_ref[...])

        @pl.when(pl.program_id(1) == nt - 1)
        def _():
            gw_ref[...] = acc[...].reshape(N_DEV, SH, ZT).astype(bf16)

    return _pcall(
        body, name="inproj_bwd_w", grid=(NZT, nt),
        in_specs=[pl.BlockSpec((tm, D), lambda j, i: (i, 0)),
                  pl.BlockSpec((tm, ZT), lambda j, i: (i, j))] + [pl.BlockSpec(memory_space=pl.ANY)] * len(extra),
        out_specs=pl.BlockSpec((N_DEV, SH, ZT), lambda j, i: (0, li, j)),
        out_shape=jax.ShapeDtypeStruct((N_DEV, DEPTH * SH, NZ), bf16),
        scratch_shapes=[pltpu.VMEM((D, ZT), f32)],
        input_output_aliases={2: 0} if extra else {},
        compiler_params=_params("parallel", "arbitrary"),
    )(h, dz, *extra)


def _zblock(tm, col256):
    return pl.BlockSpec((tm, W), lambda i, c=col256: (i, c))


def _conv_taps(zc, halo, cw_ref, n):
    ext = jnp.concatenate([halo, zc], axis=0)
    z1 = pltpu.roll(ext, 1, 0)[8:]
    z2 = pltpu.roll(ext, 2, 0)[8:]
    return z1, z2


def _conv_fwd(z, cw, cb, tm):
    T = z.shape[0]
    c0 = OFF_CONV // W
    hb = tm // 8

    def body(ax_ref, ab_ref, ac_ref, ag_ref, hx_ref, hc_ref, cw_ref, cb_ref, y_ref):
        i = pl.program_id(0)
        zc = ac_ref[...] * ax_ref[...]
        halo = jnp.where(i > 0, hc_ref[...] * hx_ref[...], 0.0)
        z1, z2 = _conv_taps(zc, halo, cw_ref, tm)
        y = cw_ref[2:3, :] * zc + cw_ref[1:2, :] * z1 + cw_ref[0:1, :] * z2
        ya = ab_ref[...] * (y + cb_ref[...])
        y_ref[...] = (ya * _silu(ag_ref[...])).astype(bf16)

    halo_spec = lambda col: pl.BlockSpec((8, W), lambda i, c=col: (jnp.maximum(i * hb - 1, 0), c))
    return _pcall(
        body, name="conv_fwd", grid=(T // tm,),
        in_specs=[_zblock(tm, c0), _zblock(tm, c0 + 1), _zblock(tm, c0 + 2), _zblock(tm, c0 + 3),
                  halo_spec(c0), halo_spec(c0 + 2),
                  pl.BlockSpec((CONV_WIDTH, W), lambda i: (0, 0)),
                  pl.BlockSpec((1, W), lambda i: (0, 0))],
        out_specs=pl.BlockSpec((tm, W), lambda i: (i, 0)),
        out_shape=jax.ShapeDtypeStruct((T, W), bf16),
        compiler_params=_params("parallel"),
    )(z, z, z, z, z, z, cw, cb)


def _conv_bwd(z, dy, cw, cb, dzbuf, tm):
    T = z.shape[0]
    c0 = OFF_CONV // W
    hb = tm // 8
    nt = T // tm

    def body(ax_ref, ab_ref, ac_ref, ag_ref, hx_ref, hc_ref, nb_ref, ng_ref, dy_ref, ndy_ref,
             cw_ref, cb_ref, dzin_ref, dz_ref, gcw_ref, gcb_ref):
        i = pl.program_id(0)

        @pl.when(i == 0)
        def _():
            gcw_ref[...] = jnp.zeros_like(gcw_ref)
            gcb_ref[...] = jnp.zeros_like(gcb_ref)

        ax, ab, ac, ag = ax_ref[...], ab_ref[...], ac_ref[...], ag_ref[...]
        w0, w1, w2 = cw_ref[0:1, :], cw_ref[1:2, :], cw_ref[2:3, :]
        zc = ac * ax
        halo = jnp.where(i > 0, hc_ref[...] * hx_ref[...], 0.0)
        z1, z2 = _conv_taps(zc, halo, cw_ref, tm)
        yb = w2 * zc + w1 * z1 + w0 * z2 + cb_ref[...]
        ya = ab * yb
        dyg = dy_ref[...]
        dag = dyg * ya * _dsilu(ag)
        dya = dyg * _silu(ag)
        dab = dya * yb
        dyc = dya * ab
        nxt = jnp.where(i < nt - 1, ndy_ref[...] * _silu(ng_ref[...]) * nb_ref[...], 0.0)
        ext = jnp.concatenate([dyc, nxt], axis=0)
        d1 = pltpu.roll(ext, tm + 8 - 1, 0)[:tm]
        d2 = pltpu.roll(ext, tm + 8 - 2, 0)[:tm]
        dzc = w2 * dyc + w1 * d1 + w0 * d2
        dz_ref[:, 0:W] = (dzc * ac).astype(bf16)
        dz_ref[:, W:2 * W] = dab.astype(bf16)
        dz_ref[:, 2 * W:3 * W] = (dzc * ax).astype(bf16)
        dz_ref[:, 3 * W:4 * W] = dag.astype(bf16)
        gcb_ref[...] += jnp.sum(dyc, axis=0, keepdims=True)
        gcw_ref[...] += _rows3(jnp.sum(dyc * z2, axis=0, keepdims=True),
                               jnp.sum(dyc * z1, axis=0, keepdims=True),
                               jnp.sum(dyc * zc, axis=0, keepdims=True), W)

    prev_spec = lambda col: pl.BlockSpec((8, W), lambda i, c=col: (jnp.maximum(i * hb - 1, 0), c))
    next_z = lambda col: pl.BlockSpec((8, W), lambda i, c=col: (jnp.minimum((i + 1) * hb, T // 8 - 1), c))
    next_dy = pl.BlockSpec((8, W), lambda i: (jnp.minimum((i + 1) * hb, T // 8 - 1), 0))
    return _pcall(
        body, name="conv_bwd", grid=(nt,),
        in_specs=[_zblock(tm, c0), _zblock(tm, c0 + 1), _zblock(tm, c0 + 2), _zblock(tm, c0 + 3),
                  prev_spec(c0), prev_spec(c0 + 2), next_z(c0 + 1), next_z(c0 + 3),
                  pl.BlockSpec((tm, W), lambda i: (i, 0)), next_dy,
                  pl.BlockSpec((CONV_WIDTH, W), lambda i: (0, 0)),
                  pl.BlockSpec((1, W), lambda i: (0, 0)),
                  pl.BlockSpec(memory_space=pl.ANY)],
        out_specs=[pl.BlockSpec((tm, 4 * W), lambda i: (i, OFF_CONV // (4 * W))),
                   pl.BlockSpec((8, W), lambda i: (0, 0)),
                   pl.BlockSpec((1, W), lambda i: (0, 0))],
        out_shape=[jax.ShapeDtypeStruct((T, NZ), bf16), jax.ShapeDtypeStruct((8, W), f32),
                   jax.ShapeDtypeStruct((1, W), f32)],
        input_output_aliases={12: 0},
        compiler_params=_params("arbitrary"),
    )(z, z, z, z, z, z, z, z, dy, dy, cw, cb, dzbuf)


def _sgu_core(dv_ref, gv_ref, sw_ref, sbe_ref, s_scr, tm):
    v = dv_ref[...]
    gm = _group_mean_matrix(W, DH)
    rv = lax.rsqrt(_group_mean(v * v, gm) + EPS)
    vh = v * rv
    vnb = (vh * gv_ref[...]).astype(bf16)
    causal = _iota2((SGU_CHUNK, SGU_CHUNK), 0) >= _iota2((SGU_CHUNK, SGU_CHUNK), 1)
    wgs = [jnp.where(causal, sw_ref[g], 0.0).astype(bf16) for g in range(NH)]
    for c in range(tm // SGU_CHUNK):
        rows = slice(c * SGU_CHUNK, (c + 1) * SGU_CHUNK)
        for g in range(NH):
            cols = slice(g * DH, (g + 1) * DH)
            s_scr[rows, cols] = _mm(wgs[g], vnb[rows, cols])
    sb = sbe_ref[...]
    s = s_scr[...] + jnp.concatenate([sb] * (tm // SGU_CHUNK), axis=0)
    return v, rv, vh, vnb, wgs, causal, gm, s


def _sgu_fwd(z, gv, sw, sbe, tm):
    T = z.shape[0]
    c0 = OFF_SGU // W

    def body(du_ref, dv_ref, dg_ref, gv_ref, sw_ref, sbe_ref, y_ref, s_scr):
        s = _sgu_core(dv_ref, gv_ref, sw_ref, sbe_ref, s_scr, tm)[-1]
        y_ref[...] = ((du_ref[...] * s) * _silu(dg_ref[...])).astype(bf16)

    return _pcall(
        body, name="sgu_fwd", grid=(T // tm,),
        in_specs=[_zblock(tm, c0), _zblock(tm, c0 + 1), _zblock(tm, c0 + 2),
                  pl.BlockSpec((1, W), lambda i: (0, 0)),
                  pl.BlockSpec((NH, SGU_CHUNK, SGU_CHUNK), lambda i: (0, 0, 0)),
                  pl.BlockSpec((SGU_CHUNK, W), lambda i: (0, 0))],
        out_specs=pl.BlockSpec((tm, W), lambda i: (i, 0)),
        out_shape=jax.ShapeDtypeStruct((T, W), bf16),
        scratch_shapes=[pltpu.VMEM((tm, W), f32)],
        compiler_params=_params("parallel"),
    )(z, z, z, gv, sw, sbe)


def _sgu_bwd(z, dy, gv, sw, sbe, dzbuf, tm):
    T = z.shape[0]
    c0 = OFF_SGU // W
    nt = T // tm

    def body(du_ref, dv_ref, dg_ref, dy_ref, gv_ref, sw_ref, sbe_ref, dzin_ref,
             dz_ref, gsw_ref, gsb_ref, ggv_ref, s_scr, dvn_scr, sb_acc):
        i = pl.program_id(0)

        @pl.when(i == 0)
        def _():
            gsw_ref[...] = jnp.zeros_like(gsw_ref)
            ggv_ref[...] = jnp.zeros_like(ggv_ref)
            sb_acc[...] = jnp.zeros_like(sb_acc)

        v, rv, vh, vnb, wgs, causal, gm, s = _sgu_core(dv_ref, gv_ref, sw_ref, sbe_ref, s_scr, tm)
        du, dg, dyv = du_ref[...], dg_ref[...], dy_ref[...]
        ddg = dyv * (du * s) * _dsilu(dg)
        t = dyv * _silu(dg)
        ddu = t * s
        ds = t * du
        dsb = ds.astype(bf16)
        acc = sb_acc[...]
        for c in range(tm // SGU_CHUNK):
            rows = slice(c * SGU_CHUNK, (c + 1) * SGU_CHUNK)
            acc = acc + ds[rows, :]
            for g in range(NH):
                cols = slice(g * DH, (g + 1) * DH)
                gsw_ref[g] += jnp.where(causal, _mm_nt(dsb[rows, cols], vnb[rows, cols]), 0.0)
                dvn_scr[rows, cols] = _mm_tn(wgs[g], dsb[rows, cols])
        sb_acc[...] = acc
        dvn = dvn_scr[...]
        ggv_ref[...] += jnp.sum(dvn * vh, axis=0, keepdims=True)
        u = dvn * gv_ref[...]
        ddv = rv * u - v * (rv * rv * rv) * _group_mean(u * v, gm)
        dz_ref[:, 0:W] = ddu.astype(bf16)
        dz_ref[:, W:2 * W] = ddv.astype(bf16)
        dz_ref[:, 2 * W:3 * W] = ddg.astype(bf16)

        @pl.when(i == nt - 1)
        def _():
            gsb_ref[...] = _group_mean(sb_acc[...], gm) * float(DH)

    return _pcall(
        body, name="sgu_bwd", grid=(nt,),
        in_specs=[_zblock(tm, c0), _zblock(tm, c0 + 1), _zblock(tm, c0 + 2),
                  pl.BlockSpec((tm, W), lambda i: (i, 0)),
                  pl.BlockSpec((1, W), lambda i: (0, 0)),
                  pl.BlockSpec((NH, SGU_CHUNK, SGU_CHUNK), lambda i: (0, 0, 0)),
                  pl.BlockSpec((SGU_CHUNK, W), lambda i: (0, 0)),
                  pl.BlockSpec(memory_space=pl.ANY)],
        out_specs=[pl.BlockSpec((tm, 3 * W), lambda i: (i, OFF_SGU // (3 * W))),
                   pl.BlockSpec((NH, SGU_CHUNK, SGU_CHUNK), lambda i: (0, 0, 0)),
                   pl.BlockSpec((SGU_CHUNK, W), lambda i: (0, 0)),
                   pl.BlockSpec((1, W), lambda i: (0, 0))],
        out_shape=[jax.ShapeDtypeStruct((T, NZ), bf16),
                   jax.ShapeDtypeStruct((NH, SGU_CHUNK, SGU_CHUNK), f32),
                   jax.ShapeDtypeStruct((SGU_CHUNK, W), f32),
                   jax.ShapeDtypeStruct((1, W), f32)],
        scratch_shapes=[pltpu.VMEM((tm, W), f32), pltpu.VMEM((tm, W), f32), pltpu.VMEM((SGU_CHUNK, W), f32)],
        input_output_aliases={7: 0},
        compiler_params=_params("arbitrary"),
    )(z, z, z, dy, gv, sw, sbe, dzbuf)


def _hgrn_gates(cq_ref, cf_ref, lb_ref):
    q = _silu(cq_ref[...])
    sig = _sigmoid(cf_ref[...])
    lb = lb_ref[...]
    g = lb + (1.0 - lb) * sig
    return q, sig, g, jnp.log(g), (1.0 - lb) * (1.0 - sig)


def _hgrn_chunk_terms(lgc, qc, kc):
    C = GLA_CHUNK
    b = jnp.dot(_lower_tri(C), lgc, precision=HI, preferred_element_type=f32)
    bl = jnp.sum(lgc, axis=0, keepdims=True)
    mid = jnp.sum(jnp.where(_iota2((C, W), 0) <= C // 2, lgc, 0.0), axis=0, keepdims=True)
    eb = jnp.exp(b)
    em = jnp.exp(jnp.minimum(b - mid, EXP_CLAMP))
    emi = jnp.exp(jnp.minimum(mid - b, EXP_CLAMP))
    ek = jnp.exp(bl - b)
    return dict(eb=eb, em=em, emi=emi, ek=ek, ebl=jnp.exp(bl),
                qe=qc * eb, qm=qc * em, km=kc * emi, kd=kc * ek)


def _hgrn_fwd(z, lb, gain, tm):
    T = z.shape[0]
    c0 = OFF_HGRN // W
    C = GLA_CHUNK
    ncp = tm // C

    def body(cq_ref, cf_ref, ci_ref, cg_ref, lb_ref, gn_ref, y_ref, o_ref, st_ref, state, o_scr):
        @pl.when(pl.program_id(0) == 0)
        def _():
            state[...] = jnp.zeros_like(state)

        q, sig, g, lg, kf = _hgrn_gates(cq_ref, cf_ref, lb_ref)
        v = ci_ref[...]
        causal = _iota2((C, C), 0) >= _iota2((C, C), 1)
        for c in range(ncp):
            rows = slice(c * C, (c + 1) * C)
            tr = _hgrn_chunk_terms(lg[rows], q[rows], kf[rows])
            vb = v[rows].astype(bf16)
            qmb, kmb, qeb, kdb = (tr[n].astype(bf16) for n in ("qm", "km", "qe", "kd"))
            for h in range(NH):
                cols = slice(h * DH, (h + 1) * DH)
                hr = slice(h * DH, (h + 1) * DH)
                st = state[hr, :]
                st_ref[c, hr, :] = st
                p = jnp.where(causal, _mm_nt(qmb[:, cols], kmb[:, cols]), 0.0)
                o_scr[rows, cols] = _mm(p.astype(bf16), vb[:, cols]) + _mm_nt(qeb[:, cols], st.astype(bf16))
                state[hr, :] = st * tr["ebl"][:, cols] + _mm_tn(vb[:, cols], kdb[:, cols])
        o = o_scr[...]
        o_ref[...] = o
        gm = _group_mean_matrix(W, DH)
        r = lax.rsqrt(_group_mean(o * o, gm) + EPS)
        y_ref[...] = ((o * r * gn_ref[...]) * _silu(cg_ref[...])).astype(bf16)

    return _pcall(
        body, name="hgrn_fwd", grid=(T // tm,),
        in_specs=[_zblock(tm, c0), _zblock(tm, c0 + 1), _zblock(tm, c0 + 2), _zblock(tm, c0 + 3),
                  pl.BlockSpec((1, W), lambda i: (0, 0)), pl.BlockSpec((1, W), lambda i: (0, 0))],
        out_specs=[pl.BlockSpec((tm, W), lambda i: (i, 0)),
                   pl.BlockSpec((tm, W), lambda i: (i, 0)),
                   pl.BlockSpec((ncp, W, DH), lambda i: (i, 0, 0))],
        out_shape=[jax.ShapeDtypeStruct((T, W), bf16), jax.ShapeDtypeStruct((T, W), f32),
                   jax.ShapeDtypeStruct((T // C, W, DH), f32)],
        scratch_shapes=[pltpu.VMEM((W, DH), f32), pltpu.VMEM((tm, W), f32)],
        compiler_params=_params("arbitrary"),
    )(z, z, z, z, lb, gain)


def _hgrn_bwd(z, lb, gain, o_pre, states, dy, dzbuf, tm):
    T = z.shape[0]
    c0 = OFF_HGRN // W
    C = GLA_CHUNK
    ncp = tm // C
    nt = T // tm

    def body(cq_ref, cf_ref, ci_ref, cg_ref, lb_ref, gn_ref, o_ref, st_ref, dy_ref, dzin_ref,
             dz_ref, ggn_ref, glb_ref, dstate, dq_s, dk_s, dv_s, db_s):
        @pl.when(pl.program_id(0) == 0)
        def _():
            dstate[...] = jnp.zeros_like(dstate)
            ggn_ref[...] = jnp.zeros_like(ggn_ref)
            glb_ref[...] = jnp.zeros_like(glb_ref)

        cq, cg = cq_ref[...], cg_ref[...]
        q, sig, g, lg, kf = _hgrn_gates(cq_ref, cf_ref, lb_ref)
        lb = lb_ref[...]
        v = ci_ref[...]
        o = o_ref[...]
        gm = _group_mean_matrix(W, DH)
        r = lax.rsqrt(_group_mean(o * o, gm) + EPS)
        oh = o * r
        gn = gn_ref[...]
        dyv = dy_ref[...]
        dcg = dyv * (oh * gn) * _dsilu(cg)
        don = dyv * _silu(cg)
        ggn_ref[...] += jnp.sum(don * oh, axis=0, keepdims=True)
        u = don * gn
        do = r * u - o * (r * r * r) * _group_mean(u * o, gm)

        causal = _iota2((C, C), 0) >= _iota2((C, C), 1)
        last_row = _iota2((C, DH), 0) == C - 1
        for c in reversed(range(ncp)):
            rows = slice(c * C, (c + 1) * C)
            tr = _hgrn_chunk_terms(lg[rows], q[rows], kf[rows])
            vb = v[rows].astype(bf16)
            dob = do[rows].astype(bf16)
            qmb, kmb, qeb, kdb = (tr[n].astype(bf16) for n in ("qm", "km", "qe", "kd"))
            for h in range(NH):
                cols = slice(h * DH, (h + 1) * DH)
                hr = slice(h * DH, (h + 1) * DH)
                st0 = st_ref[c, hr, :]
                dst = dstate[hr, :]
                dstb = dst.astype(bf16)
                doh = dob[:, cols]
                p = jnp.where(causal, _mm_nt(qmb[:, cols], kmb[:, cols]), 0.0)
                dp = jnp.where(causal, _mm_nt(doh, vb[:, cols]), 0.0)
                dpb = dp.astype(bf16)
                dvh = _mm_tn(p.astype(bf16), doh) + _mm_nt(kdb[:, cols], dstb)
                dqm = _mm(dpb, kmb[:, cols])
                dkm = _mm_tn(dpb, qmb[:, cols])
                dqe = _mm(doh, st0.astype(bf16))
                dkd = _mm(vb[:, cols], dstb)
                ebl = tr["ebl"][:, cols]
                dstate[hr, :] = dst * ebl + _mm_tn(doh, qeb[:, cols])
                qm, km, qe, kd = (a[:, cols].astype(f32) for a in (qmb, kmb, qeb, kdb))
                kterm = dkd * kd
                dbh = dqm * qm - dkm * km + dqe * qe - kterm
                extra = jnp.sum(kterm, axis=0, keepdims=True) + ebl * jnp.sum(dst * st0, axis=0, keepdims=True)
                dbh = dbh + jnp.where(last_row, extra, 0.0)
                dq_s[rows, cols] = dqm * tr["em"][:, cols] + dqe * tr["eb"][:, cols]
                dk_s[rows, cols] = dkm * tr["emi"][:, cols] + dkd * tr["ek"][:, cols]
                dv_s[rows, cols] = dvh
                db_s[rows, cols] = dbh
            db_s[rows, :] = jnp.dot(_upper_tri(C), db_s[rows, :], precision=HI, preferred_element_type=f32)
        dlg = db_s[...]
        dk = dk_s[...]
        dsig = sig * (1.0 - sig)
        one_lb = 1.0 - lb
        dcf = (dlg / g - dk) * one_lb * dsig
        glb_ref[...] += jnp.sum((dlg / g - dk) * (1.0 - sig), axis=0, keepdims=True)
        dz_ref[:, 0:W] = (dq_s[...] * _dsilu(cq)).astype(bf16)
        dz_ref[:, W:2 * W] = dcf.astype(bf16)
        dz_ref[:, 2 * W:3 * W] = dv_s[...].astype(bf16)
        dz_ref[:, 3 * W:4 * W] = dcg.astype(bf16)

    rev = lambda i: nt - 1 - i
    zb = lambda col: pl.BlockSpec((tm, W), lambda i, c=col: (rev(i), c))
    return _pcall(
        body, name="hgrn_bwd", grid=(nt,),
        in_specs=[zb(c0), zb(c0 + 1), zb(c0 + 2), zb(c0 + 3),
                  pl.BlockSpec((1, W), lambda i: (0, 0)), pl.BlockSpec((1, W), lambda i: (0, 0)),
                  pl.BlockSpec((tm, W), lambda i: (rev(i), 0)),
                  pl.BlockSpec((ncp, W, DH), lambda i: (rev(i), 0, 0)),
                  pl.BlockSpec((tm, W), lambda i: (rev(i), 0)),
                  pl.BlockSpec(memory_space=pl.ANY)],
        out_specs=[pl.BlockSpec((tm, 4 * W), lambda i: (rev(i), OFF_HGRN // (4 * W))),
                   pl.BlockSpec((1, W), lambda i: (0, 0)),
                   pl.BlockSpec((1, W), lambda i: (0, 0))],
        out_shape=[jax.ShapeDtypeStruct((T, NZ), bf16), jax.ShapeDtypeStruct((1, W), f32),
                   jax.ShapeDtypeStruct((1, W), f32)],
        scratch_shapes=[pltpu.VMEM((W, DH), f32)] + [pltpu.VMEM((tm, W), f32)] * 4,
        input_output_aliases={9: 0},
        compiler_params=_params("arbitrary"),
    )(z, z, z, z, lb, gain, o_pre, states, dy, dzbuf)


def _attn_prep(z, fbias, gq, gk, tm):
    T = z.shape[0]
    c0 = OFF_ATT // W

    def body(q_ref, k_ref, v_ref, f_ref, fb_ref, gq_ref, gk_ref, qt_ref, kt_ref, vt_ref, kh_ref, vh_ref, cum_ref,
             carry):
        @pl.when(pl.program_id(0) == 0)
        def _():
            carry[...] = jnp.zeros_like(carry)

        gm = _group_mean_matrix(W, DH)
        q, k, v = q_ref[...], k_ref[...], v_ref[...]
        qs = q * lax.rsqrt(_group_mean(q * q, gm) + EPS) * (gq_ref[...] * (DH ** -0.5 * LOG2E))
        kn = k * lax.rsqrt(_group_mean(k * k, gm) + EPS) * gk_ref[...]
        qt_ref[...] = qs.T.astype(bf16)
        kt_ref[...] = kn.T.astype(bf16)
        vt_ref[...] = v.T.astype(bf16)
        for h in range(NH):
            cols = slice(h * DH, (h + 1) * DH)
            kh_ref[h] = kn[:, cols].astype(bf16)
            vh_ref[h] = v[:, cols].astype(bf16)
        ls = _logsigmoid(f_ref[...] + fb_ref[...])
        cum = jnp.dot(_lower_tri(tm), ls, precision=HI, preferred_element_type=f32) + carry[...]
        cum_ref[...] = cum * LOG2E
        carry[...] += jnp.sum(ls, axis=0, keepdims=True)

    hspec = pl.BlockSpec((NH, tm, DH), lambda i: (0, i, 0))
    tspec = pl.BlockSpec((W, tm), lambda i: (0, i))
    return _pcall(
        body, name="attn_prep", grid=(T // tm,),
        in_specs=[_zblock(tm, c0), _zblock(tm, c0 + 1), _zblock(tm, c0 + 2),
                  pl.BlockSpec((tm, 128), lambda i: (i, OFF_F // 128)),
                  pl.BlockSpec((1, 128), lambda i: (0, 0)),
                  pl.BlockSpec((1, W), lambda i: (0, 0)), pl.BlockSpec((1, W), lambda i: (0, 0))],
        out_specs=[tspec, tspec, tspec, hspec, hspec, pl.BlockSpec((tm, 128), lambda i: (i, 0))],
        out_shape=[jax.ShapeDtypeStruct((W, T), bf16)] * 3 + [jax.ShapeDtypeStruct((NH, T, DH), bf16)] * 2
        + [jax.ShapeDtypeStruct((T, 128), f32)],
        scratch_shapes=[pltpu.VMEM((1, 128), f32)],
        compiler_params=_params("arbitrary"),
    )(z, z, z, z, fbias, gq, gk)


HP = 2


def _causal_pairs(nq, key_major):
    if key_major:
        pairs = [(qi, ki) for ki in range(nq) for qi in range(ki, nq)]
    else:
        pairs = [(qi, ki) for qi in range(nq) for ki in range(qi + 1)]
    return (jnp.asarray([p[0] for p in pairs], jnp.int32), jnp.asarray([p[1] for p in pairs], jnp.int32))


def _head_rows(rows, n):
    return jnp.concatenate([jnp.broadcast_to(r, (DH, n)) for r in rows], axis=0)


def _attn_fwd(qt, kh, vt, crow, ccol, bq):
    T = qt.shape[1]
    nq = T // bq
    bk = bq
    qs, ks = _causal_pairs(nq, key_major=False)
    BW = HP * DH

    def body(qs_ref, ks_ref, qt_ref, k_ref, vt_ref, cr_ref, cc_ref, o_ref, lse_ref, m_s, l_s, acc_s):
        i = pl.program_id(1)
        qi, ki = qs_ref[i], ks_ref[i]

        @pl.when(ki == 0)
        def _():
            m_s[...] = jnp.full_like(m_s, MASK_VALUE)
            l_s[...] = jnp.zeros_like(l_s)
            acc_s[...] = jnp.zeros_like(acc_s)

        def step(diagonal):
            for h in range(HP):
                rows = slice(h * DH, (h + 1) * DH)
                s = _mm(k_ref[h], qt_ref[rows, :]) - cc_ref[h]
                if diagonal:
                    s = jnp.where(_iota2((bk, bq), 0) <= _iota2((bk, bq), 1), s, MASK_VALUE)
                cr = cr_ref[h]
                m_old = m_s[h]
                m_new = jnp.maximum(m_old, jnp.max(s, axis=0, keepdims=True) + cr)
                p = jnp.exp2(s + (cr - m_new))
                alpha = jnp.exp2(m_old - m_new)
                l_s[h] = alpha * l_s[h] + jnp.sum(p, axis=0, keepdims=True)
                acc_s[rows, :] = alpha * acc_s[rows, :] + _mm(vt_ref[rows, :], p.astype(bf16))
                m_s[h] = m_new

        @pl.when(ki < qi)
        def _():
            step(False)

        @pl.when(ki == qi)
        def _():
            step(True)
            o_ref[...] = (acc_s[...] / _head_rows([l_s[h] for h in range(HP)], bq)).T
            for h in range(HP):
                lse_ref[h] = m_s[h] + jnp.log(l_s[h]) * LOG2E

    qcol = lambda hp, i, qs, ks: (hp, qs[i])
    kcol = lambda hp, i, qs, ks: (hp, ks[i])
    qrow = lambda hp, i, qs, ks: (hp, 0, qs[i])
    return _pcall(
        body, name="attn_fwd",
        grid_spec=pltpu.PrefetchScalarGridSpec(
            num_scalar_prefetch=2, grid=(NH // HP, qs.shape[0]),
            in_specs=[pl.BlockSpec((BW, bq), qcol),
                      pl.BlockSpec((HP, bk, DH), lambda hp, i, qs, ks: (hp, ks[i], 0)),
                      pl.BlockSpec((BW, bk), kcol),
                      pl.BlockSpec((HP, 1, bq), qrow),
                      pl.BlockSpec((HP, bk, 1), lambda hp, i, qs, ks: (hp, ks[i], 0))],
            out_specs=[pl.BlockSpec((bq, BW), lambda hp, i, qs, ks: (qs[i], hp)),
                       pl.BlockSpec((HP, 1, bq), qrow)],
            scratch_shapes=[pltpu.VMEM((HP, 1, bq), f32), pltpu.VMEM((HP, 1, bq), f32),
                            pltpu.VMEM((BW, bq), f32)]),
        out_shape=[jax.ShapeDtypeStruct((T, W), f32), jax.ShapeDtypeStruct((NH, 1, T), f32)],
        compiler_params=_params("parallel", "arbitrary"),
    )(qs, ks, qt, kh, vt, crow, ccol)


def _attn_bwd_prep(dy, oh, z, tm):
    T = dy.shape[0]
    cg = OFF_ATT // W + 3

    def body(dy_ref, o_ref, g_ref, dot_ref, dl_ref):
        do = (dy_ref[...] * _silu(g_ref[...])).astype(bf16)
        dot_ref[...] = do.astype(f32).T.astype(bf16)
        prod = (do.astype(f32) * o_ref[...]).T
        for h in range(NH):
            dl_ref[h] = jnp.sum(prod[h * DH:(h + 1) * DH, :], axis=0, keepdims=True)

    return _pcall(
        body, name="attn_bwd_prep", grid=(T // tm,),
        in_specs=[pl.BlockSpec((tm, W), lambda i: (i, 0)),
                  pl.BlockSpec((tm, W), lambda i: (i, 0)),
                  _zblock(tm, cg)],
        out_specs=[pl.BlockSpec((W, tm), lambda i: (0, i)),
                   pl.BlockSpec((NH, 1, tm), lambda i: (0, 0, i))],
        out_shape=[jax.ShapeDtypeStruct((W, T), bf16), jax.ShapeDtypeStruct((NH, 1, T), f32)],
        compiler_params=_params("parallel"),
    )(dy, oh, z)


def _slab_exchange_phases(buf, land, row0, rows, send_sems, recv_sems, local_sem):
    me = _my_id()

    def local():
        return pltpu.make_async_copy(buf.at[me, pl.ds(row0, rows), :], land.at[me], local_sem)

    def remote(k, receive):
        peer, pid = _peer(k)
        return pltpu.make_async_remote_copy(
            src_ref=buf.at[pid, pl.ds(row0, rows), :], dst_ref=land.at[pid] if receive else land.at[me],
            send_sem=send_sems.at[k - 1], recv_sem=recv_sems.at[k - 1],
            device_id=peer, device_id_type=pl.DeviceIdType.MESH)

    def start():
        local().start()
        for k in range(1, N_DEV):
            remote(k, False).start()

    def finish():
        for k in range(1, N_DEV):
            remote(k, True).wait_recv()
        for k in range(1, N_DEV):
            remote(k, False).wait_send()
        local().wait()

    return start, finish


def _attn_bwd(qt, kt, kh, vh, crow, ccol, dot, lse, delta, bq, exchange=None):
    T = qt.shape[1]
    nq = T // bq
    bk = bq
    qs, ks = _causal_pairs(nq, key_major=True)
    BW = HP * DH
    exchange = list(exchange or [])
    nx = len(exchange)

    def body(qs_ref, ks_ref, qt_ref, kt_ref, k_ref, v_ref, cr_ref, cc_ref, dot_ref, lse_ref, dl_ref, *rest):
        dq_ref, dk_ref, dv_ref, dck_ref, dcq_ref = rest[nx:nx + 5]
        dq_s, dk_s, dv_s, dck_s = rest[2 * nx + 5:2 * nx + 9]
        i = pl.program_id(1)
        qi, ki = qs_ref[i], ks_ref[i]
        phases = [_slab_exchange_phases(rest[e], rest[nx + 5 + e], exchange[e][1], exchange[e][2],
                                        *rest[2 * nx + 9 + 3 * e:2 * nx + 12 + 3 * e]) for e in range(nx)]
        if nx:
            @pl.when((pl.program_id(0) == 0) & (i == 0))
            def _():
                for start, _ in phases:
                    start()

        @pl.when(i == 0)
        def _():
            dq_s[...] = jnp.zeros_like(dq_s)
            dcq_ref[...] = jnp.zeros_like(dcq_ref)

        @pl.when(qi == ki)
        def _():
            dk_s[...] = jnp.zeros_like(dk_s)
            dv_s[...] = jnp.zeros_like(dv_s)
            dck_s[...] = jnp.zeros_like(dck_s)

        def step(diagonal):
            colsums = []
            for h in range(HP):
                rows = slice(h * DH, (h + 1) * DH)
                qth, doth = qt_ref[rows, :], dot_ref[rows, :]
                p = jnp.exp2(_mm(k_ref[h], qth) + (cr_ref[h] - lse_ref[h]) - cc_ref[h])
                if diagonal:
                    p = jnp.where(_iota2((bk, bq), 0) <= _iota2((bk, bq), 1), p, 0.0)
                dv_s[rows, :] += _mm_nt(doth, p.astype(bf16))
                ds = p * (_mm(v_ref[h], doth) - dl_ref[h])
                dsb = ds.astype(bf16)
                dk_s[rows, :] += _mm_nt(qth, dsb)
                dq_s[qi, rows, :] += _mm(kt_ref[rows, :], dsb)
                part = ds[:, 0:128]
                for c in range(1, bq // 128):
                    part = part + ds[:, c * 128:(c + 1) * 128]
                dck_s[h] += part
                colsums.append(jnp.sum(ds, axis=0, keepdims=True))
            dcq_ref[qi] += _stack_rows(colsums, bq)

        @pl.when(qi > ki)
        def _():
            step(False)

        @pl.when(qi == ki)
        def _():
            step(True)

        @pl.when(qi == nq - 1)
        def _():
            dk_ref[...] = (dk_s[...] * (1.0 / LOG2E)).T
            dv_ref[...] = dv_s[...].T
            lane = _iota2((bk, 128), 1)
            out = jnp.zeros((bk, 128), f32)
            for h in range(HP):
                out = out - jnp.where(lane == pl.program_id(0) * HP + h,
                                      jnp.sum(dck_s[h], axis=1, keepdims=True), 0.0)
            dck_ref[...] = out

        @pl.when(i == qs.shape[0] - 1)
        def _():
            for qb in range(nq):
                dq_ref[qb * bq:(qb + 1) * bq, :] = dq_s[qb].T

        if nx:
            @pl.when((pl.program_id(0) == NH // HP - 1) & (i == qs.shape[0] - 1))
            def _():
                for _, finish in phases:
                    finish()

    qcol = lambda hp, i, qs, ks: (hp, qs[i])
    kcol = lambda hp, i, qs, ks: (hp, ks[i])
    qrow = lambda hp, i, qs, ks: (hp, 0, qs[i])
    kh_spec = pl.BlockSpec((HP, bk, DH), lambda hp, i, qs, ks: (hp, ks[i], 0))
    hbm = pl.BlockSpec(memory_space=pltpu.HBM)
    extra_in = [e[0] for e in exchange]
    extra_out = [jax.ShapeDtypeStruct((N_DEV, e[2], e[0].shape[2]), e[0].dtype) for e in exchange]
    extra_scratch = [pltpu.SemaphoreType.DMA((N_DEV - 1,)), pltpu.SemaphoreType.DMA((N_DEV - 1,)),
                     pltpu.SemaphoreType.DMA] * nx
    return _pcall(
        body, name="attn_bwd_exchange" if nx else "attn_bwd",
        grid_spec=pltpu.PrefetchScalarGridSpec(
            num_scalar_prefetch=2, grid=(NH // HP, qs.shape[0]),
            in_specs=[pl.BlockSpec((BW, bq), qcol), pl.BlockSpec((BW, bk), kcol), kh_spec, kh_spec,
                      pl.BlockSpec((HP, 1, bq), qrow),
                      pl.BlockSpec((HP, bk, 1), lambda hp, i, qs, ks: (hp, ks[i], 0)),
                      pl.BlockSpec((BW, bq), qcol), pl.BlockSpec((HP, 1, bq), qrow), pl.BlockSpec((HP, 1, bq), qrow)]
            + [hbm] * nx,
            out_specs=[pl.BlockSpec((T, BW), lambda hp, i, qs, ks: (0, hp)),
                       pl.BlockSpec((bk, BW), lambda hp, i, qs, ks: (ks[i], hp)),
                       pl.BlockSpec((bk, BW), lambda hp, i, qs, ks: (ks[i], hp)),
                       pl.BlockSpec((None, bk, 128), lambda hp, i, qs, ks: (hp, ks[i], 0)),
                       pl.BlockSpec((None, nq, 8, bq), lambda hp, i, qs, ks: (hp, 0, 0, 0))] + [hbm] * nx,
            scratch_shapes=[pltpu.VMEM((nq, BW, bq), f32), pltpu.VMEM((BW, bk), f32), pltpu.VMEM((BW, bk), f32),
                            pltpu.VMEM((HP, bk, 128), f32)] + extra_scratch),
        out_shape=[jax.ShapeDtypeStruct((T, W), f32)] * 3 + [jax.ShapeDtypeStruct((NH // HP, T, 128), f32),
                                                             jax.ShapeDtypeStruct((NH // HP, nq, 8, bq), f32)]
        + extra_out,
        compiler_params=_params("arbitrary" if nx else "parallel", "arbitrary"),
    )(qs, ks, qt, kt, kh, vh, crow, ccol, dot, lse, delta, *extra_in)


def _attn_post(z, dy, oh, dqh, dkh, dvh, dck, dcq, fbias, gq, gk, dzbuf, tm):
    T = z.shape[0]
    c0 = OFF_ATT // W
    nt = T // tm

    def body(q_ref, k_ref, g_ref, f_ref, dy_ref, o_ref, dq_ref, dk_ref, dv_ref, dck_ref, dcq_ref, fb_ref, gq_ref,
             gk_ref, dzin_ref, dz_ref, ggq_ref, ggk_ref, gfb_ref, carry):
        @pl.when(pl.program_id(0) == 0)
        def _():
            carry[...] = jnp.zeros_like(carry)
            ggq_ref[...] = jnp.zeros_like(ggq_ref)
            ggk_ref[...] = jnp.zeros_like(ggk_ref)
            gfb_ref[...] = jnp.zeros_like(gfb_ref)

        gm = _group_mean_matrix(W, DH)
        hs = jnp.where((_iota2((W, W), 0) & (DH - 1)) == (_iota2((W, W), 1) & (DH - 1)), 1.0, 0.0).astype(f32)

        def norm_bwd(x, dn, gain):
            r = lax.rsqrt(_group_mean(x * x, gm) + EPS)
            gg = jnp.sum(dn * x * r, axis=0, keepdims=True)
            u = dn * gain
            return r * u - x * (r * r * r) * _group_mean(u * x, gm), gg

        q, k, gate = q_ref[...], k_ref[...], g_ref[...]
        dq, ggq = norm_bwd(q, dq_ref[...] * (DH ** -0.5), gq_ref[...])
        dk, ggk = norm_bwd(k, dk_ref[...], gk_ref[...])
        ggq_ref[...] += jnp.dot(jnp.broadcast_to(ggq, (8, W)), hs, precision=HI, preferred_element_type=f32)[0:1]
        ggk_ref[...] += jnp.dot(jnp.broadcast_to(ggk, (8, W)), hs, precision=HI, preferred_element_type=f32)[0:1]
        dgate = dy_ref[...] * o_ref[...] * _dsilu(gate)
        dck_v = dcq_ref[...]
        for hp in range(NH // HP):
            dck_v = dck_v + dck_ref[hp]
        rc = jnp.dot(_upper_tri(tm), dck_v, precision=HI, preferred_element_type=f32) + carry[...]
        carry[...] += jnp.sum(dck_v, axis=0, keepdims=True)
        f = f_ref[...] + fb_ref[...]
        df = jnp.where(_iota2((tm, 128), 1) < NH, rc * _sigmoid(-f), 0.0)
        gfb_ref[...] += jnp.sum(df, axis=0, keepdims=True)
        dz_ref[:, 0:W] = dq.astype(bf16)
        dz_ref[:, W:2 * W] = dk.astype(bf16)
        dz_ref[:, 2 * W:3 * W] = dv_ref[...].astype(bf16)
        dz_ref[:, 3 * W:4 * W] = dgate.astype(bf16)
        dz_ref[:, 4 * W:4 * W + 128] = df.astype(bf16)

    rev = lambda i: nt - 1 - i
    zb = lambda col: pl.BlockSpec((tm, W), lambda i, c=col: (rev(i), c))
    hspec = pl.BlockSpec((tm, W), lambda i: (rev(i), 0))
    return _pcall(
        body, name="attn_post", grid=(nt,),
        in_specs=[zb(c0), zb(c0 + 1), zb(c0 + 3),
                  pl.BlockSpec((tm, 128), lambda i: (rev(i), OFF_F // 128)),
                  pl.BlockSpec((tm, W), lambda i: (rev(i), 0)),
                  hspec, hspec, hspec, hspec,
                  pl.BlockSpec((NH // HP, tm, 128), lambda i: (0, rev(i), 0)),
                  pl.BlockSpec((tm, 128), lambda i: (rev(i), 0)),
                  pl.BlockSpec((1, 128), lambda i: (0, 0)),
                  pl.BlockSpec((1, W), lambda i: (0, 0)), pl.BlockSpec((1, W), lambda i: (0, 0)),
                  pl.BlockSpec(memory_space=pl.ANY)],
        out_specs=[pl.BlockSpec((tm, 4 * W + 128), lambda i: (rev(i), OFF_ATT // (4 * W + 128))),
                   pl.BlockSpec((1, W), lambda i: (0, 0)), pl.BlockSpec((1, W), lambda i: (0, 0)),
                   pl.BlockSpec((1, 128), lambda i: (0, 0))],
        out_shape=[jax.ShapeDtypeStruct((T, NZ), bf16), jax.ShapeDtypeStruct((1, W), f32),
                   jax.ShapeDtypeStruct((1, W), f32), jax.ShapeDtypeStruct((1, 128), f32)],
        scratch_shapes=[pltpu.VMEM((1, 128), f32)],
        input_output_aliases={14: 0},
        compiler_params=_params("arbitrary"),
    )(z, z, z, z, dy, oh, dqh, dkh, dvh, dck, dcq, fbias, gq, gk, dzbuf)


def _merge_fwd(ya, oh, z, yc, yd, mb, x, p, wup, wo, gp, wpg, wpp, tm):
    T = x.shape[0]
    cg = OFF_ATT // W + 3

    def body(ya_ref, oh_ref, bg_ref, yc_ref, yd_ref, ml_ref, mb_ref, x_ref, p_ref, wup_ref, wo_ref, gp_ref,
             wpg_ref, wpp_ref, yb_ref, mg_ref, x1_ref, x2_ref):
        yb = (oh_ref[...] * _silu(bg_ref[...])).astype(bf16)
        yb_ref[...] = yb
        ys = (ya_ref[...], yb, yc_ref[...], yd_ref[...])
        merged = jnp.zeros((tm, D), f32)
        for b in range(NBR):
            sg = _sigmoid(ml_ref[:, b * D:(b + 1) * D] + mb_ref[b:b + 1, :])
            merged = merged + sg * _mm(ys[b], wup_ref[b])
        mgb = merged.astype(bf16)
        mg_ref[...] = mgb
        x1 = x_ref[...] + _mm(mgb, wo_ref[...])
        x1_ref[...] = x1
        r = lax.rsqrt(jnp.mean(x1 * x1, axis=-1, keepdims=True) + EPS)
        hp = (x1 * r * gp_ref[...]).astype(bf16)
        gate = _sigmoid(_mm(hp, wpg_ref[...]))
        x2_ref[...] = x1 + gate * _mm(p_ref[...].astype(bf16), wpp_ref[...])

    row = lambda width: pl.BlockSpec((tm, width), lambda i: (i, 0))
    full = lambda *shape: pl.BlockSpec(shape, lambda i: (0,) * len(shape))
    return _pcall(
        body, name="merge_fwd", grid=(T // tm,),
        in_specs=[row(W), row(W), _zblock(tm, cg), row(W), row(W),
                  pl.BlockSpec((tm, NBR * D), lambda i: (i, 0)), full(NBR, D), row(D), row(PLE),
                  full(NBR, W, D), full(D, D), full(1, D), full(D, D), full(PLE, D)],
        out_specs=[row(W), row(D), row(D), row(D)],
        out_shape=[jax.ShapeDtypeStruct((T, W), bf16), jax.ShapeDtypeStruct((T, D), bf16),
                   jax.ShapeDtypeStruct((T, D), f32), jax.ShapeDtypeStruct((T, D), f32)],
        compiler_params=_params("parallel"),
    )(ya, oh, z, yc, yd, z, mb, x, p, wup, wo, gp, wpg, wpp)


def _layer_slabs(li, bufs):
    if bufs is None:
        return [], []
    return list(bufs), [pl.BlockSpec(memory_space=pl.ANY)] * len(bufs)


def _ple_bwd(dx2, x1, p, gp, wpg, wpp, tm, li, bufs):
    T = x1.shape[0]
    SH = D // N_DEV
    nt = T // tm
    extra, extra_specs = _layer_slabs(li, bufs)

    def body(dx2_ref, x1_ref, p_ref, gp_ref, wpg_ref, wpp_ref, *rest):
        dx1_ref, gwpg_ref, gwpp_ref, ggp_ref, gwpg_acc, gwpp_acc = rest[len(extra):]

        @pl.when(pl.program_id(0) == 0)
        def _():
            gwpg_acc[...] = jnp.zeros_like(gwpg_acc)
            gwpp_acc[...] = jnp.zeros_like(gwpp_acc)
            ggp_ref[...] = jnp.zeros_like(ggp_ref)

        x1, dx2 = x1_ref[...], dx2_ref[...]
        r = lax.rsqrt(jnp.mean(x1 * x1, axis=-1, keepdims=True) + EPS)
        xh = x1 * r
        gp = gp_ref[...]
        hp = (xh * gp).astype(bf16)
        gate = _sigmoid(_mm(hp, wpg_ref[...]))
        pb = p_ref[...].astype(bf16)
        pp = _mm(pb, wpp_ref[...])
        dpre = ((dx2 * pp) * gate * (1.0 - gate)).astype(bf16)
        gwpp_acc[...] += _mm_tn(pb, (dx2 * gate).astype(bf16))
        gwpg_acc[...] += _mm_tn(hp, dpre)
        dhp = _mm_nt(dpre, wpg_ref[...])
        ggp_ref[...] += jnp.sum(dhp * xh, axis=0, keepdims=True)
        u = dhp * gp
        dx1_ref[...] = dx2 + r * u - x1 * (r * r * r) * jnp.mean(u * x1, axis=-1, keepdims=True)

        @pl.when(pl.program_id(0) == nt - 1)
        def _():
            gwpg_ref[...] = gwpg_acc[...].reshape(N_DEV, SH, D).astype(bf16)
            for d in range(N_DEV):
                gwpp_ref[d] = gwpp_acc[:, d * SH:(d + 1) * SH].astype(bf16)

    row = lambda width: pl.BlockSpec((tm, width), lambda i: (i, 0))
    full = lambda *shape: pl.BlockSpec(shape, lambda i: (0,) * len(shape))
    n_in = 6
    return _pcall(
        body, name="ple_bwd", grid=(nt,),
        in_specs=[row(D), row(D), row(PLE), full(1, D), full(D, D), full(PLE, D)] + extra_specs,
        out_specs=[row(D), pl.BlockSpec((N_DEV, SH, D), lambda i: (0, li, 0)),
                   pl.BlockSpec((N_DEV, PLE, SH), lambda i: (0, li, 0)), full(1, D)],
        out_shape=[jax.ShapeDtypeStruct((T, D), f32), jax.ShapeDtypeStruct((N_DEV, DEPTH * SH, D), bf16),
                   jax.ShapeDtypeStruct((N_DEV, DEPTH * PLE, SH), bf16), jax.ShapeDtypeStruct((1, D), f32)],
        scratch_shapes=[pltpu.VMEM((D, D), f32), pltpu.VMEM((PLE, D), f32)],
        input_output_aliases={n_in + k: 1 + k for k in range(len(extra))},
        compiler_params=_params("arbitrary"),
    )(dx2, x1, p, gp, wpg, wpp, *extra)


def _merge_bwd(dx1, mg, ya, yb, yc, yd, z, mb, wup, wo, tm, li, bufs):
    T = dx1.shape[0]
    SH = D // N_DEV
    nt = T // tm
    extra, extra_specs = _layer_slabs(li, bufs)

    def body(dx1_ref, mg_ref, ya_ref, yb_ref, yc_ref, yd_ref, ml_ref, mb_ref, wup_ref, wo_ref, *rest):
        dml_ref, dya_ref, dyb_ref, dyc_ref, dyd_ref, gwo_ref, gwup_ref, gmb_ref, gwo_acc, gwup_acc = rest[len(extra):]

        @pl.when(pl.program_id(0) == 0)
        def _():
            gwo_acc[...] = jnp.zeros_like(gwo_acc)
            gwup_acc[...] = jnp.zeros_like(gwup_acc)
            gmb_ref[...] = jnp.zeros_like(gmb_ref)

        dx1b = dx1_ref[...].astype(bf16)
        gwo_acc[...] += _mm_tn(mg_ref[...], dx1b)
        dm = _mm_nt(dx1b, wo_ref[...])
        ys = (ya_ref, yb_ref, yc_ref, yd_ref)
        dys = (dya_ref, dyb_ref, dyc_ref, dyd_ref)
        for b in range(NBR):
            y = ys[b][...]
            up = _mm(y, wup_ref[b])
            sg = _sigmoid(ml_ref[:, b * D:(b + 1) * D] + mb_ref[b:b + 1, :])
            dup = (dm * sg).astype(bf16)
            dml = dm * up * sg * (1.0 - sg)
            gmb_ref[b:b + 1, :] += jnp.sum(dml, axis=0, keepdims=True)
            dml_ref[:, b * D:(b + 1) * D] = dml.astype(bf16)
            gwup_acc[b] += _mm_tn(y, dup)
            dys[b][...] = _mm_nt(dup, wup_ref[b])

        @pl.when(pl.program_id(0) == nt - 1)
        def _():
            gwo_ref[...] = gwo_acc[...].reshape(N_DEV, SH, D).astype(bf16)
            for d in range(N_DEV):
                gwup_ref[d] = gwup_acc[:, :, d * SH:(d + 1) * SH].reshape(NBR * W, SH).astype(bf16)

    row = lambda width: pl.BlockSpec((tm, width), lambda i: (i, 0))
    full = lambda *shape: pl.BlockSpec(shape, lambda i: (0,) * len(shape))
    n_in = 10
    return _pcall(
        body, name="merge_bwd", grid=(nt,),
        in_specs=[row(D), row(D), row(W), row(W), row(W), row(W), row(NBR * D), full(NBR, D),
                  full(NBR, W, D), full(D, D)] + extra_specs,
        out_specs=[row(NBR * D), row(W), row(W), row(W), row(W),
                   pl.BlockSpec((N_DEV, SH, D), lambda i: (0, li, 0)),
                   pl.BlockSpec((N_DEV, NBR * W, SH), lambda i: (0, li, 0)), full(NBR, D)],
        out_shape=[jax.ShapeDtypeStruct((T, NZ), bf16)] + [jax.ShapeDtypeStruct((T, W), f32)] * 4
        + [jax.ShapeDtypeStruct((N_DEV, DEPTH * SH, D), bf16),
           jax.ShapeDtypeStruct((N_DEV, DEPTH * NBR * W, SH), bf16),
           jax.ShapeDtypeStruct((NBR, D), f32)],
        scratch_shapes=[pltpu.VMEM((D, D), f32), pltpu.VMEM((NBR, W, D), f32)],
        input_output_aliases={n_in + k: 5 + k for k in range(len(extra))},
        compiler_params=_params("arbitrary"),
    )(dx1, mg, ya, yb, yc, yd, z, mb, wup, wo, *extra)


def _loss_head(y, target, tm):
    T = y.shape[0]

    def body(y_ref, t_ref, loss_ref, dy_ref, acc):
        i = pl.program_id(0)

        @pl.when(i == 0)
        def _():
            acc[...] = jnp.zeros_like(acc)

        e = y_ref[...] - t_ref[...]
        dy_ref[...] = e * (1.0 / D)
        acc[...] += jnp.sum(e * e, axis=0, keepdims=True)

        @pl.when(i == T // tm - 1)
        def _():
            loss_ref[...] = jnp.sum(acc[...], axis=1, keepdims=True) * (0.5 / D)

    return _pcall(
        body, name="loss_head", grid=(T // tm,),
        in_specs=[pl.BlockSpec((tm, D), lambda i: (i, 0)), pl.BlockSpec((tm, D), lambda i: (i, 0))],
        out_specs=[pl.BlockSpec((1, 1), lambda i: (0, 0)), pl.BlockSpec((tm, D), lambda i: (i, 0))],
        out_shape=[jax.ShapeDtypeStruct((1, 1), f32), jax.ShapeDtypeStruct((T, D), f32)],
        scratch_shapes=[pltpu.VMEM((1, D), f32)],
        compiler_params=_params("arbitrary"),
    )(y, target)


def _lb_softmax_rows(l_ref):
    rows = [l_ref[i:i + 1, :] for i in range(DEPTH)]
    m = rows[0]
    for r in rows[1:]:
        m = jnp.maximum(m, r)
    es = [jnp.exp(r - m) for r in rows]
    tot = es[0]
    for e in es[1:]:
        tot = tot + e
    return [e / tot for e in es]


def _lb_partial_sums(pr):
    sums = [jnp.zeros_like(pr[0])]
    for i in range(1, DEPTH):
        sums.append(sums[-1] + pr[i])
    return sums


def _stack_rows(rows, width):
    idx = _iota2((8, width), 0)
    out = jnp.zeros((8, width), f32)
    for i, r in enumerate(rows):
        out = jnp.where(idx == i, r, out)
    return out


def _lower_bounds(lb_logits):
    def body(l_ref, o_ref):
        sums = _lb_partial_sums(_lb_softmax_rows(l_ref))
        o_ref[...] = _stack_rows([jnp.clip(s, 0.0, 1.0) for s in sums], W)

    return _pcall(body, name="lower_bounds", out_shape=jax.ShapeDtypeStruct((8, W), f32))(lb_logits)


def _lower_bounds_bwd(lb_logits, dlower):
    def body(l_ref, d_ref, o_ref):
        pr = _lb_softmax_rows(l_ref)
        sums = _lb_partial_sums(pr)
        dl = [jnp.where((sums[i] > 0.0) & (sums[i] < 1.0), d_ref[i:i + 1, :], 0.0) for i in range(DEPTH)]
        dp = [jnp.zeros_like(pr[0])] * DEPTH
        run = jnp.zeros_like(pr[0])
        for j in reversed(range(1, DEPTH)):
            run = run + dl[j]
            dp[j] = run
        inner = pr[0] * dp[0]
        for j in range(1, DEPTH):
            inner = inner + pr[j] * dp[j]
        o_ref[...] = _stack_rows([pr[j] * (dp[j] - inner) for j in range(DEPTH)], W)

    return _pcall(body, name="lower_bounds_bwd", out_shape=jax.ShapeDtypeStruct((8, W), f32))(lb_logits, dlower)


def _row_tile(rows, cols, budget_bytes=1 << 20, mult=8):
    if rows % mult:
        return rows
    best = mult
    for t in range(mult, rows + 1, mult):
        if rows % t == 0 and t * cols * 4 <= budget_bytes:
            best = t
    return best


def _sum_slabs(land):
    N, R, C = land.shape
    tr = _row_tile(R, C * N, mult=16)

    def body(l_ref, o_ref):
        acc = l_ref[0].astype(f32)
        for j in range(1, N):
            acc = acc + l_ref[j].astype(f32)
        o_ref[...] = acc

    return _pcall(
        body, name="sum_slabs", grid=(R // tr,),
        in_specs=[pl.BlockSpec((N, tr, C), lambda i: (0, i, 0))],
        out_specs=pl.BlockSpec((tr, C), lambda i: (i, 0)),
        out_shape=jax.ShapeDtypeStruct((R, C), f32),
        compiler_params=_params("parallel"),
    )(land)


def _adamw_update(w_ref, g_ref, m_ref, v_ref, d_ref, nm_ref, nv_ref):
    c1 = 1.0 / (1.0 - ADAM_B1 ** ADAM_STEP)
    c2 = 1.0 / (1.0 - ADAM_B2 ** ADAM_STEP)
    gv = g_ref[...]
    nm = ADAM_B1 * m_ref[...] + (1.0 - ADAM_B1) * gv
    nv = ADAM_B2 * v_ref[...] + (1.0 - ADAM_B2) * (gv * gv)
    nm_ref[...] = nm
    nv_ref[...] = nv
    d_ref[...] = -ADAM_LR * ((nm * c1) / (jnp.sqrt(nv * c2) + ADAM_EPS) + ADAM_WD * w_ref[...])


def _adamw3(w, g, m, v):
    L, R, C = w.shape
    tr = _row_tile(R, C)

    def body(*refs):
        _adamw_update(*refs)

    spec = pl.BlockSpec((None, tr, C), lambda l, i: (l, i, 0))
    return _pcall(
        body, name="adamw3", grid=(L, R // tr),
        in_specs=[spec] * 4, out_specs=[spec] * 3,
        out_shape=[jax.ShapeDtypeStruct((L, R, C), f32)] * 3,
        compiler_params=_params("parallel", "parallel"),
    )(w, g, m, v)


def _adamw(w, g, m, v):
    if w.ndim == 3:
        return _adamw3(w, g, m, v)
    R, C = w.shape
    tr = _row_tile(R, C)
    c1 = 1.0 / (1.0 - ADAM_B1 ** ADAM_STEP)
    c2 = 1.0 / (1.0 - ADAM_B2 ** ADAM_STEP)

    def body(w_ref, g_ref, m_ref, v_ref, d_ref, nm_ref, nv_ref):
        gv = g_ref[...]
        nm = ADAM_B1 * m_ref[...] + (1.0 - ADAM_B1) * gv
        nv = ADAM_B2 * v_ref[...] + (1.0 - ADAM_B2) * (gv * gv)
        nm_ref[...] = nm
        nv_ref[...] = nv
        d_ref[...] = -ADAM_LR * ((nm * c1) / (jnp.sqrt(nv * c2) + ADAM_EPS) + ADAM_WD * w_ref[...])

    spec = pl.BlockSpec((tr, C), lambda i: (i, 0))
    return _pcall(
        body, name="adamw", grid=(R // tr,),
        in_specs=[spec] * 4, out_specs=[spec] * 3,
        out_shape=[jax.ShapeDtypeStruct((R, C), f32)] * 3,
        compiler_params=_params("parallel"),
    )(w, g, m, v)


def _my_id():
    return lax.axis_index("x") * 4 + lax.axis_index("y") * 2 + lax.axis_index("c")


def _peer(k):
    x, y, c = lax.axis_index("x"), lax.axis_index("y"), lax.axis_index("c")
    kx, ky, kc = (k >> 2) & 1, (k >> 1) & 1, k & 1
    px, py, pc = x ^ kx, y ^ ky, c ^ kc
    return (px, py, pc), px * 4 + py * 2 + pc


def _all_gather(shards, axes):
    n = len(shards)

    def body(*refs):
        start, forward, finish = _gather_phases(shards, axes, refs[:n], refs[n:2 * n], *refs[2 * n:])
        start()
        forward()
        finish()

    hbm = pl.BlockSpec(memory_space=pltpu.HBM)
    return _pcall(
        body, name="all_gather",
        in_specs=[hbm] * n, out_specs=[hbm] * n,
        out_shape=_gathered_shapes(shards, axes),
        scratch_shapes=_gather_semaphores(n),
    )(*shards)


def _gathered_shapes(shards, axes):
    def full_shape(s, ax):
        shp = list(s.shape)
        shp[ax] *= N_DEV
        return tuple(shp)

    return [jax.ShapeDtypeStruct(full_shape(s, ax), s.dtype) for s, ax in zip(shards, axes)]


def _gather_semaphores(n):
    return [pltpu.SemaphoreType.DMA((n, N_DEV - 1)), pltpu.SemaphoreType.DMA((n, N_DEV - 1)),
            pltpu.SemaphoreType.DMA((n,))]


def _gather_phases(shards, axes, srcs, outs, send_sems, recv_sems, local_sems):
    n = len(shards)
    x, y, c = lax.axis_index("x"), lax.axis_index("y"), lax.axis_index("c")
    me, sibling = (x, y, c), (x, y, 1 - c)
    chips = [(1 - x, y), (x, 1 - y), (1 - x, 1 - y)]

    def block(a, dev):
        j = dev[0] * 4 + dev[1] * 2 + dev[2]
        size = shards[a].shape[axes[a]]
        start = pl.multiple_of(j * size, size)
        if axes[a] == 0:
            return outs[a].at[pl.ds(start, size), :]
        if axes[a] == 1:
            return outs[a].at[:, pl.ds(start, size), :]
        return outs[a].at[:, pl.ds(start, size)]

    def copy(a, k, dev, to, src=None):
        return pltpu.make_async_remote_copy(
            src_ref=block(a, dev) if src is None else src, dst_ref=block(a, dev),
            send_sem=send_sems.at[a, k], recv_sem=recv_sems.at[a, k],
            device_id=to, device_id_type=pl.DeviceIdType.MESH)

    def mine():
        return [pltpu.make_async_copy(srcs[a], block(a, me), local_sems.at[a]) for a in range(n)]

    def first():
        cps = []
        for a in range(n):
            cps.append(copy(a, 0, me, sibling, src=srcs[a]))
            cps += [copy(a, 1 + j, me, (*chip, c), src=srcs[a]) for j, chip in enumerate(chips)]
        return cps

    def passed():
        return [copy(a, 4 + j, (*chip, c), sibling) for j, chip in enumerate(chips) for a in range(n)]

    def start():
        for cp in mine() + first():
            cp.start()

    def forward():
        for j, chip in enumerate(chips):
            for a in range(n):
                copy(a, 1 + j, (*chip, c), me).wait_recv()
                copy(a, 4 + j, (*chip, c), sibling).start()

    def finish():
        for a in range(n):
            copy(a, 0, sibling, me).wait_recv()
            for j, chip in enumerate(chips):
                copy(a, 4 + j, (*chip, 1 - c), me).wait_recv()
        for cp in first() + passed():
            cp.wait_send()
        for cp in mine():
            cp.wait()

    return start, forward, finish


N_CHIP = N_DEV // 2


def _exchange_sibling(sliced):
    n = len(sliced)

    def body(*refs):
        srcs, outs = refs[:n], refs[n:2 * n]
        send_sems, recv_sems = refs[2 * n:]
        x, y, c = lax.axis_index("x"), lax.axis_index("y"), lax.axis_index("c")
        copies = []
        for a in range(n):
            for q in range(N_CHIP):
                cp = pltpu.make_async_remote_copy(
                    src_ref=srcs[a].at[2 * q + (1 - c)], dst_ref=outs[a].at[q],
                    send_sem=send_sems.at[a, q], recv_sem=recv_sems.at[a, q],
                    device_id=(x, y, 1 - c), device_id_type=pl.DeviceIdType.MESH)
                cp.start()
                copies.append(cp)
        for cp in copies:
            cp.wait_recv()
        for cp in copies:
            cp.wait_send()

    hbm = pl.BlockSpec(memory_space=pltpu.HBM)
    return _pcall(
        body, name="grad_exchange_sibling",
        in_specs=[hbm] * n, out_specs=[hbm] * n,
        out_shape=[jax.ShapeDtypeStruct((N_CHIP,) + s.shape[1:], s.dtype) for s in sliced],
        scratch_shapes=[pltpu.SemaphoreType.DMA((n, N_CHIP)), pltpu.SemaphoreType.DMA((n, N_CHIP))],
    )(*sliced)


def _pair_sum(own, recv):
    _, R, C = own.shape
    tr = _row_tile(R, C, mult=16)
    side = lax.axis_index("c").astype(jnp.int32).reshape(1)

    def body(c_ref, own_ref, recv_ref, o_ref):
        o_ref[...] = (own_ref[...].astype(f32) + recv_ref[...].astype(f32)).astype(o_ref.dtype)

    return _pcall(
        body, name="pair_sum",
        grid_spec=pltpu.PrefetchScalarGridSpec(
            num_scalar_prefetch=1, grid=(N_CHIP, R // tr),
            in_specs=[pl.BlockSpec((None, tr, C), lambda q, i, c: (2 * q + c[0], i, 0)),
                      pl.BlockSpec((None, tr, C), lambda q, i, c: (q, i, 0))],
            out_specs=pl.BlockSpec((None, tr, C), lambda q, i, c: (q, i, 0))),
        out_shape=jax.ShapeDtypeStruct((N_CHIP, R, C), own.dtype),
        compiler_params=_params("parallel", "parallel"),
    )(side, own, recv)


def _exchange_chips(partial, whole):
    ns, nw = len(partial), len(whole)

    def body(*refs):
        srcs, outs = refs[:ns + nw], refs[ns + nw:2 * (ns + nw)]
        send_sems, recv_sems, wsend_sems, wrecv_sems, local_sems = refs[2 * (ns + nw):]
        x, y, c = lax.axis_index("x"), lax.axis_index("y"), lax.axis_index("c")
        me, myq = _my_id(), x * 2 + y
        chips = [(1 - x, y), (x, 1 - y), (1 - x, 1 - y)]
        locals_ = [pltpu.make_async_copy(srcs[a].at[myq], outs[a].at[myq], local_sems.at[a]) for a in range(ns)]
        locals_ += [pltpu.make_async_copy(srcs[ns + b], outs[ns + b].at[me], local_sems.at[ns + b])
                    for b in range(nw)]
        for cp in locals_:
            cp.start()
        sends, recvs = [], []
        for j, chip in enumerate(chips):
            q = chip[0] * 2 + chip[1]
            for a in range(ns):
                cp = pltpu.make_async_remote_copy(
                    src_ref=srcs[a].at[q], dst_ref=outs[a].at[myq],
                    send_sem=send_sems.at[a, j], recv_sem=recv_sems.at[a, j],
                    device_id=(*chip, c), device_id_type=pl.DeviceIdType.MESH)
                cp.start()
                sends.append(cp)
                recvs.append(pltpu.make_async_remote_copy(
                    src_ref=srcs[a].at[q], dst_ref=outs[a].at[q],
                    send_sem=send_sems.at[a, j], recv_sem=recv_sems.at[a, j],
                    device_id=(*chip, c), device_id_type=pl.DeviceIdType.MESH))
        for k in range(1, N_DEV):
            peer, pid = _peer(k)
            for b in range(nw):
                cp = pltpu.make_async_remote_copy(
                    src_ref=srcs[ns + b], dst_ref=outs[ns + b].at[me],
                    send_sem=wsend_sems.at[b, k - 1], recv_sem=wrecv_sems.at[b, k - 1],
                    device_id=peer, device_id_type=pl.DeviceIdType.MESH)
                cp.start()
                sends.append(cp)
                recvs.append(pltpu.make_async_remote_copy(
                    src_ref=srcs[ns + b], dst_ref=outs[ns + b].at[pid],
                    send_sem=wsend_sems.at[b, k - 1], recv_sem=wrecv_sems.at[b, k - 1],
                    device_id=peer, device_id_type=pl.DeviceIdType.MESH))
        for cp in recvs:
            cp.wait_recv()
        for cp in sends:
            cp.wait_send()
        for cp in locals_:
            cp.wait()

    hbm = pl.BlockSpec(memory_space=pltpu.HBM)
    shapes = [jax.ShapeDtypeStruct(s.shape, s.dtype) for s in partial]
    shapes += [jax.ShapeDtypeStruct((N_DEV,) + s.shape, s.dtype) for s in whole]
    return _pcall(
        body, name="grad_exchange_chips",
        in_specs=[hbm] * (ns + nw), out_specs=[hbm] * (ns + nw), out_shape=shapes,
        scratch_shapes=[pltpu.SemaphoreType.DMA((ns, N_CHIP - 1)), pltpu.SemaphoreType.DMA((ns, N_CHIP - 1)),
                        pltpu.SemaphoreType.DMA((nw, N_DEV - 1)), pltpu.SemaphoreType.DMA((nw, N_DEV - 1)),
                        pltpu.SemaphoreType.DMA((ns + nw,))],
    )(*partial, *whole)


def _permute_cols(w):
    pad = jnp.zeros(w.shape[:-1] + (NZ - OFF_F - NH,), w.dtype)
    return jnp.concatenate([
        w[..., 3844:7940],
        w[..., 0:1024],
        w[..., 2052:3076],
        w[..., 3076:3844],
        w[..., 1024:2048],
        w[..., 2048:2052], pad], axis=-1)


def _unpermute_cols(g):
    return jnp.concatenate([
        g[..., OFF_CONV:OFF_CONV + 1024],
        g[..., OFF_ATT:OFF_ATT + 1024],
        g[..., OFF_F:OFF_F + NH],
        g[..., OFF_HGRN:OFF_HGRN + 1024],
        g[..., OFF_SGU:OFF_SGU + 768],
        g[..., 0:4096]], axis=-1)


_SMALL = (
    ("norm_mix", (DEPTH, D)), ("conv_w", (DEPTH, CONV_WIDTH, W)), ("conv_b", (DEPTH, W)),
    ("fgate_bias", (DEPTH, NH)), ("q_norm", (DEPTH, DH)), ("k_norm", (DEPTH, DH)),
    ("lb_logits", (DEPTH, W)), ("hgrn_norm", (DEPTH, W)), ("sgu_norm", (DEPTH, W)),
    ("spatial_w", (DEPTH, NH, SGU_CHUNK, SGU_CHUNK)), ("spatial_b", (DEPTH, NH, SGU_CHUNK)),
    ("merge_b", (DEPTH, NBR, D)), ("norm_ple", (DEPTH, D)),
)


def _small_rows(shape):
    size = 1
    for s in shape:
        size *= s
    rows = -(-size // 128)
    return size, -(-rows // 8) * 8


def _pack_small(parts):
    out = []
    for name, shape in _SMALL:
        size, rows = _small_rows(shape)
        flat = parts[name].astype(f32).reshape(-1)
        flat = jnp.pad(flat, (0, rows * 128 - size))
        out.append(flat.reshape(rows, 128))
    return jnp.concatenate(out, axis=0)


def _unpack_small(buf):
    parts, r0 = {}, 0
    for name, shape in _SMALL:
        size, rows = _small_rows(shape)
        parts[name] = buf[r0:r0 + rows].reshape(-1)[:size].reshape(shape)
        r0 += rows
    return parts


def _shard_cols(a, width):
    return lax.dynamic_slice_in_dim(a, _my_id() * width, width, axis=a.ndim - 1)


def kernel(x, p, norm_mix, w_in, conv_w, conv_b, fgate_bias, q_norm, k_norm, lb_logits, hgrn_norm, sgu_norm, spatial_w, spatial_b, w_up, merge_b, w_o, norm_ple, w_ple_gate, w_ple_proj, loss_target, m_norm_mix, m_w_in, m_conv_w, m_conv_b, m_fgate_bias, m_q_norm, m_k_norm, m_lb_logits, m_hgrn_norm, m_sgu_norm, m_spatial_w, m_spatial_b, m_w_up, m_merge_b, m_w_o, m_norm_ple, m_w_ple_gate, m_w_ple_proj, v_norm_mix, v_w_in, v_conv_w, v_conv_b, v_fgate_bias, v_q_norm, v_k_norm, v_lb_logits, v_hgrn_norm, v_sgu_norm, v_spatial_w, v_spatial_b, v_w_up, v_merge_b, v_w_o, v_norm_ple, v_w_ple_gate, v_w_ple_proj):
    T = x.shape[1]
    SH = D // N_DEV
    CW = W // N_DEV
    tm = 512 if T % 512 == 0 else T
    tmm = 256 if T % 256 == 0 else T
    x0 = x.reshape(T, D)
    target = loss_target.reshape(T, D)

    small_shard = jnp.concatenate([
        merge_b.reshape(DEPTH * NBR, SH),
        jnp.pad(conv_w.reshape(DEPTH * CONV_WIDTH, CW), ((0, 16 - DEPTH * CONV_WIDTH), (0, SH - CW)))], axis=0)
    win_s = _permute_cols(w_in).astype(bf16)
    win0, wup_f, wo_f, wpg_f, wpp_f, g_small = _all_gather(
        [win_s[0],
         w_up.astype(bf16).reshape(DEPTH * NBR * W, SH),
         w_o.astype(bf16),
         w_ple_gate.astype(bf16),
         w_ple_proj.astype(bf16).reshape(DEPTH * PLE, SH),
         small_shard],
        [0, -1, 1, 1, -1, -1])
    win_f = [win0]
    win_later = [win_s[li] for li in range(1, DEPTH)]
    wup_f = wup_f.reshape(DEPTH, NBR, W, D)
    wpp_f = wpp_f.reshape(DEPTH, PLE, D)
    mb_f = g_small[0:DEPTH * NBR].reshape(DEPTH, NBR, D)
    cw_f = g_small[16:16 + DEPTH * CONV_WIDTH].reshape(DEPTH, CONV_WIDTH, N_DEV, SH)[..., 0:CW]
    cw_f = cw_f.reshape(DEPTH, CONV_WIDTH, W)

    loss_local, dx, gw, gs_full = _forward_backward(
        x0, p[:, 0], target, win_f, wup_f, wo_f, wpg_f, wpp_f, mb_f, cw_f, norm_mix, conv_b, fgate_bias, q_norm,
        k_norm, lb_logits, hgrn_norm, sgu_norm, spatial_w, spatial_b, norm_ple, win_later)
    loss = lax.psum(loss_local[0, 0], AXES)
    grad_x = dx.reshape(1, T, D)

    weights = dict(norm_mix=norm_mix, w_in=w_in, conv_w=conv_w, conv_b=conv_b, fgate_bias=fgate_bias, q_norm=q_norm,
                   k_norm=k_norm, lb_logits=lb_logits, hgrn_norm=hgrn_norm, sgu_norm=sgu_norm, spatial_w=spatial_w,
                   spatial_b=spatial_b, w_up=w_up, merge_b=merge_b, w_o=w_o, norm_ple=norm_ple,
                   w_ple_gate=w_ple_gate, w_ple_proj=w_ple_proj)
    ms = dict(norm_mix=m_norm_mix, w_in=m_w_in, conv_w=m_conv_w, conv_b=m_conv_b, fgate_bias=m_fgate_bias,
              q_norm=m_q_norm, k_norm=m_k_norm, lb_logits=m_lb_logits, hgrn_norm=m_hgrn_norm, sgu_norm=m_sgu_norm,
              spatial_w=m_spatial_w, spatial_b=m_spatial_b, w_up=m_w_up, merge_b=m_merge_b, w_o=m_w_o,
              norm_ple=m_norm_ple, w_ple_gate=m_w_ple_gate, w_ple_proj=m_w_ple_proj)
    vs = dict(norm_mix=v_norm_mix, w_in=v_w_in, conv_w=v_conv_w, conv_b=v_conv_b, fgate_bias=v_fgate_bias,
              q_norm=v_q_norm, k_norm=v_k_norm, lb_logits=v_lb_logits, hgrn_norm=v_hgrn_norm, sgu_norm=v_sgu_norm,
              spatial_w=v_spatial_w, spatial_b=v_spatial_b, w_up=v_w_up, merge_b=v_merge_b, w_o=v_w_o,
              norm_ple=v_norm_ple, w_ple_gate=v_w_ple_gate, w_ple_proj=v_w_ple_proj)
    return _exchange_and_update(loss, grad_x, gw, gs_full, weights, ms, vs)


def _forward_backward(x0, p, target, win_f, wup_f, wo_f, wpg_f, wpp_f, mb_f, cw_f, norm_mix, conv_b, fgate_bias,
                      q_norm, k_norm, lb_logits, hgrn_norm, sgu_norm, spatial_w, spatial_b, norm_ple, win_later=()):
    T = x0.shape[0]
    tm = 512 if T % 512 == 0 else T
    tmm = 256 if T % 256 == 0 else T
    tmi = 1024 if T % 1024 == 0 else tm
    lower = _lower_bounds(lb_logits)
    fb_pad = jnp.pad(fgate_bias, ((0, 0), (0, 128 - NH)))
    gq_t = jnp.tile(q_norm, (1, NH))
    gk_t = jnp.tile(k_norm, (1, NH))
    sbe = jnp.repeat(jnp.swapaxes(spatial_b, 1, 2), DH, axis=2)

    saved = []
    xc = x0
    p = p[:, None]
    for li in range(DEPTH):
        row = lambda a: a[li:li + 1]
        if li == 0 and win_later:
            z, h, *gathered = _inproj_fwd(xc, row(norm_mix), win_f[0], tmi, gather=win_later)
            win_f = [win_f[0]] + gathered
        else:
            z, h = _inproj_fwd(xc, row(norm_mix), win_f[li], tmi)
        ya = _conv_fwd(z, cw_f[li], row(conv_b), tm)
        yd = _sgu_fwd(z, row(sgu_norm), spatial_w[li], sbe[li], tm)
        yc, o_pre, states = _hgrn_fwd(z, lower[li:li + 1], row(hgrn_norm), tmm)
        qt, kt, vt, kh, vh, cum = _attn_prep(z, row(fb_pad), row(gq_t), row(gk_t), tm)
        cum4 = jnp.transpose(cum[:, 0:NH])
        ccol, crow = cum4[:, :, None], cum4[:, None, :]
        oh, lse = _attn_fwd(qt, kh, vt, crow, ccol, tm)
        yb, mg, x1, x2 = _merge_fwd(ya, oh, z, yc, yd, mb_f[li], xc, p[li, 0], wup_f[li], wo_f[li],
                                    row(norm_ple), wpg_f[li], wpp_f[li], tmm)
        saved.append(dict(x=xc, z=z, h=h, ya=ya, yb=yb, yc=yc, yd=yd, o_pre=o_pre, states=states,
                          qt=qt, kt=kt, kh=kh, vh=vh, crow=crow, ccol=ccol, oh=oh, lse=lse, mg=mg, x1=x1))
        xc = x2

    loss_local, dx = _loss_head(xc, target, tm)

    gw = dict(w_in=None, w_up=None, w_o=None, w_ple_gate=None, w_ple_proj=None)
    gs = {n: [None] * DEPTH for n, _ in _SMALL}
    dlower = [None] * DEPTH
    landed = {n: {} for n in gw}
    for li in reversed(range(DEPTH)):
        s = saved[li]
        row = lambda a: a[li:li + 1]
        first = li == DEPTH - 1
        dx1, gw["w_ple_gate"], gw["w_ple_proj"], ggp = _ple_bwd(
            dx, s["x1"], p[li, 0], row(norm_ple), wpg_f[li], wpp_f[li], tmm, li,
            None if first else (gw["w_ple_gate"], gw["w_ple_proj"]))
        gs["norm_ple"][li] = ggp[0]
        dz, dya, dyb, dyc, dyd, gw["w_o"], gw["w_up"], gs["merge_b"][li] = _merge_bwd(
            dx1, s["mg"], s["ya"], s["yb"], s["yc"], s["yd"], s["z"], mb_f[li], wup_f[li], wo_f[li], tmm, li,
            None if first else (gw["w_o"], gw["w_up"]))
        dz, gcw, gcb = _conv_bwd(s["z"], dya, cw_f[li], row(conv_b), dz, tm)
        gs["conv_w"][li], gs["conv_b"][li] = gcw[0:CONV_WIDTH], gcb[0]
        dz, gs["spatial_w"][li], gsb, ggv = _sgu_bwd(s["z"], dyd, row(sgu_norm), spatial_w[li], sbe[li], dz, tm)
        gs["spatial_b"][li] = jnp.transpose(gsb[:, ::DH])
        gs["sgu_norm"][li] = ggv[0]
        dz, ggn, glb = _hgrn_bwd(s["z"], lower[li:li + 1], row(hgrn_norm), s["o_pre"], s["states"], dyc, dz, tmm)
        gs["hgrn_norm"][li], dlower[li] = ggn[0], glb[0]
        dot, delta = _attn_bwd_prep(dyb, s["oh"], s["z"], tm)
        SH = D // N_DEV
        sent = [] if first else [("w_in", li + 1, SH)]
        sent += [("w_up", li, NBR * W), ("w_o", li, SH), ("w_ple_gate", li, SH), ("w_ple_proj", li, PLE)]
        dqh, dkh, dvh, dck, dcq, *lands = _attn_bwd(
            s["qt"], s["kt"], s["kh"], s["vh"], s["crow"], s["ccol"], dot, s["lse"], delta, tm,
            [(gw[name], layer * rows, rows) for name, layer, rows in sent])
        for (name, layer, _), land in zip(sent, lands):
            landed[name][layer] = land
        dcq_t = jnp.transpose(dcq[:, :, 0:HP, :], (0, 2, 1, 3)).reshape(NH, T)
        dcq_t = jnp.pad(jnp.transpose(dcq_t), ((0, 0), (0, 128 - NH)))
        dz, ggq, ggk, gfb = _attn_post(s["z"], dyb, s["oh"], dqh, dkh, dvh, dck, dcq_t, row(fb_pad),
                                       row(gq_t), row(gk_t), dz, tmm)
        gs["q_norm"][li], gs["k_norm"][li], gs["fgate_bias"][li] = ggq[0, 0:DH], ggk[0, 0:DH], gfb[0, 0:NH]
        dx, gnm = _inproj_bwd_x(dz, win_f[li], s["x"], dx1, row(norm_mix), tmi)
        gs["norm_mix"][li] = gnm[0]
        gw["w_in"] = _inproj_bwd_w(s["h"], dz, tmi, li, gw["w_in"])
    dlower8 = jnp.pad(jnp.stack(dlower), ((0, 8 - DEPTH), (0, 0)))
    gs_full = {n: jnp.stack(v) for n, v in gs.items() if n != "lb_logits"}
    gs_full["lb_logits"] = _lower_bounds_bwd(lb_logits, dlower8)[0:DEPTH]
    gw["landed"] = landed
    return loss_local, dx, gw, gs_full


def _exchange_and_update(loss, grad_x, gw, gs_full, weights, ms, vs):
    SH = D // N_DEV
    CW = W // N_DEV

    small_buf = _pack_small(gs_full)
    landed = gw["landed"]
    rest = [li for li in range(DEPTH) if li not in landed["w_in"]]
    win_rest = jnp.concatenate([gw["w_in"][:, li * SH:(li + 1) * SH] for li in rest], axis=1)
    own = [win_rest]
    from_sibling = _exchange_sibling(own)
    chip_sums = [_pair_sum(o, r) for o, r in zip(own, from_sibling)]
    l_win, l_small = _exchange_chips(chip_sums, [small_buf])

    g_rest = _sum_slabs(l_win)
    g_layers = {li: g_rest[n * SH:(n + 1) * SH] for n, li in enumerate(rest)}
    g_layers.update({li: _sum_slabs(land) for li, land in landed["w_in"].items()})
    g_w_in = _unpermute_cols(jnp.stack([g_layers[li] for li in range(DEPTH)]))

    def landed_sum(name):
        return _sum_slabs(jnp.concatenate([landed[name][li] for li in range(DEPTH)], axis=1))

    g_w_up = landed_sum("w_up").reshape(DEPTH, NBR, W, SH)
    g_w_o = landed_sum("w_o").reshape(DEPTH, SH, D)
    g_w_pg = landed_sum("w_ple_gate").reshape(DEPTH, SH, D)
    g_w_pp = landed_sum("w_ple_proj").reshape(DEPTH, PLE, SH)
    g_small = _unpack_small(_sum_slabs(l_small))
    g_small_local = dict(g_small)
    g_small_local["conv_w"] = _shard_cols(g_small["conv_w"], CW)
    g_small_local["merge_b"] = _shard_cols(g_small["merge_b"], SH)

    grads = dict(w_in=g_w_in, w_up=g_w_up, w_o=g_w_o, w_ple_gate=g_w_pg, w_ple_proj=g_w_pp)
    deltas, new_m, new_v = {}, {}, {}
    for name in ("w_in", "w_up", "w_o", "w_ple_gate", "w_ple_proj"):
        shape = weights[name].shape
        as3 = (shape[0], -1, shape[-1])
        d_, m_, v_ = _adamw(weights[name].reshape(as3), grads[name].reshape(as3),
                            ms[name].reshape(as3), vs[name].reshape(as3))
        deltas[name], new_m[name], new_v[name] = d_.reshape(shape), m_.reshape(shape), v_.reshape(shape)

    def local_shapes(parts):
        return {n: (parts[n] if parts[n].shape == s else jnp.pad(
            parts[n], [(0, 0)] * (len(s) - 1) + [(0, s[-1] - parts[n].shape[-1])])) for n, s in _SMALL}

    d_, m_, v_ = _adamw(_pack_small(local_shapes(weights)), _pack_small(local_shapes(g_small_local)),
                        _pack_small(local_shapes(ms)), _pack_small(local_shapes(vs)))
    for buf, dst in ((d_, deltas), (m_, new_m), (v_, new_v)):
        parts = _unpack_small(buf)
        for n, _ in _SMALL:
            dst[n] = parts[n][..., :weights[n].shape[-1]]
    for n, _ in _SMALL:
        grads[n] = g_small_local[n]

    order = ["norm_mix", "w_in", "conv_w", "conv_b", "fgate_bias", "q_norm", "k_norm", "lb_logits", "hgrn_norm",
             "sgu_norm", "spatial_w", "spatial_b", "w_up", "merge_b", "w_o", "norm_ple", "w_ple_gate", "w_ple_proj"]
    return (loss, grad_x, *[grads[n] for n in order], *[deltas[n] for n in order],
            *[new_m[n] for n in order], *[new_v[n] for n in order])
```

```python
import functools

import jax
import jax.numpy as jnp
from jax import lax
from jax.experimental import pallas as pl
from jax.experimental.pallas import tpu as pltpu

f32 = jnp.float32
bf16 = jnp.bfloat16

D = 1024
W = 256
NH = 4
DH = 64
NBR = 4
PLE = 256
DEPTH = 4
CONV_WIDTH = 3
SGU_CHUNK = 128
GLA_CHUNK = 128
EPS = 1e-6
MASK_VALUE = -1e30
IN_COLS = 7940
NZ = 8064
OFF_CONV = 4096
OFF_HGRN = 5120
OFF_SGU = 6144
OFF_ATT = 6912
OFF_F = 7936
ZT = 1152
NZT = NZ // ZT
EXP_CLAMP = 80.0
LOG2E = 1.4426950408889634

ADAM_LR = 0.001
ADAM_B1 = 0.9
ADAM_B2 = 0.999
ADAM_EPS = 1e-08
ADAM_WD = 0.01
ADAM_STEP = 10

N_DEV = 8
AXES = ("x", "y", "c")
VMEM_LIMIT = 56 * 1024 * 1024
HI = lax.Precision.HIGHEST

NT_DIMS = (((1,), (1,)), ((), ()))
TN_DIMS = (((0,), (0,)), ((), ()))


def _pcall(body, **kw):
    return pl.pallas_call(body, **kw)


def _params(*sem):
    return pltpu.CompilerParams(dimension_semantics=sem, vmem_limit_bytes=VMEM_LIMIT)


def _mm(a, b):
    return jnp.dot(a, b, preferred_element_type=f32)


def _mm_nt(a, b):
    return lax.dot_general(a, b, NT_DIMS, preferred_element_type=f32)


def _mm_tn(a, b):
    return lax.dot_general(a, b, TN_DIMS, preferred_element_type=f32)


def _sigmoid(x):
    return 1.0 / (1.0 + jnp.exp(-x))


def _silu(x):
    return x * _sigmoid(x)


def _dsilu(x):
    s = _sigmoid(x)
    return s * (1.0 + x * (1.0 - s))


def _logsigmoid(x):
    return jnp.minimum(x, 0.0) - jnp.log(1.0 + jnp.exp(-jnp.abs(x)))


def _iota2(shape, axis):
    return lax.broadcasted_iota(jnp.int32, shape, axis)


def _group_mean_matrix(n, group):
    shift = group.bit_length() - 1
    r = lax.shift_right_logical(_iota2((n, n), 0), shift)
    c = lax.shift_right_logical(_iota2((n, n), 1), shift)
    return jnp.where(r == c, 1.0 / group, 0.0).astype(f32)


def _group_mean(x, gm):
    return jnp.dot(x, gm, precision=HI, preferred_element_type=f32)


def _lower_tri(n):
    return jnp.where(_iota2((n, n), 0) >= _iota2((n, n), 1), 1.0, 0.0).astype(f32)


def _upper_tri(n):
    return jnp.where(_iota2((n, n), 0) <= _iota2((n, n), 1), 1.0, 0.0).astype(f32)


def _rows3(r0, r1, r2, width):
    row = _iota2((8, width), 0)
    return jnp.where(row == 0, r0, jnp.where(row == 1, r1, jnp.where(row == 2, r2, 0.0)))


def _inproj_fwd(x, g, w, tm, gather=()):
    T = x.shape[0]
    n = len(gather)
    axes = [0] * n
    steps = (T // tm) * NZT

    def body(x_ref, g_ref, w_ref, *rest):
        z_ref, h_ref = rest[n:n + 2]

        @pl.when(pl.program_id(1) == 0)
        def _():
            xv = x_ref[...]
            r = lax.rsqrt(jnp.mean(xv * xv, axis=-1, keepdims=True) + EPS)
            h_ref[...] = (xv * r * g_ref[...]).astype(bf16)

        if n:
            start, forward, finish = _gather_phases(gather, axes, rest[:n], rest[n + 2:2 * n + 2], *rest[2 * n + 2:])
            step = pl.program_id(0) * NZT + pl.program_id(1)
            pl.when(step == 0)(start)
            pl.when(step == steps // 2)(forward)

        z_ref[...] = _mm(h_ref[...], w_ref[...])

        if n:
            pl.when(step == steps - 1)(finish)

    hbm = pl.BlockSpec(memory_space=pltpu.HBM)
    return _pcall(
        body, name="inproj_fwd_gather" if n else "inproj_fwd", grid=(T // tm, NZT),
        in_specs=[pl.BlockSpec((tm, D), lambda i, j: (i, 0)),
                  pl.BlockSpec((1, D), lambda i, j: (0, 0)),
                  pl.BlockSpec((D, ZT), lambda i, j: (0, j))] + [hbm] * n,
        out_specs=[pl.BlockSpec((tm, ZT), lambda i, j: (i, j)),
                   pl.BlockSpec((tm, D), lambda i, j: (i, 0))] + [hbm] * n,
        out_shape=[jax.ShapeDtypeStruct((T, NZ), f32), jax.ShapeDtypeStruct((T, D), bf16)]
        + _gathered_shapes(gather, axes),
        scratch_shapes=_gather_semaphores(n) if n else [],
        compiler_params=_params("arbitrary" if n else "parallel", "arbitrary"),
    )(x, g, w, *gather)


def _inproj_bwd_x(dz, w, x, dx1, g, tm):
    T = x.shape[0]

    def body(dz_ref, w_ref, x_ref, dx1_ref, g_ref, dx_ref, gg_ref, acc):
        i, k = pl.program_id(0), pl.program_id(1)

        @pl.when(k == 0)
        def _():
            acc[...] = jnp.zeros_like(acc)

        @pl.when((i == 0) & (k == 0))
        def _():
            gg_ref[...] = jnp.zeros_like(gg_ref)

        acc[...] += _mm_nt(dz_ref[...], w_ref[...])

        @pl.when(k == NZT - 1)
        def _():
            xv = x_ref[...]
            r = lax.rsqrt(jnp.mean(xv * xv, axis=-1, keepdims=True) + EPS)
            dh = acc[...]
            gg_ref[...] += jnp.sum(dh * xv * r, axis=0, keepdims=True)
            u = dh * g_ref[...]
            dx_ref[...] = dx1_ref[...] + r * u - xv * (r * r * r) * jnp.mean(u * xv, axis=-1, keepdims=True)

    return _pcall(
        body, name="inproj_bwd_x", grid=(T // tm, NZT),
        in_specs=[pl.BlockSpec((tm, ZT), lambda i, k: (i, k)),
                  pl.BlockSpec((D, ZT), lambda i, k: (0, k)),
                  pl.BlockSpec((tm, D), lambda i, k: (i, 0)),
                  pl.BlockSpec((tm, D), lambda i, k: (i, 0)),
                  pl.BlockSpec((1, D), lambda i, k: (0, 0))],
        out_specs=[pl.BlockSpec((tm, D), lambda i, k: (i, 0)),
                   pl.BlockSpec((1, D), lambda i, k: (0, 0))],
        out_shape=[jax.ShapeDtypeStruct((T, D), f32), jax.ShapeDtypeStruct((1, D), f32)],
        scratch_shapes=[pltpu.VMEM((tm, D), f32)],
        compiler_params=_params("arbitrary", "arbitrary"),
    )(dz, w, x, dx1, g)


def _inproj_bwd_w(h, dz, tm, li, buf):
    T = h.shape[0]
    SH = D // N_DEV
    nt = T // tm
    extra = [] if buf is None else [buf]

    def body(h_ref, dz_ref, *rest):
        gw_ref, acc = rest[len(extra):]

        @pl.when(pl.program_id(1) == 0)
        def _():
            acc[...] = jnp.zeros_like(acc)

        acc[...] += _mm_tn(h_ref[...], dz_ref[...])

        @pl.when(pl.program_id(1) == nt - 1)
        def _():
            gw_ref[...] = acc[...].reshape(N_DEV, SH, ZT).astype(bf16)

    return _pcall(
        body, name="inproj_bwd_w", grid=(NZT, nt),
        in_specs=[pl.BlockSpec((tm, D), lambda j, i: (i, 0)),
                  pl.BlockSpec((tm, ZT), lambda j, i: (i, j))] + [pl.BlockSpec(memory_space=pl.ANY)] * len(extra),
        out_specs=pl.BlockSpec((N_DEV, SH, ZT), lambda j, i: (0, li, j)),
        out_shape=jax.ShapeDtypeStruct((N_DEV, DEPTH * SH, NZ), bf16),
        scratch_shapes=[pltpu.VMEM((D, ZT), f32)],
        input_output_aliases={2: 0} if extra else {},
        compiler_params=_params("parallel", "arbitrary"),
    )(h, dz, *extra)


def _zblock(tm, col256):
    return pl.BlockSpec((tm, W), lambda i, c=col256: (i, c))


def _conv_taps(zc, halo, cw_ref, n):
    ext = jnp.concatenate([halo, zc], axis=0)
    z1 = pltpu.roll(ext, 1, 0)[8:]
    z2 = pltpu.roll(ext, 2, 0)[8:]
    return z1, z2


def _conv_fwd(z, cw, cb, tm):
    T = z.shape[0]
    c0 = OFF_CONV // W
    hb = tm // 8

    def body(ax_ref, ab_ref, ac_ref, ag_ref, hx_ref, hc_ref, cw_ref, cb_ref, y_ref):
        i = pl.program_id(0)
        zc = ac_ref[...] * ax_ref[...]
        halo = jnp.where(i > 0, hc_ref[...] * hx_ref[...], 0.0)
        z1, z2 = _conv_taps(zc, halo, cw_ref, tm)
        y = cw_ref[2:3, :] * zc + cw_ref[1:2, :] * z1 + cw_ref[0:1, :] * z2
        ya = ab_ref[...] * (y + cb_ref[...])
        y_ref[...] = (ya * _silu(ag_ref[...])).astype(bf16)

    halo_spec = lambda col: pl.BlockSpec((8, W), lambda i, c=col: (jnp.maximum(i * hb - 1, 0), c))
    return _pcall(
        body, name="conv_fwd", grid=(T // tm,),
        in_specs=[_zblock(tm, c0), _zblock(tm, c0 + 1), _zblock(tm, c0 + 2), _zblock(tm, c0 + 3),
                  halo_spec(c0), halo_spec(c0 + 2),
                  pl.BlockSpec((CONV_WIDTH, W), lambda i: (0, 0)),
                  pl.BlockSpec((1, W), lambda i: (0, 0))],
        out_specs=pl.BlockSpec((tm, W), lambda i: (i, 0)),
        out_shape=jax.ShapeDtypeStruct((T, W), bf16),
        compiler_params=_params("parallel"),
    )(z, z, z, z, z, z, cw, cb)


def _conv_bwd(z, dy, cw, cb, dzbuf, tm):
    T = z.shape[0]
    c0 = OFF_CONV // W
    hb = tm // 8
    nt = T // tm

    def body(ax_ref, ab_ref, ac_ref, ag_ref, hx_ref, hc_ref, nb_ref, ng_ref, dy_ref, ndy_ref,
             cw_ref, cb_ref, dzin_ref, dz_ref, gcw_ref, gcb_ref):
        i = pl.program_id(0)

        @pl.when(i == 0)
        def _():
            gcw_ref[...] = jnp.zeros_like(gcw_ref)
            gcb_ref[...] = jnp.zeros_like(gcb_ref)

        ax, ab, ac, ag = ax_ref[...], ab_ref[...], ac_ref[...], ag_ref[...]
        w0, w1, w2 = cw_ref[0:1, :], cw_ref[1:2, :], cw_ref[2:3, :]
        zc = ac * ax
        halo = jnp.where(i > 0, hc_ref[...] * hx_ref[...], 0.0)
        z1, z2 = _conv_taps(zc, halo, cw_ref, tm)
        yb = w2 * zc + w1 * z1 + w0 * z2 + cb_ref[...]
        ya = ab * yb
        dyg = dy_ref[...]
        dag = dyg * ya * _dsilu(ag)
        dya = dyg * _silu(ag)
        dab = dya * yb
        dyc = dya * ab
        nxt = jnp.where(i < nt - 1, ndy_ref[...] * _silu(ng_ref[...]) * nb_ref[...], 0.0)
        ext = jnp.concatenate([dyc, nxt], axis=0)
        d1 = pltpu.roll(ext, tm + 8 - 1, 0)[:tm]
        d2 = pltpu.roll(ext, tm + 8 - 2, 0)[:tm]
        dzc = w2 * dyc + w1 * d1 + w0 * d2
        dz_ref[:, 0:W] = (dzc * ac).astype(bf16)
        dz_ref[:, W:2 * W] = dab.astype(bf16)
        dz_ref[:, 2 * W:3 * W] = (dzc * ax).astype(bf16)
        dz_ref[:, 3 * W:4 * W] = dag.astype(bf16)
        gcb_ref[...] += jnp.sum(dyc, axis=0, keepdims=True)
        gcw_ref[...] += _rows3(jnp.sum(dyc * z2, axis=0, keepdims=True),
                               jnp.sum(dyc * z1, axis=0, keepdims=True),
                               jnp.sum(dyc * zc, axis=0, keepdims=True), W)

    prev_spec = lambda col: pl.BlockSpec((8, W), lambda i, c=col: (jnp.maximum(i * hb - 1, 0), c))
    next_z = lambda col: pl.BlockSpec((8, W), lambda i, c=col: (jnp.minimum((i + 1) * hb, T // 8 - 1), c))
    next_dy = pl.BlockSpec((8, W), lambda i: (jnp.minimum((i + 1) * hb, T // 8 - 1), 0))
    return _pcall(
        body, name="conv_bwd", grid=(nt,),
        in_specs=[_zblock(tm, c0), _zblock(tm, c0 + 1), _zblock(tm, c0 + 2), _zblock(tm, c0 + 3),
                  prev_spec(c0), prev_spec(c0 + 2), next_z(c0 + 1), next_z(c0 + 3),
                  pl.BlockSpec((tm, W), lambda i: (i, 0)), next_dy,
                  pl.BlockSpec((CONV_WIDTH, W), lambda i: (0, 0)),
                  pl.BlockSpec((1, W), lambda i: (0, 0)),
                  pl.BlockSpec(memory_space=pl.ANY)],
        out_specs=[pl.BlockSpec((tm, 4 * W), lambda i: (i, OFF_CONV // (4 * W))),
                   pl.BlockSpec((8, W), lambda i: (0, 0)),
                   pl.BlockSpec((1, W), lambda i: (0, 0))],
        out_shape=[jax.ShapeDtypeStruct((T, NZ), bf16), jax.ShapeDtypeStruct((8, W), f32),
                   jax.ShapeDtypeStruct((1, W), f32)],
        input_output_aliases={12: 0},
        compiler_params=_params("arbitrary"),
    )(z, z, z, z, z, z, z, z, dy, dy, cw, cb, dzbuf)


def _sgu_core(dv_ref, gv_ref, sw_ref, sbe_ref, s_scr, tm):
    v = dv_ref[...]
    gm = _group_mean_matrix(W, DH)
    rv = lax.rsqrt(_group_mean(v * v, gm) + EPS)
    vh = v * rv
    vnb = (vh * gv_ref[...]).astype(bf16)
    causal = _iota2((SGU_CHUNK, SGU_CHUNK), 0) >= _iota2((SGU_CHUNK, SGU_CHUNK), 1)
    wgs = [jnp.where(causal, sw_ref[g], 0.0).astype(bf16) for g in range(NH)]
    for c in range(tm // SGU_CHUNK):
        rows = slice(c * SGU_CHUNK, (c + 1) * SGU_CHUNK)
        for g in range(NH):
            cols = slice(g * DH, (g + 1) * DH)
            s_scr[rows, cols] = _mm(wgs[g], vnb[rows, cols])
    sb = sbe_ref[...]
    s = s_scr[...] + jnp.concatenate([sb] * (tm // SGU_CHUNK), axis=0)
    return v, rv, vh, vnb, wgs, causal, gm, s


def _sgu_fwd(z, gv, sw, sbe, tm):
    T = z.shape[0]
    c0 = OFF_SGU // W

    def body(du_ref, dv_ref, dg_ref, gv_ref, sw_ref, sbe_ref, y_ref, s_scr):
        s = _sgu_core(dv_ref, gv_ref, sw_ref, sbe_ref, s_scr, tm)[-1]
        y_ref[...] = ((du_ref[...] * s) * _silu(dg_ref[...])).astype(bf16)

    return _pcall(
        body, name="sgu_fwd", grid=(T // tm,),
        in_specs=[_zblock(tm, c0), _zblock(tm, c0 + 1), _zblock(tm, c0 + 2),
                  pl.BlockSpec((1, W), lambda i: (0, 0)),
                  pl.BlockSpec((NH, SGU_CHUNK, SGU_CHUNK), lambda i: (0, 0, 0)),
                  pl.BlockSpec((SGU_CHUNK, W), lambda i: (0, 0))],
        out_specs=pl.BlockSpec((tm, W), lambda i: (i, 0)),
        out_shape=jax.ShapeDtypeStruct((T, W), bf16),
        scratch_shapes=[pltpu.VMEM((tm, W), f32)],
        compiler_params=_params("parallel"),
    )(z, z, z, gv, sw, sbe)


def _sgu_bwd(z, dy, gv, sw, sbe, dzbuf, tm):
    T = z.shape[0]
    c0 = OFF_SGU // W
    nt = T // tm

    def body(du_ref, dv_ref, dg_ref, dy_ref, gv_ref, sw_ref, sbe_ref, dzin_ref,
             dz_ref, gsw_ref, gsb_ref, ggv_ref, s_scr, dvn_scr, sb_acc):
        i = pl.program_id(0)

        @pl.when(i == 0)
        def _():
            gsw_ref[...] = jnp.zeros_like(gsw_ref)
            ggv_ref[...] = jnp.zeros_like(ggv_ref)
            sb_acc[...] = jnp.zeros_like(sb_acc)

        v, rv, vh, vnb, wgs, causal, gm, s = _sgu_core(dv_ref, gv_ref, sw_ref, sbe_ref, s_scr, tm)
        du, dg, dyv = du_ref[...], dg_ref[...], dy_ref[...]
        ddg = dyv * (du * s) * _dsilu(dg)
        t = dyv * _silu(dg)
        ddu = t * s
        ds = t * du
        dsb = ds.astype(bf16)
        acc = sb_acc[...]
        for c in range(tm // SGU_CHUNK):
            rows = slice(c * SGU_CHUNK, (c + 1) * SGU_CHUNK)
            acc = acc + ds[rows, :]
            for g in range(NH):
                cols = slice(g * DH, (g + 1) * DH)
                gsw_ref[g] += jnp.where(causal, _mm_nt(dsb[rows, cols], vnb[rows, cols]), 0.0)
                dvn_scr[rows, cols] = _mm_tn(wgs[g], dsb[rows, cols])
        sb_acc[...] = acc
        dvn = dvn_scr[...]
        ggv_ref[...] += jnp.sum(dvn * vh, axis=0, keepdims=True)
        u = dvn * gv_ref[...]
        ddv = rv * u - v * (rv * rv * rv) * _group_mean(u * v, gm)
        dz_ref[:, 0:W] = ddu.astype(bf16)
        dz_ref[:, W:2 * W] = ddv.astype(bf16)
        dz_ref[:, 2 * W:3 * W] = ddg.astype(bf16)

        @pl.when(i == nt - 1)
        def _():
            gsb_ref[...] = _group_mean(sb_acc[...], gm) * float(DH)

    return _pcall(
        body, name="sgu_bwd", grid=(nt,),
        in_specs=[_zblock(tm, c0), _zblock(tm, c0 + 1), _zblock(tm, c0 + 2),
                  pl.BlockSpec((tm, W), lambda i: (i, 0)),
                  pl.BlockSpec((1, W), lambda i: (0, 0)),
                  pl.BlockSpec((NH, SGU_CHUNK, SGU_CHUNK), lambda i: (0, 0, 0)),
                  pl.BlockSpec((SGU_CHUNK, W), lambda i: (0, 0)),
                  pl.BlockSpec(memory_space=pl.ANY)],
        out_specs=[pl.BlockSpec((tm, 3 * W), lambda i: (i, OFF_SGU // (3 * W))),
                   pl.BlockSpec((NH, SGU_CHUNK, SGU_CHUNK), lambda i: (0, 0, 0)),
                   pl.BlockSpec((SGU_CHUNK, W), lambda i: (0, 0)),
                   pl.BlockSpec((1, W), lambda i: (0, 0))],
        out_shape=[jax.ShapeDtypeStruct((T, NZ), bf16),
                   jax.ShapeDtypeStruct((NH, SGU_CHUNK, SGU_CHUNK), f32),
                   jax.ShapeDtypeStruct((SGU_CHUNK, W), f32),
                   jax.ShapeDtypeStruct((1, W), f32)],
        scratch_shapes=[pltpu.VMEM((tm, W), f32), pltpu.VMEM((tm, W), f32), pltpu.VMEM((SGU_CHUNK, W), f32)],
        input_output_aliases={7: 0},
        compiler_params=_params("arbitrary"),
    )(z, z, z, dy, gv, sw, sbe, dzbuf)


def _hgrn_gates(cq_ref, cf_ref, lb_ref):
    q = _silu(cq_ref[...])
    sig = _sigmoid(cf_ref[...])
    lb = lb_ref[...]
    g = lb + (1.0 - lb) * sig
    return q, sig, g, jnp.log(g), (1.0 - lb) * (1.0 - sig)


def _hgrn_chunk_terms(lgc, qc, kc):
    C = GLA_CHUNK
    b = jnp.dot(_lower_tri(C), lgc, precision=HI, preferred_element_type=f32)
    bl = jnp.sum(lgc, axis=0, keepdims=True)
    mid = jnp.sum(jnp.where(_iota2((C, W), 0) <= C // 2, lgc, 0.0), axis=0, keepdims=True)
    eb = jnp.exp(b)
    em = jnp.exp(jnp.minimum(b - mid, EXP_CLAMP))
    emi = jnp.exp(jnp.minimum(mid - b, EXP_CLAMP))
    ek = jnp.exp(bl - b)
    return dict(eb=eb, em=em, emi=emi, ek=ek, ebl=jnp.exp(bl),
                qe=qc * eb, qm=qc * em, km=kc * emi, kd=kc * ek)


def _hgrn_fwd(z, lb, gain, tm):
    T = z.shape[0]
    c0 = OFF_HGRN // W
    C = GLA_CHUNK
    ncp = tm // C

    def body(cq_ref, cf_ref, ci_ref, cg_ref, lb_ref, gn_ref, y_ref, o_ref, st_ref, state, o_scr):
        @pl.when(pl.program_id(0) == 0)
        def _():
            state[...] = jnp.zeros_like(state)

        q, sig, g, lg, kf = _hgrn_gates(cq_ref, cf_ref, lb_ref)
        v = ci_ref[...]
        causal = _iota2((C, C), 0) >= _iota2((C, C), 1)
        for c in range(ncp):
            rows = slice(c * C, (c + 1) * C)
            tr = _hgrn_chunk_terms(lg[rows], q[rows], kf[rows])
            vb = v[rows].astype(bf16)
            qmb, kmb, qeb, kdb = (tr[n].astype(bf16) for n in ("qm", "km", "qe", "kd"))
            for h in range(NH):
                cols = slice(h * DH, (h + 1) * DH)
                hr = slice(h * DH, (h + 1) * DH)
                st = state[hr, :]
                st_ref[c, hr, :] = st
                p = jnp.where(causal, _mm_nt(qmb[:, cols], kmb[:, cols]), 0.0)
                o_scr[rows, cols] = _mm(p.astype(bf16), vb[:, cols]) + _mm_nt(qeb[:, cols], st.astype(bf16))
                state[hr, :] = st * tr["ebl"][:, cols] + _mm_tn(vb[:, cols], kdb[:, cols])
        o = o_scr[...]
        o_ref[...] = o
        gm = _group_mean_matrix(W, DH)
        r = lax.rsqrt(_group_mean(o * o, gm) + EPS)
        y_ref[...] = ((o * r * gn_ref[...]) * _silu(cg_ref[...])).astype(bf16)

    return _pcall(
        body, name="hgrn_fwd", grid=(T // tm,),
        in_specs=[_zblock(tm, c0), _zblock(tm, c0 + 1), _zblock(tm, c0 + 2), _zblock(tm, c0 + 3),
                  pl.BlockSpec((1, W), lambda i: (0, 0)), pl.BlockSpec((1, W), lambda i: (0, 0))],
        out_specs=[pl.BlockSpec((tm, W), lambda i: (i, 0)),
                   pl.BlockSpec((tm, W), lambda i: (i, 0)),
                   pl.BlockSpec((ncp, W, DH), lambda i: (i, 0, 0))],
        out_shape=[jax.ShapeDtypeStruct((T, W), bf16), jax.ShapeDtypeStruct((T, W), f32),
                   jax.ShapeDtypeStruct((T // C, W, DH), f32)],
        scratch_shapes=[pltpu.VMEM((W, DH), f32), pltpu.VMEM((tm, W), f32)],
        compiler_params=_params("arbitrary"),
    )(z, z, z, z, lb, gain)


def _hgrn_bwd(z, lb, gain, o_pre, states, dy, dzbuf, tm):
    T = z.shape[0]
    c0 = OFF_HGRN // W
    C = GLA_CHUNK
    ncp = tm // C
    nt = T // tm

    def body(cq_ref, cf_ref, ci_ref, cg_ref, lb_ref, gn_ref, o_ref, st_ref, dy_ref, dzin_ref,
             dz_ref, ggn_ref, glb_ref, dstate, dq_s, dk_s, dv_s, db_s):
        @pl.when(pl.program_id(0) == 0)
        def _():
            dstate[...] = jnp.zeros_like(dstate)
            ggn_ref[...] = jnp.zeros_like(ggn_ref)
            glb_ref[...] = jnp.zeros_like(glb_ref)

        cq, cg = cq_ref[...], cg_ref[...]
        q, sig, g, lg, kf = _hgrn_gates(cq_ref, cf_ref, lb_ref)
        lb = lb_ref[...]
        v = ci_ref[...]
        o = o_ref[...]
        gm = _group_mean_matrix(W, DH)
        r = lax.rsqrt(_group_mean(o * o, gm) + EPS)
        oh = o * r
        gn = gn_ref[...]
        dyv = dy_ref[...]
        dcg = dyv * (oh * gn) * _dsilu(cg)
        don = dyv * _silu(cg)
        ggn_ref[...] += jnp.sum(don * oh, axis=0, keepdims=True)
        u = don * gn
        do = r * u - o * (r * r * r) * _group_mean(u * o, gm)

        causal = _iota2((C, C), 0) >= _iota2((C, C), 1)
        last_row = _iota2((C, DH), 0) == C - 1
        for c in reversed(range(ncp)):
            rows = slice(c * C, (c + 1) * C)
            tr = _hgrn_chunk_terms(lg[rows], q[rows], kf[rows])
            vb = v[rows].astype(bf16)
            dob = do[rows].astype(bf16)
            qmb, kmb, qeb, kdb = (tr[n].astype(bf16) for n in ("qm", "km", "qe", "kd"))
            for h in range(NH):
                cols = slice(h * DH, (h + 1) * DH)
                hr = slice(h * DH, (h + 1) * DH)
                st0 = st_ref[c, hr, :]
                dst = dstate[hr, :]
                dstb = dst.astype(bf16)
                doh = dob[:, cols]
                p = jnp.where(causal, _mm_nt(qmb[:, cols], kmb[:, cols]), 0.0)
                dp = jnp.where(causal, _mm_nt(doh, vb[:, cols]), 0.0)
                dpb = dp.astype(bf16)
                dvh = _mm_tn(p.astype(bf16), doh) + _mm_nt(kdb[:, cols], dstb)
                dqm = _mm(dpb, kmb[:, cols])
                dkm = _mm_tn(dpb, qmb[:, cols])
                dqe = _mm(doh, st0.astype(bf16))
                dkd = _mm(vb[:, cols], dstb)
                ebl = tr["ebl"][:, cols]
                dstate[hr, :] = dst * ebl + _mm_tn(doh, qeb[:, cols])
                qm, km, qe, kd = (a[:, cols].astype(f32) for a in (qmb, kmb, qeb, kdb))
                kterm = dkd * kd
                dbh = dqm * qm - dkm * km + dqe * qe - kterm
                extra = jnp.sum(kterm, axis=0, keepdims=True) + ebl * jnp.sum(dst * st0, axis=0, keepdims=True)
                dbh = dbh + jnp.where(last_row, extra, 0.0)
                dq_s[rows, cols] = dqm * tr["em"][:, cols] + dqe * tr["eb"][:, cols]
                dk_s[rows, cols] = dkm * tr["emi"][:, cols] + dkd * tr["ek"][:, cols]
                dv_s[rows, cols] = dvh
                db_s[rows, cols] = dbh
            db_s[rows, :] = jnp.dot(_upper_tri(C), db_s[rows, :], precision=HI, preferred_element_type=f32)
        dlg = db_s[...]
        dk = dk_s[...]
        dsig = sig * (1.0 - sig)
        one_lb = 1.0 - lb
        dcf = (dlg / g - dk) * one_lb * dsig
        glb_ref[...] += jnp.sum((dlg / g - dk) * (1.0 - sig), axis=0, keepdims=True)
        dz_ref[:, 0:W] = (dq_s[...] * _dsilu(cq)).astype(bf16)
        dz_ref[:, W:2 * W] = dcf.astype(bf16)
        dz_ref[:, 2 * W:3 * W] = dv_s[...].astype(bf16)
        dz_ref[:, 3 * W:4 * W] = dcg.astype(bf16)

    rev = lambda i: nt - 1 - i
    zb = lambda col: pl.BlockSpec((tm, W), lambda i, c=col: (rev(i), c))
    return _pcall(
        body, name="hgrn_bwd", grid=(nt,),
        in_specs=[zb(c0), zb(c0 + 1), zb(c0 + 2), zb(c0 + 3),
                  pl.BlockSpec((1, W), lambda i: (0, 0)), pl.BlockSpec((1, W), lambda i: (0, 0)),
                  pl.BlockSpec((tm, W), lambda i: (rev(i), 0)),
                  pl.BlockSpec((ncp, W, DH), lambda i: (rev(i), 0, 0)),
                  pl.BlockSpec((tm, W), lambda i: (rev(i), 0)),
                  pl.BlockSpec(memory_space=pl.ANY)],
        out_specs=[pl.BlockSpec((tm, 4 * W), lambda i: (rev(i), OFF_HGRN // (4 * W))),
                   pl.BlockSpec((1, W), lambda i: (0, 0)),
                   pl.BlockSpec((1, W), lambda i: (0, 0))],
        out_shape=[jax.ShapeDtypeStruct((T, NZ), bf16), jax.ShapeDtypeStruct((1, W), f32),
                   jax.ShapeDtypeStruct((1, W), f32)],
        scratch_shapes=[pltpu.VMEM((W, DH), f32)] + [pltpu.VMEM((tm, W), f32)] * 4,
        input_output_aliases={9: 0},
        compiler_params=_params("arbitrary"),
    )(z, z, z, z, lb, gain, o_pre, states, dy, dzbuf)


def _attn_prep(z, fbias, gq, gk, tm):
    T = z.shape[0]
    c0 = OFF_ATT // W

    def body(q_ref, k_ref, v_ref, f_ref, fb_ref, gq_ref, gk_ref, qt_ref, kt_ref, vt_ref, kh_ref, vh_ref, cum_ref,
             carry):
        @pl.when(pl.program_id(0) == 0)
        def _():
            carry[...] = jnp.zeros_like(carry)

        gm = _group_mean_matrix(W, DH)
        q, k, v = q_ref[...], k_ref[...], v_ref[...]
        qs = q * lax.rsqrt(_group_mean(q * q, gm) + EPS) * (gq_ref[...] * (DH ** -0.5 * LOG2E))
        kn = k * lax.rsqrt(_group_mean(k * k, gm) + EPS) * gk_ref[...]
        qt_ref[...] = qs.T.astype(bf16)
        kt_ref[...] = kn.T.astype(bf16)
        vt_ref[...] = v.T.astype(bf16)
        for h in range(NH):
            cols = slice(h * DH, (h + 1) * DH)
            kh_ref[h] = kn[:, cols].astype(bf16)
            vh_ref[h] = v[:, cols].astype(bf16)
        ls = _logsigmoid(f_ref[...] + fb_ref[...])
        cum = jnp.dot(_lower_tri(tm), ls, precision=HI, preferred_element_type=f32) + carry[...]
        cum_ref[...] = cum * LOG2E
        carry[...] += jnp.sum(ls, axis=0, keepdims=True)

    hspec = pl.BlockSpec((NH, tm, DH), lambda i: (0, i, 0))
    tspec = pl.BlockSpec((W, tm), lambda i: (0, i))
    return _pcall(
        body, name="attn_prep", grid=(T // tm,),
        in_specs=[_zblock(tm, c0), _zblock(tm, c0 + 1), _zblock(tm, c0 + 2),
                  pl.BlockSpec((tm, 128), lambda i: (i, OFF_F // 128)),
                  pl.BlockSpec((1, 128), lambda i: (0, 0)),
                  pl.BlockSpec((1, W), lambda i: (0, 0)), pl.BlockSpec((1, W), lambda i: (0, 0))],
        out_specs=[tspec, tspec, tspec, hspec, hspec, pl.BlockSpec((tm, 128), lambda i: (i, 0))],
        out_shape=[jax.ShapeDtypeStruct((W, T), bf16)] * 3 + [jax.ShapeDtypeStruct((NH, T, DH), bf16)] * 2
        + [jax.ShapeDtypeStruct((T, 128), f32)],
        scratch_shapes=[pltpu.VMEM((1, 128), f32)],
        compiler_params=_params("arbitrary"),
    )(z, z, z, z, fbias, gq, gk)


HP = 2


def _causal_pairs(nq, key_major):
    if key_major:
        pairs = [(qi, ki) for ki in range(nq) for qi in range(ki, nq)]
    else:
        pairs = [(qi, ki) for qi in range(nq) for ki in range(qi + 1)]
    return (jnp.asarray([p[0] for p in pairs], jnp.int32), jnp.asarray([p[1] for p in pairs], jnp.int32))


def _head_rows(rows, n):
    return jnp.concatenate([jnp.broadcast_to(r, (DH, n)) for r in rows], axis=0)


def _attn_fwd(qt, kh, vt, crow, ccol, bq, gather=()):
    T = qt.shape[1]
    nq = T // bq
    bk = bq
    qs, ks = _causal_pairs(nq, key_major=False)
    BW = HP * DH
    n = len(gather)
    axes = [0] * n
    last_hp, last_i = NH // HP - 1, qs.shape[0] - 1

    def body(qs_ref, ks_ref, qt_ref, k_ref, vt_ref, cr_ref, cc_ref, *rest):
        o_ref, lse_ref = rest[n:n + 2]
        m_s, l_s, acc_s = rest[2 * n + 2:2 * n + 5]
        hp, i = pl.program_id(0), pl.program_id(1)
        qi, ki = qs_ref[i], ks_ref[i]
        if n:
            start, forward, finish = _gather_phases(gather, axes, rest[:n], rest[n + 2:2 * n + 2], *rest[2 * n + 5:])
            pl.when((hp == 0) & (i == 0))(start)
            pl.when((hp == last_hp) & (i == 0))(forward)

        @pl.when(ki == 0)
        def _():
            m_s[...] = jnp.full_like(m_s, MASK_VALUE)
            l_s[...] = jnp.zeros_like(l_s)
            acc_s[...] = jnp.zeros_like(acc_s)

        def step(diagonal):
            for h in range(HP):
                rows = slice(h * DH, (h + 1) * DH)
                s = _mm(k_ref[h], qt_ref[rows, :]) - cc_ref[h]
                if diagonal:
                    s = jnp.where(_iota2((bk, bq), 0) <= _iota2((bk, bq), 1), s, MASK_VALUE)
                cr = cr_ref[h]
                m_old = m_s[h]
                m_new = jnp.maximum(m_old, jnp.max(s, axis=0, keepdims=True) + cr)
                p = jnp.exp2(s + (cr - m_new))
                alpha = jnp.exp2(m_old - m_new)
                l_s[h] = alpha * l_s[h] + jnp.sum(p, axis=0, keepdims=True)
                acc_s[rows, :] = alpha * acc_s[rows, :] + _mm(vt_ref[rows, :], p.astype(bf16))
                m_s[h] = m_new

        @pl.when(ki < qi)
        def _():
            step(False)

        @pl.when(ki == qi)
        def _():
            step(True)
            o_ref[...] = (acc_s[...] / _head_rows([l_s[h] for h in range(HP)], bq)).T
            for h in range(HP):
                lse_ref[h] = m_s[h] + jnp.log(l_s[h]) * LOG2E

        if n:
            pl.when((hp == last_hp) & (i == last_i))(finish)

    qcol = lambda hp, i, qs, ks: (hp, qs[i])
    kcol = lambda hp, i, qs, ks: (hp, ks[i])
    qrow = lambda hp, i, qs, ks: (hp, 0, qs[i])
    hbm = pl.BlockSpec(memory_space=pltpu.HBM)
    return _pcall(
        body, name="attn_fwd_gather" if n else "attn_fwd",
        grid_spec=pltpu.PrefetchScalarGridSpec(
            num_scalar_prefetch=2, grid=(NH // HP, qs.shape[0]),
            in_specs=[pl.BlockSpec((BW, bq), qcol),
                      pl.BlockSpec((HP, bk, DH), lambda hp, i, qs, ks: (hp, ks[i], 0)),
                      pl.BlockSpec((BW, bk), kcol),
                      pl.BlockSpec((HP, 1, bq), qrow),
                      pl.BlockSpec((HP, bk, 1), lambda hp, i, qs, ks: (hp, ks[i], 0))] + [hbm] * n,
            out_specs=[pl.BlockSpec((bq, BW), lambda hp, i, qs, ks: (qs[i], hp)),
                       pl.BlockSpec((HP, 1, bq), qrow)] + [hbm] * n,
            scratch_shapes=[pltpu.VMEM((HP, 1, bq), f32), pltpu.VMEM((HP, 1, bq), f32),
                            pltpu.VMEM((BW, bq), f32)] + (_gather_semaphores(n) if n else [])),
        out_shape=[jax.ShapeDtypeStruct((T, W), f32), jax.ShapeDtypeStruct((NH, 1, T), f32)]
        + _gathered_shapes(gather, axes),
        compiler_params=_params("arbitrary" if n else "parallel", "arbitrary"),
    )(qs, ks, qt, kh, vt, crow, ccol, *gather)


def _attn_bwd_prep(dy, oh, z, tm):
    T = dy.shape[0]
    cg = OFF_ATT // W + 3

    def body(dy_ref, o_ref, g_ref, dot_ref, dl_ref):
        do = (dy_ref[...] * _silu(g_ref[...])).astype(bf16)
        dot_ref[...] = do.astype(f32).T.astype(bf16)
        prod = (do.astype(f32) * o_ref[...]).T
        for h in range(NH):
            dl_ref[h] = jnp.sum(prod[h * DH:(h + 1) * DH, :], axis=0, keepdims=True)

    return _pcall(
        body, name="attn_bwd_prep", grid=(T // tm,),
        in_specs=[pl.BlockSpec((tm, W), lambda i: (i, 0)),
                  pl.BlockSpec((tm, W), lambda i: (i, 0)),
                  _zblock(tm, cg)],
        out_specs=[pl.BlockSpec((W, tm), lambda i: (0, i)),
                   pl.BlockSpec((NH, 1, tm), lambda i: (0, 0, i))],
        out_shape=[jax.ShapeDtypeStruct((W, T), bf16), jax.ShapeDtypeStruct((NH, 1, T), f32)],
        compiler_params=_params("parallel"),
    )(dy, oh, z)


def _slab_exchange_phases(buf, land, row0, rows, send_sems, recv_sems, local_sem):
    me = _my_id()

    def local():
        return pltpu.make_async_copy(buf.at[me, pl.ds(row0, rows), :], land.at[me], local_sem)

    def remote(k, receive):
        peer, pid = _peer(k)
        return pltpu.make_async_remote_copy(
            src_ref=buf.at[pid, pl.ds(row0, rows), :], dst_ref=land.at[pid] if receive else land.at[me],
            send_sem=send_sems.at[k - 1], recv_sem=recv_sems.at[k - 1],
            device_id=peer, device_id_type=pl.DeviceIdType.MESH)

    def start():
        local().start()
        for k in range(1, N_DEV):
            remote(k, False).start()

    def finish():
        for k in range(1, N_DEV):
            remote(k, True).wait_recv()
        for k in range(1, N_DEV):
            remote(k, False).wait_send()
        local().wait()

    return start, finish


def _attn_bwd(qt, kt, kh, vh, crow, ccol, dot, lse, delta, bq, exchange=None):
    T = qt.shape[1]
    nq = T // bq
    bk = bq
    qs, ks = _causal_pairs(nq, key_major=True)
    BW = HP * DH
    exchange = list(exchange or [])
    nx = len(exchange)

    def body(qs_ref, ks_ref, qt_ref, kt_ref, k_ref, v_ref, cr_ref, cc_ref, dot_ref, lse_ref, dl_ref, *rest):
        dq_ref, dk_ref, dv_ref, dck_ref, dcq_ref = rest[nx:nx + 5]
        dq_s, dk_s, dv_s, dck_s = rest[2 * nx + 5:2 * nx + 9]
        i = pl.program_id(1)
        qi, ki = qs_ref[i], ks_ref[i]
        phases = [_slab_exchange_phases(rest[e], rest[nx + 5 + e], exchange[e][1], exchange[e][2],
                                        *rest[2 * nx + 9 + 3 * e:2 * nx + 12 + 3 * e]) for e in range(nx)]
        if nx:
            @pl.when((pl.program_id(0) == 0) & (i == 0))
            def _():
                for start, _ in phases:
                    start()

        @pl.when(i == 0)
        def _():
            dq_s[...] = jnp.zeros_like(dq_s)
            dcq_ref[...] = jnp.zeros_like(dcq_ref)

        @pl.when(qi == ki)
        def _():
            dk_s[...] = jnp.zeros_like(dk_s)
            dv_s[...] = jnp.zeros_like(dv_s)
            dck_s[...] = jnp.zeros_like(dck_s)

        def step(diagonal):
            colsums = []
            for h in range(HP):
                rows = slice(h * DH, (h + 1) * DH)
                qth, doth = qt_ref[rows, :], dot_ref[rows, :]
                p = jnp.exp2(_mm(k_ref[h], qth) + (cr_ref[h] - lse_ref[h]) - cc_ref[h])
                if diagonal:
                    p = jnp.where(_iota2((bk, bq), 0) <= _iota2((bk, bq), 1), p, 0.0)
                dv_s[rows, :] += _mm_nt(doth, p.astype(bf16))
                ds = p * (_mm(v_ref[h], doth) - dl_ref[h])
                dsb = ds.astype(bf16)
                dk_s[rows, :] += _mm_nt(qth, dsb)
                dq_s[qi, rows, :] += _mm(kt_ref[rows, :], dsb)
                part = ds[:, 0:128]
                for c in range(1, bq // 128):
                    part = part + ds[:, c * 128:(c + 1) * 128]
                dck_s[h] += part
                colsums.append(jnp.sum(ds, axis=0, keepdims=True))
            dcq_ref[qi] += _stack_rows(colsums, bq)

        @pl.when(qi > ki)
        def _():
            step(False)

        @pl.when(qi == ki)
        def _():
            step(True)

        @pl.when(qi == nq - 1)
        def _():
            dk_ref[...] = (dk_s[...] * (1.0 / LOG2E)).T
            dv_ref[...] = dv_s[...].T
            lane = _iota2((bk, 128), 1)
            out = jnp.zeros((bk, 128), f32)
            for h in range(HP):
                out = out - jnp.where(lane == pl.program_id(0) * HP + h,
                                      jnp.sum(dck_s[h], axis=1, keepdims=True), 0.0)
            dck_ref[...] = out

        @pl.when(i == qs.shape[0] - 1)
        def _():
            for qb in range(nq):
                dq_ref[qb * bq:(qb + 1) * bq, :] = dq_s[qb].T

        if nx:
            @pl.when((pl.program_id(0) == NH // HP - 1) & (i == qs.shape[0] - 1))
            def _():
                for _, finish in phases:
                    finish()

    qcol = lambda hp, i, qs, ks: (hp, qs[i])
    kcol = lambda hp, i, qs, ks: (hp, ks[i])
    qrow = lambda hp, i, qs, ks: (hp, 0, qs[i])
    kh_spec = pl.BlockSpec((HP, bk, DH), lambda hp, i, qs, ks: (hp, ks[i], 0))
    hbm = pl.BlockSpec(memory_space=pltpu.HBM)
    extra_in = [e[0] for e in exchange]
    extra_out = [jax.ShapeDtypeStruct((N_DEV, e[2], e[0].shape[2]), e[0].dtype) for e in exchange]
    extra_scratch = [pltpu.SemaphoreType.DMA((N_DEV - 1,)), pltpu.SemaphoreType.DMA((N_DEV - 1,)),
                     pltpu.SemaphoreType.DMA] * nx
    return _pcall(
        body, name="attn_bwd_exchange" if nx else "attn_bwd",
        grid_spec=pltpu.PrefetchScalarGridSpec(
            num_scalar_prefetch=2, grid=(NH // HP, qs.shape[0]),
            in_specs=[pl.BlockSpec((BW, bq), qcol), pl.BlockSpec((BW, bk), kcol), kh_spec, kh_spec,
                      pl.BlockSpec((HP, 1, bq), qrow),
                      pl.BlockSpec((HP, bk, 1), lambda hp, i, qs, ks: (hp, ks[i], 0)),
                      pl.BlockSpec((BW, bq), qcol), pl.BlockSpec((HP, 1, bq), qrow), pl.BlockSpec((HP, 1, bq), qrow)]
            + [hbm] * nx,
            out_specs=[pl.BlockSpec((T, BW), lambda hp, i, qs, ks: (0, hp)),
                       pl.BlockSpec((bk, BW), lambda hp, i, qs, ks: (ks[i], hp)),
                       pl.BlockSpec((bk, BW), lambda hp, i, qs, ks: (ks[i], hp)),
                       pl.BlockSpec((None, bk, 128), lambda hp, i, qs, ks: (hp, ks[i], 0)),
                       pl.BlockSpec((None, nq, 8, bq), lambda hp, i, qs, ks: (hp, 0, 0, 0))] + [hbm] * nx,
            scratch_shapes=[pltpu.VMEM((nq, BW, bq), f32), pltpu.VMEM((BW, bk), f32), pltpu.VMEM((BW, bk), f32),
                            pltpu.VMEM((HP, bk, 128), f32)] + extra_scratch),
        out_shape=[jax.ShapeDtypeStruct((T, W), f32)] * 3 + [jax.ShapeDtypeStruct((NH // HP, T, 128), f32),
                                                             jax.ShapeDtypeStruct((NH // HP, nq, 8, bq), f32)]
        + extra_out,
        compiler_params=_params("arbitrary" if nx else "parallel", "arbitrary"),
    )(qs, ks, qt, kt, kh, vh, crow, ccol, dot, lse, delta, *extra_in)


def _attn_post(z, dy, oh, dqh, dkh, dvh, dck, dcq, fbias, gq, gk, dzbuf, tm):
    T = z.shape[0]
    c0 = OFF_ATT // W
    nt = T // tm

    def body(q_ref, k_ref, g_ref, f_ref, dy_ref, o_ref, dq_ref, dk_ref, dv_ref, dck_ref, dcq_ref, fb_ref, gq_ref,
             gk_ref, dzin_ref, dz_ref, ggq_ref, ggk_ref, gfb_ref, carry):
        @pl.when(pl.program_id(0) == 0)
        def _():
            carry[...] = jnp.zeros_like(carry)
            ggq_ref[...] = jnp.zeros_like(ggq_ref)
            ggk_ref[...] = jnp.zeros_like(ggk_ref)
            gfb_ref[...] = jnp.zeros_like(gfb_ref)

        gm = _group_mean_matrix(W, DH)
        hs = jnp.where((_iota2((W, W), 0) & (DH - 1)) == (_iota2((W, W), 1) & (DH - 1)), 1.0, 0.0).astype(f32)

        def norm_bwd(x, dn, gain):
            r = lax.rsqrt(_group_mean(x * x, gm) + EPS)
            gg = jnp.sum(dn * x * r, axis=0, keepdims=True)
            u = dn * gain
            return r * u - x * (r * r * r) * _group_mean(u * x, gm), gg

        q, k, gate = q_ref[...], k_ref[...], g_ref[...]
        dq, ggq = norm_bwd(q, dq_ref[...] * (DH ** -0.5), gq_ref[...])
        dk, ggk = norm_bwd(k, dk_ref[...], gk_ref[...])
        ggq_ref[...] += jnp.dot(jnp.broadcast_to(ggq, (8, W)), hs, precision=HI, preferred_element_type=f32)[0:1]
        ggk_ref[...] += jnp.dot(jnp.broadcast_to(ggk, (8, W)), hs, precision=HI, preferred_element_type=f32)[0:1]
        dgate = dy_ref[...] * o_ref[...] * _dsilu(gate)
        dck_v = dcq_ref[...]
        for hp in range(NH // HP):
            dck_v = dck_v + dck_ref[hp]
        rc = jnp.dot(_upper_tri(tm), dck_v, precision=HI, preferred_element_type=f32) + carry[...]
        carry[...] += jnp.sum(dck_v, axis=0, keepdims=True)
        f = f_ref[...] + fb_ref[...]
        df = jnp.where(_iota2((tm, 128), 1) < NH, rc * _sigmoid(-f), 0.0)
        gfb_ref[...] += jnp.sum(df, axis=0, keepdims=True)
        dz_ref[:, 0:W] = dq.astype(bf16)
        dz_ref[:, W:2 * W] = dk.astype(bf16)
        dz_ref[:, 2 * W:3 * W] = dv_ref[...].astype(bf16)
        dz_ref[:, 3 * W:4 * W] = dgate.astype(bf16)
        dz_ref[:, 4 * W:4 * W + 128] = df.astype(bf16)

    rev = lambda i: nt - 1 - i
    zb = lambda col: pl.BlockSpec((tm, W), lambda i, c=col: (rev(i), c))
    hspec = pl.BlockSpec((tm, W), lambda i: (rev(i), 0))
    return _pcall(
        body, name="attn_post", grid=(nt,),
        in_specs=[zb(c0), zb(c0 + 1), zb(c0 + 3),
                  pl.BlockSpec((tm, 128), lambda i: (rev(i), OFF_F // 128)),
                  pl.BlockSpec((tm, W), lambda i: (rev(i), 0)),
                  hspec, hspec, hspec, hspec,
                  pl.BlockSpec((NH // HP, tm, 128), lambda i: (0, rev(i), 0)),
                  pl.BlockSpec((tm, 128), lambda i: (rev(i), 0)),
                  pl.BlockSpec((1, 128), lambda i: (0, 0)),
                  pl.BlockSpec((1, W), lambda i: (0, 0)), pl.BlockSpec((1, W), lambda i: (0, 0)),
                  pl.BlockSpec(memory_space=pl.ANY)],
        out_specs=[pl.BlockSpec((tm, 4 * W + 128), lambda i: (rev(i), OFF_ATT // (4 * W + 128))),
                   pl.BlockSpec((1, W), lambda i: (0, 0)), pl.BlockSpec((1, W), lambda i: (0, 0)),
                   pl.BlockSpec((1, 128), lambda i: (0, 0))],
        out_shape=[jax.ShapeDtypeStruct((T, NZ), bf16), jax.ShapeDtypeStruct((1, W), f32),
                   jax.ShapeDtypeStruct((1, W), f32), jax.ShapeDtypeStruct((1, 128), f32)],
        scratch_shapes=[pltpu.VMEM((1, 128), f32)],
        input_output_aliases={14: 0},
        compiler_params=_params("arbitrary"),
    )(z, z, z, z, dy, oh, dqh, dkh, dvh, dck, dcq, fbias, gq, gk, dzbuf)


def _merge_fwd(ya, oh, z, yc, yd, mb, x, p, wup, wo, gp, wpg, wpp, tm):
    T = x.shape[0]
    cg = OFF_ATT // W + 3

    def body(ya_ref, oh_ref, bg_ref, yc_ref, yd_ref, ml_ref, mb_ref, x_ref, p_ref, wup_ref, wo_ref, gp_ref,
             wpg_ref, wpp_ref, yb_ref, mg_ref, x1_ref, x2_ref):
        yb = (oh_ref[...] * _silu(bg_ref[...])).astype(bf16)
        yb_ref[...] = yb
        ys = (ya_ref[...], yb, yc_ref[...], yd_ref[...])
        merged = jnp.zeros((tm, D), f32)
        for b in range(NBR):
            sg = _sigmoid(ml_ref[:, b * D:(b + 1) * D] + mb_ref[b:b + 1, :])
            merged = merged + sg * _mm(ys[b], wup_ref[b])
        mgb = merged.astype(bf16)
        mg_ref[...] = mgb
        x1 = x_ref[...] + _mm(mgb, wo_ref[...])
        x1_ref[...] = x1
        r = lax.rsqrt(jnp.mean(x1 * x1, axis=-1, keepdims=True) + EPS)
        hp = (x1 * r * gp_ref[...]).astype(bf16)
        gate = _sigmoid(_mm(hp, wpg_ref[...]))
        x2_ref[...] = x1 + gate * _mm(p_ref[...].astype(bf16), wpp_ref[...])

    row = lambda width: pl.BlockSpec((tm, width), lambda i: (i, 0))
    full = lambda *shape: pl.BlockSpec(shape, lambda i: (0,) * len(shape))
    return _pcall(
        body, name="merge_fwd", grid=(T // tm,),
        in_specs=[row(W), row(W), _zblock(tm, cg), row(W), row(W),
                  pl.BlockSpec((tm, NBR * D), lambda i: (i, 0)), full(NBR, D), row(D), row(PLE),
                  full(NBR, W, D), full(D, D), full(1, D), full(D, D), full(PLE, D)],
        out_specs=[row(W), row(D), row(D), row(D)],
        out_shape=[jax.ShapeDtypeStruct((T, W), bf16), jax.ShapeDtypeStruct((T, D), bf16),
                   jax.ShapeDtypeStruct((T, D), f32), jax.ShapeDtypeStruct((T, D), f32)],
        compiler_params=_params("parallel"),
    )(ya, oh, z, yc, yd, z, mb, x, p, wup, wo, gp, wpg, wpp)


def _layer_slabs(li, bufs):
    if bufs is None:
        return [], []
    return list(bufs), [pl.BlockSpec(memory_space=pl.ANY)] * len(bufs)


def _ple_bwd(dx2, x1, p, gp, wpg, wpp, tm, li, bufs):
    T = x1.shape[0]
    SH = D // N_DEV
    nt = T // tm
    extra, extra_specs = _layer_slabs(li, bufs)

    def body(dx2_ref, x1_ref, p_ref, gp_ref, wpg_ref, wpp_ref, *rest):
        dx1_ref, gwpg_ref, gwpp_ref, ggp_ref, gwpg_acc, gwpp_acc = rest[len(extra):]

        @pl.when(pl.program_id(0) == 0)
        def _():
            gwpg_acc[...] = jnp.zeros_like(gwpg_acc)
            gwpp_acc[...] = jnp.zeros_like(gwpp_acc)
            ggp_ref[...] = jnp.zeros_like(ggp_ref)

        x1, dx2 = x1_ref[...], dx2_ref[...]
        r = lax.rsqrt(jnp.mean(x1 * x1, axis=-1, keepdims=True) + EPS)
        xh = x1 * r
        gp = gp_ref[...]
        hp = (xh * gp).astype(bf16)
        gate = _sigmoid(_mm(hp, wpg_ref[...]))
        pb = p_ref[...].astype(bf16)
        pp = _mm(pb, wpp_ref[...])
        dpre = ((dx2 * pp) * gate * (1.0 - gate)).astype(bf16)
        gwpp_acc[...] += _mm_tn(pb, (dx2 * gate).astype(bf16))
        gwpg_acc[...] += _mm_tn(hp, dpre)
        dhp = _mm_nt(dpre, wpg_ref[...])
        ggp_ref[...] += jnp.sum(dhp * xh, axis=0, keepdims=True)
        u = dhp * gp
        dx1_ref[...] = dx2 + r * u - x1 * (r * r * r) * jnp.mean(u * x1, axis=-1, keepdims=True)

        @pl.when(pl.program_id(0) == nt - 1)
        def _():
            gwpg_ref[...] = gwpg_acc[...].reshape(N_DEV, SH, D).astype(bf16)
            for d in range(N_DEV):
                gwpp_ref[d] = gwpp_acc[:, d * SH:(d + 1) * SH].astype(bf16)

    row = lambda width: pl.BlockSpec((tm, width), lambda i: (i, 0))
    full = lambda *shape: pl.BlockSpec(shape, lambda i: (0,) * len(shape))
    n_in = 6
    return _pcall(
        body, name="ple_bwd", grid=(nt,),
        in_specs=[row(D), row(D), row(PLE), full(1, D), full(D, D), full(PLE, D)] + extra_specs,
        out_specs=[row(D), pl.BlockSpec((N_DEV, SH, D), lambda i: (0, li, 0)),
                   pl.BlockSpec((N_DEV, PLE, SH), lambda i: (0, li, 0)), full(1, D)],
        out_shape=[jax.ShapeDtypeStruct((T, D), f32), jax.ShapeDtypeStruct((N_DEV, DEPTH * SH, D), bf16),
                   jax.ShapeDtypeStruct((N_DEV, DEPTH * PLE, SH), bf16), jax.ShapeDtypeStruct((1, D), f32)],
        scratch_shapes=[pltpu.VMEM((D, D), f32), pltpu.VMEM((PLE, D), f32)],
        input_output_aliases={n_in + k: 1 + k for k in range(len(extra))},
        compiler_params=_params("arbitrary"),
    )(dx2, x1, p, gp, wpg, wpp, *extra)


def _merge_bwd(dx1, mg, ya, yb, yc, yd, z, mb, wup, wo, tm, li, bufs):
    T = dx1.shape[0]
    SH = D // N_DEV
    nt = T // tm
    extra, extra_specs = _layer_slabs(li, bufs)

    def body(dx1_ref, mg_ref, ya_ref, yb_ref, yc_ref, yd_ref, ml_ref, mb_ref, wup_ref, wo_ref, *rest):
        dml_ref, dya_ref, dyb_ref, dyc_ref, dyd_ref, gwo_ref, gwup_ref, gmb_ref, gwo_acc, gwup_acc = rest[len(extra):]

        @pl.when(pl.program_id(0) == 0)
        def _():
            gwo_acc[...] = jnp.zeros_like(gwo_acc)
            gwup_acc[...] = jnp.zeros_like(gwup_acc)
            gmb_ref[...] = jnp.zeros_like(gmb_ref)

        dx1b = dx1_ref[...].astype(bf16)
        gwo_acc[...] += _mm_tn(mg_ref[...], dx1b)
        dm = _mm_nt(dx1b, wo_ref[...])
        ys = (ya_ref, yb_ref, yc_ref, yd_ref)
        dys = (dya_ref, dyb_ref, dyc_ref, dyd_ref)
        for b in range(NBR):
            y = ys[b][...]
            up = _mm(y, wup_ref[b])
            sg = _sigmoid(ml_ref[:, b * D:(b + 1) * D] + mb_ref[b:b + 1, :])
            dup = (dm * sg).astype(bf16)
            dml = dm * up * sg * (1.0 - sg)
            gmb_ref[b:b + 1, :] += jnp.sum(dml, axis=0, keepdims=True)
            dml_ref[:, b * D:(b + 1) * D] = dml.astype(bf16)
            gwup_acc[b] += _mm_tn(y, dup)
            dys[b][...] = _mm_nt(dup, wup_ref[b])

        @pl.when(pl.program_id(0) == nt - 1)
        def _():
            gwo_ref[...] = gwo_acc[...].reshape(N_DEV, SH, D).astype(bf16)
            for d in range(N_DEV):
                gwup_ref[d] = gwup_acc[:, :, d * SH:(d + 1) * SH].reshape(NBR * W, SH).astype(bf16)

    row = lambda width: pl.BlockSpec((tm, width), lambda i: (i, 0))
    full = lambda *shape: pl.BlockSpec(shape, lambda i: (0,) * len(shape))
    n_in = 10
    return _pcall(
        body, name="merge_bwd", grid=(nt,),
        in_specs=[row(D), row(D), row(W), row(W), row(W), row(W), row(NBR * D), full(NBR, D),
                  full(NBR, W, D), full(D, D)] + extra_specs,
        out_specs=[row(NBR * D), row(W), row(W), row(W), row(W),
                   pl.BlockSpec((N_DEV, SH, D), lambda i: (0, li, 0)),
                   pl.BlockSpec((N_DEV, NBR * W, SH), lambda i: (0, li, 0)), full(NBR, D)],
        out_shape=[jax.ShapeDtypeStruct((T, NZ), bf16)] + [jax.ShapeDtypeStruct((T, W), f32)] * 4
        + [jax.ShapeDtypeStruct((N_DEV, DEPTH * SH, D), bf16),
           jax.ShapeDtypeStruct((N_DEV, DEPTH * NBR * W, SH), bf16),
           jax.ShapeDtypeStruct((NBR, D), f32)],
        scratch_shapes=[pltpu.VMEM((D, D), f32), pltpu.VMEM((NBR, W, D), f32)],
        input_output_aliases={n_in + k: 5 + k for k in range(len(extra))},
        compiler_params=_params("arbitrary"),
    )(dx1, mg, ya, yb, yc, yd, z, mb, wup, wo, *extra)


def _loss_head(y, target, tm):
    T = y.shape[0]

    def body(y_ref, t_ref, loss_ref, dy_ref, acc):
        i = pl.program_id(0)

        @pl.when(i == 0)
        def _():
            acc[...] = jnp.zeros_like(acc)

        e = y_ref[...] - t_ref[...]
        dy_ref[...] = e * (1.0 / D)
        acc[...] += jnp.sum(e * e, axis=0, keepdims=True)

        @pl.when(i == T // tm - 1)
        def _():
            loss_ref[...] = jnp.sum(acc[...], axis=1, keepdims=True) * (0.5 / D)

    return _pcall(
        body, name="loss_head", grid=(T // tm,),
        in_specs=[pl.BlockSpec((tm, D), lambda i: (i, 0)), pl.BlockSpec((tm, D), lambda i: (i, 0))],
        out_specs=[pl.BlockSpec((1, 1), lambda i: (0, 0)), pl.BlockSpec((tm, D), lambda i: (i, 0))],
        out_shape=[jax.ShapeDtypeStruct((1, 1), f32), jax.ShapeDtypeStruct((T, D), f32)],
        scratch_shapes=[pltpu.VMEM((1, D), f32)],
        compiler_params=_params("arbitrary"),
    )(y, target)


def _lb_softmax_rows(l_ref):
    rows = [l_ref[i:i + 1, :] for i in range(DEPTH)]
    m = rows[0]
    for r in rows[1:]:
        m = jnp.maximum(m, r)
    es = [jnp.exp(r - m) for r in rows]
    tot = es[0]
    for e in es[1:]:
        tot = tot + e
    return [e / tot for e in es]


def _lb_partial_sums(pr):
    sums = [jnp.zeros_like(pr[0])]
    for i in range(1, DEPTH):
        sums.append(sums[-1] + pr[i])
    return sums


def _stack_rows(rows, width):
    idx = _iota2((8, width), 0)
    out = jnp.zeros((8, width), f32)
    for i, r in enumerate(rows):
        out = jnp.where(idx == i, r, out)
    return out


def _lower_bounds(lb_logits):
    def body(l_ref, o_ref):
        sums = _lb_partial_sums(_lb_softmax_rows(l_ref))
        o_ref[...] = _stack_rows([jnp.clip(s, 0.0, 1.0) for s in sums], W)

    return _pcall(body, name="lower_bounds", out_shape=jax.ShapeDtypeStruct((8, W), f32))(lb_logits)


def _lower_bounds_bwd(lb_logits, dlower):
    def body(l_ref, d_ref, o_ref):
        pr = _lb_softmax_rows(l_ref)
        sums = _lb_partial_sums(pr)
        dl = [jnp.where((sums[i] > 0.0) & (sums[i] < 1.0), d_ref[i:i + 1, :], 0.0) for i in range(DEPTH)]
        dp = [jnp.zeros_like(pr[0])] * DEPTH
        run = jnp.zeros_like(pr[0])
        for j in reversed(range(1, DEPTH)):
            run = run + dl[j]
            dp[j] = run
        inner = pr[0] * dp[0]
        for j in range(1, DEPTH):
            inner = inner + pr[j] * dp[j]
        o_ref[...] = _stack_rows([pr[j] * (dp[j] - inner) for j in range(DEPTH)], W)

    return _pcall(body, name="lower_bounds_bwd", out_shape=jax.ShapeDtypeStruct((8, W), f32))(lb_logits, dlower)


def _row_tile(rows, cols, budget_bytes=1 << 20, mult=8):
    if rows % mult:
        return rows
    best = mult
    for t in range(mult, rows + 1, mult):
        if rows % t == 0 and t * cols * 4 <= budget_bytes:
            best = t
    return best


def _sum_slabs(land):
    N, R, C = land.shape
    tr = _row_tile(R, C * N, mult=16)

    def body(l_ref, o_ref):
        acc = l_ref[0].astype(f32)
        for j in range(1, N):
            acc = acc + l_ref[j].astype(f32)
        o_ref[...] = acc

    return _pcall(
        body, name="sum_slabs", grid=(R // tr,),
        in_specs=[pl.BlockSpec((N, tr, C), lambda i: (0, i, 0))],
        out_specs=pl.BlockSpec((tr, C), lambda i: (i, 0)),
        out_shape=jax.ShapeDtypeStruct((R, C), f32),
        compiler_params=_params("parallel"),
    )(land)


def _adamw_update(w_ref, g_ref, m_ref, v_ref, d_ref, nm_ref, nv_ref):
    c1 = 1.0 / (1.0 - ADAM_B1 ** ADAM_STEP)
    c2 = 1.0 / (1.0 - ADAM_B2 ** ADAM_STEP)
    gv = g_ref[...]
    nm = ADAM_B1 * m_ref[...] + (1.0 - ADAM_B1) * gv
    nv = ADAM_B2 * v_ref[...] + (1.0 - ADAM_B2) * (gv * gv)
    nm_ref[...] = nm
    nv_ref[...] = nv
    d_ref[...] = -ADAM_LR * ((nm * c1) / (jnp.sqrt(nv * c2) + ADAM_EPS) + ADAM_WD * w_ref[...])


def _adamw3(w, g, m, v):
    L, R, C = w.shape
    tr = _row_tile(R, C)

    def body(*refs):
        _adamw_update(*refs)

    spec = pl.BlockSpec((None, tr, C), lambda l, i: (l, i, 0))
    return _pcall(
        body, name="adamw3", grid=(L, R // tr),
        in_specs=[spec] * 4, out_specs=[spec] * 3,
        out_shape=[jax.ShapeDtypeStruct((L, R, C), f32)] * 3,
        compiler_params=_params("parallel", "parallel"),
    )(w, g, m, v)


def _adamw(w, g, m, v):
    if w.ndim == 3:
        return _adamw3(w, g, m, v)
    R, C = w.shape
    tr = _row_tile(R, C)
    c1 = 1.0 / (1.0 - ADAM_B1 ** ADAM_STEP)
    c2 = 1.0 / (1.0 - ADAM_B2 ** ADAM_STEP)

    def body(w_ref, g_ref, m_ref, v_ref, d_ref, nm_ref, nv_ref):
        gv = g_ref[...]
        nm = ADAM_B1 * m_ref[...] + (1.0 - ADAM_B1) * gv
        nv = ADAM_B2 * v_ref[...] + (1.0 - ADAM_B2) * (gv * gv)
        nm_ref[...] = nm
        nv_ref[...] = nv
        d_ref[...] = -ADAM_LR * ((nm * c1) / (jnp.sqrt(nv * c2) + ADAM_EPS) + ADAM_WD * w_ref[...])

    spec = pl.BlockSpec((tr, C), lambda i: (i, 0))
    return _pcall(
        body, name="adamw", grid=(R // tr,),
        in_specs=[spec] * 4, out_specs=[spec] * 3,
        out_shape=[jax.ShapeDtypeStruct((R, C), f32)] * 3,
        compiler_params=_params("parallel"),
    )(w, g, m, v)


def _my_id():
    return lax.axis_index("x") * 4 + lax.axis_index("y") * 2 + lax.axis_index("c")


def _peer(k):
    x, y, c = lax.axis_index("x"), lax.axis_index("y"), lax.axis_index("c")
    kx, ky, kc = (k >> 2) & 1, (k >> 1) & 1, k & 1
    px, py, pc = x ^ kx, y ^ ky, c ^ kc
    return (px, py, pc), px * 4 + py * 2 + pc


def _all_gather(shards, axes):
    n = len(shards)

    def body(*refs):
        start, forward, finish = _gather_phases(shards, axes, refs[:n], refs[n:2 * n], *refs[2 * n:])
        start()
        forward()
        finish()

    hbm = pl.BlockSpec(memory_space=pltpu.HBM)
    return _pcall(
        body, name="all_gather",
        in_specs=[hbm] * n, out_specs=[hbm] * n,
        out_shape=_gathered_shapes(shards, axes),
        scratch_shapes=_gather_semaphores(n),
    )(*shards)


def _gathered_shapes(shards, axes):
    def full_shape(s, ax):
        shp = list(s.shape)
        shp[ax] *= N_DEV
        return tuple(shp)

    return [jax.ShapeDtypeStruct(full_shape(s, ax), s.dtype) for s, ax in zip(shards, axes)]


def _gather_semaphores(n):
    return [pltpu.SemaphoreType.DMA((n, N_DEV - 1)), pltpu.SemaphoreType.DMA((n, N_DEV - 1)),
            pltpu.SemaphoreType.DMA((n,))]


def _gather_phases(shards, axes, srcs, outs, send_sems, recv_sems, local_sems):
    n = len(shards)
    x, y, c = lax.axis_index("x"), lax.axis_index("y"), lax.axis_index("c")
    me, sibling = (x, y, c), (x, y, 1 - c)
    chips = [(1 - x, y), (x, 1 - y), (1 - x, 1 - y)]

    def block(a, dev):
        j = dev[0] * 4 + dev[1] * 2 + dev[2]
        size = shards[a].shape[axes[a]]
        start = pl.multiple_of(j * size, size)
        if axes[a] == 0:
            return outs[a].at[pl.ds(start, size), :]
        if axes[a] == 1:
            return outs[a].at[:, pl.ds(start, size), :]
        return outs[a].at[:, pl.ds(start, size)]

    def copy(a, k, dev, to, src=None):
        return pltpu.make_async_remote_copy(
            src_ref=block(a, dev) if src is None else src, dst_ref=block(a, dev),
            send_sem=send_sems.at[a, k], recv_sem=recv_sems.at[a, k],
            device_id=to, device_id_type=pl.DeviceIdType.MESH)

    def mine():
        return [pltpu.make_async_copy(srcs[a], block(a, me), local_sems.at[a]) for a in range(n)]

    def first():
        cps = []
        for a in range(n):
            cps.append(copy(a, 0, me, sibling, src=srcs[a]))
            cps += [copy(a, 1 + j, me, (*chip, c), src=srcs[a]) for j, chip in enumerate(chips)]
        return cps

    def passed():
        return [copy(a, 4 + j, (*chip, c), sibling) for j, chip in enumerate(chips) for a in range(n)]

    def start():
        for cp in mine() + first():
            cp.start()

    def forward():
        for j, chip in enumerate(chips):
            for a in range(n):
                copy(a, 1 + j, (*chip, c), me).wait_recv()
                copy(a, 4 + j, (*chip, c), sibling).start()

    def finish():
        for a in range(n):
            copy(a, 0, sibling, me).wait_recv()
            for j, chip in enumerate(chips):
                copy(a, 4 + j, (*chip, 1 - c), me).wait_recv()
        for cp in first() + passed():
            cp.wait_send()
        for cp in mine():
            cp.wait()

    return start, forward, finish


N_CHIP = N_DEV // 2


def _exchange_sibling(sliced):
    n = len(sliced)

    def body(*refs):
        srcs, outs = refs[:n], refs[n:2 * n]
        send_sems, recv_sems = refs[2 * n:]
        x, y, c = lax.axis_index("x"), lax.axis_index("y"), lax.axis_index("c")
        copies = []
        for a in range(n):
            for q in range(N_CHIP):
                cp = pltpu.make_async_remote_copy(
                    src_ref=srcs[a].at[2 * q + (1 - c)], dst_ref=outs[a].at[q],
                    send_sem=send_sems.at[a, q], recv_sem=recv_sems.at[a, q],
                    device_id=(x, y, 1 - c), device_id_type=pl.DeviceIdType.MESH)
                cp.start()
                copies.append(cp)
        for cp in copies:
            cp.wait_recv()
        for cp in copies:
            cp.wait_send()

    hbm = pl.BlockSpec(memory_space=pltpu.HBM)
    return _pcall(
        body, name="grad_exchange_sibling",
        in_specs=[hbm] * n, out_specs=[hbm] * n,
        out_shape=[jax.ShapeDtypeStruct((N_CHIP,) + s.shape[1:], s.dtype) for s in sliced],
        scratch_shapes=[pltpu.SemaphoreType.DMA((n, N_CHIP)), pltpu.SemaphoreType.DMA((n, N_CHIP))],
    )(*sliced)


def _pair_sum(own, recv):
    _, R, C = own.shape
    tr = _row_tile(R, C, mult=16)
    side = lax.axis_index("c").astype(jnp.int32).reshape(1)

    def body(c_ref, own_ref, recv_ref, o_ref):
        o_ref[...] = (own_ref[...].astype(f32) + recv_ref[...].astype(f32)).astype(o_ref.dtype)

    return _pcall(
        body, name="pair_sum",
        grid_spec=pltpu.PrefetchScalarGridSpec(
            num_scalar_prefetch=1, grid=(N_CHIP, R // tr),
            in_specs=[pl.BlockSpec((None, tr, C), lambda q, i, c: (2 * q + c[0], i, 0)),
                      pl.BlockSpec((None, tr, C), lambda q, i, c: (q, i, 0))],
            out_specs=pl.BlockSpec((None, tr, C), lambda q, i, c: (q, i, 0))),
        out_shape=jax.ShapeDtypeStruct((N_CHIP, R, C), own.dtype),
        compiler_params=_params("parallel", "parallel"),
    )(side, own, recv)


def _exchange_chips(partial, whole):
    ns, nw = len(partial), len(whole)

    def body(*refs):
        srcs, outs = refs[:ns + nw], refs[ns + nw:2 * (ns + nw)]
        send_sems, recv_sems, wsend_sems, wrecv_sems, local_sems = refs[2 * (ns + nw):]
        x, y, c = lax.axis_index("x"), lax.axis_index("y"), lax.axis_index("c")
        me, myq = _my_id(), x * 2 + y
        chips = [(1 - x, y), (x, 1 - y), (1 - x, 1 - y)]
        locals_ = [pltpu.make_async_copy(srcs[a].at[myq], outs[a].at[myq], local_sems.at[a]) for a in range(ns)]
        locals_ += [pltpu.make_async_copy(srcs[ns + b], outs[ns + b].at[me], local_sems.at[ns + b])
                    for b in range(nw)]
        for cp in locals_:
            cp.start()
        sends, recvs = [], []
        for j, chip in enumerate(chips):
            q = chip[0] * 2 + chip[1]
            for a in range(ns):
                cp = pltpu.make_async_remote_copy(
                    src_ref=srcs[a].at[q], dst_ref=outs[a].at[myq],
                    send_sem=send_sems.at[a, j], recv_sem=recv_sems.at[a, j],
                    device_id=(*chip, c), device_id_type=pl.DeviceIdType.MESH)
                cp.start()
                sends.append(cp)
                recvs.append(pltpu.make_async_remote_copy(
                    src_ref=srcs[a].at[q], dst_ref=outs[a].at[q],
                    send_sem=send_sems.at[a, j], recv_sem=recv_sems.at[a, j],
                    device_id=(*chip, c), device_id_type=pl.DeviceIdType.MESH))
        for k in range(1, N_DEV):
            peer, pid = _peer(k)
            for b in range(nw):
                cp = pltpu.make_async_remote_copy(
                    src_ref=srcs[ns + b], dst_ref=outs[ns + b].at[me],
                    send_sem=wsend_sems.at[b, k - 1], recv_sem=wrecv_sems.at[b, k - 1],
                    device_id=peer, device_id_type=pl.DeviceIdType.MESH)
                cp.start()
                sends.append(cp)
                recvs.append(pltpu.make_async_remote_copy(
                    src_ref=srcs[ns + b], dst_ref=outs[ns + b].at[pid],
                    send_sem=wsend_sems.at[b, k - 1], recv_sem=wrecv_sems.at[b, k - 1],
                    device_id=peer, device_id_type=pl.DeviceIdType.MESH))
        for cp in recvs:
            cp.wait_recv()
        for cp in sends:
            cp.wait_send()
        for cp in locals_:
            cp.wait()

    hbm = pl.BlockSpec(memory_space=pltpu.HBM)
    shapes = [jax.ShapeDtypeStruct(s.shape, s.dtype) for s in partial]
    shapes += [jax.ShapeDtypeStruct((N_DEV,) + s.shape, s.dtype) for s in whole]
    return _pcall(
        body, name="grad_exchange_chips",
        in_specs=[hbm] * (ns + nw), out_specs=[hbm] * (ns + nw), out_shape=shapes,
        scratch_shapes=[pltpu.SemaphoreType.DMA((ns, N_CHIP - 1)), pltpu.SemaphoreType.DMA((ns, N_CHIP - 1)),
                        pltpu.SemaphoreType.DMA((nw, N_DEV - 1)), pltpu.SemaphoreType.DMA((nw, N_DEV - 1)),
                        pltpu.SemaphoreType.DMA((ns + nw,))],
    )(*partial, *whole)


def _permute_cols(w):
    pad = jnp.zeros(w.shape[:-1] + (NZ - OFF_F - NH,), w.dtype)
    return jnp.concatenate([
        w[..., 3844:7940],
        w[..., 0:1024],
        w[..., 2052:3076],
        w[..., 3076:3844],
        w[..., 1024:2048],
        w[..., 2048:2052], pad], axis=-1)


def _unpermute_cols(g):
    return jnp.concatenate([
        g[..., OFF_CONV:OFF_CONV + 1024],
        g[..., OFF_ATT:OFF_ATT + 1024],
        g[..., OFF_F:OFF_F + NH],
        g[..., OFF_HGRN:OFF_HGRN + 1024],
        g[..., OFF_SGU:OFF_SGU + 768],
        g[..., 0:4096]], axis=-1)


_SMALL = (
    ("norm_mix", (DEPTH, D)), ("conv_w", (DEPTH, CONV_WIDTH, W)), ("conv_b", (DEPTH, W)),
    ("fgate_bias", (DEPTH, NH)), ("q_norm", (DEPTH, DH)), ("k_norm", (DEPTH, DH)),
    ("lb_logits", (DEPTH, W)), ("hgrn_norm", (DEPTH, W)), ("sgu_norm", (DEPTH, W)),
    ("spatial_w", (DEPTH, NH, SGU_CHUNK, SGU_CHUNK)), ("spatial_b", (DEPTH, NH, SGU_CHUNK)),
    ("merge_b", (DEPTH, NBR, D)), ("norm_ple", (DEPTH, D)),
)


def _small_rows(shape):
    size = 1
    for s in shape:
        size *= s
    rows = -(-size // 128)
    return size, -(-rows // 8) * 8


def _pack_small(parts):
    out = []
    for name, shape in _SMALL:
        size, rows = _small_rows(shape)
        flat = parts[name].astype(f32).reshape(-1)
        flat = jnp.pad(flat, (0, rows * 128 - size))
        out.append(flat.reshape(rows, 128))
    return jnp.concatenate(out, axis=0)


def _unpack_small(buf):
    parts, r0 = {}, 0
    for name, shape in _SMALL:
        size, rows = _small_rows(shape)
        parts[name] = buf[r0:r0 + rows].reshape(-1)[:size].reshape(shape)
        r0 += rows
    return parts


def _shard_cols(a, width):
    return lax.dynamic_slice_in_dim(a, _my_id() * width, width, axis=a.ndim - 1)


def kernel(x, p, norm_mix, w_in, conv_w, conv_b, fgate_bias, q_norm, k_norm, lb_logits, hgrn_norm, sgu_norm, spatial_w, spatial_b, w_up, merge_b, w_o, norm_ple, w_ple_gate, w_ple_proj, loss_target, m_norm_mix, m_w_in, m_conv_w, m_conv_b, m_fgate_bias, m_q_norm, m_k_norm, m_lb_logits, m_hgrn_norm, m_sgu_norm, m_spatial_w, m_spatial_b, m_w_up, m_merge_b, m_w_o, m_norm_ple, m_w_ple_gate, m_w_ple_proj, v_norm_mix, v_w_in, v_conv_w, v_conv_b, v_fgate_bias, v_q_norm, v_k_norm, v_lb_logits, v_hgrn_norm, v_sgu_norm, v_spatial_w, v_spatial_b, v_w_up, v_merge_b, v_w_o, v_norm_ple, v_w_ple_gate, v_w_ple_proj):
    T = x.shape[1]
    SH = D // N_DEV
    CW = W // N_DEV
    tm = 512 if T % 512 == 0 else T
    tmm = 256 if T % 256 == 0 else T
    x0 = x.reshape(T, D)
    target = loss_target.reshape(T, D)

    small_shard = jnp.concatenate([
        merge_b.reshape(DEPTH * NBR, SH),
        jnp.pad(conv_w.reshape(DEPTH * CONV_WIDTH, CW), ((0, 16 - DEPTH * CONV_WIDTH), (0, SH - CW)))], axis=0)
    win_s = _permute_cols(w_in).astype(bf16)
    win0, wup_f, wo_f, wpg_f, wpp_f, g_small = _all_gather(
        [win_s[0],
         w_up.astype(bf16).reshape(DEPTH * NBR * W, SH),
         w_o.astype(bf16),
         w_ple_gate.astype(bf16),
         w_ple_proj.astype(bf16).reshape(DEPTH * PLE, SH),
         small_shard],
        [0, -1, 1, 1, -1, -1])
    win_f = [win0]
    win_later = [win_s[li] for li in range(1, DEPTH)]
    wup_f = wup_f.reshape(DEPTH, NBR, W, D)
    wpp_f = wpp_f.reshape(DEPTH, PLE, D)
    mb_f = g_small[0:DEPTH * NBR].reshape(DEPTH, NBR, D)
    cw_f = g_small[16:16 + DEPTH * CONV_WIDTH].reshape(DEPTH, CONV_WIDTH, N_DEV, SH)[..., 0:CW]
    cw_f = cw_f.reshape(DEPTH, CONV_WIDTH, W)

    loss_local, dx, gw, gs_full = _forward_backward(
        x0, p[:, 0], target, win_f, wup_f, wo_f, wpg_f, wpp_f, mb_f, cw_f, norm_mix, conv_b, fgate_bias, q_norm,
        k_norm, lb_logits, hgrn_norm, sgu_norm, spatial_w, spatial_b, norm_ple, win_later)
    loss = lax.psum(loss_local[0, 0], AXES)
    grad_x = dx.reshape(1, T, D)

    weights = dict(norm_mix=norm_mix, w_in=w_in, conv_w=conv_w, conv_b=conv_b, fgate_bias=fgate_bias, q_norm=q_norm,
                   k_norm=k_norm, lb_logits=lb_logits, hgrn_norm=hgrn_norm, sgu_norm=sgu_norm, spatial_w=spatial_w,
                   spatial_b=spatial_b, w_up=w_up, merge_b=merge_b, w_o=w_o, norm_ple=norm_ple,
                   w_ple_gate=w_ple_gate, w_ple_proj=w_ple_proj)
    ms = dict(norm_mix=m_norm_mix, w_in=m_w_in, conv_w=m_conv_w, conv_b=m_conv_b, fgate_bias=m_fgate_bias,
              q_norm=m_q_norm, k_norm=m_k_norm, lb_logits=m_lb_logits, hgrn_norm=m_hgrn_norm, sgu_norm=m_sgu_norm,
              spatial_w=m_spatial_w, spatial_b=m_spatial_b, w_up=m_w_up, merge_b=m_merge_b, w_o=m_w_o,
              norm_ple=m_norm_ple, w_ple_gate=m_w_ple_gate, w_ple_proj=m_w_ple_proj)
    vs = dict(norm_mix=v_norm_mix, w_in=v_w_in, conv_w=v_conv_w, conv_b=v_conv_b, fgate_bias=v_fgate_bias,
              q_norm=v_q_norm, k_norm=v_k_norm, lb_logits=v_lb_logits, hgrn_norm=v_hgrn_norm, sgu_norm=v_sgu_norm,
              spatial_w=v_spatial_w, spatial_b=v_spatial_b, w_up=v_w_up, merge_b=v_merge_b, w_o=v_w_o,
              norm_ple=v_norm_ple, w_ple_gate=v_w_ple_gate, w_ple_proj=v_w_ple_proj)
    return _exchange_and_update(loss, grad_x, gw, gs_full, weights, ms, vs)


def _forward_backward(x0, p, target, win_f, wup_f, wo_f, wpg_f, wpp_f, mb_f, cw_f, norm_mix, conv_b, fgate_bias,
                      q_norm, k_norm, lb_logits, hgrn_norm, sgu_norm, spatial_w, spatial_b, norm_ple, win_later=()):
    T = x0.shape[0]
    tm = 512 if T % 512 == 0 else T
    tmm = 256 if T % 256 == 0 else T
    tmi = 1024 if T % 1024 == 0 else tm
    lower = _lower_bounds(lb_logits)
    fb_pad = jnp.pad(fgate_bias, ((0, 0), (0, 128 - NH)))
    gq_t = jnp.tile(q_norm, (1, NH))
    gk_t = jnp.tile(k_norm, (1, NH))
    sbe = jnp.repeat(jnp.swapaxes(spatial_b, 1, 2), DH, axis=2)

    saved = []
    xc = x0
    p = p[:, None]
    for li in range(DEPTH):
        row = lambda a: a[li:li + 1]
        if li == 0 and win_later:
            z, h, *gathered = _inproj_fwd(xc, row(norm_mix), win_f[0], tmi, gather=win_later[:1])
            win_f = [win_f[0]] + gathered
        else:
            z, h = _inproj_fwd(xc, row(norm_mix), win_f[li], tmi)
        ya = _conv_fwd(z, cw_f[li], row(conv_b), tm)
        yd = _sgu_fwd(z, row(sgu_norm), spatial_w[li], sbe[li], tm)
        yc, o_pre, states = _hgrn_fwd(z, lower[li:li + 1], row(hgrn_norm), tmm)
        qt, kt, vt, kh, vh, cum = _attn_prep(z, row(fb_pad), row(gq_t), row(gk_t), tm)
        cum4 = jnp.transpose(cum[:, 0:NH])
        ccol, crow = cum4[:, :, None], cum4[:, None, :]
        if li == 0 and len(win_later) > 1:
            oh, lse, *gathered = _attn_fwd(qt, kh, vt, crow, ccol, tm, gather=win_later[1:])
            win_f = win_f + gathered
        else:
            oh, lse = _attn_fwd(qt, kh, vt, crow, ccol, tm)
        yb, mg, x1, x2 = _merge_fwd(ya, oh, z, yc, yd, mb_f[li], xc, p[li, 0], wup_f[li], wo_f[li],
                                    row(norm_ple), wpg_f[li], wpp_f[li], tmm)
        saved.append(dict(x=xc, z=z, h=h, ya=ya, yb=yb, yc=yc, yd=yd, o_pre=o_pre, states=states,
                          qt=qt, kt=kt, kh=kh, vh=vh, crow=crow, ccol=ccol, oh=oh, lse=lse, mg=mg, x1=x1))
        xc = x2

    loss_local, dx = _loss_head(xc, target, tm)

    gw = dict(w_in=None, w_up=None, w_o=None, w_ple_gate=None, w_ple_proj=None)
    gs = {n: [None] * DEPTH for n, _ in _SMALL}
    dlower = [None] * DEPTH
    landed = {n: {} for n in gw}
    for li in reversed(range(DEPTH)):
        s = saved[li]
        row = lambda a: a[li:li + 1]
        first = li == DEPTH - 1
        dx1, gw["w_ple_gate"], gw["w_ple_proj"], ggp = _ple_bwd(
            dx, s["x1"], p[li, 0], row(norm_ple), wpg_f[li], wpp_f[li], tmm, li,
            None if first else (gw["w_ple_gate"], gw["w_ple_proj"]))
        gs["norm_ple"][li] = ggp[0]
        dz, dya, dyb, dyc, dyd, gw["w_o"], gw["w_up"], gs["merge_b"][li] = _merge_bwd(
            dx1, s["mg"], s["ya"], s["yb"], s["yc"], s["yd"], s["z"], mb_f[li], wup_f[li], wo_f[li], tmm, li,
            None if first else (gw["w_o"], gw["w_up"]))
        dz, gcw, gcb = _conv_bwd(s["z"], dya, cw_f[li], row(conv_b), dz, tm)
        gs["conv_w"][li], gs["conv_b"][li] = gcw[0:CONV_WIDTH], gcb[0]
        dz, gs["spatial_w"][li], gsb, ggv = _sgu_bwd(s["z"], dyd, row(sgu_norm), spatial_w[li], sbe[li], dz, tm)
        gs["spatial_b"][li] = jnp.transpose(gsb[:, ::DH])
        gs["sgu_norm"][li] = ggv[0]
        dz, ggn, glb = _hgrn_bwd(s["z"], lower[li:li + 1], row(hgrn_norm), s["o_pre"], s["states"], dyc, dz, tmm)
        gs["hgrn_norm"][li], dlower[li] = ggn[0], glb[0]
        dot, delta = _attn_bwd_prep(dyb, s["oh"], s["z"], tm)
        SH = D // N_DEV
        sent = [] if first else [("w_in", li + 1, SH)]
        sent += [("w_up", li, NBR * W), ("w_o", li, SH), ("w_ple_gate", li, SH), ("w_ple_proj", li, PLE)]
        dqh, dkh, dvh, dck, dcq, *lands = _attn_bwd(
            s["qt"], s["kt"], s["kh"], s["vh"], s["crow"], s["ccol"], dot, s["lse"], delta, tm,
            [(gw[name], layer * rows, rows) for name, layer, rows in sent])
        for (name, layer, _), land in zip(sent, lands):
            landed[name][layer] = land
        dcq_t = jnp.transpose(dcq[:, :, 0:HP, :], (0, 2, 1, 3)).reshape(NH, T)
        dcq_t = jnp.pad(jnp.transpose(dcq_t), ((0, 0), (0, 128 - NH)))
        dz, ggq, ggk, gfb = _attn_post(s["z"], dyb, s["oh"], dqh, dkh, dvh, dck, dcq_t, row(fb_pad),
                                       row(gq_t), row(gk_t), dz, tmm)
        gs["q_norm"][li], gs["k_norm"][li], gs["fgate_bias"][li] = ggq[0, 0:DH], ggk[0, 0:DH], gfb[0, 0:NH]
        dx, gnm = _inproj_bwd_x(dz, win_f[li], s["x"], dx1, row(norm_mix), tmi)
        gs["norm_mix"][li] = gnm[0]
        gw["w_in"] = _inproj_bwd_w(s["h"], dz, tmi, li, gw["w_in"])
    dlower8 = jnp.pad(jnp.stack(dlower), ((0, 8 - DEPTH), (0, 0)))
    gs_full = {n: jnp.stack(v) for n, v in gs.items() if n != "lb_logits"}
    gs_full["lb_logits"] = _lower_bounds_bwd(lb_logits, dlower8)[0:DEPTH]
    gw["landed"] = landed
    return loss_local, dx, gw, gs_full


def _exchange_and_update(loss, grad_x, gw, gs_full, weights, ms, vs):
    SH = D // N_DEV
    CW = W // N_DEV

    small_buf = _pack_small(gs_full)
    landed = gw["landed"]
    rest = [li for li in range(DEPTH) if li not in landed["w_in"]]
    win_rest = jnp.concatenate([gw["w_in"][:, li * SH:(li + 1) * SH] for li in rest], axis=1)
    own = [win_rest]
    from_sibling = _exchange_sibling(own)
    chip_sums = [_pair_sum(o, r) for o, r in zip(own, from_sibling)]
    l_win, l_small = _exchange_chips(chip_sums, [small_buf])

    g_rest = _sum_slabs(l_win)
    g_layers = {li: g_rest[n * SH:(n + 1) * SH] for n, li in enumerate(rest)}
    g_layers.update({li: _sum_slabs(land) for li, land in landed["w_in"].items()})
    g_w_in = _unpermute_cols(jnp.stack([g_layers[li] for li in range(DEPTH)]))

    def landed_sum(name):
        return _sum_slabs(jnp.concatenate([landed[name][li] for li in range(DEPTH)], axis=1))

    g_w_up = landed_sum("w_up").reshape(DEPTH, NBR, W, SH)
    g_w_o = landed_sum("w_o").reshape(DEPTH, SH, D)
    g_w_pg = landed_sum("w_ple_gate").reshape(DEPTH, SH, D)
    g_w_pp = landed_sum("w_ple_proj").reshape(DEPTH, PLE, SH)
    g_small = _unpack_small(_sum_slabs(l_small))
    g_small_local = dict(g_small)
    g_small_local["conv_w"] = _shard_cols(g_small["conv_w"], CW)
    g_small_local["merge_b"] = _shard_cols(g_small["merge_b"], SH)

    grads = dict(w_in=g_w_in, w_up=g_w_up, w_o=g_w_o, w_ple_gate=g_w_pg, w_ple_proj=g_w_pp)
    deltas, new_m, new_v = {}, {}, {}
    for name in ("w_in", "w_up", "w_o", "w_ple_gate", "w_ple_proj"):
        shape = weights[name].shape
        as3 = (shape[0], -1, shape[-1])
        d_, m_, v_ = _adamw(weights[name].reshape(as3), grads[name].reshape(as3),
                            ms[name].reshape(as3), vs[name].reshape(as3))
        deltas[name], new_m[name], new_v[name] = d_.reshape(shape), m_.reshape(shape), v_.reshape(shape)

    def local_shapes(parts):
        return {n: (parts[n] if parts[n].shape == s else jnp.pad(
            parts[n], [(0, 0)] * (len(s) - 1) + [(0, s[-1] - parts[n].shape[-1])])) for n, s in _SMALL}

    d_, m_, v_ = _adamw(_pack_small(local_shapes(weights)), _pack_small(local_shapes(g_small_local)),
                        _pack_small(local_shapes(ms)), _pack_small(local_shapes(vs)))
    for buf, dst in ((d_, deltas), (m_, new_m), (v_, new_v)):
        parts = _unpack_small(buf)
        for n, _ in _SMALL:
            dst[n] = parts[n][..., :weights[n].shape[-1]]
    for n, _ in _SMALL:
        grads[n] = g_small_local[n]

    order = ["norm_mix", "w_in", "conv_w", "conv_b", "fgate_bias", "q_norm", "k_norm", "lb_logits", "hgrn_norm",
             "sgu_norm", "spatial_w", "spatial_b", "w_up", "merge_b", "w_o", "norm_ple", "w_ple_gate", "w_ple_proj"]
    return (loss, grad_x, *[grads[n] for n in order], *[deltas[n] for n in order],
            *[new_m[n] for n in order], *[new_v[n] for n in order])
```

```python
import functools

import jax
import jax.numpy as jnp
from jax import lax
from jax.experimental import pallas as pl
from jax.experimental.pallas import tpu as pltpu

f32 = jnp.float32
bf16 = jnp.bfloat16

D = 1024
W = 256
NH = 4
DH = 64
NBR = 4
PLE = 256
DEPTH = 4
CONV_WIDTH = 3
SGU_CHUNK = 128
GLA_CHUNK = 128
EPS = 1e-6
MASK_VALUE = -1e30
IN_COLS = 7940
NZ = 8064
OFF_CONV = 4096
OFF_HGRN = 5120
OFF_SGU = 6144
OFF_ATT = 6912
OFF_F = 7936
ZT = 1152
NZT = NZ // ZT
EXP_CLAMP = 80.0
LOG2E = 1.4426950408889634

ADAM_LR = 0.001
ADAM_B1 = 0.9
ADAM_B2 = 0.999
ADAM_EPS = 1e-08
ADAM_WD = 0.01
ADAM_STEP = 10

N_DEV = 8
AXES = ("x", "y", "c")
VMEM_LIMIT = 56 * 1024 * 1024
HI = lax.Precision.HIGHEST

NT_DIMS = (((1,), (1,)), ((), ()))
TN_DIMS = (((0,), (0,)), ((), ()))


def _pcall(body, **kw):
    return pl.pallas_call(body, **kw)


def _params(*sem):
    return pltpu.CompilerParams(dimension_semantics=sem, vmem_limit_bytes=VMEM_LIMIT)


def _mm(a, b):
    return jnp.dot(a, b, preferred_element_type=f32)


def _mm_nt(a, b):
    return lax.dot_general(a, b, NT_DIMS, preferred_element_type=f32)


def _mm_tn(a, b):
    return lax.dot_general(a, b, TN_DIMS, preferred_element_type=f32)


def _sigmoid(x):
    return 1.0 / (1.0 + jnp.exp(-x))


def _silu(x):
    return x * _sigmoid(x)


def _dsilu(x):
    s = _sigmoid(x)
    return s * (1.0 + x * (1.0 - s))


def _logsigmoid(x):
    return jnp.minimum(x, 0.0) - jnp.log(1.0 + jnp.exp(-jnp.abs(x)))


def _iota2(shape, axis):
    return lax.broadcasted_iota(jnp.int32, shape, axis)


def _group_mean_matrix(n, group):
    shift = group.bit_length() - 1
    r = lax.shift_right_logical(_iota2((n, n), 0), shift)
    c = lax.shift_right_logical(_iota2((n, n), 1), shift)
    return jnp.where(r == c, 1.0 / group, 0.0).astype(f32)


def _group_mean(x, gm):
    return jnp.dot(x, gm, precision=HI, preferred_element_type=f32)


def _lower_tri(n):
    return jnp.where(_iota2((n, n), 0) >= _iota2((n, n), 1), 1.0, 0.0).astype(f32)


def _upper_tri(n):
    return jnp.where(_iota2((n, n), 0) <= _iota2((n, n), 1), 1.0, 0.0).astype(f32)


def _rows3(r0, r1, r2, width):
    row = _iota2((8, width), 0)
    return jnp.where(row == 0, r0, jnp.where(row == 1, r1, jnp.where(row == 2, r2, 0.0)))


def _inproj_fwd(x, g, w, tm, gather=()):
    T = x.shape[0]
    n = len(gather)
    axes = [0] * n
    steps = (T // tm) * NZT

    def body(x_ref, g_ref, w_ref, *rest):
        z_ref, h_ref = rest[n:n + 2]

        @pl.when(pl.program_id(1) == 0)
        def _():
            xv = x_ref[...]
            r = lax.rsqrt(jnp.mean(xv * xv, axis=-1, keepdims=True) + EPS)
            h_ref[...] = (xv * r * g_ref[...]).astype(bf16)

        if n:
            start, forward, finish = _gather_phases(gather, axes, rest[:n], rest[n + 2:2 * n + 2], *rest[2 * n + 2:])
            step = pl.program_id(0) * NZT + pl.program_id(1)
            pl.when(step == 0)(start)
            pl.when(step == steps // 2)(forward)

        z_ref[...] = _mm(h_ref[...], w_ref[...])

        if n:
            pl.when(step == steps - 1)(finish)

    hbm = pl.BlockSpec(memory_space=pltpu.HBM)
    return _pcall(
        body, name="inproj_fwd_gather" if n else "inproj_fwd", grid=(T // tm, NZT),
        in_specs=[pl.BlockSpec((tm, D), lambda i, j: (i, 0)),
                  pl.BlockSpec((1, D), lambda i, j: (0, 0)),
                  pl.BlockSpec((D, ZT), lambda i, j: (0, j))] + [hbm] * n,
        out_specs=[pl.BlockSpec((tm, ZT), lambda i, j: (i, j)),
                   pl.BlockSpec((tm, D), lambda i, j: (i, 0))] + [hbm] * n,
        out_shape=[jax.ShapeDtypeStruct((T, NZ), f32), jax.ShapeDtypeStruct((T, D), bf16)]
        + _gathered_shapes(gather, axes),
        scratch_shapes=_gather_semaphores(n) if n else [],
        compiler_params=_params("arbitrary" if n else "parallel", "arbitrary"),
    )(x, g, w, *gather)


def _inproj_bwd_x(dz, w, x, dx1, g, tm):
    T = x.shape[0]

    def body(dz_ref, w_ref, x_ref, dx1_ref, g_ref, dx_ref, gg_ref, acc):
        i, k = pl.program_id(0), pl.program_id(1)

        @pl.when(k == 0)
        def _():
            acc[...] = jnp.zeros_like(acc)

        @pl.when((i == 0) & (k == 0))
        def _():
            gg_ref[...] = jnp.zeros_like(gg_ref)

        acc[...] += _mm_nt(dz_ref[...], w_ref[...])

        @pl.when(k == NZT - 1)
        def _():
            xv = x_ref[...]
            r = lax.rsqrt(jnp.mean(xv * xv, axis=-1, keepdims=True) + EPS)
            dh = acc[...]
            gg_ref[...] += jnp.sum(dh * xv * r, axis=0, keepdims=True)
            u = dh * g_ref[...]
            dx_ref[...] = dx1_ref[...] + r * u - xv * (r * r * r) * jnp.mean(u * xv, axis=-1, keepdims=True)

    return _pcall(
        body, name="inproj_bwd_x", grid=(T // tm, NZT),
        in_specs=[pl.BlockSpec((tm, ZT), lambda i, k: (i, k)),
                  pl.BlockSpec((D, ZT), lambda i, k: (0, k)),
                  pl.BlockSpec((tm, D), lambda i, k: (i, 0)),
                  pl.BlockSpec((tm, D), lambda i, k: (i, 0)),
                  pl.BlockSpec((1, D), lambda i, k: (0, 0))],
        out_specs=[pl.BlockSpec((tm, D), lambda i, k: (i, 0)),
                   pl.BlockSpec((1, D), lambda i, k: (0, 0))],
        out_shape=[jax.ShapeDtypeStruct((T, D), f32), jax.ShapeDtypeStruct((1, D), f32)],
        scratch_shapes=[pltpu.VMEM((tm, D), f32)],
        compiler_params=_params("arbitrary", "arbitrary"),
    )(dz, w, x, dx1, g)


def _inproj_bwd_w(h, dz, tm, li, buf):
    T = h.shape[0]
    SH = D // N_DEV
    nt = T // tm
    extra = [] if buf is None else [buf]

    def body(h_ref, dz_ref, *rest):
        gw_ref, acc = rest[len(extra):]

        @pl.when(pl.program_id(1) == 0)
        def _():
            acc[...] = jnp.zeros_like(acc)

        acc[...] += _mm_tn(h_ref[...], dz_ref[...])

        @pl.when(pl.program_id(1) == nt - 1)
        def _():
            gw_ref[...] = acc[...].reshape(N_DEV, SH, ZT).astype(bf16)

    return _pcall(
        body, name="inproj_bwd_w", grid=(NZT, nt),
        in_specs=[pl.BlockSpec((tm, D), lambda j, i: (i, 0)),
                  pl.BlockSpec((tm, ZT), lambda j, i: (i, j))] + [pl.BlockSpec(memory_space=pl.ANY)] * len(extra),
        out_specs=pl.BlockSpec((N_DEV, SH, ZT), lambda j, i: (0, li, j)),
        out_shape=jax.ShapeDtypeStruct((N_DEV, DEPTH * SH, NZ), bf16),
        scratch_shapes=[pltpu.VMEM((D, ZT), f32)],
        input_output_aliases={2: 0} if extra else {},
        compiler_params=_params("parallel", "arbitrary"),
    )(h, dz, *extra)


def _zblock(tm, col256):
    return pl.BlockSpec((tm, W), lambda i, c=col256: (i, c))


def _conv_taps(zc, halo, cw_ref, n):
    ext = jnp.concatenate([halo, zc], axis=0)
    z1 = pltpu.roll(ext, 1, 0)[8:]
    z2 = pltpu.roll(ext, 2, 0)[8:]
    return z1, z2


def _conv_fwd(z, cw, cb, tm):
    T = z.shape[0]
    c0 = OFF_CONV // W
    hb = tm // 8

    def body(ax_ref, ab_ref, ac_ref, ag_ref, hx_ref, hc_ref, cw_ref, cb_ref, y_ref):
        i = pl.program_id(0)
        zc = ac_ref[...] * ax_ref[...]
        halo = jnp.where(i > 0, hc_ref[...] * hx_ref[...], 0.0)
        z1, z2 = _conv_taps(zc, halo, cw_ref, tm)
        y = cw_ref[2:3, :] * zc + cw_ref[1:2, :] * z1 + cw_ref[0:1, :] * z2
        ya = ab_ref[...] * (y + cb_ref[...])
        y_ref[...] = (ya * _silu(ag_ref[...])).astype(bf16)

    halo_spec = lambda col: pl.BlockSpec((8, W), lambda i, c=col: (jnp.maximum(i * hb - 1, 0), c))
    return _pcall(
        body, name="conv_fwd", grid=(T // tm,),
        in_specs=[_zblock(tm, c0), _zblock(tm, c0 + 1), _zblock(tm, c0 + 2), _zblock(tm, c0 + 3),
                  halo_spec(c0), halo_spec(c0 + 2),
                  pl.BlockSpec((CONV_WIDTH, W), lambda i: (0, 0)),
                  pl.BlockSpec((1, W), lambda i: (0, 0))],
        out_specs=pl.BlockSpec((tm, W), lambda i: (i, 0)),
        out_shape=jax.ShapeDtypeStruct((T, W), bf16),
        compiler_params=_params("parallel"),
    )(z, z, z, z, z, z, cw, cb)


def _conv_bwd(z, dy, cw, cb, dzbuf, tm):
    T = z.shape[0]
    c0 = OFF_CONV // W
    hb = tm // 8
    nt = T // tm

    def body(ax_ref, ab_ref, ac_ref, ag_ref, hx_ref, hc_ref, nb_ref, ng_ref, dy_ref, ndy_ref,
             cw_ref, cb_ref, dzin_ref, dz_ref, gcw_ref, gcb_ref):
        i = pl.program_id(0)

        @pl.when(i == 0)
        def _():
            gcw_ref[...] = jnp.zeros_like(gcw_ref)
            gcb_ref[...] = jnp.zeros_like(gcb_ref)

        ax, ab, ac, ag = ax_ref[...], ab_ref[...], ac_ref[...], ag_ref[...]
        w0, w1, w2 = cw_ref[0:1, :], cw_ref[1:2, :], cw_ref[2:3, :]
        zc = ac * ax
        halo = jnp.where(i > 0, hc_ref[...] * hx_ref[...], 0.0)
        z1, z2 = _conv_taps(zc, halo, cw_ref, tm)
        yb = w2 * zc + w1 * z1 + w0 * z2 + cb_ref[...]
        ya = ab * yb
        dyg = dy_ref[...]
        dag = dyg * ya * _dsilu(ag)
        dya = dyg * _silu(ag)
        dab = dya * yb
        dyc = dya * ab
        nxt = jnp.where(i < nt - 1, ndy_ref[...] * _silu(ng_ref[...]) * nb_ref[...], 0.0)
        ext = jnp.concatenate([dyc, nxt], axis=0)
        d1 = pltpu.roll(ext, tm + 8 - 1, 0)[:tm]
        d2 = pltpu.roll(ext, tm + 8 - 2, 0)[:tm]
        dzc = w2 * dyc + w1 * d1 + w0 * d2
        dz_ref[:, 0:W] = (dzc * ac).astype(bf16)
        dz_ref[:, W:2 * W] = dab.astype(bf16)
        dz_ref[:, 2 * W:3 * W] = (dzc * ax).astype(bf16)
        dz_ref[:, 3 * W:4 * W] = dag.astype(bf16)
        gcb_ref[...] += jnp.sum(dyc, axis=0, keepdims=True)
        gcw_ref[...] += _rows3(jnp.sum(dyc * z2, axis=0, keepdims=True),
                               jnp.sum(dyc * z1, axis=0, keepdims=True),
                               jnp.sum(dyc * zc, axis=0, keepdims=True), W)

    prev_spec = lambda col: pl.BlockSpec((8, W), lambda i, c=col: (jnp.maximum(i * hb - 1, 0), c))
    next_z = lambda col: pl.BlockSpec((8, W), lambda i, c=col: (jnp.minimum((i + 1) * hb, T // 8 - 1), c))
    next_dy = pl.BlockSpec((8, W), lambda i: (jnp.minimum((i + 1) * hb, T // 8 - 1), 0))
    return _pcall(
        body, name="conv_bwd", grid=(nt,),
        in_specs=[_zblock(tm, c0), _zblock(tm, c0 + 1), _zblock(tm, c0 + 2), _zblock(tm, c0 + 3),
                  prev_spec(c0), prev_spec(c0 + 2), next_z(c0 + 1), next_z(c0 + 3),
                  pl.BlockSpec((tm, W), lambda i: (i, 0)), next_dy,
                  pl.BlockSpec((CONV_WIDTH, W), lambda i: (0, 0)),
                  pl.BlockSpec((1, W), lambda i: (0, 0)),
                  pl.BlockSpec(memory_space=pl.ANY)],
        out_specs=[pl.BlockSpec((tm, 4 * W), lambda i: (i, OFF_CONV // (4 * W))),
                   pl.BlockSpec((8, W), lambda i: (0, 0)),
                   pl.BlockSpec((1, W), lambda i: (0, 0))],
        out_shape=[jax.ShapeDtypeStruct((T, NZ), bf16), jax.ShapeDtypeStruct((8, W), f32),
                   jax.ShapeDtypeStruct((1, W), f32)],
        input_output_aliases={12: 0},
        compiler_params=_params("arbitrary"),
    )(z, z, z, z, z, z, z, z, dy, dy, cw, cb, dzbuf)


def _sgu_core(dv_ref, gv_ref, sw_ref, sbe_ref, s_scr, tm):
    v = dv_ref[...]
    gm = _group_mean_matrix(W, DH)
    rv = lax.rsqrt(_group_mean(v * v, gm) + EPS)
    vh = v * rv
    vnb = (vh * gv_ref[...]).astype(bf16)
    causal = _iota2((SGU_CHUNK, SGU_CHUNK), 0) >= _iota2((SGU_CHUNK, SGU_CHUNK), 1)
    wgs = [jnp.where(causal, sw_ref[g], 0.0).astype(bf16) for g in range(NH)]
    for c in range(tm // SGU_CHUNK):
        rows = slice(c * SGU_CHUNK, (c + 1) * SGU_CHUNK)
        for g in range(NH):
            cols = slice(g * DH, (g + 1) * DH)
            s_scr[rows, cols] = _mm(wgs[g], vnb[rows, cols])
    sb = sbe_ref[...]
    s = s_scr[...] + jnp.concatenate([sb] * (tm // SGU_CHUNK), axis=0)
    return v, rv, vh, vnb, wgs, causal, gm, s


def _sgu_fwd(z, gv, sw, sbe, tm):
    T = z.shape[0]
    c0 = OFF_SGU // W

    def body(du_ref, dv_ref, dg_ref, gv_ref, sw_ref, sbe_ref, y_ref, s_scr):
        s = _sgu_core(dv_ref, gv_ref, sw_ref, sbe_ref, s_scr, tm)[-1]
        y_ref[...] = ((du_ref[...] * s) * _silu(dg_ref[...])).astype(bf16)

    return _pcall(
        body, name="sgu_fwd", grid=(T // tm,),
        in_specs=[_zblock(tm, c0), _zblock(tm, c0 + 1), _zblock(tm, c0 + 2),
                  pl.BlockSpec((1, W), lambda i: (0, 0)),
                  pl.BlockSpec((NH, SGU_CHUNK, SGU_CHUNK), lambda i: (0, 0, 0)),
                  pl.BlockSpec((SGU_CHUNK, W), lambda i: (0, 0))],
        out_specs=pl.BlockSpec((tm, W), lambda i: (i, 0)),
        out_shape=jax.ShapeDtypeStruct((T, W), bf16),
        scratch_shapes=[pltpu.VMEM((tm, W), f32)],
        compiler_params=_params("parallel"),
    )(z, z, z, gv, sw, sbe)


def _sgu_bwd(z, dy, gv, sw, sbe, dzbuf, tm):
    T = z.shape[0]
    c0 = OFF_SGU // W
    nt = T // tm

    def body(du_ref, dv_ref, dg_ref, dy_ref, gv_ref, sw_ref, sbe_ref, dzin_ref,
             dz_ref, gsw_ref, gsb_ref, ggv_ref, s_scr, dvn_scr, sb_acc):
        i = pl.program_id(0)

        @pl.when(i == 0)
        def _():
            gsw_ref[...] = jnp.zeros_like(gsw_ref)
            ggv_ref[...] = jnp.zeros_like(ggv_ref)
            sb_acc[...] = jnp.zeros_like(sb_acc)

        v, rv, vh, vnb, wgs, causal, gm, s = _sgu_core(dv_ref, gv_ref, sw_ref, sbe_ref, s_scr, tm)
        du, dg, dyv = du_ref[...], dg_ref[...], dy_ref[...]
        ddg = dyv * (du * s) * _dsilu(dg)
        t = dyv * _silu(dg)
        ddu = t * s
        ds = t * du
        dsb = ds.astype(bf16)
        acc = sb_acc[...]
        for c in range(tm // SGU_CHUNK):
            rows = slice(c * SGU_CHUNK, (c + 1) * SGU_CHUNK)
            acc = acc + ds[rows, :]
            for g in range(NH):
                cols = slice(g * DH, (g + 1) * DH)
                gsw_ref[g] += jnp.where(causal, _mm_nt(dsb[rows, cols], vnb[rows, cols]), 0.0)
                dvn_scr[rows, cols] = _mm_tn(wgs[g], dsb[rows, cols])
        sb_acc[...] = acc
        dvn = dvn_scr[...]
        ggv_ref[...] += jnp.sum(dvn * vh, axis=0, keepdims=True)
        u = dvn * gv_ref[...]
        ddv = rv * u - v * (rv * rv * rv) * _group_mean(u * v, gm)
        dz_ref[:, 0:W] = ddu.astype(bf16)
        dz_ref[:, W:2 * W] = ddv.astype(bf16)
        dz_ref[:, 2 * W:3 * W] = ddg.astype(bf16)

        @pl.when(i == nt - 1)
        def _():
            gsb_ref[...] = _group_mean(sb_acc[...], gm) * float(DH)

    return _pcall(
        body, name="sgu_bwd", grid=(nt,),
        in_specs=[_zblock(tm, c0), _zblock(tm, c0 + 1), _zblock(tm, c0 + 2),
                  pl.BlockSpec((tm, W), lambda i: (i, 0)),
                  pl.BlockSpec((1, W), lambda i: (0, 0)),
                  pl.BlockSpec((NH, SGU_CHUNK, SGU_CHUNK), lambda i: (0, 0, 0)),
                  pl.BlockSpec((SGU_CHUNK, W), lambda i: (0, 0)),
                  pl.BlockSpec(memory_space=pl.ANY)],
        out_specs=[pl.BlockSpec((tm, 3 * W), lambda i: (i, OFF_SGU // (3 * W))),
                   pl.BlockSpec((NH, SGU_CHUNK, SGU_CHUNK), lambda i: (0, 0, 0)),
                   pl.BlockSpec((SGU_CHUNK, W), lambda i: (0, 0)),
                   pl.BlockSpec((1, W), lambda i: (0, 0))],
        out_shape=[jax.ShapeDtypeStruct((T, NZ), bf16),
                   jax.ShapeDtypeStruct((NH, SGU_CHUNK, SGU_CHUNK), f32),
                   jax.ShapeDtypeStruct((SGU_CHUNK, W), f32),
                   jax.ShapeDtypeStruct((1, W), f32)],
        scratch_shapes=[pltpu.VMEM((tm, W), f32), pltpu.VMEM((tm, W), f32), pltpu.VMEM((SGU_CHUNK, W), f32)],
        input_output_aliases={7: 0},
        compiler_params=_params("arbitrary"),
    )(z, z, z, dy, gv, sw, sbe, dzbuf)


def _hgrn_gates(cq_ref, cf_ref, lb_ref):
    q = _silu(cq_ref[...])
    sig = _sigmoid(cf_ref[...])
    lb = lb_ref[...]
    g = lb + (1.0 - lb) * sig
    return q, sig, g, jnp.log(g), (1.0 - lb) * (1.0 - sig)


def _hgrn_chunk_terms(lgc, qc, kc):
    C = GLA_CHUNK
    b = jnp.dot(_lower_tri(C), lgc, precision=HI, preferred_element_type=f32)
    bl = jnp.sum(lgc, axis=0, keepdims=True)
    mid = jnp.sum(jnp.where(_iota2((C, W), 0) <= C // 2, lgc, 0.0), axis=0, keepdims=True)
    eb = jnp.exp(b)
    em = jnp.exp(jnp.minimum(b - mid, EXP_CLAMP))
    emi = jnp.exp(jnp.minimum(mid - b, EXP_CLAMP))
    ek = jnp.exp(bl - b)
    return dict(eb=eb, em=em, emi=emi, ek=ek, ebl=jnp.exp(bl),
                qe=qc * eb, qm=qc * em, km=kc * emi, kd=kc * ek)


def _hgrn_fwd(z, lb, gain, tm):
    T = z.shape[0]
    c0 = OFF_HGRN // W
    C = GLA_CHUNK
    ncp = tm // C

    def body(cq_ref, cf_ref, ci_ref, cg_ref, lb_ref, gn_ref, y_ref, o_ref, st_ref, state, o_scr):
        @pl.when(pl.program_id(0) == 0)
        def _():
            state[...] = jnp.zeros_like(state)

        q, sig, g, lg, kf = _hgrn_gates(cq_ref, cf_ref, lb_ref)
        v = ci_ref[...]
        causal = _iota2((C, C), 0) >= _iota2((C, C), 1)
        for c in range(ncp):
            rows = slice(c * C, (c + 1) * C)
            tr = _hgrn_chunk_terms(lg[rows], q[rows], kf[rows])
            vb = v[rows].astype(bf16)
            qmb, kmb, qeb, kdb = (tr[n].astype(bf16) for n in ("qm", "km", "qe", "kd"))
            for h in range(NH):
                cols = slice(h * DH, (h + 1) * DH)
                hr = slice(h * DH, (h + 1) * DH)
                st = state[hr, :]
                st_ref[c, hr, :] = st
                p = jnp.where(causal, _mm_nt(qmb[:, cols], kmb[:, cols]), 0.0)
                o_scr[rows, cols] = _mm(p.astype(bf16), vb[:, cols]) + _mm_nt(qeb[:, cols], st.astype(bf16))
                state[hr, :] = st * tr["ebl"][:, cols] + _mm_tn(vb[:, cols], kdb[:, cols])
        o = o_scr[...]
        o_ref[...] = o
        gm = _group_mean_matrix(W, DH)
        r = lax.rsqrt(_group_mean(o * o, gm) + EPS)
        y_ref[...] = ((o * r * gn_ref[...]) * _silu(cg_ref[...])).astype(bf16)

    return _pcall(
        body, name="hgrn_fwd", grid=(T // tm,),
        in_specs=[_zblock(tm, c0), _zblock(tm, c0 + 1), _zblock(tm, c0 + 2), _zblock(tm, c0 + 3),
                  pl.BlockSpec((1, W), lambda i: (0, 0)), pl.BlockSpec((1, W), lambda i: (0, 0))],
        out_specs=[pl.BlockSpec((tm, W), lambda i: (i, 0)),
                   pl.BlockSpec((tm, W), lambda i: (i, 0)),
                   pl.BlockSpec((ncp, W, DH), lambda i: (i, 0, 0))],
        out_shape=[jax.ShapeDtypeStruct((T, W), bf16), jax.ShapeDtypeStruct((T, W), f32),
                   jax.ShapeDtypeStruct((T // C, W, DH), f32)],
        scratch_shapes=[pltpu.VMEM((W, DH), f32), pltpu.VMEM((tm, W), f32)],
        compiler_params=_params("arbitrary"),
    )(z, z, z, z, lb, gain)


def _hgrn_bwd(z, lb, gain, o_pre, states, dy, dzbuf, tm):
    T = z.shape[0]
    c0 = OFF_HGRN // W
    C = GLA_CHUNK
    ncp = tm // C
    nt = T // tm

    def body(cq_ref, cf_ref, ci_ref, cg_ref, lb_ref, gn_ref, o_ref, st_ref, dy_ref, dzin_ref,
             dz_ref, ggn_ref, glb_ref, dstate, dq_s, dk_s, dv_s, db_s):
        @pl.when(pl.program_id(0) == 0)
        def _():
            dstate[...] = jnp.zeros_like(dstate)
            ggn_ref[...] = jnp.zeros_like(ggn_ref)
            glb_ref[...] = jnp.zeros_like(glb_ref)

        cq, cg = cq_ref[...], cg_ref[...]
        q, sig, g, lg, kf = _hgrn_gates(cq_ref, cf_ref, lb_ref)
        lb = lb_ref[...]
        v = ci_ref[...]
        o = o_ref[...]
        gm = _group_mean_matrix(W, DH)
        r = lax.rsqrt(_group_mean(o * o, gm) + EPS)
        oh = o * r
        gn = gn_ref[...]
        dyv = dy_ref[...]
        dcg = dyv * (oh * gn) * _dsilu(cg)
        don = dyv * _silu(cg)
        ggn_ref[...] += jnp.sum(don * oh, axis=0, keepdims=True)
        u = don * gn
        do = r * u - o * (r * r * r) * _group_mean(u * o, gm)

        causal = _iota2((C, C), 0) >= _iota2((C, C), 1)
        last_row = _iota2((C, DH), 0) == C - 1
        for c in reversed(range(ncp)):
            rows = slice(c * C, (c + 1) * C)
            tr = _hgrn_chunk_terms(lg[rows], q[rows], kf[rows])
            vb = v[rows].astype(bf16)
            dob = do[rows].astype(bf16)
            qmb, kmb, qeb, kdb = (tr[n].astype(bf16) for n in ("qm", "km", "qe", "kd"))
            for h in range(NH):
                cols = slice(h * DH, (h + 1) * DH)
                hr = slice(h * DH, (h + 1) * DH)
                st0 = st_ref[c, hr, :]
                dst = dstate[hr, :]
                dstb = dst.astype(bf16)
                doh = dob[:, cols]
                p = jnp.where(causal, _mm_nt(qmb[:, cols], kmb[:, cols]), 0.0)
                dp = jnp.where(causal, _mm_nt(doh, vb[:, cols]), 0.0)
                dpb = dp.astype(bf16)
                dvh = _mm_tn(p.astype(bf16), doh) + _mm_nt(kdb[:, cols], dstb)
                dqm = _mm(dpb, kmb[:, cols])
                dkm = _mm_tn(dpb, qmb[:, cols])
                dqe = _mm(doh, st0.astype(bf16))
                dkd = _mm(vb[:, cols], dstb)
                ebl = tr["ebl"][:, cols]
                dstate[hr, :] = dst * ebl + _mm_tn(doh, qeb[:, cols])
                qm, km, qe, kd = (a[:, cols].astype(f32) for a in (qmb, kmb, qeb, kdb))
                kterm = dkd * kd
                dbh = dqm * qm - dkm * km + dqe * qe - kterm
                extra = jnp.sum(kterm, axis=0, keepdims=True) + ebl * jnp.sum(dst * st0, axis=0, keepdims=True)
                dbh = dbh + jnp.where(last_row, extra, 0.0)
                dq_s[rows, cols] = dqm * tr["em"][:, cols] + dqe * tr["eb"][:, cols]
                dk_s[rows, cols] = dkm * tr["emi"][:, cols] + dkd * tr["ek"][:, cols]
                dv_s[rows, cols] = dvh
                db_s[rows, cols] = dbh
            db_s[rows, :] = jnp.dot(_upper_tri(C), db_s[rows, :], precision=HI, preferred_element_type=f32)
        dlg = db_s[...]
        dk = dk_s[...]
        dsig = sig * (1.0 - sig)
        one_lb = 1.0 - lb
        dcf = (dlg / g - dk) * one_lb * dsig
        glb_ref[...] += jnp.sum((dlg / g - dk) * (1.0 - sig), axis=0, keepdims=True)
        dz_ref[:, 0:W] = (dq_s[...] * _dsilu(cq)).astype(bf16)
        dz_ref[:, W:2 * W] = dcf.astype(bf16)
        dz_ref[:, 2 * W:3 * W] = dv_s[...].astype(bf16)
        dz_ref[:, 3 * W:4 * W] = dcg.astype(bf16)

    rev = lambda i: nt - 1 - i
    zb = lambda col: pl.BlockSpec((tm, W), lambda i, c=col: (rev(i), c))
    return _pcall(
        body, name="hgrn_bwd", grid=(nt,),
        in_specs=[zb(c0), zb(c0 + 1), zb(c0 + 2), zb(c0 + 3),
                  pl.BlockSpec((1, W), lambda i: (0, 0)), pl.BlockSpec((1, W), lambda i: (0, 0)),
                  pl.BlockSpec((tm, W), lambda i: (rev(i), 0)),
                  pl.BlockSpec((ncp, W, DH), lambda i: (rev(i), 0, 0)),
                  pl.BlockSpec((tm, W), lambda i: (rev(i), 0)),
                  pl.BlockSpec(memory_space=pl.ANY)],
        out_specs=[pl.BlockSpec((tm, 4 * W), lambda i: (rev(i), OFF_HGRN // (4 * W))),
                   pl.BlockSpec((1, W), lambda i: (0, 0)),
                   pl.BlockSpec((1, W), lambda i: (0, 0))],
        out_shape=[jax.ShapeDtypeStruct((T, NZ), bf16), jax.ShapeDtypeStruct((1, W), f32),
                   jax.ShapeDtypeStruct((1, W), f32)],
        scratch_shapes=[pltpu.VMEM((W, DH), f32)] + [pltpu.VMEM((tm, W), f32)] * 4,
        input_output_aliases={9: 0},
        compiler_params=_params("arbitrary"),
    )(z, z, z, z, lb, gain, o_pre, states, dy, dzbuf)


def _attn_prep(z, fbias, gq, gk, tm):
    T = z.shape[0]
    c0 = OFF_ATT // W

    def body(q_ref, k_ref, v_ref, f_ref, fb_ref, gq_ref, gk_ref, qt_ref, kt_ref, vt_ref, kh_ref, vh_ref, cum_ref,
             carry):
        @pl.when(pl.program_id(0) == 0)
        def _():
            carry[...] = jnp.zeros_like(carry)

        gm = _group_mean_matrix(W, DH)
        q, k, v = q_ref[...], k_ref[...], v_ref[...]
        qs = q * lax.rsqrt(_group_mean(q * q, gm) + EPS) * (gq_ref[...] * (DH ** -0.5 * LOG2E))
        kn = k * lax.rsqrt(_group_mean(k * k, gm) + EPS) * gk_ref[...]
        qt_ref[...] = qs.T.astype(bf16)
        kt_ref[...] = kn.T.astype(bf16)
        vt_ref[...] = v.T.astype(bf16)
        for h in range(NH):
            cols = slice(h * DH, (h + 1) * DH)
            kh_ref[h] = kn[:, cols].astype(bf16)
            vh_ref[h] = v[:, cols].astype(bf16)
        ls = _logsigmoid(f_ref[...] + fb_ref[...])
        cum = jnp.dot(_lower_tri(tm), ls, precision=HI, preferred_element_type=f32) + carry[...]
        cum_ref[...] = cum * LOG2E
        carry[...] += jnp.sum(ls, axis=0, keepdims=True)

    hspec = pl.BlockSpec((NH, tm, DH), lambda i: (0, i, 0))
    tspec = pl.BlockSpec((W, tm), lambda i: (0, i))
    return _pcall(
        body, name="attn_prep", grid=(T // tm,),
        in_specs=[_zblock(tm, c0), _zblock(tm, c0 + 1), _zblock(tm, c0 + 2),
                  pl.BlockSpec((tm, 128), lambda i: (i, OFF_F // 128)),
                  pl.BlockSpec((1, 128), lambda i: (0, 0)),
                  pl.BlockSpec((1, W), lambda i: (0, 0)), pl.BlockSpec((1, W), lambda i: (0, 0))],
        out_specs=[tspec, tspec, tspec, hspec, hspec, pl.BlockSpec((tm, 128), lambda i: (i, 0))],
        out_shape=[jax.ShapeDtypeStruct((W, T), bf16)] * 3 + [jax.ShapeDtypeStruct((NH, T, DH), bf16)] * 2
        + [jax.ShapeDtypeStruct((T, 128), f32)],
        scratch_shapes=[pltpu.VMEM((1, 128), f32)],
        compiler_params=_params("arbitrary"),
    )(z, z, z, z, fbias, gq, gk)


HP = 2


def _causal_pairs(nq, key_major):
    if key_major:
        pairs = [(qi, ki) for ki in range(nq) for qi in range(ki, nq)]
    else:
        pairs = [(qi, ki) for qi in range(nq) for ki in range(qi + 1)]
    return (jnp.asarray([p[0] for p in pairs], jnp.int32), jnp.asarray([p[1] for p in pairs], jnp.int32))


def _head_rows(rows, n):
    return jnp.concatenate([jnp.broadcast_to(r, (DH, n)) for r in rows], axis=0)


def _attn_fwd(qt, kh, vt, crow, ccol, bq, gather=(), gather_axes=None):
    T = qt.shape[1]
    nq = T // bq
    bk = bq
    qs, ks = _causal_pairs(nq, key_major=False)
    BW = HP * DH
    n = len(gather)
    axes = list(gather_axes) if gather_axes is not None else [0] * n
    last_hp, last_i = NH // HP - 1, qs.shape[0] - 1

    def body(qs_ref, ks_ref, qt_ref, k_ref, vt_ref, cr_ref, cc_ref, *rest):
        o_ref, lse_ref = rest[n:n + 2]
        m_s, l_s, acc_s = rest[2 * n + 2:2 * n + 5]
        hp, i = pl.program_id(0), pl.program_id(1)
        qi, ki = qs_ref[i], ks_ref[i]
        if n:
            start, forward, finish = _gather_phases(gather, axes, rest[:n], rest[n + 2:2 * n + 2], *rest[2 * n + 5:])
            pl.when((hp == 0) & (i == 0))(start)
            pl.when((hp == last_hp) & (i == 0))(forward)

        @pl.when(ki == 0)
        def _():
            m_s[...] = jnp.full_like(m_s, MASK_VALUE)
            l_s[...] = jnp.zeros_like(l_s)
            acc_s[...] = jnp.zeros_like(acc_s)

        def step(diagonal):
            for h in range(HP):
                rows = slice(h * DH, (h + 1) * DH)
                s = _mm(k_ref[h], qt_ref[rows, :]) - cc_ref[h]
                if diagonal:
                    s = jnp.where(_iota2((bk, bq), 0) <= _iota2((bk, bq), 1), s, MASK_VALUE)
                cr = cr_ref[h]
                m_old = m_s[h]
                m_new = jnp.maximum(m_old, jnp.max(s, axis=0, keepdims=True) + cr)
                p = jnp.exp2(s + (cr - m_new))
                alpha = jnp.exp2(m_old - m_new)
                l_s[h] = alpha * l_s[h] + jnp.sum(p, axis=0, keepdims=True)
                acc_s[rows, :] = alpha * acc_s[rows, :] + _mm(vt_ref[rows, :], p.astype(bf16))
                m_s[h] = m_new

        @pl.when(ki < qi)
        def _():
            step(False)

        @pl.when(ki == qi)
        def _():
            step(True)
            o_ref[...] = (acc_s[...] / _head_rows([l_s[h] for h in range(HP)], bq)).T
            for h in range(HP):
                lse_ref[h] = m_s[h] + jnp.log(l_s[h]) * LOG2E

        if n:
            pl.when((hp == last_hp) & (i == last_i))(finish)

    qcol = lambda hp, i, qs, ks: (hp, qs[i])
    kcol = lambda hp, i, qs, ks: (hp, ks[i])
    qrow = lambda hp, i, qs, ks: (hp, 0, qs[i])
    hbm = pl.BlockSpec(memory_space=pltpu.HBM)
    return _pcall(
        body, name="attn_fwd_gather" if n else "attn_fwd",
        grid_spec=pltpu.PrefetchScalarGridSpec(
            num_scalar_prefetch=2, grid=(NH // HP, qs.shape[0]),
            in_specs=[pl.BlockSpec((BW, bq), qcol),
                      pl.BlockSpec((HP, bk, DH), lambda hp, i, qs, ks: (hp, ks[i], 0)),
                      pl.BlockSpec((BW, bk), kcol),
                      pl.BlockSpec((HP, 1, bq), qrow),
                      pl.BlockSpec((HP, bk, 1), lambda hp, i, qs, ks: (hp, ks[i], 0))] + [hbm] * n,
            out_specs=[pl.BlockSpec((bq, BW), lambda hp, i, qs, ks: (qs[i], hp)),
                       pl.BlockSpec((HP, 1, bq), qrow)] + [hbm] * n,
            scratch_shapes=[pltpu.VMEM((HP, 1, bq), f32), pltpu.VMEM((HP, 1, bq), f32),
                            pltpu.VMEM((BW, bq), f32)] + (_gather_semaphores(n) if n else [])),
        out_shape=[jax.ShapeDtypeStruct((T, W), f32), jax.ShapeDtypeStruct((NH, 1, T), f32)]
        + _gathered_shapes(gather, axes),
        compiler_params=_params("arbitrary" if n else "parallel", "arbitrary"),
    )(qs, ks, qt, kh, vt, crow, ccol, *gather)


def _attn_bwd_prep(dy, oh, z, tm):
    T = dy.shape[0]
    cg = OFF_ATT // W + 3

    def body(dy_ref, o_ref, g_ref, dot_ref, dl_ref):
        do = (dy_ref[...] * _silu(g_ref[...])).astype(bf16)
        dot_ref[...] = do.astype(f32).T.astype(bf16)
        prod = (do.astype(f32) * o_ref[...]).T
        for h in range(NH):
            dl_ref[h] = jnp.sum(prod[h * DH:(h + 1) * DH, :], axis=0, keepdims=True)

    return _pcall(
        body, name="attn_bwd_prep", grid=(T // tm,),
        in_specs=[pl.BlockSpec((tm, W), lambda i: (i, 0)),
                  pl.BlockSpec((tm, W), lambda i: (i, 0)),
                  _zblock(tm, cg)],
        out_specs=[pl.BlockSpec((W, tm), lambda i: (0, i)),
                   pl.BlockSpec((NH, 1, tm), lambda i: (0, 0, i))],
        out_shape=[jax.ShapeDtypeStruct((W, T), bf16), jax.ShapeDtypeStruct((NH, 1, T), f32)],
        compiler_params=_params("parallel"),
    )(dy, oh, z)


def _slab_exchange_phases(buf, land, row0, rows, send_sems, recv_sems, local_sem):
    me = _my_id()

    def local():
        return pltpu.make_async_copy(buf.at[me, pl.ds(row0, rows), :], land.at[me], local_sem)

    def remote(k, receive):
        peer, pid = _peer(k)
        return pltpu.make_async_remote_copy(
            src_ref=buf.at[pid, pl.ds(row0, rows), :], dst_ref=land.at[pid] if receive else land.at[me],
            send_sem=send_sems.at[k - 1], recv_sem=recv_sems.at[k - 1],
            device_id=peer, device_id_type=pl.DeviceIdType.MESH)

    def start():
        local().start()
        for k in range(1, N_DEV):
            remote(k, False).start()

    def finish():
        for k in range(1, N_DEV):
            remote(k, True).wait_recv()
        for k in range(1, N_DEV):
            remote(k, False).wait_send()
        local().wait()

    return start, finish


def _attn_bwd(qt, kt, kh, vh, crow, ccol, dot, lse, delta, bq, exchange=None):
    T = qt.shape[1]
    nq = T // bq
    bk = bq
    qs, ks = _causal_pairs(nq, key_major=True)
    BW = HP * DH
    exchange = list(exchange or [])
    nx = len(exchange)

    def body(qs_ref, ks_ref, qt_ref, kt_ref, k_ref, v_ref, cr_ref, cc_ref, dot_ref, lse_ref, dl_ref, *rest):
        dq_ref, dk_ref, dv_ref, dck_ref, dcq_ref = rest[nx:nx + 5]
        dq_s, dk_s, dv_s, dck_s = rest[2 * nx + 5:2 * nx + 9]
        i = pl.program_id(1)
        qi, ki = qs_ref[i], ks_ref[i]
        phases = [_slab_exchange_phases(rest[e], rest[nx + 5 + e], exchange[e][1], exchange[e][2],
                                        *rest[2 * nx + 9 + 3 * e:2 * nx + 12 + 3 * e]) for e in range(nx)]
        if nx:
            @pl.when((pl.program_id(0) == 0) & (i == 0))
            def _():
                for start, _ in phases:
                    start()

        @pl.when(i == 0)
        def _():
            dq_s[...] = jnp.zeros_like(dq_s)
            dcq_ref[...] = jnp.zeros_like(dcq_ref)

        @pl.when(qi == ki)
        def _():
            dk_s[...] = jnp.zeros_like(dk_s)
            dv_s[...] = jnp.zeros_like(dv_s)
            dck_s[...] = jnp.zeros_like(dck_s)

        def step(diagonal):
            colsums = []
            for h in range(HP):
                rows = slice(h * DH, (h + 1) * DH)
                qth, doth = qt_ref[rows, :], dot_ref[rows, :]
                p = jnp.exp2(_mm(k_ref[h], qth) + (cr_ref[h] - lse_ref[h]) - cc_ref[h])
                if diagonal:
                    p = jnp.where(_iota2((bk, bq), 0) <= _iota2((bk, bq), 1), p, 0.0)
                dv_s[rows, :] += _mm_nt(doth, p.astype(bf16))
                ds = p * (_mm(v_ref[h], doth) - dl_ref[h])
                dsb = ds.astype(bf16)
                dk_s[rows, :] += _mm_nt(qth, dsb)
                dq_s[qi, rows, :] += _mm(kt_ref[rows, :], dsb)
                part = ds[:, 0:128]
                for c in range(1, bq // 128):
                    part = part + ds[:, c * 128:(c + 1) * 128]
                dck_s[h] += part
                colsums.append(jnp.sum(ds, axis=0, keepdims=True))
            dcq_ref[qi] += _stack_rows(colsums, bq)

        @pl.when(qi > ki)
        def _():
            step(False)

        @pl.when(qi == ki)
        def _():
            step(True)

        @pl.when(qi == nq - 1)
        def _():
            dk_ref[...] = (dk_s[...] * (1.0 / LOG2E)).T
            dv_ref[...] = dv_s[...].T
            lane = _iota2((bk, 128), 1)
            out = jnp.zeros((bk, 128), f32)
            for h in range(HP):
                out = out - jnp.where(lane == pl.program_id(0) * HP + h,
                                      jnp.sum(dck_s[h], axis=1, keepdims=True), 0.0)
            dck_ref[...] = out

        @pl.when(i == qs.shape[0] - 1)
        def _():
            for qb in range(nq):
                dq_ref[qb * bq:(qb + 1) * bq, :] = dq_s[qb].T

        if nx:
            @pl.when((pl.program_id(0) == NH // HP - 1) & (i == qs.shape[0] - 1))
            def _():
                for _, finish in phases:
                    finish()

    qcol = lambda hp, i, qs, ks: (hp, qs[i])
    kcol = lambda hp, i, qs, ks: (hp, ks[i])
    qrow = lambda hp, i, qs, ks: (hp, 0, qs[i])
    kh_spec = pl.BlockSpec((HP, bk, DH), lambda hp, i, qs, ks: (hp, ks[i], 0))
    hbm = pl.BlockSpec(memory_space=pltpu.HBM)
    extra_in = [e[0] for e in exchange]
    extra_out = [jax.ShapeDtypeStruct((N_DEV, e[2], e[0].shape[2]), e[0].dtype) for e in exchange]
    extra_scratch = [pltpu.SemaphoreType.DMA((N_DEV - 1,)), pltpu.SemaphoreType.DMA((N_DEV - 1,)),
                     pltpu.SemaphoreType.DMA] * nx
    return _pcall(
        body, name="attn_bwd_exchange" if nx else "attn_bwd",
        grid_spec=pltpu.PrefetchScalarGridSpec(
            num_scalar_prefetch=2, grid=(NH // HP, qs.shape[0]),
            in_specs=[pl.BlockSpec((BW, bq), qcol), pl.BlockSpec((BW, bk), kcol), kh_spec, kh_spec,
                      pl.BlockSpec((HP, 1, bq), qrow),
                      pl.BlockSpec((HP, bk, 1), lambda hp, i, qs, ks: (hp, ks[i], 0)),
                      pl.BlockSpec((BW, bq), qcol), pl.BlockSpec((HP, 1, bq), qrow), pl.BlockSpec((HP, 1, bq), qrow)]
            + [hbm] * nx,
            out_specs=[pl.BlockSpec((T, BW), lambda hp, i, qs, ks: (0, hp)),
                       pl.BlockSpec((bk, BW), lambda hp, i, qs, ks: (ks[i], hp)),
                       pl.BlockSpec((bk, BW), lambda hp, i, qs, ks: (ks[i], hp)),
                       pl.BlockSpec((None, bk, 128), lambda hp, i, qs, ks: (hp, ks[i], 0)),
                       pl.BlockSpec((None, nq, 8, bq), lambda hp, i, qs, ks: (hp, 0, 0, 0))] + [hbm] * nx,
            scratch_shapes=[pltpu.VMEM((nq, BW, bq), f32), pltpu.VMEM((BW, bk), f32), pltpu.VMEM((BW, bk), f32),
                            pltpu.VMEM((HP, bk, 128), f32)] + extra_scratch),
        out_shape=[jax.ShapeDtypeStruct((T, W), f32)] * 3 + [jax.ShapeDtypeStruct((NH // HP, T, 128), f32),
                                                             jax.ShapeDtypeStruct((NH // HP, nq, 8, bq), f32)]
        + extra_out,
        compiler_params=_params("arbitrary" if nx else "parallel", "arbitrary"),
    )(qs, ks, qt, kt, kh, vh, crow, ccol, dot, lse, delta, *extra_in)


def _attn_post(z, dy, oh, dqh, dkh, dvh, dck, dcq, fbias, gq, gk, dzbuf, tm):
    T = z.shape[0]
    c0 = OFF_ATT // W
    nt = T // tm

    def body(q_ref, k_ref, g_ref, f_ref, dy_ref, o_ref, dq_ref, dk_ref, dv_ref, dck_ref, dcq_ref, fb_ref, gq_ref,
             gk_ref, dzin_ref, dz_ref, ggq_ref, ggk_ref, gfb_ref, carry):
        @pl.when(pl.program_id(0) == 0)
        def _():
            carry[...] = jnp.zeros_like(carry)
            ggq_ref[...] = jnp.zeros_like(ggq_ref)
            ggk_ref[...] = jnp.zeros_like(ggk_ref)
            gfb_ref[...] = jnp.zeros_like(gfb_ref)

        gm = _group_mean_matrix(W, DH)
        hs = jnp.where((_iota2((W, W), 0) & (DH - 1)) == (_iota2((W, W), 1) & (DH - 1)), 1.0, 0.0).astype(f32)

        def norm_bwd(x, dn, gain):
            r = lax.rsqrt(_group_mean(x * x, gm) + EPS)
            gg = jnp.sum(dn * x * r, axis=0, keepdims=True)
            u = dn * gain
            return r * u - x * (r * r * r) * _group_mean(u * x, gm), gg

        q, k, gate = q_ref[...], k_ref[...], g_ref[...]
        dq, ggq = norm_bwd(q, dq_ref[...] * (DH ** -0.5), gq_ref[...])
        dk, ggk = norm_bwd(k, dk_ref[...], gk_ref[...])
        ggq_ref[...] += jnp.dot(jnp.broadcast_to(ggq, (8, W)), hs, precision=HI, preferred_element_type=f32)[0:1]
        ggk_ref[...] += jnp.dot(jnp.broadcast_to(ggk, (8, W)), hs, precision=HI, preferred_element_type=f32)[0:1]
        dgate = dy_ref[...] * o_ref[...] * _dsilu(gate)
        dck_v = dcq_ref[...]
        for hp in range(NH // HP):
            dck_v = dck_v + dck_ref[hp]
        rc = jnp.dot(_upper_tri(tm), dck_v, precision=HI, preferred_element_type=f32) + carry[...]
        carry[...] += jnp.sum(dck_v, axis=0, keepdims=True)
        f = f_ref[...] + fb_ref[...]
        df = jnp.where(_iota2((tm, 128), 1) < NH, rc * _sigmoid(-f), 0.0)
        gfb_ref[...] += jnp.sum(df, axis=0, keepdims=True)
        dz_ref[:, 0:W] = dq.astype(bf16)
        dz_ref[:, W:2 * W] = dk.astype(bf16)
        dz_ref[:, 2 * W:3 * W] = dv_ref[...].astype(bf16)
        dz_ref[:, 3 * W:4 * W] = dgate.astype(bf16)
        dz_ref[:, 4 * W:4 * W + 128] = df.astype(bf16)

    rev = lambda i: nt - 1 - i
    zb = lambda col: pl.BlockSpec((tm, W), lambda i, c=col: (rev(i), c))
    hspec = pl.BlockSpec((tm, W), lambda i: (rev(i), 0))
    return _pcall(
        body, name="attn_post", grid=(nt,),
        in_specs=[zb(c0), zb(c0 + 1), zb(c0 + 3),
                  pl.BlockSpec((tm, 128), lambda i: (rev(i), OFF_F // 128)),
                  pl.BlockSpec((tm, W), lambda i: (rev(i), 0)),
                  hspec, hspec, hspec, hspec,
                  pl.BlockSpec((NH // HP, tm, 128), lambda i: (0, rev(i), 0)),
                  pl.BlockSpec((tm, 128), lambda i: (rev(i), 0)),
                  pl.BlockSpec((1, 128), lambda i: (0, 0)),
                  pl.BlockSpec((1, W), lambda i: (0, 0)), pl.BlockSpec((1, W), lambda i: (0, 0)),
                  pl.BlockSpec(memory_space=pl.ANY)],
        out_specs=[pl.BlockSpec((tm, 4 * W + 128), lambda i: (rev(i), OFF_ATT // (4 * W + 128))),
                   pl.BlockSpec((1, W), lambda i: (0, 0)), pl.BlockSpec((1, W), lambda i: (0, 0)),
                   pl.BlockSpec((1, 128), lambda i: (0, 0))],
        out_shape=[jax.ShapeDtypeStruct((T, NZ), bf16), jax.ShapeDtypeStruct((1, W), f32),
                   jax.ShapeDtypeStruct((1, W), f32), jax.ShapeDtypeStruct((1, 128), f32)],
        scratch_shapes=[pltpu.VMEM((1, 128), f32)],
        input_output_aliases={14: 0},
        compiler_params=_params("arbitrary"),
    )(z, z, z, z, dy, oh, dqh, dkh, dvh, dck, dcq, fbias, gq, gk, dzbuf)


def _merge_fwd(ya, oh, z, yc, yd, mb, x, p, wup, wo, gp, wpg, wpp, tm):
    T = x.shape[0]
    cg = OFF_ATT // W + 3

    def body(ya_ref, oh_ref, bg_ref, yc_ref, yd_ref, ml_ref, mb_ref, x_ref, p_ref, wup_ref, wo_ref, gp_ref,
             wpg_ref, wpp_ref, yb_ref, mg_ref, x1_ref, x2_ref):
        yb = (oh_ref[...] * _silu(bg_ref[...])).astype(bf16)
        yb_ref[...] = yb
        ys = (ya_ref[...], yb, yc_ref[...], yd_ref[...])
        merged = jnp.zeros((tm, D), f32)
        for b in range(NBR):
            sg = _sigmoid(ml_ref[:, b * D:(b + 1) * D] + mb_ref[b:b + 1, :])
            merged = merged + sg * _mm(ys[b], wup_ref[b])
        mgb = merged.astype(bf16)
        mg_ref[...] = mgb
        x1 = x_ref[...] + _mm(mgb, wo_ref[...])
        x1_ref[...] = x1
        r = lax.rsqrt(jnp.mean(x1 * x1, axis=-1, keepdims=True) + EPS)
        hp = (x1 * r * gp_ref[...]).astype(bf16)
        gate = _sigmoid(_mm(hp, wpg_ref[...]))
        x2_ref[...] = x1 + gate * _mm(p_ref[...].astype(bf16), wpp_ref[...])

    row = lambda width: pl.BlockSpec((tm, width), lambda i: (i, 0))
    full = lambda *shape: pl.BlockSpec(shape, lambda i: (0,) * len(shape))
    return _pcall(
        body, name="merge_fwd", grid=(T // tm,),
        in_specs=[row(W), row(W), _zblock(tm, cg), row(W), row(W),
                  pl.BlockSpec((tm, NBR * D), lambda i: (i, 0)), full(NBR, D), row(D), row(PLE),
                  full(NBR, W, D), full(D, D), full(1, D), full(D, D), full(PLE, D)],
        out_specs=[row(W), row(D), row(D), row(D)],
        out_shape=[jax.ShapeDtypeStruct((T, W), bf16), jax.ShapeDtypeStruct((T, D), bf16),
                   jax.ShapeDtypeStruct((T, D), f32), jax.ShapeDtypeStruct((T, D), f32)],
        compiler_params=_params("parallel"),
    )(ya, oh, z, yc, yd, z, mb, x, p, wup, wo, gp, wpg, wpp)


def _layer_slabs(li, bufs):
    if bufs is None:
        return [], []
    return list(bufs), [pl.BlockSpec(memory_space=pl.ANY)] * len(bufs)


def _ple_bwd(dx2, x1, p, gp, wpg, wpp, tm, li, bufs):
    T = x1.shape[0]
    SH = D // N_DEV
    nt = T // tm
    extra, extra_specs = _layer_slabs(li, bufs)

    def body(dx2_ref, x1_ref, p_ref, gp_ref, wpg_ref, wpp_ref, *rest):
        dx1_ref, gwpg_ref, gwpp_ref, ggp_ref, gwpg_acc, gwpp_acc = rest[len(extra):]

        @pl.when(pl.program_id(0) == 0)
        def _():
            gwpg_acc[...] = jnp.zeros_like(gwpg_acc)
            gwpp_acc[...] = jnp.zeros_like(gwpp_acc)
            ggp_ref[...] = jnp.zeros_like(ggp_ref)

        x1, dx2 = x1_ref[...], dx2_ref[...]
        r = lax.rsqrt(jnp.mean(x1 * x1, axis=-1, keepdims=True) + EPS)
        xh = x1 * r
        gp = gp_ref[...]
        hp = (xh * gp).astype(bf16)
        gate = _sigmoid(_mm(hp, wpg_ref[...]))
        pb = p_ref[...].astype(bf16)
        pp = _mm(pb, wpp_ref[...])
        dpre = ((dx2 * pp) * gate * (1.0 - gate)).astype(bf16)
        gwpp_acc[...] += _mm_tn(pb, (dx2 * gate).astype(bf16))
        gwpg_acc[...] += _mm_tn(hp, dpre)
        dhp = _mm_nt(dpre, wpg_ref[...])
        ggp_ref[...] += jnp.sum(dhp * xh, axis=0, keepdims=True)
        u = dhp * gp
        dx1_ref[...] = dx2 + r * u - x1 * (r * r * r) * jnp.mean(u * x1, axis=-1, keepdims=True)

        @pl.when(pl.program_id(0) == nt - 1)
        def _():
            gwpg_ref[...] = gwpg_acc[...].reshape(N_DEV, SH, D).astype(bf16)
            for d in range(N_DEV):
                gwpp_ref[d] = gwpp_acc[:, d * SH:(d + 1) * SH].astype(bf16)

    row = lambda width: pl.BlockSpec((tm, width), lambda i: (i, 0))
    full = lambda *shape: pl.BlockSpec(shape, lambda i: (0,) * len(shape))
    n_in = 6
    return _pcall(
        body, name="ple_bwd", grid=(nt,),
        in_specs=[row(D), row(D), row(PLE), full(1, D), full(D, D), full(PLE, D)] + extra_specs,
        out_specs=[row(D), pl.BlockSpec((N_DEV, SH, D), lambda i: (0, li, 0)),
                   pl.BlockSpec((N_DEV, PLE, SH), lambda i: (0, li, 0)), full(1, D)],
        out_shape=[jax.ShapeDtypeStruct((T, D), f32), jax.ShapeDtypeStruct((N_DEV, DEPTH * SH, D), bf16),
                   jax.ShapeDtypeStruct((N_DEV, DEPTH * PLE, SH), bf16), jax.ShapeDtypeStruct((1, D), f32)],
        scratch_shapes=[pltpu.VMEM((D, D), f32), pltpu.VMEM((PLE, D), f32)],
        input_output_aliases={n_in + k: 1 + k for k in range(len(extra))},
        compiler_params=_params("arbitrary"),
    )(dx2, x1, p, gp, wpg, wpp, *extra)


def _merge_bwd(dx1, mg, ya, yb, yc, yd, z, mb, wup, wo, tm, li, bufs):
    T = dx1.shape[0]
    SH = D // N_DEV
    nt = T // tm
    extra, extra_specs = _layer_slabs(li, bufs)

    def body(dx1_ref, mg_ref, ya_ref, yb_ref, yc_ref, yd_ref, ml_ref, mb_ref, wup_ref, wo_ref, *rest):
        dml_ref, dya_ref, dyb_ref, dyc_ref, dyd_ref, gwo_ref, gwup_ref, gmb_ref, gwo_acc, gwup_acc = rest[len(extra):]

        @pl.when(pl.program_id(0) == 0)
        def _():
            gwo_acc[...] = jnp.zeros_like(gwo_acc)
            gwup_acc[...] = jnp.zeros_like(gwup_acc)
            gmb_ref[...] = jnp.zeros_like(gmb_ref)

        dx1b = dx1_ref[...].astype(bf16)
        gwo_acc[...] += _mm_tn(mg_ref[...], dx1b)
        dm = _mm_nt(dx1b, wo_ref[...])
        ys = (ya_ref, yb_ref, yc_ref, yd_ref)
        dys = (dya_ref, dyb_ref, dyc_ref, dyd_ref)
        for b in range(NBR):
            y = ys[b][...]
            up = _mm(y, wup_ref[b])
            sg = _sigmoid(ml_ref[:, b * D:(b + 1) * D] + mb_ref[b:b + 1, :])
            dup = (dm * sg).astype(bf16)
            dml = dm * up * sg * (1.0 - sg)
            gmb_ref[b:b + 1, :] += jnp.sum(dml, axis=0, keepdims=True)
            dml_ref[:, b * D:(b + 1) * D] = dml.astype(bf16)
            gwup_acc[b] += _mm_tn(y, dup)
            dys[b][...] = _mm_nt(dup, wup_ref[b])

        @pl.when(pl.program_id(0) == nt - 1)
        def _():
            gwo_ref[...] = gwo_acc[...].reshape(N_DEV, SH, D).astype(bf16)
            for d in range(N_DEV):
                gwup_ref[d] = gwup_acc[:, :, d * SH:(d + 1) * SH].reshape(NBR * W, SH).astype(bf16)

    row = lambda width: pl.BlockSpec((tm, width), lambda i: (i, 0))
    full = lambda *shape: pl.BlockSpec(shape, lambda i: (0,) * len(shape))
    n_in = 10
    return _pcall(
        body, name="merge_bwd", grid=(nt,),
        in_specs=[row(D), row(D), row(W), row(W), row(W), row(W), row(NBR * D), full(NBR, D),
                  full(NBR, W, D), full(D, D)] + extra_specs,
        out_specs=[row(NBR * D), row(W), row(W), row(W), row(W),
                   pl.BlockSpec((N_DEV, SH, D), lambda i: (0, li, 0)),
                   pl.BlockSpec((N_DEV, NBR * W, SH), lambda i: (0, li, 0)), full(NBR, D)],
        out_shape=[jax.ShapeDtypeStruct((T, NZ), bf16)] + [jax.ShapeDtypeStruct((T, W), f32)] * 4
        + [jax.ShapeDtypeStruct((N_DEV, DEPTH * SH, D), bf16),
           jax.ShapeDtypeStruct((N_DEV, DEPTH * NBR * W, SH), bf16),
           jax.ShapeDtypeStruct((NBR, D), f32)],
        scratch_shapes=[pltpu.VMEM((D, D), f32), pltpu.VMEM((NBR, W, D), f32)],
        input_output_aliases={n_in + k: 5 + k for k in range(len(extra))},
        compiler_params=_params("arbitrary"),
    )(dx1, mg, ya, yb, yc, yd, z, mb, wup, wo, *extra)


def _loss_head(y, target, tm):
    T = y.shape[0]

    def body(y_ref, t_ref, loss_ref, dy_ref, acc):
        i = pl.program_id(0)

        @pl.when(i == 0)
        def _():
            acc[...] = jnp.zeros_like(acc)

        e = y_ref[...] - t_ref[...]
        dy_ref[...] = e * (1.0 / D)
        acc[...] += jnp.sum(e * e, axis=0, keepdims=True)

        @pl.when(i == T // tm - 1)
        def _():
            loss_ref[...] = jnp.sum(acc[...], axis=1, keepdims=True) * (0.5 / D)

    return _pcall(
        body, name="loss_head", grid=(T // tm,),
        in_specs=[pl.BlockSpec((tm, D), lambda i: (i, 0)), pl.BlockSpec((tm, D), lambda i: (i, 0))],
        out_specs=[pl.BlockSpec((1, 1), lambda i: (0, 0)), pl.BlockSpec((tm, D), lambda i: (i, 0))],
        out_shape=[jax.ShapeDtypeStruct((1, 1), f32), jax.ShapeDtypeStruct((T, D), f32)],
        scratch_shapes=[pltpu.VMEM((1, D), f32)],
        compiler_params=_params("arbitrary"),
    )(y, target)


def _lb_softmax_rows(l_ref):
    rows = [l_ref[i:i + 1, :] for i in range(DEPTH)]
    m = rows[0]
    for r in rows[1:]:
        m = jnp.maximum(m, r)
    es = [jnp.exp(r - m) for r in rows]
    tot = es[0]
    for e in es[1:]:
        tot = tot + e
    return [e / tot for e in es]


def _lb_partial_sums(pr):
    sums = [jnp.zeros_like(pr[0])]
    for i in range(1, DEPTH):
        sums.append(sums[-1] + pr[i])
    return sums


def _stack_rows(rows, width):
    idx = _iota2((8, width), 0)
    out = jnp.zeros((8, width), f32)
    for i, r in enumerate(rows):
        out = jnp.where(idx == i, r, out)
    return out


def _lower_bounds(lb_logits):
    def body(l_ref, o_ref):
        sums = _lb_partial_sums(_lb_softmax_rows(l_ref))
        o_ref[...] = _stack_rows([jnp.clip(s, 0.0, 1.0) for s in sums], W)

    return _pcall(body, name="lower_bounds", out_shape=jax.ShapeDtypeStruct((8, W), f32))(lb_logits)


def _lower_bounds_bwd(lb_logits, dlower):
    def body(l_ref, d_ref, o_ref):
        pr = _lb_softmax_rows(l_ref)
        sums = _lb_partial_sums(pr)
        dl = [jnp.where((sums[i] > 0.0) & (sums[i] < 1.0), d_ref[i:i + 1, :], 0.0) for i in range(DEPTH)]
        dp = [jnp.zeros_like(pr[0])] * DEPTH
        run = jnp.zeros_like(pr[0])
        for j in reversed(range(1, DEPTH)):
            run = run + dl[j]
            dp[j] = run
        inner = pr[0] * dp[0]
        for j in range(1, DEPTH):
            inner = inner + pr[j] * dp[j]
        o_ref[...] = _stack_rows([pr[j] * (dp[j] - inner) for j in range(DEPTH)], W)

    return _pcall(body, name="lower_bounds_bwd", out_shape=jax.ShapeDtypeStruct((8, W), f32))(lb_logits, dlower)


def _row_tile(rows, cols, budget_bytes=1 << 20, mult=8):
    if rows % mult:
        return rows
    best = mult
    for t in range(mult, rows + 1, mult):
        if rows % t == 0 and t * cols * 4 <= budget_bytes:
            best = t
    return best


def _sum_slabs(land):
    N, R, C = land.shape
    tr = _row_tile(R, C * N, mult=16)

    def body(l_ref, o_ref):
        acc = l_ref[0].astype(f32)
        for j in range(1, N):
            acc = acc + l_ref[j].astype(f32)
        o_ref[...] = acc

    return _pcall(
        body, name="sum_slabs", grid=(R // tr,),
        in_specs=[pl.BlockSpec((N, tr, C), lambda i: (0, i, 0))],
        out_specs=pl.BlockSpec((tr, C), lambda i: (i, 0)),
        out_shape=jax.ShapeDtypeStruct((R, C), f32),
        compiler_params=_params("parallel"),
    )(land)


def _adamw_update(w_ref, g_ref, m_ref, v_ref, d_ref, nm_ref, nv_ref):
    c1 = 1.0 / (1.0 - ADAM_B1 ** ADAM_STEP)
    c2 = 1.0 / (1.0 - ADAM_B2 ** ADAM_STEP)
    gv = g_ref[...]
    nm = ADAM_B1 * m_ref[...] + (1.0 - ADAM_B1) * gv
    nv = ADAM_B2 * v_ref[...] + (1.0 - ADAM_B2) * (gv * gv)
    nm_ref[...] = nm
    nv_ref[...] = nv
    d_ref[...] = -ADAM_LR * ((nm * c1) / (jnp.sqrt(nv * c2) + ADAM_EPS) + ADAM_WD * w_ref[...])


def _adamw3(w, g, m, v):
    L, R, C = w.shape
    tr = _row_tile(R, C)

    def body(*refs):
        _adamw_update(*refs)

    spec = pl.BlockSpec((None, tr, C), lambda l, i: (l, i, 0))
    return _pcall(
        body, name="adamw3", grid=(L, R // tr),
        in_specs=[spec] * 4, out_specs=[spec] * 3,
        out_shape=[jax.ShapeDtypeStruct((L, R, C), f32)] * 3,
        compiler_params=_params("parallel", "parallel"),
    )(w, g, m, v)


def _adamw(w, g, m, v):
    if w.ndim == 3:
        return _adamw3(w, g, m, v)
    R, C = w.shape
    tr = _row_tile(R, C)
    c1 = 1.0 / (1.0 - ADAM_B1 ** ADAM_STEP)
    c2 = 1.0 / (1.0 - ADAM_B2 ** ADAM_STEP)

    def body(w_ref, g_ref, m_ref, v_ref, d_ref, nm_ref, nv_ref):
        gv = g_ref[...]
        nm = ADAM_B1 * m_ref[...] + (1.0 - ADAM_B1) * gv
        nv = ADAM_B2 * v_ref[...] + (1.0 - ADAM_B2) * (gv * gv)
        nm_ref[...] = nm
        nv_ref[...] = nv
        d_ref[...] = -ADAM_LR * ((nm * c1) / (jnp.sqrt(nv * c2) + ADAM_EPS) + ADAM_WD * w_ref[...])

    spec = pl.BlockSpec((tr, C), lambda i: (i, 0))
    return _pcall(
        body, name="adamw", grid=(R // tr,),
        in_specs=[spec] * 4, out_specs=[spec] * 3,
        out_shape=[jax.ShapeDtypeStruct((R, C), f32)] * 3,
        compiler_params=_params("parallel"),
    )(w, g, m, v)


def _my_id():
    return lax.axis_index("x") * 4 + lax.axis_index("y") * 2 + lax.axis_index("c")


def _peer(k):
    x, y, c = lax.axis_index("x"), lax.axis_index("y"), lax.axis_index("c")
    kx, ky, kc = (k >> 2) & 1, (k >> 1) & 1, k & 1
    px, py, pc = x ^ kx, y ^ ky, c ^ kc
    return (px, py, pc), px * 4 + py * 2 + pc


def _all_gather(shards, axes):
    n = len(shards)

    def body(*refs):
        start, forward, finish = _gather_phases(shards, axes, refs[:n], refs[n:2 * n], *refs[2 * n:])
        start()
        forward()
        finish()

    hbm = pl.BlockSpec(memory_space=pltpu.HBM)
    return _pcall(
        body, name="all_gather",
        in_specs=[hbm] * n, out_specs=[hbm] * n,
        out_shape=_gathered_shapes(shards, axes),
        scratch_shapes=_gather_semaphores(n),
    )(*shards)


def _gathered_shapes(shards, axes):
    def full_shape(s, ax):
        shp = list(s.shape)
        shp[ax] *= N_DEV
        return tuple(shp)

    return [jax.ShapeDtypeStruct(full_shape(s, ax), s.dtype) for s, ax in zip(shards, axes)]


def _gather_semaphores(n):
    return [pltpu.SemaphoreType.DMA((n, N_DEV - 1)), pltpu.SemaphoreType.DMA((n, N_DEV - 1)),
            pltpu.SemaphoreType.DMA((n,))]


def _gather_phases(shards, axes, srcs, outs, send_sems, recv_sems, local_sems):
    n = len(shards)
    x, y, c = lax.axis_index("x"), lax.axis_index("y"), lax.axis_index("c")
    me, sibling = (x, y, c), (x, y, 1 - c)
    chips = [(1 - x, y), (x, 1 - y), (1 - x, 1 - y)]

    def block(a, dev):
        j = dev[0] * 4 + dev[1] * 2 + dev[2]
        size = shards[a].shape[axes[a]]
        start = pl.multiple_of(j * size, size)
        if axes[a] == 0:
            return outs[a].at[pl.ds(start, size), :]
        if axes[a] == 1:
            return outs[a].at[:, pl.ds(start, size), :]
        return outs[a].at[:, pl.ds(start, size)]

    def copy(a, k, dev, to, src=None):
        return pltpu.make_async_remote_copy(
            src_ref=block(a, dev) if src is None else src, dst_ref=block(a, dev),
            send_sem=send_sems.at[a, k], recv_sem=recv_sems.at[a, k],
            device_id=to, device_id_type=pl.DeviceIdType.MESH)

    def mine():
        return [pltpu.make_async_copy(srcs[a], block(a, me), local_sems.at[a]) for a in range(n)]

    def first():
        cps = []
        for a in range(n):
            cps.append(copy(a, 0, me, sibling, src=srcs[a]))
            cps += [copy(a, 1 + j, me, (*chip, c), src=srcs[a]) for j, chip in enumerate(chips)]
        return cps

    def passed():
        return [copy(a, 4 + j, (*chip, c), sibling) for j, chip in enumerate(chips) for a in range(n)]

    def start():
        for cp in mine() + first():
            cp.start()

    def forward():
        for j, chip in enumerate(chips):
            for a in range(n):
                copy(a, 1 + j, (*chip, c), me).wait_recv()
                copy(a, 4 + j, (*chip, c), sibling).start()

    def finish():
        for a in range(n):
            copy(a, 0, sibling, me).wait_recv()
            for j, chip in enumerate(chips):
                copy(a, 4 + j, (*chip, 1 - c), me).wait_recv()
        for cp in first() + passed():
            cp.wait_send()
        for cp in mine():
            cp.wait()

    return start, forward, finish


N_CHIP = N_DEV // 2


def _exchange_sibling(sliced):
    n = len(sliced)

    def body(*refs):
        srcs, outs = refs[:n], refs[n:2 * n]
        send_sems, recv_sems = refs[2 * n:]
        x, y, c = lax.axis_index("x"), lax.axis_index("y"), lax.axis_index("c")
        copies = []
        for a in range(n):
            for q in range(N_CHIP):
                cp = pltpu.make_async_remote_copy(
                    src_ref=srcs[a].at[2 * q + (1 - c)], dst_ref=outs[a].at[q],
                    send_sem=send_sems.at[a, q], recv_sem=recv_sems.at[a, q],
                    device_id=(x, y, 1 - c), device_id_type=pl.DeviceIdType.MESH)
                cp.start()
                copies.append(cp)
        for cp in copies:
            cp.wait_recv()
        for cp in copies:
            cp.wait_send()

    hbm = pl.BlockSpec(memory_space=pltpu.HBM)
    return _pcall(
        body, name="grad_exchange_sibling",
        in_specs=[hbm] * n, out_specs=[hbm] * n,
        out_shape=[jax.ShapeDtypeStruct((N_CHIP,) + s.shape[1:], s.dtype) for s in sliced],
        scratch_shapes=[pltpu.SemaphoreType.DMA((n, N_CHIP)), pltpu.SemaphoreType.DMA((n, N_CHIP))],
    )(*sliced)


def _pair_sum(own, recv):
    _, R, C = own.shape
    tr = _row_tile(R, C, mult=16)
    side = lax.axis_index("c").astype(jnp.int32).reshape(1)

    def body(c_ref, own_ref, recv_ref, o_ref):
        o_ref[...] = (own_ref[...].astype(f32) + recv_ref[...].astype(f32)).astype(o_ref.dtype)

    return _pcall(
        body, name="pair_sum",
        grid_spec=pltpu.PrefetchScalarGridSpec(
            num_scalar_prefetch=1, grid=(N_CHIP, R // tr),
            in_specs=[pl.BlockSpec((None, tr, C), lambda q, i, c: (2 * q + c[0], i, 0)),
                      pl.BlockSpec((None, tr, C), lambda q, i, c: (q, i, 0))],
            out_specs=pl.BlockSpec((None, tr, C), lambda q, i, c: (q, i, 0))),
        out_shape=jax.ShapeDtypeStruct((N_CHIP, R, C), own.dtype),
        compiler_params=_params("parallel", "parallel"),
    )(side, own, recv)


def _exchange_chips(partial, whole):
    ns, nw = len(partial), len(whole)

    def body(*refs):
        srcs, outs = refs[:ns + nw], refs[ns + nw:2 * (ns + nw)]
        send_sems, recv_sems, wsend_sems, wrecv_sems, local_sems = refs[2 * (ns + nw):]
        x, y, c = lax.axis_index("x"), lax.axis_index("y"), lax.axis_index("c")
        me, myq = _my_id(), x * 2 + y
        chips = [(1 - x, y), (x, 1 - y), (1 - x, 1 - y)]
        locals_ = [pltpu.make_async_copy(srcs[a].at[myq], outs[a].at[myq], local_sems.at[a]) for a in range(ns)]
        locals_ += [pltpu.make_async_copy(srcs[ns + b], outs[ns + b].at[me], local_sems.at[ns + b])
                    for b in range(nw)]
        for cp in locals_:
            cp.start()
        sends, recvs = [], []
        for j, chip in enumerate(chips):
            q = chip[0] * 2 + chip[1]
            for a in range(ns):
                cp = pltpu.make_async_remote_copy(
                    src_ref=srcs[a].at[q], dst_ref=outs[a].at[myq],
                    send_sem=send_sems.at[a, j], recv_sem=recv_sems.at[a, j],
                    device_id=(*chip, c), device_id_type=pl.DeviceIdType.MESH)
                cp.start()
                sends.append(cp)
                recvs.append(pltpu.make_async_remote_copy(
                    src_ref=srcs[a].at[q], dst_ref=outs[a].at[q],
                    send_sem=send_sems.at[a, j], recv_sem=recv_sems.at[a, j],
                    device_id=(*chip, c), device_id_type=pl.DeviceIdType.MESH))
        for k in range(1, N_DEV):
            peer, pid = _peer(k)
            for b in range(nw):
                cp = pltpu.make_async_remote_copy(
                    src_ref=srcs[ns + b], dst_ref=outs[ns + b].at[me],
                    send_sem=wsend_sems.at[b, k - 1], recv_sem=wrecv_sems.at[b, k - 1],
                    device_id=peer, device_id_type=pl.DeviceIdType.MESH)
                cp.start()
                sends.append(cp)
                recvs.append(pltpu.make_async_remote_copy(
                    src_ref=srcs[ns + b], dst_ref=outs[ns + b].at[pid],
                    send_sem=wsend_sems.at[b, k - 1], recv_sem=wrecv_sems.at[b, k - 1],
                    device_id=peer, device_id_type=pl.DeviceIdType.MESH))
        for cp in recvs:
            cp.wait_recv()
        for cp in sends:
            cp.wait_send()
        for cp in locals_:
            cp.wait()

    hbm = pl.BlockSpec(memory_space=pltpu.HBM)
    shapes = [jax.ShapeDtypeStruct(s.shape, s.dtype) for s in partial]
    shapes += [jax.ShapeDtypeStruct((N_DEV,) + s.shape, s.dtype) for s in whole]
    return _pcall(
        body, name="grad_exchange_chips",
        in_specs=[hbm] * (ns + nw), out_specs=[hbm] * (ns + nw), out_shape=shapes,
        scratch_shapes=[pltpu.SemaphoreType.DMA((ns, N_CHIP - 1)), pltpu.SemaphoreType.DMA((ns, N_CHIP - 1)),
                        pltpu.SemaphoreType.DMA((nw, N_DEV - 1)), pltpu.SemaphoreType.DMA((nw, N_DEV - 1)),
                        pltpu.SemaphoreType.DMA((ns + nw,))],
    )(*partial, *whole)


def _permute_cols(w):
    pad = jnp.zeros(w.shape[:-1] + (NZ - OFF_F - NH,), w.dtype)
    return jnp.concatenate([
        w[..., 3844:7940],
        w[..., 0:1024],
        w[..., 2052:3076],
        w[..., 3076:3844],
        w[..., 1024:2048],
        w[..., 2048:2052], pad], axis=-1)


def _unpermute_cols(g):
    return jnp.concatenate([
        g[..., OFF_CONV:OFF_CONV + 1024],
        g[..., OFF_ATT:OFF_ATT + 1024],
        g[..., OFF_F:OFF_F + NH],
        g[..., OFF_HGRN:OFF_HGRN + 1024],
        g[..., OFF_SGU:OFF_SGU + 768],
        g[..., 0:4096]], axis=-1)


_SMALL = (
    ("norm_mix", (DEPTH, D)), ("conv_w", (DEPTH, CONV_WIDTH, W)), ("conv_b", (DEPTH, W)),
    ("fgate_bias", (DEPTH, NH)), ("q_norm", (DEPTH, DH)), ("k_norm", (DEPTH, DH)),
    ("lb_logits", (DEPTH, W)), ("hgrn_norm", (DEPTH, W)), ("sgu_norm", (DEPTH, W)),
    ("spatial_w", (DEPTH, NH, SGU_CHUNK, SGU_CHUNK)), ("spatial_b", (DEPTH, NH, SGU_CHUNK)),
    ("merge_b", (DEPTH, NBR, D)), ("norm_ple", (DEPTH, D)),
)


def _small_rows(shape):
    size = 1
    for s in shape:
        size *= s
    rows = -(-size // 128)
    return size, -(-rows // 8) * 8


def _pack_small(parts):
    out = []
    for name, shape in _SMALL:
        size, rows = _small_rows(shape)
        flat = parts[name].astype(f32).reshape(-1)
        flat = jnp.pad(flat, (0, rows * 128 - size))
        out.append(flat.reshape(rows, 128))
    return jnp.concatenate(out, axis=0)


def _unpack_small(buf):
    parts, r0 = {}, 0
    for name, shape in _SMALL:
        size, rows = _small_rows(shape)
        parts[name] = buf[r0:r0 + rows].reshape(-1)[:size].reshape(shape)
        r0 += rows
    return parts


def _shard_cols(a, width):
    return lax.dynamic_slice_in_dim(a, _my_id() * width, width, axis=a.ndim - 1)


def kernel(x, p, norm_mix, w_in, conv_w, conv_b, fgate_bias, q_norm, k_norm, lb_logits, hgrn_norm, sgu_norm, spatial_w, spatial_b, w_up, merge_b, w_o, norm_ple, w_ple_gate, w_ple_proj, loss_target, m_norm_mix, m_w_in, m_conv_w, m_conv_b, m_fgate_bias, m_q_norm, m_k_norm, m_lb_logits, m_hgrn_norm, m_sgu_norm, m_spatial_w, m_spatial_b, m_w_up, m_merge_b, m_w_o, m_norm_ple, m_w_ple_gate, m_w_ple_proj, v_norm_mix, v_w_in, v_conv_w, v_conv_b, v_fgate_bias, v_q_norm, v_k_norm, v_lb_logits, v_hgrn_norm, v_sgu_norm, v_spatial_w, v_spatial_b, v_w_up, v_merge_b, v_w_o, v_norm_ple, v_w_ple_gate, v_w_ple_proj):
    T = x.shape[1]
    SH = D // N_DEV
    CW = W // N_DEV
    tm = 512 if T % 512 == 0 else T
    tmm = 256 if T % 256 == 0 else T
    x0 = x.reshape(T, D)
    target = loss_target.reshape(T, D)

    small_shard = jnp.concatenate([
        merge_b.reshape(DEPTH * NBR, SH),
        jnp.pad(conv_w.reshape(DEPTH * CONV_WIDTH, CW), ((0, 16 - DEPTH * CONV_WIDTH), (0, SH - CW)))], axis=0)
    win_s = _permute_cols(w_in).astype(bf16)
    wup_s = w_up.astype(bf16).reshape(DEPTH, NBR * W, SH)
    wo_s, wpg_s, wpp_s = w_o.astype(bf16), w_ple_gate.astype(bf16), w_ple_proj.astype(bf16)
    win0, wup0, wo0, wpg0, wpp0, g_small = _all_gather(
        [win_s[0], wup_s[0], wo_s[0], wpg_s[0], wpp_s[0], small_shard], [0, -1, 0, 0, -1, -1])
    win_f, wup_f, wo_f, wpg_f, wpp_f = [win0], [wup0.reshape(NBR, W, D)], [wo0], [wpg0], [wpp0]
    win_later = [win_s[li] for li in range(1, DEPTH)]
    other_later = [(wup_s[li], wo_s[li], wpg_s[li], wpp_s[li]) for li in range(1, DEPTH)]
    mb_f = g_small[0:DEPTH * NBR].reshape(DEPTH, NBR, D)
    cw_f = g_small[16:16 + DEPTH * CONV_WIDTH].reshape(DEPTH, CONV_WIDTH, N_DEV, SH)[..., 0:CW]
    cw_f = cw_f.reshape(DEPTH, CONV_WIDTH, W)

    loss_local, dx, gw, gs_full = _forward_backward(
        x0, p[:, 0], target, win_f, wup_f, wo_f, wpg_f, wpp_f, mb_f, cw_f, norm_mix, conv_b, fgate_bias, q_norm,
        k_norm, lb_logits, hgrn_norm, sgu_norm, spatial_w, spatial_b, norm_ple, win_later, other_later)
    loss = lax.psum(loss_local[0, 0], AXES)
    grad_x = dx.reshape(1, T, D)

    weights = dict(norm_mix=norm_mix, w_in=w_in, conv_w=conv_w, conv_b=conv_b, fgate_bias=fgate_bias, q_norm=q_norm,
                   k_norm=k_norm, lb_logits=lb_logits, hgrn_norm=hgrn_norm, sgu_norm=sgu_norm, spatial_w=spatial_w,
                   spatial_b=spatial_b, w_up=w_up, merge_b=merge_b, w_o=w_o, norm_ple=norm_ple,
                   w_ple_gate=w_ple_gate, w_ple_proj=w_ple_proj)
    ms = dict(norm_mix=m_norm_mix, w_in=m_w_in, conv_w=m_conv_w, conv_b=m_conv_b, fgate_bias=m_fgate_bias,
              q_norm=m_q_norm, k_norm=m_k_norm, lb_logits=m_lb_logits, hgrn_norm=m_hgrn_norm, sgu_norm=m_sgu_norm,
              spatial_w=m_spatial_w, spatial_b=m_spatial_b, w_up=m_w_up, merge_b=m_merge_b, w_o=m_w_o,
              norm_ple=m_norm_ple, w_ple_gate=m_w_ple_gate, w_ple_proj=m_w_ple_proj)
    vs = dict(norm_mix=v_norm_mix, w_in=v_w_in, conv_w=v_conv_w, conv_b=v_conv_b, fgate_bias=v_fgate_bias,
              q_norm=v_q_norm, k_norm=v_k_norm, lb_logits=v_lb_logits, hgrn_norm=v_hgrn_norm, sgu_norm=v_sgu_norm,
              spatial_w=v_spatial_w, spatial_b=v_spatial_b, w_up=v_w_up, merge_b=v_merge_b, w_o=v_w_o,
              norm_ple=v_norm_ple, w_ple_gate=v_w_ple_gate, w_ple_proj=v_w_ple_proj)
    return _exchange_and_update(loss, grad_x, gw, gs_full, weights, ms, vs)


def _forward_backward(x0, p, target, win_f, wup_f, wo_f, wpg_f, wpp_f, mb_f, cw_f, norm_mix, conv_b, fgate_bias,
                      q_norm, k_norm, lb_logits, hgrn_norm, sgu_norm, spatial_w, spatial_b, norm_ple, win_later=(),
                      other_later=()):
    T = x0.shape[0]
    tm = 512 if T % 512 == 0 else T
    tmm = 256 if T % 256 == 0 else T
    tmi = 1024 if T % 1024 == 0 else tm
    lower = _lower_bounds(lb_logits)
    fb_pad = jnp.pad(fgate_bias, ((0, 0), (0, 128 - NH)))
    gq_t = jnp.tile(q_norm, (1, NH))
    gk_t = jnp.tile(k_norm, (1, NH))
    sbe = jnp.repeat(jnp.swapaxes(spatial_b, 1, 2), DH, axis=2)

    saved = []
    xc = x0
    p = p[:, None]
    for li in range(DEPTH):
        row = lambda a: a[li:li + 1]
        if li == 0 and win_later:
            z, h, *gathered = _inproj_fwd(xc, row(norm_mix), win_f[0], tmi, gather=win_later[:1])
            win_f = [win_f[0]] + gathered
        else:
            z, h = _inproj_fwd(xc, row(norm_mix), win_f[li], tmi)
        ya = _conv_fwd(z, cw_f[li], row(conv_b), tm)
        yd = _sgu_fwd(z, row(sgu_norm), spatial_w[li], sbe[li], tm)
        yc, o_pre, states = _hgrn_fwd(z, lower[li:li + 1], row(hgrn_norm), tmm)
        qt, kt, vt, kh, vh, cum = _attn_prep(z, row(fb_pad), row(gq_t), row(gk_t), tm)
        cum4 = jnp.transpose(cum[:, 0:NH])
        ccol, crow = cum4[:, :, None], cum4[:, None, :]
        if li == 0 and (len(win_later) > 1 or other_later):
            shards = list(win_later[1:]) + [a for group in other_later for a in group]
            axes = [0] * len(win_later[1:]) + [-1, 0, 0, -1] * len(other_later)
            oh, lse, *gathered = _attn_fwd(qt, kh, vt, crow, ccol, tm, gather=shards, gather_axes=axes)
            nw = len(win_later[1:])
            win_f = win_f + gathered[:nw]
            later = [gathered[nw + 4 * k:nw + 4 * k + 4] for k in range(len(other_later))]
            wup_f = [wup_f[0]] + [g4[0].reshape(NBR, W, D) for g4 in later]
            wo_f = [wo_f[0]] + [g4[1] for g4 in later]
            wpg_f = [wpg_f[0]] + [g4[2] for g4 in later]
            wpp_f = [wpp_f[0]] + [g4[3] for g4 in later]
        else:
            oh, lse = _attn_fwd(qt, kh, vt, crow, ccol, tm)
        yb, mg, x1, x2 = _merge_fwd(ya, oh, z, yc, yd, mb_f[li], xc, p[li, 0], wup_f[li], wo_f[li],
                                    row(norm_ple), wpg_f[li], wpp_f[li], tmm)
        saved.append(dict(x=xc, z=z, h=h, ya=ya, yb=yb, yc=yc, yd=yd, o_pre=o_pre, states=states,
                          qt=qt, kt=kt, kh=kh, vh=vh, crow=crow, ccol=ccol, oh=oh, lse=lse, mg=mg, x1=x1))
        xc = x2

    loss_local, dx = _loss_head(xc, target, tm)

    gw = dict(w_in=None, w_up=None, w_o=None, w_ple_gate=None, w_ple_proj=None)
    gs = {n: [None] * DEPTH for n, _ in _SMALL}
    dlower = [None] * DEPTH
    landed = {n: {} for n in gw}
    for li in reversed(range(DEPTH)):
        s = saved[li]
        row = lambda a: a[li:li + 1]
        first = li == DEPTH - 1
        dx1, gw["w_ple_gate"], gw["w_ple_proj"], ggp = _ple_bwd(
            dx, s["x1"], p[li, 0], row(norm_ple), wpg_f[li], wpp_f[li], tmm, li,
            None if first else (gw["w_ple_gate"], gw["w_ple_proj"]))
        gs["norm_ple"][li] = ggp[0]
        dz, dya, dyb, dyc, dyd, gw["w_o"], gw["w_up"], gs["merge_b"][li] = _merge_bwd(
            dx1, s["mg"], s["ya"], s["yb"], s["yc"], s["yd"], s["z"], mb_f[li], wup_f[li], wo_f[li], tmm, li,
            None if first else (gw["w_o"], gw["w_up"]))
        dz, gcw, gcb = _conv_bwd(s["z"], dya, cw_f[li], row(conv_b), dz, tm)
        gs["conv_w"][li], gs["conv_b"][li] = gcw[0:CONV_WIDTH], gcb[0]
        dz, gs["spatial_w"][li], gsb, ggv = _sgu_bwd(s["z"], dyd, row(sgu_norm), spatial_w[li], sbe[li], dz, tm)
        gs["spatial_b"][li] = jnp.transpose(gsb[:, ::DH])
        gs["sgu_norm"][li] = ggv[0]
        dz, ggn, glb = _hgrn_bwd(s["z"], lower[li:li + 1], row(hgrn_norm), s["o_pre"], s["states"], dyc, dz, tmm)
        gs["hgrn_norm"][li], dlower[li] = ggn[0], glb[0]
        dot, delta = _attn_bwd_prep(dyb, s["oh"], s["z"], tm)
        SH = D // N_DEV
        sent = [] if first else [("w_in", li + 1, SH)]
        sent += [("w_up", li, NBR * W), ("w_o", li, SH), ("w_ple_gate", li, SH), ("w_ple_proj", li, PLE)]
        dqh, dkh, dvh, dck, dcq, *lands = _attn_bwd(
            s["qt"], s["kt"], s["kh"], s["vh"], s["crow"], s["ccol"], dot, s["lse"], delta, tm,
            [(gw[name], layer * rows, rows) for name, layer, rows in sent])
        for (name, layer, _), land in zip(sent, lands):
            landed[name][layer] = land
        dcq_t = jnp.transpose(dcq[:, :, 0:HP, :], (0, 2, 1, 3)).reshape(NH, T)
        dcq_t = jnp.pad(jnp.transpose(dcq_t), ((0, 0), (0, 128 - NH)))
        dz, ggq, ggk, gfb = _attn_post(s["z"], dyb, s["oh"], dqh, dkh, dvh, dck, dcq_t, row(fb_pad),
                                       row(gq_t), row(gk_t), dz, tmm)
        gs["q_norm"][li], gs["k_norm"][li], gs["fgate_bias"][li] = ggq[0, 0:DH], ggk[0, 0:DH], gfb[0, 0:NH]
        dx, gnm = _inproj_bwd_x(dz, win_f[li], s["x"], dx1, row(norm_mix), tmi)
        gs["norm_mix"][li] = gnm[0]
        gw["w_in"] = _inproj_bwd_w(s["h"], dz, tmi, li, gw["w_in"])
    dlower8 = jnp.pad(jnp.stack(dlower), ((0, 8 - DEPTH), (0, 0)))
    gs_full = {n: jnp.stack(v) for n, v in gs.items() if n != "lb_logits"}
    gs_full["lb_logits"] = _lower_bounds_bwd(lb_logits, dlower8)[0:DEPTH]
    gw["landed"] = landed
    return loss_local, dx, gw, gs_full


def _exchange_and_update(loss, grad_x, gw, gs_full, weights, ms, vs):
    SH = D // N_DEV
    CW = W // N_DEV

    small_buf = _pack_small(gs_full)
    landed = gw["landed"]
    rest = [li for li in range(DEPTH) if li not in landed["w_in"]]
    win_rest = jnp.concatenate([gw["w_in"][:, li * SH:(li + 1) * SH] for li in rest], axis=1)
    own = [win_rest]
    from_sibling = _exchange_sibling(own)
    chip_sums = [_pair_sum(o, r) for o, r in zip(own, from_sibling)]
    l_win, l_small = _exchange_chips(chip_sums, [small_buf])

    g_rest = _sum_slabs(l_win)
    g_layers = {li: g_rest[n * SH:(n + 1) * SH] for n, li in enumerate(rest)}
    g_layers.update({li: _sum_slabs(land) for li, land in landed["w_in"].items()})
    g_w_in = _unpermute_cols(jnp.stack([g_layers[li] for li in range(DEPTH)]))

    def landed_sum(name):
        return _sum_slabs(jnp.concatenate([landed[name][li] for li in range(DEPTH)], axis=1))

    g_w_up = landed_sum("w_up").reshape(DEPTH, NBR, W, SH)
    g_w_o = landed_sum("w_o").reshape(DEPTH, SH, D)
    g_w_pg = landed_sum("w_ple_gate").reshape(DEPTH, SH, D)
    g_w_pp = landed_sum("w_ple_proj").reshape(DEPTH, PLE, SH)
    g_small = _unpack_small(_sum_slabs(l_small))
    g_small_local = dict(g_small)
    g_small_local["conv_w"] = _shard_cols(g_small["conv_w"], CW)
    g_small_local["merge_b"] = _shard_cols(g_small["merge_b"], SH)

    grads = dict(w_in=g_w_in, w_up=g_w_up, w_o=g_w_o, w_ple_gate=g_w_pg, w_ple_proj=g_w_pp)
    deltas, new_m, new_v = {}, {}, {}
    for name in ("w_in", "w_up", "w_o", "w_ple_gate", "w_ple_proj"):
        shape = weights[name].shape
        as3 = (shape[0], -1, shape[-1])
        d_, m_, v_ = _adamw(weights[name].reshape(as3), grads[name].reshape(as3),
                            ms[name].reshape(as3), vs[name].reshape(as3))
        deltas[name], new_m[name], new_v[name] = d_.reshape(shape), m_.reshape(shape), v_.reshape(shape)

    def local_shapes(parts):
        return {n: (parts[n] if parts[n].shape == s else jnp.pad(
            parts[n], [(0, 0)] * (len(s) - 1) + [(0, s[-1] - parts[n].shape[-1])])) for n, s in _SMALL}

    d_, m_, v_ = _adamw(_pack_small(local_shapes(weights)), _pack_small(local_shapes(g_small_local)),
                        _pack_small(local_shapes(ms)), _pack_small(local_shapes(vs)))
    for buf, dst in ((d_, deltas), (m_, new_m), (v_, new_v)):
        parts = _unpack_small(buf)
        for n, _ in _SMALL:
            dst[n] = parts[n][..., :weights[n].shape[-1]]
    for n, _ in _SMALL:
        grads[n] = g_small_local[n]

    order = ["norm_mix", "w_in", "conv_w", "conv_b", "fgate_bias", "q_norm", "k_norm", "lb_logits", "hgrn_norm",
             "sgu_norm", "spatial_w", "spatial_b", "w_up", "merge_b", "w_o", "norm_ple", "w_ple_gate", "w_ple_proj"]
    return (loss, grad_x, *[grads[n] for n in order], *[deltas[n] for n in order],
            *[new_m[n] for n in order], *[new_v[n] for n in order])
```

```python
import functools

import jax
import jax.numpy as jnp
from jax import lax
from jax.experimental import pallas as pl
from jax.experimental.pallas import tpu as pltpu

f32 = jnp.float32
bf16 = jnp.bfloat16

D = 1024
W = 256
NH = 4
DH = 64
NBR = 4
PLE = 256
DEPTH = 4
CONV_WIDTH = 3
SGU_CHUNK = 128
GLA_CHUNK = 128
EPS = 1e-6
MASK_VALUE = -1e30
IN_COLS = 7940
NZ = 8064
OFF_CONV = 4096
OFF_HGRN = 5120
OFF_SGU = 6144
OFF_ATT = 6912
OFF_F = 7936
ZT = 1152
NZT = NZ // ZT
EXP_CLAMP = 80.0
LOG2E = 1.4426950408889634

ADAM_LR = 0.001
ADAM_B1 = 0.9
ADAM_B2 = 0.999
ADAM_EPS = 1e-08
ADAM_WD = 0.01
ADAM_STEP = 10

N_DEV = 8
AXES = ("x", "y", "c")
VMEM_LIMIT = 56 * 1024 * 1024
HI = lax.Precision.HIGHEST

NT_DIMS = (((1,), (1,)), ((), ()))
TN_DIMS = (((0,), (0,)), ((), ()))


def _pcall(body, **kw):
    return pl.pallas_call(body, **kw)


def _params(*sem):
    return pltpu.CompilerParams(dimension_semantics=sem, vmem_limit_bytes=VMEM_LIMIT)


def _mm(a, b):
    return jnp.dot(a, b, preferred_element_type=f32)


def _mm_nt(a, b):
    return lax.dot_general(a, b, NT_DIMS, preferred_element_type=f32)


def _mm_tn(a, b):
    return lax.dot_general(a, b, TN_DIMS, preferred_element_type=f32)


def _sigmoid(x):
    return 1.0 / (1.0 + jnp.exp(-x))


def _silu(x):
    return x * _sigmoid(x)


def _dsilu(x):
    s = _sigmoid(x)
    return s * (1.0 + x * (1.0 - s))


def _logsigmoid(x):
    return jnp.minimum(x, 0.0) - jnp.log(1.0 + jnp.exp(-jnp.abs(x)))


def _iota2(shape, axis):
    return lax.broadcasted_iota(jnp.int32, shape, axis)


def _group_mean_matrix(n, group):
    shift = group.bit_length() - 1
    r = lax.shift_right_logical(_iota2((n, n), 0), shift)
    c = lax.shift_right_logical(_iota2((n, n), 1), shift)
    return jnp.where(r == c, 1.0 / group, 0.0).astype(f32)


def _group_mean(x, gm):
    return jnp.dot(x, gm, precision=HI, preferred_element_type=f32)


def _lower_tri(n):
    return jnp.where(_iota2((n, n), 0) >= _iota2((n, n), 1), 1.0, 0.0).astype(f32)


def _upper_tri(n):
    return jnp.where(_iota2((n, n), 0) <= _iota2((n, n), 1), 1.0, 0.0).astype(f32)


def _rows3(r0, r1, r2, width):
    row = _iota2((8, width), 0)
    return jnp.where(row == 0, r0, jnp.where(row == 1, r1, jnp.where(row == 2, r2, 0.0)))


def _inproj_fwd(x, g, w, tm, gather=()):
    T = x.shape[0]
    n = len(gather)
    axes = [0] * n
    steps = (T // tm) * NZT

    def body(x_ref, g_ref, w_ref, *rest):
        z_ref, h_ref = rest[n:n + 2]

        @pl.when(pl.program_id(1) == 0)
        def _():
            xv = x_ref[...]
            r = lax.rsqrt(jnp.mean(xv * xv, axis=-1, keepdims=True) + EPS)
            h_ref[...] = (xv * r * g_ref[...]).astype(bf16)

        if n:
            start, forward, finish = _gather_phases(gather, axes, rest[:n], rest[n + 2:2 * n + 2], *rest[2 * n + 2:])
            step = pl.program_id(0) * NZT + pl.program_id(1)
            pl.when(step == 0)(start)
            pl.when(step == steps // 2)(forward)

        z_ref[...] = _mm(h_ref[...], w_ref[...])

        if n:
            pl.when(step == steps - 1)(finish)

    hbm = pl.BlockSpec(memory_space=pltpu.HBM)
    return _pcall(
        body, name="inproj_fwd_gather" if n else "inproj_fwd", grid=(T // tm, NZT),
        in_specs=[pl.BlockSpec((tm, D), lambda i, j: (i, 0)),
                  pl.BlockSpec((1, D), lambda i, j: (0, 0)),
                  pl.BlockSpec((D, ZT), lambda i, j: (0, j))] + [hbm] * n,
        out_specs=[pl.BlockSpec((tm, ZT), lambda i, j: (i, j)),
                   pl.BlockSpec((tm, D), lambda i, j: (i, 0))] + [hbm] * n,
        out_shape=[jax.ShapeDtypeStruct((T, NZ), f32), jax.ShapeDtypeStruct((T, D), bf16)]
        + _gathered_shapes(gather, axes),
        scratch_shapes=_gather_semaphores(n) if n else [],
        compiler_params=_params("arbitrary" if n else "parallel", "arbitrary"),
    )(x, g, w, *gather)


def _inproj_bwd_x(dz, w, x, dx1, g, tm):
    T = x.shape[0]

    def body(dz_ref, w_ref, x_ref, dx1_ref, g_ref, dx_ref, gg_ref, acc):
        i, k = pl.program_id(0), pl.program_id(1)

        @pl.when(k == 0)
        def _():
            acc[...] = jnp.zeros_like(acc)

        @pl.when((i == 0) & (k == 0))
        def _():
            gg_ref[...] = jnp.zeros_like(gg_ref)

        acc[...] += _mm_nt(dz_ref[...], w_ref[...])

        @pl.when(k == NZT - 1)
        def _():
            xv = x_ref[...]
            r = lax.rsqrt(jnp.mean(xv * xv, axis=-1, keepdims=True) + EPS)
            dh = acc[...]
            gg_ref[...] += jnp.sum(dh * xv * r, axis=0, keepdims=True)
            u = dh * g_ref[...]
            dx_ref[...] = dx1_ref[...] + r * u - xv * (r * r * r) * jnp.mean(u * xv, axis=-1, keepdims=True)

    return _pcall(
        body, name="inproj_bwd_x", grid=(T // tm, NZT),
        in_specs=[pl.BlockSpec((tm, ZT), lambda i, k: (i, k)),
                  pl.BlockSpec((D, ZT), lambda i, k: (0, k)),
                  pl.BlockSpec((tm, D), lambda i, k: (i, 0)),
                  pl.BlockSpec((tm, D), lambda i, k: (i, 0)),
                  pl.BlockSpec((1, D), lambda i, k: (0, 0))],
        out_specs=[pl.BlockSpec((tm, D), lambda i, k: (i, 0)),
                   pl.BlockSpec((1, D), lambda i, k: (0, 0))],
        out_shape=[jax.ShapeDtypeStruct((T, D), f32), jax.ShapeDtypeStruct((1, D), f32)],
        scratch_shapes=[pltpu.VMEM((tm, D), f32)],
        compiler_params=_params("arbitrary", "arbitrary"),
    )(dz, w, x, dx1, g)


def _inproj_bwd_w(h, dz, tm, li, buf):
    T = h.shape[0]
    SH = D // N_DEV
    nt = T // tm
    extra = [] if buf is None else [buf]

    def body(h_ref, dz_ref, *rest):
        gw_ref, acc = rest[len(extra):]

        @pl.when(pl.program_id(1) == 0)
        def _():
            acc[...] = jnp.zeros_like(acc)

        acc[...] += _mm_tn(h_ref[...], dz_ref[...])

        @pl.when(pl.program_id(1) == nt - 1)
        def _():
            gw_ref[...] = acc[...].reshape(N_DEV, SH, ZT).astype(bf16)

    return _pcall(
        body, name="inproj_bwd_w", grid=(NZT, nt),
        in_specs=[pl.BlockSpec((tm, D), lambda j, i: (i, 0)),
                  pl.BlockSpec((tm, ZT), lambda j, i: (i, j))] + [pl.BlockSpec(memory_space=pl.ANY)] * len(extra),
        out_specs=pl.BlockSpec((N_DEV, SH, ZT), lambda j, i: (0, li, j)),
        out_shape=jax.ShapeDtypeStruct((N_DEV, DEPTH * SH, NZ), bf16),
        scratch_shapes=[pltpu.VMEM((D, ZT), f32)],
        input_output_aliases={2: 0} if extra else {},
        compiler_params=_params("parallel", "arbitrary"),
    )(h, dz, *extra)


def _zblock(tm, col256):
    return pl.BlockSpec((tm, W), lambda i, c=col256: (i, c))


def _conv_taps(zc, halo, cw_ref, n):
    ext = jnp.concatenate([halo, zc], axis=0)
    z1 = pltpu.roll(ext, 1, 0)[8:]
    z2 = pltpu.roll(ext, 2, 0)[8:]
    return z1, z2


def _conv_fwd(z, cw, cb, tm):
    T = z.shape[0]
    c0 = OFF_CONV // W
    hb = tm // 8

    def body(ax_ref, ab_ref, ac_ref, ag_ref, hx_ref, hc_ref, cw_ref, cb_ref, y_ref):
        i = pl.program_id(0)
        zc = ac_ref[...] * ax_ref[...]
        halo = jnp.where(i > 0, hc_ref[...] * hx_ref[...], 0.0)
        z1, z2 = _conv_taps(zc, halo, cw_ref, tm)
        y = cw_ref[2:3, :] * zc + cw_ref[1:2, :] * z1 + cw_ref[0:1, :] * z2
        ya = ab_ref[...] * (y + cb_ref[...])
        y_ref[...] = (ya * _silu(ag_ref[...])).astype(bf16)

    halo_spec = lambda col: pl.BlockSpec((8, W), lambda i, c=col: (jnp.maximum(i * hb - 1, 0), c))
    return _pcall(
        body, name="conv_fwd", grid=(T // tm,),
        in_specs=[_zblock(tm, c0), _zblock(tm, c0 + 1), _zblock(tm, c0 + 2), _zblock(tm, c0 + 3),
                  halo_spec(c0), halo_spec(c0 + 2),
                  pl.BlockSpec((CONV_WIDTH, W), lambda i: (0, 0)),
                  pl.BlockSpec((1, W), lambda i: (0, 0))],
        out_specs=pl.BlockSpec((tm, W), lambda i: (i, 0)),
        out_shape=jax.ShapeDtypeStruct((T, W), bf16),
        compiler_params=_params("parallel"),
    )(z, z, z, z, z, z, cw, cb)


def _conv_bwd(z, dy, cw, cb, dzbuf, tm):
    T = z.shape[0]
    c0 = OFF_CONV // W
    hb = tm // 8
    nt = T // tm

    def body(ax_ref, ab_ref, ac_ref, ag_ref, hx_ref, hc_ref, nb_ref, ng_ref, dy_ref, ndy_ref,
             cw_ref, cb_ref, dzin_ref, dz_ref, gcw_ref, gcb_ref):
        i = pl.program_id(0)

        @pl.when(i == 0)
        def _():
            gcw_ref[...] = jnp.zeros_like(gcw_ref)
            gcb_ref[...] = jnp.zeros_like(gcb_ref)

        ax, ab, ac, ag = ax_ref[...], ab_ref[...], ac_ref[...], ag_ref[...]
        w0, w1, w2 = cw_ref[0:1, :], cw_ref[1:2, :], cw_ref[2:3, :]
        zc = ac * ax
        halo = jnp.where(i > 0, hc_ref[...] * hx_ref[...], 0.0)
        z1, z2 = _conv_taps(zc, halo, cw_ref, tm)
        yb = w2 * zc + w1 * z1 + w0 * z2 + cb_ref[...]
        ya = ab * yb
        dyg = dy_ref[...]
        dag = dyg * ya * _dsilu(ag)
        dya = dyg * _silu(ag)
        dab = dya * yb
        dyc = dya * ab
        nxt = jnp.where(i < nt - 1, ndy_ref[...] * _silu(ng_ref[...]) * nb_ref[...], 0.0)
        ext = jnp.concatenate([dyc, nxt], axis=0)
        d1 = pltpu.roll(ext, tm + 8 - 1, 0)[:tm]
        d2 = pltpu.roll(ext, tm + 8 - 2, 0)[:tm]
        dzc = w2 * dyc + w1 * d1 + w0 * d2
        dz_ref[:, 0:W] = (dzc * ac).astype(bf16)
        dz_ref[:, W:2 * W] = dab.astype(bf16)
        dz_ref[:, 2 * W:3 * W] = (dzc * ax).astype(bf16)
        dz_ref[:, 3 * W:4 * W] = dag.astype(bf16)
        gcb_ref[...] += jnp.sum(dyc, axis=0, keepdims=True)
        gcw_ref[...] += _rows3(jnp.sum(dyc * z2, axis=0, keepdims=True),
                               jnp.sum(dyc * z1, axis=0, keepdims=True),
                               jnp.sum(dyc * zc, axis=0, keepdims=True), W)

    prev_spec = lambda col: pl.BlockSpec((8, W), lambda i, c=col: (jnp.maximum(i * hb - 1, 0), c))
    next_z = lambda col: pl.BlockSpec((8, W), lambda i, c=col: (jnp.minimum((i + 1) * hb, T // 8 - 1), c))
    next_dy = pl.BlockSpec((8, W), lambda i: (jnp.minimum((i + 1) * hb, T // 8 - 1), 0))
    return _pcall(
        body, name="conv_bwd", grid=(nt,),
        in_specs=[_zblock(tm, c0), _zblock(tm, c0 + 1), _zblock(tm, c0 + 2), _zblock(tm, c0 + 3),
                  prev_spec(c0), prev_spec(c0 + 2), next_z(c0 + 1), next_z(c0 + 3),
                  pl.BlockSpec((tm, W), lambda i: (i, 0)), next_dy,
                  pl.BlockSpec((CONV_WIDTH, W), lambda i: (0, 0)),
                  pl.BlockSpec((1, W), lambda i: (0, 0)),
                  pl.BlockSpec(memory_space=pl.ANY)],
        out_specs=[pl.BlockSpec((tm, 4 * W), lambda i: (i, OFF_CONV // (4 * W))),
                   pl.BlockSpec((8, W), lambda i: (0, 0)),
                   pl.BlockSpec((1, W), lambda i: (0, 0))],
        out_shape=[jax.ShapeDtypeStruct((T, NZ), bf16), jax.ShapeDtypeStruct((8, W), f32),
                   jax.ShapeDtypeStruct((1, W), f32)],
        input_output_aliases={12: 0},
        compiler_params=_params("arbitrary"),
    )(z, z, z, z, z, z, z, z, dy, dy, cw, cb, dzbuf)


def _sgu_core(dv_ref, gv_ref, sw_ref, sbe_ref, s_scr, tm):
    v = dv_ref[...]
    gm = _group_mean_matrix(W, DH)
    rv = lax.rsqrt(_group_mean(v * v, gm) + EPS)
    vh = v * rv
    vnb = (vh * gv_ref[...]).astype(bf16)
    causal = _iota2((SGU_CHUNK, SGU_CHUNK), 0) >= _iota2((SGU_CHUNK, SGU_CHUNK), 1)
    wgs = [jnp.where(causal, sw_ref[g], 0.0).astype(bf16) for g in range(NH)]
    for c in range(tm // SGU_CHUNK):
        rows = slice(c * SGU_CHUNK, (c + 1) * SGU_CHUNK)
        for g in range(NH):
            cols = slice(g * DH, (g + 1) * DH)
            s_scr[rows, cols] = _mm(wgs[g], vnb[rows, cols])
    sb = sbe_ref[...]
    s = s_scr[...] + jnp.concatenate([sb] * (tm // SGU_CHUNK), axis=0)
    return v, rv, vh, vnb, wgs, causal, gm, s


def _sgu_fwd(z, gv, sw, sbe, tm):
    T = z.shape[0]
    c0 = OFF_SGU // W

    def body(du_ref, dv_ref, dg_ref, gv_ref, sw_ref, sbe_ref, y_ref, s_scr):
        s = _sgu_core(dv_ref, gv_ref, sw_ref, sbe_ref, s_scr, tm)[-1]
        y_ref[...] = ((du_ref[...] * s) * _silu(dg_ref[...])).astype(bf16)

    return _pcall(
        body, name="sgu_fwd", grid=(T // tm,),
        in_specs=[_zblock(tm, c0), _zblock(tm, c0 + 1), _zblock(tm, c0 + 2),
                  pl.BlockSpec((1, W), lambda i: (0, 0)),
                  pl.BlockSpec((NH, SGU_CHUNK, SGU_CHUNK), lambda i: (0, 0, 0)),
                  pl.BlockSpec((SGU_CHUNK, W), lambda i: (0, 0))],
        out_specs=pl.BlockSpec((tm, W), lambda i: (i, 0)),
        out_shape=jax.ShapeDtypeStruct((T, W), bf16),
        scratch_shapes=[pltpu.VMEM((tm, W), f32)],
        compiler_params=_params("parallel"),
    )(z, z, z, gv, sw, sbe)


def _sgu_bwd(z, dy, gv, sw, sbe, dzbuf, tm):
    T = z.shape[0]
    c0 = OFF_SGU // W
    nt = T // tm

    def body(du_ref, dv_ref, dg_ref, dy_ref, gv_ref, sw_ref, sbe_ref, dzin_ref,
             dz_ref, gsw_ref, gsb_ref, ggv_ref, s_scr, dvn_scr, sb_acc):
        i = pl.program_id(0)

        @pl.when(i == 0)
        def _():
            gsw_ref[...] = jnp.zeros_like(gsw_ref)
            ggv_ref[...] = jnp.zeros_like(ggv_ref)
            sb_acc[...] = jnp.zeros_like(sb_acc)

        v, rv, vh, vnb, wgs, causal, gm, s = _sgu_core(dv_ref, gv_ref, sw_ref, sbe_ref, s_scr, tm)
        du, dg, dyv = du_ref[...], dg_ref[...], dy_ref[...]
        ddg = dyv * (du * s) * _dsilu(dg)
        t = dyv * _silu(dg)
        ddu = t * s
        ds = t * du
        dsb = ds.astype(bf16)
        acc = sb_acc[...]
        for c in range(tm // SGU_CHUNK):
            rows = slice(c * SGU_CHUNK, (c + 1) * SGU_CHUNK)
            acc = acc + ds[rows, :]
            for g in range(NH):
                cols = slice(g * DH, (g + 1) * DH)
                gsw_ref[g] += jnp.where(causal, _mm_nt(dsb[rows, cols], vnb[rows, cols]), 0.0)
                dvn_scr[rows, cols] = _mm_tn(wgs[g], dsb[rows, cols])
        sb_acc[...] = acc
        dvn = dvn_scr[...]
        ggv_ref[...] += jnp.sum(dvn * vh, axis=0, keepdims=True)
        u = dvn * gv_ref[...]
        ddv = rv * u - v * (rv * rv * rv) * _group_mean(u * v, gm)
        dz_ref[:, 0:W] = ddu.astype(bf16)
        dz_ref[:, W:2 * W] = ddv.astype(bf16)
        dz_ref[:, 2 * W:3 * W] = ddg.astype(bf16)

        @pl.when(i == nt - 1)
        def _():
            gsb_ref[...] = _group_mean(sb_acc[...], gm) * float(DH)

    return _pcall(
        body, name="sgu_bwd", grid=(nt,),
        in_specs=[_zblock(tm, c0), _zblock(tm, c0 + 1), _zblock(tm, c0 + 2),
                  pl.BlockSpec((tm, W), lambda i: (i, 0)),
                  pl.BlockSpec((1, W), lambda i: (0, 0)),
                  pl.BlockSpec((NH, SGU_CHUNK, SGU_CHUNK), lambda i: (0, 0, 0)),
                  pl.BlockSpec((SGU_CHUNK, W), lambda i: (0, 0)),
                  pl.BlockSpec(memory_space=pl.ANY)],
        out_specs=[pl.BlockSpec((tm, 3 * W), lambda i: (i, OFF_SGU // (3 * W))),
                   pl.BlockSpec((NH, SGU_CHUNK, SGU_CHUNK), lambda i: (0, 0, 0)),
                   pl.BlockSpec((SGU_CHUNK, W), lambda i: (0, 0)),
                   pl.BlockSpec((1, W), lambda i: (0, 0))],
        out_shape=[jax.ShapeDtypeStruct((T, NZ), bf16),
                   jax.ShapeDtypeStruct((NH, SGU_CHUNK, SGU_CHUNK), f32),
                   jax.ShapeDtypeStruct((SGU_CHUNK, W), f32),
                   jax.ShapeDtypeStruct((1, W), f32)],
        scratch_shapes=[pltpu.VMEM((tm, W), f32), pltpu.VMEM((tm, W), f32), pltpu.VMEM((SGU_CHUNK, W), f32)],
        input_output_aliases={7: 0},
        compiler_params=_params("arbitrary"),
    )(z, z, z, dy, gv, sw, sbe, dzbuf)


def _hgrn_gates(cq_ref, cf_ref, lb_ref):
    q = _silu(cq_ref[...])
    sig = _sigmoid(cf_ref[...])
    lb = lb_ref[...]
    g = lb + (1.0 - lb) * sig
    return q, sig, g, jnp.log(g), (1.0 - lb) * (1.0 - sig)


def _hgrn_chunk_terms(lgc, qc, kc):
    C = GLA_CHUNK
    b = jnp.dot(_lower_tri(C), lgc, precision=HI, preferred_element_type=f32)
    bl = jnp.sum(lgc, axis=0, keepdims=True)
    mid = jnp.sum(jnp.where(_iota2((C, W), 0) <= C // 2, lgc, 0.0), axis=0, keepdims=True)
    eb = jnp.exp(b)
    em = jnp.exp(jnp.minimum(b - mid, EXP_CLAMP))
    emi = jnp.exp(jnp.minimum(mid - b, EXP_CLAMP))
    ek = jnp.exp(bl - b)
    return dict(eb=eb, em=em, emi=emi, ek=ek, ebl=jnp.exp(bl),
                qe=qc * eb, qm=qc * em, km=kc * emi, kd=kc * ek)


def _hgrn_fwd(z, lb, gain, tm):
    T = z.shape[0]
    c0 = OFF_HGRN // W
    C = GLA_CHUNK
    ncp = tm // C

    def body(cq_ref, cf_ref, ci_ref, cg_ref, lb_ref, gn_ref, y_ref, o_ref, st_ref, state, o_scr):
        @pl.when(pl.program_id(0) == 0)
        def _():
            state[...] = jnp.zeros_like(state)

        q, sig, g, lg, kf = _hgrn_gates(cq_ref, cf_ref, lb_ref)
        v = ci_ref[...]
        causal = _iota2((C, C), 0) >= _iota2((C, C), 1)
        for c in range(ncp):
            rows = slice(c * C, (c + 1) * C)
            tr = _hgrn_chunk_terms(lg[rows], q[rows], kf[rows])
            vb = v[rows].astype(bf16)
            qmb, kmb, qeb, kdb = (tr[n].astype(bf16) for n in ("qm", "km", "qe", "kd"))
            for h in range(NH):
                cols = slice(h * DH, (h + 1) * DH)
                hr = slice(h * DH, (h + 1) * DH)
                st = state[hr, :]
                st_ref[c, hr, :] = st
                p = jnp.where(causal, _mm_nt(qmb[:, cols], kmb[:, cols]), 0.0)
                o_scr[rows, cols] = _mm(p.astype(bf16), vb[:, cols]) + _mm_nt(qeb[:, cols], st.astype(bf16))
                state[hr, :] = st * tr["ebl"][:, cols] + _mm_tn(vb[:, cols], kdb[:, cols])
        o = o_scr[...]
        o_ref[...] = o
        gm = _group_mean_matrix(W, DH)
        r = lax.rsqrt(_group_mean(o * o, gm) + EPS)
        y_ref[...] = ((o * r * gn_ref[...]) * _silu(cg_ref[...])).astype(bf16)

    return _pcall(
        body, name="hgrn_fwd", grid=(T // tm,),
        in_specs=[_zblock(tm, c0), _zblock(tm, c0 + 1), _zblock(tm, c0 + 2), _zblock(tm, c0 + 3),
                  pl.BlockSpec((1, W), lambda i: (0, 0)), pl.BlockSpec((1, W), lambda i: (0, 0))],
        out_specs=[pl.BlockSpec((tm, W), lambda i: (i, 0)),
                   pl.BlockSpec((tm, W), lambda i: (i, 0)),
                   pl.BlockSpec((ncp, W, DH), lambda i: (i, 0, 0))],
        out_shape=[jax.ShapeDtypeStruct((T, W), bf16), jax.ShapeDtypeStruct((T, W), f32),
                   jax.ShapeDtypeStruct((T // C, W, DH), f32)],
        scratch_shapes=[pltpu.VMEM((W, DH), f32), pltpu.VMEM((tm, W), f32)],
        compiler_params=_params("arbitrary"),
    )(z, z, z, z, lb, gain)


def _hgrn_bwd(z, lb, gain, o_pre, states, dy, dzbuf, tm):
    T = z.shape[0]
    c0 = OFF_HGRN // W
    C = GLA_CHUNK
    ncp = tm // C
    nt = T // tm

    def body(cq_ref, cf_ref, ci_ref, cg_ref, lb_ref, gn_ref, o_ref, st_ref, dy_ref, dzin_ref,
             dz_ref, ggn_ref, glb_ref, dstate, dq_s, dk_s, dv_s, db_s):
        @pl.when(pl.program_id(0) == 0)
        def _():
            dstate[...] = jnp.zeros_like(dstate)
            ggn_ref[...] = jnp.zeros_like(ggn_ref)
            glb_ref[...] = jnp.zeros_like(glb_ref)

        cq, cg = cq_ref[...], cg_ref[...]
        q, sig, g, lg, kf = _hgrn_gates(cq_ref, cf_ref, lb_ref)
        lb = lb_ref[...]
        v = ci_ref[...]
        o = o_ref[...]
        gm = _group_mean_matrix(W, DH)
        r = lax.rsqrt(_group_mean(o * o, gm) + EPS)
        oh = o * r
        gn = gn_ref[...]
        dyv = dy_ref[...]
        dcg = dyv * (oh * gn) * _dsilu(cg)
        don = dyv * _silu(cg)
        ggn_ref[...] += jnp.sum(don * oh, axis=0, keepdims=True)
        u = don * gn
        do = r * u - o * (r * r * r) * _group_mean(u * o, gm)

        causal = _iota2((C, C), 0) >= _iota2((C, C), 1)
        last_row = _iota2((C, DH), 0) == C - 1
        for c in reversed(range(ncp)):
            rows = slice(c * C, (c + 1) * C)
            tr = _hgrn_chunk_terms(lg[rows], q[rows], kf[rows])
            vb = v[rows].astype(bf16)
            dob = do[rows].astype(bf16)
            qmb, kmb, qeb, kdb = (tr[n].astype(bf16) for n in ("qm", "km", "qe", "kd"))
            for h in range(NH):
                cols = slice(h * DH, (h + 1) * DH)
                hr = slice(h * DH, (h + 1) * DH)
                st0 = st_ref[c, hr, :]
                dst = dstate[hr, :]
                dstb = dst.astype(bf16)
                doh = dob[:, cols]
                p = jnp.where(causal, _mm_nt(qmb[:, cols], kmb[:, cols]), 0.0)
                dp = jnp.where(causal, _mm_nt(doh, vb[:, cols]), 0.0)
                dpb = dp.astype(bf16)
                dvh = _mm_tn(p.astype(bf16), doh) + _mm_nt(kdb[:, cols], dstb)
                dqm = _mm(dpb, kmb[:, cols])
                dkm = _mm_tn(dpb, qmb[:, cols])
                dqe = _mm(doh, st0.astype(bf16))
                dkd = _mm(vb[:, cols], dstb)
                ebl = tr["ebl"][:, cols]
                dstate[hr, :] = dst * ebl + _mm_tn(doh, qeb[:, cols])
                qm, km, qe, kd = (a[:, cols].astype(f32) for a in (qmb, kmb, qeb, kdb))
                kterm = dkd * kd
                dbh = dqm * qm - dkm * km + dqe * qe - kterm
                extra = jnp.sum(kterm, axis=0, keepdims=True) + ebl * jnp.sum(dst * st0, axis=0, keepdims=True)
                dbh = dbh + jnp.where(last_row, extra, 0.0)
                dq_s[rows, cols] = dqm * tr["em"][:, cols] + dqe * tr["eb"][:, cols]
                dk_s[rows, cols] = dkm * tr["emi"][:, cols] + dkd * tr["ek"][:, cols]
                dv_s[rows, cols] = dvh
                db_s[rows, cols] = dbh
            db_s[rows, :] = jnp.dot(_upper_tri(C), db_s[rows, :], precision=HI, preferred_element_type=f32)
        dlg = db_s[...]
        dk = dk_s[...]
        dsig = sig * (1.0 - sig)
        one_lb = 1.0 - lb
        dcf = (dlg / g - dk) * one_lb * dsig
        glb_ref[...] += jnp.sum((dlg / g - dk) * (1.0 - sig), axis=0, keepdims=True)
        dz_ref[:, 0:W] = (dq_s[...] * _dsilu(cq)).astype(bf16)
        dz_ref[:, W:2 * W] = dcf.astype(bf16)
        dz_ref[:, 2 * W:3 * W] = dv_s[...].astype(bf16)
        dz_ref[:, 3 * W:4 * W] = dcg.astype(bf16)

    rev = lambda i: nt - 1 - i
    zb = lambda col: pl.BlockSpec((tm, W), lambda i, c=col: (rev(i), c))
    return _pcall(
        body, name="hgrn_bwd", grid=(nt,),
        in_specs=[zb(c0), zb(c0 + 1), zb(c0 + 2), zb(c0 + 3),
                  pl.BlockSpec((1, W), lambda i: (0, 0)), pl.BlockSpec((1, W), lambda i: (0, 0)),
                  pl.BlockSpec((tm, W), lambda i: (rev(i), 0)),
                  pl.BlockSpec((ncp, W, DH), lambda i: (rev(i), 0, 0)),
                  pl.BlockSpec((tm, W), lambda i: (rev(i), 0)),
                  pl.BlockSpec(memory_space=pl.ANY)],
        out_specs=[pl.BlockSpec((tm, 4 * W), lambda i: (rev(i), OFF_HGRN // (4 * W))),
                   pl.BlockSpec((1, W), lambda i: (0, 0)),
                   pl.BlockSpec((1, W), lambda i: (0, 0))],
        out_shape=[jax.ShapeDtypeStruct((T, NZ), bf16), jax.ShapeDtypeStruct((1, W), f32),
                   jax.ShapeDtypeStruct((1, W), f32)],
        scratch_shapes=[pltpu.VMEM((W, DH), f32)] + [pltpu.VMEM((tm, W), f32)] * 4,
        input_output_aliases={9: 0},
        compiler_params=_params("arbitrary"),
    )(z, z, z, z, lb, gain, o_pre, states, dy, dzbuf)


def _attn_prep(z, fbias, gq, gk, tm):
    T = z.shape[0]
    c0 = OFF_ATT // W

    def body(q_ref, k_ref, v_ref, f_ref, fb_ref, gq_ref, gk_ref, qt_ref, kt_ref, vt_ref, kh_ref, vh_ref, cum_ref,
             carry):
        @pl.when(pl.program_id(0) == 0)
        def _():
            carry[...] = jnp.zeros_like(carry)

        gm = _group_mean_matrix(W, DH)
        q, k, v = q_ref[...], k_ref[...], v_ref[...]
        qs = q * lax.rsqrt(_group_mean(q * q, gm) + EPS) * (gq_ref[...] * (DH ** -0.5 * LOG2E))
        kn = k * lax.rsqrt(_group_mean(k * k, gm) + EPS) * gk_ref[...]
        qt_ref[...] = qs.T.astype(bf16)
        kt_ref[...] = kn.T.astype(bf16)
        vt_ref[...] = v.T.astype(bf16)
        for h in range(NH):
            cols = slice(h * DH, (h + 1) * DH)
            kh_ref[h] = kn[:, cols].astype(bf16)
            vh_ref[h] = v[:, cols].astype(bf16)
        ls = _logsigmoid(f_ref[...] + fb_ref[...])
        cum = jnp.dot(_lower_tri(tm), ls, precision=HI, preferred_element_type=f32) + carry[...]
        cum_ref[...] = cum * LOG2E
        carry[...] += jnp.sum(ls, axis=0, keepdims=True)

    hspec = pl.BlockSpec((NH, tm, DH), lambda i: (0, i, 0))
    tspec = pl.BlockSpec((W, tm), lambda i: (0, i))
    return _pcall(
        body, name="attn_prep", grid=(T // tm,),
        in_specs=[_zblock(tm, c0), _zblock(tm, c0 + 1), _zblock(tm, c0 + 2),
                  pl.BlockSpec((tm, 128), lambda i: (i, OFF_F // 128)),
                  pl.BlockSpec((1, 128), lambda i: (0, 0)),
                  pl.BlockSpec((1, W), lambda i: (0, 0)), pl.BlockSpec((1, W), lambda i: (0, 0))],
        out_specs=[tspec, tspec, tspec, hspec, hspec, pl.BlockSpec((tm, 128), lambda i: (i, 0))],
        out_shape=[jax.ShapeDtypeStruct((W, T), bf16)] * 3 + [jax.ShapeDtypeStruct((NH, T, DH), bf16)] * 2
        + [jax.ShapeDtypeStruct((T, 128), f32)],
        scratch_shapes=[pltpu.VMEM((1, 128), f32)],
        compiler_params=_params("arbitrary"),
    )(z, z, z, z, fbias, gq, gk)


HP = 2


def _causal_pairs(nq, key_major):
    if key_major:
        pairs = [(qi, ki) for ki in range(nq) for qi in range(ki, nq)]
    else:
        pairs = [(qi, ki) for qi in range(nq) for ki in range(qi + 1)]
    return (jnp.asarray([p[0] for p in pairs], jnp.int32), jnp.asarray([p[1] for p in pairs], jnp.int32))


def _head_rows(rows, n):
    return jnp.concatenate([jnp.broadcast_to(r, (DH, n)) for r in rows], axis=0)


def _attn_fwd(qt, kh, vt, crow, ccol, bq, gather=(), gather_axes=None):
    T = qt.shape[1]
    nq = T // bq
    bk = bq
    qs, ks = _causal_pairs(nq, key_major=False)
    BW = HP * DH
    n = len(gather)
    axes = list(gather_axes) if gather_axes is not None else [0] * n
    last_hp, last_i = NH // HP - 1, qs.shape[0] - 1

    def body(qs_ref, ks_ref, qt_ref, k_ref, vt_ref, cr_ref, cc_ref, *rest):
        o_ref, lse_ref = rest[n:n + 2]
        m_s, l_s, acc_s = rest[2 * n + 2:2 * n + 5]
        hp, i = pl.program_id(0), pl.program_id(1)
        qi, ki = qs_ref[i], ks_ref[i]
        if n:
            start, forward, finish = _gather_phases(gather, axes, rest[:n], rest[n + 2:2 * n + 2], *rest[2 * n + 5:])
            pl.when((hp == 0) & (i == 0))(start)
            pl.when((hp == last_hp) & (i == 0))(forward)

        @pl.when(ki == 0)
        def _():
            m_s[...] = jnp.full_like(m_s, MASK_VALUE)
            l_s[...] = jnp.zeros_like(l_s)
            acc_s[...] = jnp.zeros_like(acc_s)

        def step(diagonal):
            for h in range(HP):
                rows = slice(h * DH, (h + 1) * DH)
                s = _mm(k_ref[h], qt_ref[rows, :]) - cc_ref[h]
                if diagonal:
                    s = jnp.where(_iota2((bk, bq), 0) <= _iota2((bk, bq), 1), s, MASK_VALUE)
                cr = cr_ref[h]
                m_old = m_s[h]
                m_new = jnp.maximum(m_old, jnp.max(s, axis=0, keepdims=True) + cr)
                p = jnp.exp2(s + (cr - m_new))
                alpha = jnp.exp2(m_old - m_new)
                l_s[h] = alpha * l_s[h] + jnp.sum(p, axis=0, keepdims=True)
                acc_s[rows, :] = alpha * acc_s[rows, :] + _mm(vt_ref[rows, :], p.astype(bf16))
                m_s[h] = m_new

        @pl.when(ki < qi)
        def _():
            step(False)

        @pl.when(ki == qi)
        def _():
            step(True)
            o_ref[...] = (acc_s[...] / _head_rows([l_s[h] for h in range(HP)], bq)).T
            for h in range(HP):
                lse_ref[h] = m_s[h] + jnp.log(l_s[h]) * LOG2E

        if n:
            pl.when((hp == last_hp) & (i == last_i))(finish)

    qcol = lambda hp, i, qs, ks: (hp, qs[i])
    kcol = lambda hp, i, qs, ks: (hp, ks[i])
    qrow = lambda hp, i, qs, ks: (hp, 0, qs[i])
    hbm = pl.BlockSpec(memory_space=pltpu.HBM)
    return _pcall(
        body, name="attn_fwd_gather" if n else "attn_fwd",
        grid_spec=pltpu.PrefetchScalarGridSpec(
            num_scalar_prefetch=2, grid=(NH // HP, qs.shape[0]),
            in_specs=[pl.BlockSpec((BW, bq), qcol),
                      pl.BlockSpec((HP, bk, DH), lambda hp, i, qs, ks: (hp, ks[i], 0)),
                      pl.BlockSpec((BW, bk), kcol),
                      pl.BlockSpec((HP, 1, bq), qrow),
                      pl.BlockSpec((HP, bk, 1), lambda hp, i, qs, ks: (hp, ks[i], 0))] + [hbm] * n,
            out_specs=[pl.BlockSpec((bq, BW), lambda hp, i, qs, ks: (qs[i], hp)),
                       pl.BlockSpec((HP, 1, bq), qrow)] + [hbm] * n,
            scratch_shapes=[pltpu.VMEM((HP, 1, bq), f32), pltpu.VMEM((HP, 1, bq), f32),
                            pltpu.VMEM((BW, bq), f32)] + (_gather_semaphores(n) if n else [])),
        out_shape=[jax.ShapeDtypeStruct((T, W), f32), jax.ShapeDtypeStruct((NH, 1, T), f32)]
        + _gathered_shapes(gather, axes),
        compiler_params=_params("arbitrary" if n else "parallel", "arbitrary"),
    )(qs, ks, qt, kh, vt, crow, ccol, *gather)


def _attn_bwd_prep(dy, oh, z, tm):
    T = dy.shape[0]
    cg = OFF_ATT // W + 3

    def body(dy_ref, o_ref, g_ref, dot_ref, dl_ref):
        do = (dy_ref[...] * _silu(g_ref[...])).astype(bf16)
        dot_ref[...] = do.astype(f32).T.astype(bf16)
        prod = (do.astype(f32) * o_ref[...]).T
        for h in range(NH):
            dl_ref[h] = jnp.sum(prod[h * DH:(h + 1) * DH, :], axis=0, keepdims=True)

    return _pcall(
        body, name="attn_bwd_prep", grid=(T // tm,),
        in_specs=[pl.BlockSpec((tm, W), lambda i: (i, 0)),
                  pl.BlockSpec((tm, W), lambda i: (i, 0)),
                  _zblock(tm, cg)],
        out_specs=[pl.BlockSpec((W, tm), lambda i: (0, i)),
                   pl.BlockSpec((NH, 1, tm), lambda i: (0, 0, i))],
        out_shape=[jax.ShapeDtypeStruct((W, T), bf16), jax.ShapeDtypeStruct((NH, 1, T), f32)],
        compiler_params=_params("parallel"),
    )(dy, oh, z)


def _slab_exchange_phases(buf, land, row0, rows, send_sems, recv_sems, local_sem):
    me = _my_id()

    def local():
        return pltpu.make_async_copy(buf.at[me, pl.ds(row0, rows), :], land.at[me], local_sem)

    def remote(k, receive):
        peer, pid = _peer(k)
        return pltpu.make_async_remote_copy(
            src_ref=buf.at[pid, pl.ds(row0, rows), :], dst_ref=land.at[pid] if receive else land.at[me],
            send_sem=send_sems.at[k - 1], recv_sem=recv_sems.at[k - 1],
            device_id=peer, device_id_type=pl.DeviceIdType.MESH)

    def start():
        local().start()
        for k in range(1, N_DEV):
            remote(k, False).start()

    def finish():
        for k in range(1, N_DEV):
            remote(k, True).wait_recv()
        for k in range(1, N_DEV):
            remote(k, False).wait_send()
        local().wait()

    return start, finish


def _attn_bwd(qt, kt, kh, vh, crow, ccol, dot, lse, delta, bq, exchange=None):
    T = qt.shape[1]
    nq = T // bq
    bk = bq
    qs, ks = _causal_pairs(nq, key_major=True)
    BW = HP * DH
    exchange = list(exchange or [])
    nx = len(exchange)

    def body(qs_ref, ks_ref, qt_ref, kt_ref, k_ref, v_ref, cr_ref, cc_ref, dot_ref, lse_ref, dl_ref, *rest):
        dq_ref, dk_ref, dv_ref, dck_ref, dcq_ref = rest[nx:nx + 5]
        dq_s, dk_s, dv_s, dck_s = rest[2 * nx + 5:2 * nx + 9]
        i = pl.program_id(1)
        qi, ki = qs_ref[i], ks_ref[i]
        phases = [_slab_exchange_phases(rest[e], rest[nx + 5 + e], exchange[e][1], exchange[e][2],
                                        *rest[2 * nx + 9 + 3 * e:2 * nx + 12 + 3 * e]) for e in range(nx)]
        if nx:
            @pl.when((pl.program_id(0) == 0) & (i == 0))
            def _():
                for start, _ in phases:
                    start()

        @pl.when(i == 0)
        def _():
            dq_s[...] = jnp.zeros_like(dq_s)
            dcq_ref[...] = jnp.zeros_like(dcq_ref)

        @pl.when(qi == ki)
        def _():
            dk_s[...] = jnp.zeros_like(dk_s)
            dv_s[...] = jnp.zeros_like(dv_s)
            dck_s[...] = jnp.zeros_like(dck_s)

        def step(diagonal):
            colsums = []
            for h in range(HP):
                rows = slice(h * DH, (h + 1) * DH)
                qth, doth = qt_ref[rows, :], dot_ref[rows, :]
                p = jnp.exp2(_mm(k_ref[h], qth) + (cr_ref[h] - lse_ref[h]) - cc_ref[h])
                if diagonal:
                    p = jnp.where(_iota2((bk, bq), 0) <= _iota2((bk, bq), 1), p, 0.0)
                dv_s[rows, :] += _mm_nt(doth, p.astype(bf16))
                ds = p * (_mm(v_ref[h], doth) - dl_ref[h])
                dsb = ds.astype(bf16)
                dk_s[rows, :] += _mm_nt(qth, dsb)
                dq_s[qi, rows, :] += _mm(kt_ref[rows, :], dsb)
                part = ds[:, 0:128]
                for c in range(1, bq // 128):
                    part = part + ds[:, c * 128:(c + 1) * 128]
                dck_s[h] += part
                colsums.append(jnp.sum(ds, axis=0, keepdims=True))
            dcq_ref[qi] += _stack_rows(colsums, bq)

        @pl.when(qi > ki)
        def _():
            step(False)

        @pl.when(qi == ki)
        def _():
            step(True)

        @pl.when(qi == nq - 1)
        def _():
            dk_ref[...] = (dk_s[...] * (1.0 / LOG2E)).T
            dv_ref[...] = dv_s[...].T
            lane = _iota2((bk, 128), 1)
            out = jnp.zeros((bk, 128), f32)
            for h in range(HP):
                out = out - jnp.where(lane == pl.program_id(0) * HP + h,
                                      jnp.sum(dck_s[h], axis=1, keepdims=True), 0.0)
            dck_ref[...] = out

        @pl.when(i == qs.shape[0] - 1)
        def _():
            for qb in range(nq):
                dq_ref[qb * bq:(qb + 1) * bq, :] = dq_s[qb].T

        if nx:
            @pl.when((pl.program_id(0) == NH // HP - 1) & (i == qs.shape[0] - 1))
            def _():
                for _, finish in phases:
                    finish()

    qcol = lambda hp, i, qs, ks: (hp, qs[i])
    kcol = lambda hp, i, qs, ks: (hp, ks[i])
    qrow = lambda hp, i, qs, ks: (hp, 0, qs[i])
    kh_spec = pl.BlockSpec((HP, bk, DH), lambda hp, i, qs, ks: (hp, ks[i], 0))
    hbm = pl.BlockSpec(memory_space=pltpu.HBM)
    extra_in = [e[0] for e in exchange]
    extra_out = [jax.ShapeDtypeStruct((N_DEV, e[2], e[0].shape[2]), e[0].dtype) for e in exchange]
    extra_scratch = [pltpu.SemaphoreType.DMA((N_DEV - 1,)), pltpu.SemaphoreType.DMA((N_DEV - 1,)),
                     pltpu.SemaphoreType.DMA] * nx
    return _pcall(
        body, name="attn_bwd_exchange" if nx else "attn_bwd",
        grid_spec=pltpu.PrefetchScalarGridSpec(
            num_scalar_prefetch=2, grid=(NH // HP, qs.shape[0]),
            in_specs=[pl.BlockSpec((BW, bq), qcol), pl.BlockSpec((BW, bk), kcol), kh_spec, kh_spec,
                      pl.BlockSpec((HP, 1, bq), qrow),
                      pl.BlockSpec((HP, bk, 1), lambda hp, i, qs, ks: (hp, ks[i], 0)),
                      pl.BlockSpec((BW, bq), qcol), pl.BlockSpec((HP, 1, bq), qrow), pl.BlockSpec((HP, 1, bq), qrow)]
            + [hbm] * nx,
            out_specs=[pl.BlockSpec((T, BW), lambda hp, i, qs, ks: (0, hp)),
                       pl.BlockSpec((bk, BW), lambda hp, i, qs, ks: (ks[i], hp)),
                       pl.BlockSpec((bk, BW), lambda hp, i, qs, ks: (ks[i], hp)),
                       pl.BlockSpec((None, bk, 128), lambda hp, i, qs, ks: (hp, ks[i], 0)),
                       pl.BlockSpec((None, nq, 8, bq), lambda hp, i, qs, ks: (hp, 0, 0, 0))] + [hbm] * nx,
            scratch_shapes=[pltpu.VMEM((nq, BW, bq), f32), pltpu.VMEM((BW, bk), f32), pltpu.VMEM((BW, bk), f32),
                            pltpu.VMEM((HP, bk, 128), f32)] + extra_scratch),
        out_shape=[jax.ShapeDtypeStruct((T, W), f32)] * 3 + [jax.ShapeDtypeStruct((NH // HP, T, 128), f32),
                                                             jax.ShapeDtypeStruct((NH // HP, nq, 8, bq), f32)]
        + extra_out,
        compiler_params=_params("arbitrary" if nx else "parallel", "arbitrary"),
    )(qs, ks, qt, kt, kh, vh, crow, ccol, dot, lse, delta, *extra_in)


def _attn_post(z, dy, oh, dqh, dkh, dvh, dck, dcq, fbias, gq, gk, dzbuf, tm):
    T = z.shape[0]
    c0 = OFF_ATT // W
    nt = T // tm

    def body(q_ref, k_ref, g_ref, f_ref, dy_ref, o_ref, dq_ref, dk_ref, dv_ref, dck_ref, dcq_ref, fb_ref, gq_ref,
             gk_ref, dzin_ref, dz_ref, ggq_ref, ggk_ref, gfb_ref, carry):
        @pl.when(pl.program_id(0) == 0)
        def _():
            carry[...] = jnp.zeros_like(carry)
            ggq_ref[...] = jnp.zeros_like(ggq_ref)
            ggk_ref[...] = jnp.zeros_like(ggk_ref)
            gfb_ref[...] = jnp.zeros_like(gfb_ref)

        gm = _group_mean_matrix(W, DH)
        hs = jnp.where((_iota2((W, W), 0) & (DH - 1)) == (_iota2((W, W), 1) & (DH - 1)), 1.0, 0.0).astype(f32)

        def norm_bwd(x, dn, gain):
            r = lax.rsqrt(_group_mean(x * x, gm) + EPS)
            gg = jnp.sum(dn * x * r, axis=0, keepdims=True)
            u = dn * gain
            return r * u - x * (r * r * r) * _group_mean(u * x, gm), gg

        q, k, gate = q_ref[...], k_ref[...], g_ref[...]
        dq, ggq = norm_bwd(q, dq_ref[...] * (DH ** -0.5), gq_ref[...])
        dk, ggk = norm_bwd(k, dk_ref[...], gk_ref[...])
        ggq_ref[...] += jnp.dot(jnp.broadcast_to(ggq, (8, W)), hs, precision=HI, preferred_element_type=f32)[0:1]
        ggk_ref[...] += jnp.dot(jnp.broadcast_to(ggk, (8, W)), hs, precision=HI, preferred_element_type=f32)[0:1]
        dgate = dy_ref[...] * o_ref[...] * _dsilu(gate)
        dck_v = dcq_ref[...]
        for hp in range(NH // HP):
            dck_v = dck_v + dck_ref[hp]
        rc = jnp.dot(_upper_tri(tm), dck_v, precision=HI, preferred_element_type=f32) + carry[...]
        carry[...] += jnp.sum(dck_v, axis=0, keepdims=True)
        f = f_ref[...] + fb_ref[...]
        df = jnp.where(_iota2((tm, 128), 1) < NH, rc * _sigmoid(-f), 0.0)
        gfb_ref[...] += jnp.sum(df, axis=0, keepdims=True)
        dz_ref[:, 0:W] = dq.astype(bf16)
        dz_ref[:, W:2 * W] = dk.astype(bf16)
        dz_ref[:, 2 * W:3 * W] = dv_ref[...].astype(bf16)
        dz_ref[:, 3 * W:4 * W] = dgate.astype(bf16)
        dz_ref[:, 4 * W:4 * W + 128] = df.astype(bf16)

    rev = lambda i: nt - 1 - i
    zb = lambda col: pl.BlockSpec((tm, W), lambda i, c=col: (rev(i), c))
    hspec = pl.BlockSpec((tm, W), lambda i: (rev(i), 0))
    return _pcall(
        body, name="attn_post", grid=(nt,),
        in_specs=[zb(c0), zb(c0 + 1), zb(c0 + 3),
                  pl.BlockSpec((tm, 128), lambda i: (rev(i), OFF_F // 128)),
                  pl.BlockSpec((tm, W), lambda i: (rev(i), 0)),
                  hspec, hspec, hspec, hspec,
                  pl.BlockSpec((NH // HP, tm, 128), lambda i: (0, rev(i), 0)),
                  pl.BlockSpec((tm, 128), lambda i: (rev(i), 0)),
                  pl.BlockSpec((1, 128), lambda i: (0, 0)),
                  pl.BlockSpec((1, W), lambda i: (0, 0)), pl.BlockSpec((1, W), lambda i: (0, 0)),
                  pl.BlockSpec(memory_space=pl.ANY)],
        out_specs=[pl.BlockSpec((tm, 4 * W + 128), lambda i: (rev(i), OFF_ATT // (4 * W + 128))),
                   pl.BlockSpec((1, W), lambda i: (0, 0)), pl.BlockSpec((1, W), lambda i: (0, 0)),
                   pl.BlockSpec((1, 128), lambda i: (0, 0))],
        out_shape=[jax.ShapeDtypeStruct((T, NZ), bf16), jax.ShapeDtypeStruct((1, W), f32),
                   jax.ShapeDtypeStruct((1, W), f32), jax.ShapeDtypeStruct((1, 128), f32)],
        scratch_shapes=[pltpu.VMEM((1, 128), f32)],
        input_output_aliases={14: 0},
        compiler_params=_params("arbitrary"),
    )(z, z, z, z, dy, oh, dqh, dkh, dvh, dck, dcq, fbias, gq, gk, dzbuf)


def _merge_fwd(ya, oh, z, yc, yd, mb, x, p, wup, wo, gp, wpg, wpp, tm):
    T = x.shape[0]
    cg = OFF_ATT // W + 3

    def body(ya_ref, oh_ref, bg_ref, yc_ref, yd_ref, ml_ref, mb_ref, x_ref, p_ref, wup_ref, wo_ref, gp_ref,
             wpg_ref, wpp_ref, yb_ref, mg_ref, x1_ref, x2_ref):
        yb = (oh_ref[...] * _silu(bg_ref[...])).astype(bf16)
        yb_ref[...] = yb
        ys = (ya_ref[...], yb, yc_ref[...], yd_ref[...])
        merged = jnp.zeros((tm, D), f32)
        for b in range(NBR):
            sg = _sigmoid(ml_ref[:, b * D:(b + 1) * D] + mb_ref[b:b + 1, :])
            merged = merged + sg * _mm(ys[b], wup_ref[b])
        mgb = merged.astype(bf16)
        mg_ref[...] = mgb
        x1 = x_ref[...] + _mm(mgb, wo_ref[...])
        x1_ref[...] = x1
        r = lax.rsqrt(jnp.mean(x1 * x1, axis=-1, keepdims=True) + EPS)
        hp = (x1 * r * gp_ref[...]).astype(bf16)
        gate = _sigmoid(_mm(hp, wpg_ref[...]))
        x2_ref[...] = x1 + gate * _mm(p_ref[...].astype(bf16), wpp_ref[...])

    row = lambda width: pl.BlockSpec((tm, width), lambda i: (i, 0))
    full = lambda *shape: pl.BlockSpec(shape, lambda i: (0,) * len(shape))
    return _pcall(
        body, name="merge_fwd", grid=(T // tm,),
        in_specs=[row(W), row(W), _zblock(tm, cg), row(W), row(W),
                  pl.BlockSpec((tm, NBR * D), lambda i: (i, 0)), full(NBR, D), row(D), row(PLE),
                  full(NBR, W, D), full(D, D), full(1, D), full(D, D), full(PLE, D)],
        out_specs=[row(W), row(D), row(D), row(D)],
        out_shape=[jax.ShapeDtypeStruct((T, W), bf16), jax.ShapeDtypeStruct((T, D), bf16),
                   jax.ShapeDtypeStruct((T, D), f32), jax.ShapeDtypeStruct((T, D), f32)],
        compiler_params=_params("parallel"),
    )(ya, oh, z, yc, yd, z, mb, x, p, wup, wo, gp, wpg, wpp)


def _layer_slabs(li, bufs):
    if bufs is None:
        return [], []
    return list(bufs), [pl.BlockSpec(memory_space=pl.ANY)] * len(bufs)


def _ple_bwd(dx2, x1, p, gp, wpg, wpp, tm, li, bufs):
    T = x1.shape[0]
    SH = D // N_DEV
    nt = T // tm
    extra, extra_specs = _layer_slabs(li, bufs)

    def body(dx2_ref, x1_ref, p_ref, gp_ref, wpg_ref, wpp_ref, *rest):
        dx1_ref, gwpg_ref, gwpp_ref, ggp_ref, gwpg_acc, gwpp_acc = rest[len(extra):]

        @pl.when(pl.program_id(0) == 0)
        def _():
            gwpg_acc[...] = jnp.zeros_like(gwpg_acc)
            gwpp_acc[...] = jnp.zeros_like(gwpp_acc)
            ggp_ref[...] = jnp.zeros_like(ggp_ref)

        x1, dx2 = x1_ref[...], dx2_ref[...]
        r = lax.rsqrt(jnp.mean(x1 * x1, axis=-1, keepdims=True) + EPS)
        xh = x1 * r
        gp = gp_ref[...]
        hp = (xh * gp).astype(bf16)
        gate = _sigmoid(_mm(hp, wpg_ref[...]))
        pb = p_ref[...].astype(bf16)
        pp = _mm(pb, wpp_ref[...])
        dpre = ((dx2 * pp) * gate * (1.0 - gate)).astype(bf16)
        gwpp_acc[...] += _mm_tn(pb, (dx2 * gate).astype(bf16))
        gwpg_acc[...] += _mm_tn(hp, dpre)
        dhp = _mm_nt(dpre, wpg_ref[...])
        ggp_ref[...] += jnp.sum(dhp * xh, axis=0, keepdims=True)
        u = dhp * gp
        dx1_ref[...] = dx2 + r * u - x1 * (r * r * r) * jnp.mean(u * x1, axis=-1, keepdims=True)

        @pl.when(pl.program_id(0) == nt - 1)
        def _():
            gwpg_ref[...] = gwpg_acc[...].reshape(N_DEV, SH, D).astype(bf16)
            for d in range(N_DEV):
                gwpp_ref[d] = gwpp_acc[:, d * SH:(d + 1) * SH].astype(bf16)

    row = lambda width: pl.BlockSpec((tm, width), lambda i: (i, 0))
    full = lambda *shape: pl.BlockSpec(shape, lambda i: (0,) * len(shape))
    n_in = 6
    return _pcall(
        body, name="ple_bwd", grid=(nt,),
        in_specs=[row(D), row(D), row(PLE), full(1, D), full(D, D), full(PLE, D)] + extra_specs,
        out_specs=[row(D), pl.BlockSpec((N_DEV, SH, D), lambda i: (0, li, 0)),
                   pl.BlockSpec((N_DEV, PLE, SH), lambda i: (0, li, 0)), full(1, D)],
        out_shape=[jax.ShapeDtypeStruct((T, D), f32), jax.ShapeDtypeStruct((N_DEV, DEPTH * SH, D), bf16),
                   jax.ShapeDtypeStruct((N_DEV, DEPTH * PLE, SH), bf16), jax.ShapeDtypeStruct((1, D), f32)],
        scratch_shapes=[pltpu.VMEM((D, D), f32), pltpu.VMEM((PLE, D), f32)],
        input_output_aliases={n_in + k: 1 + k for k in range(len(extra))},
        compiler_params=_params("arbitrary"),
    )(dx2, x1, p, gp, wpg, wpp, *extra)


def _merge_bwd(dx1, mg, ya, yb, yc, yd, z, mb, wup, wo, tm, li, bufs):
    T = dx1.shape[0]
    SH = D // N_DEV
    nt = T // tm
    extra, extra_specs = _layer_slabs(li, bufs)

    def body(dx1_ref, mg_ref, ya_ref, yb_ref, yc_ref, yd_ref, ml_ref, mb_ref, wup_ref, wo_ref, *rest):
        dml_ref, dya_ref, dyb_ref, dyc_ref, dyd_ref, gwo_ref, gwup_ref, gmb_ref, gwo_acc, gwup_acc = rest[len(extra):]

        @pl.when(pl.program_id(0) == 0)
        def _():
            gwo_acc[...] = jnp.zeros_like(gwo_acc)
            gwup_acc[...] = jnp.zeros_like(gwup_acc)
            gmb_ref[...] = jnp.zeros_like(gmb_ref)

        dx1b = dx1_ref[...].astype(bf16)
        gwo_acc[...] += _mm_tn(mg_ref[...], dx1b)
        dm = _mm_nt(dx1b, wo_ref[...])
        ys = (ya_ref, yb_ref, yc_ref, yd_ref)
        dys = (dya_ref, dyb_ref, dyc_ref, dyd_ref)
        for b in range(NBR):
            y = ys[b][...]
            up = _mm(y, wup_ref[b])
            sg = _sigmoid(ml_ref[:, b * D:(b + 1) * D] + mb_ref[b:b + 1, :])
            dup = (dm * sg).astype(bf16)
            dml = dm * up * sg * (1.0 - sg)
            gmb_ref[b:b + 1, :] += jnp.sum(dml, axis=0, keepdims=True)
            dml_ref[:, b * D:(b + 1) * D] = dml.astype(bf16)
            gwup_acc[b] += _mm_tn(y, dup)
            dys[b][...] = _mm_nt(dup, wup_ref[b])

        @pl.when(pl.program_id(0) == nt - 1)
        def _():
            gwo_ref[...] = gwo_acc[...].reshape(N_DEV, SH, D).astype(bf16)
            for d in range(N_DEV):
                gwup_ref[d] = gwup_acc[:, :, d * SH:(d + 1) * SH].reshape(NBR * W, SH).astype(bf16)

    row = lambda width: pl.BlockSpec((tm, width), lambda i: (i, 0))
    full = lambda *shape: pl.BlockSpec(shape, lambda i: (0,) * len(shape))
    n_in = 10
    return _pcall(
        body, name="merge_bwd", grid=(nt,),
        in_specs=[row(D), row(D), row(W), row(W), row(W), row(W), row(NBR * D), full(NBR, D),
                  full(NBR, W, D), full(D, D)] + extra_specs,
        out_specs=[row(NBR * D), row(W), row(W), row(W), row(W),
                   pl.BlockSpec((N_DEV, SH, D), lambda i: (0, li, 0)),
                   pl.BlockSpec((N_DEV, NBR * W, SH), lambda i: (0, li, 0)), full(NBR, D)],
        out_shape=[jax.ShapeDtypeStruct((T, NZ), bf16)] + [jax.ShapeDtypeStruct((T, W), f32)] * 4
        + [jax.ShapeDtypeStruct((N_DEV, DEPTH * SH, D), bf16),
           jax.ShapeDtypeStruct((N_DEV, DEPTH * NBR * W, SH), bf16),
           jax.ShapeDtypeStruct((NBR, D), f32)],
        scratch_shapes=[pltpu.VMEM((D, D), f32), pltpu.VMEM((NBR, W, D), f32)],
        input_output_aliases={n_in + k: 5 + k for k in range(len(extra))},
        compiler_params=_params("arbitrary"),
    )(dx1, mg, ya, yb, yc, yd, z, mb, wup, wo, *extra)


def _loss_head(y, target, tm):
    T = y.shape[0]

    def body(y_ref, t_ref, loss_ref, dy_ref, acc):
        i = pl.program_id(0)

        @pl.when(i == 0)
        def _():
            acc[...] = jnp.zeros_like(acc)

        e = y_ref[...] - t_ref[...]
        dy_ref[...] = e * (1.0 / D)
        acc[...] += jnp.sum(e * e, axis=0, keepdims=True)

        @pl.when(i == T // tm - 1)
        def _():
            loss_ref[...] = jnp.sum(acc[...], axis=1, keepdims=True) * (0.5 / D)

    return _pcall(
        body, name="loss_head", grid=(T // tm,),
        in_specs=[pl.BlockSpec((tm, D), lambda i: (i, 0)), pl.BlockSpec((tm, D), lambda i: (i, 0))],
        out_specs=[pl.BlockSpec((1, 1), lambda i: (0, 0)), pl.BlockSpec((tm, D), lambda i: (i, 0))],
        out_shape=[jax.ShapeDtypeStruct((1, 1), f32), jax.ShapeDtypeStruct((T, D), f32)],
        scratch_shapes=[pltpu.VMEM((1, D), f32)],
        compiler_params=_params("arbitrary"),
    )(y, target)


def _lb_softmax_rows(l_ref):
    rows = [l_ref[i:i + 1, :] for i in range(DEPTH)]
    m = rows[0]
    for r in rows[1:]:
        m = jnp.maximum(m, r)
    es = [jnp.exp(r - m) for r in rows]
    tot = es[0]
    for e in es[1:]:
        tot = tot + e
    return [e / tot for e in es]


def _lb_partial_sums(pr):
    sums = [jnp.zeros_like(pr[0])]
    for i in range(1, DEPTH):
        sums.append(sums[-1] + pr[i])
    return sums


def _stack_rows(rows, width):
    idx = _iota2((8, width), 0)
    out = jnp.zeros((8, width), f32)
    for i, r in enumerate(rows):
        out = jnp.where(idx == i, r, out)
    return out


def _lower_bounds(lb_logits):
    def body(l_ref, o_ref):
        sums = _lb_partial_sums(_lb_softmax_rows(l_ref))
        o_ref[...] = _stack_rows([jnp.clip(s, 0.0, 1.0) for s in sums], W)

    return _pcall(body, name="lower_bounds", out_shape=jax.ShapeDtypeStruct((8, W), f32))(lb_logits)


def _lower_bounds_bwd(lb_logits, dlower):
    def body(l_ref, d_ref, o_ref):
        pr = _lb_softmax_rows(l_ref)
        sums = _lb_partial_sums(pr)
        dl = [jnp.where((sums[i] > 0.0) & (sums[i] < 1.0), d_ref[i:i + 1, :], 0.0) for i in range(DEPTH)]
        dp = [jnp.zeros_like(pr[0])] * DEPTH
        run = jnp.zeros_like(pr[0])
        for j in reversed(range(1, DEPTH)):
            run = run + dl[j]
            dp[j] = run
        inner = pr[0] * dp[0]
        for j in range(1, DEPTH):
            inner = inner + pr[j] * dp[j]
        o_ref[...] = _stack_rows([pr[j] * (dp[j] - inner) for j in range(DEPTH)], W)

    return _pcall(body, name="lower_bounds_bwd", out_shape=jax.ShapeDtypeStruct((8, W), f32))(lb_logits, dlower)


def _row_tile(rows, cols, budget_bytes=1 << 20, mult=8):
    if rows % mult:
        return rows
    best = mult
    for t in range(mult, rows + 1, mult):
        if rows % t == 0 and t * cols * 4 <= budget_bytes:
            best = t
    return best


def _sum_slabs(land):
    N, R, C = land.shape
    tr = _row_tile(R, C * N, mult=16)

    def body(l_ref, o_ref):
        acc = l_ref[0].astype(f32)
        for j in range(1, N):
            acc = acc + l_ref[j].astype(f32)
        o_ref[...] = acc

    return _pcall(
        body, name="sum_slabs", grid=(R // tr,),
        in_specs=[pl.BlockSpec((N, tr, C), lambda i: (0, i, 0))],
        out_specs=pl.BlockSpec((tr, C), lambda i: (i, 0)),
        out_shape=jax.ShapeDtypeStruct((R, C), f32),
        compiler_params=_params("parallel"),
    )(land)


def _adamw_update(w_ref, g_ref, m_ref, v_ref, d_ref, nm_ref, nv_ref):
    c1 = 1.0 / (1.0 - ADAM_B1 ** ADAM_STEP)
    c2 = 1.0 / (1.0 - ADAM_B2 ** ADAM_STEP)
    gv = g_ref[...]
    nm = ADAM_B1 * m_ref[...] + (1.0 - ADAM_B1) * gv
    nv = ADAM_B2 * v_ref[...] + (1.0 - ADAM_B2) * (gv * gv)
    nm_ref[...] = nm
    nv_ref[...] = nv
    d_ref[...] = -ADAM_LR * ((nm * c1) / (jnp.sqrt(nv * c2) + ADAM_EPS) + ADAM_WD * w_ref[...])


def _adamw3(w, g, m, v):
    L, R, C = w.shape
    tr = _row_tile(R, C)

    def body(*refs):
        _adamw_update(*refs)

    spec = pl.BlockSpec((None, tr, C), lambda l, i: (l, i, 0))
    return _pcall(
        body, name="adamw3", grid=(L, R // tr),
        in_specs=[spec] * 4, out_specs=[spec] * 3,
        out_shape=[jax.ShapeDtypeStruct((L, R, C), f32)] * 3,
        compiler_params=_params("parallel", "parallel"),
    )(w, g, m, v)


def _adamw_slabs(w, land, m, v):
    L, R, C = w.shape
    N = land.shape[0]
    tr = _row_tile(R, C * N, mult=16)
    nb = R // tr

    def body(w_ref, l_ref, m_ref, v_ref, g_ref, d_ref, nm_ref, nv_ref):
        acc = l_ref[0].astype(f32)
        for j in range(1, N):
            acc = acc + l_ref[j].astype(f32)
        g_ref[...] = acc
        _adamw_update(w_ref, g_ref, m_ref, v_ref, d_ref, nm_ref, nv_ref)

    spec = pl.BlockSpec((None, tr, C), lambda l, i: (l, i, 0))
    return _pcall(
        body, name="adamw_slabs", grid=(L, nb),
        in_specs=[spec, pl.BlockSpec((N, tr, C), lambda l, i: (0, l * nb + i, 0)), spec, spec],
        out_specs=[spec] * 4,
        out_shape=[jax.ShapeDtypeStruct((L, R, C), f32)] * 4,
        compiler_params=_params("parallel", "parallel"),
    )(w, land, m, v)


def _adamw(w, g, m, v):
    if w.ndim == 3:
        return _adamw3(w, g, m, v)
    R, C = w.shape
    tr = _row_tile(R, C)
    c1 = 1.0 / (1.0 - ADAM_B1 ** ADAM_STEP)
    c2 = 1.0 / (1.0 - ADAM_B2 ** ADAM_STEP)

    def body(w_ref, g_ref, m_ref, v_ref, d_ref, nm_ref, nv_ref):
        gv = g_ref[...]
        nm = ADAM_B1 * m_ref[...] + (1.0 - ADAM_B1) * gv
        nv = ADAM_B2 * v_ref[...] + (1.0 - ADAM_B2) * (gv * gv)
        nm_ref[...] = nm
        nv_ref[...] = nv
        d_ref[...] = -ADAM_LR * ((nm * c1) / (jnp.sqrt(nv * c2) + ADAM_EPS) + ADAM_WD * w_ref[...])

    spec = pl.BlockSpec((tr, C), lambda i: (i, 0))
    return _pcall(
        body, name="adamw", grid=(R // tr,),
        in_specs=[spec] * 4, out_specs=[spec] * 3,
        out_shape=[jax.ShapeDtypeStruct((R, C), f32)] * 3,
        compiler_params=_params("parallel"),
    )(w, g, m, v)


def _my_id():
    return lax.axis_index("x") * 4 + lax.axis_index("y") * 2 + lax.axis_index("c")


def _peer(k):
    x, y, c = lax.axis_index("x"), lax.axis_index("y"), lax.axis_index("c")
    kx, ky, kc = (k >> 2) & 1, (k >> 1) & 1, k & 1
    px, py, pc = x ^ kx, y ^ ky, c ^ kc
    return (px, py, pc), px * 4 + py * 2 + pc


def _all_gather(shards, axes):
    n = len(shards)

    def body(*refs):
        start, forward, finish = _gather_phases(shards, axes, refs[:n], refs[n:2 * n], *refs[2 * n:])
        start()
        forward()
        finish()

    hbm = pl.BlockSpec(memory_space=pltpu.HBM)
    return _pcall(
        body, name="all_gather",
        in_specs=[hbm] * n, out_specs=[hbm] * n,
        out_shape=_gathered_shapes(shards, axes),
        scratch_shapes=_gather_semaphores(n),
    )(*shards)


def _gathered_shapes(shards, axes):
    def full_shape(s, ax):
        shp = list(s.shape)
        shp[ax] *= N_DEV
        return tuple(shp)

    return [jax.ShapeDtypeStruct(full_shape(s, ax), s.dtype) for s, ax in zip(shards, axes)]


def _gather_semaphores(n):
    return [pltpu.SemaphoreType.DMA((n, N_DEV - 1)), pltpu.SemaphoreType.DMA((n, N_DEV - 1)),
            pltpu.SemaphoreType.DMA((n,))]


def _gather_phases(shards, axes, srcs, outs, send_sems, recv_sems, local_sems):
    n = len(shards)
    x, y, c = lax.axis_index("x"), lax.axis_index("y"), lax.axis_index("c")
    me, sibling = (x, y, c), (x, y, 1 - c)
    chips = [(1 - x, y), (x, 1 - y), (1 - x, 1 - y)]

    def block(a, dev):
        j = dev[0] * 4 + dev[1] * 2 + dev[2]
        size = shards[a].shape[axes[a]]
        start = pl.multiple_of(j * size, size)
        if axes[a] == 0:
            return outs[a].at[pl.ds(start, size), :]
        if axes[a] == 1:
            return outs[a].at[:, pl.ds(start, size), :]
        return outs[a].at[:, pl.ds(start, size)]

    def copy(a, k, dev, to, src=None):
        return pltpu.make_async_remote_copy(
            src_ref=block(a, dev) if src is None else src, dst_ref=block(a, dev),
            send_sem=send_sems.at[a, k], recv_sem=recv_sems.at[a, k],
            device_id=to, device_id_type=pl.DeviceIdType.MESH)

    def mine():
        return [pltpu.make_async_copy(srcs[a], block(a, me), local_sems.at[a]) for a in range(n)]

    def first():
        cps = []
        for a in range(n):
            cps.append(copy(a, 0, me, sibling, src=srcs[a]))
            cps += [copy(a, 1 + j, me, (*chip, c), src=srcs[a]) for j, chip in enumerate(chips)]
        return cps

    def passed():
        return [copy(a, 4 + j, (*chip, c), sibling) for j, chip in enumerate(chips) for a in range(n)]

    def start():
        for cp in mine() + first():
            cp.start()

    def forward():
        for j, chip in enumerate(chips):
            for a in range(n):
                copy(a, 1 + j, (*chip, c), me).wait_recv()
                copy(a, 4 + j, (*chip, c), sibling).start()

    def finish():
        for a in range(n):
            copy(a, 0, sibling, me).wait_recv()
            for j, chip in enumerate(chips):
                copy(a, 4 + j, (*chip, 1 - c), me).wait_recv()
        for cp in first() + passed():
            cp.wait_send()
        for cp in mine():
            cp.wait()

    return start, forward, finish


N_CHIP = N_DEV // 2


def _exchange_sibling(sliced):
    n = len(sliced)

    def body(*refs):
        srcs, outs = refs[:n], refs[n:2 * n]
        send_sems, recv_sems = refs[2 * n:]
        x, y, c = lax.axis_index("x"), lax.axis_index("y"), lax.axis_index("c")
        copies = []
        for a in range(n):
            for q in range(N_CHIP):
                cp = pltpu.make_async_remote_copy(
                    src_ref=srcs[a].at[2 * q + (1 - c)], dst_ref=outs[a].at[q],
                    send_sem=send_sems.at[a, q], recv_sem=recv_sems.at[a, q],
                    device_id=(x, y, 1 - c), device_id_type=pl.DeviceIdType.MESH)
                cp.start()
                copies.append(cp)
        for cp in copies:
            cp.wait_recv()
        for cp in copies:
            cp.wait_send()

    hbm = pl.BlockSpec(memory_space=pltpu.HBM)
    return _pcall(
        body, name="grad_exchange_sibling",
        in_specs=[hbm] * n, out_specs=[hbm] * n,
        out_shape=[jax.ShapeDtypeStruct((N_CHIP,) + s.shape[1:], s.dtype) for s in sliced],
        scratch_shapes=[pltpu.SemaphoreType.DMA((n, N_CHIP)), pltpu.SemaphoreType.DMA((n, N_CHIP))],
    )(*sliced)


def _pair_sum(own, recv):
    _, R, C = own.shape
    tr = _row_tile(R, C, mult=16)
    side = lax.axis_index("c").astype(jnp.int32).reshape(1)

    def body(c_ref, own_ref, recv_ref, o_ref):
        o_ref[...] = (own_ref[...].astype(f32) + recv_ref[...].astype(f32)).astype(o_ref.dtype)

    return _pcall(
        body, name="pair_sum",
        grid_spec=pltpu.PrefetchScalarGridSpec(
            num_scalar_prefetch=1, grid=(N_CHIP, R // tr),
            in_specs=[pl.BlockSpec((None, tr, C), lambda q, i, c: (2 * q + c[0], i, 0)),
                      pl.BlockSpec((None, tr, C), lambda q, i, c: (q, i, 0))],
            out_specs=pl.BlockSpec((None, tr, C), lambda q, i, c: (q, i, 0))),
        out_shape=jax.ShapeDtypeStruct((N_CHIP, R, C), own.dtype),
        compiler_params=_params("parallel", "parallel"),
    )(side, own, recv)


def _exchange_chips(partial, whole):
    ns, nw = len(partial), len(whole)

    def body(*refs):
        srcs, outs = refs[:ns + nw], refs[ns + nw:2 * (ns + nw)]
        send_sems, recv_sems, wsend_sems, wrecv_sems, local_sems = refs[2 * (ns + nw):]
        x, y, c = lax.axis_index("x"), lax.axis_index("y"), lax.axis_index("c")
        me, myq = _my_id(), x * 2 + y
        chips = [(1 - x, y), (x, 1 - y), (1 - x, 1 - y)]
        locals_ = [pltpu.make_async_copy(srcs[a].at[myq], outs[a].at[myq], local_sems.at[a]) for a in range(ns)]
        locals_ += [pltpu.make_async_copy(srcs[ns + b], outs[ns + b].at[me], local_sems.at[ns + b])
                    for b in range(nw)]
        for cp in locals_:
            cp.start()
        sends, recvs = [], []
        for j, chip in enumerate(chips):
            q = chip[0] * 2 + chip[1]
            for a in range(ns):
                cp = pltpu.make_async_remote_copy(
                    src_ref=srcs[a].at[q], dst_ref=outs[a].at[myq],
                    send_sem=send_sems.at[a, j], recv_sem=recv_sems.at[a, j],
                    device_id=(*chip, c), device_id_type=pl.DeviceIdType.MESH)
                cp.start()
                sends.append(cp)
                recvs.append(pltpu.make_async_remote_copy(
                    src_ref=srcs[a].at[q], dst_ref=outs[a].at[q],
                    send_sem=send_sems.at[a, j], recv_sem=recv_sems.at[a, j],
                    device_id=(*chip, c), device_id_type=pl.DeviceIdType.MESH))
        for k in range(1, N_DEV):
            peer, pid = _peer(k)
            for b in range(nw):
                cp = pltpu.make_async_remote_copy(
                    src_ref=srcs[ns + b], dst_ref=outs[ns + b].at[me],
                    send_sem=wsend_sems.at[b, k - 1], recv_sem=wrecv_sems.at[b, k - 1],
                    device_id=peer, device_id_type=pl.DeviceIdType.MESH)
                cp.start()
                sends.append(cp)
                recvs.append(pltpu.make_async_remote_copy(
                    src_ref=srcs[ns + b], dst_ref=outs[ns + b].at[pid],
                    send_sem=wsend_sems.at[b, k - 1], recv_sem=wrecv_sems.at[b, k - 1],
                    device_id=peer, device_id_type=pl.DeviceIdType.MESH))
        for cp in recvs:
            cp.wait_recv()
        for cp in sends:
            cp.wait_send()
        for cp in locals_:
            cp.wait()

    hbm = pl.BlockSpec(memory_space=pltpu.HBM)
    shapes = [jax.ShapeDtypeStruct(s.shape, s.dtype) for s in partial]
    shapes += [jax.ShapeDtypeStruct((N_DEV,) + s.shape, s.dtype) for s in whole]
    return _pcall(
        body, name="grad_exchange_chips",
        in_specs=[hbm] * (ns + nw), out_specs=[hbm] * (ns + nw), out_shape=shapes,
        scratch_shapes=[pltpu.SemaphoreType.DMA((ns, N_CHIP - 1)), pltpu.SemaphoreType.DMA((ns, N_CHIP - 1)),
                        pltpu.SemaphoreType.DMA((nw, N_DEV - 1)), pltpu.SemaphoreType.DMA((nw, N_DEV - 1)),
                        pltpu.SemaphoreType.DMA((ns + nw,))],
    )(*partial, *whole)


def _permute_cols(w):
    pad = jnp.zeros(w.shape[:-1] + (NZ - OFF_F - NH,), w.dtype)
    return jnp.concatenate([
        w[..., 3844:7940],
        w[..., 0:1024],
        w[..., 2052:3076],
        w[..., 3076:3844],
        w[..., 1024:2048],
        w[..., 2048:2052], pad], axis=-1)


def _unpermute_cols(g):
    return jnp.concatenate([
        g[..., OFF_CONV:OFF_CONV + 1024],
        g[..., OFF_ATT:OFF_ATT + 1024],
        g[..., OFF_F:OFF_F + NH],
        g[..., OFF_HGRN:OFF_HGRN + 1024],
        g[..., OFF_SGU:OFF_SGU + 768],
        g[..., 0:4096]], axis=-1)


_SMALL = (
    ("norm_mix", (DEPTH, D)), ("conv_w", (DEPTH, CONV_WIDTH, W)), ("conv_b", (DEPTH, W)),
    ("fgate_bias", (DEPTH, NH)), ("q_norm", (DEPTH, DH)), ("k_norm", (DEPTH, DH)),
    ("lb_logits", (DEPTH, W)), ("hgrn_norm", (DEPTH, W)), ("sgu_norm", (DEPTH, W)),
    ("spatial_w", (DEPTH, NH, SGU_CHUNK, SGU_CHUNK)), ("spatial_b", (DEPTH, NH, SGU_CHUNK)),
    ("merge_b", (DEPTH, NBR, D)), ("norm_ple", (DEPTH, D)),
)


def _small_rows(shape):
    size = 1
    for s in shape:
        size *= s
    rows = -(-size // 128)
    return size, -(-rows // 8) * 8


def _pack_small(parts):
    out = []
    for name, shape in _SMALL:
        size, rows = _small_rows(shape)
        flat = parts[name].astype(f32).reshape(-1)
        flat = jnp.pad(flat, (0, rows * 128 - size))
        out.append(flat.reshape(rows, 128))
    return jnp.concatenate(out, axis=0)


def _unpack_small(buf):
    parts, r0 = {}, 0
    for name, shape in _SMALL:
        size, rows = _small_rows(shape)
        parts[name] = buf[r0:r0 + rows].reshape(-1)[:size].reshape(shape)
        r0 += rows
    return parts


def _shard_cols(a, width):
    return lax.dynamic_slice_in_dim(a, _my_id() * width, width, axis=a.ndim - 1)


def kernel(x, p, norm_mix, w_in, conv_w, conv_b, fgate_bias, q_norm, k_norm, lb_logits, hgrn_norm, sgu_norm, spatial_w, spatial_b, w_up, merge_b, w_o, norm_ple, w_ple_gate, w_ple_proj, loss_target, m_norm_mix, m_w_in, m_conv_w, m_conv_b, m_fgate_bias, m_q_norm, m_k_norm, m_lb_logits, m_hgrn_norm, m_sgu_norm, m_spatial_w, m_spatial_b, m_w_up, m_merge_b, m_w_o, m_norm_ple, m_w_ple_gate, m_w_ple_proj, v_norm_mix, v_w_in, v_conv_w, v_conv_b, v_fgate_bias, v_q_norm, v_k_norm, v_lb_logits, v_hgrn_norm, v_sgu_norm, v_spatial_w, v_spatial_b, v_w_up, v_merge_b, v_w_o, v_norm_ple, v_w_ple_gate, v_w_ple_proj):
    T = x.shape[1]
    SH = D // N_DEV
    CW = W // N_DEV
    tm = 512 if T % 512 == 0 else T
    tmm = 256 if T % 256 == 0 else T
    x0 = x.reshape(T, D)
    target = loss_target.reshape(T, D)

    small_shard = jnp.concatenate([
        merge_b.reshape(DEPTH * NBR, SH),
        jnp.pad(conv_w.reshape(DEPTH * CONV_WIDTH, CW), ((0, 16 - DEPTH * CONV_WIDTH), (0, SH - CW)))], axis=0)
    win_s = _permute_cols(w_in).astype(bf16)
    wup_s = w_up.astype(bf16).reshape(DEPTH, NBR * W, SH)
    wo_s, wpg_s, wpp_s = w_o.astype(bf16), w_ple_gate.astype(bf16), w_ple_proj.astype(bf16)
    win0, wup0, wo0, wpg0, wpp0, g_small = _all_gather(
        [win_s[0], wup_s[0], wo_s[0], wpg_s[0], wpp_s[0], small_shard], [0, -1, 0, 0, -1, -1])
    win_f, wup_f, wo_f, wpg_f, wpp_f = [win0], [wup0.reshape(NBR, W, D)], [wo0], [wpg0], [wpp0]
    win_later = [win_s[li] for li in range(1, DEPTH)]
    other_later = [(wup_s[li], wo_s[li], wpg_s[li], wpp_s[li]) for li in range(1, DEPTH)]
    mb_f = g_small[0:DEPTH * NBR].reshape(DEPTH, NBR, D)
    cw_f = g_small[16:16 + DEPTH * CONV_WIDTH].reshape(DEPTH, CONV_WIDTH, N_DEV, SH)[..., 0:CW]
    cw_f = cw_f.reshape(DEPTH, CONV_WIDTH, W)

    loss_local, dx, gw, gs_full = _forward_backward(
        x0, p[:, 0], target, win_f, wup_f, wo_f, wpg_f, wpp_f, mb_f, cw_f, norm_mix, conv_b, fgate_bias, q_norm,
        k_norm, lb_logits, hgrn_norm, sgu_norm, spatial_w, spatial_b, norm_ple, win_later, other_later)
    loss = lax.psum(loss_local[0, 0], AXES)
    grad_x = dx.reshape(1, T, D)

    weights = dict(norm_mix=norm_mix, w_in=w_in, conv_w=conv_w, conv_b=conv_b, fgate_bias=fgate_bias, q_norm=q_norm,
                   k_norm=k_norm, lb_logits=lb_logits, hgrn_norm=hgrn_norm, sgu_norm=sgu_norm, spatial_w=spatial_w,
                   spatial_b=spatial_b, w_up=w_up, merge_b=merge_b, w_o=w_o, norm_ple=norm_ple,
                   w_ple_gate=w_ple_gate, w_ple_proj=w_ple_proj)
    ms = dict(norm_mix=m_norm_mix, w_in=m_w_in, conv_w=m_conv_w, conv_b=m_conv_b, fgate_bias=m_fgate_bias,
              q_norm=m_q_norm, k_norm=m_k_norm, lb_logits=m_lb_logits, hgrn_norm=m_hgrn_norm, sgu_norm=m_sgu_norm,
              spatial_w=m_spatial_w, spatial_b=m_spatial_b, w_up=m_w_up, merge_b=m_merge_b, w_o=m_w_o,
              norm_ple=m_norm_ple, w_ple_gate=m_w_ple_gate, w_ple_proj=m_w_ple_proj)
    vs = dict(norm_mix=v_norm_mix, w_in=v_w_in, conv_w=v_conv_w, conv_b=v_conv_b, fgate_bias=v_fgate_bias,
              q_norm=v_q_norm, k_norm=v_k_norm, lb_logits=v_lb_logits, hgrn_norm=v_hgrn_norm, sgu_norm=v_sgu_norm,
              spatial_w=v_spatial_w, spatial_b=v_spatial_b, w_up=v_w_up, merge_b=v_merge_b, w_o=v_w_o,
              norm_ple=v_norm_ple, w_ple_gate=v_w_ple_gate, w_ple_proj=v_w_ple_proj)
    return _exchange_and_update(loss, grad_x, gw, gs_full, weights, ms, vs)


def _forward_backward(x0, p, target, win_f, wup_f, wo_f, wpg_f, wpp_f, mb_f, cw_f, norm_mix, conv_b, fgate_bias,
                      q_norm, k_norm, lb_logits, hgrn_norm, sgu_norm, spatial_w, spatial_b, norm_ple, win_later=(),
                      other_later=()):
    T = x0.shape[0]
    tm = 512 if T % 512 == 0 else T
    tmm = 256 if T % 256 == 0 else T
    tmi = 1024 if T % 1024 == 0 else tm
    lower = _lower_bounds(lb_logits)
    fb_pad = jnp.pad(fgate_bias, ((0, 0), (0, 128 - NH)))
    gq_t = jnp.tile(q_norm, (1, NH))
    gk_t = jnp.tile(k_norm, (1, NH))
    sbe = jnp.repeat(jnp.swapaxes(spatial_b, 1, 2), DH, axis=2)

    saved = []
    xc = x0
    p = p[:, None]
    for li in range(DEPTH):
        row = lambda a: a[li:li + 1]
        if li == 0 and win_later:
            z, h, *gathered = _inproj_fwd(xc, row(norm_mix), win_f[0], tmi, gather=win_later[:1])
            win_f = [win_f[0]] + gathered
        else:
            z, h = _inproj_fwd(xc, row(norm_mix), win_f[li], tmi)
        ya = _conv_fwd(z, cw_f[li], row(conv_b), tm)
        yd = _sgu_fwd(z, row(sgu_norm), spatial_w[li], sbe[li], tm)
        yc, o_pre, states = _hgrn_fwd(z, lower[li:li + 1], row(hgrn_norm), tmm)
        qt, kt, vt, kh, vh, cum = _attn_prep(z, row(fb_pad), row(gq_t), row(gk_t), tm)
        cum4 = jnp.transpose(cum[:, 0:NH])
        ccol, crow = cum4[:, :, None], cum4[:, None, :]
        if li == 0 and (len(win_later) > 1 or other_later):
            shards = list(win_later[1:]) + [a for group in other_later for a in group]
            axes = [0] * len(win_later[1:]) + [-1, 0, 0, -1] * len(other_later)
            oh, lse, *gathered = _attn_fwd(qt, kh, vt, crow, ccol, tm, gather=shards, gather_axes=axes)
            nw = len(win_later[1:])
            win_f = win_f + gathered[:nw]
            later = [gathered[nw + 4 * k:nw + 4 * k + 4] for k in range(len(other_later))]
            wup_f = [wup_f[0]] + [g4[0].reshape(NBR, W, D) for g4 in later]
            wo_f = [wo_f[0]] + [g4[1] for g4 in later]
            wpg_f = [wpg_f[0]] + [g4[2] for g4 in later]
            wpp_f = [wpp_f[0]] + [g4[3] for g4 in later]
        else:
            oh, lse = _attn_fwd(qt, kh, vt, crow, ccol, tm)
        yb, mg, x1, x2 = _merge_fwd(ya, oh, z, yc, yd, mb_f[li], xc, p[li, 0], wup_f[li], wo_f[li],
                                    row(norm_ple), wpg_f[li], wpp_f[li], tmm)
        saved.append(dict(x=xc, z=z, h=h, ya=ya, yb=yb, yc=yc, yd=yd, o_pre=o_pre, states=states,
                          qt=qt, kt=kt, kh=kh, vh=vh, crow=crow, ccol=ccol, oh=oh, lse=lse, mg=mg, x1=x1))
        xc = x2

    loss_local, dx = _loss_head(xc, target, tm)

    gw = dict(w_in=None, w_up=None, w_o=None, w_ple_gate=None, w_ple_proj=None)
    gs = {n: [None] * DEPTH for n, _ in _SMALL}
    dlower = [None] * DEPTH
    landed = {n: {} for n in gw}
    for li in reversed(range(DEPTH)):
        s = saved[li]
        row = lambda a: a[li:li + 1]
        first = li == DEPTH - 1
        dx1, gw["w_ple_gate"], gw["w_ple_proj"], ggp = _ple_bwd(
            dx, s["x1"], p[li, 0], row(norm_ple), wpg_f[li], wpp_f[li], tmm, li,
            None if first else (gw["w_ple_gate"], gw["w_ple_proj"]))
        gs["norm_ple"][li] = ggp[0]
        dz, dya, dyb, dyc, dyd, gw["w_o"], gw["w_up"], gs["merge_b"][li] = _merge_bwd(
            dx1, s["mg"], s["ya"], s["yb"], s["yc"], s["yd"], s["z"], mb_f[li], wup_f[li], wo_f[li], tmm, li,
            None if first else (gw["w_o"], gw["w_up"]))
        dz, gcw, gcb = _conv_bwd(s["z"], dya, cw_f[li], row(conv_b), dz, tm)
        gs["conv_w"][li], gs["conv_b"][li] = gcw[0:CONV_WIDTH], gcb[0]
        dz, gs["spatial_w"][li], gsb, ggv = _sgu_bwd(s["z"], dyd, row(sgu_norm), spatial_w[li], sbe[li], dz, tm)
        gs["spatial_b"][li] = jnp.transpose(gsb[:, ::DH])
        gs["sgu_norm"][li] = ggv[0]
        dz, ggn, glb = _hgrn_bwd(s["z"], lower[li:li + 1], row(hgrn_norm), s["o_pre"], s["states"], dyc, dz, tmm)
        gs["hgrn_norm"][li], dlower[li] = ggn[0], glb[0]
        dot, delta = _attn_bwd_prep(dyb, s["oh"], s["z"], tm)
        SH = D // N_DEV
        sent = [] if first else [("w_in", li + 1, SH)]
        sent += [("w_up", li, NBR * W), ("w_o", li, SH), ("w_ple_gate", li, SH), ("w_ple_proj", li, PLE)]
        dqh, dkh, dvh, dck, dcq, *lands = _attn_bwd(
            s["qt"], s["kt"], s["kh"], s["vh"], s["crow"], s["ccol"], dot, s["lse"], delta, tm,
            [(gw[name], layer * rows, rows) for name, layer, rows in sent])
        for (name, layer, _), land in zip(sent, lands):
            landed[name][layer] = land
        dcq_t = jnp.transpose(dcq[:, :, 0:HP, :], (0, 2, 1, 3)).reshape(NH, T)
        dcq_t = jnp.pad(jnp.transpose(dcq_t), ((0, 0), (0, 128 - NH)))
        dz, ggq, ggk, gfb = _attn_post(s["z"], dyb, s["oh"], dqh, dkh, dvh, dck, dcq_t, row(fb_pad),
                                       row(gq_t), row(gk_t), dz, tmm)
        gs["q_norm"][li], gs["k_norm"][li], gs["fgate_bias"][li] = ggq[0, 0:DH], ggk[0, 0:DH], gfb[0, 0:NH]
        dx, gnm = _inproj_bwd_x(dz, win_f[li], s["x"], dx1, row(norm_mix), tmi)
        gs["norm_mix"][li] = gnm[0]
        gw["w_in"] = _inproj_bwd_w(s["h"], dz, tmi, li, gw["w_in"])
    dlower8 = jnp.pad(jnp.stack(dlower), ((0, 8 - DEPTH), (0, 0)))
    gs_full = {n: jnp.stack(v) for n, v in gs.items() if n != "lb_logits"}
    gs_full["lb_logits"] = _lower_bounds_bwd(lb_logits, dlower8)[0:DEPTH]
    gw["landed"] = landed
    return loss_local, dx, gw, gs_full


def _exchange_and_update(loss, grad_x, gw, gs_full, weights, ms, vs):
    SH = D // N_DEV
    CW = W // N_DEV

    small_buf = _pack_small(gs_full)
    landed = gw["landed"]
    rest = [li for li in range(DEPTH) if li not in landed["w_in"]]
    win_rest = jnp.concatenate([gw["w_in"][:, li * SH:(li + 1) * SH] for li in rest], axis=1)
    own = [win_rest]
    from_sibling = _exchange_sibling(own)
    chip_sums = [_pair_sum(o, r) for o, r in zip(own, from_sibling)]
    l_win, l_small = _exchange_chips(chip_sums, [small_buf])

    g_rest = _sum_slabs(l_win)
    g_layers = {li: g_rest[n * SH:(n + 1) * SH] for n, li in enumerate(rest)}
    g_layers.update({li: _sum_slabs(land) for li, land in landed["w_in"].items()})
    g_w_in = _unpermute_cols(jnp.stack([g_layers[li] for li in range(DEPTH)]))

    g_small = _unpack_small(_sum_slabs(l_small))
    g_small_local = dict(g_small)
    g_small_local["conv_w"] = _shard_cols(g_small["conv_w"], CW)
    g_small_local["merge_b"] = _shard_cols(g_small["merge_b"], SH)

    grads = dict(w_in=g_w_in)
    deltas, new_m, new_v = {}, {}, {}
    for name in ("w_in", "w_up", "w_o", "w_ple_gate", "w_ple_proj"):
        shape = weights[name].shape
        as3 = (shape[0], -1, shape[-1])
        if name == "w_in":
            d_, m_, v_ = _adamw(weights[name].reshape(as3), grads[name].reshape(as3),
                                ms[name].reshape(as3), vs[name].reshape(as3))
        else:
            land = jnp.concatenate([landed[name][li] for li in range(DEPTH)], axis=1)
            g_, d_, m_, v_ = _adamw_slabs(weights[name].reshape(as3), land, ms[name].reshape(as3),
                                          vs[name].reshape(as3))
            grads[name] = g_.reshape(shape)
        deltas[name], new_m[name], new_v[name] = d_.reshape(shape), m_.reshape(shape), v_.reshape(shape)

    def local_shapes(parts):
        return {n: (parts[n] if parts[n].shape == s else jnp.pad(
            parts[n], [(0, 0)] * (len(s) - 1) + [(0, s[-1] - parts[n].shape[-1])])) for n, s in _SMALL}

    d_, m_, v_ = _adamw(_pack_small(local_shapes(weights)), _pack_small(local_shapes(g_small_local)),
                        _pack_small(local_shapes(ms)), _pack_small(local_shapes(vs)))
    for buf, dst in ((d_, deltas), (m_, new_m), (v_, new_v)):
        parts = _unpack_small(buf)
        for n, _ in _SMALL:
            dst[n] = parts[n][..., :weights[n].shape[-1]]
    for n, _ in _SMALL:
        grads[n] = g_small_local[n]

    order = ["norm_mix", "w_in", "conv_w", "conv_b", "fgate_bias", "q_norm", "k_norm", "lb_logits", "hgrn_norm",
             "sgu_norm", "spatial_w", "spatial_b", "w_up", "merge_b", "w_o", "norm_ple", "w_ple_gate", "w_ple_proj"]
    return (loss, grad_x, *[grads[n] for n in order], *[deltas[n] for n in order],
            *[new_m[n] for n in order], *[new_v[n] for n in order])
```
